```python
import math
import jax, jax.numpy as jnp
from jax import lax
import numpy as np

D_MODEL = 2048
BATCH = 8
SEQ = 2048
DEPTH = 4

MEM_LEN = 256
A_HEAD_DIM = 128
A_HEADS = D_MODEL // 256
A_KV_HEADS = A_HEADS // 4
A_GROUP = A_HEADS // A_KV_HEADS
WINDOW = 128
BLOCK = 128
N_BUCKETS = 32
MAX_DISTANCE = 128
B_HEADS = D_MODEL // 512
B_KEY_DIM = 64
B_VAL_DIM = 128
GATE_RANK = 16
GATE_TAU = 16.0
GLA_CHUNK = 16
C_WIDTH = D_MODEL // 4
C_BLOCKS = 4
C_BLOCK_DIM = C_WIDTH // C_BLOCKS
CONV_WIDTH = 4
CONV_LEFT = 2
LRU_C = 8.0
X_HEADS = 4
X_HEAD_DIM = D_MODEL // X_HEADS
D_FF = 4 * D_MODEL
EPS = 1e-6
NEG_INF = -1e30

A_Q = A_HEADS * A_HEAD_DIM
A_KV = A_KV_HEADS * A_HEAD_DIM
B_QK = B_HEADS * B_KEY_DIM
B_V = B_HEADS * B_VAL_DIM
SPLIT_SIZES = (A_Q, A_KV, A_KV, B_QK, B_QK, B_V, B_V, GATE_RANK, GATE_RANK, C_WIDTH, C_WIDTH)
D_IN = sum(SPLIT_SIZES)
D_MIX = A_Q + B_V + C_WIDTH

kernel_name = 'hymba_style_parallel_hybrid_encoder'


def rmsnorm(x, g):
    xf = x.astype(jnp.float32)
    y = xf * lax.rsqrt(jnp.mean(jnp.square(xf), axis=-1, keepdims=True) + EPS)
    return (y * g.astype(jnp.float32)).astype(x.dtype)


def t5_bucket(rel):
    nb = N_BUCKETS // 2
    max_exact = nb // 2
    ret = jnp.where(rel > 0, nb, 0)
    n = jnp.abs(rel)
    nf = jnp.maximum(n, 1).astype(jnp.float32)
    large = max_exact + (jnp.log(nf / max_exact) / math.log(MAX_DISTANCE / max_exact)
                         * (nb - max_exact)).astype(jnp.int32)
    large = jnp.minimum(large, nb - 1)
    return ret + jnp.where(n < max_exact, n, large)


def windowed_gqa(q, k, v, rel_table, sink):
    bsz, seq = q.shape[:2]
    nblk = seq // BLOCK
    qb = q.reshape(bsz, nblk, BLOCK, A_KV_HEADS, A_GROUP, A_HEAD_DIM)

    def band(t):
        tp = jnp.pad(t, ((0, 0), (BLOCK, BLOCK), (0, 0), (0, 0)))
        tp = tp.reshape(bsz, nblk + 2, BLOCK, A_KV_HEADS, A_HEAD_DIM)
        return jnp.concatenate([tp[:, :-2], tp[:, 1:-1], tp[:, 2:]], axis=2)

    kb, vb = band(k), band(v)
    s = jnp.einsum('bnqhgd,bnkhd->bnhgqk', qb, kb).astype(jnp.float32) * (A_HEAD_DIM ** -0.5)
    qi = jnp.arange(BLOCK)[:, None]
    kj = jnp.arange(3 * BLOCK)[None, :]
    rel = kj - BLOCK - qi
    bias = rel_table.astype(jnp.float32)[t5_bucket(rel)]
    bias = jnp.transpose(bias, (2, 0, 1)).reshape(A_KV_HEADS, A_GROUP, BLOCK, 3 * BLOCK)
    kpos = jnp.arange(nblk)[:, None] * BLOCK + jnp.arange(3 * BLOCK)[None, :] - BLOCK
    valid = (jnp.abs(rel) <= WINDOW)[None] & ((kpos >= 0) & (kpos < seq))[:, None, :]
    s = jnp.where(valid[None, :, None, None], s + bias, NEG_INF)
    sink_col = jnp.broadcast_to(sink.astype(jnp.float32).reshape(A_KV_HEADS, A_GROUP, 1, 1),
                                s.shape[:-1] + (1,))
    p = jax.nn.softmax(jnp.concatenate([s, sink_col], axis=-1), axis=-1)[..., :-1]
    o = jnp.einsum('bnhgqk,bnkhd->bnqhgd', p.astype(v.dtype), vb)
    return o.reshape(bsz, seq, A_Q)


def gla_chunked(q, k, v, log_a, inclusive):
    bsz, nh, seq, dk = q.shape
    dv = v.shape[-1]
    nc = seq // GLA_CHUNK
    q = q.reshape(bsz, nh, nc, GLA_CHUNK, dk)
    k = k.reshape(bsz, nh, nc, GLA_CHUNK, dk)
    log_a = log_a.reshape(bsz, nh, nc, GLA_CHUNK, dk)
    v = v.reshape(bsz, nh, nc, GLA_CHUNK, dv)
    b = jnp.cumsum(log_a, axis=3)
    idx = jnp.arange(GLA_CHUNK)
    mask = (idx[:, None] >= idx[None, :]) if inclusive else (idx[:, None] > idx[None, :])
    m3 = mask[:, :, None]
    diff = b[:, :, :, :, None, :] - b[:, :, :, None, :, :]
    decay = jnp.where(m3, jnp.exp(jnp.where(m3, diff, 0.0)), 0.0)
    attn = jnp.einsum('bhnid,bhnjd,bhnijd->bhnij', q, k, decay)
    o = jnp.einsum('bhnij,bhnjd->bhnid', attn, v)
    b_last = b[:, :, :, -1, :]
    u = jnp.einsum('bhncd,bhnce->bhnde', k * jnp.exp(b_last[:, :, :, None, :] - b), v)

    def step(state, xs):
        dec, un = xs
        return dec[..., None] * state + un, state

    s0 = jnp.zeros((bsz, nh, dk, dv), jnp.float32)
    _, s_prev = lax.scan(step, s0, (jnp.moveaxis(jnp.exp(b_last), 2, 0), jnp.moveaxis(u, 2, 0)))
    s_prev = jnp.moveaxis(s_prev, 0, 2)
    o = o + jnp.einsum('bhncd,bhnde->bhnce', q * jnp.exp(b), s_prev)
    return o.reshape(bsz, nh, seq, dv)


def gla_mixer(q, k, v, g, zf, zb, w2f, b2f, w2b, b2b, gn):
    bsz, seq = q.shape[:2]
    f32 = jnp.float32

    def heads(t, d):
        return t.astype(f32).reshape(bsz, seq, B_HEADS, d).transpose(0, 2, 1, 3)

    qh = heads(q, B_KEY_DIM) * (B_KEY_DIM ** -0.5)
    kh = heads(k, B_KEY_DIM)
    vh = heads(v, B_VAL_DIM)
    la_f = heads(jax.nn.log_sigmoid(zf.astype(f32) @ w2f.astype(f32) + b2f.astype(f32)) / GATE_TAU, B_KEY_DIM)
    la_b = heads(jax.nn.log_sigmoid(zb.astype(f32) @ w2b.astype(f32) + b2b.astype(f32)) / GATE_TAU, B_KEY_DIM)
    o_f = gla_chunked(qh, kh, vh, la_f, True)
    flip = lambda t: jnp.flip(t, axis=2)
    o_b = flip(gla_chunked(flip(qh), flip(kh), flip(vh), flip(la_b), False))
    o = o_f + o_b
    o = o * lax.rsqrt(jnp.mean(jnp.square(o), axis=-1, keepdims=True) + EPS)
    o = o.transpose(0, 2, 1, 3).reshape(bsz, seq, B_V) * gn.astype(f32)
    return (o * jax.nn.silu(g.astype(f32))).astype(q.dtype)


def linear_scan(a, u, reverse):
    def combine(l, r):
        return (l[0] * r[0], r[0] * l[1] + r[1])
    _, h = lax.associative_scan(combine, (a, u), reverse=reverse, axis=1)
    return h


def rglru_mixer(xc, y, conv_w, conv_b, w_a, b_a, w_x, b_x, lam):
    bsz, seq, _ = xc.shape
    f32 = jnp.float32
    xp = jnp.pad(xc, ((0, 0), (CONV_LEFT, CONV_WIDTH - 1 - CONV_LEFT), (0, 0)))
    xconv = sum(xp[:, j:j + seq] * conv_w[j] for j in range(CONV_WIDTH)) + conv_b
    xf = xconv.astype(f32)
    xblk = xf.reshape(bsz, seq, C_BLOCKS, C_BLOCK_DIM)
    r = jax.nn.sigmoid(jnp.einsum('btgi,sgij->sbtgj', xblk, w_a.astype(f32)).reshape(2, bsz, seq, C_WIDTH)
                       + b_a.astype(f32)[:, None, None])
    i = jax.nn.sigmoid(jnp.einsum('btgi,sgij->sbtgj', xblk, w_x.astype(f32)).reshape(2, bsz, seq, C_WIDTH)
                       + b_x.astype(f32)[:, None, None])
    log_a = -LRU_C * r * jax.nn.softplus(-lam.astype(f32))[:, None, None]
    a = jnp.exp(log_a)
    u = jnp.sqrt(-jnp.expm1(2.0 * log_a)) * (i * xf[None])
    h = linear_scan(a[0], u[0], False) + linear_scan(a[1], u[1], True)
    return (h * jax.nn.gelu(y.astype(f32))).astype(xc.dtype)


def cross_attention(xn, memn, wq, wk, wv, wo):
    bsz, seq, _ = xn.shape
    mlen = memn.shape[1]
    q = (xn @ wq).reshape(bsz, seq, X_HEADS, X_HEAD_DIM)
    k = (memn @ wk).reshape(bsz, mlen, X_HEADS, X_HEAD_DIM)
    v = (memn @ wv).reshape(bsz, mlen, X_HEADS, X_HEAD_DIM)
    s = jnp.einsum('bthd,bmhd->bhtm', q, k).astype(jnp.float32) * (X_HEAD_DIM ** -0.5)
    p = jax.nn.softmax(s, axis=-1)
    o = jnp.einsum('bhtm,bmhd->bthd', p.astype(v.dtype), v).reshape(bsz, seq, D_MODEL)
    return o @ wo


def _fwd_setup_inputs(seed: int = 0) -> dict:
    key = jax.random.key(seed)
    ks = jax.random.split(key, 32)
    f32 = jnp.float32
    L = DEPTH

    def nrm(k, shape, scale):
        return jax.random.normal(k, shape, f32) * scale

    def gain(k, shape):
        return 1.0 + 0.05 * jax.random.normal(k, shape, f32)

    a_init = jax.random.uniform(ks[17], (L, 2, C_WIDTH), f32, 0.9, 0.999) ** (1.0 / LRU_C)
    lru_lambda = jnp.log(a_init) - jnp.log1p(-a_init)
    return {
        'x': nrm(ks[0], (BATCH, SEQ, D_MODEL), 1.0),
        'mem': nrm(ks[1], (BATCH, MEM_LEN, D_MODEL), 1.0),
        'rel_bias': nrm(ks[2], (N_BUCKETS, A_HEADS), 0.5),
        'w_in': nrm(ks[3], (L, D_MODEL, D_IN), D_MODEL ** -0.5),
        'w_out': nrm(ks[4], (L, D_MIX, D_MODEL), D_MIX ** -0.5),
        'attn_sink': nrm(ks[5], (L, A_HEADS), 0.5),
        'gla_w2_f': nrm(ks[6], (L, GATE_RANK, B_QK), GATE_RANK ** -0.5),
        'gla_b2_f': nrm(ks[7], (L, B_QK), 0.1),
        'gla_w2_b': nrm(ks[8], (L, GATE_RANK, B_QK), GATE_RANK ** -0.5),
        'gla_b2_b': nrm(ks[9], (L, B_QK), 0.1),
        'gla_norm': gain(ks[10], (L, B_V)),
        'conv_w': nrm(ks[11], (L, CONV_WIDTH, C_WIDTH), CONV_WIDTH ** -0.5),
        'conv_b': nrm(ks[12], (L, C_WIDTH), 0.02),
        'lru_wa': nrm(ks[13], (L, 2, C_BLOCKS, C_BLOCK_DIM, C_BLOCK_DIM), C_BLOCK_DIM ** -0.5),
        'lru_ba': nrm(ks[14], (L, 2, C_WIDTH), 0.1),
        'lru_wx': nrm(ks[15], (L, 2, C_BLOCKS, C_BLOCK_DIM, C_BLOCK_DIM), C_BLOCK_DIM ** -0.5),
        'lru_bx': nrm(ks[16], (L, 2, C_WIDTH), 0.1),
        'lru_lambda': lru_lambda,
        'xq': nrm(ks[18], (L, D_MODEL, D_MODEL), D_MODEL ** -0.5),
        'xk': nrm(ks[19], (L, D_MODEL, D_MODEL), D_MODEL ** -0.5),
        'xv': nrm(ks[20], (L, D_MODEL, D_MODEL), D_MODEL ** -0.5),
        'xo': nrm(ks[21], (L, D_MODEL, D_MODEL), D_MODEL ** -0.5),
        'w_up': nrm(ks[22], (L, D_MODEL, D_FF), D_MODEL ** -0.5),
        'w_down': nrm(ks[23], (L, D_FF, D_MODEL), D_FF ** -0.5),
        'norm_mix_pre': gain(ks[24], (L, D_MODEL)),
        'norm_mix_post': gain(ks[25], (L, D_MODEL)),
        'norm_mem': gain(ks[26], (L, D_MODEL)),
        'norm_x_pre': gain(ks[27], (L, D_MODEL)),
        'norm_x_post': gain(ks[28], (L, D_MODEL)),
        'norm_ff_pre': gain(ks[29], (L, D_MODEL)),
        'norm_ff_post': gain(ks[30], (L, D_MODEL)),
    }


def _fwd_reference(x, mem, rel_bias, w_in, w_out, attn_sink, gla_w2_f, gla_b2_f, gla_w2_b, gla_b2_b,
              gla_norm, conv_w, conv_b, lru_wa, lru_ba, lru_wx, lru_bx, lru_lambda,
              xq, xk, xv, xo, w_up, w_down, norm_mix_pre, norm_mix_post, norm_mem,
              norm_x_pre, norm_x_post, norm_ff_pre, norm_ff_post):
    bsz, seq, _ = x.shape
    offsets = np.cumsum(SPLIT_SIZES)[:-1].tolist()
    for l in range(DEPTH):
        h = rmsnorm(x, norm_mix_pre[l])
        aq, ak, av, bq, bk, bv, bg, zf, zb, cx, cy = jnp.split(h @ w_in[l], offsets, axis=-1)
        oa = windowed_gqa(aq.reshape(bsz, seq, A_HEADS, A_HEAD_DIM),
                          ak.reshape(bsz, seq, A_KV_HEADS, A_HEAD_DIM),
                          av.reshape(bsz, seq, A_KV_HEADS, A_HEAD_DIM),
                          rel_bias, attn_sink[l])
        ob = gla_mixer(bq, bk, bv, bg, zf, zb, gla_w2_f[l], gla_b2_f[l], gla_w2_b[l], gla_b2_b[l], gla_norm[l])
        oc = rglru_mixer(cx, cy, conv_w[l], conv_b[l], lru_wa[l], lru_ba[l], lru_wx[l], lru_bx[l], lru_lambda[l])
        mixed = jnp.concatenate([oa, ob.astype(oa.dtype), oc.astype(oa.dtype)], axis=-1) @ w_out[l]
        x = x + rmsnorm(mixed, norm_mix_post[l])
        h = rmsnorm(x, norm_x_pre[l])
        memn = rmsnorm(mem, norm_mem[l])
        x = x + rmsnorm(cross_attention(h, memn, xq[l], xk[l], xv[l], xo[l]), norm_x_post[l])
        h = rmsnorm(x, norm_ff_pre[l])
        ff = jnp.square(jax.nn.relu(h @ w_up[l])) @ w_down[l]
        x = x + rmsnorm(ff, norm_ff_post[l])
    return x


import jax as _jax
import jax.numpy as _jnp

TWIN_FORMAT = 'train_step'
FWD_PARAMS = ['x', 'mem', 'rel_bias', 'w_in', 'w_out', 'attn_sink', 'gla_w2_f', 'gla_b2_f', 'gla_w2_b', 'gla_b2_b', 'gla_norm', 'conv_w', 'conv_b', 'lru_wa', 'lru_ba', 'lru_wx', 'lru_bx', 'lru_lambda', 'xq', 'xk', 'xv', 'xo', 'w_up', 'w_down', 'norm_mix_pre', 'norm_mix_post', 'norm_mem', 'norm_x_pre', 'norm_x_post', 'norm_ff_pre', 'norm_ff_post']
TWIN_WEIGHTS = ['rel_bias', 'w_in', 'w_out', 'attn_sink', 'gla_w2_f', 'gla_b2_f', 'gla_w2_b', 'gla_b2_b', 'gla_norm', 'conv_w', 'conv_b', 'lru_wa', 'lru_ba', 'lru_wx', 'lru_bx', 'lru_lambda', 'xq', 'xk', 'xv', 'xo', 'w_up', 'w_down', 'norm_mix_pre', 'norm_mix_post', 'norm_mem', 'norm_x_pre', 'norm_x_post', 'norm_ff_pre', 'norm_ff_post']
TWIN_DIFF_INPUT = 'x'
TWIN_INPUTS = ['x', 'mem', 'rel_bias', 'w_in', 'w_out', 'attn_sink', 'gla_w2_f', 'gla_b2_f', 'gla_w2_b', 'gla_b2_b', 'gla_norm', 'conv_w', 'conv_b', 'lru_wa', 'lru_ba', 'lru_wx', 'lru_bx', 'lru_lambda', 'xq', 'xk', 'xv', 'xo', 'w_up', 'w_down', 'norm_mix_pre', 'norm_mix_post', 'norm_mem', 'norm_x_pre', 'norm_x_post', 'norm_ff_pre', 'norm_ff_post', 'loss_target', 'm_rel_bias', 'm_w_in', 'm_w_out', 'm_attn_sink', 'm_gla_w2_f', 'm_gla_b2_f', 'm_gla_w2_b', 'm_gla_b2_b', 'm_gla_norm', 'm_conv_w', 'm_conv_b', 'm_lru_wa', 'm_lru_ba', 'm_lru_wx', 'm_lru_bx', 'm_lru_lambda', 'm_xq', 'm_xk', 'm_xv', 'm_xo', 'm_w_up', 'm_w_down', 'm_norm_mix_pre', 'm_norm_mix_post', 'm_norm_mem', 'm_norm_x_pre', 'm_norm_x_post', 'm_norm_ff_pre', 'm_norm_ff_post', 'v_rel_bias', 'v_w_in', 'v_w_out', 'v_attn_sink', 'v_gla_w2_f', 'v_gla_b2_f', 'v_gla_w2_b', 'v_gla_b2_b', 'v_gla_norm', 'v_conv_w', 'v_conv_b', 'v_lru_wa', 'v_lru_ba', 'v_lru_wx', 'v_lru_bx', 'v_lru_lambda', 'v_xq', 'v_xk', 'v_xv', 'v_xo', 'v_w_up', 'v_w_down', 'v_norm_mix_pre', 'v_norm_mix_post', 'v_norm_mem', 'v_norm_x_pre', 'v_norm_x_post', 'v_norm_ff_pre', 'v_norm_ff_post']
TWIN_OUTPUTS = ['loss', 'grad_x', 'grad_rel_bias', 'grad_w_in', 'grad_w_out', 'grad_attn_sink', 'grad_gla_w2_f', 'grad_gla_b2_f', 'grad_gla_w2_b', 'grad_gla_b2_b', 'grad_gla_norm', 'grad_conv_w', 'grad_conv_b', 'grad_lru_wa', 'grad_lru_ba', 'grad_lru_wx', 'grad_lru_bx', 'grad_lru_lambda', 'grad_xq', 'grad_xk', 'grad_xv', 'grad_xo', 'grad_w_up', 'grad_w_down', 'grad_norm_mix_pre', 'grad_norm_mix_post', 'grad_norm_mem', 'grad_norm_x_pre', 'grad_norm_x_post', 'grad_norm_ff_pre', 'grad_norm_ff_post', 'delta_rel_bias', 'delta_w_in', 'delta_w_out', 'delta_attn_sink', 'delta_gla_w2_f', 'delta_gla_b2_f', 'delta_gla_w2_b', 'delta_gla_b2_b', 'delta_gla_norm', 'delta_conv_w', 'delta_conv_b', 'delta_lru_wa', 'delta_lru_ba', 'delta_lru_wx', 'delta_lru_bx', 'delta_lru_lambda', 'delta_xq', 'delta_xk', 'delta_xv', 'delta_xo', 'delta_w_up', 'delta_w_down', 'delta_norm_mix_pre', 'delta_norm_mix_post', 'delta_norm_mem', 'delta_norm_x_pre', 'delta_norm_x_post', 'delta_norm_ff_pre', 'delta_norm_ff_post', 'new_m_rel_bias', 'new_m_w_in', 'new_m_w_out', 'new_m_attn_sink', 'new_m_gla_w2_f', 'new_m_gla_b2_f', 'new_m_gla_w2_b', 'new_m_gla_b2_b', 'new_m_gla_norm', 'new_m_conv_w', 'new_m_conv_b', 'new_m_lru_wa', 'new_m_lru_ba', 'new_m_lru_wx', 'new_m_lru_bx', 'new_m_lru_lambda', 'new_m_xq', 'new_m_xk', 'new_m_xv', 'new_m_xo', 'new_m_w_up', 'new_m_w_down', 'new_m_norm_mix_pre', 'new_m_norm_mix_post', 'new_m_norm_mem', 'new_m_norm_x_pre', 'new_m_norm_x_post', 'new_m_norm_ff_pre', 'new_m_norm_ff_post', 'new_v_rel_bias', 'new_v_w_in', 'new_v_w_out', 'new_v_attn_sink', 'new_v_gla_w2_f', 'new_v_gla_b2_f', 'new_v_gla_w2_b', 'new_v_gla_b2_b', 'new_v_gla_norm', 'new_v_conv_w', 'new_v_conv_b', 'new_v_lru_wa', 'new_v_lru_ba', 'new_v_lru_wx', 'new_v_lru_bx', 'new_v_lru_lambda', 'new_v_xq', 'new_v_xk', 'new_v_xv', 'new_v_xo', 'new_v_w_up', 'new_v_w_down', 'new_v_norm_mix_pre', 'new_v_norm_mix_post', 'new_v_norm_mem', 'new_v_norm_x_pre', 'new_v_norm_x_post', 'new_v_norm_ff_pre', 'new_v_norm_ff_post']
TWIN_LEAF_KINDS = {'loss': 'loss', 'grad_x': 'grad_x', 'grad_rel_bias': 'grad_w', 'grad_w_in': 'grad_w', 'grad_w_out': 'grad_w', 'grad_attn_sink': 'grad_w', 'grad_gla_w2_f': 'grad_w', 'grad_gla_b2_f': 'grad_w', 'grad_gla_w2_b': 'grad_w', 'grad_gla_b2_b': 'grad_w', 'grad_gla_norm': 'grad_w', 'grad_conv_w': 'grad_w', 'grad_conv_b': 'grad_w', 'grad_lru_wa': 'grad_w', 'grad_lru_ba': 'grad_w', 'grad_lru_wx': 'grad_w', 'grad_lru_bx': 'grad_w', 'grad_lru_lambda': 'grad_w', 'grad_xq': 'grad_w', 'grad_xk': 'grad_w', 'grad_xv': 'grad_w', 'grad_xo': 'grad_w', 'grad_w_up': 'grad_w', 'grad_w_down': 'grad_w', 'grad_norm_mix_pre': 'grad_w', 'grad_norm_mix_post': 'grad_w', 'grad_norm_mem': 'grad_w', 'grad_norm_x_pre': 'grad_w', 'grad_norm_x_post': 'grad_w', 'grad_norm_ff_pre': 'grad_w', 'grad_norm_ff_post': 'grad_w', 'delta_rel_bias': 'delta_w', 'delta_w_in': 'delta_w', 'delta_w_out': 'delta_w', 'delta_attn_sink': 'delta_w', 'delta_gla_w2_f': 'delta_w', 'delta_gla_b2_f': 'delta_w', 'delta_gla_w2_b': 'delta_w', 'delta_gla_b2_b': 'delta_w', 'delta_gla_norm': 'delta_w', 'delta_conv_w': 'delta_w', 'delta_conv_b': 'delta_w', 'delta_lru_wa': 'delta_w', 'delta_lru_ba': 'delta_w', 'delta_lru_wx': 'delta_w', 'delta_lru_bx': 'delta_w', 'delta_lru_lambda': 'delta_w', 'delta_xq': 'delta_w', 'delta_xk': 'delta_w', 'delta_xv': 'delta_w', 'delta_xo': 'delta_w', 'delta_w_up': 'delta_w', 'delta_w_down': 'delta_w', 'delta_norm_mix_pre': 'delta_w', 'delta_norm_mix_post': 'delta_w', 'delta_norm_mem': 'delta_w', 'delta_norm_x_pre': 'delta_w', 'delta_norm_x_post': 'delta_w', 'delta_norm_ff_pre': 'delta_w', 'delta_norm_ff_post': 'delta_w', 'new_m_rel_bias': 'new_m', 'new_m_w_in': 'new_m', 'new_m_w_out': 'new_m', 'new_m_attn_sink': 'new_m', 'new_m_gla_w2_f': 'new_m', 'new_m_gla_b2_f': 'new_m', 'new_m_gla_w2_b': 'new_m', 'new_m_gla_b2_b': 'new_m', 'new_m_gla_norm': 'new_m', 'new_m_conv_w': 'new_m', 'new_m_conv_b': 'new_m', 'new_m_lru_wa': 'new_m', 'new_m_lru_ba': 'new_m', 'new_m_lru_wx': 'new_m', 'new_m_lru_bx': 'new_m', 'new_m_lru_lambda': 'new_m', 'new_m_xq': 'new_m', 'new_m_xk': 'new_m', 'new_m_xv': 'new_m', 'new_m_xo': 'new_m', 'new_m_w_up': 'new_m', 'new_m_w_down': 'new_m', 'new_m_norm_mix_pre': 'new_m', 'new_m_norm_mix_post': 'new_m', 'new_m_norm_mem': 'new_m', 'new_m_norm_x_pre': 'new_m', 'new_m_norm_x_post': 'new_m', 'new_m_norm_ff_pre': 'new_m', 'new_m_norm_ff_post': 'new_m', 'new_v_rel_bias': 'new_v', 'new_v_w_in': 'new_v', 'new_v_w_out': 'new_v', 'new_v_attn_sink': 'new_v', 'new_v_gla_w2_f': 'new_v', 'new_v_gla_b2_f': 'new_v', 'new_v_gla_w2_b': 'new_v', 'new_v_gla_b2_b': 'new_v', 'new_v_gla_norm': 'new_v', 'new_v_conv_w': 'new_v', 'new_v_conv_b': 'new_v', 'new_v_lru_wa': 'new_v', 'new_v_lru_ba': 'new_v', 'new_v_lru_wx': 'new_v', 'new_v_lru_bx': 'new_v', 'new_v_lru_lambda': 'new_v', 'new_v_xq': 'new_v', 'new_v_xk': 'new_v', 'new_v_xv': 'new_v', 'new_v_xo': 'new_v', 'new_v_w_up': 'new_v', 'new_v_w_down': 'new_v', 'new_v_norm_mix_pre': 'new_v', 'new_v_norm_mix_post': 'new_v', 'new_v_norm_mem': 'new_v', 'new_v_norm_x_pre': 'new_v', 'new_v_norm_x_post': 'new_v', 'new_v_norm_ff_pre': 'new_v', 'new_v_norm_ff_post': 'new_v'}


def _forward(args):
    return _fwd_reference(*[args[k] for k in FWD_PARAMS])


def _output_shape():
    out = _jax.eval_shape(lambda: _forward(_fwd_setup_inputs(0)))
    return out.shape, out.dtype

N_MICROBATCH = 1
ADAM_LR = 0.001
ADAM_B1 = 0.9
ADAM_B2 = 0.999
ADAM_EPS = 1e-08
ADAM_WD = 0.01
ADAM_STEP = 10
PER_EXAMPLE_BATCH_AXIS = {'x': 0, 'mem': 0, 'loss_target': 0}
SHARED_INPUTS = []
_WEIGHT_DTYPES = {'rel_bias': _jnp.float32, 'w_in': _jnp.float32, 'w_out': _jnp.float32, 'attn_sink': _jnp.float32, 'gla_w2_f': _jnp.float32, 'gla_b2_f': _jnp.float32, 'gla_w2_b': _jnp.float32, 'gla_b2_b': _jnp.float32, 'gla_norm': _jnp.float32, 'conv_w': _jnp.float32, 'conv_b': _jnp.float32, 'lru_wa': _jnp.float32, 'lru_ba': _jnp.float32, 'lru_wx': _jnp.float32, 'lru_bx': _jnp.float32, 'lru_lambda': _jnp.float32, 'xq': _jnp.float32, 'xk': _jnp.float32, 'xv': _jnp.float32, 'xo': _jnp.float32, 'w_up': _jnp.float32, 'w_down': _jnp.float32, 'norm_mix_pre': _jnp.float32, 'norm_mix_post': _jnp.float32, 'norm_mem': _jnp.float32, 'norm_x_pre': _jnp.float32, 'norm_x_post': _jnp.float32, 'norm_ff_pre': _jnp.float32, 'norm_ff_post': _jnp.float32}
MOMENT_SCALE = {'rel_bias': 5.026137e-01, 'w_in': 3.993564e+00, 'w_out': 4.925449e+00, 'attn_sink': 6.537025e-02, 'gla_w2_f': 8.484293e-02, 'gla_b2_f': 2.835004e-01, 'gla_w2_b': 9.229518e-02, 'gla_b2_b': 2.736141e-01, 'gla_norm': 7.901109e-01, 'conv_w': 8.866060e+00, 'conv_b': 5.842444e+01, 'lru_wa': 8.564560e-01, 'lru_ba': 6.936605e-01, 'lru_wx': 1.700755e+00, 'lru_bx': 1.814531e+00, 'lru_lambda': 1.505689e+00, 'xq': 2.591369e+00, 'xk': 2.609140e+00, 'xv': 7.099002e+00, 'xo': 7.123697e+00, 'w_up': 1.814918e+00, 'w_down': 6.367508e+00, 'norm_mix_pre': 5.642744e+00, 'norm_mix_post': 9.555511e+00, 'norm_mem': 7.605283e+00, 'norm_x_pre': 2.545884e+00, 'norm_x_post': 1.120672e+01, 'norm_ff_pre': 3.530773e+00, 'norm_ff_post': 1.052213e+01}


def _to_microbatches(a, axis):
    t = _jnp.moveaxis(a, axis, 0)
    t = t.reshape((N_MICROBATCH, t.shape[0] // N_MICROBATCH) + t.shape[1:])
    return _jnp.moveaxis(t, 1, axis + 1)


def setup_inputs(seed: int = 0) -> dict:
    inp = _fwd_setup_inputs(seed)
    key = _jax.random.fold_in(_jax.random.key(seed), 7919)
    shape, _ = _output_shape()
    out = dict(inp)
    out["loss_target"] = _jax.random.normal(_jax.random.fold_in(key, 0), shape, _jnp.float32)
    for i, name in enumerate(TWIN_WEIGHTS):
        w = inp[name].astype(_jnp.float32)
        if MOMENT_SCALE is None:
            s = _jnp.sqrt(_jnp.mean(_jnp.square(w)) + 1e-30)
        else:
            s = MOMENT_SCALE[name]
        km, kv = _jax.random.split(_jax.random.fold_in(key, i + 1))
        out[name] = w
        out["m_" + name] = s * _jax.random.normal(km, w.shape, _jnp.float32)
        out["v_" + name] = (s * s) * _jax.random.uniform(kv, w.shape, _jnp.float32, 0.5, 1.5)
    if N_MICROBATCH > 1:
        for name, axis in PER_EXAMPLE_BATCH_AXIS.items():
            out[name] = _to_microbatches(out[name], axis)
    return {'x': out['x'], 'mem': out['mem'], 'rel_bias': out['rel_bias'], 'w_in': out['w_in'], 'w_out': out['w_out'], 'attn_sink': out['attn_sink'], 'gla_w2_f': out['gla_w2_f'], 'gla_b2_f': out['gla_b2_f'], 'gla_w2_b': out['gla_w2_b'], 'gla_b2_b': out['gla_b2_b'], 'gla_norm': out['gla_norm'], 'conv_w': out['conv_w'], 'conv_b': out['conv_b'], 'lru_wa': out['lru_wa'], 'lru_ba': out['lru_ba'], 'lru_wx': out['lru_wx'], 'lru_bx': out['lru_bx'], 'lru_lambda': out['lru_lambda'], 'xq': out['xq'], 'xk': out['xk'], 'xv': out['xv'], 'xo': out['xo'], 'w_up': out['w_up'], 'w_down': out['w_down'], 'norm_mix_pre': out['norm_mix_pre'], 'norm_mix_post': out['norm_mix_post'], 'norm_mem': out['norm_mem'], 'norm_x_pre': out['norm_x_pre'], 'norm_x_post': out['norm_x_post'], 'norm_ff_pre': out['norm_ff_pre'], 'norm_ff_post': out['norm_ff_post'], 'loss_target': out['loss_target'], 'm_rel_bias': out['m_rel_bias'], 'm_w_in': out['m_w_in'], 'm_w_out': out['m_w_out'], 'm_attn_sink': out['m_attn_sink'], 'm_gla_w2_f': out['m_gla_w2_f'], 'm_gla_b2_f': out['m_gla_b2_f'], 'm_gla_w2_b': out['m_gla_w2_b'], 'm_gla_b2_b': out['m_gla_b2_b'], 'm_gla_norm': out['m_gla_norm'], 'm_conv_w': out['m_conv_w'], 'm_conv_b': out['m_conv_b'], 'm_lru_wa': out['m_lru_wa'], 'm_lru_ba': out['m_lru_ba'], 'm_lru_wx': out['m_lru_wx'], 'm_lru_bx': out['m_lru_bx'], 'm_lru_lambda': out['m_lru_lambda'], 'm_xq': out['m_xq'], 'm_xk': out['m_xk'], 'm_xv': out['m_xv'], 'm_xo': out['m_xo'], 'm_w_up': out['m_w_up'], 'm_w_down': out['m_w_down'], 'm_norm_mix_pre': out['m_norm_mix_pre'], 'm_norm_mix_post': out['m_norm_mix_post'], 'm_norm_mem': out['m_norm_mem'], 'm_norm_x_pre': out['m_norm_x_pre'], 'm_norm_x_post': out['m_norm_x_post'], 'm_norm_ff_pre': out['m_norm_ff_pre'], 'm_norm_ff_post': out['m_norm_ff_post'], 'v_rel_bias': out['v_rel_bias'], 'v_w_in': out['v_w_in'], 'v_w_out': out['v_w_out'], 'v_attn_sink': out['v_attn_sink'], 'v_gla_w2_f': out['v_gla_w2_f'], 'v_gla_b2_f': out['v_gla_b2_f'], 'v_gla_w2_b': out['v_gla_w2_b'], 'v_gla_b2_b': out['v_gla_b2_b'], 'v_gla_norm': out['v_gla_norm'], 'v_conv_w': out['v_conv_w'], 'v_conv_b': out['v_conv_b'], 'v_lru_wa': out['v_lru_wa'], 'v_lru_ba': out['v_lru_ba'], 'v_lru_wx': out['v_lru_wx'], 'v_lru_bx': out['v_lru_bx'], 'v_lru_lambda': out['v_lru_lambda'], 'v_xq': out['v_xq'], 'v_xk': out['v_xk'], 'v_xv': out['v_xv'], 'v_xo': out['v_xo'], 'v_w_up': out['v_w_up'], 'v_w_down': out['v_w_down'], 'v_norm_mix_pre': out['v_norm_mix_pre'], 'v_norm_mix_post': out['v_norm_mix_post'], 'v_norm_mem': out['v_norm_mem'], 'v_norm_x_pre': out['v_norm_x_pre'], 'v_norm_x_post': out['v_norm_x_post'], 'v_norm_ff_pre': out['v_norm_ff_pre'], 'v_norm_ff_post': out['v_norm_ff_post']}


def _loss(weights, diff, rest, loss_target):
    with _jax.named_scope("forward"):
        args = {**rest, TWIN_DIFF_INPUT: diff, **{k: w.astype(_WEIGHT_DTYPES[k]) for k, w in weights.items()}}
        y = _forward(args)
    with _jax.named_scope("loss_head"):
        err = _jnp.square(y.astype(_jnp.float32) - loss_target)
        return 0.5 * _jnp.sum(_jnp.mean(err, axis=-1)) if err.ndim else 0.5 * err


def _adamw(w, g, m, v):
    m = ADAM_B1 * m + (1.0 - ADAM_B1) * g
    v = ADAM_B2 * v + (1.0 - ADAM_B2) * _jnp.square(g)
    m_hat = m / (1.0 - ADAM_B1 ** ADAM_STEP)
    v_hat = v / (1.0 - ADAM_B2 ** ADAM_STEP)
    delta = -ADAM_LR * (m_hat / (_jnp.sqrt(v_hat) + ADAM_EPS) + ADAM_WD * w)
    return delta, m, v


def reference(x, mem, rel_bias, w_in, w_out, attn_sink, gla_w2_f, gla_b2_f, gla_w2_b, gla_b2_b, gla_norm, conv_w, conv_b, lru_wa, lru_ba, lru_wx, lru_bx, lru_lambda, xq, xk, xv, xo, w_up, w_down, norm_mix_pre, norm_mix_post, norm_mem, norm_x_pre, norm_x_post, norm_ff_pre, norm_ff_post, loss_target, m_rel_bias, m_w_in, m_w_out, m_attn_sink, m_gla_w2_f, m_gla_b2_f, m_gla_w2_b, m_gla_b2_b, m_gla_norm, m_conv_w, m_conv_b, m_lru_wa, m_lru_ba, m_lru_wx, m_lru_bx, m_lru_lambda, m_xq, m_xk, m_xv, m_xo, m_w_up, m_w_down, m_norm_mix_pre, m_norm_mix_post, m_norm_mem, m_norm_x_pre, m_norm_x_post, m_norm_ff_pre, m_norm_ff_post, v_rel_bias, v_w_in, v_w_out, v_attn_sink, v_gla_w2_f, v_gla_b2_f, v_gla_w2_b, v_gla_b2_b, v_gla_norm, v_conv_w, v_conv_b, v_lru_wa, v_lru_ba, v_lru_wx, v_lru_bx, v_lru_lambda, v_xq, v_xk, v_xv, v_xo, v_w_up, v_w_down, v_norm_mix_pre, v_norm_mix_post, v_norm_mem, v_norm_x_pre, v_norm_x_post, v_norm_ff_pre, v_norm_ff_post):
    given = dict(x=x, mem=mem, rel_bias=rel_bias, w_in=w_in, w_out=w_out, attn_sink=attn_sink, gla_w2_f=gla_w2_f, gla_b2_f=gla_b2_f, gla_w2_b=gla_w2_b, gla_b2_b=gla_b2_b, gla_norm=gla_norm, conv_w=conv_w, conv_b=conv_b, lru_wa=lru_wa, lru_ba=lru_ba, lru_wx=lru_wx, lru_bx=lru_bx, lru_lambda=lru_lambda, xq=xq, xk=xk, xv=xv, xo=xo, w_up=w_up, w_down=w_down, norm_mix_pre=norm_mix_pre, norm_mix_post=norm_mix_post, norm_mem=norm_mem, norm_x_pre=norm_x_pre, norm_x_post=norm_x_post, norm_ff_pre=norm_ff_pre, norm_ff_post=norm_ff_post, loss_target=loss_target, m_rel_bias=m_rel_bias, m_w_in=m_w_in, m_w_out=m_w_out, m_attn_sink=m_attn_sink, m_gla_w2_f=m_gla_w2_f, m_gla_b2_f=m_gla_b2_f, m_gla_w2_b=m_gla_w2_b, m_gla_b2_b=m_gla_b2_b, m_gla_norm=m_gla_norm, m_conv_w=m_conv_w, m_conv_b=m_conv_b, m_lru_wa=m_lru_wa, m_lru_ba=m_lru_ba, m_lru_wx=m_lru_wx, m_lru_bx=m_lru_bx, m_lru_lambda=m_lru_lambda, m_xq=m_xq, m_xk=m_xk, m_xv=m_xv, m_xo=m_xo, m_w_up=m_w_up, m_w_down=m_w_down, m_norm_mix_pre=m_norm_mix_pre, m_norm_mix_post=m_norm_mix_post, m_norm_mem=m_norm_mem, m_norm_x_pre=m_norm_x_pre, m_norm_x_post=m_norm_x_post, m_norm_ff_pre=m_norm_ff_pre, m_norm_ff_post=m_norm_ff_post, v_rel_bias=v_rel_bias, v_w_in=v_w_in, v_w_out=v_w_out, v_attn_sink=v_attn_sink, v_gla_w2_f=v_gla_w2_f, v_gla_b2_f=v_gla_b2_f, v_gla_w2_b=v_gla_w2_b, v_gla_b2_b=v_gla_b2_b, v_gla_norm=v_gla_norm, v_conv_w=v_conv_w, v_conv_b=v_conv_b, v_lru_wa=v_lru_wa, v_lru_ba=v_lru_ba, v_lru_wx=v_lru_wx, v_lru_bx=v_lru_bx, v_lru_lambda=v_lru_lambda, v_xq=v_xq, v_xk=v_xk, v_xv=v_xv, v_xo=v_xo, v_w_up=v_w_up, v_w_down=v_w_down, v_norm_mix_pre=v_norm_mix_pre, v_norm_mix_post=v_norm_mix_post, v_norm_mem=v_norm_mem, v_norm_x_pre=v_norm_x_pre, v_norm_x_post=v_norm_x_post, v_norm_ff_pre=v_norm_ff_pre, v_norm_ff_post=v_norm_ff_post)
    weights = {n: given[n] for n in TWIN_WEIGHTS}
    shared = {n: given[n] for n in SHARED_INPUTS}
    per_example = {n: given[n] for n in ['x', 'mem']}
    grad_fn = _jax.value_and_grad(_loss, argnums=(0, 1))

    def one_microbatch(ex, loss_target):
        ex = dict(ex)
        diff = ex.pop(TWIN_DIFF_INPUT)
        return grad_fn(weights, diff, {**shared, **ex}, loss_target)

    if N_MICROBATCH == 1:
        loss, (grad_w, grad_x) = one_microbatch(per_example, given["loss_target"])
    else:
        def body(carry, xs):
            loss_sum, grad_sum = carry
            l_k, (gw_k, gx_k) = one_microbatch(xs[0], xs[1])
            with _jax.named_scope("update"):
                return (loss_sum + l_k, _jax.tree.map(_jnp.add, grad_sum, gw_k)), gx_k

        init = (_jnp.zeros((), _jnp.float32), _jax.tree.map(_jnp.zeros_like, weights))
        (loss, grad_w), grad_x = _jax.lax.scan(body, init, (per_example, given["loss_target"]))
    with _jax.named_scope("update"):
        delta_w, new_m, new_v = {}, {}, {}
        for n in TWIN_WEIGHTS:
            delta_w[n], new_m[n], new_v[n] = _adamw(weights[n], grad_w[n], given["m_" + n], given["v_" + n])
    return (loss, grad_x, *[grad_w[n] for n in TWIN_WEIGHTS], *[delta_w[n] for n in TWIN_WEIGHTS],
            *[new_m[n] for n in TWIN_WEIGHTS], *[new_v[n] for n in TWIN_WEIGHTS])
```

```python
import functools
import math

import numpy as np
import jax
import jax.numpy as jnp
from jax import lax
from jax.experimental import pallas as pl
from jax.experimental.pallas import tpu as pltpu

F32, BF16 = jnp.float32, jnp.bfloat16
HI = lax.Precision.HIGHEST
MESH = pl.DeviceIdType.MESH

VMEM_LIMIT_BYTES = 56 * 1024 * 1024
LANES = 128
SUBLANES = 8

D_MODEL = 2048
DEPTH = 4
A_HEAD_DIM = 128
A_HEADS = 8
A_KV_HEADS = 2
A_GROUP = 4
WINDOW = 128
BLOCK = 128
N_BUCKETS = 32
MAX_DISTANCE = 128
B_HEADS = 4
B_KEY_DIM = 64
B_VAL_DIM = 128
GATE_RANK = 16
GATE_TAU = 16.0
C_WIDTH = 512
C_BLOCKS = 4
C_BLOCK_DIM = 128
CONV_WIDTH = 4
CONV_LEFT = 2
LRU_C = 8.0
X_HEADS = 4
X_HEAD_DIM = 512
D_FF = 4 * D_MODEL
EPS = 1e-6
NEG_INF = -1e30
A_Q, A_KV, B_QK, B_V = 1024, 256, 256, 512
SPLIT_SIZES = (A_Q, A_KV, A_KV, B_QK, B_QK, B_V, B_V, GATE_RANK, GATE_RANK, C_WIDTH, C_WIDTH)
D_IN = sum(SPLIT_SIZES)
D_INP = 4224
OFF_AQ, OFF_AK, OFF_AV, OFF_BQ, OFF_BK, OFF_BV, OFF_BG, OFF_CX, OFF_CY, OFF_Z = (
    0, 1024, 1280, 1536, 1792, 2048, 2560, 3072, 3584, 4096)
GLA_CHUNK = 128

ADAM_LR, ADAM_B1, ADAM_B2, ADAM_EPS, ADAM_WD, ADAM_STEP = 0.001, 0.9, 0.999, 1e-08, 0.01, 10


def _cparams(sem=None):
    return pltpu.CompilerParams(dimension_semantics=sem, vmem_limit_bytes=VMEM_LIMIT_BYTES)


def _full_spec(a):
    nd = a.ndim
    return pl.BlockSpec(a.shape, lambda *_: (0,) * nd)


def _tup(r):
    return r if isinstance(r, tuple) else (r,)


def _row_ops(rows, tr):
    arrs, specs, widths = [], [], []
    for r in rows:
        arr, n, j = r if isinstance(r, tuple) else (r, r.shape[1], 0)
        arrs.append(arr)
        widths.append(n)
        specs.append(pl.BlockSpec((tr, n), lambda i, j=j: (i, j)))
    return arrs, specs, widths


def rowmap(name, f, rows, params, outs, tr, accs=()):
    rows, row_specs, _ = _row_ops(rows, tr)
    T = rows[0].shape[0]
    nin, nout, nacc = len(rows) + len(params), len(outs), len(accs)

    def body(*refs):
        res = _tup(f(*[r[...] for r in refs[:nin]]))
        for o, r in zip(refs[nin:nin + nout], res[:nout]):
            o[...] = r.astype(o.dtype)
        arefs = refs[nin + nout:]
        if nacc:
            @pl.when(pl.program_id(0) == 0)
            def _():
                for a in arefs:
                    a[...] = jnp.zeros(a.shape, a.dtype)
            for a, r in zip(arefs, res[nout:]):
                a[...] += r.astype(F32)

    in_specs = row_specs + [_full_spec(p) for p in params]
    out_specs = [pl.BlockSpec((tr, n), lambda i: (i, 0)) for n, _ in outs] + \
                [pl.BlockSpec(s, lambda i, nd=len(s): (0,) * nd) for s in accs]
    out_shape = [jax.ShapeDtypeStruct((T, n), d) for n, d in outs] + [jax.ShapeDtypeStruct(s, F32) for s in accs]
    res = pl.pallas_call(body, grid=(T // tr,), in_specs=in_specs, out_specs=out_specs, out_shape=out_shape,
                         compiler_params=_cparams(("arbitrary",)), name=name)(*rows, *params)
    return tuple(res)


def rowmap_bwd(name, f, rows, params, cots, tr, drow_dtypes, want_params):
    rows, row_specs, widths = _row_ops(rows, tr)
    T = rows[0].shape[0]
    nr, npar = len(rows), len(params)
    cot_arrays, cot_specs, _ = _row_ops([c for c in cots if c is not None], tr)
    nc = len(cot_arrays)
    ridx = [i for i, d in enumerate(drow_dtypes) if d is not None]
    pidx = [i for i, w in enumerate(want_params) if w]

    def body(*refs):
        rvals = [r[...] for r in refs[:nr]]
        pvals = [r[...] for r in refs[nr:nr + npar]]
        crefs = list(refs[nr + npar:nr + npar + nc])
        orefs = refs[nr + npar + nc:]
        outs, vjp = jax.vjp(f, *rvals, *pvals)
        outs = _tup(outs)
        cts = []
        for c, o in zip(cots, outs):
            cts.append(jnp.ones(o.shape, o.dtype) if c is None else crefs.pop(0)[...].astype(o.dtype))
        grads = vjp(tuple(cts) if len(cts) > 1 else cts[0])
        for o, i in zip(orefs[:len(ridx)], ridx):
            o[...] = grads[i].astype(o.dtype)
        prefs = orefs[len(ridx):]
        if prefs:
            @pl.when(pl.program_id(0) == 0)
            def _():
                for a in prefs:
                    a[...] = jnp.zeros(a.shape, a.dtype)
            for a, i in zip(prefs, pidx):
                a[...] += grads[nr + i].astype(F32)

    in_specs = row_specs + [_full_spec(p) for p in params] + cot_specs
    out_specs = [pl.BlockSpec((tr, widths[i]), lambda i: (i, 0)) for i in ridx] + [_full_spec(params[i]) for i in pidx]
    out_shape = [jax.ShapeDtypeStruct((T, widths[i]), drow_dtypes[i]) for i in ridx] + \
                [jax.ShapeDtypeStruct(params[i].shape, F32) for i in pidx]
    res = pl.pallas_call(body, grid=(T // tr,), in_specs=in_specs, out_specs=out_specs, out_shape=out_shape,
                         compiler_params=_cparams(("arbitrary",)), name=name)(*rows, *params, *cot_arrays)
    res = tuple(res)
    return res[:len(ridx)], res[len(ridx):]


def _pick(n, pref):
    best = None
    for d in range(LANES, min(n, pref) + 1, LANES):
        if n % d == 0:
            best = d
    return best if best is not None else n


def _spec2(arr, tile, pos):
    tr, tc = tile
    if arr.ndim == 2:
        return pl.BlockSpec((tr, tc), lambda i, j, k: pos(i, j, k))
    per = arr.shape[2] // tc

    def imap(i, j, k):
        r, c = pos(i, j, k)
        return (c // per, r, c % per)
    return pl.BlockSpec((None, tr, tc), imap)


def _dims2(arr):
    return (arr.shape[0], arr.shape[1]) if arr.ndim == 2 else (arr.shape[1], arr.shape[0] * arr.shape[2])


def mm(name, a, b, mode, outs, epilogue=None, extras=(), pm=1024, pn=1024, pk=512):
    ar, ac = _dims2(a)
    br, bc = _dims2(b)
    if mode == "nn":
        M, K, N = ar, ac, bc
    elif mode == "nt":
        M, K, N = ar, ac, br
    else:
        M, K, N = ac, ar, bc
    tm, tn, tk = _pick(M, pm), _pick(N, pn), _pick(K, pk)
    for arr in (a, b) + tuple(extras):
        if arr.ndim == 3:
            assert arr.shape[2] % LANES == 0
    if mode == "nn":
        a_spec = _spec2(a, (tm, tk), lambda i, j, k: (i, k))
        b_spec = _spec2(b, (tk, tn), lambda i, j, k: (k, j))
        dims = (((1,), (0,)), ((), ()))
    elif mode == "nt":
        a_spec = _spec2(a, (tm, tk), lambda i, j, k: (i, k))
        b_spec = _spec2(b, (tn, tk), lambda i, j, k: (j, k))
        dims = (((1,), (1,)), ((), ()))
    else:
        a_spec = _spec2(a, (tk, tm), lambda i, j, k: (k, i))
        b_spec = _spec2(b, (tk, tn), lambda i, j, k: (k, j))
        dims = (((0,), (0,)), ((), ()))
    nk = K // tk
    nex = len(extras)

    def body(*refs):
        a_ref, b_ref = refs[0], refs[1]
        ex_refs = refs[2:2 + nex]
        o_refs = refs[2 + nex:-1]
        acc = refs[-1]
        k = pl.program_id(2)

        @pl.when(k == 0)
        def _():
            acc[...] = jnp.zeros(acc.shape, F32)

        acc[...] += lax.dot_general(a_ref[...].astype(BF16), b_ref[...].astype(BF16), dims, preferred_element_type=F32)

        @pl.when(k == nk - 1)
        def _():
            r = acc[...]
            res = (r,) if epilogue is None else _tup(epilogue(r, *[e[...] for e in ex_refs]))
            for o, v in zip(o_refs, res):
                o[...] = v.astype(o.dtype)

    out_shape, out_specs = [], []
    for dt, chunks in outs:
        if chunks == 1:
            shp = (M, N)
        else:
            shp = (chunks, M, N // chunks)
        o = jax.ShapeDtypeStruct(shp, dt)
        out_shape.append(o)
        out_specs.append(_spec2(o, (tm, tn), lambda i, j, k: (i, j)))
    ex_specs = [_spec2(e, (tm, tn), lambda i, j, k: (i, j)) for e in extras]
    res = pl.pallas_call(
        body, grid=(M // tm, N // tn, nk), in_specs=[a_spec, b_spec] + ex_specs, out_specs=out_specs,
        out_shape=out_shape, scratch_shapes=[pltpu.VMEM((tm, tn), F32)],
        compiler_params=_cparams(("parallel", "parallel", "arbitrary")), name=name)(a, b, *extras)
    return tuple(res)


def _rms(x, g):
    return x * lax.rsqrt(jnp.mean(x * x, axis=-1, keepdims=True) + EPS) * g


def f_norm(x, g):
    return _rms(x, g)


def f_norm_keep(x, g):
    return x, _rms(x, g)


def f_resnorm(xp, m, gpost, gnext):
    xn = xp + _rms(m, gpost)
    return xn, _rms(xn, gnext)


def f_final_rows(xp, m, tgt, gpost):
    xn = xp + _rms(m, gpost)
    return 0.5 * jnp.mean(jnp.square(xn - tgt), axis=-1, keepdims=True)


def f_final_loss(xp, m, tgt, gpost):
    return jnp.sum(f_final_rows(xp, m, tgt, gpost), axis=0, keepdims=True)


def f_norm_twice(x, g):
    y = _rms(x, g)
    return y, y


def f_xattn(q, k, v):
    outs = []
    for h in range(X_HEADS):
        sl = slice(h * X_HEAD_DIM, (h + 1) * X_HEAD_DIM)
        s = lax.dot_general(q[:, sl].astype(BF16), k[:, sl].astype(BF16), (((1,), (1,)), ((), ())),
                            preferred_element_type=F32) * (X_HEAD_DIM ** -0.5)
        m = jnp.max(s, axis=-1, keepdims=True)
        e = jnp.exp(s - m)
        p = e / jnp.sum(e, axis=-1, keepdims=True)
        outs.append(jnp.dot(p.astype(BF16), v[:, sl].astype(BF16), preferred_element_type=F32))
    return jnp.concatenate(outs, axis=1)


def f_adamw(w, g, m, v):
    m = ADAM_B1 * m + (1.0 - ADAM_B1) * g
    v = ADAM_B2 * v + (1.0 - ADAM_B2) * jnp.square(g)
    m_hat = m / (1.0 - ADAM_B1 ** ADAM_STEP)
    v_hat = v / (1.0 - ADAM_B2 ** ADAM_STEP)
    delta = -ADAM_LR * (m_hat / (jnp.sqrt(v_hat) + ADAM_EPS) + ADAM_WD * w)
    return delta, m, v


def f_sum8(*xs):
    t = xs[0]
    for x in xs[1:]:
        t = t + x
    return t


def t5_bucket_map():
    qi = jnp.arange(BLOCK)[:, None]
    kj = jnp.arange(3 * BLOCK)[None, :]
    rel = kj - BLOCK - qi
    nb = N_BUCKETS // 2
    max_exact = nb // 2
    ret = jnp.where(rel > 0, nb, 0)
    n = jnp.abs(rel)
    nf = jnp.maximum(n, 1).astype(jnp.float32)
    large = max_exact + (jnp.log(nf / max_exact) / math.log(MAX_DISTANCE / max_exact) * (nb - max_exact)).astype(jnp.int32)
    large = jnp.minimum(large, nb - 1)
    return (ret + jnp.where(n < max_exact, n, large)).astype(jnp.int32)


def bias_table_fwd(table, bucket):
    def body(t_ref, b_ref, o_ref):
        bk = b_ref[...]
        for h in range(A_HEADS):
            acc = jnp.zeros(bk.shape, F32)
            for b in range(N_BUCKETS):
                acc = jnp.where(bk == b, t_ref[b, h], acc)
            o_ref[h] = acc
    return pl.pallas_call(
        body, in_specs=[pl.BlockSpec(memory_space=pltpu.SMEM), pl.BlockSpec(memory_space=pltpu.VMEM)],
        out_specs=pl.BlockSpec(memory_space=pltpu.VMEM),
        out_shape=jax.ShapeDtypeStruct((A_HEADS, BLOCK, 3 * BLOCK), F32), name="bias_table_fwd")(table, bucket)


def bias_table_bwd(dbias_list, bucket):
    n = len(dbias_list)

    def body(*refs):
        b_ref, o_ref = refs[n], refs[n + 1]
        bk = b_ref[...]
        row = lax.broadcasted_iota(jnp.int32, (N_BUCKETS, LANES), 0)
        col = lax.broadcasted_iota(jnp.int32, (N_BUCKETS, LANES), 1)
        out = jnp.zeros((N_BUCKETS, LANES), F32)
        for h in range(A_HEADS):
            d = refs[0][h]
            for r in refs[1:n]:
                d = d + r[h]
            for b in range(N_BUCKETS):
                s = jnp.sum(jnp.where(bk == b, d, 0.0), keepdims=True)
                out = out + jnp.where((row == b) & (col == h), s, 0.0)
        o_ref[...] = out
    return pl.pallas_call(
        body, out_shape=jax.ShapeDtypeStruct((N_BUCKETS, LANES), F32), name="bias_table_bwd",
        compiler_params=_cparams())(*dbias_list, bucket)


def _attn_mask(n, nblk):
    i = lax.broadcasted_iota(jnp.int32, (BLOCK, 3 * BLOCK), 0)
    j = lax.broadcasted_iota(jnp.int32, (BLOCK, 3 * BLOCK), 1)
    kpos = n * BLOCK + j - BLOCK
    return (jnp.abs(j - BLOCK - i) <= WINDOW) & (kpos >= 0) & (kpos < nblk * BLOCK)


def f_attn_block(q, k3, v3, bias, sink, mask):
    kb, vb = k3.astype(BF16), v3.astype(BF16)
    outs = []
    for g in range(A_GROUP):
        qg = q[:, g * A_HEAD_DIM:(g + 1) * A_HEAD_DIM].astype(BF16)
        s = lax.dot_general(qg, kb, (((1,), (1,)), ((), ())), preferred_element_type=F32) * (A_HEAD_DIM ** -0.5)
        s = jnp.where(mask, s + bias[g], NEG_INF)
        sk = sink[g:g + 1, :]
        m = jnp.maximum(jnp.max(s, axis=-1, keepdims=True), sk)
        e = jnp.exp(s - m)
        den = jnp.sum(e, axis=-1, keepdims=True) + jnp.exp(sk - m)
        p = e / den
        outs.append(jnp.dot(p.astype(BF16), vb, preferred_element_type=F32))
    return jnp.concatenate(outs, axis=1)


def _attn_in_specs(nblk):
    qw = A_GROUP * A_HEAD_DIM
    kc, vc = OFF_AK // A_HEAD_DIM, OFF_AV // A_HEAD_DIM
    return [
        pl.BlockSpec((BLOCK, qw), lambda h, n: (n, h)),
        pl.BlockSpec((BLOCK, A_HEAD_DIM), lambda h, n: (jnp.maximum(n - 1, 0), kc + h)),
        pl.BlockSpec((BLOCK, A_HEAD_DIM), lambda h, n: (n, kc + h)),
        pl.BlockSpec((BLOCK, A_HEAD_DIM), lambda h, n: (jnp.minimum(n + 1, nblk - 1), kc + h)),
        pl.BlockSpec((BLOCK, A_HEAD_DIM), lambda h, n: (jnp.maximum(n - 1, 0), vc + h)),
        pl.BlockSpec((BLOCK, A_HEAD_DIM), lambda h, n: (n, vc + h)),
        pl.BlockSpec((BLOCK, A_HEAD_DIM), lambda h, n: (jnp.minimum(n + 1, nblk - 1), vc + h)),
        pl.BlockSpec((A_GROUP, BLOCK, 3 * BLOCK), lambda h, n: (h, 0, 0)),
        pl.BlockSpec((None, A_GROUP, 1), lambda h, n: (h, 0, 0)),
    ]


def attn_fwd(proj, bias, sink):
    T = proj.shape[0]
    nblk = T // BLOCK

    def body(q_ref, k0, k1, k2, v0, v1, v2, b_ref, s_ref, o_ref):
        n = pl.program_id(1)
        k3 = jnp.concatenate([k0[...], k1[...], k2[...]], axis=0)
        v3 = jnp.concatenate([v0[...], v1[...], v2[...]], axis=0)
        o = f_attn_block(q_ref[...], k3, v3, b_ref[...], s_ref[...], _attn_mask(n, nblk))
        o_ref[...] = o.astype(o_ref.dtype)

    return pl.pallas_call(
        body, grid=(A_KV_HEADS, nblk), in_specs=_attn_in_specs(nblk),
        out_specs=pl.BlockSpec((BLOCK, A_GROUP * A_HEAD_DIM), lambda h, n: (n, h)),
        out_shape=jax.ShapeDtypeStruct((T, A_Q), BF16),
        compiler_params=_cparams(("arbitrary", "arbitrary")), name="attn_fwd")(
            proj, proj, proj, proj, proj, proj, proj, bias, sink)


def attn_bwd(proj, bias, sink, dcat):
    T = proj.shape[0]
    nblk = T // BLOCK
    qw = A_GROUP * A_HEAD_DIM

    def body(q_ref, k0, k1, k2, v0, v1, v2, b_ref, s_ref, do_ref, dq_ref, dk_ref, dv_ref, db_ref, ds_ref, dk_acc, dv_acc):
        n = pl.program_id(1)

        @pl.when(n == 0)
        def _():
            dk_acc[...] = jnp.zeros(dk_acc.shape, F32)
            dv_acc[...] = jnp.zeros(dv_acc.shape, F32)
            db_ref[...] = jnp.zeros(db_ref.shape, F32)
            ds_ref[...] = jnp.zeros(ds_ref.shape, F32)

        k3 = jnp.concatenate([k0[...], k1[...], k2[...]], axis=0)
        v3 = jnp.concatenate([v0[...], v1[...], v2[...]], axis=0)
        mask = _attn_mask(n, nblk)
        _, vjp = jax.vjp(lambda q, k, v, b, s: f_attn_block(q, k, v, b, s, mask), q_ref[...], k3, v3, b_ref[...], s_ref[...])
        dq, dk3, dv3, db, ds = vjp(do_ref[...])
        dq_ref[...] = dq.astype(dq_ref.dtype)
        db_ref[...] += db
        ds_ref[...] += ds
        mid = pl.multiple_of(n * BLOCK, BLOCK)
        dk_acc[pl.ds(mid, BLOCK), :] += dk3[BLOCK:2 * BLOCK]
        dv_acc[pl.ds(mid, BLOCK), :] += dv3[BLOCK:2 * BLOCK]

        @pl.when(n > 0)
        def _():
            lo = pl.multiple_of((n - 1) * BLOCK, BLOCK)
            dk_acc[pl.ds(lo, BLOCK), :] += dk3[0:BLOCK]
            dv_acc[pl.ds(lo, BLOCK), :] += dv3[0:BLOCK]

        @pl.when(n < nblk - 1)
        def _():
            hi = pl.multiple_of((n + 1) * BLOCK, BLOCK)
            dk_acc[pl.ds(hi, BLOCK), :] += dk3[2 * BLOCK:3 * BLOCK]
            dv_acc[pl.ds(hi, BLOCK), :] += dv3[2 * BLOCK:3 * BLOCK]

        @pl.when(n == nblk - 1)
        def _():
            dk_ref[...] = dk_acc[...].astype(dk_ref.dtype)
            dv_ref[...] = dv_acc[...].astype(dv_ref.dtype)

    in_specs = _attn_in_specs(nblk) + [pl.BlockSpec((BLOCK, qw), lambda h, n: (n, h))]
    out_specs = [
        pl.BlockSpec((BLOCK, qw), lambda h, n: (n, h)),
        pl.BlockSpec((T, A_HEAD_DIM), lambda h, n: (0, h)),
        pl.BlockSpec((T, A_HEAD_DIM), lambda h, n: (0, h)),
        pl.BlockSpec((A_GROUP, BLOCK, 3 * BLOCK), lambda h, n: (h, 0, 0)),
        pl.BlockSpec((None, A_GROUP, 1), lambda h, n: (h, 0, 0)),
    ]
    out_shape = [
        jax.ShapeDtypeStruct((T, A_Q), BF16), jax.ShapeDtypeStruct((T, A_KV), BF16), jax.ShapeDtypeStruct((T, A_KV), BF16),
        jax.ShapeDtypeStruct((A_HEADS, BLOCK, 3 * BLOCK), F32), jax.ShapeDtypeStruct((A_KV_HEADS, A_GROUP, 1), F32),
    ]
    return pl.pallas_call(
        body, grid=(A_KV_HEADS, nblk), in_specs=in_specs, out_specs=out_specs, out_shape=out_shape,
        scratch_shapes=[pltpu.VMEM((T, A_HEAD_DIM), F32), pltpu.VMEM((T, A_HEAD_DIM), F32)],
        compiler_params=_cparams(("arbitrary", "arbitrary")), name="attn_bwd")(
            proj, proj, proj, proj, proj, proj, proj, bias, sink, dcat)


def f_gla_gate(z, w2f, b2f, w2b, b2b):
    laf = jax.nn.log_sigmoid(jnp.dot(z, w2f, precision=HI, preferred_element_type=F32) + b2f) / GATE_TAU
    lab = jax.nn.log_sigmoid(jnp.dot(z, w2b, precision=HI, preferred_element_type=F32) + b2b) / GATE_TAU
    return laf, lab


def f_gla_post(o, g, gn):
    outs = []
    for h in range(B_HEADS):
        sl = slice(h * B_VAL_DIM, (h + 1) * B_VAL_DIM)
        oh = o[:, sl]
        outs.append(oh * lax.rsqrt(jnp.mean(oh * oh, axis=-1, keepdims=True) + EPS))
    return jnp.concatenate(outs, axis=1) * gn * jax.nn.silu(g)


def _gla_consts(forward):
    C = GLA_CHUNK
    i = lax.broadcasted_iota(jnp.int32, (C, C), 0)
    j = lax.broadcasted_iota(jnp.int32, (C, C), 1)
    if forward:
        return (j <= i).astype(F32), j <= i
    return (j >= i).astype(F32), j > i


def _gla_chunk(q, k, v, la, st, tri, msk, forward):
    C = q.shape[0]
    b = jnp.dot(tri, la, precision=HI, preferred_element_type=F32)
    bl = b[C - 1:C] if forward else b[0:1]
    qe = (q * (B_KEY_DIM ** -0.5)) * jnp.exp(b)
    ke = k * jnp.exp(-b)
    kl = k * jnp.exp(bl - b)
    att = lax.dot_general(qe.astype(BF16), ke.astype(BF16), (((1,), (1,)), ((), ())), preferred_element_type=F32)
    att = jnp.where(msk, att, 0.0)
    o = jnp.dot(att.astype(BF16), v.astype(BF16), preferred_element_type=F32)
    o = o + lax.dot_general(qe.astype(BF16), st.astype(BF16), (((1,), (1,)), ((), ())), preferred_element_type=F32)
    st_new = st * jnp.exp(bl) + lax.dot_general(v.astype(BF16), kl.astype(BF16), (((0,), (0,)), ((), ())),
                                                preferred_element_type=F32)
    return o, st_new


def _gla_specs(T):
    qc, kc, vc = OFF_BQ // LANES, OFF_BK // LANES, OFF_BV // (2 * B_VAL_DIM)
    return [
        pl.BlockSpec((T, LANES), lambda p: (0, qc + p)),
        pl.BlockSpec((T, LANES), lambda p: (0, kc + p)),
        pl.BlockSpec((T, 2 * B_VAL_DIM), lambda p: (0, vc + p)),
        pl.BlockSpec((T, LANES), lambda p: (0, p)),
        pl.BlockSpec((T, LANES), lambda p: (0, p)),
    ]


def _rows(c):
    return pl.ds(pl.multiple_of(c * GLA_CHUNK, GLA_CHUNK), GLA_CHUNK)


def gla_fwd(proj, laf, lab):
    T = proj.shape[0]
    nc = T // GLA_CHUNK

    def body(q_ref, k_ref, v_ref, laf_ref, lab_ref, o_ref, ob_scr):
        tri_f, msk_f = _gla_consts(True)
        tri_b, msk_b = _gla_consts(False)
        zero = jnp.zeros((B_VAL_DIM, B_KEY_DIM), F32)

        def step(c, carry):
            rf, rb = _rows(c), _rows(nc - 1 - c)
            new = []
            for hh in range(2):
                ks = slice(hh * B_KEY_DIM, (hh + 1) * B_KEY_DIM)
                vs = slice(hh * B_VAL_DIM, (hh + 1) * B_VAL_DIM)
                o, s = _gla_chunk(q_ref[rf, ks], k_ref[rf, ks], v_ref[rf, vs], laf_ref[rf, ks], carry[2 * hh], tri_f, msk_f, True)
                o_ref[rf, vs] = o
                new.append(s)
                o, s = _gla_chunk(q_ref[rb, ks], k_ref[rb, ks], v_ref[rb, vs], lab_ref[rb, ks], carry[2 * hh + 1], tri_b, msk_b, False)
                ob_scr[rb, vs] = o
                new.append(s)
            return tuple(new)

        lax.fori_loop(0, nc, step, (zero,) * 4)
        o_ref[...] += ob_scr[...]

    return pl.pallas_call(
        body, grid=(B_HEADS // 2,), in_specs=_gla_specs(T),
        out_specs=pl.BlockSpec((T, 2 * B_VAL_DIM), lambda p: (0, p)),
        out_shape=jax.ShapeDtypeStruct((T, B_V), F32),
        scratch_shapes=[pltpu.VMEM((T, 2 * B_VAL_DIM), F32)],
        compiler_params=_cparams(("arbitrary",)), name="gla_fwd")(proj, proj, proj, laf, lab)


def gla_bwd(proj, laf, lab, do):
    T = proj.shape[0]
    nc = T // GLA_CHUNK
    SROWS = 2 * B_VAL_DIM

    def body(q_ref, k_ref, v_ref, laf_ref, lab_ref, do_ref, dq_ref, dk_ref, dv_ref, dlaf_ref, dlab_ref,
             sf_scr, sb_scr, dq_acc, dk_acc, dv_acc):
        tri_f, msk_f = _gla_consts(True)
        tri_b, msk_b = _gla_consts(False)
        zero = jnp.zeros((B_VAL_DIM, B_KEY_DIM), F32)
        dq_acc[...] = jnp.zeros(dq_acc.shape, F32)
        dk_acc[...] = jnp.zeros(dk_acc.shape, F32)
        dv_acc[...] = jnp.zeros(dv_acc.shape, F32)

        def srow(c, hh):
            return pl.ds(pl.multiple_of(c * SROWS + hh * B_VAL_DIM, B_VAL_DIM), B_VAL_DIM)

        def states(c, carry):
            cf, cb = c, nc - 1 - c
            rf, rb = _rows(cf), _rows(cb)
            new = []
            for hh in range(2):
                ks = slice(hh * B_KEY_DIM, (hh + 1) * B_KEY_DIM)
                vs = slice(hh * B_VAL_DIM, (hh + 1) * B_VAL_DIM)
                sf_scr[srow(cf, hh), :] = carry[2 * hh]
                _, s = _gla_chunk(q_ref[rf, ks], k_ref[rf, ks], v_ref[rf, vs], laf_ref[rf, ks], carry[2 * hh], tri_f, msk_f, True)
                new.append(s)
                sb_scr[srow(cb, hh), :] = carry[2 * hh + 1]
                _, s = _gla_chunk(q_ref[rb, ks], k_ref[rb, ks], v_ref[rb, vs], lab_ref[rb, ks], carry[2 * hh + 1], tri_b, msk_b, False)
                new.append(s)
            return tuple(new)

        lax.fori_loop(0, nc, states, (zero,) * 4)

        def back(c, carry):
            cf, cb = nc - 1 - c, c
            rf, rb = _rows(cf), _rows(cb)
            new = []
            for hh in range(2):
                ks = slice(hh * B_KEY_DIM, (hh + 1) * B_KEY_DIM)
                vs = slice(hh * B_VAL_DIM, (hh + 1) * B_VAL_DIM)
                for fwd, r, c_, la_ref, dla_ref, s_scr, g, tri, msk in (
                        (True, rf, cf, laf_ref, dlaf_ref, sf_scr, carry[2 * hh], tri_f, msk_f),
                        (False, rb, cb, lab_ref, dlab_ref, sb_scr, carry[2 * hh + 1], tri_b, msk_b)):
                    _, vjp = jax.vjp(
                        lambda q, k, v, la, st: _gla_chunk(q, k, v, la, st, tri, msk, fwd),
                        q_ref[r, ks], k_ref[r, ks], v_ref[r, vs], la_ref[r, ks], s_scr[srow(c_, hh), :])
                    dq, dk, dv, dla, dst = vjp((do_ref[r, vs], g))
                    dq_acc[r, ks] += dq
                    dk_acc[r, ks] += dk
                    dv_acc[r, vs] += dv
                    dla_ref[r, ks] = dla
                    new.append(dst)
            return tuple(new)

        lax.fori_loop(0, nc, back, (zero,) * 4)
        dq_ref[...] = dq_acc[...].astype(dq_ref.dtype)
        dk_ref[...] = dk_acc[...].astype(dk_ref.dtype)
        dv_ref[...] = dv_acc[...].astype(dv_ref.dtype)

    in_specs = _gla_specs(T) + [pl.BlockSpec((T, 2 * B_VAL_DIM), lambda p: (0, p))]
    out_specs = [
        pl.BlockSpec((T, LANES), lambda p: (0, p)), pl.BlockSpec((T, LANES), lambda p: (0, p)),
        pl.BlockSpec((T, 2 * B_VAL_DIM), lambda p: (0, p)),
        pl.BlockSpec((T, LANES), lambda p: (0, p)), pl.BlockSpec((T, LANES), lambda p: (0, p)),
    ]
    out_shape = [
        jax.ShapeDtypeStruct((T, B_QK), BF16), jax.ShapeDtypeStruct((T, B_QK), BF16), jax.ShapeDtypeStruct((T, B_V), BF16),
        jax.ShapeDtypeStruct((T, B_QK), F32), jax.ShapeDtypeStruct((T, B_QK), F32),
    ]
    scratch = [
        pltpu.VMEM((nc * SROWS, B_KEY_DIM), F32), pltpu.VMEM((nc * SROWS, B_KEY_DIM), F32),
        pltpu.VMEM((T, LANES), F32), pltpu.VMEM((T, LANES), F32), pltpu.VMEM((T, 2 * B_VAL_DIM), F32),
    ]
    return pl.pallas_call(
        body, grid=(B_HEADS // 2,), in_specs=in_specs, out_specs=out_specs, out_shape=out_shape, scratch_shapes=scratch,
        compiler_params=_cparams(("arbitrary",)), name="gla_bwd")(proj, proj, proj, laf, lab, do)


def _shift_raw(x, k):
    T = x.shape[0]
    r = lax.broadcasted_iota(jnp.int32, x.shape, 0)
    if k > 0:
        return jnp.where(r >= k, pltpu.roll(x, k, 0), 0.0)
    return jnp.where(r < T + k, pltpu.roll(x, T + k, 0), 0.0)


@functools.partial(jax.custom_vjp, nondiff_argnums=(1,))
def _shift(x, k):
    return _shift_raw(x, k)


_shift.defvjp(lambda x, k: (_shift_raw(x, k), None), lambda k, _, g: (_shift_raw(g, -k),))


def _scan_raw(a, u, reverse):
    T = a.shape[0]
    d = 1
    while d < T:
        k = -d if reverse else d
        u = a * _shift_raw(u, k) + u
        a = a * _shift_raw(a, k)
        d *= 2
    return u


@functools.partial(jax.custom_vjp, nondiff_argnums=(2,))
def _scan(a, u, reverse):
    return _scan_raw(a, u, reverse)


def _scan_f(a, u, reverse):
    h = _scan_raw(a, u, reverse)
    return h, (a, h)


def _scan_b(reverse, res, dh):
    a, h = res
    k = 1 if reverse else -1
    du = _scan_raw(_shift_raw(a, k), dh, not reverse)
    return du * _shift_raw(h, -k), du


_scan.defvjp(_scan_f, _scan_b)


def f_lru(cx, cy, cw, cb, wa, ba, wx, bx, lam, diff):
    shift, scan = (_shift, _scan) if diff else (_shift_raw, _scan_raw)
    xc = cx * cw[CONV_LEFT:CONV_LEFT + 1]
    for j in range(CONV_WIDTH):
        if j != CONV_LEFT:
            xc = xc + shift(cx, CONV_LEFT - j) * cw[j:j + 1]
    xc = xc + cb
    xb = xc.astype(BF16)
    h = None
    for s in range(2):
        r = jax.nn.sigmoid(jnp.dot(xb, wa[s].astype(BF16), preferred_element_type=F32) + ba[s:s + 1])
        i = jax.nn.sigmoid(jnp.dot(xb, wx[s].astype(BF16), preferred_element_type=F32) + bx[s:s + 1])
        log_a = -LRU_C * r * jax.nn.softplus(-lam[s:s + 1])
        a = jnp.exp(log_a)
        one_minus_a2 = -jnp.tanh(log_a) * (a * a + 1.0)
        u = jnp.sqrt(one_minus_a2) * (i * xc)
        hs = scan(a, u, s == 1)
        h = hs if h is None else h + hs
    return h * jax.nn.gelu(cy)


def _lru_specs(T):
    xc, yc = OFF_CX // LANES, OFF_CY // LANES
    return [
        pl.BlockSpec((T, LANES), lambda b: (0, xc + b)),
        pl.BlockSpec((T, LANES), lambda b: (0, yc + b)),
        pl.BlockSpec((CONV_WIDTH, LANES), lambda b: (0, b)),
        pl.BlockSpec((1, LANES), lambda b: (0, b)),
        pl.BlockSpec((2, None, C_BLOCK_DIM, C_BLOCK_DIM), lambda b: (0, b, 0, 0)),
        pl.BlockSpec((2, LANES), lambda b: (0, b)),
        pl.BlockSpec((2, None, C_BLOCK_DIM, C_BLOCK_DIM), lambda b: (0, b, 0, 0)),
        pl.BlockSpec((2, LANES), lambda b: (0, b)),
        pl.BlockSpec((2, LANES), lambda b: (0, b)),
    ]


def lru_fwd(proj, cw, cb, wa, ba, wx, bx, lam):
    T = proj.shape[0]

    def body(cx, cy, cw_r, cb_r, wa_r, ba_r, wx_r, bx_r, lam_r, o_ref):
        o = f_lru(cx[...], cy[...], cw_r[...], cb_r[...], wa_r[...], ba_r[...], wx_r[...], bx_r[...], lam_r[...], False)
        o_ref[...] = o.astype(o_ref.dtype)

    return pl.pallas_call(
        body, grid=(C_BLOCKS,), in_specs=_lru_specs(T), out_specs=pl.BlockSpec((T, LANES), lambda b: (0, b)),
        out_shape=jax.ShapeDtypeStruct((T, C_WIDTH), BF16),
        compiler_params=_cparams(("arbitrary",)), name="lru_fwd")(proj, proj, cw, cb, wa, ba, wx, bx, lam)


def lru_bwd(proj, cw, cb, wa, ba, wx, bx, lam, dcat):
    T = proj.shape[0]
    oc = (A_Q + B_V) // LANES

    def body(cx, cy, cw_r, cb_r, wa_r, ba_r, wx_r, bx_r, lam_r, do_ref, *outs):
        _, vjp = jax.vjp(functools.partial(f_lru, diff=True), cx[...], cy[...], cw_r[...], cb_r[...], wa_r[...],
                         ba_r[...], wx_r[...], bx_r[...], lam_r[...])
        grads = vjp(do_ref[...])
        for o, g in zip(outs, grads):
            o[...] = g.astype(o.dtype)

    specs = _lru_specs(T)
    out_specs = [pl.BlockSpec((T, LANES), lambda b: (0, b)), pl.BlockSpec((T, LANES), lambda b: (0, b))] + specs[2:]
    out_shape = [jax.ShapeDtypeStruct((T, C_WIDTH), BF16), jax.ShapeDtypeStruct((T, C_WIDTH), BF16)] + \
                [jax.ShapeDtypeStruct(p.shape, F32) for p in (cw, cb, wa, ba, wx, bx, lam)]
    return pl.pallas_call(
        body, grid=(C_BLOCKS,), in_specs=specs + [pl.BlockSpec((T, LANES), lambda b: (0, oc + b))],
        out_specs=out_specs, out_shape=out_shape,
        compiler_params=_cparams(("arbitrary",)), name="lru_bwd")(proj, proj, cw, cb, wa, ba, wx, bx, lam, dcat)


HBM_SPEC = pl.BlockSpec(memory_space=pltpu.HBM)


def _place():
    x, y, c = lax.axis_index("x"), lax.axis_index("y"), lax.axis_index("c")
    others = [(1 - x, y), (x, 1 - y), (1 - x, 1 - y)]
    return x, y, c, 2 * x + y, others


def _remote(src, dst, send_sems, recv_sems, k, to):
    return pltpu.make_async_remote_copy(src_ref=src, dst_ref=dst, send_sem=send_sems.at[k], recv_sem=recv_sems.at[k],
                                        device_id=to, device_id_type=MESH)


def all_gather8(name, blk):
    def body(x_ref, out_ref, send_sems, recv_sems, local_sem):
        x, y, c, _, others = _place()
        sibling = (x, y, 1 - c)

        def slab(px, py, pc):
            return out_ref.at[4 * px + 2 * py + pc]

        mine = pltpu.make_async_copy(x_ref, slab(x, y, c), local_sem)
        mine.start()
        first = [_remote(x_ref, slab(x, y, c), send_sems, recv_sems, 0, sibling)]
        first += [_remote(x_ref, slab(x, y, c), send_sems, recv_sems, 1 + j, (*ch, c)) for j, ch in enumerate(others)]
        for cp in first:
            cp.start()
        passed = [_remote(slab(*ch, c), slab(*ch, c), send_sems, recv_sems, 4 + j, sibling) for j, ch in enumerate(others)]
        for j, ch in enumerate(others):
            _remote(x_ref, slab(*ch, c), send_sems, recv_sems, 1 + j, (x, y, c)).wait_recv()
            passed[j].start()
        _remote(x_ref, slab(x, y, 1 - c), send_sems, recv_sems, 0, (x, y, c)).wait_recv()
        for j, ch in enumerate(others):
            _remote(x_ref, slab(*ch, 1 - c), send_sems, recv_sems, 4 + j, (x, y, c)).wait_recv()
        for cp in first + passed:
            cp.wait_send()
        mine.wait()

    return pl.pallas_call(
        body, out_shape=jax.ShapeDtypeStruct((8,) + blk.shape, blk.dtype), in_specs=[HBM_SPEC], out_specs=HBM_SPEC,
        scratch_shapes=[pltpu.SemaphoreType.DMA((7,)), pltpu.SemaphoreType.DMA((7,)), pltpu.SemaphoreType.DMA],
        name=name)(blk)


def chip_gather(name, shard):
    def body(x_ref, out_ref, send_sems, recv_sems, local_sem):
        x, y, c, chip, others = _place()
        sibling = (x, y, 1 - c)

        def idx(ch):
            return 2 * ch[0] + ch[1]

        mine = pltpu.make_async_copy(x_ref, out_ref.at[chip], local_sem)
        mine.start()
        first = [_remote(x_ref.at[c], out_ref.at[chip, c], send_sems, recv_sems, j, (*ch, c)) for j, ch in enumerate(others)]
        for cp in first:
            cp.start()
        passed = [_remote(out_ref.at[idx(ch), c], out_ref.at[idx(ch), c], send_sems, recv_sems, 3 + j, sibling)
                  for j, ch in enumerate(others)]
        for j, ch in enumerate(others):
            _remote(x_ref.at[c], out_ref.at[idx(ch), c], send_sems, recv_sems, j, (x, y, c)).wait_recv()
            passed[j].start()
        for j, ch in enumerate(others):
            _remote(x_ref.at[c], out_ref.at[idx(ch), 1 - c], send_sems, recv_sems, 3 + j, (x, y, c)).wait_recv()
        for cp in first + passed:
            cp.wait_send()
        mine.wait()

    return pl.pallas_call(
        body, out_shape=jax.ShapeDtypeStruct((4,) + shard.shape, shard.dtype), in_specs=[HBM_SPEC], out_specs=HBM_SPEC,
        scratch_shapes=[pltpu.SemaphoreType.DMA((6,)), pltpu.SemaphoreType.DMA((6,)), pltpu.SemaphoreType.DMA],
        name=name)(shard)


def chip_scatter(name, parts):
    def body(x_ref, out_ref, send_sems, recv_sems, local_sem):
        x, y, c, chip, others = _place()

        def idx(ch):
            return 2 * ch[0] + ch[1]

        mine = pltpu.make_async_copy(x_ref.at[chip], out_ref.at[chip], local_sem)
        mine.start()
        sends = [_remote(x_ref.at[idx(ch)], out_ref.at[chip], send_sems, recv_sems, j, (*ch, c)) for j, ch in enumerate(others)]
        for cp in sends:
            cp.start()
        for j, ch in enumerate(others):
            _remote(x_ref.at[chip], out_ref.at[idx(ch)], send_sems, recv_sems, j, (x, y, c)).wait_recv()
        for cp in sends:
            cp.wait_send()
        mine.wait()

    return pl.pallas_call(
        body, out_shape=jax.ShapeDtypeStruct(parts.shape, parts.dtype), in_specs=[HBM_SPEC], out_specs=HBM_SPEC,
        scratch_shapes=[pltpu.SemaphoreType.DMA((3,)), pltpu.SemaphoreType.DMA((3,)), pltpu.SemaphoreType.DMA],
        name=name)(parts)


def sibling_take(name, halves):
    def body(x_ref, out_ref, send_sem, recv_sem):
        x, y, c, _, _ = _place()
        cp = pltpu.make_async_remote_copy(src_ref=x_ref.at[1 - c], dst_ref=out_ref, send_sem=send_sem, recv_sem=recv_sem,
                                          device_id=(x, y, 1 - c), device_id_type=MESH)
        cp.start()
        cp.wait()

    return pl.pallas_call(
        body, out_shape=jax.ShapeDtypeStruct(halves.shape[1:], halves.dtype), in_specs=[HBM_SPEC], out_specs=HBM_SPEC,
        scratch_shapes=[pltpu.SemaphoreType.DMA, pltpu.SemaphoreType.DMA], name=name)(halves)


def sibling_pair(name, half):
    def body(x_ref, out_ref, send_sem, recv_sem, local_sem):
        x, y, c, _, _ = _place()
        mine = pltpu.make_async_copy(x_ref, out_ref.at[c], local_sem)
        mine.start()
        cp = pltpu.make_async_remote_copy(src_ref=x_ref, dst_ref=out_ref.at[c], send_sem=send_sem, recv_sem=recv_sem,
                                          device_id=(x, y, 1 - c), device_id_type=MESH)
        cp.start()
        pltpu.make_async_remote_copy(src_ref=x_ref, dst_ref=out_ref.at[1 - c], send_sem=send_sem, recv_sem=recv_sem,
                                     device_id=(x, y, c), device_id_type=MESH).wait_recv()
        cp.wait_send()
        mine.wait()

    return pl.pallas_call(
        body, out_shape=jax.ShapeDtypeStruct((2,) + half.shape, half.dtype), in_specs=[HBM_SPEC], out_specs=HBM_SPEC,
        scratch_shapes=[pltpu.SemaphoreType.DMA, pltpu.SemaphoreType.DMA, pltpu.SemaphoreType.DMA], name=name)(half)


def sum_slabs(name, r, out_dtype, tr):
    S, R, W = r.shape

    def body(*refs):
        t = refs[0][...].astype(F32)
        for s in range(1, S):
            t = t + refs[s][...].astype(F32)
        refs[S][...] = t.astype(out_dtype)

    return pl.pallas_call(
        body, grid=(R // tr,), in_specs=[pl.BlockSpec((None, tr, W), lambda i, s=s: (s, i, 0)) for s in range(S)],
        out_specs=pl.BlockSpec((tr, W), lambda i: (i, 0)), out_shape=jax.ShapeDtypeStruct((R, W), out_dtype),
        compiler_params=_cparams(("parallel",)), name=name)(*([r] * S))


def add_kept_half(name, halves, got, c, tr):
    _, R, W = halves.shape

    def body(c_ref, h_ref, g_ref, o_ref):
        o_ref[...] = (h_ref[...].astype(F32) + g_ref[...].astype(F32)).astype(o_ref.dtype)

    grid_spec = pltpu.PrefetchScalarGridSpec(
        num_scalar_prefetch=1, grid=(R // tr,),
        in_specs=[pl.BlockSpec((None, tr, W), lambda i, c_ref: (c_ref[0], i, 0)), pl.BlockSpec((tr, W), lambda i, c_ref: (i, 0))],
        out_specs=pl.BlockSpec((tr, W), lambda i, c_ref: (i, 0)))
    return pl.pallas_call(body, grid_spec=grid_spec, out_shape=jax.ShapeDtypeStruct((R, W), halves.dtype),
                          compiler_params=_cparams(("parallel",)), name=name)(c.reshape(1).astype(jnp.int32), halves, got)


PACK_W = 1024
BIG = ("w_in", "w_out", "xq", "xk", "xv", "xo", "w_up", "w_down")
PACK_ROWS = {"w_in": 2560, "w_out": 1024, "xq": 1024, "xk": 1024, "xv": 1024, "xo": 1024, "w_up": 4096, "w_down": 4096}
W_IN_ROWS = D_MODEL * (D_IN // 4) // PACK_W
PACK_TOTAL = sum(PACK_ROWS.values())
PACK_HALF = PACK_TOTAL // 2
SUM_TILE = 496
PACK_OFF = {}
_o = 0
for _n in BIG:
    PACK_OFF[_n] = _o
    _o += PACK_ROWS[_n]

_SPLIT_OFF = np.cumsum((0,) + SPLIT_SIZES)
_KORDER = (0, 1, 2, 3, 4, 5, 6, 9, 10, 7, 8)


def w_in_to_kernel_cols(w):
    parts = [w[..., _SPLIT_OFF[i]:_SPLIT_OFF[i + 1]] for i in _KORDER]
    parts.append(jnp.zeros(w.shape[:-1] + (D_INP - D_IN,), w.dtype))
    return jnp.concatenate(parts, axis=-1)


def w_in_from_kernel_cols(w):
    offs = np.cumsum((0,) + tuple(SPLIT_SIZES[i] for i in _KORDER))
    pos = {k: (offs[n], offs[n + 1]) for n, k in enumerate(_KORDER)}
    return jnp.concatenate([w[..., pos[i][0]:pos[i][1]] for i in range(len(SPLIT_SIZES))], axis=-1)


def pack_shards(shards, dtype):
    parts = []
    for n in BIG:
        s = shards[n]
        lead = s.shape[:-2]
        flat = s.astype(dtype).reshape(lead + (-1, PACK_W))
        if flat.shape[-2] < PACK_ROWS[n]:
            flat = jnp.concatenate([flat, jnp.zeros(lead + (PACK_ROWS[n] - flat.shape[-2], PACK_W), dtype)], axis=-2)
        parts.append(flat)
    return jnp.concatenate(parts, axis=-2)


def unpack_rows(packed, name):
    rows = W_IN_ROWS if name == "w_in" else PACK_ROWS[name]
    return packed[..., PACK_OFF[name]:PACK_OFF[name] + rows, :]


WEIGHTS = ("rel_bias", "w_in", "w_out", "attn_sink", "gla_w2_f", "gla_b2_f", "gla_w2_b", "gla_b2_b", "gla_norm", "conv_w",
           "conv_b", "lru_wa", "lru_ba", "lru_wx", "lru_bx", "lru_lambda", "xq", "xk", "xv", "xo", "w_up", "w_down",
           "norm_mix_pre", "norm_mix_post", "norm_mem", "norm_x_pre", "norm_x_post", "norm_ff_pre", "norm_ff_post")
SMALL = tuple(n for n in WEIGHTS if n not in BIG)
SMALL_SHARDED = ("gla_w2_f", "gla_w2_b", "conv_w", "lru_ba", "lru_bx", "lru_lambda")
ROW_TILE = 256
SMALL_TILE = 512


def _pack_small(arrs):
    flat = jnp.concatenate([a.reshape(-1).astype(F32) for a in arrs])
    n = flat.shape[0]
    rows = -(-n // (SMALL_TILE * LANES)) * SMALL_TILE
    return jnp.pad(flat, (0, rows * LANES - n)).reshape(rows, LANES)


def _unpack_small(buf, shapes):
    lead = buf.shape[:-2]
    flat = buf.reshape(lead + (-1,))
    out, o = [], 0
    for s in shapes:
        n = int(np.prod(s))
        out.append(flat[..., o:o + n].reshape(lead + tuple(s)))
        o += n
    return out


def _relu2(r):
    return r, jnp.square(jnp.maximum(r, 0.0))


def _drelu2(r, u):
    return r * (2.0 * jnp.maximum(u, 0.0))


def kernel(x, mem, rel_bias, w_in, w_out, attn_sink, gla_w2_f, gla_b2_f, gla_w2_b, gla_b2_b, gla_norm, conv_w, conv_b, lru_wa, lru_ba, lru_wx, lru_bx, lru_lambda, xq, xk, xv, xo, w_up, w_down, norm_mix_pre, norm_mix_post, norm_mem, norm_x_pre, norm_x_post, norm_ff_pre, norm_ff_post, loss_target, m_rel_bias, m_w_in, m_w_out, m_attn_sink, m_gla_w2_f, m_gla_b2_f, m_gla_w2_b, m_gla_b2_b, m_gla_norm, m_conv_w, m_conv_b, m_lru_wa, m_lru_ba, m_lru_wx, m_lru_bx, m_lru_lambda, m_xq, m_xk, m_xv, m_xo, m_w_up, m_w_down, m_norm_mix_pre, m_norm_mix_post, m_norm_mem, m_norm_x_pre, m_norm_x_post, m_norm_ff_pre, m_norm_ff_post, v_rel_bias, v_w_in, v_w_out, v_attn_sink, v_gla_w2_f, v_gla_b2_f, v_gla_w2_b, v_gla_b2_b, v_gla_norm, v_conv_w, v_conv_b, v_lru_wa, v_lru_ba, v_lru_wx, v_lru_bx, v_lru_lambda, v_xq, v_xk, v_xv, v_xo, v_w_up, v_w_down, v_norm_mix_pre, v_norm_mix_post, v_norm_mem, v_norm_x_pre, v_norm_x_post, v_norm_ff_pre, v_norm_ff_post):
    w_args = (rel_bias, w_in, w_out, attn_sink, gla_w2_f, gla_b2_f, gla_w2_b, gla_b2_b, gla_norm, conv_w, conv_b, lru_wa,
              lru_ba, lru_wx, lru_bx, lru_lambda, xq, xk, xv, xo, w_up, w_down, norm_mix_pre, norm_mix_post, norm_mem,
              norm_x_pre, norm_x_post, norm_ff_pre, norm_ff_post)
    m_args = (m_rel_bias, m_w_in, m_w_out, m_attn_sink, m_gla_w2_f, m_gla_b2_f, m_gla_w2_b, m_gla_b2_b, m_gla_norm, m_conv_w,
              m_conv_b, m_lru_wa, m_lru_ba, m_lru_wx, m_lru_bx, m_lru_lambda, m_xq, m_xk, m_xv, m_xo, m_w_up, m_w_down,
              m_norm_mix_pre, m_norm_mix_post, m_norm_mem, m_norm_x_pre, m_norm_x_post, m_norm_ff_pre, m_norm_ff_post)
    v_args = (v_rel_bias, v_w_in, v_w_out, v_attn_sink, v_gla_w2_f, v_gla_b2_f, v_gla_w2_b, v_gla_b2_b, v_gla_norm, v_conv_w,
              v_conv_b, v_lru_wa, v_lru_ba, v_lru_wx, v_lru_bx, v_lru_lambda, v_xq, v_xk, v_xv, v_xo, v_w_up, v_w_down,
              v_norm_mix_pre, v_norm_mix_post, v_norm_mem, v_norm_x_pre, v_norm_x_post, v_norm_ff_pre, v_norm_ff_post)
    Wt, Mo, Vo = dict(zip(WEIGHTS, w_args)), dict(zip(WEIGHTS, m_args)), dict(zip(WEIGHTS, v_args))
    x, mem, tgt = x[0], mem[0], loss_target[0]
    D = D_MODEL
    depth = w_in.shape[0]
    chip = 2 * lax.axis_index("x") + lax.axis_index("y")
    core = lax.axis_index("c")

    sm_shapes = [Wt[n].shape for n in SMALL_SHARDED]
    g8 = all_gather8("gather_small_weights", _pack_small([Wt[n] for n in SMALL_SHARDED]))
    per_chip = _unpack_small(g8[0::2], sm_shapes)
    whole = {n: jnp.concatenate([p[j] for j in range(4)], axis=-1) for n, p in zip(SMALL_SHARDED, per_chip)}

    Wfull = []
    for l in range(depth):
        shard = pack_shards({n: Wt[n][l] for n in BIG}, BF16).reshape(2, PACK_HALF, PACK_W)
        g = chip_gather("gather_layer_weights", shard).reshape(4, PACK_TOTAL, PACK_W)
        W = {}
        win = unpack_rows(g, "w_in").reshape(4, D, D_IN // 4).transpose(1, 0, 2).reshape(D, D_IN)
        W["w_in"] = w_in_to_kernel_cols(win)
        for n in ("w_out", "xq", "xk", "xv", "xo"):
            W[n] = unpack_rows(g, n).reshape(D, D)
        W["w_up"] = unpack_rows(g, "w_up").reshape(4, D, D)
        W["w_down"] = unpack_rows(g, "w_down").reshape(D_FF, D)
        Wfull.append(W)

    bucket = t5_bucket_map()
    bias = bias_table_fwd(rel_bias, bucket)

    def gain(name, l):
        return Wt[name][l][None]

    def layer_params(l):
        w2fp = jnp.zeros((LANES, B_QK), F32).at[0:GATE_RANK].set(whole["gla_w2_f"][l])
        w2bp = jnp.zeros((LANES, B_QK), F32).at[GATE_RANK:2 * GATE_RANK].set(whole["gla_w2_b"][l])
        gate = [w2fp, gla_b2_f[l][None], w2bp, gla_b2_b[l][None]]
        lru = [whole["conv_w"][l], conv_b[l][None], lru_wa[l], whole["lru_ba"][l], lru_wx[l], whole["lru_bx"][l],
               whole["lru_lambda"][l]]
        return attn_sink[l].reshape(A_KV_HEADS, A_GROUP, 1), gate, gla_norm[l][None], lru

    saved = []
    xcur = x
    (h1,) = rowmap("norm_first", f_norm, [x], [gain("norm_mix_pre", 0)], [(D, BF16)], ROW_TILE)
    loss_acc = None
    for l in range(depth):
        W = Wfull[l]
        sink3, gate, gn, lru = layer_params(l)
        (proj,) = mm("mm_in", h1, W["w_in"], "nn", [(F32, 1)], pn=1408)
        oa = attn_fwd(proj, bias, sink3)
        zrow, grow = (proj, LANES, OFF_Z // LANES), (proj, B_V, OFF_BG // B_V)
        laf, lab = rowmap("gla_gate", f_gla_gate, [zrow], gate, [(B_QK, F32), (B_QK, F32)], ROW_TILE)
        oraw = gla_fwd(proj, laf, lab)
        (ob,) = rowmap("gla_post", f_gla_post, [oraw, grow], [gn], [(B_V, BF16)], ROW_TILE)
        oc = lru_fwd(proj, *lru)
        cat = jnp.concatenate([oa, ob, oc], axis=1)
        (mixed,) = mm("mm_out", cat, W["w_out"], "nn", [(F32, 1)])
        x1, h2 = rowmap("resnorm_mix", f_resnorm, [xcur, mixed], [gain("norm_mix_post", l), gain("norm_x_pre", l)],
                        [(D, F32), (D, BF16)], ROW_TILE)
        (memn,) = rowmap("norm_mem", f_norm, [mem], [gain("norm_mem", l)], [(D, BF16)], ROW_TILE)
        (q,) = mm("mm_xq", h2, W["xq"], "nn", [(BF16, 1)])
        (k,) = mm("mm_xk", memn, W["xk"], "nn", [(F32, 1)])
        (v,) = mm("mm_xv", memn, W["xv"], "nn", [(F32, 1)])
        (o,) = rowmap("xattn", f_xattn, [q], [k, v], [(D, BF16)], ROW_TILE)
        (xo_out,) = mm("mm_xo", o, W["xo"], "nn", [(F32, 1)])
        x2, h3 = rowmap("resnorm_x", f_resnorm, [x1, xo_out], [gain("norm_x_post", l), gain("norm_ff_pre", l)],
                        [(D, F32), (D, BF16)], ROW_TILE)
        u, act = mm("mm_up", h3, W["w_up"], "nn", [(F32, 1), (BF16, 1)], epilogue=_relu2)
        (ff,) = mm("mm_down", act, W["w_down"], "nn", [(F32, 1)])
        saved.append(dict(x0=xcur, h1=h1, proj=proj, laf=laf, lab=lab, oraw=oraw, cat=cat, mixed=mixed, x1=x1, h2=h2,
                          memn=memn, q=q, k=k, v=v, o=o, xo_out=xo_out, x2=x2, h3=h3, u=u, act=act, ff=ff))
        if l < depth - 1:
            xcur, h1 = rowmap("resnorm_ff", f_resnorm, [x2, ff], [gain("norm_ff_post", l), gain("norm_mix_pre", l + 1)],
                              [(D, F32), (D, BF16)], ROW_TILE)
        else:
            (loss_acc,) = rowmap("final_loss", f_final_loss, [x2, ff, tgt], [gain("norm_ff_post", l)], [], ROW_TILE,
                                 accs=[(1, 1)])
    loss = lax.psum(loss_acc[0, 0], ("x", "y", "c"))

    small_g = {n: [None] * depth for n in SMALL if n != "rel_bias"}
    big_g = {n: [None] * depth for n in BIG}
    dbias_all = []
    dx_next = dh1_next = None
    grad_x = None
    for l in reversed(range(depth)):
        W, S = Wfull[l], saved[l]
        sink3, gate, gn, lru = layer_params(l)
        if l == depth - 1:
            (dx2, dff), (dgp,) = rowmap_bwd("final_bwd", f_final_rows, [S["x2"], S["ff"], tgt], [gain("norm_ff_post", l)],
                                            [None], ROW_TILE, [F32, F32, None], [True])
        else:
            (dx2, dff), (dgp, dgn_next) = rowmap_bwd(
                "resnorm_ff_bwd", f_resnorm, [S["x2"], S["ff"]], [gain("norm_ff_post", l), gain("norm_mix_pre", l + 1)],
                [dx_next, dh1_next], ROW_TILE, [F32, F32], [True, True])
            small_g["norm_mix_pre"][l + 1] = dgn_next[0]
        small_g["norm_ff_post"][l] = dgp[0]
        dW = {}
        (du,) = mm("mm_down_bwd", dff, W["w_down"], "nt", [(BF16, 1)], epilogue=_drelu2, extras=[S["u"]])
        (dW["w_down"],) = mm("mm_down_wgrad", S["act"], dff, "tn", [(BF16, 1)])
        (dW["w_up"],) = mm("mm_up_wgrad", S["h3"], du, "tn", [(BF16, 4)])
        (dh3,) = mm("mm_up_bwd", du, W["w_up"], "nt", [(F32, 1)])
        (dx1, dxo_out), (dg1, dg2) = rowmap_bwd(
            "resnorm_x_bwd", f_resnorm, [S["x1"], S["xo_out"]], [gain("norm_x_post", l), gain("norm_ff_pre", l)],
            [dx2, dh3], ROW_TILE, [F32, F32], [True, True])
        small_g["norm_x_post"][l], small_g["norm_ff_pre"][l] = dg1[0], dg2[0]
        (do,) = mm("mm_xo_bwd", dxo_out, W["xo"], "nt", [(F32, 1)])
        (dW["xo"],) = mm("mm_xo_wgrad", S["o"], dxo_out, "tn", [(BF16, 1)])
        (dq,), (dk, dv) = rowmap_bwd("xattn_bwd", f_xattn, [S["q"]], [S["k"], S["v"]], [do], ROW_TILE, [BF16], [True, True])
        (dW["xq"],) = mm("mm_xq_wgrad", S["h2"], dq, "tn", [(BF16, 1)])
        (dh2,) = mm("mm_xq_bwd", dq, W["xq"], "nt", [(F32, 1)])
        (dW["xk"],) = mm("mm_xk_wgrad", S["memn"], dk, "tn", [(BF16, 1)])
        (dW["xv"],) = mm("mm_xv_wgrad", S["memn"], dv, "tn", [(BF16, 1)])
        (dmk,) = mm("mm_xk_bwd", dk, W["xk"], "nt", [(F32, 1)])
        (dmv,) = mm("mm_xv_bwd", dv, W["xv"], "nt", [(F32, 1)])
        _, (dgm,) = rowmap_bwd("norm_mem_bwd", f_norm_twice, [mem], [gain("norm_mem", l)], [dmk, dmv], ROW_TILE, [None], [True])
        small_g["norm_mem"][l] = dgm[0]
        (dx0, dmixed), (dg1, dg2) = rowmap_bwd(
            "resnorm_mix_bwd", f_resnorm, [S["x0"], S["mixed"]], [gain("norm_mix_post", l), gain("norm_x_pre", l)],
            [dx1, dh2], ROW_TILE, [F32, F32], [True, True])
        small_g["norm_mix_post"][l], small_g["norm_x_pre"][l] = dg1[0], dg2[0]
        (dcat,) = mm("mm_out_bwd", dmixed, W["w_out"], "nt", [(F32, 1)])
        (dW["w_out"],) = mm("mm_out_wgrad", S["cat"], dmixed, "tn", [(BF16, 1)])
        proj = S["proj"]
        daq, dak, dav, dbias, dsink = attn_bwd(proj, bias, sink3, dcat)
        dbias_all.append(dbias)
        small_g["attn_sink"][l] = dsink.reshape(A_HEADS)
        zrow, grow = (proj, LANES, OFF_Z // LANES), (proj, B_V, OFF_BG // B_V)
        (doraw, dbg), (dgn,) = rowmap_bwd("gla_post_bwd", f_gla_post, [S["oraw"], grow], [gn], [(dcat, B_V, A_Q // B_V)],
                                          ROW_TILE, [F32, BF16], [True])
        dbq, dbk, dbv, dlaf, dlab = gla_bwd(proj, S["laf"], S["lab"], doraw)
        (dz,), (dw2fp, db2f, dw2bp, db2b) = rowmap_bwd("gla_gate_bwd", f_gla_gate, [zrow], gate, [dlaf, dlab], ROW_TILE,
                                                        [BF16], [True] * 4)
        small_g["gla_norm"][l] = dgn[0]
        small_g["gla_w2_f"][l], small_g["gla_b2_f"][l] = dw2fp[0:GATE_RANK], db2f[0]
        small_g["gla_w2_b"][l], small_g["gla_b2_b"][l] = dw2bp[GATE_RANK:2 * GATE_RANK], db2b[0]
        dcx, dcy, dcw, dcb, dwa, dba, dwx, dbx, dlam = lru_bwd(proj, *lru, dcat)
        small_g["conv_w"][l], small_g["conv_b"][l] = dcw, dcb[0]
        small_g["lru_wa"][l], small_g["lru_ba"][l], small_g["lru_wx"][l] = dwa, dba, dwx
        small_g["lru_bx"][l], small_g["lru_lambda"][l] = dbx, dlam
        dproj = jnp.concatenate([daq, dak, dav, dbq, dbk, dbv, dbg, dcx, dcy, dz], axis=1)
        (dW["w_in"],) = mm("mm_in_wgrad", S["h1"], dproj, "tn", [(BF16, 1)], pn=1408)
        (dh1,) = mm("mm_in_bwd", dproj, W["w_in"], "nt", [(F32, 1)])
        if l > 0:
            dx_next, dh1_next = dx0, dh1
        else:
            (grad_x,), (dg0,) = rowmap_bwd("norm_first_bwd", f_norm_keep, [x], [gain("norm_mix_pre", 0)], [dx0, dh1],
                                           ROW_TILE, [F32], [True])
            small_g["norm_mix_pre"][0] = dg0[0]

        shards = {"w_in": w_in_from_kernel_cols(dW["w_in"]).reshape(D, 4, D_IN // 4).transpose(1, 0, 2)}
        for n in ("w_out", "xq", "xk", "xv", "xo"):
            shards[n] = dW[n].reshape(4, D // 4, D)
        shards["w_up"] = dW["w_up"]
        shards["w_down"] = dW["w_down"].reshape(4, D_FF // 4, D)
        packed = pack_shards(shards, BF16)
        halves = packed.reshape(4, 2, PACK_HALF, PACK_W).transpose(1, 0, 2, 3).reshape(2, 4 * PACK_HALF, PACK_W)
        got = sibling_take("reduce_to_half_owner", halves)
        chip_sum = add_kept_half("reduce_chip_sum", halves, got, core, SUM_TILE).reshape(4, PACK_HALF, PACK_W)
        arrived = chip_scatter("reduce_to_shard_owner", chip_sum)
        total = sum_slabs("reduce_sum_chips", arrived, F32, SUM_TILE)
        full = sibling_pair("reduce_share_halves", total).reshape(PACK_TOTAL, PACK_W)
        for n in BIG:
            big_g[n][l] = unpack_rows(full, n).reshape(Wt[n].shape[1:])

    (dtab,) = (bias_table_bwd(dbias_all, bucket),)
    sg = {n: jnp.stack(small_g[n]) for n in small_g}
    sg["rel_bias"] = dtab[:, :A_HEADS]
    sg_shapes = [sg[n].shape for n in SMALL]
    contributions = all_gather8("gather_small_grads", _pack_small([sg[n] for n in SMALL]))
    (sg_sum,) = (sum_slabs("sum_small_grads", contributions, F32, SMALL_TILE),)
    sg = dict(zip(SMALL, _unpack_small(sg_sum, sg_shapes)))
    for n in SMALL_SHARDED:
        w = Wt[n].shape[-1]
        sg[n] = lax.dynamic_slice_in_dim(sg[n], chip * w, w, axis=sg[n].ndim - 1)

    grads, delta, new_m, new_v = {}, {}, {}, {}
    for n in BIG:
        g = jnp.stack(big_g[n])
        cols = g.shape[-1]
        two = lambda a: a.reshape(-1, cols)
        d_, m_, v_ = rowmap("adamw_" + n, f_adamw, [two(Wt[n]), two(g), two(Mo[n]), two(Vo[n])], [], [(cols, F32)] * 3, ROW_TILE)
        grads[n], delta[n], new_m[n], new_v[n] = g, d_.reshape(g.shape), m_.reshape(g.shape), v_.reshape(g.shape)
    shapes = [Wt[n].shape for n in SMALL]
    packs = [_pack_small([src[n] for n in SMALL]) for src in (Wt, sg, Mo, Vo)]
    d_, m_, v_ = rowmap("adamw_small", f_adamw, packs, [], [(LANES, F32)] * 3, SMALL_TILE)
    for n, a, b, c_ in zip(SMALL, _unpack_small(d_, shapes), _unpack_small(m_, shapes), _unpack_small(v_, shapes)):
        grads[n], delta[n], new_m[n], new_v[n] = sg[n], a, b, c_

    return (loss, grad_x[None], *[grads[n] for n in WEIGHTS], *[delta[n] for n in WEIGHTS],
            *[new_m[n] for n in WEIGHTS], *[new_v[n] for n in WEIGHTS])
```

```python
import functools
import math

import numpy as np
import jax
import jax.numpy as jnp
from jax import lax
from jax.experimental import pallas as pl
from jax.experimental.pallas import tpu as pltpu

F32, BF16 = jnp.float32, jnp.bfloat16
HI = lax.Precision.HIGHEST
MESH = pl.DeviceIdType.MESH

VMEM_LIMIT_BYTES = 56 * 1024 * 1024
LANES = 128
SUBLANES = 8

D_MODEL = 2048
DEPTH = 4
A_HEAD_DIM = 128
A_HEADS = 8
A_KV_HEADS = 2
A_GROUP = 4
WINDOW = 128
BLOCK = 128
N_BUCKETS = 32
MAX_DISTANCE = 128
B_HEADS = 4
B_KEY_DIM = 64
B_VAL_DIM = 128
GATE_RANK = 16
GATE_TAU = 16.0
C_WIDTH = 512
C_BLOCKS = 4
C_BLOCK_DIM = 128
CONV_WIDTH = 4
CONV_LEFT = 2
LRU_C = 8.0
X_HEADS = 4
X_HEAD_DIM = 512
D_FF = 4 * D_MODEL
EPS = 1e-6
NEG_INF = -1e30
A_Q, A_KV, B_QK, B_V = 1024, 256, 256, 512
SPLIT_SIZES = (A_Q, A_KV, A_KV, B_QK, B_QK, B_V, B_V, GATE_RANK, GATE_RANK, C_WIDTH, C_WIDTH)
D_IN = sum(SPLIT_SIZES)
D_INP = 4224
OFF_AQ, OFF_AK, OFF_AV, OFF_BQ, OFF_BK, OFF_BV, OFF_BG, OFF_CX, OFF_CY, OFF_Z = (
    0, 1024, 1280, 1536, 1792, 2048, 2560, 3072, 3584, 4096)
GLA_CHUNK = 128

ADAM_LR, ADAM_B1, ADAM_B2, ADAM_EPS, ADAM_WD, ADAM_STEP = 0.001, 0.9, 0.999, 1e-08, 0.01, 10


def _cparams(sem=None):
    return pltpu.CompilerParams(dimension_semantics=sem, vmem_limit_bytes=VMEM_LIMIT_BYTES)


def _full_spec(a):
    nd = a.ndim
    return pl.BlockSpec(a.shape, lambda *_: (0,) * nd)


def _tup(r):
    return r if isinstance(r, tuple) else (r,)


def _row_ops(rows, tr):
    arrs, specs, widths = [], [], []
    for r in rows:
        arr, n, j = r if isinstance(r, tuple) else (r, r.shape[1], 0)
        arrs.append(arr)
        widths.append(n)
        specs.append(pl.BlockSpec((tr, n), lambda i, j=j: (i, j)))
    return arrs, specs, widths


def rowmap(name, f, rows, params, outs, tr, accs=()):
    rows, row_specs, _ = _row_ops(rows, tr)
    T = rows[0].shape[0]
    nin, nout, nacc = len(rows) + len(params), len(outs), len(accs)

    def body(*refs):
        res = _tup(f(*[r[...] for r in refs[:nin]]))
        for o, r in zip(refs[nin:nin + nout], res[:nout]):
            o[...] = r.astype(o.dtype)
        arefs = refs[nin + nout:]
        if nacc:
            @pl.when(pl.program_id(0) == 0)
            def _():
                for a in arefs:
                    a[...] = jnp.zeros(a.shape, a.dtype)
            for a, r in zip(arefs, res[nout:]):
                a[...] += r.astype(F32)

    in_specs = row_specs + [_full_spec(p) for p in params]
    out_specs = [pl.BlockSpec((tr, n), lambda i: (i, 0)) for n, _ in outs] + \
                [pl.BlockSpec(s, lambda i, nd=len(s): (0,) * nd) for s in accs]
    out_shape = [jax.ShapeDtypeStruct((T, n), d) for n, d in outs] + [jax.ShapeDtypeStruct(s, F32) for s in accs]
    res = pl.pallas_call(body, grid=(T // tr,), in_specs=in_specs, out_specs=out_specs, out_shape=out_shape,
                         compiler_params=_cparams(("arbitrary",)), name=name)(*rows, *params)
    return tuple(res)


def rowmap_bwd(name, f, rows, params, cots, tr, drow_dtypes, want_params):
    rows, row_specs, widths = _row_ops(rows, tr)
    T = rows[0].shape[0]
    nr, npar = len(rows), len(params)
    cot_arrays, cot_specs, _ = _row_ops([c for c in cots if c is not None], tr)
    nc = len(cot_arrays)
    ridx = [i for i, d in enumerate(drow_dtypes) if d is not None]
    pidx = [i for i, w in enumerate(want_params) if w]

    def body(*refs):
        rvals = [r[...] for r in refs[:nr]]
        pvals = [r[...] for r in refs[nr:nr + npar]]
        crefs = list(refs[nr + npar:nr + npar + nc])
        orefs = refs[nr + npar + nc:]
        outs, vjp = jax.vjp(f, *rvals, *pvals)
        outs = _tup(outs)
        cts = []
        for c, o in zip(cots, outs):
            cts.append(jnp.ones(o.shape, o.dtype) if c is None else crefs.pop(0)[...].astype(o.dtype))
        grads = vjp(tuple(cts) if len(cts) > 1 else cts[0])
        for o, i in zip(orefs[:len(ridx)], ridx):
            o[...] = grads[i].astype(o.dtype)
        prefs = orefs[len(ridx):]
        if prefs:
            @pl.when(pl.program_id(0) == 0)
            def _():
                for a in prefs:
                    a[...] = jnp.zeros(a.shape, a.dtype)
            for a, i in zip(prefs, pidx):
                a[...] += grads[nr + i].astype(F32)

    in_specs = row_specs + [_full_spec(p) for p in params] + cot_specs
    out_specs = [pl.BlockSpec((tr, widths[i]), lambda i: (i, 0)) for i in ridx] + [_full_spec(params[i]) for i in pidx]
    out_shape = [jax.ShapeDtypeStruct((T, widths[i]), drow_dtypes[i]) for i in ridx] + \
                [jax.ShapeDtypeStruct(params[i].shape, F32) for i in pidx]
    res = pl.pallas_call(body, grid=(T // tr,), in_specs=in_specs, out_specs=out_specs, out_shape=out_shape,
                         compiler_params=_cparams(("arbitrary",)), name=name)(*rows, *params, *cot_arrays)
    res = tuple(res)
    return res[:len(ridx)], res[len(ridx):]


def _pick(n, pref):
    best = None
    for d in range(LANES, min(n, pref) + 1, LANES):
        if n % d == 0:
            best = d
    return best if best is not None else n


def _spec2(arr, tile, pos):
    tr, tc = tile
    if arr.ndim == 2:
        return pl.BlockSpec((tr, tc), lambda i, j, k: pos(i, j, k))
    per = arr.shape[2] // tc

    def imap(i, j, k):
        r, c = pos(i, j, k)
        return (c // per, r, c % per)
    return pl.BlockSpec((None, tr, tc), imap)


def _dims2(arr):
    return (arr.shape[0], arr.shape[1]) if arr.ndim == 2 else (arr.shape[1], arr.shape[0] * arr.shape[2])


def mm(name, a, b, mode, outs, epilogue=None, extras=(), pm=1024, pn=1024, pk=512):
    ar, ac = _dims2(a)
    br, bc = _dims2(b)
    if mode == "nn":
        M, K, N = ar, ac, bc
    elif mode == "nt":
        M, K, N = ar, ac, br
    else:
        M, K, N = ac, ar, bc
    tm, tn, tk = _pick(M, pm), _pick(N, pn), _pick(K, pk)
    for arr in (a, b) + tuple(extras):
        if arr.ndim == 3:
            assert arr.shape[2] % LANES == 0
    if mode == "nn":
        a_spec = _spec2(a, (tm, tk), lambda i, j, k: (i, k))
        b_spec = _spec2(b, (tk, tn), lambda i, j, k: (k, j))
        dims = (((1,), (0,)), ((), ()))
    elif mode == "nt":
        a_spec = _spec2(a, (tm, tk), lambda i, j, k: (i, k))
        b_spec = _spec2(b, (tn, tk), lambda i, j, k: (j, k))
        dims = (((1,), (1,)), ((), ()))
    else:
        a_spec = _spec2(a, (tk, tm), lambda i, j, k: (k, i))
        b_spec = _spec2(b, (tk, tn), lambda i, j, k: (k, j))
        dims = (((0,), (0,)), ((), ()))
    nk = K // tk
    nex = len(extras)

    def body(*refs):
        a_ref, b_ref = refs[0], refs[1]
        ex_refs = refs[2:2 + nex]
        o_refs = refs[2 + nex:-1]
        acc = refs[-1]
        k = pl.program_id(2)

        @pl.when(k == 0)
        def _():
            acc[...] = jnp.zeros(acc.shape, F32)

        acc[...] += lax.dot_general(a_ref[...].astype(BF16), b_ref[...].astype(BF16), dims, preferred_element_type=F32)

        @pl.when(k == nk - 1)
        def _():
            r = acc[...]
            res = (r,) if epilogue is None else _tup(epilogue(r, *[e[...] for e in ex_refs]))
            for o, v in zip(o_refs, res):
                o[...] = v.astype(o.dtype)

    out_shape, out_specs = [], []
    for dt, chunks in outs:
        if chunks == 1:
            shp = (M, N)
        else:
            shp = (chunks, M, N // chunks)
        o = jax.ShapeDtypeStruct(shp, dt)
        out_shape.append(o)
        out_specs.append(_spec2(o, (tm, tn), lambda i, j, k: (i, j)))
    ex_specs = [_spec2(e, (tm, tn), lambda i, j, k: (i, j)) for e in extras]
    res = pl.pallas_call(
        body, grid=(M // tm, N // tn, nk), in_specs=[a_spec, b_spec] + ex_specs, out_specs=out_specs,
        out_shape=out_shape, scratch_shapes=[pltpu.VMEM((tm, tn), F32)],
        compiler_params=_cparams(("parallel", "parallel", "arbitrary")), name=name)(a, b, *extras)
    return tuple(res)


def _rms(x, g):
    return x * lax.rsqrt(jnp.mean(x * x, axis=-1, keepdims=True) + EPS) * g


def f_norm(x, g):
    return _rms(x, g)


def f_norm_keep(x, g):
    return x, _rms(x, g)


def f_resnorm(xp, m, gpost, gnext):
    xn = xp + _rms(m, gpost)
    return xn, _rms(xn, gnext)


def f_final_rows(xp, m, tgt, gpost):
    xn = xp + _rms(m, gpost)
    return 0.5 * jnp.mean(jnp.square(xn - tgt), axis=-1, keepdims=True)


def f_final_loss(xp, m, tgt, gpost):
    return jnp.sum(f_final_rows(xp, m, tgt, gpost), axis=0, keepdims=True)


def f_norm_twice(x, g):
    y = _rms(x, g)
    return y, y


def f_xattn(q, k, v):
    outs = []
    for h in range(X_HEADS):
        sl = slice(h * X_HEAD_DIM, (h + 1) * X_HEAD_DIM)
        s = lax.dot_general(q[:, sl].astype(BF16), k[:, sl].astype(BF16), (((1,), (1,)), ((), ())),
                            preferred_element_type=F32) * (X_HEAD_DIM ** -0.5)
        m = jnp.max(s, axis=-1, keepdims=True)
        e = jnp.exp(s - m)
        p = e / jnp.sum(e, axis=-1, keepdims=True)
        outs.append(jnp.dot(p.astype(BF16), v[:, sl].astype(BF16), preferred_element_type=F32))
    return jnp.concatenate(outs, axis=1)


def f_adamw(w, g, m, v):
    m = ADAM_B1 * m + (1.0 - ADAM_B1) * g
    v = ADAM_B2 * v + (1.0 - ADAM_B2) * jnp.square(g)
    m_hat = m / (1.0 - ADAM_B1 ** ADAM_STEP)
    v_hat = v / (1.0 - ADAM_B2 ** ADAM_STEP)
    delta = -ADAM_LR * (m_hat / (jnp.sqrt(v_hat) + ADAM_EPS) + ADAM_WD * w)
    return delta, m, v


def f_sum8(*xs):
    t = xs[0]
    for x in xs[1:]:
        t = t + x
    return t


def t5_bucket_map():
    qi = jnp.arange(BLOCK)[:, None]
    kj = jnp.arange(3 * BLOCK)[None, :]
    rel = kj - BLOCK - qi
    nb = N_BUCKETS // 2
    max_exact = nb // 2
    ret = jnp.where(rel > 0, nb, 0)
    n = jnp.abs(rel)
    nf = jnp.maximum(n, 1).astype(jnp.float32)
    large = max_exact + (jnp.log(nf / max_exact) / math.log(MAX_DISTANCE / max_exact) * (nb - max_exact)).astype(jnp.int32)
    large = jnp.minimum(large, nb - 1)
    return (ret + jnp.where(n < max_exact, n, large)).astype(jnp.int32)


def bias_table_fwd(table, bucket):
    def body(t_ref, b_ref, o_ref):
        bk = b_ref[...]
        for h in range(A_HEADS):
            acc = jnp.zeros(bk.shape, F32)
            for b in range(N_BUCKETS):
                acc = jnp.where(bk == b, t_ref[b, h], acc)
            o_ref[h] = acc
    return pl.pallas_call(
        body, in_specs=[pl.BlockSpec(memory_space=pltpu.SMEM), pl.BlockSpec(memory_space=pltpu.VMEM)],
        out_specs=pl.BlockSpec(memory_space=pltpu.VMEM),
        out_shape=jax.ShapeDtypeStruct((A_HEADS, BLOCK, 3 * BLOCK), F32), name="bias_table_fwd")(table, bucket)


def bias_table_bwd(dbias_list, bucket):
    n = len(dbias_list)

    def body(*refs):
        b_ref, o_ref = refs[n], refs[n + 1]
        bk = b_ref[...]
        row = lax.broadcasted_iota(jnp.int32, (N_BUCKETS, LANES), 0)
        col = lax.broadcasted_iota(jnp.int32, (N_BUCKETS, LANES), 1)
        out = jnp.zeros((N_BUCKETS, LANES), F32)
        for h in range(A_HEADS):
            d = refs[0][h]
            for r in refs[1:n]:
                d = d + r[h]
            for b in range(N_BUCKETS):
                s = jnp.sum(jnp.where(bk == b, d, 0.0), keepdims=True)
                out = out + jnp.where((row == b) & (col == h), s, 0.0)
        o_ref[...] = out
    return pl.pallas_call(
        body, out_shape=jax.ShapeDtypeStruct((N_BUCKETS, LANES), F32), name="bias_table_bwd",
        compiler_params=_cparams())(*dbias_list, bucket)


def _attn_mask(n, nblk):
    i = lax.broadcasted_iota(jnp.int32, (BLOCK, 3 * BLOCK), 0)
    j = lax.broadcasted_iota(jnp.int32, (BLOCK, 3 * BLOCK), 1)
    kpos = n * BLOCK + j - BLOCK
    return (jnp.abs(j - BLOCK - i) <= WINDOW) & (kpos >= 0) & (kpos < nblk * BLOCK)


def f_attn_block(q, k3, v3, bias, sink, mask):
    kb, vb = k3.astype(BF16), v3.astype(BF16)
    outs = []
    for g in range(A_GROUP):
        qg = q[:, g * A_HEAD_DIM:(g + 1) * A_HEAD_DIM].astype(BF16)
        s = lax.dot_general(qg, kb, (((1,), (1,)), ((), ())), preferred_element_type=F32) * (A_HEAD_DIM ** -0.5)
        s = jnp.where(mask, s + bias[g], NEG_INF)
        sk = sink[g:g + 1, :]
        m = jnp.maximum(jnp.max(s, axis=-1, keepdims=True), sk)
        e = jnp.exp(s - m)
        den = jnp.sum(e, axis=-1, keepdims=True) + jnp.exp(sk - m)
        p = e / den
        outs.append(jnp.dot(p.astype(BF16), vb, preferred_element_type=F32))
    return jnp.concatenate(outs, axis=1)


def _attn_in_specs(nblk):
    qw = A_GROUP * A_HEAD_DIM
    kc, vc = OFF_AK // A_HEAD_DIM, OFF_AV // A_HEAD_DIM
    return [
        pl.BlockSpec((BLOCK, qw), lambda h, n: (n, h)),
        pl.BlockSpec((BLOCK, A_HEAD_DIM), lambda h, n: (jnp.maximum(n - 1, 0), kc + h)),
        pl.BlockSpec((BLOCK, A_HEAD_DIM), lambda h, n: (n, kc + h)),
        pl.BlockSpec((BLOCK, A_HEAD_DIM), lambda h, n: (jnp.minimum(n + 1, nblk - 1), kc + h)),
        pl.BlockSpec((BLOCK, A_HEAD_DIM), lambda h, n: (jnp.maximum(n - 1, 0), vc + h)),
        pl.BlockSpec((BLOCK, A_HEAD_DIM), lambda h, n: (n, vc + h)),
        pl.BlockSpec((BLOCK, A_HEAD_DIM), lambda h, n: (jnp.minimum(n + 1, nblk - 1), vc + h)),
        pl.BlockSpec((A_GROUP, BLOCK, 3 * BLOCK), lambda h, n: (h, 0, 0)),
        pl.BlockSpec((None, A_GROUP, 1), lambda h, n: (h, 0, 0)),
    ]


def attn_fwd(proj, bias, sink):
    T = proj.shape[0]
    nblk = T // BLOCK

    def body(q_ref, k0, k1, k2, v0, v1, v2, b_ref, s_ref, o_ref):
        n = pl.program_id(1)
        k3 = jnp.concatenate([k0[...], k1[...], k2[...]], axis=0)
        v3 = jnp.concatenate([v0[...], v1[...], v2[...]], axis=0)
        o = f_attn_block(q_ref[...], k3, v3, b_ref[...], s_ref[...], _attn_mask(n, nblk))
        o_ref[...] = o.astype(o_ref.dtype)

    return pl.pallas_call(
        body, grid=(A_KV_HEADS, nblk), in_specs=_attn_in_specs(nblk),
        out_specs=pl.BlockSpec((BLOCK, A_GROUP * A_HEAD_DIM), lambda h, n: (n, h)),
        out_shape=jax.ShapeDtypeStruct((T, A_Q), BF16),
        compiler_params=_cparams(("arbitrary", "arbitrary")), name="attn_fwd")(
            proj, proj, proj, proj, proj, proj, proj, bias, sink)


def attn_bwd(proj, bias, sink, dcat):
    T = proj.shape[0]
    nblk = T // BLOCK
    qw = A_GROUP * A_HEAD_DIM

    def body(q_ref, k0, k1, k2, v0, v1, v2, b_ref, s_ref, do_ref, dq_ref, dk_ref, dv_ref, db_ref, ds_ref, dk_acc, dv_acc):
        n = pl.program_id(1)

        @pl.when(n == 0)
        def _():
            dk_acc[...] = jnp.zeros(dk_acc.shape, F32)
            dv_acc[...] = jnp.zeros(dv_acc.shape, F32)
            db_ref[...] = jnp.zeros(db_ref.shape, F32)
            ds_ref[...] = jnp.zeros(ds_ref.shape, F32)

        k3 = jnp.concatenate([k0[...], k1[...], k2[...]], axis=0)
        v3 = jnp.concatenate([v0[...], v1[...], v2[...]], axis=0)
        mask = _attn_mask(n, nblk)
        _, vjp = jax.vjp(lambda q, k, v, b, s: f_attn_block(q, k, v, b, s, mask), q_ref[...], k3, v3, b_ref[...], s_ref[...])
        dq, dk3, dv3, db, ds = vjp(do_ref[...])
        dq_ref[...] = dq.astype(dq_ref.dtype)
        db_ref[...] += db
        ds_ref[...] += ds
        mid = pl.multiple_of(n * BLOCK, BLOCK)
        dk_acc[pl.ds(mid, BLOCK), :] += dk3[BLOCK:2 * BLOCK]
        dv_acc[pl.ds(mid, BLOCK), :] += dv3[BLOCK:2 * BLOCK]

        @pl.when(n > 0)
        def _():
            lo = pl.multiple_of((n - 1) * BLOCK, BLOCK)
            dk_acc[pl.ds(lo, BLOCK), :] += dk3[0:BLOCK]
            dv_acc[pl.ds(lo, BLOCK), :] += dv3[0:BLOCK]

        @pl.when(n < nblk - 1)
        def _():
            hi = pl.multiple_of((n + 1) * BLOCK, BLOCK)
            dk_acc[pl.ds(hi, BLOCK), :] += dk3[2 * BLOCK:3 * BLOCK]
            dv_acc[pl.ds(hi, BLOCK), :] += dv3[2 * BLOCK:3 * BLOCK]

        @pl.when(n == nblk - 1)
        def _():
            dk_ref[...] = dk_acc[...].astype(dk_ref.dtype)
            dv_ref[...] = dv_acc[...].astype(dv_ref.dtype)

    in_specs = _attn_in_specs(nblk) + [pl.BlockSpec((BLOCK, qw), lambda h, n: (n, h))]
    out_specs = [
        pl.BlockSpec((BLOCK, qw), lambda h, n: (n, h)),
        pl.BlockSpec((T, A_HEAD_DIM), lambda h, n: (0, h)),
        pl.BlockSpec((T, A_HEAD_DIM), lambda h, n: (0, h)),
        pl.BlockSpec((A_GROUP, BLOCK, 3 * BLOCK), lambda h, n: (h, 0, 0)),
        pl.BlockSpec((None, A_GROUP, 1), lambda h, n: (h, 0, 0)),
    ]
    out_shape = [
        jax.ShapeDtypeStruct((T, A_Q), BF16), jax.ShapeDtypeStruct((T, A_KV), BF16), jax.ShapeDtypeStruct((T, A_KV), BF16),
        jax.ShapeDtypeStruct((A_HEADS, BLOCK, 3 * BLOCK), F32), jax.ShapeDtypeStruct((A_KV_HEADS, A_GROUP, 1), F32),
    ]
    return pl.pallas_call(
        body, grid=(A_KV_HEADS, nblk), in_specs=in_specs, out_specs=out_specs, out_shape=out_shape,
        scratch_shapes=[pltpu.VMEM((T, A_HEAD_DIM), F32), pltpu.VMEM((T, A_HEAD_DIM), F32)],
        compiler_params=_cparams(("arbitrary", "arbitrary")), name="attn_bwd")(
            proj, proj, proj, proj, proj, proj, proj, bias, sink, dcat)


def f_gla_gate(z, w2f, b2f, w2b, b2b):
    laf = jax.nn.log_sigmoid(jnp.dot(z, w2f, precision=HI, preferred_element_type=F32) + b2f) / GATE_TAU
    lab = jax.nn.log_sigmoid(jnp.dot(z, w2b, precision=HI, preferred_element_type=F32) + b2b) / GATE_TAU
    return laf, lab


def f_gla_post(o, g, gn):
    outs = []
    for h in range(B_HEADS):
        sl = slice(h * B_VAL_DIM, (h + 1) * B_VAL_DIM)
        oh = o[:, sl]
        outs.append(oh * lax.rsqrt(jnp.mean(oh * oh, axis=-1, keepdims=True) + EPS))
    return jnp.concatenate(outs, axis=1) * gn * jax.nn.silu(g)


def _gla_consts(forward):
    C = GLA_CHUNK
    i = lax.broadcasted_iota(jnp.int32, (C, C), 0)
    j = lax.broadcasted_iota(jnp.int32, (C, C), 1)
    if forward:
        return (j <= i).astype(F32), j <= i
    return (j >= i).astype(F32), j > i


def _gla_chunk(q, k, v, la, st, tri, msk, forward):
    C = q.shape[0]
    b = jnp.dot(tri, la, precision=HI, preferred_element_type=F32)
    bl = b[C - 1:C] if forward else b[0:1]
    qe = (q * (B_KEY_DIM ** -0.5)) * jnp.exp(b)
    ke = k * jnp.exp(-b)
    kl = k * jnp.exp(bl - b)
    att = lax.dot_general(qe.astype(BF16), ke.astype(BF16), (((1,), (1,)), ((), ())), preferred_element_type=F32)
    att = jnp.where(msk, att, 0.0)
    o = jnp.dot(att.astype(BF16), v.astype(BF16), preferred_element_type=F32)
    o = o + lax.dot_general(qe.astype(BF16), st.astype(BF16), (((1,), (1,)), ((), ())), preferred_element_type=F32)
    st_new = st * jnp.exp(bl) + lax.dot_general(v.astype(BF16), kl.astype(BF16), (((0,), (0,)), ((), ())),
                                                preferred_element_type=F32)
    return o, st_new


def _gla_specs(T):
    qc, kc, vc = OFF_BQ // LANES, OFF_BK // LANES, OFF_BV // (2 * B_VAL_DIM)
    return [
        pl.BlockSpec((T, LANES), lambda p: (0, qc + p)),
        pl.BlockSpec((T, LANES), lambda p: (0, kc + p)),
        pl.BlockSpec((T, 2 * B_VAL_DIM), lambda p: (0, vc + p)),
        pl.BlockSpec((T, LANES), lambda p: (0, p)),
        pl.BlockSpec((T, LANES), lambda p: (0, p)),
    ]


def _rows(c):
    return pl.ds(pl.multiple_of(c * GLA_CHUNK, GLA_CHUNK), GLA_CHUNK)


def gla_fwd(proj, laf, lab):
    T = proj.shape[0]
    nc = T // GLA_CHUNK

    def body(q_ref, k_ref, v_ref, laf_ref, lab_ref, o_ref, ob_scr):
        tri_f, msk_f = _gla_consts(True)
        tri_b, msk_b = _gla_consts(False)
        zero = jnp.zeros((B_VAL_DIM, B_KEY_DIM), F32)

        def step(c, carry):
            rf, rb = _rows(c), _rows(nc - 1 - c)
            new = []
            for hh in range(2):
                ks = slice(hh * B_KEY_DIM, (hh + 1) * B_KEY_DIM)
                vs = slice(hh * B_VAL_DIM, (hh + 1) * B_VAL_DIM)
                o, s = _gla_chunk(q_ref[rf, ks], k_ref[rf, ks], v_ref[rf, vs], laf_ref[rf, ks], carry[2 * hh], tri_f, msk_f, True)
                o_ref[rf, vs] = o
                new.append(s)
                o, s = _gla_chunk(q_ref[rb, ks], k_ref[rb, ks], v_ref[rb, vs], lab_ref[rb, ks], carry[2 * hh + 1], tri_b, msk_b, False)
                ob_scr[rb, vs] = o
                new.append(s)
            return tuple(new)

        lax.fori_loop(0, nc, step, (zero,) * 4)
        o_ref[...] += ob_scr[...]

    return pl.pallas_call(
        body, grid=(B_HEADS // 2,), in_specs=_gla_specs(T),
        out_specs=pl.BlockSpec((T, 2 * B_VAL_DIM), lambda p: (0, p)),
        out_shape=jax.ShapeDtypeStruct((T, B_V), F32),
        scratch_shapes=[pltpu.VMEM((T, 2 * B_VAL_DIM), F32)],
        compiler_params=_cparams(("arbitrary",)), name="gla_fwd")(proj, proj, proj, laf, lab)


def gla_bwd(proj, laf, lab, do):
    T = proj.shape[0]
    nc = T // GLA_CHUNK
    SROWS = 2 * B_VAL_DIM

    def body(q_ref, k_ref, v_ref, laf_ref, lab_ref, do_ref, dq_ref, dk_ref, dv_ref, dlaf_ref, dlab_ref,
             sf_scr, sb_scr, dq_acc, dk_acc, dv_acc):
        tri_f, msk_f = _gla_consts(True)
        tri_b, msk_b = _gla_consts(False)
        zero = jnp.zeros((B_VAL_DIM, B_KEY_DIM), F32)
        dq_acc[...] = jnp.zeros(dq_acc.shape, F32)
        dk_acc[...] = jnp.zeros(dk_acc.shape, F32)
        dv_acc[...] = jnp.zeros(dv_acc.shape, F32)

        def srow(c, hh):
            return pl.ds(pl.multiple_of(c * SROWS + hh * B_VAL_DIM, B_VAL_DIM), B_VAL_DIM)

        def states(c, carry):
            cf, cb = c, nc - 1 - c
            rf, rb = _rows(cf), _rows(cb)
            new = []
            for hh in range(2):
                ks = slice(hh * B_KEY_DIM, (hh + 1) * B_KEY_DIM)
                vs = slice(hh * B_VAL_DIM, (hh + 1) * B_VAL_DIM)
                sf_scr[srow(cf, hh), :] = carry[2 * hh]
                _, s = _gla_chunk(q_ref[rf, ks], k_ref[rf, ks], v_ref[rf, vs], laf_ref[rf, ks], carry[2 * hh], tri_f, msk_f, True)
                new.append(s)
                sb_scr[srow(cb, hh), :] = carry[2 * hh + 1]
                _, s = _gla_chunk(q_ref[rb, ks], k_ref[rb, ks], v_ref[rb, vs], lab_ref[rb, ks], carry[2 * hh + 1], tri_b, msk_b, False)
                new.append(s)
            return tuple(new)

        lax.fori_loop(0, nc, states, (zero,) * 4)

        def back(c, carry):
            cf, cb = nc - 1 - c, c
            rf, rb = _rows(cf), _rows(cb)
            new = []
            for hh in range(2):
                ks = slice(hh * B_KEY_DIM, (hh + 1) * B_KEY_DIM)
                vs = slice(hh * B_VAL_DIM, (hh + 1) * B_VAL_DIM)
                for fwd, r, c_, la_ref, dla_ref, s_scr, g, tri, msk in (
                        (True, rf, cf, laf_ref, dlaf_ref, sf_scr, carry[2 * hh], tri_f, msk_f),
                        (False, rb, cb, lab_ref, dlab_ref, sb_scr, carry[2 * hh + 1], tri_b, msk_b)):
                    _, vjp = jax.vjp(
                        lambda q, k, v, la, st: _gla_chunk(q, k, v, la, st, tri, msk, fwd),
                        q_ref[r, ks], k_ref[r, ks], v_ref[r, vs], la_ref[r, ks], s_scr[srow(c_, hh), :])
                    dq, dk, dv, dla, dst = vjp((do_ref[r, vs], g))
                    dq_acc[r, ks] += dq
                    dk_acc[r, ks] += dk
                    dv_acc[r, vs] += dv
                    dla_ref[r, ks] = dla
                    new.append(dst)
            return tuple(new)

        lax.fori_loop(0, nc, back, (zero,) * 4)
        dq_ref[...] = dq_acc[...].astype(dq_ref.dtype)
        dk_ref[...] = dk_acc[...].astype(dk_ref.dtype)
        dv_ref[...] = dv_acc[...].astype(dv_ref.dtype)

    in_specs = _gla_specs(T) + [pl.BlockSpec((T, 2 * B_VAL_DIM), lambda p: (0, p))]
    out_specs = [
        pl.BlockSpec((T, LANES), lambda p: (0, p)), pl.BlockSpec((T, LANES), lambda p: (0, p)),
        pl.BlockSpec((T, 2 * B_VAL_DIM), lambda p: (0, p)),
        pl.BlockSpec((T, LANES), lambda p: (0, p)), pl.BlockSpec((T, LANES), lambda p: (0, p)),
    ]
    out_shape = [
        jax.ShapeDtypeStruct((T, B_QK), BF16), jax.ShapeDtypeStruct((T, B_QK), BF16), jax.ShapeDtypeStruct((T, B_V), BF16),
        jax.ShapeDtypeStruct((T, B_QK), F32), jax.ShapeDtypeStruct((T, B_QK), F32),
    ]
    scratch = [
        pltpu.VMEM((nc * SROWS, B_KEY_DIM), F32), pltpu.VMEM((nc * SROWS, B_KEY_DIM), F32),
        pltpu.VMEM((T, LANES), F32), pltpu.VMEM((T, LANES), F32), pltpu.VMEM((T, 2 * B_VAL_DIM), F32),
    ]
    return pl.pallas_call(
        body, grid=(B_HEADS // 2,), in_specs=in_specs, out_specs=out_specs, out_shape=out_shape, scratch_shapes=scratch,
        compiler_params=_cparams(("arbitrary",)), name="gla_bwd")(proj, proj, proj, laf, lab, do)


def _shift_raw(x, k):
    T = x.shape[0]
    r = lax.broadcasted_iota(jnp.int32, x.shape, 0)
    if k > 0:
        return jnp.where(r >= k, pltpu.roll(x, k, 0), 0.0)
    return jnp.where(r < T + k, pltpu.roll(x, T + k, 0), 0.0)


@functools.partial(jax.custom_vjp, nondiff_argnums=(1,))
def _shift(x, k):
    return _shift_raw(x, k)


_shift.defvjp(lambda x, k: (_shift_raw(x, k), None), lambda k, _, g: (_shift_raw(g, -k),))


def _scan_raw(a, u, reverse):
    T = a.shape[0]
    d = 1
    while d < T:
        k = -d if reverse else d
        u = a * _shift_raw(u, k) + u
        a = a * _shift_raw(a, k)
        d *= 2
    return u


@functools.partial(jax.custom_vjp, nondiff_argnums=(2,))
def _scan(a, u, reverse):
    return _scan_raw(a, u, reverse)


def _scan_f(a, u, reverse):
    h = _scan_raw(a, u, reverse)
    return h, (a, h)


def _scan_b(reverse, res, dh):
    a, h = res
    k = 1 if reverse else -1
    du = _scan_raw(_shift_raw(a, k), dh, not reverse)
    return du * _shift_raw(h, -k), du


_scan.defvjp(_scan_f, _scan_b)


def f_lru(cx, cy, cw, cb, wa, ba, wx, bx, lam, diff):
    shift, scan = (_shift, _scan) if diff else (_shift_raw, _scan_raw)
    xc = cx * cw[CONV_LEFT:CONV_LEFT + 1]
    for j in range(CONV_WIDTH):
        if j != CONV_LEFT:
            xc = xc + shift(cx, CONV_LEFT - j) * cw[j:j + 1]
    xc = xc + cb
    xb = xc.astype(BF16)
    h = None
    for s in range(2):
        r = jax.nn.sigmoid(jnp.dot(xb, wa[s].astype(BF16), preferred_element_type=F32) + ba[s:s + 1])
        i = jax.nn.sigmoid(jnp.dot(xb, wx[s].astype(BF16), preferred_element_type=F32) + bx[s:s + 1])
        log_a = -LRU_C * r * jax.nn.softplus(-lam[s:s + 1])
        a = jnp.exp(log_a)
        one_minus_a2 = -jnp.tanh(log_a) * (a * a + 1.0)
        u = jnp.sqrt(one_minus_a2) * (i * xc)
        hs = scan(a, u, s == 1)
        h = hs if h is None else h + hs
    return h * jax.nn.gelu(cy)


def _lru_specs(T):
    xc, yc = OFF_CX // LANES, OFF_CY // LANES
    return [
        pl.BlockSpec((T, LANES), lambda b: (0, xc + b)),
        pl.BlockSpec((T, LANES), lambda b: (0, yc + b)),
        pl.BlockSpec((CONV_WIDTH, LANES), lambda b: (0, b)),
        pl.BlockSpec((1, LANES), lambda b: (0, b)),
        pl.BlockSpec((2, None, C_BLOCK_DIM, C_BLOCK_DIM), lambda b: (0, b, 0, 0)),
        pl.BlockSpec((2, LANES), lambda b: (0, b)),
        pl.BlockSpec((2, None, C_BLOCK_DIM, C_BLOCK_DIM), lambda b: (0, b, 0, 0)),
        pl.BlockSpec((2, LANES), lambda b: (0, b)),
        pl.BlockSpec((2, LANES), lambda b: (0, b)),
    ]


def lru_fwd(proj, cw, cb, wa, ba, wx, bx, lam):
    T = proj.shape[0]

    def body(cx, cy, cw_r, cb_r, wa_r, ba_r, wx_r, bx_r, lam_r, o_ref):
        o = f_lru(cx[...], cy[...], cw_r[...], cb_r[...], wa_r[...], ba_r[...], wx_r[...], bx_r[...], lam_r[...], False)
        o_ref[...] = o.astype(o_ref.dtype)

    return pl.pallas_call(
        body, grid=(C_BLOCKS,), in_specs=_lru_specs(T), out_specs=pl.BlockSpec((T, LANES), lambda b: (0, b)),
        out_shape=jax.ShapeDtypeStruct((T, C_WIDTH), BF16),
        compiler_params=_cparams(("arbitrary",)), name="lru_fwd")(proj, proj, cw, cb, wa, ba, wx, bx, lam)


def lru_bwd(proj, cw, cb, wa, ba, wx, bx, lam, dcat):
    T = proj.shape[0]
    oc = (A_Q + B_V) // LANES

    def body(cx, cy, cw_r, cb_r, wa_r, ba_r, wx_r, bx_r, lam_r, do_ref, *outs):
        _, vjp = jax.vjp(functools.partial(f_lru, diff=True), cx[...], cy[...], cw_r[...], cb_r[...], wa_r[...],
                         ba_r[...], wx_r[...], bx_r[...], lam_r[...])
        grads = vjp(do_ref[...])
        for o, g in zip(outs, grads):
            o[...] = g.astype(o.dtype)

    specs = _lru_specs(T)
    out_specs = [pl.BlockSpec((T, LANES), lambda b: (0, b)), pl.BlockSpec((T, LANES), lambda b: (0, b))] + specs[2:]
    out_shape = [jax.ShapeDtypeStruct((T, C_WIDTH), BF16), jax.ShapeDtypeStruct((T, C_WIDTH), BF16)] + \
                [jax.ShapeDtypeStruct(p.shape, F32) for p in (cw, cb, wa, ba, wx, bx, lam)]
    return pl.pallas_call(
        body, grid=(C_BLOCKS,), in_specs=specs + [pl.BlockSpec((T, LANES), lambda b: (0, oc + b))],
        out_specs=out_specs, out_shape=out_shape,
        compiler_params=_cparams(("arbitrary",)), name="lru_bwd")(proj, proj, cw, cb, wa, ba, wx, bx, lam, dcat)


HBM_SPEC = pl.BlockSpec(memory_space=pltpu.HBM)


def _place():
    x, y, c = lax.axis_index("x"), lax.axis_index("y"), lax.axis_index("c")
    others = [(1 - x, y), (x, 1 - y), (1 - x, 1 - y)]
    return x, y, c, 2 * x + y, others


def _remote(src, dst, send_sems, recv_sems, k, to):
    return pltpu.make_async_remote_copy(src_ref=src, dst_ref=dst, send_sem=send_sems.at[k], recv_sem=recv_sems.at[k],
                                        device_id=to, device_id_type=MESH)


def all_gather8(name, blk):
    def body(x_ref, out_ref, send_sems, recv_sems):
        x, y, c, _, others = _place()
        sibling = (x, y, 1 - c)

        def slab(px, py, pc):
            return out_ref.at[4 * px + 2 * py + pc]

        first = [_remote(x_ref, slab(x, y, c), send_sems, recv_sems, 0, sibling)]
        first += [_remote(x_ref, slab(x, y, c), send_sems, recv_sems, 1 + j, (*ch, c)) for j, ch in enumerate(others)]
        for cp in first:
            cp.start()
        passed = [_remote(slab(*ch, c), slab(*ch, c), send_sems, recv_sems, 4 + j, sibling) for j, ch in enumerate(others)]
        for j, ch in enumerate(others):
            _remote(x_ref, slab(*ch, c), send_sems, recv_sems, 1 + j, (x, y, c)).wait_recv()
            passed[j].start()
        _remote(x_ref, slab(x, y, 1 - c), send_sems, recv_sems, 0, (x, y, c)).wait_recv()
        for j, ch in enumerate(others):
            _remote(x_ref, slab(*ch, 1 - c), send_sems, recv_sems, 4 + j, (x, y, c)).wait_recv()
        for cp in first + passed:
            cp.wait_send()

    out = pl.pallas_call(
        body, out_shape=jax.ShapeDtypeStruct((8,) + blk.shape, blk.dtype), in_specs=[HBM_SPEC], out_specs=HBM_SPEC,
        scratch_shapes=[pltpu.SemaphoreType.DMA((7,)), pltpu.SemaphoreType.DMA((7,))], name=name)(blk)
    me = 4 * lax.axis_index("x") + 2 * lax.axis_index("y") + lax.axis_index("c")
    return lax.dynamic_update_index_in_dim(out, blk, me, 0)


def _exchange_call(name, body, arrays, out_shapes, n_sems):
    n = len(arrays)

    def kernel_body(*refs):
        body(refs[:n], refs[n:2 * n], refs[2 * n], refs[2 * n + 1])

    return pl.pallas_call(
        kernel_body, out_shape=out_shapes, in_specs=[HBM_SPEC] * n, out_specs=[HBM_SPEC] * n,
        scratch_shapes=[pltpu.SemaphoreType.DMA((n * n_sems,)), pltpu.SemaphoreType.DMA((n * n_sems,))], name=name)(*arrays)


def _chip_index(ch):
    return 2 * ch[0] + ch[1]


def chip_gather(name, shards):
    def body(ins, outs, send_sems, recv_sems):
        x, y, c, chip, others = _place()
        sibling = (x, y, 1 - c)
        first, passed = [], []
        for a, (x_ref, out_ref) in enumerate(zip(ins, outs)):
            first += [_remote(x_ref.at[c], out_ref.at[chip, c], send_sems, recv_sems, 6 * a + j, (*ch, c))
                      for j, ch in enumerate(others)]
        for cp in first:
            cp.start()
        for a, (x_ref, out_ref) in enumerate(zip(ins, outs)):
            for j, ch in enumerate(others):
                here = out_ref.at[_chip_index(ch), c]
                _remote(x_ref.at[c], here, send_sems, recv_sems, 6 * a + j, (x, y, c)).wait_recv()
                cp = _remote(here, here, send_sems, recv_sems, 6 * a + 3 + j, sibling)
                cp.start()
                passed.append(cp)
        for a, (x_ref, out_ref) in enumerate(zip(ins, outs)):
            for j, ch in enumerate(others):
                _remote(x_ref.at[c], out_ref.at[_chip_index(ch), 1 - c], send_sems, recv_sems, 6 * a + 3 + j, (x, y, c)).wait_recv()
        for cp in first + passed:
            cp.wait_send()

    outs = _exchange_call(name, body, shards, [jax.ShapeDtypeStruct((4,) + s.shape, s.dtype) for s in shards], 6)
    chip = 2 * lax.axis_index("x") + lax.axis_index("y")
    return [lax.dynamic_update_index_in_dim(o, s, chip, 0) for o, s in zip(outs, shards)]


def chip_scatter(name, parts):
    def body(ins, outs, send_sems, recv_sems):
        x, y, c, chip, others = _place()
        sends = []
        for a, (x_ref, out_ref) in enumerate(zip(ins, outs)):
            sends += [_remote(x_ref.at[_chip_index(ch)], out_ref.at[chip], send_sems, recv_sems, 3 * a + j, (*ch, c))
                      for j, ch in enumerate(others)]
        for cp in sends:
            cp.start()
        for a, (x_ref, out_ref) in enumerate(zip(ins, outs)):
            for j, ch in enumerate(others):
                _remote(x_ref.at[chip], out_ref.at[_chip_index(ch)], send_sems, recv_sems, 3 * a + j, (x, y, c)).wait_recv()
        for cp in sends:
            cp.wait_send()

    return _exchange_call(name, body, parts, [jax.ShapeDtypeStruct(p.shape, p.dtype) for p in parts], 3)


def sibling_take(name, halves):
    def body(ins, outs, send_sems, recv_sems):
        x, y, c, _, _ = _place()
        cps = [_remote(x_ref.at[s, 1 - c], out_ref.at[s], send_sems, recv_sems, 4 * a + s, (x, y, 1 - c))
               for a, (x_ref, out_ref) in enumerate(zip(ins, outs)) for s in range(4)]
        for cp in cps:
            cp.start()
        for cp in cps:
            cp.wait()

    return _exchange_call(name, body, halves,
                          [jax.ShapeDtypeStruct((h.shape[0],) + h.shape[2:], h.dtype) for h in halves], 4)


def sibling_pair(name, mine):
    def body(ins, outs, send_sems, recv_sems):
        x, y, c, _, _ = _place()
        cps = [_remote(x_ref, out_ref.at[c], send_sems, recv_sems, a, (x, y, 1 - c))
               for a, (x_ref, out_ref) in enumerate(zip(ins, outs))]
        for cp in cps:
            cp.start()
        for a, (x_ref, out_ref) in enumerate(zip(ins, outs)):
            _remote(x_ref, out_ref.at[1 - c], send_sems, recv_sems, a, (x, y, c)).wait_recv()
        for cp in cps:
            cp.wait_send()

    outs = _exchange_call(name, body, mine, [jax.ShapeDtypeStruct((2,) + m.shape, m.dtype) for m in mine], 1)
    core = lax.axis_index("c")
    return [lax.dynamic_update_index_in_dim(o, m, core, 0) for o, m in zip(outs, mine)]


def sum_slabs(name, r, out_dtype, tr):
    S, R, W = r.shape

    def body(*refs):
        t = refs[0][...].astype(F32)
        for s in range(1, S):
            t = t + refs[s][...].astype(F32)
        refs[S][...] = t.astype(out_dtype)

    return pl.pallas_call(
        body, grid=(R // tr,), in_specs=[pl.BlockSpec((None, tr, W), lambda i, s=s: (s, i, 0)) for s in range(S)],
        out_specs=pl.BlockSpec((tr, W), lambda i: (i, 0)), out_shape=jax.ShapeDtypeStruct((R, W), out_dtype),
        compiler_params=_cparams(("parallel",)), name=name)(*([r] * S))


def sum_chips(name, arrived, own, chip, tr):
    S, R, W = arrived.shape

    def body(chip_ref, own_ref, *refs):
        me = chip_ref[0]
        t = None
        for s in range(S):
            term = jnp.where(me == s, own_ref[...].astype(F32), refs[s][...].astype(F32))
            t = term if t is None else t + term
        refs[S][...] = t

    grid_spec = pltpu.PrefetchScalarGridSpec(
        num_scalar_prefetch=1, grid=(R // tr,),
        in_specs=[pl.BlockSpec((None, tr, W), lambda i, ch: (ch[0], i, 0))] +
                 [pl.BlockSpec((None, tr, W), lambda i, ch, s=s: (s, i, 0)) for s in range(S)],
        out_specs=pl.BlockSpec((tr, W), lambda i, ch: (i, 0)))
    return pl.pallas_call(body, grid_spec=grid_spec, out_shape=jax.ShapeDtypeStruct((R, W), F32),
                          compiler_params=_cparams(("parallel",)), name=name)(
                              chip.reshape(1).astype(jnp.int32), own, *([arrived] * S))


def add_kept_half(name, halves, got, c, tr):
    S, _, R, W = halves.shape

    def body(c_ref, h_ref, g_ref, o_ref):
        o_ref[...] = (h_ref[...].astype(F32) + g_ref[...].astype(F32)).astype(o_ref.dtype)

    grid_spec = pltpu.PrefetchScalarGridSpec(
        num_scalar_prefetch=1, grid=(S, R // tr),
        in_specs=[pl.BlockSpec((None, None, tr, W), lambda s, i, c_ref: (s, c_ref[0], i, 0)),
                  pl.BlockSpec((None, tr, W), lambda s, i, c_ref: (s, i, 0))],
        out_specs=pl.BlockSpec((None, tr, W), lambda s, i, c_ref: (s, i, 0)))
    return pl.pallas_call(body, grid_spec=grid_spec, out_shape=jax.ShapeDtypeStruct((S, R, W), halves.dtype),
                          compiler_params=_cparams(("parallel", "parallel")), name=name)(
                              c.reshape(1).astype(jnp.int32), halves, got)


def adamw_layer(name, l, g, row_off, w, m, v, prev, tr):
    L, R, C = w.shape
    off = row_off // tr

    def body(g_ref, w_ref, m_ref, v_ref, *rest):
        outs = rest[-4:]
        gv = g_ref[...]
        d, mn, vn = f_adamw(w_ref[...], gv, m_ref[...], v_ref[...])
        for o, val in zip(outs, (gv, d, mn, vn)):
            o[...] = val

    slab = pl.BlockSpec((None, tr, C), lambda i: (l, i, 0))
    in_specs = [pl.BlockSpec((tr, C), lambda i: (off + i, 0)), slab, slab, slab]
    args = [g, w, m, v]
    aliases = {}
    if prev is not None:
        in_specs += [pl.BlockSpec(memory_space=pl.ANY)] * 4
        args += list(prev)
        aliases = {4 + k: k for k in range(4)}
    return pl.pallas_call(
        body, grid=(R // tr,), in_specs=in_specs, out_specs=[slab] * 4,
        out_shape=[jax.ShapeDtypeStruct((L, R, C), F32)] * 4, input_output_aliases=aliases,
        compiler_params=_cparams(("parallel",)), name=name)(*args)


BIG = ("w_in", "w_out", "xq", "xk", "xv", "xo", "w_up", "w_down")
PACKED = BIG[1:]
PACK_ROWS = {"w_out": 512, "xq": 512, "xk": 512, "xv": 512, "xo": 512, "w_up": 2048, "w_down": 2048}
PACK_TOTAL = sum(PACK_ROWS.values())
PACK_HALF = PACK_TOTAL // 2
SUM_TILE = 256
PACK_OFF = {}
_o = 0
for _n in PACKED:
    PACK_OFF[_n] = _o
    _o += PACK_ROWS[_n]

_SPLIT_OFF = np.cumsum((0,) + SPLIT_SIZES)
_KORDER = (0, 1, 2, 3, 4, 5, 6, 9, 10, 7, 8)


def w_in_to_kernel_cols(w):
    parts = [w[..., _SPLIT_OFF[i]:_SPLIT_OFF[i + 1]] for i in _KORDER]
    parts.append(jnp.zeros(w.shape[:-1] + (D_INP - D_IN,), w.dtype))
    return jnp.concatenate(parts, axis=-1)


def w_in_from_kernel_cols(w):
    offs = np.cumsum((0,) + tuple(SPLIT_SIZES[i] for i in _KORDER))
    pos = {k: (offs[n], offs[n + 1]) for n, k in enumerate(_KORDER)}
    return jnp.concatenate([w[..., pos[i][0]:pos[i][1]] for i in range(len(SPLIT_SIZES))], axis=-1)


def pack_shards(shards, dtype):
    return jnp.concatenate([shards[n].astype(dtype) for n in PACKED], axis=-2)


def unpack_rows(packed, name):
    return packed[..., PACK_OFF[name]:PACK_OFF[name] + PACK_ROWS[name], :]


WEIGHTS = ("rel_bias", "w_in", "w_out", "attn_sink", "gla_w2_f", "gla_b2_f", "gla_w2_b", "gla_b2_b", "gla_norm", "conv_w",
           "conv_b", "lru_wa", "lru_ba", "lru_wx", "lru_bx", "lru_lambda", "xq", "xk", "xv", "xo", "w_up", "w_down",
           "norm_mix_pre", "norm_mix_post", "norm_mem", "norm_x_pre", "norm_x_post", "norm_ff_pre", "norm_ff_post")
SMALL = tuple(n for n in WEIGHTS if n not in BIG)
SMALL_SHARDED = ("gla_w2_f", "gla_w2_b", "conv_w", "lru_ba", "lru_bx", "lru_lambda")
ROW_TILE = 256
SMALL_TILE = 512


def _pack_small(arrs):
    flat = jnp.concatenate([a.reshape(-1).astype(F32) for a in arrs])
    n = flat.shape[0]
    rows = -(-n // (SMALL_TILE * LANES)) * SMALL_TILE
    return jnp.pad(flat, (0, rows * LANES - n)).reshape(rows, LANES)


def _unpack_small(buf, shapes):
    lead = buf.shape[:-2]
    flat = buf.reshape(lead + (-1,))
    out, o = [], 0
    for s in shapes:
        n = int(np.prod(s))
        out.append(flat[..., o:o + n].reshape(lead + tuple(s)))
        o += n
    return out


def _relu2(r):
    return r, jnp.square(jnp.maximum(r, 0.0))


def _drelu2(r, u):
    return r * (2.0 * jnp.maximum(u, 0.0))


def kernel(x, mem, rel_bias, w_in, w_out, attn_sink, gla_w2_f, gla_b2_f, gla_w2_b, gla_b2_b, gla_norm, conv_w, conv_b, lru_wa, lru_ba, lru_wx, lru_bx, lru_lambda, xq, xk, xv, xo, w_up, w_down, norm_mix_pre, norm_mix_post, norm_mem, norm_x_pre, norm_x_post, norm_ff_pre, norm_ff_post, loss_target, m_rel_bias, m_w_in, m_w_out, m_attn_sink, m_gla_w2_f, m_gla_b2_f, m_gla_w2_b, m_gla_b2_b, m_gla_norm, m_conv_w, m_conv_b, m_lru_wa, m_lru_ba, m_lru_wx, m_lru_bx, m_lru_lambda, m_xq, m_xk, m_xv, m_xo, m_w_up, m_w_down, m_norm_mix_pre, m_norm_mix_post, m_norm_mem, m_norm_x_pre, m_norm_x_post, m_norm_ff_pre, m_norm_ff_post, v_rel_bias, v_w_in, v_w_out, v_attn_sink, v_gla_w2_f, v_gla_b2_f, v_gla_w2_b, v_gla_b2_b, v_gla_norm, v_conv_w, v_conv_b, v_lru_wa, v_lru_ba, v_lru_wx, v_lru_bx, v_lru_lambda, v_xq, v_xk, v_xv, v_xo, v_w_up, v_w_down, v_norm_mix_pre, v_norm_mix_post, v_norm_mem, v_norm_x_pre, v_norm_x_post, v_norm_ff_pre, v_norm_ff_post):
    w_args = (rel_bias, w_in, w_out, attn_sink, gla_w2_f, gla_b2_f, gla_w2_b, gla_b2_b, gla_norm, conv_w, conv_b, lru_wa,
              lru_ba, lru_wx, lru_bx, lru_lambda, xq, xk, xv, xo, w_up, w_down, norm_mix_pre, norm_mix_post, norm_mem,
              norm_x_pre, norm_x_post, norm_ff_pre, norm_ff_post)
    m_args = (m_rel_bias, m_w_in, m_w_out, m_attn_sink, m_gla_w2_f, m_gla_b2_f, m_gla_w2_b, m_gla_b2_b, m_gla_norm, m_conv_w,
              m_conv_b, m_lru_wa, m_lru_ba, m_lru_wx, m_lru_bx, m_lru_lambda, m_xq, m_xk, m_xv, m_xo, m_w_up, m_w_down,
              m_norm_mix_pre, m_norm_mix_post, m_norm_mem, m_norm_x_pre, m_norm_x_post, m_norm_ff_pre, m_norm_ff_post)
    v_args = (v_rel_bias, v_w_in, v_w_out, v_attn_sink, v_gla_w2_f, v_gla_b2_f, v_gla_w2_b, v_gla_b2_b, v_gla_norm, v_conv_w,
              v_conv_b, v_lru_wa, v_lru_ba, v_lru_wx, v_lru_bx, v_lru_lambda, v_xq, v_xk, v_xv, v_xo, v_w_up, v_w_down,
              v_norm_mix_pre, v_norm_mix_post, v_norm_mem, v_norm_x_pre, v_norm_x_post, v_norm_ff_pre, v_norm_ff_post)
    Wt, Mo, Vo = dict(zip(WEIGHTS, w_args)), dict(zip(WEIGHTS, m_args)), dict(zip(WEIGHTS, v_args))
    x, mem, tgt = x[0], mem[0], loss_target[0]
    D = D_MODEL
    depth = w_in.shape[0]
    chip = 2 * lax.axis_index("x") + lax.axis_index("y")
    core = lax.axis_index("c")

    sm_shapes = [Wt[n].shape for n in SMALL_SHARDED]
    g8 = all_gather8("gather_small_weights", _pack_small([Wt[n] for n in SMALL_SHARDED]))
    per_chip = _unpack_small(g8[0::2], sm_shapes)
    whole = {n: jnp.concatenate([p[j] for j in range(4)], axis=-1) for n, p in zip(SMALL_SHARDED, per_chip)}

    Wfull = []
    for l in range(depth):
        shard = pack_shards({n: Wt[n][l] for n in PACKED}, BF16).reshape(2, PACK_HALF, D)
        shard_in = w_in[l].astype(BF16).reshape(2, D // 2, D_IN // 4)
        g, g_in = chip_gather("gather_layer_weights", [shard, shard_in])
        g = g.reshape(4, PACK_TOTAL, D)
        W = {}
        win = g_in.reshape(4, D, D_IN // 4).transpose(1, 0, 2).reshape(D, D_IN)
        W["w_in"] = w_in_to_kernel_cols(win)
        for n in ("w_out", "xq", "xk", "xv", "xo"):
            W[n] = unpack_rows(g, n).reshape(D, D)
        W["w_up"] = unpack_rows(g, "w_up")
        W["w_down"] = unpack_rows(g, "w_down").reshape(D_FF, D)
        Wfull.append(W)

    bucket = t5_bucket_map()
    bias = bias_table_fwd(rel_bias, bucket)

    def gain(name, l):
        return Wt[name][l][None]

    def layer_params(l):
        w2fp = jnp.zeros((LANES, B_QK), F32).at[0:GATE_RANK].set(whole["gla_w2_f"][l])
        w2bp = jnp.zeros((LANES, B_QK), F32).at[GATE_RANK:2 * GATE_RANK].set(whole["gla_w2_b"][l])
        gate = [w2fp, gla_b2_f[l][None], w2bp, gla_b2_b[l][None]]
        lru = [whole["conv_w"][l], conv_b[l][None], lru_wa[l], whole["lru_ba"][l], lru_wx[l], whole["lru_bx"][l],
               whole["lru_lambda"][l]]
        return attn_sink[l].reshape(A_KV_HEADS, A_GROUP, 1), gate, gla_norm[l][None], lru

    saved = []
    xcur = x
    (h1,) = rowmap("norm_first", f_norm, [x], [gain("norm_mix_pre", 0)], [(D, BF16)], ROW_TILE)
    loss_acc = None
    for l in range(depth):
        W = Wfull[l]
        sink3, gate, gn, lru = layer_params(l)
        (proj,) = mm("mm_in", h1, W["w_in"], "nn", [(F32, 1)], pn=1408)
        oa = attn_fwd(proj, bias, sink3)
        zrow, grow = (proj, LANES, OFF_Z // LANES), (proj, B_V, OFF_BG // B_V)
        laf, lab = rowmap("gla_gate", f_gla_gate, [zrow], gate, [(B_QK, F32), (B_QK, F32)], ROW_TILE)
        oraw = gla_fwd(proj, laf, lab)
        (ob,) = rowmap("gla_post", f_gla_post, [oraw, grow], [gn], [(B_V, BF16)], ROW_TILE)
        oc = lru_fwd(proj, *lru)
        cat = jnp.concatenate([oa, ob, oc], axis=1)
        (mixed,) = mm("mm_out", cat, W["w_out"], "nn", [(F32, 1)])
        x1, h2 = rowmap("resnorm_mix", f_resnorm, [xcur, mixed], [gain("norm_mix_post", l), gain("norm_x_pre", l)],
                        [(D, F32), (D, BF16)], ROW_TILE)
        (memn,) = rowmap("norm_mem", f_norm, [mem], [gain("norm_mem", l)], [(D, BF16)], ROW_TILE)
        (q,) = mm("mm_xq", h2, W["xq"], "nn", [(BF16, 1)])
        (k,) = mm("mm_xk", memn, W["xk"], "nn", [(F32, 1)])
        (v,) = mm("mm_xv", memn, W["xv"], "nn", [(F32, 1)])
        (o,) = rowmap("xattn", f_xattn, [q], [k, v], [(D, BF16)], ROW_TILE)
        (xo_out,) = mm("mm_xo", o, W["xo"], "nn", [(F32, 1)])
        x2, h3 = rowmap("resnorm_x", f_resnorm, [x1, xo_out], [gain("norm_x_post", l), gain("norm_ff_pre", l)],
                        [(D, F32), (D, BF16)], ROW_TILE)
        u, act = mm("mm_up", h3, W["w_up"], "nn", [(F32, 1), (BF16, 1)], epilogue=_relu2)
        (ff,) = mm("mm_down", act, W["w_down"], "nn", [(F32, 1)])
        saved.append(dict(x0=xcur, h1=h1, proj=proj, laf=laf, lab=lab, oraw=oraw, cat=cat, mixed=mixed, x1=x1, h2=h2,
                          memn=memn, q=q, k=k, v=v, o=o, xo_out=xo_out, x2=x2, h3=h3, u=u, act=act, ff=ff))
        if l < depth - 1:
            xcur, h1 = rowmap("resnorm_ff", f_resnorm, [x2, ff], [gain("norm_ff_post", l), gain("norm_mix_pre", l + 1)],
                              [(D, F32), (D, BF16)], ROW_TILE)
        else:
            (loss_acc,) = rowmap("final_loss", f_final_loss, [x2, ff, tgt], [gain("norm_ff_post", l)], [], ROW_TILE,
                                 accs=[(1, 1)])
    loss = lax.psum(loss_acc[0, 0], ("x", "y", "c"))

    small_g = {n: [None] * depth for n in SMALL if n != "rel_bias"}
    adam = {}
    dbias_all = []
    dx_next = dh1_next = None
    grad_x = None
    for l in reversed(range(depth)):
        W, S = Wfull[l], saved[l]
        sink3, gate, gn, lru = layer_params(l)
        if l == depth - 1:
            (dx2, dff), (dgp,) = rowmap_bwd("final_bwd", f_final_rows, [S["x2"], S["ff"], tgt], [gain("norm_ff_post", l)],
                                            [None], ROW_TILE, [F32, F32, None], [True])
        else:
            (dx2, dff), (dgp, dgn_next) = rowmap_bwd(
                "resnorm_ff_bwd", f_resnorm, [S["x2"], S["ff"]], [gain("norm_ff_post", l), gain("norm_mix_pre", l + 1)],
                [dx_next, dh1_next], ROW_TILE, [F32, F32], [True, True])
            small_g["norm_mix_pre"][l + 1] = dgn_next[0]
        small_g["norm_ff_post"][l] = dgp[0]
        dW = {}
        (du,) = mm("mm_down_bwd", dff, W["w_down"], "nt", [(BF16, 1)], epilogue=_drelu2, extras=[S["u"]])
        (dW["w_down"],) = mm("mm_down_wgrad", S["act"], dff, "tn", [(BF16, 1)])
        (dW["w_up"],) = mm("mm_up_wgrad", S["h3"], du, "tn", [(BF16, 4)])
        (dh3,) = mm("mm_up_bwd", du, W["w_up"], "nt", [(F32, 1)])
        (dx1, dxo_out), (dg1, dg2) = rowmap_bwd(
            "resnorm_x_bwd", f_resnorm, [S["x1"], S["xo_out"]], [gain("norm_x_post", l), gain("norm_ff_pre", l)],
            [dx2, dh3], ROW_TILE, [F32, F32], [True, True])
        small_g["norm_x_post"][l], small_g["norm_ff_pre"][l] = dg1[0], dg2[0]
        (do,) = mm("mm_xo_bwd", dxo_out, W["xo"], "nt", [(F32, 1)])
        (dW["xo"],) = mm("mm_xo_wgrad", S["o"], dxo_out, "tn", [(BF16, 1)])
        (dq,), (dk, dv) = rowmap_bwd("xattn_bwd", f_xattn, [S["q"]], [S["k"], S["v"]], [do], ROW_TILE, [BF16], [True, True])
        (dW["xq"],) = mm("mm_xq_wgrad", S["h2"], dq, "tn", [(BF16, 1)])
        (dh2,) = mm("mm_xq_bwd", dq, W["xq"], "nt", [(F32, 1)])
        (dW["xk"],) = mm("mm_xk_wgrad", S["memn"], dk, "tn", [(BF16, 1)])
        (dW["xv"],) = mm("mm_xv_wgrad", S["memn"], dv, "tn", [(BF16, 1)])
        (dmk,) = mm("mm_xk_bwd", dk, W["xk"], "nt", [(F32, 1)])
        (dmv,) = mm("mm_xv_bwd", dv, W["xv"], "nt", [(F32, 1)])
        _, (dgm,) = rowmap_bwd("norm_mem_bwd", f_norm_twice, [mem], [gain("norm_mem", l)], [dmk, dmv], ROW_TILE, [None], [True])
        small_g["norm_mem"][l] = dgm[0]
        (dx0, dmixed), (dg1, dg2) = rowmap_bwd(
            "resnorm_mix_bwd", f_resnorm, [S["x0"], S["mixed"]], [gain("norm_mix_post", l), gain("norm_x_pre", l)],
            [dx1, dh2], ROW_TILE, [F32, F32], [True, True])
        small_g["norm_mix_post"][l], small_g["norm_x_pre"][l] = dg1[0], dg2[0]
        (dcat,) = mm("mm_out_bwd", dmixed, W["w_out"], "nt", [(F32, 1)])
        (dW["w_out"],) = mm("mm_out_wgrad", S["cat"], dmixed, "tn", [(BF16, 1)])
        proj = S["proj"]
        daq, dak, dav, dbias, dsink = attn_bwd(proj, bias, sink3, dcat)
        dbias_all.append(dbias)
        small_g["attn_sink"][l] = dsink.reshape(A_HEADS)
        zrow, grow = (proj, LANES, OFF_Z // LANES), (proj, B_V, OFF_BG // B_V)
        (doraw, dbg), (dgn,) = rowmap_bwd("gla_post_bwd", f_gla_post, [S["oraw"], grow], [gn], [(dcat, B_V, A_Q // B_V)],
                                          ROW_TILE, [F32, BF16], [True])
        dbq, dbk, dbv, dlaf, dlab = gla_bwd(proj, S["laf"], S["lab"], doraw)
        (dz,), (dw2fp, db2f, dw2bp, db2b) = rowmap_bwd("gla_gate_bwd", f_gla_gate, [zrow], gate, [dlaf, dlab], ROW_TILE,
                                                        [BF16], [True] * 4)
        small_g["gla_norm"][l] = dgn[0]
        small_g["gla_w2_f"][l], small_g["gla_b2_f"][l] = dw2fp[0:GATE_RANK], db2f[0]
        small_g["gla_w2_b"][l], small_g["gla_b2_b"][l] = dw2bp[GATE_RANK:2 * GATE_RANK], db2b[0]
        dcx, dcy, dcw, dcb, dwa, dba, dwx, dbx, dlam = lru_bwd(proj, *lru, dcat)
        small_g["conv_w"][l], small_g["conv_b"][l] = dcw, dcb[0]
        small_g["lru_wa"][l], small_g["lru_ba"][l], small_g["lru_wx"][l] = dwa, dba, dwx
        small_g["lru_bx"][l], small_g["lru_lambda"][l] = dbx, dlam
        dproj = jnp.concatenate([daq, dak, dav, dbq, dbk, dbv, dbg, dcx, dcy, dz], axis=1)
        (dW["w_in"],) = mm("mm_in_wgrad", S["h1"], dproj, "tn", [(BF16, 1)], pn=1408)
        (dh1,) = mm("mm_in_bwd", dproj, W["w_in"], "nt", [(F32, 1)])
        if l > 0:
            dx_next, dh1_next = dx0, dh1
        else:
            (grad_x,), (dg0,) = rowmap_bwd("norm_first_bwd", f_norm_keep, [x], [gain("norm_mix_pre", 0)], [dx0, dh1],
                                           ROW_TILE, [F32], [True])
            small_g["norm_mix_pre"][0] = dg0[0]

        shards = {n: dW[n].reshape(4, D // 4, D) for n in ("w_out", "xq", "xk", "xv", "xo")}
        shards["w_up"] = dW["w_up"]
        shards["w_down"] = dW["w_down"].reshape(4, D_FF // 4, D)
        halves = pack_shards(shards, BF16).reshape(4, 2, PACK_HALF, D)
        halves_in = w_in_from_kernel_cols(dW["w_in"]).reshape(D, 4, D_IN // 4).transpose(1, 0, 2)
        halves_in = halves_in.reshape(4, 2, D // 2, D_IN // 4)
        got, got_in = sibling_take("reduce_to_half_owner", [halves, halves_in])
        chip_sum = add_kept_half("reduce_chip_sum", halves, got, core, SUM_TILE)
        chip_sum_in = add_kept_half("reduce_chip_sum_in", halves_in, got_in, core, SUM_TILE)
        arrived, arrived_in = chip_scatter("reduce_to_shard_owner", [chip_sum, chip_sum_in])
        total = sum_chips("reduce_sum_chips", arrived, chip_sum, chip, SUM_TILE)
        total_in = sum_chips("reduce_sum_chips_in", arrived_in, chip_sum_in, chip, SUM_TILE)
        full, full_in = sibling_pair("reduce_share_halves", [total, total_in])
        full, full_in = full.reshape(PACK_TOTAL, D), full_in.reshape(D, D_IN // 4)
        for n in PACKED:
            adam[n] = adamw_layer("adamw_" + n, l, full, PACK_OFF[n], Wt[n], Mo[n], Vo[n], adam.get(n), SUM_TILE)
        adam["w_in"] = adamw_layer("adamw_w_in", l, full_in, 0, w_in, m_w_in, v_w_in, adam.get("w_in"), SUM_TILE)

    (dtab,) = (bias_table_bwd(dbias_all, bucket),)
    sg = {n: jnp.stack(small_g[n]) for n in small_g}
    sg["rel_bias"] = dtab[:, :A_HEADS]
    sg_shapes = [sg[n].shape for n in SMALL]
    contributions = all_gather8("gather_small_grads", _pack_small([sg[n] for n in SMALL]))
    (sg_sum,) = (sum_slabs("sum_small_grads", contributions, F32, SMALL_TILE),)
    sg = dict(zip(SMALL, _unpack_small(sg_sum, sg_shapes)))
    for n in SMALL_SHARDED:
        w = Wt[n].shape[-1]
        sg[n] = lax.dynamic_slice_in_dim(sg[n], chip * w, w, axis=sg[n].ndim - 1)

    grads, delta, new_m, new_v = {}, {}, {}, {}
    for n in BIG:
        grads[n], delta[n], new_m[n], new_v[n] = adam[n]
    shapes = [Wt[n].shape for n in SMALL]
    packs = [_pack_small([src[n] for n in SMALL]) for src in (Wt, sg, Mo, Vo)]
    d_, m_, v_ = rowmap("adamw_small", f_adamw, packs, [], [(LANES, F32)] * 3, SMALL_TILE)
    for n, a, b, c_ in zip(SMALL, _unpack_small(d_, shapes), _unpack_small(m_, shapes), _unpack_small(v_, shapes)):
        grads[n], delta[n], new_m[n], new_v[n] = sg[n], a, b, c_

    return (loss, grad_x[None], *[grads[n] for n in WEIGHTS], *[delta[n] for n in WEIGHTS],
            *[new_m[n] for n in WEIGHTS], *[new_v[n] for n in WEIGHTS])
```

```python
import functools
import math

import numpy as np
import jax
import jax.numpy as jnp
from jax import lax
from jax.experimental import pallas as pl
from jax.experimental.pallas import tpu as pltpu

F32, BF16 = jnp.float32, jnp.bfloat16
HI = lax.Precision.HIGHEST
MESH = pl.DeviceIdType.MESH

VMEM_LIMIT_BYTES = 56 * 1024 * 1024
LANES = 128
SUBLANES = 8

D_MODEL = 2048
DEPTH = 4
A_HEAD_DIM = 128
A_HEADS = 8
A_KV_HEADS = 2
A_GROUP = 4
WINDOW = 128
BLOCK = 128
N_BUCKETS = 32
MAX_DISTANCE = 128
B_HEADS = 4
B_KEY_DIM = 64
B_VAL_DIM = 128
GATE_RANK = 16
GATE_TAU = 16.0
C_WIDTH = 512
C_BLOCKS = 4
C_BLOCK_DIM = 128
CONV_WIDTH = 4
CONV_LEFT = 2
LRU_C = 8.0
X_HEADS = 4
X_HEAD_DIM = 512
D_FF = 4 * D_MODEL
EPS = 1e-6
NEG_INF = -1e30
A_Q, A_KV, B_QK, B_V = 1024, 256, 256, 512
SPLIT_SIZES = (A_Q, A_KV, A_KV, B_QK, B_QK, B_V, B_V, GATE_RANK, GATE_RANK, C_WIDTH, C_WIDTH)
D_IN = sum(SPLIT_SIZES)
D_INP = 4224
OFF_AQ, OFF_AK, OFF_AV, OFF_BQ, OFF_BK, OFF_BV, OFF_BG, OFF_CX, OFF_CY, OFF_Z = (
    0, 1024, 1280, 1536, 1792, 2048, 2560, 3072, 3584, 4096)
GLA_CHUNK = 128

ADAM_LR, ADAM_B1, ADAM_B2, ADAM_EPS, ADAM_WD, ADAM_STEP = 0.001, 0.9, 0.999, 1e-08, 0.01, 10


def _cparams(sem=None):
    return pltpu.CompilerParams(dimension_semantics=sem, vmem_limit_bytes=VMEM_LIMIT_BYTES)


def _full_spec(a):
    nd = a.ndim
    return pl.BlockSpec(a.shape, lambda *_: (0,) * nd)


def _tup(r):
    return r if isinstance(r, tuple) else (r,)


HBM_SPEC = pl.BlockSpec(memory_space=pltpu.HBM)


def _place():
    x, y, c = lax.axis_index("x"), lax.axis_index("y"), lax.axis_index("c")
    others = [(1 - x, y), (x, 1 - y), (1 - x, 1 - y)]
    return x, y, c, 2 * x + y, others


def _remote(src, dst, send_sems, recv_sems, k, to):
    return pltpu.make_async_remote_copy(src_ref=src, dst_ref=dst, send_sem=send_sems.at[k], recv_sem=recv_sems.at[k],
                                        device_id=to, device_id_type=MESH)


def _chip_index(ch):
    return 2 * ch[0] + ch[1]


def _pcall(body, args, side=None, **kw):
    if side is None:
        res = pl.pallas_call(body, **kw)(*args)
        return list(res) if isinstance(res, (list, tuple)) else [res]
    single = not isinstance(kw["out_shape"], (list, tuple))
    out_shape = [kw.pop("out_shape")] if single else list(kw.pop("out_shape"))
    out_specs = [kw.pop("out_specs")] if single else list(kw.pop("out_specs"))
    in_specs = list(kw.pop("in_specs"))
    scratch = list(kw.pop("scratch_shapes", ()))
    grid = kw.get("grid", ())
    n_in, n_out, n_scr = len(in_specs), len(out_shape), len(scratch)
    srcs, bufs = side.srcs, side.bufs
    ns, nb = len(srcs), len(bufs)

    def wrapped(*refs):
        ins = refs[:n_in]
        src_refs = refs[n_in:n_in + ns]
        o0 = n_in + ns + nb
        outs = refs[o0:o0 + n_out]
        buf_refs = refs[o0 + n_out:o0 + n_out + nb]
        s0 = o0 + n_out + nb
        scr = refs[s0:s0 + n_scr]
        send_sems, recv_sems = refs[s0 + n_scr], refs[s0 + n_scr + 1]
        first = last = None
        for d, n in enumerate(grid):
            f, l_ = pl.program_id(d) == 0, pl.program_id(d) == n - 1
            first = f if first is None else first & f
            last = l_ if last is None else last & l_
        if first is None:
            side.start(src_refs, buf_refs, send_sems, recv_sems)
            body(*ins, *outs, *scr)
            side.finish(src_refs, buf_refs, send_sems, recv_sems)
            return
        pl.when(first)(lambda: side.start(src_refs, buf_refs, send_sems, recv_sems))
        body(*ins, *outs, *scr)
        pl.when(last)(lambda: side.finish(src_refs, buf_refs, send_sems, recv_sems))

    any_spec = pl.BlockSpec(memory_space=pl.ANY)
    cp = kw.pop("compiler_params", None)
    if grid:
        cp = _cparams(("arbitrary",) * len(grid))
    res = pl.pallas_call(
        wrapped, in_specs=in_specs + [any_spec] * (ns + nb), out_specs=out_specs + [any_spec] * nb,
        out_shape=out_shape + [jax.ShapeDtypeStruct(b.shape, b.dtype) for b in bufs],
        scratch_shapes=scratch + [pltpu.SemaphoreType.DMA((side.n_sems,)), pltpu.SemaphoreType.DMA((side.n_sems,))],
        input_output_aliases={n_in + ns + i: n_out + i for i in range(nb)}, compiler_params=cp, **kw)(*args, *srcs, *bufs)
    side.done(list(res[n_out:]))
    return list(res[:n_out])


class _Side:
    def __init__(self, parent, ranges, forwards):
        self.parent, self.ranges, self.forwards = parent, ranges, forwards
        self.srcs, self.bufs = parent.srcs, parent.bufs
        self.n_sems = 3 * max(1, len(ranges) + len(forwards))

    def done(self, bufs):
        self.parent.bufs = bufs

    def _copies(self, src_refs, buf_refs, send_sems, recv_sems):
        x, y, c, chip, others = _place()
        gather = self.parent.kind == "gather"
        mine, landing = [], []
        k = 0
        for a, r0, n in self.ranges:
            rows = pl.ds(r0, n)
            for ch in others:
                if gather:
                    src, dst = src_refs[a].at[c, rows], buf_refs[a].at[chip, c, rows]
                    got = buf_refs[a].at[_chip_index(ch), c, rows]
                else:
                    src, dst = src_refs[a].at[_chip_index(ch), rows], buf_refs[a].at[chip, rows]
                    got = buf_refs[a].at[_chip_index(ch), rows]
                mine.append(_remote(src, dst, send_sems, recv_sems, k, (*ch, c)))
                landing.append(_remote(src, got, send_sems, recv_sems, k, (x, y, c)))
                k += 1
        for a, r0, n in self.forwards:
            rows = pl.ds(r0, n)
            for ch in others:
                here = buf_refs[a].at[_chip_index(ch), c, rows]
                mine.append(_remote(here, here, send_sems, recv_sems, k, (x, y, 1 - c)))
                landing.append(_remote(here, buf_refs[a].at[_chip_index(ch), 1 - c, rows], send_sems, recv_sems, k, (x, y, c)))
                k += 1
        return mine, landing

    def start(self, src_refs, buf_refs, send_sems, recv_sems):
        for cp in self._copies(src_refs, buf_refs, send_sems, recv_sems)[0]:
            cp.start()

    def finish(self, src_refs, buf_refs, send_sems, recv_sems):
        mine, landing = self._copies(src_refs, buf_refs, send_sems, recv_sems)
        for cp in landing:
            cp.wait_recv()
        for cp in mine:
            cp.wait_send()


class RidingExchange:
    def __init__(self, kind, srcs, piece_rows):
        self.kind, self.srcs = kind, list(srcs)
        lead = (4, 2) if kind == "gather" else (4,)
        self.bufs = [lax.empty(lead + s.shape[1:], s.dtype) for s in srcs]
        heights = [s.shape[1] for s in srcs]
        per = [[(a, r0, min(pr, h - r0)) for r0 in range(0, h, pr)] for a, (h, pr) in enumerate(zip(heights, piece_rows))]
        self.pieces = list(per[0])
        for extra in per[1:]:
            step = max(1, len(self.pieces) // (len(extra) + 1))
            for i, p in enumerate(extra):
                self.pieces.insert(min(len(self.pieces), (i + 1) * step + i), p)
        self.pending_forward = []

    def _merge(self, pieces):
        out = []
        for a, r0, n in pieces:
            if out and out[-1][0] == a and out[-1][1] + out[-1][2] == r0:
                out[-1] = (a, out[-1][1], out[-1][2] + n)
            else:
                out.append((a, r0, n))
        return out

    def take(self, n):
        now, self.pieces = self._merge(self.pieces[:n]), self.pieces[n:]
        fwd, self.pending_forward = self.pending_forward, (now if self.kind == "gather" else [])
        if not now and not fwd:
            return None
        return _Side(self, now, fwd)

    def flush(self, name):
        while True:
            side = self.take(len(self.pieces))
            if side is None:
                return self.bufs
            _pcall(lambda: None, [], side, in_specs=[], out_specs=[], out_shape=[], name=name)


def _row_ops(rows, tr):
    arrs, specs, widths = [], [], []
    for r in rows:
        arr, n, j = r if isinstance(r, tuple) else (r, r.shape[1], 0)
        arrs.append(arr)
        widths.append(n)
        specs.append(pl.BlockSpec((tr, n), lambda i, j=j: (i, j)))
    return arrs, specs, widths


def rowmap(name, f, rows, params, outs, tr, accs=()):
    rows, row_specs, _ = _row_ops(rows, tr)
    T = rows[0].shape[0]
    nin, nout, nacc = len(rows) + len(params), len(outs), len(accs)

    def body(*refs):
        res = _tup(f(*[r[...] for r in refs[:nin]]))
        for o, r in zip(refs[nin:nin + nout], res[:nout]):
            o[...] = r.astype(o.dtype)
        arefs = refs[nin + nout:]
        if nacc:
            @pl.when(pl.program_id(0) == 0)
            def _():
                for a in arefs:
                    a[...] = jnp.zeros(a.shape, a.dtype)
            for a, r in zip(arefs, res[nout:]):
                a[...] += r.astype(F32)

    in_specs = row_specs + [_full_spec(p) for p in params]
    out_specs = [pl.BlockSpec((tr, n), lambda i: (i, 0)) for n, _ in outs] + \
                [pl.BlockSpec(s, lambda i, nd=len(s): (0,) * nd) for s in accs]
    out_shape = [jax.ShapeDtypeStruct((T, n), d) for n, d in outs] + [jax.ShapeDtypeStruct(s, F32) for s in accs]
    res = pl.pallas_call(body, grid=(T // tr,), in_specs=in_specs, out_specs=out_specs, out_shape=out_shape,
                         compiler_params=_cparams(("arbitrary",)), name=name)(*rows, *params)
    return tuple(res)


def rowmap_bwd(name, f, rows, params, cots, tr, drow_dtypes, want_params):
    rows, row_specs, widths = _row_ops(rows, tr)
    T = rows[0].shape[0]
    nr, npar = len(rows), len(params)
    cot_arrays, cot_specs, _ = _row_ops([c for c in cots if c is not None], tr)
    nc = len(cot_arrays)
    ridx = [i for i, d in enumerate(drow_dtypes) if d is not None]
    pidx = [i for i, w in enumerate(want_params) if w]

    def body(*refs):
        rvals = [r[...] for r in refs[:nr]]
        pvals = [r[...] for r in refs[nr:nr + npar]]
        crefs = list(refs[nr + npar:nr + npar + nc])
        orefs = refs[nr + npar + nc:]
        outs, vjp = jax.vjp(f, *rvals, *pvals)
        outs = _tup(outs)
        cts = []
        for c, o in zip(cots, outs):
            cts.append(jnp.ones(o.shape, o.dtype) if c is None else crefs.pop(0)[...].astype(o.dtype))
        grads = vjp(tuple(cts) if len(cts) > 1 else cts[0])
        for o, i in zip(orefs[:len(ridx)], ridx):
            o[...] = grads[i].astype(o.dtype)
        prefs = orefs[len(ridx):]
        if prefs:
            @pl.when(pl.program_id(0) == 0)
            def _():
                for a in prefs:
                    a[...] = jnp.zeros(a.shape, a.dtype)
            for a, i in zip(prefs, pidx):
                a[...] += grads[nr + i].astype(F32)

    in_specs = row_specs + [_full_spec(p) for p in params] + cot_specs
    out_specs = [pl.BlockSpec((tr, widths[i]), lambda i: (i, 0)) for i in ridx] + [_full_spec(params[i]) for i in pidx]
    out_shape = [jax.ShapeDtypeStruct((T, widths[i]), drow_dtypes[i]) for i in ridx] + \
                [jax.ShapeDtypeStruct(params[i].shape, F32) for i in pidx]
    res = pl.pallas_call(body, grid=(T // tr,), in_specs=in_specs, out_specs=out_specs, out_shape=out_shape,
                         compiler_params=_cparams(("arbitrary",)), name=name)(*rows, *params, *cot_arrays)
    res = tuple(res)
    return res[:len(ridx)], res[len(ridx):]


def _pick(n, pref):
    best = None
    for d in range(LANES, min(n, pref) + 1, LANES):
        if n % d == 0:
            best = d
    return best if best is not None else n


def _spec2(arr, tile, pos):
    tr, tc = tile
    if arr.ndim == 2:
        return pl.BlockSpec((tr, tc), lambda i, j, k: pos(i, j, k))
    assert arr.shape[2] % tc == 0, (arr.shape, tile)
    per = arr.shape[2] // tc

    def imap(i, j, k):
        r, c = pos(i, j, k)
        return (c // per, r, c % per)
    return pl.BlockSpec((None, tr, tc), imap)


def _dims2(arr):
    return (arr.shape[0], arr.shape[1]) if arr.ndim == 2 else (arr.shape[1], arr.shape[0] * arr.shape[2])


def mm(name, a, b, mode, outs, epilogue=None, extras=(), pm=1024, pn=512, pk=4224, side=None):
    ar, ac = _dims2(a)
    br, bc = _dims2(b)
    if mode == "nn":
        M, K, N = ar, ac, bc
    elif mode == "nt":
        M, K, N = ar, ac, br
    else:
        M, K, N = ac, ar, bc
    tm, tn, tk = _pick(M, pm), _pick(N, pn), _pick(K, pk)
    for arr in (a, b) + tuple(extras):
        if arr.ndim == 3:
            assert arr.shape[2] % LANES == 0
    if mode == "nn":
        a_spec = _spec2(a, (tm, tk), lambda i, j, k: (i, k))
        b_spec = _spec2(b, (tk, tn), lambda i, j, k: (k, j))
        dims = (((1,), (0,)), ((), ()))
    elif mode == "nt":
        a_spec = _spec2(a, (tm, tk), lambda i, j, k: (i, k))
        b_spec = _spec2(b, (tn, tk), lambda i, j, k: (j, k))
        dims = (((1,), (1,)), ((), ()))
    else:
        a_spec = _spec2(a, (tk, tm), lambda i, j, k: (k, i))
        b_spec = _spec2(b, (tk, tn), lambda i, j, k: (k, j))
        dims = (((0,), (0,)), ((), ()))
    nk = K // tk
    nex = len(extras)

    def body(*refs):
        a_ref, b_ref = refs[0], refs[1]
        ex_refs = refs[2:2 + nex]
        o_refs = refs[2 + nex:2 + nex + len(outs)]
        acc = refs[-1]
        k = pl.program_id(2)
        part = lax.dot_general(a_ref[...].astype(BF16), b_ref[...].astype(BF16), dims, preferred_element_type=F32)

        def finish(r):
            res = (r,) if epilogue is None else _tup(epilogue(r, *[e[...] for e in ex_refs]))
            for o, v in zip(o_refs, res):
                o[...] = v.astype(o.dtype)

        if nk == 1:
            finish(part)
            return

        @pl.when(k == 0)
        def _():
            acc[...] = part

        @pl.when(k > 0)
        def _():
            acc[...] += part

        @pl.when(k == nk - 1)
        def _():
            finish(acc[...])

    out_shape, out_specs = [], []
    for dt, chunks in outs:
        if chunks == 1:
            shp = (M, N)
        else:
            shp = (chunks, M, N // chunks)
        o = jax.ShapeDtypeStruct(shp, dt)
        out_shape.append(o)
        out_specs.append(_spec2(o, (tm, tn), lambda i, j, k: (i, j)))
    ex_specs = [_spec2(e, (tm, tn), lambda i, j, k: (i, j)) for e in extras]
    res = _pcall(
        body, [a, b, *extras], side, grid=(M // tm, N // tn, nk), in_specs=[a_spec, b_spec] + ex_specs,
        out_specs=out_specs, out_shape=out_shape, scratch_shapes=[pltpu.VMEM((tm, tn), F32)] if nk > 1 else [],
        compiler_params=_cparams(("parallel", "parallel", "arbitrary")), name=name)
    return tuple(res)


def _rms(x, g):
    return x * lax.rsqrt(jnp.mean(x * x, axis=-1, keepdims=True) + EPS) * g


def f_norm(x, g):
    return _rms(x, g)


def f_norm_keep(x, g):
    return x, _rms(x, g)


def f_resnorm(xp, m, gpost, gnext):
    xn = xp + _rms(m, gpost)
    return xn, _rms(xn, gnext)


def f_final_rows(xp, m, tgt, gpost):
    xn = xp + _rms(m, gpost)
    return 0.5 * jnp.mean(jnp.square(xn - tgt), axis=-1, keepdims=True)


def f_final_loss(xp, m, tgt, gpost):
    return jnp.sum(f_final_rows(xp, m, tgt, gpost), axis=0, keepdims=True)


def f_norm_twice(x, g):
    y = _rms(x, g)
    return y, y


def f_xattn(q, k, v):
    outs = []
    for h in range(X_HEADS):
        sl = slice(h * X_HEAD_DIM, (h + 1) * X_HEAD_DIM)
        s = lax.dot_general(q[:, sl].astype(BF16), k[:, sl].astype(BF16), (((1,), (1,)), ((), ())),
                            preferred_element_type=F32) * (X_HEAD_DIM ** -0.5)
        m = jnp.max(s, axis=-1, keepdims=True)
        e = jnp.exp(s - m)
        p = e / jnp.sum(e, axis=-1, keepdims=True)
        outs.append(jnp.dot(p.astype(BF16), v[:, sl].astype(BF16), preferred_element_type=F32))
    return jnp.concatenate(outs, axis=1)


def f_adamw(w, g, m, v):
    m = ADAM_B1 * m + (1.0 - ADAM_B1) * g
    v = ADAM_B2 * v + (1.0 - ADAM_B2) * jnp.square(g)
    m_hat = m / (1.0 - ADAM_B1 ** ADAM_STEP)
    v_hat = v / (1.0 - ADAM_B2 ** ADAM_STEP)
    delta = -ADAM_LR * (m_hat / (jnp.sqrt(v_hat) + ADAM_EPS) + ADAM_WD * w)
    return delta, m, v


def f_sum8(*xs):
    t = xs[0]
    for x in xs[1:]:
        t = t + x
    return t


def t5_bucket_map():
    qi = jnp.arange(BLOCK)[:, None]
    kj = jnp.arange(3 * BLOCK)[None, :]
    rel = kj - BLOCK - qi
    nb = N_BUCKETS // 2
    max_exact = nb // 2
    ret = jnp.where(rel > 0, nb, 0)
    n = jnp.abs(rel)
    nf = jnp.maximum(n, 1).astype(jnp.float32)
    large = max_exact + (jnp.log(nf / max_exact) / math.log(MAX_DISTANCE / max_exact) * (nb - max_exact)).astype(jnp.int32)
    large = jnp.minimum(large, nb - 1)
    return (ret + jnp.where(n < max_exact, n, large)).astype(jnp.int32)


def bias_table_fwd(table, bucket):
    def body(t_ref, b_ref, o_ref):
        bk = b_ref[...]
        for h in range(A_HEADS):
            acc = jnp.zeros(bk.shape, F32)
            for b in range(N_BUCKETS):
                acc = jnp.where(bk == b, t_ref[b, h], acc)
            o_ref[h] = acc
    return pl.pallas_call(
        body, in_specs=[pl.BlockSpec(memory_space=pltpu.SMEM), pl.BlockSpec(memory_space=pltpu.VMEM)],
        out_specs=pl.BlockSpec(memory_space=pltpu.VMEM),
        out_shape=jax.ShapeDtypeStruct((A_HEADS, BLOCK, 3 * BLOCK), F32), name="bias_table_fwd")(table, bucket)


def bias_table_bwd(dbias_list, bucket):
    n = len(dbias_list)

    def body(*refs):
        b_ref, o_ref = refs[n], refs[n + 1]
        bk = b_ref[...]
        row = lax.broadcasted_iota(jnp.int32, (N_BUCKETS, LANES), 0)
        col = lax.broadcasted_iota(jnp.int32, (N_BUCKETS, LANES), 1)
        out = jnp.zeros((N_BUCKETS, LANES), F32)
        for h in range(A_HEADS):
            d = refs[0][h]
            for r in refs[1:n]:
                d = d + r[h]
            for b in range(N_BUCKETS):
                s = jnp.sum(jnp.where(bk == b, d, 0.0), keepdims=True)
                out = out + jnp.where((row == b) & (col == h), s, 0.0)
        o_ref[...] = out
    return pl.pallas_call(
        body, out_shape=jax.ShapeDtypeStruct((N_BUCKETS, LANES), F32), name="bias_table_bwd",
        compiler_params=_cparams())(*dbias_list, bucket)


def _attn_mask(n, nblk):
    i = lax.broadcasted_iota(jnp.int32, (BLOCK, 3 * BLOCK), 0)
    j = lax.broadcasted_iota(jnp.int32, (BLOCK, 3 * BLOCK), 1)
    kpos = n * BLOCK + j - BLOCK
    return (jnp.abs(j - BLOCK - i) <= WINDOW) & (kpos >= 0) & (kpos < nblk * BLOCK)


def f_attn_block(q, k3, v3, bias, sink, mask):
    kb, vb = k3.astype(BF16), v3.astype(BF16)
    outs = []
    for g in range(A_GROUP):
        qg = q[:, g * A_HEAD_DIM:(g + 1) * A_HEAD_DIM].astype(BF16)
        s = lax.dot_general(qg, kb, (((1,), (1,)), ((), ())), preferred_element_type=F32) * (A_HEAD_DIM ** -0.5)
        s = jnp.where(mask, s + bias[g], NEG_INF)
        sk = sink[g:g + 1, :]
        m = jnp.maximum(jnp.max(s, axis=-1, keepdims=True), sk)
        e = jnp.exp(s - m)
        den = jnp.sum(e, axis=-1, keepdims=True) + jnp.exp(sk - m)
        p = e / den
        outs.append(jnp.dot(p.astype(BF16), vb, preferred_element_type=F32))
    return jnp.concatenate(outs, axis=1)


def _attn_in_specs(nblk):
    qw = A_GROUP * A_HEAD_DIM
    kc, vc = OFF_AK // A_HEAD_DIM, OFF_AV // A_HEAD_DIM
    return [
        pl.BlockSpec((BLOCK, qw), lambda h, n: (n, h)),
        pl.BlockSpec((BLOCK, A_HEAD_DIM), lambda h, n: (jnp.maximum(n - 1, 0), kc + h)),
        pl.BlockSpec((BLOCK, A_HEAD_DIM), lambda h, n: (n, kc + h)),
        pl.BlockSpec((BLOCK, A_HEAD_DIM), lambda h, n: (jnp.minimum(n + 1, nblk - 1), kc + h)),
        pl.BlockSpec((BLOCK, A_HEAD_DIM), lambda h, n: (jnp.maximum(n - 1, 0), vc + h)),
        pl.BlockSpec((BLOCK, A_HEAD_DIM), lambda h, n: (n, vc + h)),
        pl.BlockSpec((BLOCK, A_HEAD_DIM), lambda h, n: (jnp.minimum(n + 1, nblk - 1), vc + h)),
        pl.BlockSpec((A_GROUP, BLOCK, 3 * BLOCK), lambda h, n: (h, 0, 0)),
        pl.BlockSpec((None, A_GROUP, 1), lambda h, n: (h, 0, 0)),
    ]


def attn_fwd(proj, bias, sink, side=None):
    T = proj.shape[0]
    nblk = T // BLOCK

    def body(q_ref, k0, k1, k2, v0, v1, v2, b_ref, s_ref, o_ref):
        n = pl.program_id(1)
        k3 = jnp.concatenate([k0[...], k1[...], k2[...]], axis=0)
        v3 = jnp.concatenate([v0[...], v1[...], v2[...]], axis=0)
        o = f_attn_block(q_ref[...], k3, v3, b_ref[...], s_ref[...], _attn_mask(n, nblk))
        o_ref[...] = o.astype(o_ref.dtype)

    return _pcall(
        body, [proj] * 7 + [bias, sink], side, grid=(A_KV_HEADS, nblk), in_specs=_attn_in_specs(nblk),
        out_specs=pl.BlockSpec((BLOCK, A_GROUP * A_HEAD_DIM), lambda h, n: (n, h)),
        out_shape=jax.ShapeDtypeStruct((T, A_Q), BF16),
        compiler_params=_cparams(("arbitrary", "arbitrary")), name="attn_fwd")[0]


def attn_bwd(proj, bias, sink, dcat, side=None):
    T = proj.shape[0]
    nblk = T // BLOCK
    qw = A_GROUP * A_HEAD_DIM

    def body(q_ref, k0, k1, k2, v0, v1, v2, b_ref, s_ref, do_ref, dq_ref, dk_ref, dv_ref, db_ref, ds_ref, dk_acc, dv_acc):
        n = pl.program_id(1)

        @pl.when(n == 0)
        def _():
            dk_acc[...] = jnp.zeros(dk_acc.shape, F32)
            dv_acc[...] = jnp.zeros(dv_acc.shape, F32)
            db_ref[...] = jnp.zeros(db_ref.shape, F32)
            ds_ref[...] = jnp.zeros(ds_ref.shape, F32)

        k3 = jnp.concatenate([k0[...], k1[...], k2[...]], axis=0)
        v3 = jnp.concatenate([v0[...], v1[...], v2[...]], axis=0)
        mask = _attn_mask(n, nblk)
        _, vjp = jax.vjp(lambda q, k, v, b, s: f_attn_block(q, k, v, b, s, mask), q_ref[...], k3, v3, b_ref[...], s_ref[...])
        dq, dk3, dv3, db, ds = vjp(do_ref[...])
        dq_ref[...] = dq.astype(dq_ref.dtype)
        db_ref[...] += db
        ds_ref[...] += ds
        mid = pl.multiple_of(n * BLOCK, BLOCK)
        dk_acc[pl.ds(mid, BLOCK), :] += dk3[BLOCK:2 * BLOCK]
        dv_acc[pl.ds(mid, BLOCK), :] += dv3[BLOCK:2 * BLOCK]

        @pl.when(n > 0)
        def _():
            lo = pl.multiple_of((n - 1) * BLOCK, BLOCK)
            dk_acc[pl.ds(lo, BLOCK), :] += dk3[0:BLOCK]
            dv_acc[pl.ds(lo, BLOCK), :] += dv3[0:BLOCK]

        @pl.when(n < nblk - 1)
        def _():
            hi = pl.multiple_of((n + 1) * BLOCK, BLOCK)
            dk_acc[pl.ds(hi, BLOCK), :] += dk3[2 * BLOCK:3 * BLOCK]
            dv_acc[pl.ds(hi, BLOCK), :] += dv3[2 * BLOCK:3 * BLOCK]

        @pl.when(n == nblk - 1)
        def _():
            dk_ref[...] = dk_acc[...].astype(dk_ref.dtype)
            dv_ref[...] = dv_acc[...].astype(dv_ref.dtype)

    in_specs = _attn_in_specs(nblk) + [pl.BlockSpec((BLOCK, qw), lambda h, n: (n, h))]
    out_specs = [
        pl.BlockSpec((BLOCK, qw), lambda h, n: (n, h)),
        pl.BlockSpec((T, A_HEAD_DIM), lambda h, n: (0, h)),
        pl.BlockSpec((T, A_HEAD_DIM), lambda h, n: (0, h)),
        pl.BlockSpec((A_GROUP, BLOCK, 3 * BLOCK), lambda h, n: (h, 0, 0)),
        pl.BlockSpec((None, A_GROUP, 1), lambda h, n: (h, 0, 0)),
    ]
    out_shape = [
        jax.ShapeDtypeStruct((T, A_Q), BF16), jax.ShapeDtypeStruct((T, A_KV), BF16), jax.ShapeDtypeStruct((T, A_KV), BF16),
        jax.ShapeDtypeStruct((A_HEADS, BLOCK, 3 * BLOCK), F32), jax.ShapeDtypeStruct((A_KV_HEADS, A_GROUP, 1), F32),
    ]
    return _pcall(
        body, [proj] * 7 + [bias, sink, dcat], side, grid=(A_KV_HEADS, nblk), in_specs=in_specs, out_specs=out_specs,
        out_shape=out_shape, scratch_shapes=[pltpu.VMEM((T, A_HEAD_DIM), F32), pltpu.VMEM((T, A_HEAD_DIM), F32)],
        compiler_params=_cparams(("arbitrary", "arbitrary")), name="attn_bwd")


def f_gla_gate(z, w2f, b2f, w2b, b2b):
    laf = jax.nn.log_sigmoid(jnp.dot(z, w2f, precision=HI, preferred_element_type=F32) + b2f) / GATE_TAU
    lab = jax.nn.log_sigmoid(jnp.dot(z, w2b, precision=HI, preferred_element_type=F32) + b2b) / GATE_TAU
    return laf, lab


def f_gla_post(o, g, gn):
    outs = []
    for h in range(B_HEADS):
        sl = slice(h * B_VAL_DIM, (h + 1) * B_VAL_DIM)
        oh = o[:, sl]
        outs.append(oh * lax.rsqrt(jnp.mean(oh * oh, axis=-1, keepdims=True) + EPS))
    return jnp.concatenate(outs, axis=1) * gn * jax.nn.silu(g)


def _gla_consts(forward):
    C = GLA_CHUNK
    i = lax.broadcasted_iota(jnp.int32, (C, C), 0)
    j = lax.broadcasted_iota(jnp.int32, (C, C), 1)
    if forward:
        return (j <= i).astype(F32), j <= i
    return (j >= i).astype(F32), j > i


def _gla_chunk(q, k, v, la, st, tri, msk, forward):
    C = q.shape[0]
    b = jnp.dot(tri, la, precision=HI, preferred_element_type=F32)
    bl = b[C - 1:C] if forward else b[0:1]
    qe = (q * (B_KEY_DIM ** -0.5)) * jnp.exp(b)
    ke = k * jnp.exp(-b)
    kl = k * jnp.exp(bl - b)
    att = lax.dot_general(qe.astype(BF16), ke.astype(BF16), (((1,), (1,)), ((), ())), preferred_element_type=F32)
    att = jnp.where(msk, att, 0.0)
    o = jnp.dot(att.astype(BF16), v.astype(BF16), preferred_element_type=F32)
    o = o + lax.dot_general(qe.astype(BF16), st.astype(BF16), (((1,), (1,)), ((), ())), preferred_element_type=F32)
    st_new = st * jnp.exp(bl) + lax.dot_general(v.astype(BF16), kl.astype(BF16), (((0,), (0,)), ((), ())),
                                                preferred_element_type=F32)
    return o, st_new


def _gla_specs(T):
    qc, kc, vc = OFF_BQ // LANES, OFF_BK // LANES, OFF_BV // (2 * B_VAL_DIM)
    return [
        pl.BlockSpec((T, LANES), lambda p: (0, qc + p)),
        pl.BlockSpec((T, LANES), lambda p: (0, kc + p)),
        pl.BlockSpec((T, 2 * B_VAL_DIM), lambda p: (0, vc + p)),
        pl.BlockSpec((T, LANES), lambda p: (0, p)),
        pl.BlockSpec((T, LANES), lambda p: (0, p)),
    ]


def _rows(c):
    return pl.ds(pl.multiple_of(c * GLA_CHUNK, GLA_CHUNK), GLA_CHUNK)


def gla_fwd(proj, laf, lab, side=None):
    T = proj.shape[0]
    nc = T // GLA_CHUNK

    def body(q_ref, k_ref, v_ref, laf_ref, lab_ref, o_ref, ob_scr):
        tri_f, msk_f = _gla_consts(True)
        tri_b, msk_b = _gla_consts(False)
        zero = jnp.zeros((B_VAL_DIM, B_KEY_DIM), F32)

        def step(c, carry):
            rf, rb = _rows(c), _rows(nc - 1 - c)
            new = []
            for hh in range(2):
                ks = slice(hh * B_KEY_DIM, (hh + 1) * B_KEY_DIM)
                vs = slice(hh * B_VAL_DIM, (hh + 1) * B_VAL_DIM)
                o, s = _gla_chunk(q_ref[rf, ks], k_ref[rf, ks], v_ref[rf, vs], laf_ref[rf, ks], carry[2 * hh], tri_f, msk_f, True)
                o_ref[rf, vs] = o
                new.append(s)
                o, s = _gla_chunk(q_ref[rb, ks], k_ref[rb, ks], v_ref[rb, vs], lab_ref[rb, ks], carry[2 * hh + 1], tri_b, msk_b, False)
                ob_scr[rb, vs] = o
                new.append(s)
            return tuple(new)

        lax.fori_loop(0, nc, step, (zero,) * 4)
        o_ref[...] += ob_scr[...]

    return _pcall(
        body, [proj, proj, proj, laf, lab], side, grid=(B_HEADS // 2,), in_specs=_gla_specs(T),
        out_specs=pl.BlockSpec((T, 2 * B_VAL_DIM), lambda p: (0, p)),
        out_shape=jax.ShapeDtypeStruct((T, B_V), F32),
        scratch_shapes=[pltpu.VMEM((T, 2 * B_VAL_DIM), F32)],
        compiler_params=_cparams(("arbitrary",)), name="gla_fwd")[0]


def gla_bwd(proj, laf, lab, do, side=None):
    T = proj.shape[0]
    nc = T // GLA_CHUNK
    SROWS = 2 * B_VAL_DIM

    def body(q_ref, k_ref, v_ref, laf_ref, lab_ref, do_ref, dq_ref, dk_ref, dv_ref, dlaf_ref, dlab_ref,
             sf_scr, sb_scr, dq_acc, dk_acc, dv_acc):
        tri_f, msk_f = _gla_consts(True)
        tri_b, msk_b = _gla_consts(False)
        zero = jnp.zeros((B_VAL_DIM, B_KEY_DIM), F32)
        dq_acc[...] = jnp.zeros(dq_acc.shape, F32)
        dk_acc[...] = jnp.zeros(dk_acc.shape, F32)
        dv_acc[...] = jnp.zeros(dv_acc.shape, F32)

        def srow(c, hh):
            return pl.ds(pl.multiple_of(c * SROWS + hh * B_VAL_DIM, B_VAL_DIM), B_VAL_DIM)

        def states(c, carry):
            cf, cb = c, nc - 1 - c
            rf, rb = _rows(cf), _rows(cb)
            new = []
            for hh in range(2):
                ks = slice(hh * B_KEY_DIM, (hh + 1) * B_KEY_DIM)
                vs = slice(hh * B_VAL_DIM, (hh + 1) * B_VAL_DIM)
                sf_scr[srow(cf, hh), :] = carry[2 * hh]
                _, s = _gla_chunk(q_ref[rf, ks], k_ref[rf, ks], v_ref[rf, vs], laf_ref[rf, ks], carry[2 * hh], tri_f, msk_f, True)
                new.append(s)
                sb_scr[srow(cb, hh), :] = carry[2 * hh + 1]
                _, s = _gla_chunk(q_ref[rb, ks], k_ref[rb, ks], v_ref[rb, vs], lab_ref[rb, ks], carry[2 * hh + 1], tri_b, msk_b, False)
                new.append(s)
            return tuple(new)

        lax.fori_loop(0, nc, states, (zero,) * 4)

        def back(c, carry):
            cf, cb = nc - 1 - c, c
            rf, rb = _rows(cf), _rows(cb)
            new = []
            for hh in range(2):
                ks = slice(hh * B_KEY_DIM, (hh + 1) * B_KEY_DIM)
                vs = slice(hh * B_VAL_DIM, (hh + 1) * B_VAL_DIM)
                for fwd, r, c_, la_ref, dla_ref, s_scr, g, tri, msk in (
                        (True, rf, cf, laf_ref, dlaf_ref, sf_scr, carry[2 * hh], tri_f, msk_f),
                        (False, rb, cb, lab_ref, dlab_ref, sb_scr, carry[2 * hh + 1], tri_b, msk_b)):
                    _, vjp = jax.vjp(
                        lambda q, k, v, la, st: _gla_chunk(q, k, v, la, st, tri, msk, fwd),
                        q_ref[r, ks], k_ref[r, ks], v_ref[r, vs], la_ref[r, ks], s_scr[srow(c_, hh), :])
                    dq, dk, dv, dla, dst = vjp((do_ref[r, vs], g))
                    dq_acc[r, ks] += dq
                    dk_acc[r, ks] += dk
                    dv_acc[r, vs] += dv
                    dla_ref[r, ks] = dla
                    new.append(dst)
            return tuple(new)

        lax.fori_loop(0, nc, back, (zero,) * 4)
        dq_ref[...] = dq_acc[...].astype(dq_ref.dtype)
        dk_ref[...] = dk_acc[...].astype(dk_ref.dtype)
        dv_ref[...] = dv_acc[...].astype(dv_ref.dtype)

    in_specs = _gla_specs(T) + [pl.BlockSpec((T, 2 * B_VAL_DIM), lambda p: (0, p))]
    out_specs = [
        pl.BlockSpec((T, LANES), lambda p: (0, p)), pl.BlockSpec((T, LANES), lambda p: (0, p)),
        pl.BlockSpec((T, 2 * B_VAL_DIM), lambda p: (0, p)),
        pl.BlockSpec((T, LANES), lambda p: (0, p)), pl.BlockSpec((T, LANES), lambda p: (0, p)),
    ]
    out_shape = [
        jax.ShapeDtypeStruct((T, B_QK), BF16), jax.ShapeDtypeStruct((T, B_QK), BF16), jax.ShapeDtypeStruct((T, B_V), BF16),
        jax.ShapeDtypeStruct((T, B_QK), F32), jax.ShapeDtypeStruct((T, B_QK), F32),
    ]
    scratch = [
        pltpu.VMEM((nc * SROWS, B_KEY_DIM), F32), pltpu.VMEM((nc * SROWS, B_KEY_DIM), F32),
        pltpu.VMEM((T, LANES), F32), pltpu.VMEM((T, LANES), F32), pltpu.VMEM((T, 2 * B_VAL_DIM), F32),
    ]
    return _pcall(
        body, [proj, proj, proj, laf, lab, do], side, grid=(B_HEADS // 2,), in_specs=in_specs, out_specs=out_specs,
        out_shape=out_shape, scratch_shapes=scratch, compiler_params=_cparams(("arbitrary",)), name="gla_bwd")


def _shift_raw(x, k):
    T = x.shape[0]
    r = lax.broadcasted_iota(jnp.int32, x.shape, 0)
    if k > 0:
        return jnp.where(r >= k, pltpu.roll(x, k, 0), 0.0)
    return jnp.where(r < T + k, pltpu.roll(x, T + k, 0), 0.0)


@functools.partial(jax.custom_vjp, nondiff_argnums=(1,))
def _shift(x, k):
    return _shift_raw(x, k)


_shift.defvjp(lambda x, k: (_shift_raw(x, k), None), lambda k, _, g: (_shift_raw(g, -k),))


def _scan_raw(a, u, reverse):
    T = a.shape[0]
    d = 1
    while d < T:
        k = -d if reverse else d
        u = a * _shift_raw(u, k) + u
        a = a * _shift_raw(a, k)
        d *= 2
    return u


@functools.partial(jax.custom_vjp, nondiff_argnums=(2,))
def _scan(a, u, reverse):
    return _scan_raw(a, u, reverse)


def _scan_f(a, u, reverse):
    h = _scan_raw(a, u, reverse)
    return h, (a, h)


def _scan_b(reverse, res, dh):
    a, h = res
    k = 1 if reverse else -1
    du = _scan_raw(_shift_raw(a, k), dh, not reverse)
    return du * _shift_raw(h, -k), du


_scan.defvjp(_scan_f, _scan_b)


def f_lru(cx, cy, cw, cb, wa, ba, wx, bx, lam, diff):
    shift, scan = (_shift, _scan) if diff else (_shift_raw, _scan_raw)
    xc = cx * cw[CONV_LEFT:CONV_LEFT + 1]
    for j in range(CONV_WIDTH):
        if j != CONV_LEFT:
            xc = xc + shift(cx, CONV_LEFT - j) * cw[j:j + 1]
    xc = xc + cb
    xb = xc.astype(BF16)
    h = None
    for s in range(2):
        r = jax.nn.sigmoid(jnp.dot(xb, wa[s].astype(BF16), preferred_element_type=F32) + ba[s:s + 1])
        i = jax.nn.sigmoid(jnp.dot(xb, wx[s].astype(BF16), preferred_element_type=F32) + bx[s:s + 1])
        log_a = -LRU_C * r * jax.nn.softplus(-lam[s:s + 1])
        a = jnp.exp(log_a)
        one_minus_a2 = -jnp.tanh(log_a) * (a * a + 1.0)
        u = jnp.sqrt(one_minus_a2) * (i * xc)
        hs = scan(a, u, s == 1)
        h = hs if h is None else h + hs
    return h * jax.nn.gelu(cy)


def _lru_specs(T):
    xc, yc = OFF_CX // LANES, OFF_CY // LANES
    return [
        pl.BlockSpec((T, LANES), lambda b: (0, xc + b)),
        pl.BlockSpec((T, LANES), lambda b: (0, yc + b)),
        pl.BlockSpec((CONV_WIDTH, LANES), lambda b: (0, b)),
        pl.BlockSpec((1, LANES), lambda b: (0, b)),
        pl.BlockSpec((2, None, C_BLOCK_DIM, C_BLOCK_DIM), lambda b: (0, b, 0, 0)),
        pl.BlockSpec((2, LANES), lambda b: (0, b)),
        pl.BlockSpec((2, None, C_BLOCK_DIM, C_BLOCK_DIM), lambda b: (0, b, 0, 0)),
        pl.BlockSpec((2, LANES), lambda b: (0, b)),
        pl.BlockSpec((2, LANES), lambda b: (0, b)),
    ]


def lru_fwd(proj, cw, cb, wa, ba, wx, bx, lam, side=None):
    T = proj.shape[0]

    def body(cx, cy, cw_r, cb_r, wa_r, ba_r, wx_r, bx_r, lam_r, o_ref):
        o = f_lru(cx[...], cy[...], cw_r[...], cb_r[...], wa_r[...], ba_r[...], wx_r[...], bx_r[...], lam_r[...], False)
        o_ref[...] = o.astype(o_ref.dtype)

    return _pcall(
        body, [proj, proj, cw, cb, wa, ba, wx, bx, lam], side, grid=(C_BLOCKS,), in_specs=_lru_specs(T),
        out_specs=pl.BlockSpec((T, LANES), lambda b: (0, b)), out_shape=jax.ShapeDtypeStruct((T, C_WIDTH), BF16),
        compiler_params=_cparams(("arbitrary",)), name="lru_fwd")[0]


def lru_bwd(proj, cw, cb, wa, ba, wx, bx, lam, dcat, side=None):
    T = proj.shape[0]
    oc = (A_Q + B_V) // LANES

    def body(cx, cy, cw_r, cb_r, wa_r, ba_r, wx_r, bx_r, lam_r, do_ref, *outs):
        _, vjp = jax.vjp(functools.partial(f_lru, diff=True), cx[...], cy[...], cw_r[...], cb_r[...], wa_r[...],
                         ba_r[...], wx_r[...], bx_r[...], lam_r[...])
        grads = vjp(do_ref[...])
        for o, g in zip(outs, grads):
            o[...] = g.astype(o.dtype)

    specs = _lru_specs(T)
    out_specs = [pl.BlockSpec((T, LANES), lambda b: (0, b)), pl.BlockSpec((T, LANES), lambda b: (0, b))] + specs[2:]
    out_shape = [jax.ShapeDtypeStruct((T, C_WIDTH), BF16), jax.ShapeDtypeStruct((T, C_WIDTH), BF16)] + \
                [jax.ShapeDtypeStruct(p.shape, F32) for p in (cw, cb, wa, ba, wx, bx, lam)]
    return _pcall(
        body, [proj, proj, cw, cb, wa, ba, wx, bx, lam, dcat], side, grid=(C_BLOCKS,),
        in_specs=specs + [pl.BlockSpec((T, LANES), lambda b: (0, oc + b))], out_specs=out_specs, out_shape=out_shape,
        compiler_params=_cparams(("arbitrary",)), name="lru_bwd")


def all_gather8(name, blk):
    def body(x_ref, out_ref, send_sems, recv_sems):
        x, y, c, _, others = _place()
        sibling = (x, y, 1 - c)

        def slab(px, py, pc):
            return out_ref.at[4 * px + 2 * py + pc]

        first = [_remote(x_ref, slab(x, y, c), send_sems, recv_sems, 0, sibling)]
        first += [_remote(x_ref, slab(x, y, c), send_sems, recv_sems, 1 + j, (*ch, c)) for j, ch in enumerate(others)]
        for cp in first:
            cp.start()
        passed = [_remote(slab(*ch, c), slab(*ch, c), send_sems, recv_sems, 4 + j, sibling) for j, ch in enumerate(others)]
        for j, ch in enumerate(others):
            _remote(x_ref, slab(*ch, c), send_sems, recv_sems, 1 + j, (x, y, c)).wait_recv()
            passed[j].start()
        _remote(x_ref, slab(x, y, 1 - c), send_sems, recv_sems, 0, (x, y, c)).wait_recv()
        for j, ch in enumerate(others):
            _remote(x_ref, slab(*ch, 1 - c), send_sems, recv_sems, 4 + j, (x, y, c)).wait_recv()
        for cp in first + passed:
            cp.wait_send()

    out = pl.pallas_call(
        body, out_shape=jax.ShapeDtypeStruct((8,) + blk.shape, blk.dtype), in_specs=[HBM_SPEC], out_specs=HBM_SPEC,
        scratch_shapes=[pltpu.SemaphoreType.DMA((7,)), pltpu.SemaphoreType.DMA((7,))], name=name)(blk)
    me = 4 * lax.axis_index("x") + 2 * lax.axis_index("y") + lax.axis_index("c")
    return lax.dynamic_update_index_in_dim(out, blk, me, 0)


def _exchange_call(name, body, arrays, out_shapes, n_sems):
    n = len(arrays)

    def kernel_body(*refs):
        body(refs[:n], refs[n:2 * n], refs[2 * n], refs[2 * n + 1])

    return pl.pallas_call(
        kernel_body, out_shape=out_shapes, in_specs=[HBM_SPEC] * n, out_specs=[HBM_SPEC] * n,
        scratch_shapes=[pltpu.SemaphoreType.DMA((n * n_sems,)), pltpu.SemaphoreType.DMA((n * n_sems,))], name=name)(*arrays)


def chip_gather(name, shards):
    def body(ins, outs, send_sems, recv_sems):
        x, y, c, chip, others = _place()
        sibling = (x, y, 1 - c)
        first, passed = [], []
        for a, (x_ref, out_ref) in enumerate(zip(ins, outs)):
            first += [_remote(x_ref.at[c], out_ref.at[chip, c], send_sems, recv_sems, 6 * a + j, (*ch, c))
                      for j, ch in enumerate(others)]
        for cp in first:
            cp.start()
        for a, (x_ref, out_ref) in enumerate(zip(ins, outs)):
            for j, ch in enumerate(others):
                here = out_ref.at[_chip_index(ch), c]
                _remote(x_ref.at[c], here, send_sems, recv_sems, 6 * a + j, (x, y, c)).wait_recv()
                cp = _remote(here, here, send_sems, recv_sems, 6 * a + 3 + j, sibling)
                cp.start()
                passed.append(cp)
        for a, (x_ref, out_ref) in enumerate(zip(ins, outs)):
            for j, ch in enumerate(others):
                _remote(x_ref.at[c], out_ref.at[_chip_index(ch), 1 - c], send_sems, recv_sems, 6 * a + 3 + j, (x, y, c)).wait_recv()
        for cp in first + passed:
            cp.wait_send()

    outs = _exchange_call(name, body, shards, [jax.ShapeDtypeStruct((4,) + s.shape, s.dtype) for s in shards], 6)
    chip = 2 * lax.axis_index("x") + lax.axis_index("y")
    return [lax.dynamic_update_index_in_dim(o, s, chip, 0) for o, s in zip(outs, shards)]


def chip_scatter(name, parts):
    def body(ins, outs, send_sems, recv_sems):
        x, y, c, chip, others = _place()
        sends = []
        for a, (x_ref, out_ref) in enumerate(zip(ins, outs)):
            sends += [_remote(x_ref.at[_chip_index(ch)], out_ref.at[chip], send_sems, recv_sems, 3 * a + j, (*ch, c))
                      for j, ch in enumerate(others)]
        for cp in sends:
            cp.start()
        for a, (x_ref, out_ref) in enumerate(zip(ins, outs)):
            for j, ch in enumerate(others):
                _remote(x_ref.at[chip], out_ref.at[_chip_index(ch)], send_sems, recv_sems, 3 * a + j, (x, y, c)).wait_recv()
        for cp in sends:
            cp.wait_send()

    return _exchange_call(name, body, parts, [jax.ShapeDtypeStruct(p.shape, p.dtype) for p in parts], 3)


def sibling_take(name, halves):
    def body(ins, outs, send_sems, recv_sems):
        x, y, c, _, _ = _place()
        cps = [_remote(x_ref.at[s, 1 - c], out_ref.at[s], send_sems, recv_sems, 4 * a + s, (x, y, 1 - c))
               for a, (x_ref, out_ref) in enumerate(zip(ins, outs)) for s in range(4)]
        for cp in cps:
            cp.start()
        for cp in cps:
            cp.wait()

    return _exchange_call(name, body, halves,
                          [jax.ShapeDtypeStruct((h.shape[0],) + h.shape[2:], h.dtype) for h in halves], 4)


def sibling_pair(name, mine):
    def body(ins, outs, send_sems, recv_sems):
        x, y, c, _, _ = _place()
        cps = [_remote(x_ref, out_ref.at[c], send_sems, recv_sems, a, (x, y, 1 - c))
               for a, (x_ref, out_ref) in enumerate(zip(ins, outs))]
        for cp in cps:
            cp.start()
        for a, (x_ref, out_ref) in enumerate(zip(ins, outs)):
            _remote(x_ref, out_ref.at[1 - c], send_sems, recv_sems, a, (x, y, c)).wait_recv()
        for cp in cps:
            cp.wait_send()

    outs = _exchange_call(name, body, mine, [jax.ShapeDtypeStruct((2,) + m.shape, m.dtype) for m in mine], 1)
    core = lax.axis_index("c")
    return [lax.dynamic_update_index_in_dim(o, m, core, 0) for o, m in zip(outs, mine)]


def sum_slabs(name, r, out_dtype, tr):
    S, R, W = r.shape

    def body(*refs):
        t = refs[0][...].astype(F32)
        for s in range(1, S):
            t = t + refs[s][...].astype(F32)
        refs[S][...] = t.astype(out_dtype)

    return pl.pallas_call(
        body, grid=(R // tr,), in_specs=[pl.BlockSpec((None, tr, W), lambda i, s=s: (s, i, 0)) for s in range(S)],
        out_specs=pl.BlockSpec((tr, W), lambda i: (i, 0)), out_shape=jax.ShapeDtypeStruct((R, W), out_dtype),
        compiler_params=_cparams(("parallel",)), name=name)(*([r] * S))


def sum_chips(name, arrived, own, chip, tr):
    S, R, W = arrived.shape

    def body(chip_ref, own_ref, *refs):
        me = chip_ref[0]
        t = None
        for s in range(S):
            term = jnp.where(me == s, own_ref[...].astype(F32), refs[s][...].astype(F32))
            t = term if t is None else t + term
        refs[S][...] = t

    grid_spec = pltpu.PrefetchScalarGridSpec(
        num_scalar_prefetch=1, grid=(R // tr,),
        in_specs=[pl.BlockSpec((None, tr, W), lambda i, ch: (ch[0], i, 0))] +
                 [pl.BlockSpec((None, tr, W), lambda i, ch, s=s: (s, i, 0)) for s in range(S)],
        out_specs=pl.BlockSpec((tr, W), lambda i, ch: (i, 0)))
    return pl.pallas_call(body, grid_spec=grid_spec, out_shape=jax.ShapeDtypeStruct((R, W), F32),
                          compiler_params=_cparams(("parallel",)), name=name)(
                              chip.reshape(1).astype(jnp.int32), own, *([arrived] * S))


def add_kept_half(name, halves, got, c, tr):
    S, _, R, W = halves.shape

    def body(c_ref, h_ref, g_ref, o_ref):
        o_ref[...] = (h_ref[...].astype(F32) + g_ref[...].astype(F32)).astype(o_ref.dtype)

    grid_spec = pltpu.PrefetchScalarGridSpec(
        num_scalar_prefetch=1, grid=(S, R // tr),
        in_specs=[pl.BlockSpec((None, None, tr, W), lambda s, i, c_ref: (s, c_ref[0], i, 0)),
                  pl.BlockSpec((None, tr, W), lambda s, i, c_ref: (s, i, 0))],
        out_specs=pl.BlockSpec((None, tr, W), lambda s, i, c_ref: (s, i, 0)))
    return pl.pallas_call(body, grid_spec=grid_spec, out_shape=jax.ShapeDtypeStruct((S, R, W), halves.dtype),
                          compiler_params=_cparams(("parallel", "parallel")), name=name)(
                              c.reshape(1).astype(jnp.int32), halves, got)


def adamw_layer(name, l, g, row_off, w, m, v, prev, tr):
    L, R, C = w.shape
    off = row_off // tr

    def body(g_ref, w_ref, m_ref, v_ref, *rest):
        outs = rest[-4:]
        gv = g_ref[...]
        d, mn, vn = f_adamw(w_ref[...], gv, m_ref[...], v_ref[...])
        for o, val in zip(outs, (gv, d, mn, vn)):
            o[...] = val

    slab = pl.BlockSpec((None, tr, C), lambda i: (l, i, 0))
    in_specs = [pl.BlockSpec((tr, C), lambda i: (off + i, 0)), slab, slab, slab]
    args = [g, w, m, v]
    aliases = {}
    if prev is not None:
        in_specs += [pl.BlockSpec(memory_space=pl.ANY)] * 4
        args += list(prev)
        aliases = {4 + k: k for k in range(4)}
    return pl.pallas_call(
        body, grid=(R // tr,), in_specs=in_specs, out_specs=[slab] * 4,
        out_shape=[jax.ShapeDtypeStruct((L, R, C), F32)] * 4, input_output_aliases=aliases,
        compiler_params=_cparams(("parallel",)), name=name)(*args)


BIG = ("w_in", "w_out", "xq", "xk", "xv", "xo", "w_up", "w_down")
PACKED = BIG[1:]
PACK_ROWS = {"w_out": 512, "xq": 512, "xk": 512, "xv": 512, "xo": 512, "w_up": 2048, "w_down": 2048}
PACK_TOTAL = sum(PACK_ROWS.values())
PACK_HALF = PACK_TOTAL // 2
SUM_TILE = 256
PACK_OFF = {}
_o = 0
for _n in PACKED:
    PACK_OFF[_n] = _o
    _o += PACK_ROWS[_n]

_SPLIT_OFF = np.cumsum((0,) + SPLIT_SIZES)
_KORDER = (0, 1, 2, 3, 4, 5, 6, 9, 10, 7, 8)


def w_in_to_kernel_cols(w):
    parts = [w[..., _SPLIT_OFF[i]:_SPLIT_OFF[i + 1]] for i in _KORDER]
    parts.append(jnp.zeros(w.shape[:-1] + (D_INP - D_IN,), w.dtype))
    return jnp.concatenate(parts, axis=-1)


def w_in_from_kernel_cols(w):
    offs = np.cumsum((0,) + tuple(SPLIT_SIZES[i] for i in _KORDER))
    pos = {k: (offs[n], offs[n + 1]) for n, k in enumerate(_KORDER)}
    return jnp.concatenate([w[..., pos[i][0]:pos[i][1]] for i in range(len(SPLIT_SIZES))], axis=-1)


def pack_shards(shards, dtype):
    return jnp.concatenate([shards[n].astype(dtype) for n in PACKED], axis=-2)


def unpack_rows(packed, name):
    return packed[..., PACK_OFF[name]:PACK_OFF[name] + PACK_ROWS[name], :]


WEIGHTS = ("rel_bias", "w_in", "w_out", "attn_sink", "gla_w2_f", "gla_b2_f", "gla_w2_b", "gla_b2_b", "gla_norm", "conv_w",
           "conv_b", "lru_wa", "lru_ba", "lru_wx", "lru_bx", "lru_lambda", "xq", "xk", "xv", "xo", "w_up", "w_down",
           "norm_mix_pre", "norm_mix_post", "norm_mem", "norm_x_pre", "norm_x_post", "norm_ff_pre", "norm_ff_post")
SMALL = tuple(n for n in WEIGHTS if n not in BIG)
SMALL_SHARDED = ("gla_w2_f", "gla_w2_b", "conv_w", "lru_ba", "lru_bx", "lru_lambda")
ROW_TILE = 256
SMALL_TILE = 512
RIDE_PIECE_ROWS = (256, 512)


def _pack_small(arrs):
    flat = jnp.concatenate([a.reshape(-1).astype(F32) for a in arrs])
    n = flat.shape[0]
    rows = -(-n // (SMALL_TILE * LANES)) * SMALL_TILE
    return jnp.pad(flat, (0, rows * LANES - n)).reshape(rows, LANES)


def _unpack_small(buf, shapes):
    lead = buf.shape[:-2]
    flat = buf.reshape(lead + (-1,))
    out, o = [], 0
    for s in shapes:
        n = int(np.prod(s))
        out.append(flat[..., o:o + n].reshape(lead + tuple(s)))
        o += n
    return out


def _relu2(r):
    return r, jnp.square(jnp.maximum(r, 0.0))


def _drelu2(r, u):
    return r * (2.0 * jnp.maximum(u, 0.0))


def kernel(x, mem, rel_bias, w_in, w_out, attn_sink, gla_w2_f, gla_b2_f, gla_w2_b, gla_b2_b, gla_norm, conv_w, conv_b, lru_wa, lru_ba, lru_wx, lru_bx, lru_lambda, xq, xk, xv, xo, w_up, w_down, norm_mix_pre, norm_mix_post, norm_mem, norm_x_pre, norm_x_post, norm_ff_pre, norm_ff_post, loss_target, m_rel_bias, m_w_in, m_w_out, m_attn_sink, m_gla_w2_f, m_gla_b2_f, m_gla_w2_b, m_gla_b2_b, m_gla_norm, m_conv_w, m_conv_b, m_lru_wa, m_lru_ba, m_lru_wx, m_lru_bx, m_lru_lambda, m_xq, m_xk, m_xv, m_xo, m_w_up, m_w_down, m_norm_mix_pre, m_norm_mix_post, m_norm_mem, m_norm_x_pre, m_norm_x_post, m_norm_ff_pre, m_norm_ff_post, v_rel_bias, v_w_in, v_w_out, v_attn_sink, v_gla_w2_f, v_gla_b2_f, v_gla_w2_b, v_gla_b2_b, v_gla_norm, v_conv_w, v_conv_b, v_lru_wa, v_lru_ba, v_lru_wx, v_lru_bx, v_lru_lambda, v_xq, v_xk, v_xv, v_xo, v_w_up, v_w_down, v_norm_mix_pre, v_norm_mix_post, v_norm_mem, v_norm_x_pre, v_norm_x_post, v_norm_ff_pre, v_norm_ff_post):
    w_args = (rel_bias, w_in, w_out, attn_sink, gla_w2_f, gla_b2_f, gla_w2_b, gla_b2_b, gla_norm, conv_w, conv_b, lru_wa,
              lru_ba, lru_wx, lru_bx, lru_lambda, xq, xk, xv, xo, w_up, w_down, norm_mix_pre, norm_mix_post, norm_mem,
              norm_x_pre, norm_x_post, norm_ff_pre, norm_ff_post)
    m_args = (m_rel_bias, m_w_in, m_w_out, m_attn_sink, m_gla_w2_f, m_gla_b2_f, m_gla_w2_b, m_gla_b2_b, m_gla_norm, m_conv_w,
              m_conv_b, m_lru_wa, m_lru_ba, m_lru_wx, m_lru_bx, m_lru_lambda, m_xq, m_xk, m_xv, m_xo, m_w_up, m_w_down,
              m_norm_mix_pre, m_norm_mix_post, m_norm_mem, m_norm_x_pre, m_norm_x_post, m_norm_ff_pre, m_norm_ff_post)
    v_args = (v_rel_bias, v_w_in, v_w_out, v_attn_sink, v_gla_w2_f, v_gla_b2_f, v_gla_w2_b, v_gla_b2_b, v_gla_norm, v_conv_w,
              v_conv_b, v_lru_wa, v_lru_ba, v_lru_wx, v_lru_bx, v_lru_lambda, v_xq, v_xk, v_xv, v_xo, v_w_up, v_w_down,
              v_norm_mix_pre, v_norm_mix_post, v_norm_mem, v_norm_x_pre, v_norm_x_post, v_norm_ff_pre, v_norm_ff_post)
    Wt, Mo, Vo = dict(zip(WEIGHTS, w_args)), dict(zip(WEIGHTS, m_args)), dict(zip(WEIGHTS, v_args))
    x, mem, tgt = x[0], mem[0], loss_target[0]
    D = D_MODEL
    depth = w_in.shape[0]
    chip = 2 * lax.axis_index("x") + lax.axis_index("y")
    core = lax.axis_index("c")

    sm_shapes = [Wt[n].shape for n in SMALL_SHARDED]
    g8 = all_gather8("gather_small_weights", _pack_small([Wt[n] for n in SMALL_SHARDED]))
    per_chip = _unpack_small(g8[0::2], sm_shapes)
    whole = {n: jnp.concatenate([p[j] for j in range(4)], axis=-1) for n, p in zip(SMALL_SHARDED, per_chip)}

    def layer_shards(l):
        shard = pack_shards({n: Wt[n][l] for n in PACKED}, BF16).reshape(2, PACK_HALF, D)
        return [shard, w_in[l].astype(BF16).reshape(2, D // 2, D_IN // 4)]

    def whole_weights(g, g_in):
        g = g.reshape(4, PACK_TOTAL, D)
        W = {}
        win = g_in.reshape(4, D, D_IN // 4).transpose(1, 0, 2).reshape(D, D_IN)
        W["w_in"] = w_in_to_kernel_cols(win)
        for n in ("w_out", "xq", "xk", "xv", "xo"):
            W[n] = unpack_rows(g, n).reshape(D, D)
        W["w_up"] = unpack_rows(g, "w_up")
        W["w_down"] = unpack_rows(g, "w_down").reshape(D_FF, D)
        return W

    Wfull = [whole_weights(*chip_gather("gather_layer_weights", layer_shards(0)))]

    bucket = t5_bucket_map()
    bias = bias_table_fwd(rel_bias, bucket)

    def gain(name, l):
        return Wt[name][l][None]

    def layer_params(l):
        w2fp = jnp.zeros((LANES, B_QK), F32).at[0:GATE_RANK].set(whole["gla_w2_f"][l])
        w2bp = jnp.zeros((LANES, B_QK), F32).at[GATE_RANK:2 * GATE_RANK].set(whole["gla_w2_b"][l])
        gate = [w2fp, gla_b2_f[l][None], w2bp, gla_b2_b[l][None]]
        lru = [whole["conv_w"][l], conv_b[l][None], lru_wa[l], whole["lru_ba"][l], lru_wx[l], whole["lru_bx"][l],
               whole["lru_lambda"][l]]
        return attn_sink[l].reshape(A_KV_HEADS, A_GROUP, 1), gate, gla_norm[l][None], lru

    saved = []
    xcur = x
    (h1,) = rowmap("norm_first", f_norm, [x], [gain("norm_mix_pre", 0)], [(D, BF16)], ROW_TILE)
    loss_acc = None
    for l in range(depth):
        W = Wfull[l]
        sink3, gate, gn, lru = layer_params(l)
        nxt = None
        if l + 1 < depth:
            shards = layer_shards(l + 1)
            nxt = RidingExchange("gather", shards, RIDE_PIECE_ROWS)

        def ride(n):
            return None if nxt is None else nxt.take(n)

        (proj,) = mm("mm_in", h1, W["w_in"], "nn", [(F32, 1)], pm=512, pn=1408, side=ride(2))
        oa = attn_fwd(proj, bias, sink3, side=ride(3))
        zrow, grow = (proj, LANES, OFF_Z // LANES), (proj, B_V, OFF_BG // B_V)
        laf, lab = rowmap("gla_gate", f_gla_gate, [zrow], gate, [(B_QK, F32), (B_QK, F32)], ROW_TILE)
        oraw = gla_fwd(proj, laf, lab, side=ride(2))
        (ob,) = rowmap("gla_post", f_gla_post, [oraw, grow], [gn], [(B_V, BF16)], ROW_TILE)
        oc = lru_fwd(proj, *lru, side=ride(1))
        cat = jnp.concatenate([oa, ob, oc], axis=1)
        (mixed,) = mm("mm_out", cat, W["w_out"], "nn", [(F32, 1)], side=ride(1))
        x1, h2 = rowmap("resnorm_mix", f_resnorm, [xcur, mixed], [gain("norm_mix_post", l), gain("norm_x_pre", l)],
                        [(D, F32), (D, BF16)], ROW_TILE)
        (memn,) = rowmap("norm_mem", f_norm, [mem], [gain("norm_mem", l)], [(D, BF16)], ROW_TILE)
        (q,) = mm("mm_xq", h2, W["xq"], "nn", [(BF16, 1)], side=ride(1))
        (k,) = mm("mm_xk", memn, W["xk"], "nn", [(F32, 1)])
        (v,) = mm("mm_xv", memn, W["xv"], "nn", [(F32, 1)])
        (o,) = rowmap("xattn", f_xattn, [q], [k, v], [(D, BF16)], ROW_TILE)
        (xo_out,) = mm("mm_xo", o, W["xo"], "nn", [(F32, 1)], side=ride(1))
        x2, h3 = rowmap("resnorm_x", f_resnorm, [x1, xo_out], [gain("norm_x_post", l), gain("norm_ff_pre", l)],
                        [(D, F32), (D, BF16)], ROW_TILE)
        u, act = mm("mm_up", h3, W["w_up"], "nn", [(F32, 1), (BF16, 1)], epilogue=_relu2, side=ride(3))
        (ff,) = mm("mm_down", act, W["w_down"], "nn", [(F32, 1)], side=ride(2))
        if nxt is not None:
            got = nxt.flush("gather_rest")
            Wfull.append(whole_weights(*[lax.dynamic_update_index_in_dim(b, s, chip, 0) for b, s in zip(got, shards)]))
        saved.append(dict(x0=xcur, h1=h1, proj=proj, laf=laf, lab=lab, oraw=oraw, cat=cat, mixed=mixed, x1=x1, h2=h2,
                          memn=memn, q=q, k=k, v=v, o=o, xo_out=xo_out, x2=x2, h3=h3, u=u, act=act, ff=ff))
        if l < depth - 1:
            xcur, h1 = rowmap("resnorm_ff", f_resnorm, [x2, ff], [gain("norm_ff_post", l), gain("norm_mix_pre", l + 1)],
                              [(D, F32), (D, BF16)], ROW_TILE)
        else:
            (loss_acc,) = rowmap("final_loss", f_final_loss, [x2, ff, tgt], [gain("norm_ff_post", l)], [], ROW_TILE,
                                 accs=[(1, 1)])
    loss = lax.psum(loss_acc[0, 0], ("x", "y", "c"))

    small_g = {n: [None] * depth for n in SMALL if n != "rel_bias"}
    adam = {}
    dbias_all = []
    dx_next = dh1_next = None
    grad_x = None
    pending = []

    def ride(n):
        return pending[0][3].take(n) if pending else None

    def finish_reduce():
        if not pending:
            return
        lyr, chip_sum, chip_sum_in, exchange = pending.pop()
        arrived, arrived_in = exchange.flush("reduce_rest")
        total = sum_chips("reduce_sum_chips", arrived, chip_sum, chip, SUM_TILE)
        total_in = sum_chips("reduce_sum_chips_in", arrived_in, chip_sum_in, chip, SUM_TILE)
        full, full_in = sibling_pair("reduce_share_halves", [total, total_in])
        full, full_in = full.reshape(PACK_TOTAL, D), full_in.reshape(D, D_IN // 4)
        for n in PACKED:
            adam[n] = adamw_layer("adamw_" + n, lyr, full, PACK_OFF[n], Wt[n], Mo[n], Vo[n], adam.get(n), SUM_TILE)
        adam["w_in"] = adamw_layer("adamw_w_in", lyr, full_in, 0, w_in, m_w_in, v_w_in, adam.get("w_in"), SUM_TILE)

    for l in reversed(range(depth)):
        W, S = Wfull[l], saved[l]
        sink3, gate, gn, lru = layer_params(l)
        if l == depth - 1:
            (dx2, dff), (dgp,) = rowmap_bwd("final_bwd", f_final_rows, [S["x2"], S["ff"], tgt], [gain("norm_ff_post", l)],
                                            [None], ROW_TILE, [F32, F32, None], [True])
        else:
            (dx2, dff), (dgp, dgn_next) = rowmap_bwd(
                "resnorm_ff_bwd", f_resnorm, [S["x2"], S["ff"]], [gain("norm_ff_post", l), gain("norm_mix_pre", l + 1)],
                [dx_next, dh1_next], ROW_TILE, [F32, F32], [True, True])
            small_g["norm_mix_pre"][l + 1] = dgn_next[0]
        small_g["norm_ff_post"][l] = dgp[0]
        dW = {}
        (du,) = mm("mm_down_bwd", dff, W["w_down"], "nt", [(BF16, 1)], epilogue=_drelu2, extras=[S["u"]], side=ride(2))
        (dW["w_down"],) = mm("mm_down_wgrad", S["act"], dff, "tn", [(BF16, 1)], side=ride(2))
        (dW["w_up"],) = mm("mm_up_wgrad", S["h3"], du, "tn", [(BF16, 4)], side=ride(2))
        (dh3,) = mm("mm_up_bwd", du, W["w_up"], "nt", [(F32, 1)], pk=D, side=ride(2))
        (dx1, dxo_out), (dg1, dg2) = rowmap_bwd(
            "resnorm_x_bwd", f_resnorm, [S["x1"], S["xo_out"]], [gain("norm_x_post", l), gain("norm_ff_pre", l)],
            [dx2, dh3], ROW_TILE, [F32, F32], [True, True])
        small_g["norm_x_post"][l], small_g["norm_ff_pre"][l] = dg1[0], dg2[0]
        (do,) = mm("mm_xo_bwd", dxo_out, W["xo"], "nt", [(F32, 1)])
        (dW["xo"],) = mm("mm_xo_wgrad", S["o"], dxo_out, "tn", [(BF16, 1)])
        (dq,), (dk, dv) = rowmap_bwd("xattn_bwd", f_xattn, [S["q"]], [S["k"], S["v"]], [do], ROW_TILE, [BF16], [True, True])
        (dW["xq"],) = mm("mm_xq_wgrad", S["h2"], dq, "tn", [(BF16, 1)])
        (dh2,) = mm("mm_xq_bwd", dq, W["xq"], "nt", [(F32, 1)])
        (dW["xk"],) = mm("mm_xk_wgrad", S["memn"], dk, "tn", [(BF16, 1)])
        (dW["xv"],) = mm("mm_xv_wgrad", S["memn"], dv, "tn", [(BF16, 1)])
        (dmk,) = mm("mm_xk_bwd", dk, W["xk"], "nt", [(F32, 1)])
        (dmv,) = mm("mm_xv_bwd", dv, W["xv"], "nt", [(F32, 1)])
        _, (dgm,) = rowmap_bwd("norm_mem_bwd", f_norm_twice, [mem], [gain("norm_mem", l)], [dmk, dmv], ROW_TILE, [None], [True])
        small_g["norm_mem"][l] = dgm[0]
        (dx0, dmixed), (dg1, dg2) = rowmap_bwd(
            "resnorm_mix_bwd", f_resnorm, [S["x0"], S["mixed"]], [gain("norm_mix_post", l), gain("norm_x_pre", l)],
            [dx1, dh2], ROW_TILE, [F32, F32], [True, True])
        small_g["norm_mix_post"][l], small_g["norm_x_pre"][l] = dg1[0], dg2[0]
        (dcat,) = mm("mm_out_bwd", dmixed, W["w_out"], "nt", [(F32, 1)])
        (dW["w_out"],) = mm("mm_out_wgrad", S["cat"], dmixed, "tn", [(BF16, 1)])
        proj = S["proj"]
        daq, dak, dav, dbias, dsink = attn_bwd(proj, bias, sink3, dcat, side=ride(3))
        dbias_all.append(dbias)
        small_g["attn_sink"][l] = dsink.reshape(A_HEADS)
        zrow, grow = (proj, LANES, OFF_Z // LANES), (proj, B_V, OFF_BG // B_V)
        (doraw, dbg), (dgn,) = rowmap_bwd("gla_post_bwd", f_gla_post, [S["oraw"], grow], [gn], [(dcat, B_V, A_Q // B_V)],
                                          ROW_TILE, [F32, BF16], [True])
        dbq, dbk, dbv, dlaf, dlab = gla_bwd(proj, S["laf"], S["lab"], doraw, side=ride(3))
        (dz,), (dw2fp, db2f, dw2bp, db2b) = rowmap_bwd("gla_gate_bwd", f_gla_gate, [zrow], gate, [dlaf, dlab], ROW_TILE,
                                                        [BF16], [True] * 4)
        small_g["gla_norm"][l] = dgn[0]
        small_g["gla_w2_f"][l], small_g["gla_b2_f"][l] = dw2fp[0:GATE_RANK], db2f[0]
        small_g["gla_w2_b"][l], small_g["gla_b2_b"][l] = dw2bp[GATE_RANK:2 * GATE_RANK], db2b[0]
        dcx, dcy, dcw, dcb, dwa, dba, dwx, dbx, dlam = lru_bwd(proj, *lru, dcat, side=ride(2))
        small_g["conv_w"][l], small_g["conv_b"][l] = dcw, dcb[0]
        small_g["lru_wa"][l], small_g["lru_ba"][l], small_g["lru_wx"][l] = dwa, dba, dwx
        small_g["lru_bx"][l], small_g["lru_lambda"][l] = dbx, dlam
        dproj = jnp.concatenate([daq, dak, dav, dbq, dbk, dbv, dbg, dcx, dcy, dz], axis=1)
        (dW["w_in"],) = mm("mm_in_wgrad", S["h1"], dproj, "tn", [(BF16, 1)], pm=512, pn=1408, side=ride(1))
        (dh1,) = mm("mm_in_bwd", dproj, W["w_in"], "nt", [(F32, 1)], side=ride(1))
        if l > 0:
            dx_next, dh1_next = dx0, dh1
        else:
            (grad_x,), (dg0,) = rowmap_bwd("norm_first_bwd", f_norm_keep, [x], [gain("norm_mix_pre", 0)], [dx0, dh1],
                                           ROW_TILE, [F32], [True])
            small_g["norm_mix_pre"][0] = dg0[0]

        shards = {n: dW[n].reshape(4, D // 4, D) for n in ("w_out", "xq", "xk", "xv", "xo")}
        shards["w_up"] = dW["w_up"]
        shards["w_down"] = dW["w_down"].reshape(4, D_FF // 4, D)
        halves = pack_shards(shards, BF16).reshape(4, 2, PACK_HALF, D)
        halves_in = w_in_from_kernel_cols(dW["w_in"]).reshape(D, 4, D_IN // 4).transpose(1, 0, 2)
        halves_in = halves_in.reshape(4, 2, D // 2, D_IN // 4)
        finish_reduce()
        got, got_in = sibling_take("reduce_to_half_owner", [halves, halves_in])
        chip_sum = add_kept_half("reduce_chip_sum", halves, got, core, SUM_TILE)
        chip_sum_in = add_kept_half("reduce_chip_sum_in", halves_in, got_in, core, SUM_TILE)
        pending.append((l, chip_sum, chip_sum_in, RidingExchange("scatter", [chip_sum, chip_sum_in], RIDE_PIECE_ROWS)))
    finish_reduce()

    (dtab,) = (bias_table_bwd(dbias_all, bucket),)
    sg = {n: jnp.stack(small_g[n]) for n in small_g}
    sg["rel_bias"] = dtab[:, :A_HEADS]
    sg_shapes = [sg[n].shape for n in SMALL]
    contributions = all_gather8("gather_small_grads", _pack_small([sg[n] for n in SMALL]))
    (sg_sum,) = (sum_slabs("sum_small_grads", contributions, F32, SMALL_TILE),)
    sg = dict(zip(SMALL, _unpack_small(sg_sum, sg_shapes)))
    for n in SMALL_SHARDED:
        w = Wt[n].shape[-1]
        sg[n] = lax.dynamic_slice_in_dim(sg[n], chip * w, w, axis=sg[n].ndim - 1)

    grads, delta, new_m, new_v = {}, {}, {}, {}
    for n in BIG:
        grads[n], delta[n], new_m[n], new_v[n] = adam[n]
    shapes = [Wt[n].shape for n in SMALL]
    packs = [_pack_small([src[n] for n in SMALL]) for src in (Wt, sg, Mo, Vo)]
    d_, m_, v_ = rowmap("adamw_small", f_adamw, packs, [], [(LANES, F32)] * 3, SMALL_TILE)
    for n, a, b, c_ in zip(SMALL, _unpack_small(d_, shapes), _unpack_small(m_, shapes), _unpack_small(v_, shapes)):
        grads[n], delta[n], new_m[n], new_v[n] = sg[n], a, b, c_

    return (loss, grad_x[None], *[grads[n] for n in WEIGHTS], *[delta[n] for n in WEIGHTS],
            *[new_m[n] for n in WEIGHTS], *[new_v[n] for n in WEIGHTS])
```

```python
import functools
import math

import numpy as np
import jax
import jax.numpy as jnp
from jax import lax
from jax.experimental import pallas as pl
from jax.experimental.pallas import tpu as pltpu

F32, BF16 = jnp.float32, jnp.bfloat16
HI = lax.Precision.HIGHEST
MESH = pl.DeviceIdType.MESH

VMEM_LIMIT_BYTES = 56 * 1024 * 1024
LANES = 128
SUBLANES = 8

D_MODEL = 2048
DEPTH = 4
A_HEAD_DIM = 128
A_HEADS = 8
A_KV_HEADS = 2
A_GROUP = 4
WINDOW = 128
BLOCK = 128
N_BUCKETS = 32
MAX_DISTANCE = 128
B_HEADS = 4
B_KEY_DIM = 64
B_VAL_DIM = 128
GATE_RANK = 16
GATE_TAU = 16.0
C_WIDTH = 512
C_BLOCKS = 4
C_BLOCK_DIM = 128
CONV_WIDTH = 4
CONV_LEFT = 2
LRU_C = 8.0
X_HEADS = 4
X_HEAD_DIM = 512
D_FF = 4 * D_MODEL
EPS = 1e-6
NEG_INF = -1e30
A_Q, A_KV, B_QK, B_V = 1024, 256, 256, 512
SPLIT_SIZES = (A_Q, A_KV, A_KV, B_QK, B_QK, B_V, B_V, GATE_RANK, GATE_RANK, C_WIDTH, C_WIDTH)
D_IN = sum(SPLIT_SIZES)
D_INP = 4224
OFF_AQ, OFF_AK, OFF_AV, OFF_BQ, OFF_BK, OFF_BV, OFF_BG, OFF_CX, OFF_CY, OFF_Z = (
    0, 1024, 1280, 1536, 1792, 2048, 2560, 3072, 3584, 4096)
GLA_CHUNK = 128

ADAM_LR, ADAM_B1, ADAM_B2, ADAM_EPS, ADAM_WD, ADAM_STEP = 0.001, 0.9, 0.999, 1e-08, 0.01, 10


def _cparams(sem=None):
    return pltpu.CompilerParams(dimension_semantics=sem, vmem_limit_bytes=VMEM_LIMIT_BYTES)


def _full_spec(a):
    nd = a.ndim
    return pl.BlockSpec(a.shape, lambda *_: (0,) * nd)


def _tup(r):
    return r if isinstance(r, tuple) else (r,)


HBM_SPEC = pl.BlockSpec(memory_space=pltpu.HBM)


def _place():
    x, y, c = lax.axis_index("x"), lax.axis_index("y"), lax.axis_index("c")
    others = [(1 - x, y), (x, 1 - y), (1 - x, 1 - y)]
    return x, y, c, 2 * x + y, others


def _remote(src, dst, send_sems, recv_sems, k, to):
    return pltpu.make_async_remote_copy(src_ref=src, dst_ref=dst, send_sem=send_sems.at[k], recv_sem=recv_sems.at[k],
                                        device_id=to, device_id_type=MESH)


def _chip_index(ch):
    return 2 * ch[0] + ch[1]


def _pcall(body, args, side=None, **kw):
    if side is None:
        res = pl.pallas_call(body, **kw)(*args)
        return list(res) if isinstance(res, (list, tuple)) else [res]
    single = not isinstance(kw["out_shape"], (list, tuple))
    out_shape = [kw.pop("out_shape")] if single else list(kw.pop("out_shape"))
    out_specs = [kw.pop("out_specs")] if single else list(kw.pop("out_specs"))
    in_specs = list(kw.pop("in_specs"))
    scratch = list(kw.pop("scratch_shapes", ()))
    grid = kw.get("grid", ())
    n_in, n_out, n_scr = len(in_specs), len(out_shape), len(scratch)
    srcs, bufs = side.srcs, side.bufs
    ns, nb = len(srcs), len(bufs)

    def wrapped(*refs):
        ins = refs[:n_in]
        src_refs = refs[n_in:n_in + ns]
        o0 = n_in + ns + nb
        outs = refs[o0:o0 + n_out]
        buf_refs = refs[o0 + n_out:o0 + n_out + nb]
        s0 = o0 + n_out + nb
        scr = refs[s0:s0 + n_scr]
        send_sems, recv_sems = refs[s0 + n_scr], refs[s0 + n_scr + 1]
        first = last = None
        for d, n in enumerate(grid):
            f, l_ = pl.program_id(d) == 0, pl.program_id(d) == n - 1
            first = f if first is None else first & f
            last = l_ if last is None else last & l_
        if first is None:
            side.start(src_refs, buf_refs, send_sems, recv_sems)
            body(*ins, *outs, *scr)
            side.finish(src_refs, buf_refs, send_sems, recv_sems)
            return
        pl.when(first)(lambda: side.start(src_refs, buf_refs, send_sems, recv_sems))
        body(*ins, *outs, *scr)
        pl.when(last)(lambda: side.finish(src_refs, buf_refs, send_sems, recv_sems))

    any_spec = pl.BlockSpec(memory_space=pl.ANY)
    aliases = dict(kw.pop("input_output_aliases", {}))
    aliases.update({n_in + ns + i: n_out + i for i in range(nb)})
    cp = kw.pop("compiler_params", None)
    if grid:
        cp = _cparams(("arbitrary",) * len(grid))
    res = pl.pallas_call(
        wrapped, in_specs=in_specs + [any_spec] * (ns + nb), out_specs=out_specs + [any_spec] * nb,
        out_shape=out_shape + [jax.ShapeDtypeStruct(b.shape, b.dtype) for b in bufs],
        scratch_shapes=scratch + [pltpu.SemaphoreType.DMA((side.n_sems,)), pltpu.SemaphoreType.DMA((side.n_sems,))],
        input_output_aliases=aliases, compiler_params=cp, **kw)(*args, *srcs, *bufs)
    side.done(list(res[n_out:]))
    return list(res[:n_out])


class _Side:
    def __init__(self, parts):
        self.parts = parts
        self.srcs = [s for ex, _, _ in parts for s in ex.srcs]
        self.bufs = [b for ex, _, _ in parts for b in ex.bufs]
        self.n_sems = 3 * max(1, sum(len(r) + len(f) for _, r, f in parts))

    def done(self, bufs):
        for ex, _, _ in self.parts:
            ex.bufs, bufs = bufs[:len(ex.bufs)], bufs[len(ex.bufs):]

    def _copies(self, src_refs, buf_refs, send_sems, recv_sems):
        x, y, c, chip, others = _place()
        mine, landing = [], []
        k = o = 0
        for ex, ranges, forwards in self.parts:
            srcs, bufs = src_refs[o:o + len(ex.srcs)], buf_refs[o:o + len(ex.srcs)]
            o += len(ex.srcs)
            for a, r0, n in ranges:
                rows = pl.ds(r0, n)
                for ch in others:
                    if ex.kind == "gather":
                        src, dst = srcs[a].at[c, rows], bufs[a].at[chip, c, rows]
                        got = bufs[a].at[_chip_index(ch), c, rows]
                    else:
                        src, dst = srcs[a].at[_chip_index(ch), rows], bufs[a].at[chip, rows]
                        got = bufs[a].at[_chip_index(ch), rows]
                    mine.append(_remote(src, dst, send_sems, recv_sems, k, (*ch, c)))
                    landing.append(_remote(src, got, send_sems, recv_sems, k, (x, y, c)))
                    k += 1
            for a, r0, n in forwards:
                rows = pl.ds(r0, n)
                for ch in others:
                    here = bufs[a].at[_chip_index(ch), c, rows]
                    mine.append(_remote(here, here, send_sems, recv_sems, k, (x, y, 1 - c)))
                    landing.append(_remote(here, bufs[a].at[_chip_index(ch), 1 - c, rows], send_sems, recv_sems, k, (x, y, c)))
                    k += 1
        return mine, landing

    def start(self, src_refs, buf_refs, send_sems, recv_sems):
        for cp in self._copies(src_refs, buf_refs, send_sems, recv_sems)[0]:
            cp.start()

    def finish(self, src_refs, buf_refs, send_sems, recv_sems):
        mine, landing = self._copies(src_refs, buf_refs, send_sems, recv_sems)
        for cp in landing:
            cp.wait_recv()
        for cp in mine:
            cp.wait_send()


class RidingExchange:
    def __init__(self, kind, srcs, piece_rows):
        self.kind, self.srcs = kind, list(srcs)
        lead = (4, 2) if kind == "gather" else (4,)
        self.bufs = [lax.empty(lead + s.shape[1:], s.dtype) for s in srcs]
        heights = [s.shape[1] for s in srcs]
        per = [[(a, r0, min(pr, h - r0)) for r0 in range(0, h, pr)] for a, (h, pr) in enumerate(zip(heights, piece_rows))]
        self.pieces = list(per[0])
        for extra in per[1:]:
            step = max(1, len(self.pieces) // (len(extra) + 1))
            for i, p in enumerate(extra):
                self.pieces.insert(min(len(self.pieces), (i + 1) * step + i), p)
        self.pending_forward = []

    def busy(self):
        return bool(self.pieces or self.pending_forward)

    def step(self, n):
        out = []
        for a, r0, rows in self.pieces[:n]:
            if out and out[-1][0] == a and out[-1][1] + out[-1][2] == r0:
                out[-1] = (a, out[-1][1], out[-1][2] + rows)
            else:
                out.append((a, r0, rows))
        self.pieces = self.pieces[n:]
        fwd, self.pending_forward = self.pending_forward, (out if self.kind == "gather" else [])
        return out, fwd


class Stream:
    def __init__(self):
        self.queue = []

    def add(self, exchange):
        self.queue.append(exchange)
        return exchange

    def take(self, n, only=None):
        parts = []
        for ex in (self.queue if only is None else [only]):
            had = len(ex.pieces)
            now, fwd = ex.step(n)
            n -= had - len(ex.pieces)
            if now or fwd:
                parts.append((ex, now, fwd))
        return _Side(parts) if parts else None

    def finish(self, exchange, name):
        while exchange.busy():
            side = self.take(len(exchange.pieces), only=exchange)
            _pcall(lambda: None, [], side, in_specs=[], out_specs=[], out_shape=[], name=name)
        self.queue.remove(exchange)
        return exchange.bufs


def _row_ops(rows, tr):
    arrs, specs, widths = [], [], []
    for r in rows:
        arr, n, j = r if isinstance(r, tuple) else (r, r.shape[1], 0)
        arrs.append(arr)
        widths.append(n)
        specs.append(pl.BlockSpec((tr, n), lambda i, j=j: (i, j)))
    return arrs, specs, widths


def rowmap(name, f, rows, params, outs, tr, accs=()):
    rows, row_specs, _ = _row_ops(rows, tr)
    T = rows[0].shape[0]
    nin, nout, nacc = len(rows) + len(params), len(outs), len(accs)

    def body(*refs):
        res = _tup(f(*[r[...] for r in refs[:nin]]))
        for o, r in zip(refs[nin:nin + nout], res[:nout]):
            o[...] = r.astype(o.dtype)
        arefs = refs[nin + nout:]
        if nacc:
            @pl.when(pl.program_id(0) == 0)
            def _():
                for a in arefs:
                    a[...] = jnp.zeros(a.shape, a.dtype)
            for a, r in zip(arefs, res[nout:]):
                a[...] += r.astype(F32)

    in_specs = row_specs + [_full_spec(p) for p in params]
    out_specs = [pl.BlockSpec((tr, n), lambda i: (i, 0)) for n, _ in outs] + \
                [pl.BlockSpec(s, lambda i, nd=len(s): (0,) * nd) for s in accs]
    out_shape = [jax.ShapeDtypeStruct((T, n), d) for n, d in outs] + [jax.ShapeDtypeStruct(s, F32) for s in accs]
    res = pl.pallas_call(body, grid=(T // tr,), in_specs=in_specs, out_specs=out_specs, out_shape=out_shape,
                         compiler_params=_cparams(("arbitrary",)), name=name)(*rows, *params)
    return tuple(res)


def rowmap_bwd(name, f, rows, params, cots, tr, drow_dtypes, want_params):
    rows, row_specs, widths = _row_ops(rows, tr)
    T = rows[0].shape[0]
    nr, npar = len(rows), len(params)
    cot_arrays, cot_specs, _ = _row_ops([c for c in cots if c is not None], tr)
    nc = len(cot_arrays)
    ridx = [i for i, d in enumerate(drow_dtypes) if d is not None]
    pidx = [i for i, w in enumerate(want_params) if w]

    def body(*refs):
        rvals = [r[...] for r in refs[:nr]]
        pvals = [r[...] for r in refs[nr:nr + npar]]
        crefs = list(refs[nr + npar:nr + npar + nc])
        orefs = refs[nr + npar + nc:]
        outs, vjp = jax.vjp(f, *rvals, *pvals)
        outs = _tup(outs)
        cts = []
        for c, o in zip(cots, outs):
            cts.append(jnp.ones(o.shape, o.dtype) if c is None else crefs.pop(0)[...].astype(o.dtype))
        grads = vjp(tuple(cts) if len(cts) > 1 else cts[0])
        for o, i in zip(orefs[:len(ridx)], ridx):
            o[...] = grads[i].astype(o.dtype)
        prefs = orefs[len(ridx):]
        if prefs:
            @pl.when(pl.program_id(0) == 0)
            def _():
                for a in prefs:
                    a[...] = jnp.zeros(a.shape, a.dtype)
            for a, i in zip(prefs, pidx):
                a[...] += grads[nr + i].astype(F32)

    in_specs = row_specs + [_full_spec(p) for p in params] + cot_specs
    out_specs = [pl.BlockSpec((tr, widths[i]), lambda i: (i, 0)) for i in ridx] + [_full_spec(params[i]) for i in pidx]
    out_shape = [jax.ShapeDtypeStruct((T, widths[i]), drow_dtypes[i]) for i in ridx] + \
                [jax.ShapeDtypeStruct(params[i].shape, F32) for i in pidx]
    res = pl.pallas_call(body, grid=(T // tr,), in_specs=in_specs, out_specs=out_specs, out_shape=out_shape,
                         compiler_params=_cparams(("arbitrary",)), name=name)(*rows, *params, *cot_arrays)
    res = tuple(res)
    return res[:len(ridx)], res[len(ridx):]


def _pick(n, pref):
    best = None
    for d in range(LANES, min(n, pref) + 1, LANES):
        if n % d == 0:
            best = d
    return best if best is not None else n


def _spec2(arr, tile, pos):
    tr, tc = tile
    if arr.ndim == 2:
        return pl.BlockSpec((tr, tc), lambda i, j, k: pos(i, j, k))
    assert arr.shape[2] % tc == 0, (arr.shape, tile)
    per = arr.shape[2] // tc

    def imap(i, j, k):
        r, c = pos(i, j, k)
        return (c // per, r, c % per)
    return pl.BlockSpec((None, tr, tc), imap)


def _dims2(arr):
    return (arr.shape[0], arr.shape[1]) if arr.ndim == 2 else (arr.shape[1], arr.shape[0] * arr.shape[2])


def mm(name, a, b, mode, outs, epilogue=None, extras=(), pm=1024, pn=512, pk=4224, side=None):
    ar, ac = _dims2(a)
    br, bc = _dims2(b)
    if mode == "nn":
        M, K, N = ar, ac, bc
    elif mode == "nt":
        M, K, N = ar, ac, br
    else:
        M, K, N = ac, ar, bc
    tm, tn, tk = _pick(M, pm), _pick(N, pn), _pick(K, pk)
    for arr in (a, b) + tuple(extras):
        if arr.ndim == 3:
            assert arr.shape[2] % LANES == 0
    if mode == "nn":
        a_spec = _spec2(a, (tm, tk), lambda i, j, k: (i, k))
        b_spec = _spec2(b, (tk, tn), lambda i, j, k: (k, j))
        dims = (((1,), (0,)), ((), ()))
    elif mode == "nt":
        a_spec = _spec2(a, (tm, tk), lambda i, j, k: (i, k))
        b_spec = _spec2(b, (tn, tk), lambda i, j, k: (j, k))
        dims = (((1,), (1,)), ((), ()))
    else:
        a_spec = _spec2(a, (tk, tm), lambda i, j, k: (k, i))
        b_spec = _spec2(b, (tk, tn), lambda i, j, k: (k, j))
        dims = (((0,), (0,)), ((), ()))
    nk = K // tk
    nex = len(extras)

    def body(*refs):
        a_ref, b_ref = refs[0], refs[1]
        ex_refs = refs[2:2 + nex]
        o_refs = refs[2 + nex:2 + nex + len(outs)]
        acc = refs[-1]
        k = pl.program_id(2)
        part = lax.dot_general(a_ref[...].astype(BF16), b_ref[...].astype(BF16), dims, preferred_element_type=F32)

        def finish(r):
            res = (r,) if epilogue is None else _tup(epilogue(r, *[e[...] for e in ex_refs]))
            for o, v in zip(o_refs, res):
                o[...] = v.astype(o.dtype)

        if nk == 1:
            finish(part)
            return

        @pl.when(k == 0)
        def _():
            acc[...] = part

        @pl.when(k > 0)
        def _():
            acc[...] += part

        @pl.when(k == nk - 1)
        def _():
            finish(acc[...])

    out_shape, out_specs = [], []
    for dt, chunks in outs:
        if chunks == 1:
            shp = (M, N)
        else:
            shp = (chunks, M, N // chunks)
        o = jax.ShapeDtypeStruct(shp, dt)
        out_shape.append(o)
        out_specs.append(_spec2(o, (tm, tn), lambda i, j, k: (i, j)))
    ex_specs = [_spec2(e, (tm, tn), lambda i, j, k: (i, j)) for e in extras]
    res = _pcall(
        body, [a, b, *extras], side, grid=(M // tm, N // tn, nk), in_specs=[a_spec, b_spec] + ex_specs,
        out_specs=out_specs, out_shape=out_shape, scratch_shapes=[pltpu.VMEM((tm, tn), F32)] if nk > 1 else [],
        compiler_params=_cparams(("parallel", "parallel", "arbitrary")), name=name)
    return tuple(res)


def _rms(x, g):
    return x * lax.rsqrt(jnp.mean(x * x, axis=-1, keepdims=True) + EPS) * g


def f_norm(x, g):
    return _rms(x, g)


def f_norm_keep(x, g):
    return x, _rms(x, g)


def f_resnorm(xp, m, gpost, gnext):
    xn = xp + _rms(m, gpost)
    return xn, _rms(xn, gnext)


def f_final_rows(xp, m, tgt, gpost):
    xn = xp + _rms(m, gpost)
    return 0.5 * jnp.mean(jnp.square(xn - tgt), axis=-1, keepdims=True)


def f_final_loss(xp, m, tgt, gpost):
    return jnp.sum(f_final_rows(xp, m, tgt, gpost), axis=0, keepdims=True)


def f_norm_twice(x, g):
    y = _rms(x, g)
    return y, y


def f_xattn(q, k, v):
    outs = []
    for h in range(X_HEADS):
        sl = slice(h * X_HEAD_DIM, (h + 1) * X_HEAD_DIM)
        s = lax.dot_general(q[:, sl].astype(BF16), k[:, sl].astype(BF16), (((1,), (1,)), ((), ())),
                            preferred_element_type=F32) * (X_HEAD_DIM ** -0.5)
        m = jnp.max(s, axis=-1, keepdims=True)
        e = jnp.exp(s - m)
        p = e / jnp.sum(e, axis=-1, keepdims=True)
        outs.append(jnp.dot(p.astype(BF16), v[:, sl].astype(BF16), preferred_element_type=F32))
    return jnp.concatenate(outs, axis=1)


def f_adamw(w, g, m, v):
    m = ADAM_B1 * m + (1.0 - ADAM_B1) * g
    v = ADAM_B2 * v + (1.0 - ADAM_B2) * jnp.square(g)
    m_hat = m / (1.0 - ADAM_B1 ** ADAM_STEP)
    v_hat = v / (1.0 - ADAM_B2 ** ADAM_STEP)
    delta = -ADAM_LR * (m_hat / (jnp.sqrt(v_hat) + ADAM_EPS) + ADAM_WD * w)
    return delta, m, v


def f_sum8(*xs):
    t = xs[0]
    for x in xs[1:]:
        t = t + x
    return t


def t5_bucket_map():
    qi = jnp.arange(BLOCK)[:, None]
    kj = jnp.arange(3 * BLOCK)[None, :]
    rel = kj - BLOCK - qi
    nb = N_BUCKETS // 2
    max_exact = nb // 2
    ret = jnp.where(rel > 0, nb, 0)
    n = jnp.abs(rel)
    nf = jnp.maximum(n, 1).astype(jnp.float32)
    large = max_exact + (jnp.log(nf / max_exact) / math.log(MAX_DISTANCE / max_exact) * (nb - max_exact)).astype(jnp.int32)
    large = jnp.minimum(large, nb - 1)
    return (ret + jnp.where(n < max_exact, n, large)).astype(jnp.int32)


def bias_table_fwd(table, bucket):
    def body(t_ref, b_ref, o_ref):
        bk = b_ref[...]
        for h in range(A_HEADS):
            acc = jnp.zeros(bk.shape, F32)
            for b in range(N_BUCKETS):
                acc = jnp.where(bk == b, t_ref[b, h], acc)
            o_ref[h] = acc
    return pl.pallas_call(
        body, in_specs=[pl.BlockSpec(memory_space=pltpu.SMEM), pl.BlockSpec(memory_space=pltpu.VMEM)],
        out_specs=pl.BlockSpec(memory_space=pltpu.VMEM),
        out_shape=jax.ShapeDtypeStruct((A_HEADS, BLOCK, 3 * BLOCK), F32), name="bias_table_fwd")(table, bucket)


def bias_table_bwd(dbias_list, bucket):
    n = len(dbias_list)

    def body(*refs):
        b_ref, o_ref = refs[n], refs[n + 1]
        bk = b_ref[...]
        row = lax.broadcasted_iota(jnp.int32, (N_BUCKETS, LANES), 0)
        col = lax.broadcasted_iota(jnp.int32, (N_BUCKETS, LANES), 1)
        out = jnp.zeros((N_BUCKETS, LANES), F32)
        for h in range(A_HEADS):
            d = refs[0][h]
            for r in refs[1:n]:
                d = d + r[h]
            for b in range(N_BUCKETS):
                s = jnp.sum(jnp.where(bk == b, d, 0.0), keepdims=True)
                out = out + jnp.where((row == b) & (col == h), s, 0.0)
        o_ref[...] = out
    return pl.pallas_call(
        body, out_shape=jax.ShapeDtypeStruct((N_BUCKETS, LANES), F32), name="bias_table_bwd",
        compiler_params=_cparams())(*dbias_list, bucket)


def _attn_mask(n, nblk):
    i = lax.broadcasted_iota(jnp.int32, (BLOCK, 3 * BLOCK), 0)
    j = lax.broadcasted_iota(jnp.int32, (BLOCK, 3 * BLOCK), 1)
    kpos = n * BLOCK + j - BLOCK
    return (jnp.abs(j - BLOCK - i) <= WINDOW) & (kpos >= 0) & (kpos < nblk * BLOCK)


def f_attn_block(q, k3, v3, bias, sink, mask):
    kb, vb = k3.astype(BF16), v3.astype(BF16)
    outs = []
    for g in range(A_GROUP):
        qg = q[:, g * A_HEAD_DIM:(g + 1) * A_HEAD_DIM].astype(BF16)
        s = lax.dot_general(qg, kb, (((1,), (1,)), ((), ())), preferred_element_type=F32) * (A_HEAD_DIM ** -0.5)
        s = jnp.where(mask, s + bias[g], NEG_INF)
        sk = sink[g:g + 1, :]
        m = jnp.maximum(jnp.max(s, axis=-1, keepdims=True), sk)
        e = jnp.exp(s - m)
        den = jnp.sum(e, axis=-1, keepdims=True) + jnp.exp(sk - m)
        p = e / den
        outs.append(jnp.dot(p.astype(BF16), vb, preferred_element_type=F32))
    return jnp.concatenate(outs, axis=1)


def _attn_in_specs(nblk):
    qw = A_GROUP * A_HEAD_DIM
    kc, vc = OFF_AK // A_HEAD_DIM, OFF_AV // A_HEAD_DIM
    return [
        pl.BlockSpec((BLOCK, qw), lambda h, n: (n, h)),
        pl.BlockSpec((BLOCK, A_HEAD_DIM), lambda h, n: (jnp.maximum(n - 1, 0), kc + h)),
        pl.BlockSpec((BLOCK, A_HEAD_DIM), lambda h, n: (n, kc + h)),
        pl.BlockSpec((BLOCK, A_HEAD_DIM), lambda h, n: (jnp.minimum(n + 1, nblk - 1), kc + h)),
        pl.BlockSpec((BLOCK, A_HEAD_DIM), lambda h, n: (jnp.maximum(n - 1, 0), vc + h)),
        pl.BlockSpec((BLOCK, A_HEAD_DIM), lambda h, n: (n, vc + h)),
        pl.BlockSpec((BLOCK, A_HEAD_DIM), lambda h, n: (jnp.minimum(n + 1, nblk - 1), vc + h)),
        pl.BlockSpec((A_GROUP, BLOCK, 3 * BLOCK), lambda h, n: (h, 0, 0)),
        pl.BlockSpec((None, A_GROUP, 1), lambda h, n: (h, 0, 0)),
    ]


def attn_fwd(proj, bias, sink, side=None):
    T = proj.shape[0]
    nblk = T // BLOCK

    def body(q_ref, k0, k1, k2, v0, v1, v2, b_ref, s_ref, o_ref):
        n = pl.program_id(1)
        k3 = jnp.concatenate([k0[...], k1[...], k2[...]], axis=0)
        v3 = jnp.concatenate([v0[...], v1[...], v2[...]], axis=0)
        o = f_attn_block(q_ref[...], k3, v3, b_ref[...], s_ref[...], _attn_mask(n, nblk))
        o_ref[...] = o.astype(o_ref.dtype)

    return _pcall(
        body, [proj] * 7 + [bias, sink], side, grid=(A_KV_HEADS, nblk), in_specs=_attn_in_specs(nblk),
        out_specs=pl.BlockSpec((BLOCK, A_GROUP * A_HEAD_DIM), lambda h, n: (n, h)),
        out_shape=jax.ShapeDtypeStruct((T, A_Q), BF16),
        compiler_params=_cparams(("arbitrary", "arbitrary")), name="attn_fwd")[0]


def attn_bwd(proj, bias, sink, dcat, side=None):
    T = proj.shape[0]
    nblk = T // BLOCK
    qw = A_GROUP * A_HEAD_DIM

    def body(q_ref, k0, k1, k2, v0, v1, v2, b_ref, s_ref, do_ref, dq_ref, dk_ref, dv_ref, db_ref, ds_ref, dk_acc, dv_acc):
        n = pl.program_id(1)

        @pl.when(n == 0)
        def _():
            dk_acc[...] = jnp.zeros(dk_acc.shape, F32)
            dv_acc[...] = jnp.zeros(dv_acc.shape, F32)
            db_ref[...] = jnp.zeros(db_ref.shape, F32)
            ds_ref[...] = jnp.zeros(ds_ref.shape, F32)

        k3 = jnp.concatenate([k0[...], k1[...], k2[...]], axis=0)
        v3 = jnp.concatenate([v0[...], v1[...], v2[...]], axis=0)
        mask = _attn_mask(n, nblk)
        _, vjp = jax.vjp(lambda q, k, v, b, s: f_attn_block(q, k, v, b, s, mask), q_ref[...], k3, v3, b_ref[...], s_ref[...])
        dq, dk3, dv3, db, ds = vjp(do_ref[...])
        dq_ref[...] = dq.astype(dq_ref.dtype)
        db_ref[...] += db
        ds_ref[...] += ds
        mid = pl.multiple_of(n * BLOCK, BLOCK)
        dk_acc[pl.ds(mid, BLOCK), :] += dk3[BLOCK:2 * BLOCK]
        dv_acc[pl.ds(mid, BLOCK), :] += dv3[BLOCK:2 * BLOCK]

        @pl.when(n > 0)
        def _():
            lo = pl.multiple_of((n - 1) * BLOCK, BLOCK)
            dk_acc[pl.ds(lo, BLOCK), :] += dk3[0:BLOCK]
            dv_acc[pl.ds(lo, BLOCK), :] += dv3[0:BLOCK]

        @pl.when(n < nblk - 1)
        def _():
            hi = pl.multiple_of((n + 1) * BLOCK, BLOCK)
            dk_acc[pl.ds(hi, BLOCK), :] += dk3[2 * BLOCK:3 * BLOCK]
            dv_acc[pl.ds(hi, BLOCK), :] += dv3[2 * BLOCK:3 * BLOCK]

        @pl.when(n == nblk - 1)
        def _():
            dk_ref[...] = dk_acc[...].astype(dk_ref.dtype)
            dv_ref[...] = dv_acc[...].astype(dv_ref.dtype)

    in_specs = _attn_in_specs(nblk) + [pl.BlockSpec((BLOCK, qw), lambda h, n: (n, h))]
    out_specs = [
        pl.BlockSpec((BLOCK, qw), lambda h, n: (n, h)),
        pl.BlockSpec((T, A_HEAD_DIM), lambda h, n: (0, h)),
        pl.BlockSpec((T, A_HEAD_DIM), lambda h, n: (0, h)),
        pl.BlockSpec((A_GROUP, BLOCK, 3 * BLOCK), lambda h, n: (h, 0, 0)),
        pl.BlockSpec((None, A_GROUP, 1), lambda h, n: (h, 0, 0)),
    ]
    out_shape = [
        jax.ShapeDtypeStruct((T, A_Q), BF16), jax.ShapeDtypeStruct((T, A_KV), BF16), jax.ShapeDtypeStruct((T, A_KV), BF16),
        jax.ShapeDtypeStruct((A_HEADS, BLOCK, 3 * BLOCK), F32), jax.ShapeDtypeStruct((A_KV_HEADS, A_GROUP, 1), F32),
    ]
    return _pcall(
        body, [proj] * 7 + [bias, sink, dcat], side, grid=(A_KV_HEADS, nblk), in_specs=in_specs, out_specs=out_specs,
        out_shape=out_shape, scratch_shapes=[pltpu.VMEM((T, A_HEAD_DIM), F32), pltpu.VMEM((T, A_HEAD_DIM), F32)],
        compiler_params=_cparams(("arbitrary", "arbitrary")), name="attn_bwd")


def f_gla_gate(z, w2f, b2f, w2b, b2b):
    laf = jax.nn.log_sigmoid(jnp.dot(z, w2f, precision=HI, preferred_element_type=F32) + b2f) / GATE_TAU
    lab = jax.nn.log_sigmoid(jnp.dot(z, w2b, precision=HI, preferred_element_type=F32) + b2b) / GATE_TAU
    return laf, lab


def f_gla_post(o, g, gn):
    outs = []
    for h in range(B_HEADS):
        sl = slice(h * B_VAL_DIM, (h + 1) * B_VAL_DIM)
        oh = o[:, sl]
        outs.append(oh * lax.rsqrt(jnp.mean(oh * oh, axis=-1, keepdims=True) + EPS))
    return jnp.concatenate(outs, axis=1) * gn * jax.nn.silu(g)


def _gla_consts(forward):
    C = GLA_CHUNK
    i = lax.broadcasted_iota(jnp.int32, (C, C), 0)
    j = lax.broadcasted_iota(jnp.int32, (C, C), 1)
    if forward:
        return (j <= i).astype(F32), j <= i
    return (j >= i).astype(F32), j > i


def _gla_chunk(q, k, v, la, st, tri, msk, forward):
    C = q.shape[0]
    b = jnp.dot(tri, la, precision=HI, preferred_element_type=F32)
    bl = b[C - 1:C] if forward else b[0:1]
    qe = (q * (B_KEY_DIM ** -0.5)) * jnp.exp(b)
    ke = k * jnp.exp(-b)
    kl = k * jnp.exp(bl - b)
    att = lax.dot_general(qe.astype(BF16), ke.astype(BF16), (((1,), (1,)), ((), ())), preferred_element_type=F32)
    att = jnp.where(msk, att, 0.0)
    o = jnp.dot(att.astype(BF16), v.astype(BF16), preferred_element_type=F32)
    o = o + lax.dot_general(qe.astype(BF16), st.astype(BF16), (((1,), (1,)), ((), ())), preferred_element_type=F32)
    st_new = st * jnp.exp(bl) + lax.dot_general(v.astype(BF16), kl.astype(BF16), (((0,), (0,)), ((), ())),
                                                preferred_element_type=F32)
    return o, st_new


def _gla_specs(T):
    qc, kc, vc = OFF_BQ // LANES, OFF_BK // LANES, OFF_BV // (2 * B_VAL_DIM)
    return [
        pl.BlockSpec((T, LANES), lambda p: (0, qc + p)),
        pl.BlockSpec((T, LANES), lambda p: (0, kc + p)),
        pl.BlockSpec((T, 2 * B_VAL_DIM), lambda p: (0, vc + p)),
        pl.BlockSpec((T, LANES), lambda p: (0, p)),
        pl.BlockSpec((T, LANES), lambda p: (0, p)),
    ]


def _rows(c):
    return pl.ds(pl.multiple_of(c * GLA_CHUNK, GLA_CHUNK), GLA_CHUNK)


def gla_fwd(proj, laf, lab, side=None):
    T = proj.shape[0]
    nc = T // GLA_CHUNK

    def body(q_ref, k_ref, v_ref, laf_ref, lab_ref, o_ref, ob_scr):
        tri_f, msk_f = _gla_consts(True)
        tri_b, msk_b = _gla_consts(False)
        zero = jnp.zeros((B_VAL_DIM, B_KEY_DIM), F32)

        def step(c, carry):
            rf, rb = _rows(c), _rows(nc - 1 - c)
            new = []
            for hh in range(2):
                ks = slice(hh * B_KEY_DIM, (hh + 1) * B_KEY_DIM)
                vs = slice(hh * B_VAL_DIM, (hh + 1) * B_VAL_DIM)
                o, s = _gla_chunk(q_ref[rf, ks], k_ref[rf, ks], v_ref[rf, vs], laf_ref[rf, ks], carry[2 * hh], tri_f, msk_f, True)
                o_ref[rf, vs] = o
                new.append(s)
                o, s = _gla_chunk(q_ref[rb, ks], k_ref[rb, ks], v_ref[rb, vs], lab_ref[rb, ks], carry[2 * hh + 1], tri_b, msk_b, False)
                ob_scr[rb, vs] = o
                new.append(s)
            return tuple(new)

        lax.fori_loop(0, nc, step, (zero,) * 4)
        o_ref[...] += ob_scr[...]

    return _pcall(
        body, [proj, proj, proj, laf, lab], side, grid=(B_HEADS // 2,), in_specs=_gla_specs(T),
        out_specs=pl.BlockSpec((T, 2 * B_VAL_DIM), lambda p: (0, p)),
        out_shape=jax.ShapeDtypeStruct((T, B_V), F32),
        scratch_shapes=[pltpu.VMEM((T, 2 * B_VAL_DIM), F32)],
        compiler_params=_cparams(("arbitrary",)), name="gla_fwd")[0]


def gla_bwd(proj, laf, lab, do, side=None):
    T = proj.shape[0]
    nc = T // GLA_CHUNK
    SROWS = 2 * B_VAL_DIM

    def body(q_ref, k_ref, v_ref, laf_ref, lab_ref, do_ref, dq_ref, dk_ref, dv_ref, dlaf_ref, dlab_ref,
             sf_scr, sb_scr, dq_acc, dk_acc, dv_acc):
        tri_f, msk_f = _gla_consts(True)
        tri_b, msk_b = _gla_consts(False)
        zero = jnp.zeros((B_VAL_DIM, B_KEY_DIM), F32)
        dq_acc[...] = jnp.zeros(dq_acc.shape, F32)
        dk_acc[...] = jnp.zeros(dk_acc.shape, F32)
        dv_acc[...] = jnp.zeros(dv_acc.shape, F32)

        def srow(c, hh):
            return pl.ds(pl.multiple_of(c * SROWS + hh * B_VAL_DIM, B_VAL_DIM), B_VAL_DIM)

        def states(c, carry):
            cf, cb = c, nc - 1 - c
            rf, rb = _rows(cf), _rows(cb)
            new = []
            for hh in range(2):
                ks = slice(hh * B_KEY_DIM, (hh + 1) * B_KEY_DIM)
                vs = slice(hh * B_VAL_DIM, (hh + 1) * B_VAL_DIM)
                sf_scr[srow(cf, hh), :] = carry[2 * hh]
                _, s = _gla_chunk(q_ref[rf, ks], k_ref[rf, ks], v_ref[rf, vs], laf_ref[rf, ks], carry[2 * hh], tri_f, msk_f, True)
                new.append(s)
                sb_scr[srow(cb, hh), :] = carry[2 * hh + 1]
                _, s = _gla_chunk(q_ref[rb, ks], k_ref[rb, ks], v_ref[rb, vs], lab_ref[rb, ks], carry[2 * hh + 1], tri_b, msk_b, False)
                new.append(s)
            return tuple(new)

        lax.fori_loop(0, nc, states, (zero,) * 4)

        def back(c, carry):
            cf, cb = nc - 1 - c, c
            rf, rb = _rows(cf), _rows(cb)
            new = []
            for hh in range(2):
                ks = slice(hh * B_KEY_DIM, (hh + 1) * B_KEY_DIM)
                vs = slice(hh * B_VAL_DIM, (hh + 1) * B_VAL_DIM)
                for fwd, r, c_, la_ref, dla_ref, s_scr, g, tri, msk in (
                        (True, rf, cf, laf_ref, dlaf_ref, sf_scr, carry[2 * hh], tri_f, msk_f),
                        (False, rb, cb, lab_ref, dlab_ref, sb_scr, carry[2 * hh + 1], tri_b, msk_b)):
                    _, vjp = jax.vjp(
                        lambda q, k, v, la, st: _gla_chunk(q, k, v, la, st, tri, msk, fwd),
                        q_ref[r, ks], k_ref[r, ks], v_ref[r, vs], la_ref[r, ks], s_scr[srow(c_, hh), :])
                    dq, dk, dv, dla, dst = vjp((do_ref[r, vs], g))
                    dq_acc[r, ks] += dq
                    dk_acc[r, ks] += dk
                    dv_acc[r, vs] += dv
                    dla_ref[r, ks] = dla
                    new.append(dst)
            return tuple(new)

        lax.fori_loop(0, nc, back, (zero,) * 4)
        dq_ref[...] = dq_acc[...].astype(dq_ref.dtype)
        dk_ref[...] = dk_acc[...].astype(dk_ref.dtype)
        dv_ref[...] = dv_acc[...].astype(dv_ref.dtype)

    in_specs = _gla_specs(T) + [pl.BlockSpec((T, 2 * B_VAL_DIM), lambda p: (0, p))]
    out_specs = [
        pl.BlockSpec((T, LANES), lambda p: (0, p)), pl.BlockSpec((T, LANES), lambda p: (0, p)),
        pl.BlockSpec((T, 2 * B_VAL_DIM), lambda p: (0, p)),
        pl.BlockSpec((T, LANES), lambda p: (0, p)), pl.BlockSpec((T, LANES), lambda p: (0, p)),
    ]
    out_shape = [
        jax.ShapeDtypeStruct((T, B_QK), BF16), jax.ShapeDtypeStruct((T, B_QK), BF16), jax.ShapeDtypeStruct((T, B_V), BF16),
        jax.ShapeDtypeStruct((T, B_QK), F32), jax.ShapeDtypeStruct((T, B_QK), F32),
    ]
    scratch = [
        pltpu.VMEM((nc * SROWS, B_KEY_DIM), F32), pltpu.VMEM((nc * SROWS, B_KEY_DIM), F32),
        pltpu.VMEM((T, LANES), F32), pltpu.VMEM((T, LANES), F32), pltpu.VMEM((T, 2 * B_VAL_DIM), F32),
    ]
    return _pcall(
        body, [proj, proj, proj, laf, lab, do], side, grid=(B_HEADS // 2,), in_specs=in_specs, out_specs=out_specs,
        out_shape=out_shape, scratch_shapes=scratch, compiler_params=_cparams(("arbitrary",)), name="gla_bwd")


def _shift_raw(x, k):
    T = x.shape[0]
    r = lax.broadcasted_iota(jnp.int32, x.shape, 0)
    if k > 0:
        return jnp.where(r >= k, pltpu.roll(x, k, 0), 0.0)
    return jnp.where(r < T + k, pltpu.roll(x, T + k, 0), 0.0)


@functools.partial(jax.custom_vjp, nondiff_argnums=(1,))
def _shift(x, k):
    return _shift_raw(x, k)


_shift.defvjp(lambda x, k: (_shift_raw(x, k), None), lambda k, _, g: (_shift_raw(g, -k),))


def _scan_raw(a, u, reverse):
    T = a.shape[0]
    d = 1
    while d < T:
        k = -d if reverse else d
        u = a * _shift_raw(u, k) + u
        a = a * _shift_raw(a, k)
        d *= 2
    return u


@functools.partial(jax.custom_vjp, nondiff_argnums=(2,))
def _scan(a, u, reverse):
    return _scan_raw(a, u, reverse)


def _scan_f(a, u, reverse):
    h = _scan_raw(a, u, reverse)
    return h, (a, h)


def _scan_b(reverse, res, dh):
    a, h = res
    k = 1 if reverse else -1
    du = _scan_raw(_shift_raw(a, k), dh, not reverse)
    return du * _shift_raw(h, -k), du


_scan.defvjp(_scan_f, _scan_b)


def f_lru(cx, cy, cw, cb, wa, ba, wx, bx, lam, diff):
    shift, scan = (_shift, _scan) if diff else (_shift_raw, _scan_raw)
    xc = cx * cw[CONV_LEFT:CONV_LEFT + 1]
    for j in range(CONV_WIDTH):
        if j != CONV_LEFT:
            xc = xc + shift(cx, CONV_LEFT - j) * cw[j:j + 1]
    xc = xc + cb
    xb = xc.astype(BF16)
    h = None
    for s in range(2):
        r = jax.nn.sigmoid(jnp.dot(xb, wa[s].astype(BF16), preferred_element_type=F32) + ba[s:s + 1])
        i = jax.nn.sigmoid(jnp.dot(xb, wx[s].astype(BF16), preferred_element_type=F32) + bx[s:s + 1])
        log_a = -LRU_C * r * jax.nn.softplus(-lam[s:s + 1])
        a = jnp.exp(log_a)
        one_minus_a2 = -jnp.tanh(log_a) * (a * a + 1.0)
        u = jnp.sqrt(one_minus_a2) * (i * xc)
        hs = scan(a, u, s == 1)
        h = hs if h is None else h + hs
    return h * jax.nn.gelu(cy)


def _lru_specs(T):
    xc, yc = OFF_CX // LANES, OFF_CY // LANES
    return [
        pl.BlockSpec((T, LANES), lambda b: (0, xc + b)),
        pl.BlockSpec((T, LANES), lambda b: (0, yc + b)),
        pl.BlockSpec((CONV_WIDTH, LANES), lambda b: (0, b)),
        pl.BlockSpec((1, LANES), lambda b: (0, b)),
        pl.BlockSpec((2, None, C_BLOCK_DIM, C_BLOCK_DIM), lambda b: (0, b, 0, 0)),
        pl.BlockSpec((2, LANES), lambda b: (0, b)),
        pl.BlockSpec((2, None, C_BLOCK_DIM, C_BLOCK_DIM), lambda b: (0, b, 0, 0)),
        pl.BlockSpec((2, LANES), lambda b: (0, b)),
        pl.BlockSpec((2, LANES), lambda b: (0, b)),
    ]


def lru_fwd(proj, cw, cb, wa, ba, wx, bx, lam, side=None):
    T = proj.shape[0]

    def body(cx, cy, cw_r, cb_r, wa_r, ba_r, wx_r, bx_r, lam_r, o_ref):
        o = f_lru(cx[...], cy[...], cw_r[...], cb_r[...], wa_r[...], ba_r[...], wx_r[...], bx_r[...], lam_r[...], False)
        o_ref[...] = o.astype(o_ref.dtype)

    return _pcall(
        body, [proj, proj, cw, cb, wa, ba, wx, bx, lam], side, grid=(C_BLOCKS,), in_specs=_lru_specs(T),
        out_specs=pl.BlockSpec((T, LANES), lambda b: (0, b)), out_shape=jax.ShapeDtypeStruct((T, C_WIDTH), BF16),
        compiler_params=_cparams(("arbitrary",)), name="lru_fwd")[0]


def lru_bwd(proj, cw, cb, wa, ba, wx, bx, lam, dcat, side=None):
    T = proj.shape[0]
    oc = (A_Q + B_V) // LANES

    def body(cx, cy, cw_r, cb_r, wa_r, ba_r, wx_r, bx_r, lam_r, do_ref, *outs):
        _, vjp = jax.vjp(functools.partial(f_lru, diff=True), cx[...], cy[...], cw_r[...], cb_r[...], wa_r[...],
                         ba_r[...], wx_r[...], bx_r[...], lam_r[...])
        grads = vjp(do_ref[...])
        for o, g in zip(outs, grads):
            o[...] = g.astype(o.dtype)

    specs = _lru_specs(T)
    out_specs = [pl.BlockSpec((T, LANES), lambda b: (0, b)), pl.BlockSpec((T, LANES), lambda b: (0, b))] + specs[2:]
    out_shape = [jax.ShapeDtypeStruct((T, C_WIDTH), BF16), jax.ShapeDtypeStruct((T, C_WIDTH), BF16)] + \
                [jax.ShapeDtypeStruct(p.shape, F32) for p in (cw, cb, wa, ba, wx, bx, lam)]
    return _pcall(
        body, [proj, proj, cw, cb, wa, ba, wx, bx, lam, dcat], side, grid=(C_BLOCKS,),
        in_specs=specs + [pl.BlockSpec((T, LANES), lambda b: (0, oc + b))], out_specs=out_specs, out_shape=out_shape,
        compiler_params=_cparams(("arbitrary",)), name="lru_bwd")


def all_gather8(name, blk):
    def body(x_ref, out_ref, send_sems, recv_sems):
        x, y, c, _, others = _place()
        sibling = (x, y, 1 - c)

        def slab(px, py, pc):
            return out_ref.at[4 * px + 2 * py + pc]

        first = [_remote(x_ref, slab(x, y, c), send_sems, recv_sems, 0, sibling)]
        first += [_remote(x_ref, slab(x, y, c), send_sems, recv_sems, 1 + j, (*ch, c)) for j, ch in enumerate(others)]
        for cp in first:
            cp.start()
        passed = [_remote(slab(*ch, c), slab(*ch, c), send_sems, recv_sems, 4 + j, sibling) for j, ch in enumerate(others)]
        for j, ch in enumerate(others):
            _remote(x_ref, slab(*ch, c), send_sems, recv_sems, 1 + j, (x, y, c)).wait_recv()
            passed[j].start()
        _remote(x_ref, slab(x, y, 1 - c), send_sems, recv_sems, 0, (x, y, c)).wait_recv()
        for j, ch in enumerate(others):
            _remote(x_ref, slab(*ch, 1 - c), send_sems, recv_sems, 4 + j, (x, y, c)).wait_recv()
        for cp in first + passed:
            cp.wait_send()

    out = pl.pallas_call(
        body, out_shape=jax.ShapeDtypeStruct((8,) + blk.shape, blk.dtype), in_specs=[HBM_SPEC], out_specs=HBM_SPEC,
        scratch_shapes=[pltpu.SemaphoreType.DMA((7,)), pltpu.SemaphoreType.DMA((7,))], name=name)(blk)
    me = 4 * lax.axis_index("x") + 2 * lax.axis_index("y") + lax.axis_index("c")
    return lax.dynamic_update_index_in_dim(out, blk, me, 0)


def _exchange_call(name, body, arrays, out_shapes, n_sems):
    n = len(arrays)

    def kernel_body(*refs):
        body(refs[:n], refs[n:2 * n], refs[2 * n], refs[2 * n + 1])

    return pl.pallas_call(
        kernel_body, out_shape=out_shapes, in_specs=[HBM_SPEC] * n, out_specs=[HBM_SPEC] * n,
        scratch_shapes=[pltpu.SemaphoreType.DMA((n * n_sems,)), pltpu.SemaphoreType.DMA((n * n_sems,))], name=name)(*arrays)


def chip_gather(name, shards):
    def body(ins, outs, send_sems, recv_sems):
        x, y, c, chip, others = _place()
        sibling = (x, y, 1 - c)
        first, passed = [], []
        for a, (x_ref, out_ref) in enumerate(zip(ins, outs)):
            first += [_remote(x_ref.at[c], out_ref.at[chip, c], send_sems, recv_sems, 6 * a + j, (*ch, c))
                      for j, ch in enumerate(others)]
        for cp in first:
            cp.start()
        for a, (x_ref, out_ref) in enumerate(zip(ins, outs)):
            for j, ch in enumerate(others):
                here = out_ref.at[_chip_index(ch), c]
                _remote(x_ref.at[c], here, send_sems, recv_sems, 6 * a + j, (x, y, c)).wait_recv()
                cp = _remote(here, here, send_sems, recv_sems, 6 * a + 3 + j, sibling)
                cp.start()
                passed.append(cp)
        for a, (x_ref, out_ref) in enumerate(zip(ins, outs)):
            for j, ch in enumerate(others):
                _remote(x_ref.at[c], out_ref.at[_chip_index(ch), 1 - c], send_sems, recv_sems, 6 * a + 3 + j, (x, y, c)).wait_recv()
        for cp in first + passed:
            cp.wait_send()

    outs = _exchange_call(name, body, shards, [jax.ShapeDtypeStruct((4,) + s.shape, s.dtype) for s in shards], 6)
    chip = 2 * lax.axis_index("x") + lax.axis_index("y")
    return [lax.dynamic_update_index_in_dim(o, s, chip, 0) for o, s in zip(outs, shards)]


def chip_scatter(name, parts):
    def body(ins, outs, send_sems, recv_sems):
        x, y, c, chip, others = _place()
        sends = []
        for a, (x_ref, out_ref) in enumerate(zip(ins, outs)):
            sends += [_remote(x_ref.at[_chip_index(ch)], out_ref.at[chip], send_sems, recv_sems, 3 * a + j, (*ch, c))
                      for j, ch in enumerate(others)]
        for cp in sends:
            cp.start()
        for a, (x_ref, out_ref) in enumerate(zip(ins, outs)):
            for j, ch in enumerate(others):
                _remote(x_ref.at[chip], out_ref.at[_chip_index(ch)], send_sems, recv_sems, 3 * a + j, (x, y, c)).wait_recv()
        for cp in sends:
            cp.wait_send()

    return _exchange_call(name, body, parts, [jax.ShapeDtypeStruct(p.shape, p.dtype) for p in parts], 3)


def sibling_take(name, halves):
    def body(ins, outs, send_sems, recv_sems):
        x, y, c, _, _ = _place()
        cps = [_remote(x_ref.at[s, 1 - c], out_ref.at[s], send_sems, recv_sems, 4 * a + s, (x, y, 1 - c))
               for a, (x_ref, out_ref) in enumerate(zip(ins, outs)) for s in range(4)]
        for cp in cps:
            cp.start()
        for cp in cps:
            cp.wait()

    return _exchange_call(name, body, halves,
                          [jax.ShapeDtypeStruct((h.shape[0],) + h.shape[2:], h.dtype) for h in halves], 4)


def sibling_pair(name, mine):
    def body(ins, outs, send_sems, recv_sems):
        x, y, c, _, _ = _place()
        cps = [_remote(x_ref, out_ref.at[c], send_sems, recv_sems, a, (x, y, 1 - c))
               for a, (x_ref, out_ref) in enumerate(zip(ins, outs))]
        for cp in cps:
            cp.start()
        for a, (x_ref, out_ref) in enumerate(zip(ins, outs)):
            _remote(x_ref, out_ref.at[1 - c], send_sems, recv_sems, a, (x, y, c)).wait_recv()
        for cp in cps:
            cp.wait_send()

    outs = _exchange_call(name, body, mine, [jax.ShapeDtypeStruct((2,) + m.shape, m.dtype) for m in mine], 1)
    core = lax.axis_index("c")
    return [lax.dynamic_update_index_in_dim(o, m, core, 0) for o, m in zip(outs, mine)]


def sum_slabs(name, r, out_dtype, tr):
    S, R, W = r.shape

    def body(*refs):
        t = refs[0][...].astype(F32)
        for s in range(1, S):
            t = t + refs[s][...].astype(F32)
        refs[S][...] = t.astype(out_dtype)

    return pl.pallas_call(
        body, grid=(R // tr,), in_specs=[pl.BlockSpec((None, tr, W), lambda i, s=s: (s, i, 0)) for s in range(S)],
        out_specs=pl.BlockSpec((tr, W), lambda i: (i, 0)), out_shape=jax.ShapeDtypeStruct((R, W), out_dtype),
        compiler_params=_cparams(("parallel",)), name=name)(*([r] * S))


def sum_chips(name, arrived, own, chip, tr):
    S, R, W = arrived.shape

    def body(chip_ref, own_ref, *refs):
        me = chip_ref[0]
        t = None
        for s in range(S):
            term = jnp.where(me == s, own_ref[...].astype(F32), refs[s][...].astype(F32))
            t = term if t is None else t + term
        refs[S][...] = t

    grid_spec = pltpu.PrefetchScalarGridSpec(
        num_scalar_prefetch=1, grid=(R // tr,),
        in_specs=[pl.BlockSpec((None, tr, W), lambda i, ch: (ch[0], i, 0))] +
                 [pl.BlockSpec((None, tr, W), lambda i, ch, s=s: (s, i, 0)) for s in range(S)],
        out_specs=pl.BlockSpec((tr, W), lambda i, ch: (i, 0)))
    return pl.pallas_call(body, grid_spec=grid_spec, out_shape=jax.ShapeDtypeStruct((R, W), F32),
                          compiler_params=_cparams(("parallel",)), name=name)(
                              chip.reshape(1).astype(jnp.int32), own, *([arrived] * S))


def add_kept_half(name, halves, got, c, tr):
    S, _, R, W = halves.shape

    def body(c_ref, h_ref, g_ref, o_ref):
        o_ref[...] = (h_ref[...].astype(F32) + g_ref[...].astype(F32)).astype(o_ref.dtype)

    grid_spec = pltpu.PrefetchScalarGridSpec(
        num_scalar_prefetch=1, grid=(S, R // tr),
        in_specs=[pl.BlockSpec((None, None, tr, W), lambda s, i, c_ref: (s, c_ref[0], i, 0)),
                  pl.BlockSpec((None, tr, W), lambda s, i, c_ref: (s, i, 0))],
        out_specs=pl.BlockSpec((None, tr, W), lambda s, i, c_ref: (s, i, 0)))
    return pl.pallas_call(body, grid_spec=grid_spec, out_shape=jax.ShapeDtypeStruct((S, R, W), halves.dtype),
                          compiler_params=_cparams(("parallel", "parallel")), name=name)(
                              c.reshape(1).astype(jnp.int32), halves, got)


def adamw_layer(name, l, g, row_off, w, m, v, prev, tr, side=None):
    L, R, C = w.shape
    off = row_off // tr

    def body(g_ref, w_ref, m_ref, v_ref, *rest):
        outs = rest[-4:]
        gv = g_ref[...]
        d, mn, vn = f_adamw(w_ref[...], gv, m_ref[...], v_ref[...])
        for o, val in zip(outs, (gv, d, mn, vn)):
            o[...] = val

    slab = pl.BlockSpec((None, tr, C), lambda i: (l, i, 0))
    in_specs = [pl.BlockSpec((tr, C), lambda i: (off + i, 0)), slab, slab, slab]
    args = [g, w, m, v]
    aliases = {}
    if prev is not None:
        in_specs += [pl.BlockSpec(memory_space=pl.ANY)] * 4
        args += list(prev)
        aliases = {4 + k: k for k in range(4)}
    return _pcall(
        body, args, side, grid=(R // tr,), in_specs=in_specs, out_specs=[slab] * 4,
        out_shape=[jax.ShapeDtypeStruct((L, R, C), F32)] * 4, input_output_aliases=aliases,
        compiler_params=_cparams(("parallel",)), name=name)


BIG = ("w_in", "w_out", "xq", "xk", "xv", "xo", "w_up", "w_down")
GROUPS = {"mix": ("w_out", "xq", "xk", "xv", "xo"), "ff": ("w_up", "w_down")}
PACK_ROWS = {"w_out": 512, "xq": 512, "xk": 512, "xv": 512, "xo": 512, "w_up": 2048, "w_down": 2048}
GROUP_ROWS = {g: sum(PACK_ROWS[n] for n in names) for g, names in GROUPS.items()}
SUM_TILE = 256
PACK_OFF = {}
for _names in GROUPS.values():
    _o = 0
    for _n in _names:
        PACK_OFF[_n] = _o
        _o += PACK_ROWS[_n]

_SPLIT_OFF = np.cumsum((0,) + SPLIT_SIZES)
_KORDER = (0, 1, 2, 3, 4, 5, 6, 9, 10, 7, 8)


def w_in_to_kernel_cols(w):
    parts = [w[..., _SPLIT_OFF[i]:_SPLIT_OFF[i + 1]] for i in _KORDER]
    parts.append(jnp.zeros(w.shape[:-1] + (D_INP - D_IN,), w.dtype))
    return jnp.concatenate(parts, axis=-1)


def w_in_from_kernel_cols(w):
    offs = np.cumsum((0,) + tuple(SPLIT_SIZES[i] for i in _KORDER))
    pos = {k: (offs[n], offs[n + 1]) for n, k in enumerate(_KORDER)}
    return jnp.concatenate([w[..., pos[i][0]:pos[i][1]] for i in range(len(SPLIT_SIZES))], axis=-1)


def pack_shards(shards, group, dtype):
    return jnp.concatenate([shards[n].astype(dtype) for n in GROUPS[group]], axis=-2)


def unpack_rows(packed, name):
    return packed[..., PACK_OFF[name]:PACK_OFF[name] + PACK_ROWS[name], :]


WEIGHTS = ("rel_bias", "w_in", "w_out", "attn_sink", "gla_w2_f", "gla_b2_f", "gla_w2_b", "gla_b2_b", "gla_norm", "conv_w",
           "conv_b", "lru_wa", "lru_ba", "lru_wx", "lru_bx", "lru_lambda", "xq", "xk", "xv", "xo", "w_up", "w_down",
           "norm_mix_pre", "norm_mix_post", "norm_mem", "norm_x_pre", "norm_x_post", "norm_ff_pre", "norm_ff_post")
SMALL = tuple(n for n in WEIGHTS if n not in BIG)
SMALL_SHARDED = ("gla_w2_f", "gla_w2_b", "conv_w", "lru_ba", "lru_bx", "lru_lambda")
ROW_TILE = 256
SMALL_TILE = 512
RIDE_PIECE_ROWS = (256, 512)


def _small_rows(shape):
    return -(-int(np.prod(shape)) // (SUBLANES * LANES)) * SUBLANES


def _pack_small(arrs):
    parts, total = [], 0
    for a in arrs:
        n, rows = int(np.prod(a.shape)), _small_rows(a.shape)
        a = a.astype(F32)
        if n == rows * LANES:
            parts.append(a.reshape(rows, LANES))
        else:
            parts.append(jnp.pad(a.reshape(-1), (0, rows * LANES - n)).reshape(rows, LANES))
        total += rows
    pad = -total % SMALL_TILE
    if pad:
        parts.append(jnp.zeros((pad, LANES), F32))
    return jnp.concatenate(parts, axis=0)


def _unpack_small(buf, shapes):
    lead = buf.shape[:-2]
    out, o = [], 0
    for s in shapes:
        n, rows = int(np.prod(s)), _small_rows(s)
        part = buf[..., o:o + rows, :]
        if n != rows * LANES:
            part = part.reshape(lead + (-1,))[..., :n]
        out.append(part.reshape(lead + tuple(s)))
        o += rows
    return out


def _relu2(r):
    return r, jnp.square(jnp.maximum(r, 0.0))


def _drelu2(r, u):
    return r * (2.0 * jnp.maximum(u, 0.0))


def kernel(x, mem, rel_bias, w_in, w_out, attn_sink, gla_w2_f, gla_b2_f, gla_w2_b, gla_b2_b, gla_norm, conv_w, conv_b, lru_wa, lru_ba, lru_wx, lru_bx, lru_lambda, xq, xk, xv, xo, w_up, w_down, norm_mix_pre, norm_mix_post, norm_mem, norm_x_pre, norm_x_post, norm_ff_pre, norm_ff_post, loss_target, m_rel_bias, m_w_in, m_w_out, m_attn_sink, m_gla_w2_f, m_gla_b2_f, m_gla_w2_b, m_gla_b2_b, m_gla_norm, m_conv_w, m_conv_b, m_lru_wa, m_lru_ba, m_lru_wx, m_lru_bx, m_lru_lambda, m_xq, m_xk, m_xv, m_xo, m_w_up, m_w_down, m_norm_mix_pre, m_norm_mix_post, m_norm_mem, m_norm_x_pre, m_norm_x_post, m_norm_ff_pre, m_norm_ff_post, v_rel_bias, v_w_in, v_w_out, v_attn_sink, v_gla_w2_f, v_gla_b2_f, v_gla_w2_b, v_gla_b2_b, v_gla_norm, v_conv_w, v_conv_b, v_lru_wa, v_lru_ba, v_lru_wx, v_lru_bx, v_lru_lambda, v_xq, v_xk, v_xv, v_xo, v_w_up, v_w_down, v_norm_mix_pre, v_norm_mix_post, v_norm_mem, v_norm_x_pre, v_norm_x_post, v_norm_ff_pre, v_norm_ff_post):
    w_args = (rel_bias, w_in, w_out, attn_sink, gla_w2_f, gla_b2_f, gla_w2_b, gla_b2_b, gla_norm, conv_w, conv_b, lru_wa,
              lru_ba, lru_wx, lru_bx, lru_lambda, xq, xk, xv, xo, w_up, w_down, norm_mix_pre, norm_mix_post, norm_mem,
              norm_x_pre, norm_x_post, norm_ff_pre, norm_ff_post)
    m_args = (m_rel_bias, m_w_in, m_w_out, m_attn_sink, m_gla_w2_f, m_gla_b2_f, m_gla_w2_b, m_gla_b2_b, m_gla_norm, m_conv_w,
              m_conv_b, m_lru_wa, m_lru_ba, m_lru_wx, m_lru_bx, m_lru_lambda, m_xq, m_xk, m_xv, m_xo, m_w_up, m_w_down,
              m_norm_mix_pre, m_norm_mix_post, m_norm_mem, m_norm_x_pre, m_norm_x_post, m_norm_ff_pre, m_norm_ff_post)
    v_args = (v_rel_bias, v_w_in, v_w_out, v_attn_sink, v_gla_w2_f, v_gla_b2_f, v_gla_w2_b, v_gla_b2_b, v_gla_norm, v_conv_w,
              v_conv_b, v_lru_wa, v_lru_ba, v_lru_wx, v_lru_bx, v_lru_lambda, v_xq, v_xk, v_xv, v_xo, v_w_up, v_w_down,
              v_norm_mix_pre, v_norm_mix_post, v_norm_mem, v_norm_x_pre, v_norm_x_post, v_norm_ff_pre, v_norm_ff_post)
    Wt, Mo, Vo = dict(zip(WEIGHTS, w_args)), dict(zip(WEIGHTS, m_args)), dict(zip(WEIGHTS, v_args))
    x, mem, tgt = x[0], mem[0], loss_target[0]
    D = D_MODEL
    depth = w_in.shape[0]
    chip = 2 * lax.axis_index("x") + lax.axis_index("y")
    core = lax.axis_index("c")

    sm_shapes = [Wt[n].shape for n in SMALL_SHARDED]
    g8 = all_gather8("gather_small_weights", _pack_small([Wt[n] for n in SMALL_SHARDED]))
    per_chip = _unpack_small(g8[0::2], sm_shapes)
    whole = {n: jnp.concatenate([p[j] for j in range(4)], axis=-1) for n, p in zip(SMALL_SHARDED, per_chip)}

    def group_shards(l, group):
        shard = pack_shards({n: Wt[n][l] for n in GROUPS[group]}, group, BF16).reshape(2, GROUP_ROWS[group] // 2, D)
        return [shard] + ([w_in[l].astype(BF16).reshape(2, D // 2, D_IN // 4)] if group == "mix" else [])

    def whole_weights(group, gathered):
        g = gathered[0].reshape(4, GROUP_ROWS[group], D)
        if group == "ff":
            return {"w_up": unpack_rows(g, "w_up"), "w_down": unpack_rows(g, "w_down").reshape(D_FF, D)}
        W = {n: unpack_rows(g, n).reshape(D, D) for n in GROUPS["mix"]}
        win = gathered[1].reshape(4, D, D_IN // 4).transpose(1, 0, 2).reshape(D, D_IN)
        W["w_in"] = w_in_to_kernel_cols(win)
        return W

    first = group_shards(0, "mix")
    Wgot = {(0, "mix"): whole_weights("mix", chip_gather("gather_first_weights", first))}
    gathers = Stream()
    riding = {}
    for l in range(depth):
        for group in ("mix", "ff"):
            if (l, group) != (0, "mix"):
                shards = group_shards(l, group)
                riding[l, group] = (shards, gathers.add(RidingExchange("gather", shards, RIDE_PIECE_ROWS)))

    def need_weights(l, group):
        if (l, group) not in Wgot:
            shards, exchange = riding.pop((l, group))
            got = gathers.finish(exchange, "gather_rest")
            Wgot[l, group] = whole_weights(
                group, [lax.dynamic_update_index_in_dim(b, s, chip, 0) for b, s in zip(got, shards)])
        return Wgot[l, group]

    bucket = t5_bucket_map()
    bias = bias_table_fwd(rel_bias, bucket)

    def gain(name, l):
        return Wt[name][l][None]

    def layer_params(l):
        w2fp = jnp.zeros((LANES, B_QK), F32).at[0:GATE_RANK].set(whole["gla_w2_f"][l])
        w2bp = jnp.zeros((LANES, B_QK), F32).at[GATE_RANK:2 * GATE_RANK].set(whole["gla_w2_b"][l])
        gate = [w2fp, gla_b2_f[l][None], w2bp, gla_b2_b[l][None]]
        lru = [whole["conv_w"][l], conv_b[l][None], lru_wa[l], whole["lru_ba"][l], lru_wx[l], whole["lru_bx"][l],
               whole["lru_lambda"][l]]
        return attn_sink[l].reshape(A_KV_HEADS, A_GROUP, 1), gate, gla_norm[l][None], lru

    saved = []
    xcur = x
    (h1,) = rowmap("norm_first", f_norm, [x], [gain("norm_mix_pre", 0)], [(D, BF16)], ROW_TILE)
    loss_acc = None
    Wfull = []
    ride = gathers.take
    for l in range(depth):
        W = dict(need_weights(l, "mix"))
        sink3, gate, gn, lru = layer_params(l)
        (proj,) = mm("mm_in", h1, W["w_in"], "nn", [(F32, 1)], pm=512, pn=1408, side=ride(2))
        oa = attn_fwd(proj, bias, sink3, side=ride(3))
        zrow, grow = (proj, LANES, OFF_Z // LANES), (proj, B_V, OFF_BG // B_V)
        laf, lab = rowmap("gla_gate", f_gla_gate, [zrow], gate, [(B_QK, F32), (B_QK, F32)], ROW_TILE)
        oraw = gla_fwd(proj, laf, lab, side=ride(2))
        (ob,) = rowmap("gla_post", f_gla_post, [oraw, grow], [gn], [(B_V, BF16)], ROW_TILE)
        oc = lru_fwd(proj, *lru, side=ride(1))
        cat = jnp.concatenate([oa, ob, oc], axis=1)
        (mixed,) = mm("mm_out", cat, W["w_out"], "nn", [(F32, 1)], side=ride(1))
        x1, h2 = rowmap("resnorm_mix", f_resnorm, [xcur, mixed], [gain("norm_mix_post", l), gain("norm_x_pre", l)],
                        [(D, F32), (D, BF16)], ROW_TILE)
        (memn,) = rowmap("norm_mem", f_norm, [mem], [gain("norm_mem", l)], [(D, BF16)], ROW_TILE)
        (q,) = mm("mm_xq", h2, W["xq"], "nn", [(BF16, 1)], side=ride(1))
        (k,) = mm("mm_xk", memn, W["xk"], "nn", [(F32, 1)])
        (v,) = mm("mm_xv", memn, W["xv"], "nn", [(F32, 1)])
        (o,) = rowmap("xattn", f_xattn, [q], [k, v], [(D, BF16)], ROW_TILE)
        (xo_out,) = mm("mm_xo", o, W["xo"], "nn", [(F32, 1)], side=ride(1))
        x2, h3 = rowmap("resnorm_x", f_resnorm, [x1, xo_out], [gain("norm_x_post", l), gain("norm_ff_pre", l)],
                        [(D, F32), (D, BF16)], ROW_TILE)
        W.update(need_weights(l, "ff"))
        Wfull.append(W)
        u, act = mm("mm_up", h3, W["w_up"], "nn", [(F32, 1), (BF16, 1)], epilogue=_relu2, side=ride(3))
        (ff,) = mm("mm_down", act, W["w_down"], "nn", [(F32, 1)], side=ride(2))
        saved.append(dict(x0=xcur, h1=h1, proj=proj, laf=laf, lab=lab, oraw=oraw, cat=cat, mixed=mixed, x1=x1, h2=h2,
                          memn=memn, q=q, k=k, v=v, o=o, xo_out=xo_out, x2=x2, h3=h3, u=u, act=act, ff=ff))
        if l < depth - 1:
            xcur, h1 = rowmap("resnorm_ff", f_resnorm, [x2, ff], [gain("norm_ff_post", l), gain("norm_mix_pre", l + 1)],
                              [(D, F32), (D, BF16)], ROW_TILE)
        else:
            (loss_acc,) = rowmap("final_loss", f_final_loss, [x2, ff, tgt], [gain("norm_ff_post", l)], [], ROW_TILE,
                                 accs=[(1, 1)])
    loss = lax.psum(loss_acc[0, 0], ("x", "y", "c"))

    small_g = {n: [None] * depth for n in SMALL if n != "rel_bias"}
    adam = {}
    dbias_all = []
    dx_next = dh1_next = None
    grad_x = None
    scatters = Stream()
    ride = scatters.take
    inflight = []
    updates = []

    def start_reduce(lyr, group, dW):
        if group == "ff":
            shards = {"w_up": dW["w_up"], "w_down": dW["w_down"].reshape(4, D_FF // 4, D)}
        else:
            shards = {n: dW[n].reshape(4, D // 4, D) for n in GROUPS["mix"]}
        halves = [pack_shards(shards, group, BF16).reshape(4, 2, GROUP_ROWS[group] // 2, D)]
        if group == "mix":
            g_in = w_in_from_kernel_cols(dW["w_in"]).reshape(D, 4, D_IN // 4).transpose(1, 0, 2)
            halves.append(g_in.reshape(4, 2, D // 2, D_IN // 4))
        got = sibling_take("reduce_to_half_owner", halves)
        sums = [add_kept_half("reduce_chip_sum", h, g, core, SUM_TILE) for h, g in zip(halves, got)]
        inflight.append((lyr, group, sums, scatters.add(RidingExchange("scatter", sums, RIDE_PIECE_ROWS))))

    def finish_reduce():
        lyr, group, sums, exchange = inflight.pop(0)
        arrived = scatters.finish(exchange, "reduce_rest")
        totals = [sum_chips("reduce_sum_chips", a, s, chip, SUM_TILE) for a, s in zip(arrived, sums)]
        fulls = sibling_pair("reduce_share_halves", totals)
        full = fulls[0].reshape(GROUP_ROWS[group], D)
        updates.extend((n, lyr, full, PACK_OFF[n]) for n in GROUPS[group])
        if group == "mix":
            updates.append(("w_in", lyr, fulls[1].reshape(D, D_IN // 4), 0))

    def run_updates(carry):
        updates.sort(key=lambda u: -PACK_ROWS.get(u[0], 1024))
        while updates:
            n, lyr, g, off = updates.pop(0)
            adam[n] = adamw_layer("adamw_" + n, lyr, g, off, Wt[n], Mo[n], Vo[n], adam.get(n), SUM_TILE,
                                  side=ride(1) if carry else None)

    for l in reversed(range(depth)):
        W, S = Wfull[l], saved[l]
        sink3, gate, gn, lru = layer_params(l)
        if l == depth - 1:
            (dx2, dff), (dgp,) = rowmap_bwd("final_bwd", f_final_rows, [S["x2"], S["ff"], tgt], [gain("norm_ff_post", l)],
                                            [None], ROW_TILE, [F32, F32, None], [True])
        else:
            (dx2, dff), (dgp, dgn_next) = rowmap_bwd(
                "resnorm_ff_bwd", f_resnorm, [S["x2"], S["ff"]], [gain("norm_ff_post", l), gain("norm_mix_pre", l + 1)],
                [dx_next, dh1_next], ROW_TILE, [F32, F32], [True, True])
            small_g["norm_mix_pre"][l + 1] = dgn_next[0]
        small_g["norm_ff_post"][l] = dgp[0]
        dW = {}
        (du,) = mm("mm_down_bwd", dff, W["w_down"], "nt", [(BF16, 1)], epilogue=_drelu2, extras=[S["u"]], side=ride(3))
        (dW["w_down"],) = mm("mm_down_wgrad", S["act"], dff, "tn", [(BF16, 1)], side=ride(2))
        (dW["w_up"],) = mm("mm_up_wgrad", S["h3"], du, "tn", [(BF16, 4)], side=ride(2))
        if inflight:
            finish_reduce()
        start_reduce(l, "ff", dW)
        (dh3,) = mm("mm_up_bwd", du, W["w_up"], "nt", [(F32, 1)], pk=D, side=ride(2))
        (dx1, dxo_out), (dg1, dg2) = rowmap_bwd(
            "resnorm_x_bwd", f_resnorm, [S["x1"], S["xo_out"]], [gain("norm_x_post", l), gain("norm_ff_pre", l)],
            [dx2, dh3], ROW_TILE, [F32, F32], [True, True])
        small_g["norm_x_post"][l], small_g["norm_ff_pre"][l] = dg1[0], dg2[0]
        (do,) = mm("mm_xo_bwd", dxo_out, W["xo"], "nt", [(F32, 1)])
        (dW["xo"],) = mm("mm_xo_wgrad", S["o"], dxo_out, "tn", [(BF16, 1)])
        (dq,), (dk, dv) = rowmap_bwd("xattn_bwd", f_xattn, [S["q"]], [S["k"], S["v"]], [do], ROW_TILE, [BF16], [True, True])
        (dW["xq"],) = mm("mm_xq_wgrad", S["h2"], dq, "tn", [(BF16, 1)])
        (dh2,) = mm("mm_xq_bwd", dq, W["xq"], "nt", [(F32, 1)])
        (dW["xk"],) = mm("mm_xk_wgrad", S["memn"], dk, "tn", [(BF16, 1)])
        (dW["xv"],) = mm("mm_xv_wgrad", S["memn"], dv, "tn", [(BF16, 1)])
        (dmk,) = mm("mm_xk_bwd", dk, W["xk"], "nt", [(F32, 1)])
        (dmv,) = mm("mm_xv_bwd", dv, W["xv"], "nt", [(F32, 1)])
        _, (dgm,) = rowmap_bwd("norm_mem_bwd", f_norm_twice, [mem], [gain("norm_mem", l)], [dmk, dmv], ROW_TILE, [None], [True])
        small_g["norm_mem"][l] = dgm[0]
        (dx0, dmixed), (dg1, dg2) = rowmap_bwd(
            "resnorm_mix_bwd", f_resnorm, [S["x0"], S["mixed"]], [gain("norm_mix_post", l), gain("norm_x_pre", l)],
            [dx1, dh2], ROW_TILE, [F32, F32], [True, True])
        small_g["norm_mix_post"][l], small_g["norm_x_pre"][l] = dg1[0], dg2[0]
        (dcat,) = mm("mm_out_bwd", dmixed, W["w_out"], "nt", [(F32, 1)])
        (dW["w_out"],) = mm("mm_out_wgrad", S["cat"], dmixed, "tn", [(BF16, 1)])
        proj = S["proj"]
        daq, dak, dav, dbias, dsink = attn_bwd(proj, bias, sink3, dcat, side=ride(3))
        dbias_all.append(dbias)
        small_g["attn_sink"][l] = dsink.reshape(A_HEADS)
        zrow, grow = (proj, LANES, OFF_Z // LANES), (proj, B_V, OFF_BG // B_V)
        (doraw, dbg), (dgn,) = rowmap_bwd("gla_post_bwd", f_gla_post, [S["oraw"], grow], [gn], [(dcat, B_V, A_Q // B_V)],
                                          ROW_TILE, [F32, BF16], [True])
        dbq, dbk, dbv, dlaf, dlab = gla_bwd(proj, S["laf"], S["lab"], doraw, side=ride(3))
        (dz,), (dw2fp, db2f, dw2bp, db2b) = rowmap_bwd("gla_gate_bwd", f_gla_gate, [zrow], gate, [dlaf, dlab], ROW_TILE,
                                                        [BF16], [True] * 4)
        small_g["gla_norm"][l] = dgn[0]
        small_g["gla_w2_f"][l], small_g["gla_b2_f"][l] = dw2fp[0:GATE_RANK], db2f[0]
        small_g["gla_w2_b"][l], small_g["gla_b2_b"][l] = dw2bp[GATE_RANK:2 * GATE_RANK], db2b[0]
        dcx, dcy, dcw, dcb, dwa, dba, dwx, dbx, dlam = lru_bwd(proj, *lru, dcat, side=ride(2))
        small_g["conv_w"][l], small_g["conv_b"][l] = dcw, dcb[0]
        small_g["lru_wa"][l], small_g["lru_ba"][l], small_g["lru_wx"][l] = dwa, dba, dwx
        small_g["lru_bx"][l], small_g["lru_lambda"][l] = dbx, dlam
        dproj = jnp.concatenate([daq, dak, dav, dbq, dbk, dbv, dbg, dcx, dcy, dz], axis=1)
        (dW["w_in"],) = mm("mm_in_wgrad", S["h1"], dproj, "tn", [(BF16, 1)], pm=512, pn=1408, side=ride(1))
        (dh1,) = mm("mm_in_bwd", dproj, W["w_in"], "nt", [(F32, 1)], side=ride(1))
        if l > 0:
            dx_next, dh1_next = dx0, dh1
        else:
            (grad_x,), (dg0,) = rowmap_bwd("norm_first_bwd", f_norm_keep, [x], [gain("norm_mix_pre", 0)], [dx0, dh1],
                                           ROW_TILE, [F32], [True])
            small_g["norm_mix_pre"][0] = dg0[0]

        finish_reduce()
        start_reduce(l, "mix", dW)
    run_updates(carry=True)
    finish_reduce()
    run_updates(carry=False)

    (dtab,) = (bias_table_bwd(dbias_all, bucket),)
    sg = {n: jnp.stack(small_g[n]) for n in small_g}
    sg["rel_bias"] = dtab[:, :A_HEADS]
    sg_shapes = [sg[n].shape for n in SMALL]
    contributions = all_gather8("gather_small_grads", _pack_small([sg[n] for n in SMALL]))
    (sg_sum,) = (sum_slabs("sum_small_grads", contributions, F32, SMALL_TILE),)
    sg = dict(zip(SMALL, _unpack_small(sg_sum, sg_shapes)))
    for n in SMALL_SHARDED:
        w = Wt[n].shape[-1]
        sg[n] = lax.dynamic_slice_in_dim(sg[n], chip * w, w, axis=sg[n].ndim - 1)

    grads, delta, new_m, new_v = {}, {}, {}, {}
    for n in BIG:
        grads[n], delta[n], new_m[n], new_v[n] = adam[n]
    shapes = [Wt[n].shape for n in SMALL]
    packs = [_pack_small([src[n] for n in SMALL]) for src in (Wt, sg, Mo, Vo)]
    d_, m_, v_ = rowmap("adamw_small", f_adamw, packs, [], [(LANES, F32)] * 3, SMALL_TILE)
    for n, a, b, c_ in zip(SMALL, _unpack_small(d_, shapes), _unpack_small(m_, shapes), _unpack_small(v_, shapes)):
        grads[n], delta[n], new_m[n], new_v[n] = sg[n], a, b, c_

    return (loss, grad_x[None], *[grads[n] for n in WEIGHTS], *[delta[n] for n in WEIGHTS],
            *[new_m[n] for n in WEIGHTS], *[new_v[n] for n in WEIGHTS])
```

```python
import functools
import math

import numpy as np
import jax
import jax.numpy as jnp
from jax import lax
from jax.experimental import pallas as pl
from jax.experimental.pallas import tpu as pltpu

F32, BF16 = jnp.float32, jnp.bfloat16
HI = lax.Precision.HIGHEST
MESH = pl.DeviceIdType.MESH

VMEM_LIMIT_BYTES = 56 * 1024 * 1024
LANES = 128
SUBLANES = 8

D_MODEL = 2048
DEPTH = 4
A_HEAD_DIM = 128
A_HEADS = 8
A_KV_HEADS = 2
A_GROUP = 4
WINDOW = 128
BLOCK = 128
N_BUCKETS = 32
MAX_DISTANCE = 128
B_HEADS = 4
B_KEY_DIM = 64
B_VAL_DIM = 128
GATE_RANK = 16
GATE_TAU = 16.0
C_WIDTH = 512
C_BLOCKS = 4
C_BLOCK_DIM = 128
CONV_WIDTH = 4
CONV_LEFT = 2
LRU_C = 8.0
X_HEADS = 4
X_HEAD_DIM = 512
D_FF = 4 * D_MODEL
EPS = 1e-6
NEG_INF = -1e30
A_Q, A_KV, B_QK, B_V = 1024, 256, 256, 512
SPLIT_SIZES = (A_Q, A_KV, A_KV, B_QK, B_QK, B_V, B_V, GATE_RANK, GATE_RANK, C_WIDTH, C_WIDTH)
D_IN = sum(SPLIT_SIZES)
D_INP = 4224
OFF_AQ, OFF_AK, OFF_AV, OFF_BQ, OFF_BK, OFF_BV, OFF_BG, OFF_CX, OFF_CY, OFF_Z = (
    0, 1024, 1280, 1536, 1792, 2048, 2560, 3072, 3584, 4096)
GLA_CHUNK = 128

ADAM_LR, ADAM_B1, ADAM_B2, ADAM_EPS, ADAM_WD, ADAM_STEP = 0.001, 0.9, 0.999, 1e-08, 0.01, 10


def _cparams(sem=None):
    return pltpu.CompilerParams(dimension_semantics=sem, vmem_limit_bytes=VMEM_LIMIT_BYTES)


def _full_spec(a):
    nd = a.ndim
    return pl.BlockSpec(a.shape, lambda *_: (0,) * nd)


def _tup(r):
    return r if isinstance(r, tuple) else (r,)


HBM_SPEC = pl.BlockSpec(memory_space=pltpu.HBM)


def _place():
    x, y, c = lax.axis_index("x"), lax.axis_index("y"), lax.axis_index("c")
    others = [(1 - x, y), (x, 1 - y), (1 - x, 1 - y)]
    return x, y, c, 2 * x + y, others


def _remote(src, dst, send_sems, recv_sems, k, to):
    return pltpu.make_async_remote_copy(src_ref=src, dst_ref=dst, send_sem=send_sems.at[k], recv_sem=recv_sems.at[k],
                                        device_id=to, device_id_type=MESH)


def _chip_index(ch):
    return 2 * ch[0] + ch[1]


def _pcall(body, args, side=None, **kw):
    if side is None:
        res = pl.pallas_call(body, **kw)(*args)
        return list(res) if isinstance(res, (list, tuple)) else [res]
    single = not isinstance(kw["out_shape"], (list, tuple))
    out_shape = [kw.pop("out_shape")] if single else list(kw.pop("out_shape"))
    out_specs = [kw.pop("out_specs")] if single else list(kw.pop("out_specs"))
    in_specs = list(kw.pop("in_specs"))
    scratch = list(kw.pop("scratch_shapes", ()))
    grid = kw.get("grid", ())
    n_in, n_out, n_scr = len(in_specs), len(out_shape), len(scratch)
    srcs, bufs = side.srcs, side.bufs
    ns, nb = len(srcs), len(bufs)

    def wrapped(*refs):
        ins = refs[:n_in]
        src_refs = refs[n_in:n_in + ns]
        o0 = n_in + ns + nb
        outs = refs[o0:o0 + n_out]
        buf_refs = refs[o0 + n_out:o0 + n_out + nb]
        s0 = o0 + n_out + nb
        scr = refs[s0:s0 + n_scr]
        send_sems, recv_sems = refs[s0 + n_scr], refs[s0 + n_scr + 1]
        first = last = None
        for d, n in enumerate(grid):
            f, l_ = pl.program_id(d) == 0, pl.program_id(d) == n - 1
            first = f if first is None else first & f
            last = l_ if last is None else last & l_
        if first is None:
            side.start(src_refs, buf_refs, send_sems, recv_sems)
            body(*ins, *outs, *scr)
            side.finish(src_refs, buf_refs, send_sems, recv_sems)
            return
        pl.when(first)(lambda: side.start(src_refs, buf_refs, send_sems, recv_sems))
        body(*ins, *outs, *scr)
        pl.when(last)(lambda: side.finish(src_refs, buf_refs, send_sems, recv_sems))

    any_spec = pl.BlockSpec(memory_space=pl.ANY)
    aliases = dict(kw.pop("input_output_aliases", {}))
    aliases.update({n_in + ns + i: n_out + i for i in range(nb)})
    cp = kw.pop("compiler_params", None)
    if grid:
        cp = _cparams(("arbitrary",) * len(grid))
    res = pl.pallas_call(
        wrapped, in_specs=in_specs + [any_spec] * (ns + nb), out_specs=out_specs + [any_spec] * nb,
        out_shape=out_shape + [jax.ShapeDtypeStruct(b.shape, b.dtype) for b in bufs],
        scratch_shapes=scratch + [pltpu.SemaphoreType.DMA((side.n_sems,)), pltpu.SemaphoreType.DMA((side.n_sems,))],
        input_output_aliases=aliases, compiler_params=cp, **kw)(*args, *srcs, *bufs)
    side.done(list(res[n_out:]))
    return list(res[:n_out])


class _Side:
    def __init__(self, parts):
        self.parts = parts
        self.srcs = [s for p in parts for s in p[0].srcs]
        self.bufs = [b for p in parts for b in p[0].bufs]
        self.n_sems = max(1, sum(3 * len(now) + 4 * len(relay) + len(last) for _, now, relay, last in parts))

    def done(self, bufs):
        for p in self.parts:
            p[0].bufs, bufs = bufs[:len(p[0].bufs)], bufs[len(p[0].bufs):]

    def _copies(self, src_refs, buf_refs, send_sems, recv_sems):
        x, y, c, chip, others = _place()
        xn, yn, dg = others
        me, sibling = (x, y, c), (x, y, 1 - c)
        mine, landing = [], []
        k = o = 0

        def pair(src, dst, got, to):
            nonlocal k
            mine.append(_remote(src, dst, send_sems, recv_sems, k, to))
            landing.append(_remote(src, got, send_sems, recv_sems, k, me))
            k += 1

        for ex, now, relay, last in self.parts:
            srcs, bufs = src_refs[o:o + len(ex.srcs)], buf_refs[o:o + len(ex.srcs)]
            o += len(ex.srcs)
            for a, r0, n in now:
                rows = pl.ds(r0, n)
                if ex.kind == "gather":
                    for ch in (xn, yn):
                        pair(srcs[a].at[c, rows], bufs[a].at[chip, c, rows], bufs[a].at[_chip_index(ch), c, rows], (*ch, c))
                else:
                    for ch in others:
                        pair(srcs[a].at[_chip_index(ch), rows], bufs[a].at[chip, rows], bufs[a].at[_chip_index(ch), rows],
                             (*ch, c))
            for a, r0, n in relay:
                top, bottom, rows = pl.ds(r0, n // 2), pl.ds(r0 + n // 2, n // 2), pl.ds(r0, n)
                from_x, from_y = bufs[a].at[_chip_index(xn), c, top], bufs[a].at[_chip_index(yn), c, bottom]
                pair(from_x, from_x, bufs[a].at[_chip_index(dg), c, top], (*yn, c))
                pair(from_y, from_y, bufs[a].at[_chip_index(dg), c, bottom], (*xn, c))
                for ch in (xn, yn):
                    here = bufs[a].at[_chip_index(ch), c, rows]
                    pair(here, here, bufs[a].at[_chip_index(ch), 1 - c, rows], sibling)
            for a, r0, n in last:
                here = bufs[a].at[_chip_index(dg), c, pl.ds(r0, n)]
                pair(here, here, bufs[a].at[_chip_index(dg), 1 - c, pl.ds(r0, n)], sibling)
        return mine, landing

    def start(self, src_refs, buf_refs, send_sems, recv_sems):
        for cp in self._copies(src_refs, buf_refs, send_sems, recv_sems)[0]:
            cp.start()

    def finish(self, src_refs, buf_refs, send_sems, recv_sems):
        mine, landing = self._copies(src_refs, buf_refs, send_sems, recv_sems)
        for cp in landing:
            cp.wait_recv()
        for cp in mine:
            cp.wait_send()


class RidingExchange:
    def __init__(self, kind, srcs, piece_rows):
        self.kind, self.srcs = kind, list(srcs)
        lead = (4, 2) if kind == "gather" else (4,)
        self.bufs = [lax.empty(lead + s.shape[1:], s.dtype) for s in srcs]
        heights = [s.shape[1] for s in srcs]
        per = [[(a, r0, min(pr, h - r0)) for r0 in range(0, h, pr)] for a, (h, pr) in enumerate(zip(heights, piece_rows))]
        self.pieces = list(per[0])
        for extra in per[1:]:
            step = max(1, len(self.pieces) // (len(extra) + 1))
            for i, p in enumerate(extra):
                self.pieces.insert(min(len(self.pieces), (i + 1) * step + i), p)
        self.landed, self.relayed = [], []

    def busy(self):
        return bool(self.pieces or self.landed or self.relayed)

    def step(self, n):
        out = []
        for a, r0, rows in self.pieces[:n]:
            if out and out[-1][0] == a and out[-1][1] + out[-1][2] == r0:
                out[-1] = (a, out[-1][1], out[-1][2] + rows)
            else:
                out.append((a, r0, rows))
        self.pieces = self.pieces[n:]
        relay, last = self.landed, self.relayed
        self.landed, self.relayed = (out if self.kind == "gather" else []), relay
        return out, relay, last


class Stream:
    def __init__(self):
        self.queue = []

    def add(self, exchange):
        self.queue.append(exchange)
        return exchange

    def take(self, n, only=None):
        parts = []
        for ex in (self.queue if only is None else [only]):
            had = len(ex.pieces)
            now, relay, last = ex.step(n)
            n -= had - len(ex.pieces)
            if now or relay or last:
                parts.append((ex, now, relay, last))
        return _Side(parts) if parts else None

    def finish(self, exchange, name):
        while exchange.busy():
            side = self.take(len(exchange.pieces), only=exchange)
            _pcall(lambda: None, [], side, in_specs=[], out_specs=[], out_shape=[], name=name)
        self.queue.remove(exchange)
        return exchange.bufs


def _row_ops(rows, tr):
    arrs, specs, widths = [], [], []
    for r in rows:
        arr, n, j = r if isinstance(r, tuple) else (r, r.shape[1], 0)
        arrs.append(arr)
        widths.append(n)
        specs.append(pl.BlockSpec((tr, n), lambda i, j=j: (i, j)))
    return arrs, specs, widths


def rowmap(name, f, rows, params, outs, tr, accs=()):
    rows, row_specs, _ = _row_ops(rows, tr)
    T = rows[0].shape[0]
    nin, nout, nacc = len(rows) + len(params), len(outs), len(accs)

    def body(*refs):
        res = _tup(f(*[r[...] for r in refs[:nin]]))
        for o, r in zip(refs[nin:nin + nout], res[:nout]):
            o[...] = r.astype(o.dtype)
        arefs = refs[nin + nout:]
        if nacc:
            @pl.when(pl.program_id(0) == 0)
            def _():
                for a in arefs:
                    a[...] = jnp.zeros(a.shape, a.dtype)
            for a, r in zip(arefs, res[nout:]):
                a[...] += r.astype(F32)

    in_specs = row_specs + [_full_spec(p) for p in params]
    out_specs = [pl.BlockSpec((tr, n), lambda i: (i, 0)) for n, _ in outs] + \
                [pl.BlockSpec(s, lambda i, nd=len(s): (0,) * nd) for s in accs]
    out_shape = [jax.ShapeDtypeStruct((T, n), d) for n, d in outs] + [jax.ShapeDtypeStruct(s, F32) for s in accs]
    res = pl.pallas_call(body, grid=(T // tr,), in_specs=in_specs, out_specs=out_specs, out_shape=out_shape,
                         compiler_params=_cparams(("arbitrary",)), name=name)(*rows, *params)
    return tuple(res)


def rowmap_bwd(name, f, rows, params, cots, tr, drow_dtypes, want_params):
    rows, row_specs, widths = _row_ops(rows, tr)
    T = rows[0].shape[0]
    nr, npar = len(rows), len(params)
    cot_arrays, cot_specs, _ = _row_ops([c for c in cots if c is not None], tr)
    nc = len(cot_arrays)
    ridx = [i for i, d in enumerate(drow_dtypes) if d is not None]
    pidx = [i for i, w in enumerate(want_params) if w]

    def body(*refs):
        rvals = [r[...] for r in refs[:nr]]
        pvals = [r[...] for r in refs[nr:nr + npar]]
        crefs = list(refs[nr + npar:nr + npar + nc])
        orefs = refs[nr + npar + nc:]
        outs, vjp = jax.vjp(f, *rvals, *pvals)
        outs = _tup(outs)
        cts = []
        for c, o in zip(cots, outs):
            cts.append(jnp.ones(o.shape, o.dtype) if c is None else crefs.pop(0)[...].astype(o.dtype))
        grads = vjp(tuple(cts) if len(cts) > 1 else cts[0])
        for o, i in zip(orefs[:len(ridx)], ridx):
            o[...] = grads[i].astype(o.dtype)
        prefs = orefs[len(ridx):]
        if prefs:
            @pl.when(pl.program_id(0) == 0)
            def _():
                for a in prefs:
                    a[...] = jnp.zeros(a.shape, a.dtype)
            for a, i in zip(prefs, pidx):
                a[...] += grads[nr + i].astype(F32)

    in_specs = row_specs + [_full_spec(p) for p in params] + cot_specs
    out_specs = [pl.BlockSpec((tr, widths[i]), lambda i: (i, 0)) for i in ridx] + [_full_spec(params[i]) for i in pidx]
    out_shape = [jax.ShapeDtypeStruct((T, widths[i]), drow_dtypes[i]) for i in ridx] + \
                [jax.ShapeDtypeStruct(params[i].shape, F32) for i in pidx]
    res = pl.pallas_call(body, grid=(T // tr,), in_specs=in_specs, out_specs=out_specs, out_shape=out_shape,
                         compiler_params=_cparams(("arbitrary",)), name=name)(*rows, *params, *cot_arrays)
    res = tuple(res)
    return res[:len(ridx)], res[len(ridx):]


def _pick(n, pref):
    best = None
    for d in range(LANES, min(n, pref) + 1, LANES):
        if n % d == 0:
            best = d
    return best if best is not None else n


def _spec2(arr, tile, pos):
    tr, tc = tile
    if arr.ndim == 2:
        return pl.BlockSpec((tr, tc), lambda i, j, k: pos(i, j, k))
    assert arr.shape[2] % tc == 0, (arr.shape, tile)
    per = arr.shape[2] // tc

    def imap(i, j, k):
        r, c = pos(i, j, k)
        return (c // per, r, c % per)
    return pl.BlockSpec((None, tr, tc), imap)


def _dims2(arr):
    return (arr.shape[0], arr.shape[1]) if arr.ndim == 2 else (arr.shape[1], arr.shape[0] * arr.shape[2])


def mm(name, a, b, mode, outs, epilogue=None, extras=(), pm=1024, pn=512, pk=4224, side=None):
    ar, ac = _dims2(a)
    br, bc = _dims2(b)
    if mode == "nn":
        M, K, N = ar, ac, bc
    elif mode == "nt":
        M, K, N = ar, ac, br
    else:
        M, K, N = ac, ar, bc
    tm, tn, tk = _pick(M, pm), _pick(N, pn), _pick(K, pk)
    for arr in (a, b) + tuple(extras):
        if arr.ndim == 3:
            assert arr.shape[2] % LANES == 0
    if mode == "nn":
        a_spec = _spec2(a, (tm, tk), lambda i, j, k: (i, k))
        b_spec = _spec2(b, (tk, tn), lambda i, j, k: (k, j))
        dims = (((1,), (0,)), ((), ()))
    elif mode == "nt":
        a_spec = _spec2(a, (tm, tk), lambda i, j, k: (i, k))
        b_spec = _spec2(b, (tn, tk), lambda i, j, k: (j, k))
        dims = (((1,), (1,)), ((), ()))
    else:
        a_spec = _spec2(a, (tk, tm), lambda i, j, k: (k, i))
        b_spec = _spec2(b, (tk, tn), lambda i, j, k: (k, j))
        dims = (((0,), (0,)), ((), ()))
    nk = K // tk
    nex = len(extras)

    def body(*refs):
        a_ref, b_ref = refs[0], refs[1]
        ex_refs = refs[2:2 + nex]
        o_refs = refs[2 + nex:2 + nex + len(outs)]
        acc = refs[-1]
        k = pl.program_id(2)
        part = lax.dot_general(a_ref[...].astype(BF16), b_ref[...].astype(BF16), dims, preferred_element_type=F32)

        def finish(r):
            res = (r,) if epilogue is None else _tup(epilogue(r, *[e[...] for e in ex_refs]))
            for o, v in zip(o_refs, res):
                o[...] = v.astype(o.dtype)

        if nk == 1:
            finish(part)
            return

        @pl.when(k == 0)
        def _():
            acc[...] = part

        @pl.when(k > 0)
        def _():
            acc[...] += part

        @pl.when(k == nk - 1)
        def _():
            finish(acc[...])

    out_shape, out_specs = [], []
    args, in_specs, aliases = [a, b, *extras], [a_spec, b_spec], {}
    in_specs += [_spec2(e, (tm, tn), lambda i, j, k: (i, j)) for e in extras]
    for dt, chunks in outs:
        if isinstance(chunks, tuple):
            how, rows, off, buf = chunks
            o = jax.ShapeDtypeStruct((4, rows, D_MODEL), dt)
            assert off % tm == 0 and D_MODEL % tn == 0 and D_MODEL % tm == 0
            if how == "cols":
                per = D_MODEL // tn
                spec = pl.BlockSpec((None, tm, tn), lambda i, j, k: (j // per, off // tm + i, j % per))
            else:
                per = D_MODEL // tm
                spec = pl.BlockSpec((None, tm, tn), lambda i, j, k: (i // per, off // tm + i % per, j))
            if buf is not None:
                aliases[len(args)] = len(out_shape)
                args.append(buf)
                in_specs.append(pl.BlockSpec(memory_space=pl.ANY))
        else:
            o = jax.ShapeDtypeStruct((M, N) if chunks == 1 else (chunks, M, N // chunks), dt)
            spec = _spec2(o, (tm, tn), lambda i, j, k: (i, j))
        out_shape.append(o)
        out_specs.append(spec)
    n_extra_in = len(args) - 2 - nex
    res = _pcall(
        (lambda *refs: body(*refs[:2 + nex], *refs[2 + nex + n_extra_in:])) if n_extra_in else body, args, side,
        grid=(M // tm, N // tn, nk), in_specs=in_specs, out_specs=out_specs, out_shape=out_shape,
        scratch_shapes=[pltpu.VMEM((tm, tn), F32)] if nk > 1 else [], input_output_aliases=aliases,
        compiler_params=_cparams(("parallel", "parallel", "arbitrary")), name=name)
    return tuple(res)


def _rms(x, g):
    return x * lax.rsqrt(jnp.mean(x * x, axis=-1, keepdims=True) + EPS) * g


def f_norm(x, g):
    return _rms(x, g)


def f_norm_keep(x, g):
    return x, _rms(x, g)


def f_resnorm(xp, m, gpost, gnext):
    xn = xp + _rms(m, gpost)
    return xn, _rms(xn, gnext)


def f_final_rows(xp, m, tgt, gpost):
    xn = xp + _rms(m, gpost)
    return 0.5 * jnp.mean(jnp.square(xn - tgt), axis=-1, keepdims=True)


def f_final_loss(xp, m, tgt, gpost):
    return jnp.sum(f_final_rows(xp, m, tgt, gpost), axis=0, keepdims=True)


def f_norm_twice(x, g):
    y = _rms(x, g)
    return y, y


def f_xattn(q, k, v):
    outs = []
    for h in range(X_HEADS):
        sl = slice(h * X_HEAD_DIM, (h + 1) * X_HEAD_DIM)
        s = lax.dot_general(q[:, sl].astype(BF16), k[:, sl].astype(BF16), (((1,), (1,)), ((), ())),
                            preferred_element_type=F32) * (X_HEAD_DIM ** -0.5)
        m = jnp.max(s, axis=-1, keepdims=True)
        e = jnp.exp(s - m)
        p = e / jnp.sum(e, axis=-1, keepdims=True)
        outs.append(jnp.dot(p.astype(BF16), v[:, sl].astype(BF16), preferred_element_type=F32))
    return jnp.concatenate(outs, axis=1)


def f_adamw(w, g, m, v):
    m = ADAM_B1 * m + (1.0 - ADAM_B1) * g
    v = ADAM_B2 * v + (1.0 - ADAM_B2) * jnp.square(g)
    m_hat = m / (1.0 - ADAM_B1 ** ADAM_STEP)
    v_hat = v / (1.0 - ADAM_B2 ** ADAM_STEP)
    delta = -ADAM_LR * (m_hat / (jnp.sqrt(v_hat) + ADAM_EPS) + ADAM_WD * w)
    return delta, m, v


def f_sum8(*xs):
    t = xs[0]
    for x in xs[1:]:
        t = t + x
    return t


def t5_bucket_map():
    qi = jnp.arange(BLOCK)[:, None]
    kj = jnp.arange(3 * BLOCK)[None, :]
    rel = kj - BLOCK - qi
    nb = N_BUCKETS // 2
    max_exact = nb // 2
    ret = jnp.where(rel > 0, nb, 0)
    n = jnp.abs(rel)
    nf = jnp.maximum(n, 1).astype(jnp.float32)
    large = max_exact + (jnp.log(nf / max_exact) / math.log(MAX_DISTANCE / max_exact) * (nb - max_exact)).astype(jnp.int32)
    large = jnp.minimum(large, nb - 1)
    return (ret + jnp.where(n < max_exact, n, large)).astype(jnp.int32)


def bias_table_fwd(table, bucket):
    def body(t_ref, b_ref, o_ref):
        bk = b_ref[...]
        for h in range(A_HEADS):
            acc = jnp.zeros(bk.shape, F32)
            for b in range(N_BUCKETS):
                acc = jnp.where(bk == b, t_ref[b, h], acc)
            o_ref[h] = acc
    return pl.pallas_call(
        body, in_specs=[pl.BlockSpec(memory_space=pltpu.SMEM), pl.BlockSpec(memory_space=pltpu.VMEM)],
        out_specs=pl.BlockSpec(memory_space=pltpu.VMEM),
        out_shape=jax.ShapeDtypeStruct((A_HEADS, BLOCK, 3 * BLOCK), F32), name="bias_table_fwd")(table, bucket)


def bias_table_bwd(dbias_list, bucket):
    n = len(dbias_list)

    def body(*refs):
        b_ref, o_ref = refs[n], refs[n + 1]
        bk = b_ref[...]
        row = lax.broadcasted_iota(jnp.int32, (N_BUCKETS, LANES), 0)
        col = lax.broadcasted_iota(jnp.int32, (N_BUCKETS, LANES), 1)
        out = jnp.zeros((N_BUCKETS, LANES), F32)
        for h in range(A_HEADS):
            d = refs[0][h]
            for r in refs[1:n]:
                d = d + r[h]
            for b in range(N_BUCKETS):
                s = jnp.sum(jnp.where(bk == b, d, 0.0), keepdims=True)
                out = out + jnp.where((row == b) & (col == h), s, 0.0)
        o_ref[...] = out
    return pl.pallas_call(
        body, out_shape=jax.ShapeDtypeStruct((N_BUCKETS, LANES), F32), name="bias_table_bwd",
        compiler_params=_cparams())(*dbias_list, bucket)


def _attn_mask(n, nblk):
    i = lax.broadcasted_iota(jnp.int32, (BLOCK, 3 * BLOCK), 0)
    j = lax.broadcasted_iota(jnp.int32, (BLOCK, 3 * BLOCK), 1)
    kpos = n * BLOCK + j - BLOCK
    return (jnp.abs(j - BLOCK - i) <= WINDOW) & (kpos >= 0) & (kpos < nblk * BLOCK)


def f_attn_block(q, k3, v3, bias, sink, mask):
    kb, vb = k3.astype(BF16), v3.astype(BF16)
    outs = []
    for g in range(A_GROUP):
        qg = q[:, g * A_HEAD_DIM:(g + 1) * A_HEAD_DIM].astype(BF16)
        s = lax.dot_general(qg, kb, (((1,), (1,)), ((), ())), preferred_element_type=F32) * (A_HEAD_DIM ** -0.5)
        s = jnp.where(mask, s + bias[g], NEG_INF)
        sk = sink[g:g + 1, :]
        m = jnp.maximum(jnp.max(s, axis=-1, keepdims=True), sk)
        e = jnp.exp(s - m)
        den = jnp.sum(e, axis=-1, keepdims=True) + jnp.exp(sk - m)
        p = e / den
        outs.append(jnp.dot(p.astype(BF16), vb, preferred_element_type=F32))
    return jnp.concatenate(outs, axis=1)


def _attn_in_specs(nblk):
    qw = A_GROUP * A_HEAD_DIM
    kc, vc = OFF_AK // A_HEAD_DIM, OFF_AV // A_HEAD_DIM
    return [
        pl.BlockSpec((BLOCK, qw), lambda h, n: (n, h)),
        pl.BlockSpec((BLOCK, A_HEAD_DIM), lambda h, n: (jnp.maximum(n - 1, 0), kc + h)),
        pl.BlockSpec((BLOCK, A_HEAD_DIM), lambda h, n: (n, kc + h)),
        pl.BlockSpec((BLOCK, A_HEAD_DIM), lambda h, n: (jnp.minimum(n + 1, nblk - 1), kc + h)),
        pl.BlockSpec((BLOCK, A_HEAD_DIM), lambda h, n: (jnp.maximum(n - 1, 0), vc + h)),
        pl.BlockSpec((BLOCK, A_HEAD_DIM), lambda h, n: (n, vc + h)),
        pl.BlockSpec((BLOCK, A_HEAD_DIM), lambda h, n: (jnp.minimum(n + 1, nblk - 1), vc + h)),
        pl.BlockSpec((A_GROUP, BLOCK, 3 * BLOCK), lambda h, n: (h, 0, 0)),
        pl.BlockSpec((None, A_GROUP, 1), lambda h, n: (h, 0, 0)),
    ]


def attn_fwd(proj, bias, sink, side=None):
    T = proj.shape[0]
    nblk = T // BLOCK

    def body(q_ref, k0, k1, k2, v0, v1, v2, b_ref, s_ref, o_ref):
        n = pl.program_id(1)
        k3 = jnp.concatenate([k0[...], k1[...], k2[...]], axis=0)
        v3 = jnp.concatenate([v0[...], v1[...], v2[...]], axis=0)
        o = f_attn_block(q_ref[...], k3, v3, b_ref[...], s_ref[...], _attn_mask(n, nblk))
        o_ref[...] = o.astype(o_ref.dtype)

    return _pcall(
        body, [proj] * 7 + [bias, sink], side, grid=(A_KV_HEADS, nblk), in_specs=_attn_in_specs(nblk),
        out_specs=pl.BlockSpec((BLOCK, A_GROUP * A_HEAD_DIM), lambda h, n: (n, h)),
        out_shape=jax.ShapeDtypeStruct((T, A_Q), BF16),
        compiler_params=_cparams(("arbitrary", "arbitrary")), name="attn_fwd")[0]


def attn_bwd(proj, bias, sink, dcat, side=None):
    T = proj.shape[0]
    nblk = T // BLOCK
    qw = A_GROUP * A_HEAD_DIM

    def body(q_ref, k0, k1, k2, v0, v1, v2, b_ref, s_ref, do_ref, dq_ref, dk_ref, dv_ref, db_ref, ds_ref, dk_acc, dv_acc):
        n = pl.program_id(1)

        @pl.when(n == 0)
        def _():
            dk_acc[...] = jnp.zeros(dk_acc.shape, F32)
            dv_acc[...] = jnp.zeros(dv_acc.shape, F32)
            db_ref[...] = jnp.zeros(db_ref.shape, F32)
            ds_ref[...] = jnp.zeros(ds_ref.shape, F32)

        k3 = jnp.concatenate([k0[...], k1[...], k2[...]], axis=0)
        v3 = jnp.concatenate([v0[...], v1[...], v2[...]], axis=0)
        mask = _attn_mask(n, nblk)
        _, vjp = jax.vjp(lambda q, k, v, b, s: f_attn_block(q, k, v, b, s, mask), q_ref[...], k3, v3, b_ref[...], s_ref[...])
        dq, dk3, dv3, db, ds = vjp(do_ref[...])
        dq_ref[...] = dq.astype(dq_ref.dtype)
        db_ref[...] += db
        ds_ref[...] += ds
        mid = pl.multiple_of(n * BLOCK, BLOCK)
        dk_acc[pl.ds(mid, BLOCK), :] += dk3[BLOCK:2 * BLOCK]
        dv_acc[pl.ds(mid, BLOCK), :] += dv3[BLOCK:2 * BLOCK]

        @pl.when(n > 0)
        def _():
            lo = pl.multiple_of((n - 1) * BLOCK, BLOCK)
            dk_acc[pl.ds(lo, BLOCK), :] += dk3[0:BLOCK]
            dv_acc[pl.ds(lo, BLOCK), :] += dv3[0:BLOCK]

        @pl.when(n < nblk - 1)
        def _():
            hi = pl.multiple_of((n + 1) * BLOCK, BLOCK)
            dk_acc[pl.ds(hi, BLOCK), :] += dk3[2 * BLOCK:3 * BLOCK]
            dv_acc[pl.ds(hi, BLOCK), :] += dv3[2 * BLOCK:3 * BLOCK]

        @pl.when(n == nblk - 1)
        def _():
            dk_ref[...] = dk_acc[...].astype(dk_ref.dtype)
            dv_ref[...] = dv_acc[...].astype(dv_ref.dtype)

    in_specs = _attn_in_specs(nblk) + [pl.BlockSpec((BLOCK, qw), lambda h, n: (n, h))]
    out_specs = [
        pl.BlockSpec((BLOCK, qw), lambda h, n: (n, h)),
        pl.BlockSpec((T, A_HEAD_DIM), lambda h, n: (0, h)),
        pl.BlockSpec((T, A_HEAD_DIM), lambda h, n: (0, h)),
        pl.BlockSpec((A_GROUP, BLOCK, 3 * BLOCK), lambda h, n: (h, 0, 0)),
        pl.BlockSpec((None, A_GROUP, 1), lambda h, n: (h, 0, 0)),
    ]
    out_shape = [
        jax.ShapeDtypeStruct((T, A_Q), BF16), jax.ShapeDtypeStruct((T, A_KV), BF16), jax.ShapeDtypeStruct((T, A_KV), BF16),
        jax.ShapeDtypeStruct((A_HEADS, BLOCK, 3 * BLOCK), F32), jax.ShapeDtypeStruct((A_KV_HEADS, A_GROUP, 1), F32),
    ]
    return _pcall(
        body, [proj] * 7 + [bias, sink, dcat], side, grid=(A_KV_HEADS, nblk), in_specs=in_specs, out_specs=out_specs,
        out_shape=out_shape, scratch_shapes=[pltpu.VMEM((T, A_HEAD_DIM), F32), pltpu.VMEM((T, A_HEAD_DIM), F32)],
        compiler_params=_cparams(("arbitrary", "arbitrary")), name="attn_bwd")


def f_gla_gate(z, w2f, b2f, w2b, b2b):
    laf = jax.nn.log_sigmoid(jnp.dot(z, w2f, precision=HI, preferred_element_type=F32) + b2f) / GATE_TAU
    lab = jax.nn.log_sigmoid(jnp.dot(z, w2b, precision=HI, preferred_element_type=F32) + b2b) / GATE_TAU
    return laf, lab


def f_gla_post(o, g, gn):
    outs = []
    for h in range(B_HEADS):
        sl = slice(h * B_VAL_DIM, (h + 1) * B_VAL_DIM)
        oh = o[:, sl]
        outs.append(oh * lax.rsqrt(jnp.mean(oh * oh, axis=-1, keepdims=True) + EPS))
    return jnp.concatenate(outs, axis=1) * gn * jax.nn.silu(g)


def _gla_consts(forward):
    C = GLA_CHUNK
    i = lax.broadcasted_iota(jnp.int32, (C, C), 0)
    j = lax.broadcasted_iota(jnp.int32, (C, C), 1)
    if forward:
        return (j <= i).astype(F32), j <= i
    return (j >= i).astype(F32), j > i


def _gla_chunk(q, k, v, la, st, tri, msk, forward):
    C = q.shape[0]
    b = jnp.dot(tri, la, precision=HI, preferred_element_type=F32)
    bl = b[C - 1:C] if forward else b[0:1]
    qe = (q * (B_KEY_DIM ** -0.5)) * jnp.exp(b)
    ke = k * jnp.exp(-b)
    kl = k * jnp.exp(bl - b)
    att = lax.dot_general(qe.astype(BF16), ke.astype(BF16), (((1,), (1,)), ((), ())), preferred_element_type=F32)
    att = jnp.where(msk, att, 0.0)
    o = jnp.dot(att.astype(BF16), v.astype(BF16), preferred_element_type=F32)
    o = o + lax.dot_general(qe.astype(BF16), st.astype(BF16), (((1,), (1,)), ((), ())), preferred_element_type=F32)
    st_new = st * jnp.exp(bl) + lax.dot_general(v.astype(BF16), kl.astype(BF16), (((0,), (0,)), ((), ())),
                                                preferred_element_type=F32)
    return o, st_new


def _gla_specs(T):
    qc, kc, vc = OFF_BQ // LANES, OFF_BK // LANES, OFF_BV // (2 * B_VAL_DIM)
    return [
        pl.BlockSpec((T, LANES), lambda p: (0, qc + p)),
        pl.BlockSpec((T, LANES), lambda p: (0, kc + p)),
        pl.BlockSpec((T, 2 * B_VAL_DIM), lambda p: (0, vc + p)),
        pl.BlockSpec((T, LANES), lambda p: (0, p)),
        pl.BlockSpec((T, LANES), lambda p: (0, p)),
    ]


def _rows(c):
    return pl.ds(pl.multiple_of(c * GLA_CHUNK, GLA_CHUNK), GLA_CHUNK)


def gla_fwd(proj, laf, lab, side=None):
    T = proj.shape[0]
    nc = T // GLA_CHUNK

    def body(q_ref, k_ref, v_ref, laf_ref, lab_ref, o_ref, ob_scr):
        tri_f, msk_f = _gla_consts(True)
        tri_b, msk_b = _gla_consts(False)
        zero = jnp.zeros((B_VAL_DIM, B_KEY_DIM), F32)

        def step(c, carry):
            rf, rb = _rows(c), _rows(nc - 1 - c)
            new = []
            for hh in range(2):
                ks = slice(hh * B_KEY_DIM, (hh + 1) * B_KEY_DIM)
                vs = slice(hh * B_VAL_DIM, (hh + 1) * B_VAL_DIM)
                o, s = _gla_chunk(q_ref[rf, ks], k_ref[rf, ks], v_ref[rf, vs], laf_ref[rf, ks], carry[2 * hh], tri_f, msk_f, True)
                o_ref[rf, vs] = o
                new.append(s)
                o, s = _gla_chunk(q_ref[rb, ks], k_ref[rb, ks], v_ref[rb, vs], lab_ref[rb, ks], carry[2 * hh + 1], tri_b, msk_b, False)
                ob_scr[rb, vs] = o
                new.append(s)
            return tuple(new)

        lax.fori_loop(0, nc, step, (zero,) * 4)
        o_ref[...] += ob_scr[...]

    return _pcall(
        body, [proj, proj, proj, laf, lab], side, grid=(B_HEADS // 2,), in_specs=_gla_specs(T),
        out_specs=pl.BlockSpec((T, 2 * B_VAL_DIM), lambda p: (0, p)),
        out_shape=jax.ShapeDtypeStruct((T, B_V), F32),
        scratch_shapes=[pltpu.VMEM((T, 2 * B_VAL_DIM), F32)],
        compiler_params=_cparams(("arbitrary",)), name="gla_fwd")[0]


def gla_bwd(proj, laf, lab, do, side=None):
    T = proj.shape[0]
    nc = T // GLA_CHUNK
    SROWS = 2 * B_VAL_DIM

    def body(q_ref, k_ref, v_ref, laf_ref, lab_ref, do_ref, dq_ref, dk_ref, dv_ref, dlaf_ref, dlab_ref,
             sf_scr, sb_scr, dq_acc, dk_acc, dv_acc):
        tri_f, msk_f = _gla_consts(True)
        tri_b, msk_b = _gla_consts(False)
        zero = jnp.zeros((B_VAL_DIM, B_KEY_DIM), F32)
        dq_acc[...] = jnp.zeros(dq_acc.shape, F32)
        dk_acc[...] = jnp.zeros(dk_acc.shape, F32)
        dv_acc[...] = jnp.zeros(dv_acc.shape, F32)

        def srow(c, hh):
            return pl.ds(pl.multiple_of(c * SROWS + hh * B_VAL_DIM, B_VAL_DIM), B_VAL_DIM)

        def states(c, carry):
            cf, cb = c, nc - 1 - c
            rf, rb = _rows(cf), _rows(cb)
            new = []
            for hh in range(2):
                ks = slice(hh * B_KEY_DIM, (hh + 1) * B_KEY_DIM)
                vs = slice(hh * B_VAL_DIM, (hh + 1) * B_VAL_DIM)
                sf_scr[srow(cf, hh), :] = carry[2 * hh]
                _, s = _gla_chunk(q_ref[rf, ks], k_ref[rf, ks], v_ref[rf, vs], laf_ref[rf, ks], carry[2 * hh], tri_f, msk_f, True)
                new.append(s)
                sb_scr[srow(cb, hh), :] = carry[2 * hh + 1]
                _, s = _gla_chunk(q_ref[rb, ks], k_ref[rb, ks], v_ref[rb, vs], lab_ref[rb, ks], carry[2 * hh + 1], tri_b, msk_b, False)
                new.append(s)
            return tuple(new)

        lax.fori_loop(0, nc, states, (zero,) * 4)

        def back(c, carry):
            cf, cb = nc - 1 - c, c
            rf, rb = _rows(cf), _rows(cb)
            new = []
            for hh in range(2):
                ks = slice(hh * B_KEY_DIM, (hh + 1) * B_KEY_DIM)
                vs = slice(hh * B_VAL_DIM, (hh + 1) * B_VAL_DIM)
                for fwd, r, c_, la_ref, dla_ref, s_scr, g, tri, msk in (
                        (True, rf, cf, laf_ref, dlaf_ref, sf_scr, carry[2 * hh], tri_f, msk_f),
                        (False, rb, cb, lab_ref, dlab_ref, sb_scr, carry[2 * hh + 1], tri_b, msk_b)):
                    _, vjp = jax.vjp(
                        lambda q, k, v, la, st: _gla_chunk(q, k, v, la, st, tri, msk, fwd),
                        q_ref[r, ks], k_ref[r, ks], v_ref[r, vs], la_ref[r, ks], s_scr[srow(c_, hh), :])
                    dq, dk, dv, dla, dst = vjp((do_ref[r, vs], g))
                    dq_acc[r, ks] += dq
                    dk_acc[r, ks] += dk
                    dv_acc[r, vs] += dv
                    dla_ref[r, ks] = dla
                    new.append(dst)
            return tuple(new)

        lax.fori_loop(0, nc, back, (zero,) * 4)
        dq_ref[...] = dq_acc[...].astype(dq_ref.dtype)
        dk_ref[...] = dk_acc[...].astype(dk_ref.dtype)
        dv_ref[...] = dv_acc[...].astype(dv_ref.dtype)

    in_specs = _gla_specs(T) + [pl.BlockSpec((T, 2 * B_VAL_DIM), lambda p: (0, p))]
    out_specs = [
        pl.BlockSpec((T, LANES), lambda p: (0, p)), pl.BlockSpec((T, LANES), lambda p: (0, p)),
        pl.BlockSpec((T, 2 * B_VAL_DIM), lambda p: (0, p)),
        pl.BlockSpec((T, LANES), lambda p: (0, p)), pl.BlockSpec((T, LANES), lambda p: (0, p)),
    ]
    out_shape = [
        jax.ShapeDtypeStruct((T, B_QK), BF16), jax.ShapeDtypeStruct((T, B_QK), BF16), jax.ShapeDtypeStruct((T, B_V), BF16),
        jax.ShapeDtypeStruct((T, B_QK), F32), jax.ShapeDtypeStruct((T, B_QK), F32),
    ]
    scratch = [
        pltpu.VMEM((nc * SROWS, B_KEY_DIM), F32), pltpu.VMEM((nc * SROWS, B_KEY_DIM), F32),
        pltpu.VMEM((T, LANES), F32), pltpu.VMEM((T, LANES), F32), pltpu.VMEM((T, 2 * B_VAL_DIM), F32),
    ]
    return _pcall(
        body, [proj, proj, proj, laf, lab, do], side, grid=(B_HEADS // 2,), in_specs=in_specs, out_specs=out_specs,
        out_shape=out_shape, scratch_shapes=scratch, compiler_params=_cparams(("arbitrary",)), name="gla_bwd")


def _shift_raw(x, k):
    T = x.shape[0]
    r = lax.broadcasted_iota(jnp.int32, x.shape, 0)
    if k > 0:
        return jnp.where(r >= k, pltpu.roll(x, k, 0), 0.0)
    return jnp.where(r < T + k, pltpu.roll(x, T + k, 0), 0.0)


@functools.partial(jax.custom_vjp, nondiff_argnums=(1,))
def _shift(x, k):
    return _shift_raw(x, k)


_shift.defvjp(lambda x, k: (_shift_raw(x, k), None), lambda k, _, g: (_shift_raw(g, -k),))


def _scan_raw(a, u, reverse):
    T = a.shape[0]
    d = 1
    while d < T:
        k = -d if reverse else d
        u = a * _shift_raw(u, k) + u
        a = a * _shift_raw(a, k)
        d *= 2
    return u


@functools.partial(jax.custom_vjp, nondiff_argnums=(2,))
def _scan(a, u, reverse):
    return _scan_raw(a, u, reverse)


def _scan_f(a, u, reverse):
    h = _scan_raw(a, u, reverse)
    return h, (a, h)


def _scan_b(reverse, res, dh):
    a, h = res
    k = 1 if reverse else -1
    du = _scan_raw(_shift_raw(a, k), dh, not reverse)
    return du * _shift_raw(h, -k), du


_scan.defvjp(_scan_f, _scan_b)


def f_lru(cx, cy, cw, cb, wa, ba, wx, bx, lam, diff):
    shift, scan = (_shift, _scan) if diff else (_shift_raw, _scan_raw)
    xc = cx * cw[CONV_LEFT:CONV_LEFT + 1]
    for j in range(CONV_WIDTH):
        if j != CONV_LEFT:
            xc = xc + shift(cx, CONV_LEFT - j) * cw[j:j + 1]
    xc = xc + cb
    xb = xc.astype(BF16)
    h = None
    for s in range(2):
        r = jax.nn.sigmoid(jnp.dot(xb, wa[s].astype(BF16), preferred_element_type=F32) + ba[s:s + 1])
        i = jax.nn.sigmoid(jnp.dot(xb, wx[s].astype(BF16), preferred_element_type=F32) + bx[s:s + 1])
        log_a = -LRU_C * r * jax.nn.softplus(-lam[s:s + 1])
        a = jnp.exp(log_a)
        one_minus_a2 = -jnp.tanh(log_a) * (a * a + 1.0)
        u = jnp.sqrt(one_minus_a2) * (i * xc)
        hs = scan(a, u, s == 1)
        h = hs if h is None else h + hs
    return h * jax.nn.gelu(cy)


def _lru_specs(T):
    xc, yc = OFF_CX // LANES, OFF_CY // LANES
    return [
        pl.BlockSpec((T, LANES), lambda b: (0, xc + b)),
        pl.BlockSpec((T, LANES), lambda b: (0, yc + b)),
        pl.BlockSpec((CONV_WIDTH, LANES), lambda b: (0, b)),
        pl.BlockSpec((1, LANES), lambda b: (0, b)),
        pl.BlockSpec((2, None, C_BLOCK_DIM, C_BLOCK_DIM), lambda b: (0, b, 0, 0)),
        pl.BlockSpec((2, LANES), lambda b: (0, b)),
        pl.BlockSpec((2, None, C_BLOCK_DIM, C_BLOCK_DIM), lambda b: (0, b, 0, 0)),
        pl.BlockSpec((2, LANES), lambda b: (0, b)),
        pl.BlockSpec((2, LANES), lambda b: (0, b)),
    ]


def lru_fwd(proj, cw, cb, wa, ba, wx, bx, lam, side=None):
    T = proj.shape[0]

    def body(cx, cy, cw_r, cb_r, wa_r, ba_r, wx_r, bx_r, lam_r, o_ref):
        o = f_lru(cx[...], cy[...], cw_r[...], cb_r[...], wa_r[...], ba_r[...], wx_r[...], bx_r[...], lam_r[...], False)
        o_ref[...] = o.astype(o_ref.dtype)

    return _pcall(
        body, [proj, proj, cw, cb, wa, ba, wx, bx, lam], side, grid=(C_BLOCKS,), in_specs=_lru_specs(T),
        out_specs=pl.BlockSpec((T, LANES), lambda b: (0, b)), out_shape=jax.ShapeDtypeStruct((T, C_WIDTH), BF16),
        compiler_params=_cparams(("arbitrary",)), name="lru_fwd")[0]


def lru_bwd(proj, cw, cb, wa, ba, wx, bx, lam, dcat, side=None):
    T = proj.shape[0]
    oc = (A_Q + B_V) // LANES

    def body(cx, cy, cw_r, cb_r, wa_r, ba_r, wx_r, bx_r, lam_r, do_ref, *outs):
        _, vjp = jax.vjp(functools.partial(f_lru, diff=True), cx[...], cy[...], cw_r[...], cb_r[...], wa_r[...],
                         ba_r[...], wx_r[...], bx_r[...], lam_r[...])
        grads = vjp(do_ref[...])
        for o, g in zip(outs, grads):
            o[...] = g.astype(o.dtype)

    specs = _lru_specs(T)
    out_specs = [pl.BlockSpec((T, LANES), lambda b: (0, b)), pl.BlockSpec((T, LANES), lambda b: (0, b))] + specs[2:]
    out_shape = [jax.ShapeDtypeStruct((T, C_WIDTH), BF16), jax.ShapeDtypeStruct((T, C_WIDTH), BF16)] + \
                [jax.ShapeDtypeStruct(p.shape, F32) for p in (cw, cb, wa, ba, wx, bx, lam)]
    return _pcall(
        body, [proj, proj, cw, cb, wa, ba, wx, bx, lam, dcat], side, grid=(C_BLOCKS,),
        in_specs=specs + [pl.BlockSpec((T, LANES), lambda b: (0, oc + b))], out_specs=out_specs, out_shape=out_shape,
        compiler_params=_cparams(("arbitrary",)), name="lru_bwd")


def all_gather8(name, blk):
    def body(x_ref, out_ref, send_sems, recv_sems):
        x, y, c, _, others = _place()
        sibling = (x, y, 1 - c)

        def slab(px, py, pc):
            return out_ref.at[4 * px + 2 * py + pc]

        first = [_remote(x_ref, slab(x, y, c), send_sems, recv_sems, 0, sibling)]
        first += [_remote(x_ref, slab(x, y, c), send_sems, recv_sems, 1 + j, (*ch, c)) for j, ch in enumerate(others)]
        for cp in first:
            cp.start()
        passed = [_remote(slab(*ch, c), slab(*ch, c), send_sems, recv_sems, 4 + j, sibling) for j, ch in enumerate(others)]
        for j, ch in enumerate(others):
            _remote(x_ref, slab(*ch, c), send_sems, recv_sems, 1 + j, (x, y, c)).wait_recv()
            passed[j].start()
        _remote(x_ref, slab(x, y, 1 - c), send_sems, recv_sems, 0, (x, y, c)).wait_recv()
        for j, ch in enumerate(others):
            _remote(x_ref, slab(*ch, 1 - c), send_sems, recv_sems, 4 + j, (x, y, c)).wait_recv()
        for cp in first + passed:
            cp.wait_send()

    out = pl.pallas_call(
        body, out_shape=jax.ShapeDtypeStruct((8,) + blk.shape, blk.dtype), in_specs=[HBM_SPEC], out_specs=HBM_SPEC,
        scratch_shapes=[pltpu.SemaphoreType.DMA((7,)), pltpu.SemaphoreType.DMA((7,))], name=name)(blk)
    me = 4 * lax.axis_index("x") + 2 * lax.axis_index("y") + lax.axis_index("c")
    return lax.dynamic_update_index_in_dim(out, blk, me, 0)


def _exchange_call(name, body, arrays, out_shapes, n_sems):
    n = len(arrays)

    def kernel_body(*refs):
        body(refs[:n], refs[n:2 * n], refs[2 * n], refs[2 * n + 1])

    return pl.pallas_call(
        kernel_body, out_shape=out_shapes, in_specs=[HBM_SPEC] * n, out_specs=[HBM_SPEC] * n,
        scratch_shapes=[pltpu.SemaphoreType.DMA((n * n_sems,)), pltpu.SemaphoreType.DMA((n * n_sems,))], name=name)(*arrays)


def chip_gather(name, shards):
    def body(ins, outs, send_sems, recv_sems):
        x, y, c, chip, others = _place()
        sibling = (x, y, 1 - c)
        first, passed = [], []
        for a, (x_ref, out_ref) in enumerate(zip(ins, outs)):
            first += [_remote(x_ref.at[c], out_ref.at[chip, c], send_sems, recv_sems, 6 * a + j, (*ch, c))
                      for j, ch in enumerate(others)]
        for cp in first:
            cp.start()
        for a, (x_ref, out_ref) in enumerate(zip(ins, outs)):
            for j, ch in enumerate(others):
                here = out_ref.at[_chip_index(ch), c]
                _remote(x_ref.at[c], here, send_sems, recv_sems, 6 * a + j, (x, y, c)).wait_recv()
                cp = _remote(here, here, send_sems, recv_sems, 6 * a + 3 + j, sibling)
                cp.start()
                passed.append(cp)
        for a, (x_ref, out_ref) in enumerate(zip(ins, outs)):
            for j, ch in enumerate(others):
                _remote(x_ref.at[c], out_ref.at[_chip_index(ch), 1 - c], send_sems, recv_sems, 6 * a + 3 + j, (x, y, c)).wait_recv()
        for cp in first + passed:
            cp.wait_send()

    outs = _exchange_call(name, body, shards, [jax.ShapeDtypeStruct((4,) + s.shape, s.dtype) for s in shards], 6)
    chip = 2 * lax.axis_index("x") + lax.axis_index("y")
    return [lax.dynamic_update_index_in_dim(o, s, chip, 0) for o, s in zip(outs, shards)]


def chip_scatter(name, parts):
    def body(ins, outs, send_sems, recv_sems):
        x, y, c, chip, others = _place()
        sends = []
        for a, (x_ref, out_ref) in enumerate(zip(ins, outs)):
            sends += [_remote(x_ref.at[_chip_index(ch)], out_ref.at[chip], send_sems, recv_sems, 3 * a + j, (*ch, c))
                      for j, ch in enumerate(others)]
        for cp in sends:
            cp.start()
        for a, (x_ref, out_ref) in enumerate(zip(ins, outs)):
            for j, ch in enumerate(others):
                _remote(x_ref.at[chip], out_ref.at[_chip_index(ch)], send_sems, recv_sems, 3 * a + j, (x, y, c)).wait_recv()
        for cp in sends:
            cp.wait_send()

    return _exchange_call(name, body, parts, [jax.ShapeDtypeStruct(p.shape, p.dtype) for p in parts], 3)


def sibling_take(name, halves):
    def body(ins, outs, send_sems, recv_sems):
        x, y, c, _, _ = _place()
        cps = [_remote(x_ref.at[s, 1 - c], out_ref.at[s], send_sems, recv_sems, 4 * a + s, (x, y, 1 - c))
               for a, (x_ref, out_ref) in enumerate(zip(ins, outs)) for s in range(4)]
        for cp in cps:
            cp.start()
        for cp in cps:
            cp.wait()

    return _exchange_call(name, body, halves,
                          [jax.ShapeDtypeStruct((h.shape[0],) + h.shape[2:], h.dtype) for h in halves], 4)


def sibling_pair(name, mine):
    def body(ins, outs, send_sems, recv_sems):
        x, y, c, _, _ = _place()
        cps = [_remote(x_ref, out_ref.at[c], send_sems, recv_sems, a, (x, y, 1 - c))
               for a, (x_ref, out_ref) in enumerate(zip(ins, outs))]
        for cp in cps:
            cp.start()
        for a, (x_ref, out_ref) in enumerate(zip(ins, outs)):
            _remote(x_ref, out_ref.at[1 - c], send_sems, recv_sems, a, (x, y, c)).wait_recv()
        for cp in cps:
            cp.wait_send()

    outs = _exchange_call(name, body, mine, [jax.ShapeDtypeStruct((2,) + m.shape, m.dtype) for m in mine], 1)
    core = lax.axis_index("c")
    return [lax.dynamic_update_index_in_dim(o, m, core, 0) for o, m in zip(outs, mine)]


def sum_slabs(name, r, out_dtype, tr):
    S, R, W = r.shape

    def body(*refs):
        t = refs[0][...].astype(F32)
        for s in range(1, S):
            t = t + refs[s][...].astype(F32)
        refs[S][...] = t.astype(out_dtype)

    return pl.pallas_call(
        body, grid=(R // tr,), in_specs=[pl.BlockSpec((None, tr, W), lambda i, s=s: (s, i, 0)) for s in range(S)],
        out_specs=pl.BlockSpec((tr, W), lambda i: (i, 0)), out_shape=jax.ShapeDtypeStruct((R, W), out_dtype),
        compiler_params=_cparams(("parallel",)), name=name)(*([r] * S))


def sum_chips(name, arrived, own, chip, tr):
    S, R, W = arrived.shape

    def body(chip_ref, own_ref, *refs):
        me = chip_ref[0]
        t = None
        for s in range(S):
            term = jnp.where(me == s, own_ref[...].astype(F32), refs[s][...].astype(F32))
            t = term if t is None else t + term
        refs[S][...] = t

    grid_spec = pltpu.PrefetchScalarGridSpec(
        num_scalar_prefetch=1, grid=(R // tr,),
        in_specs=[pl.BlockSpec((None, tr, W), lambda i, ch: (ch[0], i, 0))] +
                 [pl.BlockSpec((None, tr, W), lambda i, ch, s=s: (s, i, 0)) for s in range(S)],
        out_specs=pl.BlockSpec((tr, W), lambda i, ch: (i, 0)))
    return pl.pallas_call(body, grid_spec=grid_spec, out_shape=jax.ShapeDtypeStruct((R, W), F32),
                          compiler_params=_cparams(("parallel",)), name=name)(
                              chip.reshape(1).astype(jnp.int32), own, *([arrived] * S))


def add_kept_half(name, halves, got, c, tr):
    S, _, R, W = halves.shape

    def body(c_ref, h_ref, g_ref, o_ref):
        o_ref[...] = (h_ref[...].astype(F32) + g_ref[...].astype(F32)).astype(o_ref.dtype)

    grid_spec = pltpu.PrefetchScalarGridSpec(
        num_scalar_prefetch=1, grid=(S, R // tr),
        in_specs=[pl.BlockSpec((None, None, tr, W), lambda s, i, c_ref: (s, c_ref[0], i, 0)),
                  pl.BlockSpec((None, tr, W), lambda s, i, c_ref: (s, i, 0))],
        out_specs=pl.BlockSpec((None, tr, W), lambda s, i, c_ref: (s, i, 0)))
    return pl.pallas_call(body, grid_spec=grid_spec, out_shape=jax.ShapeDtypeStruct((S, R, W), halves.dtype),
                          compiler_params=_cparams(("parallel", "parallel")), name=name)(
                              c.reshape(1).astype(jnp.int32), halves, got)


def adamw_layer(name, l, g, row_off, w, m, v, prev, tr, tc=None, side=None):
    L, R, C = w.shape
    tc = C if tc is None else tc
    off = row_off // tr

    def body(g_ref, w_ref, m_ref, v_ref, *rest):
        outs = rest[-4:]
        gv = g_ref[...]
        d, mn, vn = f_adamw(w_ref[...], gv, m_ref[...], v_ref[...])
        for o, val in zip(outs, (gv, d, mn, vn)):
            o[...] = val

    slab = pl.BlockSpec((None, tr, tc), lambda i, j: (l, i, j))
    in_specs = [pl.BlockSpec((tr, tc), lambda i, j: (off + i, j)), slab, slab, slab]
    args = [g, w, m, v]
    aliases = {}
    if prev is not None:
        in_specs += [pl.BlockSpec(memory_space=pl.ANY)] * 4
        args += list(prev)
        aliases = {4 + k: k for k in range(4)}
    return _pcall(
        body, args, side, grid=(R // tr, C // tc), in_specs=in_specs, out_specs=[slab] * 4,
        out_shape=[jax.ShapeDtypeStruct((L, R, C), F32)] * 4, input_output_aliases=aliases,
        compiler_params=_cparams(("parallel", "parallel")), name=name)


BIG = ("w_in", "w_out", "xq", "xk", "xv", "xo", "w_up", "w_down")
GROUPS = {"mix": ("w_out", "xq", "xk", "xv", "xo"), "ff": ("w_up", "w_down")}
PACK_ROWS = {"w_out": 512, "xq": 512, "xk": 512, "xv": 512, "xo": 512, "w_up": 2048, "w_down": 2048}
GROUP_ROWS = {g: sum(PACK_ROWS[n] for n in names) for g, names in GROUPS.items()}
SUM_TILE = 256
PACK_OFF = {}
for _names in GROUPS.values():
    _o = 0
    for _n in _names:
        PACK_OFF[_n] = _o
        _o += PACK_ROWS[_n]

_SPLIT_OFF = np.cumsum((0,) + SPLIT_SIZES)
_KORDER = (0, 1, 2, 3, 4, 5, 6, 9, 10, 7, 8)


def w_in_to_kernel_cols(w):
    parts = [w[..., _SPLIT_OFF[i]:_SPLIT_OFF[i + 1]] for i in _KORDER]
    parts.append(jnp.zeros(w.shape[:-1] + (D_INP - D_IN,), w.dtype))
    return jnp.concatenate(parts, axis=-1)


def w_in_from_kernel_cols(w):
    offs = np.cumsum((0,) + tuple(SPLIT_SIZES[i] for i in _KORDER))
    pos = {k: (offs[n], offs[n + 1]) for n, k in enumerate(_KORDER)}
    return jnp.concatenate([w[..., pos[i][0]:pos[i][1]] for i in range(len(SPLIT_SIZES))], axis=-1)


def pack_shards(shards, group, dtype):
    return jnp.concatenate([shards[n].astype(dtype) for n in GROUPS[group]], axis=-2)


def unpack_rows(packed, name):
    return packed[..., PACK_OFF[name]:PACK_OFF[name] + PACK_ROWS[name], :]


WEIGHTS = ("rel_bias", "w_in", "w_out", "attn_sink", "gla_w2_f", "gla_b2_f", "gla_w2_b", "gla_b2_b", "gla_norm", "conv_w",
           "conv_b", "lru_wa", "lru_ba", "lru_wx", "lru_bx", "lru_lambda", "xq", "xk", "xv", "xo", "w_up", "w_down",
           "norm_mix_pre", "norm_mix_post", "norm_mem", "norm_x_pre", "norm_x_post", "norm_ff_pre", "norm_ff_post")
SMALL = tuple(n for n in WEIGHTS if n not in BIG)
SMALL_SHARDED = ("gla_w2_f", "gla_w2_b", "conv_w", "lru_ba", "lru_bx", "lru_lambda")
ROW_TILE = 256
SMALL_TILE = 512
W_IN_TILE = (344, 1024)
RIDE_PIECE_ROWS = (256, 512)


def _small_rows(shape):
    return -(-int(np.prod(shape)) // (SUBLANES * LANES)) * SUBLANES


def _pack_small(arrs):
    parts, total = [], 0
    for a in arrs:
        n, rows = int(np.prod(a.shape)), _small_rows(a.shape)
        a = a.astype(F32)
        if n == rows * LANES:
            parts.append(a.reshape(rows, LANES))
        else:
            parts.append(jnp.pad(a.reshape(-1), (0, rows * LANES - n)).reshape(rows, LANES))
        total += rows
    pad = -total % SMALL_TILE
    if pad:
        parts.append(jnp.zeros((pad, LANES), F32))
    return jnp.concatenate(parts, axis=0)


def _unpack_small(buf, shapes):
    lead = buf.shape[:-2]
    out, o = [], 0
    for s in shapes:
        n, rows = int(np.prod(s)), _small_rows(s)
        part = buf[..., o:o + rows, :]
        if n != rows * LANES:
            part = part.reshape(lead + (-1,))[..., :n]
        out.append(part.reshape(lead + tuple(s)))
        o += rows
    return out


def _relu2(r):
    return r, jnp.square(jnp.maximum(r, 0.0))


def _drelu2(r, u):
    return r * (2.0 * jnp.maximum(u, 0.0))


def kernel(x, mem, rel_bias, w_in, w_out, attn_sink, gla_w2_f, gla_b2_f, gla_w2_b, gla_b2_b, gla_norm, conv_w, conv_b, lru_wa, lru_ba, lru_wx, lru_bx, lru_lambda, xq, xk, xv, xo, w_up, w_down, norm_mix_pre, norm_mix_post, norm_mem, norm_x_pre, norm_x_post, norm_ff_pre, norm_ff_post, loss_target, m_rel_bias, m_w_in, m_w_out, m_attn_sink, m_gla_w2_f, m_gla_b2_f, m_gla_w2_b, m_gla_b2_b, m_gla_norm, m_conv_w, m_conv_b, m_lru_wa, m_lru_ba, m_lru_wx, m_lru_bx, m_lru_lambda, m_xq, m_xk, m_xv, m_xo, m_w_up, m_w_down, m_norm_mix_pre, m_norm_mix_post, m_norm_mem, m_norm_x_pre, m_norm_x_post, m_norm_ff_pre, m_norm_ff_post, v_rel_bias, v_w_in, v_w_out, v_attn_sink, v_gla_w2_f, v_gla_b2_f, v_gla_w2_b, v_gla_b2_b, v_gla_norm, v_conv_w, v_conv_b, v_lru_wa, v_lru_ba, v_lru_wx, v_lru_bx, v_lru_lambda, v_xq, v_xk, v_xv, v_xo, v_w_up, v_w_down, v_norm_mix_pre, v_norm_mix_post, v_norm_mem, v_norm_x_pre, v_norm_x_post, v_norm_ff_pre, v_norm_ff_post):
    w_args = (rel_bias, w_in, w_out, attn_sink, gla_w2_f, gla_b2_f, gla_w2_b, gla_b2_b, gla_norm, conv_w, conv_b, lru_wa,
              lru_ba, lru_wx, lru_bx, lru_lambda, xq, xk, xv, xo, w_up, w_down, norm_mix_pre, norm_mix_post, norm_mem,
              norm_x_pre, norm_x_post, norm_ff_pre, norm_ff_post)
    m_args = (m_rel_bias, m_w_in, m_w_out, m_attn_sink, m_gla_w2_f, m_gla_b2_f, m_gla_w2_b, m_gla_b2_b, m_gla_norm, m_conv_w,
              m_conv_b, m_lru_wa, m_lru_ba, m_lru_wx, m_lru_bx, m_lru_lambda, m_xq, m_xk, m_xv, m_xo, m_w_up, m_w_down,
              m_norm_mix_pre, m_norm_mix_post, m_norm_mem, m_norm_x_pre, m_norm_x_post, m_norm_ff_pre, m_norm_ff_post)
    v_args = (v_rel_bias, v_w_in, v_w_out, v_attn_sink, v_gla_w2_f, v_gla_b2_f, v_gla_w2_b, v_gla_b2_b, v_gla_norm, v_conv_w,
              v_conv_b, v_lru_wa, v_lru_ba, v_lru_wx, v_lru_bx, v_lru_lambda, v_xq, v_xk, v_xv, v_xo, v_w_up, v_w_down,
              v_norm_mix_pre, v_norm_mix_post, v_norm_mem, v_norm_x_pre, v_norm_x_post, v_norm_ff_pre, v_norm_ff_post)
    Wt, Mo, Vo = dict(zip(WEIGHTS, w_args)), dict(zip(WEIGHTS, m_args)), dict(zip(WEIGHTS, v_args))
    x, mem, tgt = x[0], mem[0], loss_target[0]
    D = D_MODEL
    depth = w_in.shape[0]
    chip = 2 * lax.axis_index("x") + lax.axis_index("y")
    core = lax.axis_index("c")

    sm_shapes = [Wt[n].shape for n in SMALL_SHARDED]
    g8 = all_gather8("gather_small_weights", _pack_small([Wt[n] for n in SMALL_SHARDED]))
    per_chip = _unpack_small(g8[0::2], sm_shapes)
    whole = {n: jnp.concatenate([p[j] for j in range(4)], axis=-1) for n, p in zip(SMALL_SHARDED, per_chip)}

    def group_shards(l, group):
        shard = pack_shards({n: Wt[n][l] for n in GROUPS[group]}, group, BF16).reshape(2, GROUP_ROWS[group] // 2, D)
        return [shard] + ([w_in[l].astype(BF16).reshape(2, D // 2, D_IN // 4)] if group == "mix" else [])

    def whole_weights(group, gathered):
        g = gathered[0].reshape(4, GROUP_ROWS[group], D)
        if group == "ff":
            return {"w_up": unpack_rows(g, "w_up"), "w_down": unpack_rows(g, "w_down").reshape(D_FF, D)}
        W = {n: unpack_rows(g, n).reshape(D, D) for n in GROUPS["mix"]}
        win = gathered[1].reshape(4, D, D_IN // 4).transpose(1, 0, 2).reshape(D, D_IN)
        W["w_in"] = w_in_to_kernel_cols(win)
        return W

    first = group_shards(0, "mix")
    Wgot = {(0, "mix"): whole_weights("mix", chip_gather("gather_first_weights", first))}
    gathers = Stream()
    riding = {}
    for l in range(depth):
        for group in ("mix", "ff"):
            if (l, group) != (0, "mix"):
                shards = group_shards(l, group)
                riding[l, group] = (shards, gathers.add(RidingExchange("gather", shards, RIDE_PIECE_ROWS)))

    def need_weights(l, group):
        if (l, group) not in Wgot:
            shards, exchange = riding.pop((l, group))
            got = gathers.finish(exchange, "gather_rest")
            Wgot[l, group] = whole_weights(
                group, [lax.dynamic_update_index_in_dim(b, s, chip, 0) for b, s in zip(got, shards)])
        return Wgot[l, group]

    bucket = t5_bucket_map()
    bias = bias_table_fwd(rel_bias, bucket)

    def gain(name, l):
        return Wt[name][l][None]

    def layer_params(l):
        w2fp = jnp.zeros((LANES, B_QK), F32).at[0:GATE_RANK].set(whole["gla_w2_f"][l])
        w2bp = jnp.zeros((LANES, B_QK), F32).at[GATE_RANK:2 * GATE_RANK].set(whole["gla_w2_b"][l])
        gate = [w2fp, gla_b2_f[l][None], w2bp, gla_b2_b[l][None]]
        lru = [whole["conv_w"][l], conv_b[l][None], lru_wa[l], whole["lru_ba"][l], lru_wx[l], whole["lru_bx"][l],
               whole["lru_lambda"][l]]
        return attn_sink[l].reshape(A_KV_HEADS, A_GROUP, 1), gate, gla_norm[l][None], lru

    saved = []
    xcur = x
    (h1,) = rowmap("norm_first", f_norm, [x], [gain("norm_mix_pre", 0)], [(D, BF16)], ROW_TILE)
    loss_acc = None
    Wfull = []
    ride = gathers.take
    for l in range(depth):
        W = dict(need_weights(l, "mix"))
        sink3, gate, gn, lru = layer_params(l)
        (proj,) = mm("mm_in", h1, W["w_in"], "nn", [(F32, 1)], pm=512, pn=1408, side=ride(2))
        oa = attn_fwd(proj, bias, sink3, side=ride(3))
        zrow, grow = (proj, LANES, OFF_Z // LANES), (proj, B_V, OFF_BG // B_V)
        laf, lab = rowmap("gla_gate", f_gla_gate, [zrow], gate, [(B_QK, F32), (B_QK, F32)], ROW_TILE)
        oraw = gla_fwd(proj, laf, lab, side=ride(2))
        (ob,) = rowmap("gla_post", f_gla_post, [oraw, grow], [gn], [(B_V, BF16)], ROW_TILE)
        oc = lru_fwd(proj, *lru, side=ride(1))
        cat = jnp.concatenate([oa, ob, oc], axis=1)
        (mixed,) = mm("mm_out", cat, W["w_out"], "nn", [(F32, 1)], side=ride(1))
        x1, h2 = rowmap("resnorm_mix", f_resnorm, [xcur, mixed], [gain("norm_mix_post", l), gain("norm_x_pre", l)],
                        [(D, F32), (D, BF16)], ROW_TILE)
        (memn,) = rowmap("norm_mem", f_norm, [mem], [gain("norm_mem", l)], [(D, BF16)], ROW_TILE)
        (q,) = mm("mm_xq", h2, W["xq"], "nn", [(BF16, 1)], side=ride(1))
        (k,) = mm("mm_xk", memn, W["xk"], "nn", [(F32, 1)])
        (v,) = mm("mm_xv", memn, W["xv"], "nn", [(F32, 1)])
        (o,) = rowmap("xattn", f_xattn, [q], [k, v], [(D, BF16)], ROW_TILE)
        (xo_out,) = mm("mm_xo", o, W["xo"], "nn", [(F32, 1)], side=ride(1))
        x2, h3 = rowmap("resnorm_x", f_resnorm, [x1, xo_out], [gain("norm_x_post", l), gain("norm_ff_pre", l)],
                        [(D, F32), (D, BF16)], ROW_TILE)
        W.update(need_weights(l, "ff"))
        Wfull.append(W)
        u, act = mm("mm_up", h3, W["w_up"], "nn", [(F32, 1), (BF16, 1)], epilogue=_relu2, side=ride(3))
        (ff,) = mm("mm_down", act, W["w_down"], "nn", [(F32, 1)], side=ride(2))
        saved.append(dict(x0=xcur, h1=h1, proj=proj, laf=laf, lab=lab, oraw=oraw, cat=cat, mixed=mixed, x1=x1, h2=h2,
                          memn=memn, q=q, k=k, v=v, o=o, xo_out=xo_out, x2=x2, h3=h3, u=u, act=act, ff=ff))
        if l < depth - 1:
            xcur, h1 = rowmap("resnorm_ff", f_resnorm, [x2, ff], [gain("norm_ff_post", l), gain("norm_mix_pre", l + 1)],
                              [(D, F32), (D, BF16)], ROW_TILE)
        else:
            (loss_acc,) = rowmap("final_loss", f_final_loss, [x2, ff, tgt], [gain("norm_ff_post", l)], [], ROW_TILE,
                                 accs=[(1, 1)])
    loss = lax.psum(loss_acc[0, 0], ("x", "y", "c"))

    small_g = {n: [None] * depth for n in SMALL if n != "rel_bias"}
    adam = {}
    dbias_all = []
    dx_next = dh1_next = None
    grad_x = None
    scatters = Stream()
    ride = scatters.take
    inflight = []
    updates = []

    def start_reduce(lyr, group, dW):
        if group == "ff":
            pack = dW["ff"]
        else:
            pack = pack_shards({n: dW[n].reshape(4, D // 4, D) for n in GROUPS["mix"]}, group, BF16)
        halves = [pack.reshape(4, 2, GROUP_ROWS[group] // 2, D)]
        if group == "mix":
            g_in = w_in_from_kernel_cols(dW["w_in"]).reshape(D, 4, D_IN // 4).transpose(1, 0, 2)
            halves.append(g_in.reshape(4, 2, D // 2, D_IN // 4))
        got = sibling_take("reduce_to_half_owner", halves)
        sums = [add_kept_half("reduce_chip_sum", h, g, core, SUM_TILE) for h, g in zip(halves, got)]
        inflight.append((lyr, group, sums, scatters.add(RidingExchange("scatter", sums, RIDE_PIECE_ROWS))))

    def finish_reduce():
        lyr, group, sums, exchange = inflight.pop(0)
        arrived = scatters.finish(exchange, "reduce_rest")
        totals = [sum_chips("reduce_sum_chips", a, s, chip, SUM_TILE) for a, s in zip(arrived, sums)]
        fulls = sibling_pair("reduce_share_halves", totals)
        full = fulls[0].reshape(GROUP_ROWS[group], D)
        updates.extend((n, lyr, full, PACK_OFF[n]) for n in GROUPS[group])
        if group == "mix":
            updates.append(("w_in", lyr, fulls[1].reshape(D, D_IN // 4).T, 0))

    w_in_t = [jnp.swapaxes(a, 1, 2) for a in (w_in, m_w_in, v_w_in)]

    def run_updates(carry):
        updates.sort(key=lambda u: (u[1] == 0, -PACK_ROWS.get(u[0], 1024)))
        while updates:
            n, lyr, g, off = updates.pop(0)
            side = ride(1) if carry else None
            if n == "w_in":
                adam[n] = adamw_layer("adamw_w_in", lyr, g, 0, *w_in_t, adam.get(n), W_IN_TILE[0], W_IN_TILE[1], side=side)
            else:
                adam[n] = adamw_layer("adamw_" + n, lyr, g, off, Wt[n], Mo[n], Vo[n], adam.get(n), SUM_TILE, side=side)

    for l in reversed(range(depth)):
        W, S = Wfull[l], saved[l]
        sink3, gate, gn, lru = layer_params(l)
        if l == depth - 1:
            (dx2, dff), (dgp,) = rowmap_bwd("final_bwd", f_final_rows, [S["x2"], S["ff"], tgt], [gain("norm_ff_post", l)],
                                            [None], ROW_TILE, [F32, F32, None], [True])
        else:
            (dx2, dff), (dgp, dgn_next) = rowmap_bwd(
                "resnorm_ff_bwd", f_resnorm, [S["x2"], S["ff"]], [gain("norm_ff_post", l), gain("norm_mix_pre", l + 1)],
                [dx_next, dh1_next], ROW_TILE, [F32, F32], [True, True])
            small_g["norm_mix_pre"][l + 1] = dgn_next[0]
        small_g["norm_ff_post"][l] = dgp[0]
        dW = {}
        (du,) = mm("mm_down_bwd", dff, W["w_down"], "nt", [(BF16, 1)], epilogue=_drelu2, extras=[S["u"]], side=ride(3))
        (pack,) = mm("mm_down_wgrad", S["act"], dff, "tn",
                     [(BF16, ("rows", GROUP_ROWS["ff"], PACK_OFF["w_down"], None))], side=ride(2))
        (dW["ff"],) = mm("mm_up_wgrad", S["h3"], du, "tn",
                         [(BF16, ("cols", GROUP_ROWS["ff"], PACK_OFF["w_up"], pack))], side=ride(2))
        if inflight:
            finish_reduce()
        start_reduce(l, "ff", dW)
        (dh3,) = mm("mm_up_bwd", du, W["w_up"], "nt", [(F32, 1)], pk=D, side=ride(2))
        (dx1, dxo_out), (dg1, dg2) = rowmap_bwd(
            "resnorm_x_bwd", f_resnorm, [S["x1"], S["xo_out"]], [gain("norm_x_post", l), gain("norm_ff_pre", l)],
            [dx2, dh3], ROW_TILE, [F32, F32], [True, True])
        small_g["norm_x_post"][l], small_g["norm_ff_pre"][l] = dg1[0], dg2[0]
        (do,) = mm("mm_xo_bwd", dxo_out, W["xo"], "nt", [(F32, 1)])
        (dW["xo"],) = mm("mm_xo_wgrad", S["o"], dxo_out, "tn", [(BF16, 1)])
        (dq,), (dk, dv) = rowmap_bwd("xattn_bwd", f_xattn, [S["q"]], [S["k"], S["v"]], [do], ROW_TILE, [BF16], [True, True])
        (dW["xq"],) = mm("mm_xq_wgrad", S["h2"], dq, "tn", [(BF16, 1)])
        (dh2,) = mm("mm_xq_bwd", dq, W["xq"], "nt", [(F32, 1)])
        (dW["xk"],) = mm("mm_xk_wgrad", S["memn"], dk, "tn", [(BF16, 1)])
        (dW["xv"],) = mm("mm_xv_wgrad", S["memn"], dv, "tn", [(BF16, 1)])
        (dmk,) = mm("mm_xk_bwd", dk, W["xk"], "nt", [(F32, 1)])
        (dmv,) = mm("mm_xv_bwd", dv, W["xv"], "nt", [(F32, 1)])
        _, (dgm,) = rowmap_bwd("norm_mem_bwd", f_norm_twice, [mem], [gain("norm_mem", l)], [dmk, dmv], ROW_TILE, [None], [True])
        small_g["norm_mem"][l] = dgm[0]
        (dx0, dmixed), (dg1, dg2) = rowmap_bwd(
            "resnorm_mix_bwd", f_resnorm, [S["x0"], S["mixed"]], [gain("norm_mix_post", l), gain("norm_x_pre", l)],
            [dx1, dh2], ROW_TILE, [F32, F32], [True, True])
        small_g["norm_mix_post"][l], small_g["norm_x_pre"][l] = dg1[0], dg2[0]
        (dcat,) = mm("mm_out_bwd", dmixed, W["w_out"], "nt", [(F32, 1)])
        (dW["w_out"],) = mm("mm_out_wgrad", S["cat"], dmixed, "tn", [(BF16, 1)])
        proj = S["proj"]
        daq, dak, dav, dbias, dsink = attn_bwd(proj, bias, sink3, dcat, side=ride(3))
        dbias_all.append(dbias)
        small_g["attn_sink"][l] = dsink.reshape(A_HEADS)
        zrow, grow = (proj, LANES, OFF_Z // LANES), (proj, B_V, OFF_BG // B_V)
        (doraw, dbg), (dgn,) = rowmap_bwd("gla_post_bwd", f_gla_post, [S["oraw"], grow], [gn], [(dcat, B_V, A_Q // B_V)],
                                          ROW_TILE, [F32, BF16], [True])
        dbq, dbk, dbv, dlaf, dlab = gla_bwd(proj, S["laf"], S["lab"], doraw, side=ride(3))
        (dz,), (dw2fp, db2f, dw2bp, db2b) = rowmap_bwd("gla_gate_bwd", f_gla_gate, [zrow], gate, [dlaf, dlab], ROW_TILE,
                                                        [BF16], [True] * 4)
        small_g["gla_norm"][l] = dgn[0]
        small_g["gla_w2_f"][l], small_g["gla_b2_f"][l] = dw2fp[0:GATE_RANK], db2f[0]
        small_g["gla_w2_b"][l], small_g["gla_b2_b"][l] = dw2bp[GATE_RANK:2 * GATE_RANK], db2b[0]
        dcx, dcy, dcw, dcb, dwa, dba, dwx, dbx, dlam = lru_bwd(proj, *lru, dcat, side=ride(2))
        small_g["conv_w"][l], small_g["conv_b"][l] = dcw, dcb[0]
        small_g["lru_wa"][l], small_g["lru_ba"][l], small_g["lru_wx"][l] = dwa, dba, dwx
        small_g["lru_bx"][l], small_g["lru_lambda"][l] = dbx, dlam
        dproj = jnp.concatenate([daq, dak, dav, dbq, dbk, dbv, dbg, dcx, dcy, dz], axis=1)
        (dW["w_in"],) = mm("mm_in_wgrad", S["h1"], dproj, "tn", [(BF16, 1)], pm=512, pn=1408, side=ride(1))
        (dh1,) = mm("mm_in_bwd", dproj, W["w_in"], "nt", [(F32, 1)], side=ride(1))
        if l > 0:
            dx_next, dh1_next = dx0, dh1
        else:
            (grad_x,), (dg0,) = rowmap_bwd("norm_first_bwd", f_norm_keep, [x], [gain("norm_mix_pre", 0)], [dx0, dh1],
                                           ROW_TILE, [F32], [True])
            small_g["norm_mix_pre"][0] = dg0[0]

        finish_reduce()
        start_reduce(l, "mix", dW)
    run_updates(carry=True)
    finish_reduce()
    run_updates(carry=False)

    (dtab,) = (bias_table_bwd(dbias_all, bucket),)
    sg = {n: jnp.stack(small_g[n]) for n in small_g}
    sg["rel_bias"] = dtab[:, :A_HEADS]
    sg_shapes = [sg[n].shape for n in SMALL]
    contributions = all_gather8("gather_small_grads", _pack_small([sg[n] for n in SMALL]))
    (sg_sum,) = (sum_slabs("sum_small_grads", contributions, F32, SMALL_TILE),)
    sg = dict(zip(SMALL, _unpack_small(sg_sum, sg_shapes)))
    for n in SMALL_SHARDED:
        w = Wt[n].shape[-1]
        sg[n] = lax.dynamic_slice_in_dim(sg[n], chip * w, w, axis=sg[n].ndim - 1)

    grads, delta, new_m, new_v = {}, {}, {}, {}
    adam["w_in"] = [jnp.swapaxes(a, 1, 2) for a in adam["w_in"]]
    for n in BIG:
        grads[n], delta[n], new_m[n], new_v[n] = adam[n]
    shapes = [Wt[n].shape for n in SMALL]
    packs = [_pack_small([src[n] for n in SMALL]) for src in (Wt, sg, Mo, Vo)]
    d_, m_, v_ = rowmap("adamw_small", f_adamw, packs, [], [(LANES, F32)] * 3, SMALL_TILE)
    for n, a, b, c_ in zip(SMALL, _unpack_small(d_, shapes), _unpack_small(m_, shapes), _unpack_small(v_, shapes)):
        grads[n], delta[n], new_m[n], new_v[n] = sg[n], a, b, c_

    return (loss, grad_x[None], *[grads[n] for n in WEIGHTS], *[delta[n] for n in WEIGHTS],
            *[new_m[n] for n in WEIGHTS], *[new_v[n] for n in WEIGHTS])
```

```python
import functools
import math

import numpy as np
import jax
import jax.numpy as jnp
from jax import lax
from jax.experimental import pallas as pl
from jax.experimental.pallas import tpu as pltpu

F32, BF16 = jnp.float32, jnp.bfloat16
HI = lax.Precision.HIGHEST
MESH = pl.DeviceIdType.MESH

VMEM_LIMIT_BYTES = 56 * 1024 * 1024
LANES = 128
SUBLANES = 8

D_MODEL = 2048
DEPTH = 4
A_HEAD_DIM = 128
A_HEADS = 8
A_KV_HEADS = 2
A_GROUP = 4
WINDOW = 128
BLOCK = 128
N_BUCKETS = 32
MAX_DISTANCE = 128
B_HEADS = 4
B_KEY_DIM = 64
B_VAL_DIM = 128
GATE_RANK = 16
GATE_TAU = 16.0
C_WIDTH = 512
C_BLOCKS = 4
C_BLOCK_DIM = 128
CONV_WIDTH = 4
CONV_LEFT = 2
LRU_C = 8.0
X_HEADS = 4
X_HEAD_DIM = 512
D_FF = 4 * D_MODEL
EPS = 1e-6
NEG_INF = -1e30
A_Q, A_KV, B_QK, B_V = 1024, 256, 256, 512
SPLIT_SIZES = (A_Q, A_KV, A_KV, B_QK, B_QK, B_V, B_V, GATE_RANK, GATE_RANK, C_WIDTH, C_WIDTH)
D_IN = sum(SPLIT_SIZES)
D_INP = 4224
OFF_AQ, OFF_AK, OFF_AV, OFF_BQ, OFF_BK, OFF_BV, OFF_BG, OFF_CX, OFF_CY, OFF_Z = (
    0, 1024, 1280, 1536, 1792, 2048, 2560, 3072, 3584, 4096)
GLA_CHUNK = 128

ADAM_LR, ADAM_B1, ADAM_B2, ADAM_EPS, ADAM_WD, ADAM_STEP = 0.001, 0.9, 0.999, 1e-08, 0.01, 10


def _cparams(sem=None):
    return pltpu.CompilerParams(dimension_semantics=sem, vmem_limit_bytes=VMEM_LIMIT_BYTES)


def _full_spec(a):
    nd = a.ndim
    return pl.BlockSpec(a.shape, lambda *_: (0,) * nd)


def _tup(r):
    return r if isinstance(r, tuple) else (r,)


HBM_SPEC = pl.BlockSpec(memory_space=pltpu.HBM)


def _place():
    x, y, c = lax.axis_index("x"), lax.axis_index("y"), lax.axis_index("c")
    others = [(1 - x, y), (x, 1 - y), (1 - x, 1 - y)]
    return x, y, c, 2 * x + y, others


def _remote(src, dst, send_sems, recv_sems, k, to):
    return pltpu.make_async_remote_copy(src_ref=src, dst_ref=dst, send_sem=send_sems.at[k], recv_sem=recv_sems.at[k],
                                        device_id=to, device_id_type=MESH)


def _chip_index(ch):
    return 2 * ch[0] + ch[1]


def _pcall(body, args, side=None, **kw):
    if side is None:
        res = pl.pallas_call(body, **kw)(*args)
        return list(res) if isinstance(res, (list, tuple)) else [res]
    single = not isinstance(kw["out_shape"], (list, tuple))
    out_shape = [kw.pop("out_shape")] if single else list(kw.pop("out_shape"))
    out_specs = [kw.pop("out_specs")] if single else list(kw.pop("out_specs"))
    in_specs = list(kw.pop("in_specs"))
    scratch = list(kw.pop("scratch_shapes", ()))
    grid = kw.get("grid", ())
    n_in, n_out, n_scr = len(in_specs), len(out_shape), len(scratch)
    srcs, bufs = side.srcs, side.bufs
    ns, nb = len(srcs), len(bufs)

    def wrapped(*refs):
        ins = refs[:n_in]
        src_refs = refs[n_in:n_in + ns]
        o0 = n_in + ns + nb
        outs = refs[o0:o0 + n_out]
        buf_refs = refs[o0 + n_out:o0 + n_out + nb]
        s0 = o0 + n_out + nb
        scr = refs[s0:s0 + n_scr]
        send_sems, recv_sems = refs[s0 + n_scr], refs[s0 + n_scr + 1]
        first = last = None
        for d, n in enumerate(grid):
            f, l_ = pl.program_id(d) == 0, pl.program_id(d) == n - 1
            first = f if first is None else first & f
            last = l_ if last is None else last & l_
        if first is None:
            side.start(src_refs, buf_refs, send_sems, recv_sems)
            body(*ins, *outs, *scr)
            side.finish(src_refs, buf_refs, send_sems, recv_sems)
            return
        pl.when(first)(lambda: side.start(src_refs, buf_refs, send_sems, recv_sems))
        body(*ins, *outs, *scr)
        pl.when(last)(lambda: side.finish(src_refs, buf_refs, send_sems, recv_sems))

    any_spec = pl.BlockSpec(memory_space=pl.ANY)
    aliases = dict(kw.pop("input_output_aliases", {}))
    aliases.update({n_in + ns + i: n_out + i for i in range(nb)})
    cp = kw.pop("compiler_params", None)
    if grid:
        cp = _cparams(("arbitrary",) * len(grid))
    res = pl.pallas_call(
        wrapped, in_specs=in_specs + [any_spec] * (ns + nb), out_specs=out_specs + [any_spec] * nb,
        out_shape=out_shape + [jax.ShapeDtypeStruct(b.shape, b.dtype) for b in bufs],
        scratch_shapes=scratch + [pltpu.SemaphoreType.DMA((side.n_sems,)), pltpu.SemaphoreType.DMA((side.n_sems,))],
        input_output_aliases=aliases, compiler_params=cp, **kw)(*args, *srcs, *bufs)
    side.done(list(res[n_out:]))
    return list(res[:n_out])


class _Side:
    def __init__(self, parts):
        self.parts = parts
        self.srcs = [s for p in parts for s in p[0].srcs]
        self.bufs = [b for p in parts for b in p[0].bufs]
        self.n_sems = max(1, sum(3 * len(now) + 4 * len(relay) + len(last) for _, now, relay, last in parts))

    def done(self, bufs):
        for p in self.parts:
            p[0].bufs, bufs = bufs[:len(p[0].bufs)], bufs[len(p[0].bufs):]

    def _copies(self, src_refs, buf_refs, send_sems, recv_sems):
        x, y, c, chip, others = _place()
        xn, yn, dg = others
        me, sibling = (x, y, c), (x, y, 1 - c)
        mine, landing = [], []
        k = o = 0

        def pair(src, dst, got, to):
            nonlocal k
            mine.append(_remote(src, dst, send_sems, recv_sems, k, to))
            landing.append(_remote(src, got, send_sems, recv_sems, k, me))
            k += 1

        for ex, now, relay, last in self.parts:
            srcs, bufs = src_refs[o:o + len(ex.srcs)], buf_refs[o:o + len(ex.srcs)]
            o += len(ex.srcs)
            for a, r0, n in now:
                rows = pl.ds(r0, n)
                if ex.kind == "gather":
                    for ch in (xn, yn):
                        pair(srcs[a].at[c, rows], bufs[a].at[chip, c, rows], bufs[a].at[_chip_index(ch), c, rows], (*ch, c))
                else:
                    for ch in others:
                        pair(srcs[a].at[_chip_index(ch), rows], bufs[a].at[chip, rows], bufs[a].at[_chip_index(ch), rows],
                             (*ch, c))
            for a, r0, n in relay:
                top, bottom, rows = pl.ds(r0, n // 2), pl.ds(r0 + n // 2, n // 2), pl.ds(r0, n)
                from_x, from_y = bufs[a].at[_chip_index(xn), c, top], bufs[a].at[_chip_index(yn), c, bottom]
                pair(from_x, from_x, bufs[a].at[_chip_index(dg), c, top], (*yn, c))
                pair(from_y, from_y, bufs[a].at[_chip_index(dg), c, bottom], (*xn, c))
                for ch in (xn, yn):
                    here = bufs[a].at[_chip_index(ch), c, rows]
                    pair(here, here, bufs[a].at[_chip_index(ch), 1 - c, rows], sibling)
            for a, r0, n in last:
                here = bufs[a].at[_chip_index(dg), c, pl.ds(r0, n)]
                pair(here, here, bufs[a].at[_chip_index(dg), 1 - c, pl.ds(r0, n)], sibling)
        return mine, landing

    def start(self, src_refs, buf_refs, send_sems, recv_sems):
        for cp in self._copies(src_refs, buf_refs, send_sems, recv_sems)[0]:
            cp.start()

    def finish(self, src_refs, buf_refs, send_sems, recv_sems):
        mine, landing = self._copies(src_refs, buf_refs, send_sems, recv_sems)
        for cp in landing:
            cp.wait_recv()
        for cp in mine:
            cp.wait_send()


class RidingExchange:
    def __init__(self, kind, srcs, piece_rows):
        self.kind, self.srcs = kind, list(srcs)
        lead = (4, 2) if kind == "gather" else (4,)
        self.bufs = [lax.empty(lead + s.shape[1:], s.dtype) for s in srcs]
        heights = [s.shape[1] for s in srcs]
        per = [[(a, r0, min(pr, h - r0)) for r0 in range(0, h, pr)] for a, (h, pr) in enumerate(zip(heights, piece_rows))]
        self.pieces = list(per[0])
        for extra in per[1:]:
            step = max(1, len(self.pieces) // (len(extra) + 1))
            for i, p in enumerate(extra):
                self.pieces.insert(min(len(self.pieces), (i + 1) * step + i), p)
        self.landed, self.relayed = [], []

    def busy(self):
        return bool(self.pieces or self.landed or self.relayed)

    def step(self, n):
        out = []
        for a, r0, rows in self.pieces[:n]:
            if out and out[-1][0] == a and out[-1][1] + out[-1][2] == r0:
                out[-1] = (a, out[-1][1], out[-1][2] + rows)
            else:
                out.append((a, r0, rows))
        self.pieces = self.pieces[n:]
        relay, last = self.landed, self.relayed
        self.landed, self.relayed = (out if self.kind == "gather" else []), relay
        return out, relay, last


class Stream:
    def __init__(self):
        self.queue = []

    def add(self, exchange):
        self.queue.append(exchange)
        return exchange

    def take(self, n, only=None):
        parts = []
        for ex in (self.queue if only is None else [only]):
            had = len(ex.pieces)
            now, relay, last = ex.step(n)
            n -= had - len(ex.pieces)
            if now or relay or last:
                parts.append((ex, now, relay, last))
        return _Side(parts) if parts else None

    def finish(self, exchange, name):
        while exchange.busy():
            side = self.take(len(exchange.pieces), only=exchange)
            _pcall(lambda: None, [], side, in_specs=[], out_specs=[], out_shape=[], name=name)
        self.queue.remove(exchange)
        return exchange.bufs


def _row_ops(rows, tr):
    arrs, specs, widths = [], [], []
    for r in rows:
        arr, n, j = r if isinstance(r, tuple) else (r, r.shape[1], 0)
        arrs.append(arr)
        widths.append(n)
        specs.append(pl.BlockSpec((tr, n), lambda i, j=j: (i, j)))
    return arrs, specs, widths


def rowmap(name, f, rows, params, outs, tr, accs=()):
    rows, row_specs, _ = _row_ops(rows, tr)
    T = rows[0].shape[0]
    nin, nout, nacc = len(rows) + len(params), len(outs), len(accs)

    def body(*refs):
        res = _tup(f(*[r[...] for r in refs[:nin]]))
        for o, r in zip(refs[nin:nin + nout], res[:nout]):
            o[...] = r.astype(o.dtype)
        arefs = refs[nin + nout:]
        if nacc:
            @pl.when(pl.program_id(0) == 0)
            def _():
                for a in arefs:
                    a[...] = jnp.zeros(a.shape, a.dtype)
            for a, r in zip(arefs, res[nout:]):
                a[...] += r.astype(F32)

    in_specs = row_specs + [_full_spec(p) for p in params]
    out_specs = [pl.BlockSpec((tr, n), lambda i: (i, 0)) for n, _ in outs] + \
                [pl.BlockSpec(s, lambda i, nd=len(s): (0,) * nd) for s in accs]
    out_shape = [jax.ShapeDtypeStruct((T, n), d) for n, d in outs] + [jax.ShapeDtypeStruct(s, F32) for s in accs]
    res = pl.pallas_call(body, grid=(T // tr,), in_specs=in_specs, out_specs=out_specs, out_shape=out_shape,
                         compiler_params=_cparams(("arbitrary",)), name=name)(*rows, *params)
    return tuple(res)


def rowmap_bwd(name, f, rows, params, cots, tr, drow_dtypes, want_params):
    rows, row_specs, widths = _row_ops(rows, tr)
    T = rows[0].shape[0]
    nr, npar = len(rows), len(params)
    cot_arrays, cot_specs, _ = _row_ops([c for c in cots if c is not None], tr)
    nc = len(cot_arrays)
    ridx = [i for i, d in enumerate(drow_dtypes) if d is not None]
    pidx = [i for i, w in enumerate(want_params) if w]

    def body(*refs):
        rvals = [r[...] for r in refs[:nr]]
        pvals = [r[...] for r in refs[nr:nr + npar]]
        crefs = list(refs[nr + npar:nr + npar + nc])
        orefs = refs[nr + npar + nc:]
        outs, vjp = jax.vjp(f, *rvals, *pvals)
        outs = _tup(outs)
        cts = []
        for c, o in zip(cots, outs):
            cts.append(jnp.ones(o.shape, o.dtype) if c is None else crefs.pop(0)[...].astype(o.dtype))
        grads = vjp(tuple(cts) if len(cts) > 1 else cts[0])
        for o, i in zip(orefs[:len(ridx)], ridx):
            o[...] = grads[i].astype(o.dtype)
        prefs = orefs[len(ridx):]
        if prefs:
            @pl.when(pl.program_id(0) == 0)
            def _():
                for a in prefs:
                    a[...] = jnp.zeros(a.shape, a.dtype)
            for a, i in zip(prefs, pidx):
                a[...] += grads[nr + i].astype(F32)

    in_specs = row_specs + [_full_spec(p) for p in params] + cot_specs
    out_specs = [pl.BlockSpec((tr, widths[i]), lambda i: (i, 0)) for i in ridx] + [_full_spec(params[i]) for i in pidx]
    out_shape = [jax.ShapeDtypeStruct((T, widths[i]), drow_dtypes[i]) for i in ridx] + \
                [jax.ShapeDtypeStruct(params[i].shape, F32) for i in pidx]
    res = pl.pallas_call(body, grid=(T // tr,), in_specs=in_specs, out_specs=out_specs, out_shape=out_shape,
                         compiler_params=_cparams(("arbitrary",)), name=name)(*rows, *params, *cot_arrays)
    res = tuple(res)
    return res[:len(ridx)], res[len(ridx):]


def _pick(n, pref):
    best = None
    for d in range(LANES, min(n, pref) + 1, LANES):
        if n % d == 0:
            best = d
    return best if best is not None else n


def _spec2(arr, tile, pos):
    tr, tc = tile
    if arr.ndim == 2:
        return pl.BlockSpec((tr, tc), lambda i, j, k: pos(i, j, k))
    assert arr.shape[2] % tc == 0, (arr.shape, tile)
    per = arr.shape[2] // tc

    def imap(i, j, k):
        r, c = pos(i, j, k)
        return (c // per, r, c % per)
    return pl.BlockSpec((None, tr, tc), imap)


def _dims2(arr):
    return (arr.shape[0], arr.shape[1]) if arr.ndim == 2 else (arr.shape[1], arr.shape[0] * arr.shape[2])


def mm(name, a, b, mode, outs, epilogue=None, extras=(), pm=1024, pn=512, pk=4224, side=None):
    ar, ac = _dims2(a)
    br, bc = _dims2(b)
    if mode == "nn":
        M, K, N = ar, ac, bc
    elif mode == "nt":
        M, K, N = ar, ac, br
    else:
        M, K, N = ac, ar, bc
    tm, tn, tk = _pick(M, pm), _pick(N, pn), _pick(K, pk)
    for arr in (a, b) + tuple(extras):
        if arr.ndim == 3:
            assert arr.shape[2] % LANES == 0
    if mode == "nn":
        a_spec = _spec2(a, (tm, tk), lambda i, j, k: (i, k))
        b_spec = _spec2(b, (tk, tn), lambda i, j, k: (k, j))
        dims = (((1,), (0,)), ((), ()))
    elif mode == "nt":
        a_spec = _spec2(a, (tm, tk), lambda i, j, k: (i, k))
        b_spec = _spec2(b, (tn, tk), lambda i, j, k: (j, k))
        dims = (((1,), (1,)), ((), ()))
    else:
        a_spec = _spec2(a, (tk, tm), lambda i, j, k: (k, i))
        b_spec = _spec2(b, (tk, tn), lambda i, j, k: (k, j))
        dims = (((0,), (0,)), ((), ()))
    nk = K // tk
    nex = len(extras)

    def body(*refs):
        a_ref, b_ref = refs[0], refs[1]
        ex_refs = refs[2:2 + nex]
        o_refs = refs[2 + nex:2 + nex + len(outs)]
        acc = refs[-1]
        k = pl.program_id(2)
        part = lax.dot_general(a_ref[...].astype(BF16), b_ref[...].astype(BF16), dims, preferred_element_type=F32)

        def finish(r):
            res = (r,) if epilogue is None else _tup(epilogue(r, *[e[...] for e in ex_refs]))
            for o, v in zip(o_refs, res):
                o[...] = v.astype(o.dtype)

        if nk == 1:
            finish(part)
            return

        @pl.when(k == 0)
        def _():
            acc[...] = part

        @pl.when(k > 0)
        def _():
            acc[...] += part

        @pl.when(k == nk - 1)
        def _():
            finish(acc[...])

    out_shape, out_specs = [], []
    args, in_specs, aliases = [a, b, *extras], [a_spec, b_spec], {}
    in_specs += [_spec2(e, (tm, tn), lambda i, j, k: (i, j)) for e in extras]
    for dt, chunks in outs:
        if isinstance(chunks, tuple):
            how, rows, off, buf = chunks
            o = jax.ShapeDtypeStruct((4, rows, D_MODEL), dt)
            assert off % tm == 0 and D_MODEL % tn == 0 and D_MODEL % tm == 0
            if how == "cols":
                per = D_MODEL // tn
                spec = pl.BlockSpec((None, tm, tn), lambda i, j, k: (j // per, off // tm + i, j % per))
            else:
                per = D_MODEL // tm
                spec = pl.BlockSpec((None, tm, tn), lambda i, j, k: (i // per, off // tm + i % per, j))
            if buf is not None:
                aliases[len(args)] = len(out_shape)
                args.append(buf)
                in_specs.append(pl.BlockSpec(memory_space=pl.ANY))
        else:
            o = jax.ShapeDtypeStruct((M, N) if chunks == 1 else (chunks, M, N // chunks), dt)
            spec = _spec2(o, (tm, tn), lambda i, j, k: (i, j))
        out_shape.append(o)
        out_specs.append(spec)
    n_extra_in = len(args) - 2 - nex
    res = _pcall(
        (lambda *refs: body(*refs[:2 + nex], *refs[2 + nex + n_extra_in:])) if n_extra_in else body, args, side,
        grid=(M // tm, N // tn, nk), in_specs=in_specs, out_specs=out_specs, out_shape=out_shape,
        scratch_shapes=[pltpu.VMEM((tm, tn), F32)] if nk > 1 else [], input_output_aliases=aliases,
        compiler_params=_cparams(("parallel", "parallel", "arbitrary")), name=name)
    return tuple(res)


def _rms(x, g):
    return x * lax.rsqrt(jnp.mean(x * x, axis=-1, keepdims=True) + EPS) * g


def f_norm(x, g):
    return _rms(x, g)


def f_norm_keep(x, g):
    return x, _rms(x, g)


def f_resnorm(xp, m, gpost, gnext):
    xn = xp + _rms(m, gpost)
    return xn, _rms(xn, gnext)


def f_final_rows(xp, m, tgt, gpost):
    xn = xp + _rms(m, gpost)
    return 0.5 * jnp.mean(jnp.square(xn - tgt), axis=-1, keepdims=True)


def f_final_loss(xp, m, tgt, gpost):
    return jnp.sum(f_final_rows(xp, m, tgt, gpost), axis=0, keepdims=True)


def f_norm_twice(x, g):
    y = _rms(x, g)
    return y, y


def f_xattn(q, k, v):
    outs = []
    for h in range(X_HEADS):
        sl = slice(h * X_HEAD_DIM, (h + 1) * X_HEAD_DIM)
        s = lax.dot_general(q[:, sl].astype(BF16), k[:, sl].astype(BF16), (((1,), (1,)), ((), ())),
                            preferred_element_type=F32) * (X_HEAD_DIM ** -0.5)
        m = jnp.max(s, axis=-1, keepdims=True)
        e = jnp.exp(s - m)
        p = e / jnp.sum(e, axis=-1, keepdims=True)
        outs.append(jnp.dot(p.astype(BF16), v[:, sl].astype(BF16), preferred_element_type=F32))
    return jnp.concatenate(outs, axis=1)


def f_adamw(w, g, m, v):
    m = ADAM_B1 * m + (1.0 - ADAM_B1) * g
    v = ADAM_B2 * v + (1.0 - ADAM_B2) * jnp.square(g)
    m_hat = m / (1.0 - ADAM_B1 ** ADAM_STEP)
    v_hat = v / (1.0 - ADAM_B2 ** ADAM_STEP)
    delta = -ADAM_LR * (m_hat / (jnp.sqrt(v_hat) + ADAM_EPS) + ADAM_WD * w)
    return delta, m, v


def f_sum8(*xs):
    t = xs[0]
    for x in xs[1:]:
        t = t + x
    return t


def t5_bucket_map():
    qi = jnp.arange(BLOCK)[:, None]
    kj = jnp.arange(3 * BLOCK)[None, :]
    rel = kj - BLOCK - qi
    nb = N_BUCKETS // 2
    max_exact = nb // 2
    ret = jnp.where(rel > 0, nb, 0)
    n = jnp.abs(rel)
    nf = jnp.maximum(n, 1).astype(jnp.float32)
    large = max_exact + (jnp.log(nf / max_exact) / math.log(MAX_DISTANCE / max_exact) * (nb - max_exact)).astype(jnp.int32)
    large = jnp.minimum(large, nb - 1)
    return (ret + jnp.where(n < max_exact, n, large)).astype(jnp.int32)


def bias_table_fwd(table, bucket):
    def body(t_ref, b_ref, o_ref):
        bk = b_ref[...]
        for h in range(A_HEADS):
            acc = jnp.zeros(bk.shape, F32)
            for b in range(N_BUCKETS):
                acc = jnp.where(bk == b, t_ref[b, h], acc)
            o_ref[h] = acc
    return pl.pallas_call(
        body, in_specs=[pl.BlockSpec(memory_space=pltpu.SMEM), pl.BlockSpec(memory_space=pltpu.VMEM)],
        out_specs=pl.BlockSpec(memory_space=pltpu.VMEM),
        out_shape=jax.ShapeDtypeStruct((A_HEADS, BLOCK, 3 * BLOCK), F32), name="bias_table_fwd")(table, bucket)


def bias_table_bwd(dbias_list, bucket):
    n = len(dbias_list)

    def body(*refs):
        b_ref, o_ref = refs[n], refs[n + 1]
        bk = b_ref[...]
        row = lax.broadcasted_iota(jnp.int32, (N_BUCKETS, LANES), 0)
        col = lax.broadcasted_iota(jnp.int32, (N_BUCKETS, LANES), 1)
        out = jnp.zeros((N_BUCKETS, LANES), F32)
        for h in range(A_HEADS):
            d = refs[0][h]
            for r in refs[1:n]:
                d = d + r[h]
            for b in range(N_BUCKETS):
                s = jnp.sum(jnp.where(bk == b, d, 0.0), keepdims=True)
                out = out + jnp.where((row == b) & (col == h), s, 0.0)
        o_ref[...] = out
    return pl.pallas_call(
        body, out_shape=jax.ShapeDtypeStruct((N_BUCKETS, LANES), F32), name="bias_table_bwd",
        compiler_params=_cparams())(*dbias_list, bucket)


def _attn_mask(n, nblk):
    i = lax.broadcasted_iota(jnp.int32, (BLOCK, 3 * BLOCK), 0)
    j = lax.broadcasted_iota(jnp.int32, (BLOCK, 3 * BLOCK), 1)
    kpos = n * BLOCK + j - BLOCK
    return (jnp.abs(j - BLOCK - i) <= WINDOW) & (kpos >= 0) & (kpos < nblk * BLOCK)


def f_attn_block(q, k3, v3, bias, sink, mask):
    kb, vb = k3.astype(BF16), v3.astype(BF16)
    outs = []
    for g in range(A_GROUP):
        qg = q[:, g * A_HEAD_DIM:(g + 1) * A_HEAD_DIM].astype(BF16)
        s = lax.dot_general(qg, kb, (((1,), (1,)), ((), ())), preferred_element_type=F32) * (A_HEAD_DIM ** -0.5)
        s = jnp.where(mask, s + bias[g], NEG_INF)
        sk = sink[g:g + 1, :]
        m = jnp.maximum(jnp.max(s, axis=-1, keepdims=True), sk)
        e = jnp.exp(s - m)
        den = jnp.sum(e, axis=-1, keepdims=True) + jnp.exp(sk - m)
        p = e / den
        outs.append(jnp.dot(p.astype(BF16), vb, preferred_element_type=F32))
    return jnp.concatenate(outs, axis=1)


def _attn_in_specs(nblk):
    qw = A_GROUP * A_HEAD_DIM
    kc, vc = OFF_AK // A_HEAD_DIM, OFF_AV // A_HEAD_DIM
    return [
        pl.BlockSpec((BLOCK, qw), lambda h, n: (n, h)),
        pl.BlockSpec((BLOCK, A_HEAD_DIM), lambda h, n: (jnp.maximum(n - 1, 0), kc + h)),
        pl.BlockSpec((BLOCK, A_HEAD_DIM), lambda h, n: (n, kc + h)),
        pl.BlockSpec((BLOCK, A_HEAD_DIM), lambda h, n: (jnp.minimum(n + 1, nblk - 1), kc + h)),
        pl.BlockSpec((BLOCK, A_HEAD_DIM), lambda h, n: (jnp.maximum(n - 1, 0), vc + h)),
        pl.BlockSpec((BLOCK, A_HEAD_DIM), lambda h, n: (n, vc + h)),
        pl.BlockSpec((BLOCK, A_HEAD_DIM), lambda h, n: (jnp.minimum(n + 1, nblk - 1), vc + h)),
        pl.BlockSpec((A_GROUP, BLOCK, 3 * BLOCK), lambda h, n: (h, 0, 0)),
        pl.BlockSpec((None, A_GROUP, 1), lambda h, n: (h, 0, 0)),
    ]


def attn_fwd(proj, bias, sink, side=None):
    T = proj.shape[0]
    nblk = T // BLOCK

    def body(q_ref, k0, k1, k2, v0, v1, v2, b_ref, s_ref, o_ref):
        n = pl.program_id(1)
        k3 = jnp.concatenate([k0[...], k1[...], k2[...]], axis=0)
        v3 = jnp.concatenate([v0[...], v1[...], v2[...]], axis=0)
        o = f_attn_block(q_ref[...], k3, v3, b_ref[...], s_ref[...], _attn_mask(n, nblk))
        o_ref[...] = o.astype(o_ref.dtype)

    return _pcall(
        body, [proj] * 7 + [bias, sink], side, grid=(A_KV_HEADS, nblk), in_specs=_attn_in_specs(nblk),
        out_specs=pl.BlockSpec((BLOCK, A_GROUP * A_HEAD_DIM), lambda h, n: (n, h)),
        out_shape=jax.ShapeDtypeStruct((T, A_Q), BF16),
        compiler_params=_cparams(("arbitrary", "arbitrary")), name="attn_fwd")[0]


def attn_bwd(proj, bias, sink, dcat, side=None):
    T = proj.shape[0]
    nblk = T // BLOCK
    qw = A_GROUP * A_HEAD_DIM

    def body(q_ref, k0, k1, k2, v0, v1, v2, b_ref, s_ref, do_ref, dq_ref, dk_ref, dv_ref, db_ref, ds_ref, dk_acc, dv_acc):
        n = pl.program_id(1)

        @pl.when(n == 0)
        def _():
            dk_acc[...] = jnp.zeros(dk_acc.shape, F32)
            dv_acc[...] = jnp.zeros(dv_acc.shape, F32)
            db_ref[...] = jnp.zeros(db_ref.shape, F32)
            ds_ref[...] = jnp.zeros(ds_ref.shape, F32)

        k3 = jnp.concatenate([k0[...], k1[...], k2[...]], axis=0)
        v3 = jnp.concatenate([v0[...], v1[...], v2[...]], axis=0)
        mask = _attn_mask(n, nblk)
        _, vjp = jax.vjp(lambda q, k, v, b, s: f_attn_block(q, k, v, b, s, mask), q_ref[...], k3, v3, b_ref[...], s_ref[...])
        dq, dk3, dv3, db, ds = vjp(do_ref[...])
        dq_ref[...] = dq.astype(dq_ref.dtype)
        db_ref[...] += db
        ds_ref[...] += ds
        mid = pl.multiple_of(n * BLOCK, BLOCK)
        dk_acc[pl.ds(mid, BLOCK), :] += dk3[BLOCK:2 * BLOCK]
        dv_acc[pl.ds(mid, BLOCK), :] += dv3[BLOCK:2 * BLOCK]

        @pl.when(n > 0)
        def _():
            lo = pl.multiple_of((n - 1) * BLOCK, BLOCK)
            dk_acc[pl.ds(lo, BLOCK), :] += dk3[0:BLOCK]
            dv_acc[pl.ds(lo, BLOCK), :] += dv3[0:BLOCK]

        @pl.when(n < nblk - 1)
        def _():
            hi = pl.multiple_of((n + 1) * BLOCK, BLOCK)
            dk_acc[pl.ds(hi, BLOCK), :] += dk3[2 * BLOCK:3 * BLOCK]
            dv_acc[pl.ds(hi, BLOCK), :] += dv3[2 * BLOCK:3 * BLOCK]

        @pl.when(n == nblk - 1)
        def _():
            dk_ref[...] = dk_acc[...].astype(dk_ref.dtype)
            dv_ref[...] = dv_acc[...].astype(dv_ref.dtype)

    in_specs = _attn_in_specs(nblk) + [pl.BlockSpec((BLOCK, qw), lambda h, n: (n, h))]
    out_specs = [
        pl.BlockSpec((BLOCK, qw), lambda h, n: (n, h)),
        pl.BlockSpec((T, A_HEAD_DIM), lambda h, n: (0, h)),
        pl.BlockSpec((T, A_HEAD_DIM), lambda h, n: (0, h)),
        pl.BlockSpec((A_GROUP, BLOCK, 3 * BLOCK), lambda h, n: (h, 0, 0)),
        pl.BlockSpec((None, A_GROUP, 1), lambda h, n: (h, 0, 0)),
    ]
    out_shape = [
        jax.ShapeDtypeStruct((T, A_Q), BF16), jax.ShapeDtypeStruct((T, A_KV), BF16), jax.ShapeDtypeStruct((T, A_KV), BF16),
        jax.ShapeDtypeStruct((A_HEADS, BLOCK, 3 * BLOCK), F32), jax.ShapeDtypeStruct((A_KV_HEADS, A_GROUP, 1), F32),
    ]
    return _pcall(
        body, [proj] * 7 + [bias, sink, dcat], side, grid=(A_KV_HEADS, nblk), in_specs=in_specs, out_specs=out_specs,
        out_shape=out_shape, scratch_shapes=[pltpu.VMEM((T, A_HEAD_DIM), F32), pltpu.VMEM((T, A_HEAD_DIM), F32)],
        compiler_params=_cparams(("arbitrary", "arbitrary")), name="attn_bwd")


def f_gla_gate(z, w2f, b2f, w2b, b2b):
    laf = jax.nn.log_sigmoid(jnp.dot(z, w2f, precision=HI, preferred_element_type=F32) + b2f) / GATE_TAU
    lab = jax.nn.log_sigmoid(jnp.dot(z, w2b, precision=HI, preferred_element_type=F32) + b2b) / GATE_TAU
    return laf, lab


def f_gla_post(o, g, gn):
    outs = []
    for h in range(B_HEADS):
        sl = slice(h * B_VAL_DIM, (h + 1) * B_VAL_DIM)
        oh = o[:, sl]
        outs.append(oh * lax.rsqrt(jnp.mean(oh * oh, axis=-1, keepdims=True) + EPS))
    return jnp.concatenate(outs, axis=1) * gn * jax.nn.silu(g)


def _gla_consts(forward):
    C = GLA_CHUNK
    i = lax.broadcasted_iota(jnp.int32, (C, C), 0)
    j = lax.broadcasted_iota(jnp.int32, (C, C), 1)
    if forward:
        return (j <= i).astype(F32), j <= i
    return (j >= i).astype(F32), j > i


def _gla_chunk(q, k, v, la, st, tri, msk, forward):
    C = q.shape[0]
    b = jnp.dot(tri, la, precision=HI, preferred_element_type=F32)
    bl = b[C - 1:C] if forward else b[0:1]
    qe = (q * (B_KEY_DIM ** -0.5)) * jnp.exp(b)
    ke = k * jnp.exp(-b)
    kl = k * jnp.exp(bl - b)
    att = lax.dot_general(qe.astype(BF16), ke.astype(BF16), (((1,), (1,)), ((), ())), preferred_element_type=F32)
    att = jnp.where(msk, att, 0.0)
    o = jnp.dot(att.astype(BF16), v.astype(BF16), preferred_element_type=F32)
    o = o + lax.dot_general(qe.astype(BF16), st.astype(BF16), (((1,), (1,)), ((), ())), preferred_element_type=F32)
    st_new = st * jnp.exp(bl) + lax.dot_general(v.astype(BF16), kl.astype(BF16), (((0,), (0,)), ((), ())),
                                                preferred_element_type=F32)
    return o, st_new


def _gla_specs(T):
    qc, kc, vc = OFF_BQ // LANES, OFF_BK // LANES, OFF_BV // (2 * B_VAL_DIM)
    return [
        pl.BlockSpec((T, LANES), lambda p: (0, qc + p)),
        pl.BlockSpec((T, LANES), lambda p: (0, kc + p)),
        pl.BlockSpec((T, 2 * B_VAL_DIM), lambda p: (0, vc + p)),
        pl.BlockSpec((T, LANES), lambda p: (0, p)),
        pl.BlockSpec((T, LANES), lambda p: (0, p)),
    ]


def _rows(c):
    return pl.ds(pl.multiple_of(c * GLA_CHUNK, GLA_CHUNK), GLA_CHUNK)


def gla_fwd(proj, laf, lab, side=None):
    T = proj.shape[0]
    nc = T // GLA_CHUNK

    def body(q_ref, k_ref, v_ref, laf_ref, lab_ref, o_ref, ob_scr):
        tri_f, msk_f = _gla_consts(True)
        tri_b, msk_b = _gla_consts(False)
        zero = jnp.zeros((B_VAL_DIM, B_KEY_DIM), F32)

        def step(c, carry):
            rf, rb = _rows(c), _rows(nc - 1 - c)
            new = []
            for hh in range(2):
                ks = slice(hh * B_KEY_DIM, (hh + 1) * B_KEY_DIM)
                vs = slice(hh * B_VAL_DIM, (hh + 1) * B_VAL_DIM)
                o, s = _gla_chunk(q_ref[rf, ks], k_ref[rf, ks], v_ref[rf, vs], laf_ref[rf, ks], carry[2 * hh], tri_f, msk_f, True)
                o_ref[rf, vs] = o
                new.append(s)
                o, s = _gla_chunk(q_ref[rb, ks], k_ref[rb, ks], v_ref[rb, vs], lab_ref[rb, ks], carry[2 * hh + 1], tri_b, msk_b, False)
                ob_scr[rb, vs] = o
                new.append(s)
            return tuple(new)

        lax.fori_loop(0, nc, step, (zero,) * 4)
        o_ref[...] += ob_scr[...]

    return _pcall(
        body, [proj, proj, proj, laf, lab], side, grid=(B_HEADS // 2,), in_specs=_gla_specs(T),
        out_specs=pl.BlockSpec((T, 2 * B_VAL_DIM), lambda p: (0, p)),
        out_shape=jax.ShapeDtypeStruct((T, B_V), F32),
        scratch_shapes=[pltpu.VMEM((T, 2 * B_VAL_DIM), F32)],
        compiler_params=_cparams(("arbitrary",)), name="gla_fwd")[0]


def gla_bwd(proj, laf, lab, do, side=None):
    T = proj.shape[0]
    nc = T // GLA_CHUNK
    SROWS = 2 * B_VAL_DIM

    def body(q_ref, k_ref, v_ref, laf_ref, lab_ref, do_ref, dq_ref, dk_ref, dv_ref, dlaf_ref, dlab_ref,
             sf_scr, sb_scr, dq_acc, dk_acc, dv_acc):
        tri_f, msk_f = _gla_consts(True)
        tri_b, msk_b = _gla_consts(False)
        zero = jnp.zeros((B_VAL_DIM, B_KEY_DIM), F32)
        dq_acc[...] = jnp.zeros(dq_acc.shape, F32)
        dk_acc[...] = jnp.zeros(dk_acc.shape, F32)
        dv_acc[...] = jnp.zeros(dv_acc.shape, F32)

        def srow(c, hh):
            return pl.ds(pl.multiple_of(c * SROWS + hh * B_VAL_DIM, B_VAL_DIM), B_VAL_DIM)

        def states(c, carry):
            cf, cb = c, nc - 1 - c
            rf, rb = _rows(cf), _rows(cb)
            new = []
            for hh in range(2):
                ks = slice(hh * B_KEY_DIM, (hh + 1) * B_KEY_DIM)
                vs = slice(hh * B_VAL_DIM, (hh + 1) * B_VAL_DIM)
                sf_scr[srow(cf, hh), :] = carry[2 * hh]
                _, s = _gla_chunk(q_ref[rf, ks], k_ref[rf, ks], v_ref[rf, vs], laf_ref[rf, ks], carry[2 * hh], tri_f, msk_f, True)
                new.append(s)
                sb_scr[srow(cb, hh), :] = carry[2 * hh + 1]
                _, s = _gla_chunk(q_ref[rb, ks], k_ref[rb, ks], v_ref[rb, vs], lab_ref[rb, ks], carry[2 * hh + 1], tri_b, msk_b, False)
                new.append(s)
            return tuple(new)

        lax.fori_loop(0, nc, states, (zero,) * 4)

        def back(c, carry):
            cf, cb = nc - 1 - c, c
            rf, rb = _rows(cf), _rows(cb)
            new = []
            for hh in range(2):
                ks = slice(hh * B_KEY_DIM, (hh + 1) * B_KEY_DIM)
                vs = slice(hh * B_VAL_DIM, (hh + 1) * B_VAL_DIM)
                for fwd, r, c_, la_ref, dla_ref, s_scr, g, tri, msk in (
                        (True, rf, cf, laf_ref, dlaf_ref, sf_scr, carry[2 * hh], tri_f, msk_f),
                        (False, rb, cb, lab_ref, dlab_ref, sb_scr, carry[2 * hh + 1], tri_b, msk_b)):
                    _, vjp = jax.vjp(
                        lambda q, k, v, la, st: _gla_chunk(q, k, v, la, st, tri, msk, fwd),
                        q_ref[r, ks], k_ref[r, ks], v_ref[r, vs], la_ref[r, ks], s_scr[srow(c_, hh), :])
                    dq, dk, dv, dla, dst = vjp((do_ref[r, vs], g))
                    dq_acc[r, ks] += dq
                    dk_acc[r, ks] += dk
                    dv_acc[r, vs] += dv
                    dla_ref[r, ks] = dla
                    new.append(dst)
            return tuple(new)

        lax.fori_loop(0, nc, back, (zero,) * 4)
        dq_ref[...] = dq_acc[...].astype(dq_ref.dtype)
        dk_ref[...] = dk_acc[...].astype(dk_ref.dtype)
        dv_ref[...] = dv_acc[...].astype(dv_ref.dtype)

    in_specs = _gla_specs(T) + [pl.BlockSpec((T, 2 * B_VAL_DIM), lambda p: (0, p))]
    out_specs = [
        pl.BlockSpec((T, LANES), lambda p: (0, p)), pl.BlockSpec((T, LANES), lambda p: (0, p)),
        pl.BlockSpec((T, 2 * B_VAL_DIM), lambda p: (0, p)),
        pl.BlockSpec((T, LANES), lambda p: (0, p)), pl.BlockSpec((T, LANES), lambda p: (0, p)),
    ]
    out_shape = [
        jax.ShapeDtypeStruct((T, B_QK), BF16), jax.ShapeDtypeStruct((T, B_QK), BF16), jax.ShapeDtypeStruct((T, B_V), BF16),
        jax.ShapeDtypeStruct((T, B_QK), F32), jax.ShapeDtypeStruct((T, B_QK), F32),
    ]
    scratch = [
        pltpu.VMEM((nc * SROWS, B_KEY_DIM), F32), pltpu.VMEM((nc * SROWS, B_KEY_DIM), F32),
        pltpu.VMEM((T, LANES), F32), pltpu.VMEM((T, LANES), F32), pltpu.VMEM((T, 2 * B_VAL_DIM), F32),
    ]
    return _pcall(
        body, [proj, proj, proj, laf, lab, do], side, grid=(B_HEADS // 2,), in_specs=in_specs, out_specs=out_specs,
        out_shape=out_shape, scratch_shapes=scratch, compiler_params=_cparams(("arbitrary",)), name="gla_bwd")


def _shift_raw(x, k):
    T = x.shape[0]
    r = lax.broadcasted_iota(jnp.int32, x.shape, 0)
    if k > 0:
        return jnp.where(r >= k, pltpu.roll(x, k, 0), 0.0)
    return jnp.where(r < T + k, pltpu.roll(x, T + k, 0), 0.0)


@functools.partial(jax.custom_vjp, nondiff_argnums=(1,))
def _shift(x, k):
    return _shift_raw(x, k)


_shift.defvjp(lambda x, k: (_shift_raw(x, k), None), lambda k, _, g: (_shift_raw(g, -k),))


def _scan_raw(a, u, reverse):
    T = a.shape[0]
    d = 1
    while d < T:
        k = -d if reverse else d
        u = a * _shift_raw(u, k) + u
        a = a * _shift_raw(a, k)
        d *= 2
    return u


@functools.partial(jax.custom_vjp, nondiff_argnums=(2,))
def _scan(a, u, reverse):
    return _scan_raw(a, u, reverse)


def _scan_f(a, u, reverse):
    h = _scan_raw(a, u, reverse)
    return h, (a, h)


def _scan_b(reverse, res, dh):
    a, h = res
    k = 1 if reverse else -1
    du = _scan_raw(_shift_raw(a, k), dh, not reverse)
    return du * _shift_raw(h, -k), du


_scan.defvjp(_scan_f, _scan_b)


def f_lru(cx, cy, cw, cb, wa, ba, wx, bx, lam, diff):
    shift, scan = (_shift, _scan) if diff else (_shift_raw, _scan_raw)
    xc = cx * cw[CONV_LEFT:CONV_LEFT + 1]
    for j in range(CONV_WIDTH):
        if j != CONV_LEFT:
            xc = xc + shift(cx, CONV_LEFT - j) * cw[j:j + 1]
    xc = xc + cb
    xb = xc.astype(BF16)
    h = None
    for s in range(2):
        r = jax.nn.sigmoid(jnp.dot(xb, wa[s].astype(BF16), preferred_element_type=F32) + ba[s:s + 1])
        i = jax.nn.sigmoid(jnp.dot(xb, wx[s].astype(BF16), preferred_element_type=F32) + bx[s:s + 1])
        log_a = -LRU_C * r * jax.nn.softplus(-lam[s:s + 1])
        a = jnp.exp(log_a)
        one_minus_a2 = -jnp.tanh(log_a) * (a * a + 1.0)
        u = jnp.sqrt(one_minus_a2) * (i * xc)
        hs = scan(a, u, s == 1)
        h = hs if h is None else h + hs
    return h * jax.nn.gelu(cy)


def _lru_specs(T):
    xc, yc = OFF_CX // LANES, OFF_CY // LANES
    return [
        pl.BlockSpec((T, LANES), lambda b: (0, xc + b)),
        pl.BlockSpec((T, LANES), lambda b: (0, yc + b)),
        pl.BlockSpec((CONV_WIDTH, LANES), lambda b: (0, b)),
        pl.BlockSpec((1, LANES), lambda b: (0, b)),
        pl.BlockSpec((2, None, C_BLOCK_DIM, C_BLOCK_DIM), lambda b: (0, b, 0, 0)),
        pl.BlockSpec((2, LANES), lambda b: (0, b)),
        pl.BlockSpec((2, None, C_BLOCK_DIM, C_BLOCK_DIM), lambda b: (0, b, 0, 0)),
        pl.BlockSpec((2, LANES), lambda b: (0, b)),
        pl.BlockSpec((2, LANES), lambda b: (0, b)),
    ]


def lru_fwd(proj, cw, cb, wa, ba, wx, bx, lam, side=None):
    T = proj.shape[0]

    def body(cx, cy, cw_r, cb_r, wa_r, ba_r, wx_r, bx_r, lam_r, o_ref):
        o = f_lru(cx[...], cy[...], cw_r[...], cb_r[...], wa_r[...], ba_r[...], wx_r[...], bx_r[...], lam_r[...], False)
        o_ref[...] = o.astype(o_ref.dtype)

    return _pcall(
        body, [proj, proj, cw, cb, wa, ba, wx, bx, lam], side, grid=(C_BLOCKS,), in_specs=_lru_specs(T),
        out_specs=pl.BlockSpec((T, LANES), lambda b: (0, b)), out_shape=jax.ShapeDtypeStruct((T, C_WIDTH), BF16),
        compiler_params=_cparams(("arbitrary",)), name="lru_fwd")[0]


def lru_bwd(proj, cw, cb, wa, ba, wx, bx, lam, dcat, side=None):
    T = proj.shape[0]
    oc = (A_Q + B_V) // LANES

    def body(cx, cy, cw_r, cb_r, wa_r, ba_r, wx_r, bx_r, lam_r, do_ref, *outs):
        _, vjp = jax.vjp(functools.partial(f_lru, diff=True), cx[...], cy[...], cw_r[...], cb_r[...], wa_r[...],
                         ba_r[...], wx_r[...], bx_r[...], lam_r[...])
        grads = vjp(do_ref[...])
        for o, g in zip(outs, grads):
            o[...] = g.astype(o.dtype)

    specs = _lru_specs(T)
    out_specs = [pl.BlockSpec((T, LANES), lambda b: (0, b)), pl.BlockSpec((T, LANES), lambda b: (0, b))] + specs[2:]
    out_shape = [jax.ShapeDtypeStruct((T, C_WIDTH), BF16), jax.ShapeDtypeStruct((T, C_WIDTH), BF16)] + \
                [jax.ShapeDtypeStruct(p.shape, F32) for p in (cw, cb, wa, ba, wx, bx, lam)]
    return _pcall(
        body, [proj, proj, cw, cb, wa, ba, wx, bx, lam, dcat], side, grid=(C_BLOCKS,),
        in_specs=specs + [pl.BlockSpec((T, LANES), lambda b: (0, oc + b))], out_specs=out_specs, out_shape=out_shape,
        compiler_params=_cparams(("arbitrary",)), name="lru_bwd")


def all_gather8(name, blk):
    def body(x_ref, out_ref, send_sems, recv_sems):
        x, y, c, _, others = _place()
        sibling = (x, y, 1 - c)

        def slab(px, py, pc):
            return out_ref.at[4 * px + 2 * py + pc]

        first = [_remote(x_ref, slab(x, y, c), send_sems, recv_sems, 0, sibling)]
        first += [_remote(x_ref, slab(x, y, c), send_sems, recv_sems, 1 + j, (*ch, c)) for j, ch in enumerate(others)]
        for cp in first:
            cp.start()
        passed = [_remote(slab(*ch, c), slab(*ch, c), send_sems, recv_sems, 4 + j, sibling) for j, ch in enumerate(others)]
        for j, ch in enumerate(others):
            _remote(x_ref, slab(*ch, c), send_sems, recv_sems, 1 + j, (x, y, c)).wait_recv()
            passed[j].start()
        _remote(x_ref, slab(x, y, 1 - c), send_sems, recv_sems, 0, (x, y, c)).wait_recv()
        for j, ch in enumerate(others):
            _remote(x_ref, slab(*ch, 1 - c), send_sems, recv_sems, 4 + j, (x, y, c)).wait_recv()
        for cp in first + passed:
            cp.wait_send()

    out = pl.pallas_call(
        body, out_shape=jax.ShapeDtypeStruct((8,) + blk.shape, blk.dtype), in_specs=[HBM_SPEC], out_specs=HBM_SPEC,
        scratch_shapes=[pltpu.SemaphoreType.DMA((7,)), pltpu.SemaphoreType.DMA((7,))], name=name)(blk)
    me = 4 * lax.axis_index("x") + 2 * lax.axis_index("y") + lax.axis_index("c")
    return lax.dynamic_update_index_in_dim(out, blk, me, 0)


def _exchange_call(name, body, arrays, out_shapes, n_sems):
    n = len(arrays)

    def kernel_body(*refs):
        body(refs[:n], refs[n:2 * n], refs[2 * n], refs[2 * n + 1])

    return pl.pallas_call(
        kernel_body, out_shape=out_shapes, in_specs=[HBM_SPEC] * n, out_specs=[HBM_SPEC] * n,
        scratch_shapes=[pltpu.SemaphoreType.DMA((n * n_sems,)), pltpu.SemaphoreType.DMA((n * n_sems,))], name=name)(*arrays)


def chip_gather(name, shards):
    def body(ins, outs, send_sems, recv_sems):
        x, y, c, chip, others = _place()
        sibling = (x, y, 1 - c)
        first, passed = [], []
        for a, (x_ref, out_ref) in enumerate(zip(ins, outs)):
            first += [_remote(x_ref.at[c], out_ref.at[chip, c], send_sems, recv_sems, 6 * a + j, (*ch, c))
                      for j, ch in enumerate(others)]
        for cp in first:
            cp.start()
        for a, (x_ref, out_ref) in enumerate(zip(ins, outs)):
            for j, ch in enumerate(others):
                here = out_ref.at[_chip_index(ch), c]
                _remote(x_ref.at[c], here, send_sems, recv_sems, 6 * a + j, (x, y, c)).wait_recv()
                cp = _remote(here, here, send_sems, recv_sems, 6 * a + 3 + j, sibling)
                cp.start()
                passed.append(cp)
        for a, (x_ref, out_ref) in enumerate(zip(ins, outs)):
            for j, ch in enumerate(others):
                _remote(x_ref.at[c], out_ref.at[_chip_index(ch), 1 - c], send_sems, recv_sems, 6 * a + 3 + j, (x, y, c)).wait_recv()
        for cp in first + passed:
            cp.wait_send()

    outs = _exchange_call(name, body, shards, [jax.ShapeDtypeStruct((4,) + s.shape, s.dtype) for s in shards], 6)
    chip = 2 * lax.axis_index("x") + lax.axis_index("y")
    return [lax.dynamic_update_index_in_dim(o, s, chip, 0) for o, s in zip(outs, shards)]


def chip_scatter(name, parts):
    def body(ins, outs, send_sems, recv_sems):
        x, y, c, chip, others = _place()
        sends = []
        for a, (x_ref, out_ref) in enumerate(zip(ins, outs)):
            sends += [_remote(x_ref.at[_chip_index(ch)], out_ref.at[chip], send_sems, recv_sems, 3 * a + j, (*ch, c))
                      for j, ch in enumerate(others)]
        for cp in sends:
            cp.start()
        for a, (x_ref, out_ref) in enumerate(zip(ins, outs)):
            for j, ch in enumerate(others):
                _remote(x_ref.at[chip], out_ref.at[_chip_index(ch)], send_sems, recv_sems, 3 * a + j, (x, y, c)).wait_recv()
        for cp in sends:
            cp.wait_send()

    return _exchange_call(name, body, parts, [jax.ShapeDtypeStruct(p.shape, p.dtype) for p in parts], 3)


def sibling_take(name, halves):
    def body(ins, outs, send_sems, recv_sems):
        x, y, c, _, _ = _place()
        cps = [_remote(x_ref.at[s, 1 - c], out_ref.at[s], send_sems, recv_sems, 4 * a + s, (x, y, 1 - c))
               for a, (x_ref, out_ref) in enumerate(zip(ins, outs)) for s in range(4)]
        for cp in cps:
            cp.start()
        for cp in cps:
            cp.wait()

    return _exchange_call(name, body, halves,
                          [jax.ShapeDtypeStruct((h.shape[0],) + h.shape[2:], h.dtype) for h in halves], 4)


def sibling_pair(name, mine):
    def body(ins, outs, send_sems, recv_sems):
        x, y, c, _, _ = _place()
        cps = [_remote(x_ref, out_ref.at[c], send_sems, recv_sems, a, (x, y, 1 - c))
               for a, (x_ref, out_ref) in enumerate(zip(ins, outs))]
        for cp in cps:
            cp.start()
        for a, (x_ref, out_ref) in enumerate(zip(ins, outs)):
            _remote(x_ref, out_ref.at[1 - c], send_sems, recv_sems, a, (x, y, c)).wait_recv()
        for cp in cps:
            cp.wait_send()

    outs = _exchange_call(name, body, mine, [jax.ShapeDtypeStruct((2,) + m.shape, m.dtype) for m in mine], 1)
    core = lax.axis_index("c")
    return [lax.dynamic_update_index_in_dim(o, m, core, 0) for o, m in zip(outs, mine)]


def sum_slabs(name, r, out_dtype, tr):
    S, R, W = r.shape

    def body(*refs):
        t = refs[0][...].astype(F32)
        for s in range(1, S):
            t = t + refs[s][...].astype(F32)
        refs[S][...] = t.astype(out_dtype)

    return pl.pallas_call(
        body, grid=(R // tr,), in_specs=[pl.BlockSpec((None, tr, W), lambda i, s=s: (s, i, 0)) for s in range(S)],
        out_specs=pl.BlockSpec((tr, W), lambda i: (i, 0)), out_shape=jax.ShapeDtypeStruct((R, W), out_dtype),
        compiler_params=_cparams(("parallel",)), name=name)(*([r] * S))


def sum_chips(name, arrived, own, chip, tr):
    S, R, W = arrived.shape

    def body(chip_ref, own_ref, *refs):
        me = chip_ref[0]
        t = None
        for s in range(S):
            term = jnp.where(me == s, own_ref[...].astype(F32), refs[s][...].astype(F32))
            t = term if t is None else t + term
        refs[S][...] = t

    grid_spec = pltpu.PrefetchScalarGridSpec(
        num_scalar_prefetch=1, grid=(R // tr,),
        in_specs=[pl.BlockSpec((None, tr, W), lambda i, ch: (ch[0], i, 0))] +
                 [pl.BlockSpec((None, tr, W), lambda i, ch, s=s: (s, i, 0)) for s in range(S)],
        out_specs=pl.BlockSpec((tr, W), lambda i, ch: (i, 0)))
    return pl.pallas_call(body, grid_spec=grid_spec, out_shape=jax.ShapeDtypeStruct((R, W), F32),
                          compiler_params=_cparams(("parallel",)), name=name)(
                              chip.reshape(1).astype(jnp.int32), own, *([arrived] * S))


def add_kept_half(name, halves, got, c, tr):
    S, _, R, W = halves.shape

    def body(c_ref, h_ref, g_ref, o_ref):
        o_ref[...] = (h_ref[...].astype(F32) + g_ref[...].astype(F32)).astype(o_ref.dtype)

    grid_spec = pltpu.PrefetchScalarGridSpec(
        num_scalar_prefetch=1, grid=(S, R // tr),
        in_specs=[pl.BlockSpec((None, None, tr, W), lambda s, i, c_ref: (s, c_ref[0], i, 0)),
                  pl.BlockSpec((None, tr, W), lambda s, i, c_ref: (s, i, 0))],
        out_specs=pl.BlockSpec((None, tr, W), lambda s, i, c_ref: (s, i, 0)))
    return pl.pallas_call(body, grid_spec=grid_spec, out_shape=jax.ShapeDtypeStruct((S, R, W), halves.dtype),
                          compiler_params=_cparams(("parallel", "parallel")), name=name)(
                              c.reshape(1).astype(jnp.int32), halves, got)


def adamw_layer(name, l, g, row_off, w, m, v, prev, tr, tc=None, side=None):
    L, R, C = w.shape
    tc = C if tc is None else tc
    off = row_off // tr

    def body(g_ref, w_ref, m_ref, v_ref, *rest):
        outs = rest[-4:]
        gv = g_ref[...]
        d, mn, vn = f_adamw(w_ref[...], gv, m_ref[...], v_ref[...])
        for o, val in zip(outs, (gv, d, mn, vn)):
            o[...] = val

    slab = pl.BlockSpec((None, tr, tc), lambda i, j: (l, i, j))
    in_specs = [pl.BlockSpec((tr, tc), lambda i, j: (off + i, j)), slab, slab, slab]
    args = [g, w, m, v]
    aliases = {}
    if prev is not None:
        in_specs += [pl.BlockSpec(memory_space=pl.ANY)] * 4
        args += list(prev)
        aliases = {4 + k: k for k in range(4)}
    return _pcall(
        body, args, side, grid=(R // tr, C // tc), in_specs=in_specs, out_specs=[slab] * 4,
        out_shape=[jax.ShapeDtypeStruct((L, R, C), F32)] * 4, input_output_aliases=aliases,
        compiler_params=_cparams(("parallel", "parallel")), name=name)


BIG = ("w_in", "w_out", "xq", "xk", "xv", "xo", "w_up", "w_down")
GROUPS = {"mix": ("w_out", "xq", "xk", "xv", "xo"), "ff": ("w_up", "w_down")}
PACK_ROWS = {"w_out": 512, "xq": 512, "xk": 512, "xv": 512, "xo": 512, "w_up": 2048, "w_down": 2048}
GROUP_ROWS = {g: sum(PACK_ROWS[n] for n in names) for g, names in GROUPS.items()}
SUM_TILE = 256
PACK_OFF = {}
for _names in GROUPS.values():
    _o = 0
    for _n in _names:
        PACK_OFF[_n] = _o
        _o += PACK_ROWS[_n]

_SPLIT_OFF = np.cumsum((0,) + SPLIT_SIZES)
_KORDER = (0, 1, 2, 3, 4, 5, 6, 9, 10, 7, 8)


def w_in_to_kernel_cols(w):
    parts = [w[..., _SPLIT_OFF[i]:_SPLIT_OFF[i + 1]] for i in _KORDER]
    parts.append(jnp.zeros(w.shape[:-1] + (D_INP - D_IN,), w.dtype))
    return jnp.concatenate(parts, axis=-1)


def w_in_from_kernel_cols(w):
    offs = np.cumsum((0,) + tuple(SPLIT_SIZES[i] for i in _KORDER))
    pos = {k: (offs[n], offs[n + 1]) for n, k in enumerate(_KORDER)}
    return jnp.concatenate([w[..., pos[i][0]:pos[i][1]] for i in range(len(SPLIT_SIZES))], axis=-1)


def pack_shards(shards, group, dtype):
    return jnp.concatenate([shards[n].astype(dtype) for n in GROUPS[group]], axis=-2)


def unpack_rows(packed, name):
    return packed[..., PACK_OFF[name]:PACK_OFF[name] + PACK_ROWS[name], :]


WEIGHTS = ("rel_bias", "w_in", "w_out", "attn_sink", "gla_w2_f", "gla_b2_f", "gla_w2_b", "gla_b2_b", "gla_norm", "conv_w",
           "conv_b", "lru_wa", "lru_ba", "lru_wx", "lru_bx", "lru_lambda", "xq", "xk", "xv", "xo", "w_up", "w_down",
           "norm_mix_pre", "norm_mix_post", "norm_mem", "norm_x_pre", "norm_x_post", "norm_ff_pre", "norm_ff_post")
SMALL = tuple(n for n in WEIGHTS if n not in BIG)
SMALL_SHARDED = ("gla_w2_f", "gla_w2_b", "conv_w", "lru_ba", "lru_bx", "lru_lambda")
ROW_TILE = 256
SMALL_TILE = 512
W_IN_TILE = (344, 1024)
RIDE_PIECE_ROWS = (256, 512)


def _small_rows(n):
    return -(-n // (SUBLANES * LANES)) * SUBLANES


def _as_rows(a2):
    L, n = a2.shape
    rows = _small_rows(n)
    if rows * LANES != n:
        a2 = jnp.pad(a2, ((0, 0), (0, rows * LANES - n)))
    return a2.reshape(L * rows, LANES)


def _pack_small(items, layered):
    parts = []
    for it, lay in zip(items, layered):
        if isinstance(it, (list, tuple)):
            parts += [_as_rows(e.astype(F32).reshape(1, -1)) for e in it]
        else:
            parts.append(_as_rows(it.astype(F32).reshape(it.shape[0] if lay else 1, -1)))
    pad = -sum(p.shape[0] for p in parts) % SMALL_TILE
    if pad:
        parts.append(jnp.zeros((pad, LANES), F32))
    return jnp.concatenate(parts, axis=0)


def _unpack_small(buf, shapes, layered):
    lead = buf.shape[:-2]
    out, o = [], 0
    for s, lay in zip(shapes, layered):
        L = s[0] if lay else 1
        n = int(np.prod(s)) // L
        rows = _small_rows(n)
        part = buf[..., o:o + L * rows, :]
        if n != rows * LANES:
            part = part.reshape(lead + (L, rows * LANES))[..., :n]
        out.append(part.reshape(lead + tuple(s)))
        o += L * rows
    return out


def _relu2(r):
    return r, jnp.square(jnp.maximum(r, 0.0))


def _drelu2(r, u):
    return r * (2.0 * jnp.maximum(u, 0.0))


def kernel(x, mem, rel_bias, w_in, w_out, attn_sink, gla_w2_f, gla_b2_f, gla_w2_b, gla_b2_b, gla_norm, conv_w, conv_b, lru_wa, lru_ba, lru_wx, lru_bx, lru_lambda, xq, xk, xv, xo, w_up, w_down, norm_mix_pre, norm_mix_post, norm_mem, norm_x_pre, norm_x_post, norm_ff_pre, norm_ff_post, loss_target, m_rel_bias, m_w_in, m_w_out, m_attn_sink, m_gla_w2_f, m_gla_b2_f, m_gla_w2_b, m_gla_b2_b, m_gla_norm, m_conv_w, m_conv_b, m_lru_wa, m_lru_ba, m_lru_wx, m_lru_bx, m_lru_lambda, m_xq, m_xk, m_xv, m_xo, m_w_up, m_w_down, m_norm_mix_pre, m_norm_mix_post, m_norm_mem, m_norm_x_pre, m_norm_x_post, m_norm_ff_pre, m_norm_ff_post, v_rel_bias, v_w_in, v_w_out, v_attn_sink, v_gla_w2_f, v_gla_b2_f, v_gla_w2_b, v_gla_b2_b, v_gla_norm, v_conv_w, v_conv_b, v_lru_wa, v_lru_ba, v_lru_wx, v_lru_bx, v_lru_lambda, v_xq, v_xk, v_xv, v_xo, v_w_up, v_w_down, v_norm_mix_pre, v_norm_mix_post, v_norm_mem, v_norm_x_pre, v_norm_x_post, v_norm_ff_pre, v_norm_ff_post):
    w_args = (rel_bias, w_in, w_out, attn_sink, gla_w2_f, gla_b2_f, gla_w2_b, gla_b2_b, gla_norm, conv_w, conv_b, lru_wa,
              lru_ba, lru_wx, lru_bx, lru_lambda, xq, xk, xv, xo, w_up, w_down, norm_mix_pre, norm_mix_post, norm_mem,
              norm_x_pre, norm_x_post, norm_ff_pre, norm_ff_post)
    m_args = (m_rel_bias, m_w_in, m_w_out, m_attn_sink, m_gla_w2_f, m_gla_b2_f, m_gla_w2_b, m_gla_b2_b, m_gla_norm, m_conv_w,
              m_conv_b, m_lru_wa, m_lru_ba, m_lru_wx, m_lru_bx, m_lru_lambda, m_xq, m_xk, m_xv, m_xo, m_w_up, m_w_down,
              m_norm_mix_pre, m_norm_mix_post, m_norm_mem, m_norm_x_pre, m_norm_x_post, m_norm_ff_pre, m_norm_ff_post)
    v_args = (v_rel_bias, v_w_in, v_w_out, v_attn_sink, v_gla_w2_f, v_gla_b2_f, v_gla_w2_b, v_gla_b2_b, v_gla_norm, v_conv_w,
              v_conv_b, v_lru_wa, v_lru_ba, v_lru_wx, v_lru_bx, v_lru_lambda, v_xq, v_xk, v_xv, v_xo, v_w_up, v_w_down,
              v_norm_mix_pre, v_norm_mix_post, v_norm_mem, v_norm_x_pre, v_norm_x_post, v_norm_ff_pre, v_norm_ff_post)
    Wt, Mo, Vo = dict(zip(WEIGHTS, w_args)), dict(zip(WEIGHTS, m_args)), dict(zip(WEIGHTS, v_args))
    x, mem, tgt = x[0], mem[0], loss_target[0]
    D = D_MODEL
    depth = w_in.shape[0]
    chip = 2 * lax.axis_index("x") + lax.axis_index("y")
    core = lax.axis_index("c")

    sm_shapes = [Wt[n].shape for n in SMALL_SHARDED]
    yes = [True] * len(SMALL_SHARDED)
    g8 = all_gather8("gather_small_weights", _pack_small([Wt[n] for n in SMALL_SHARDED], yes))
    per_chip = _unpack_small(g8[0::2], sm_shapes, yes)
    whole = {n: jnp.concatenate([p[j] for j in range(4)], axis=-1) for n, p in zip(SMALL_SHARDED, per_chip)}

    def group_shards(l, group):
        if group == "in":
            return [w_in[l].astype(BF16).reshape(2, D // 2, D_IN // 4)]
        return [pack_shards({n: Wt[n][l] for n in GROUPS[group]}, group, BF16).reshape(2, GROUP_ROWS[group] // 2, D)]

    def whole_weights(group, gathered):
        if group == "in":
            win = gathered[0].reshape(4, D, D_IN // 4).transpose(1, 0, 2).reshape(D, D_IN)
            return {"w_in": w_in_to_kernel_cols(win)}
        g = gathered[0].reshape(4, GROUP_ROWS[group], D)
        if group == "ff":
            return {"w_up": unpack_rows(g, "w_up"), "w_down": unpack_rows(g, "w_down").reshape(D_FF, D)}
        return {n: unpack_rows(g, n).reshape(D, D) for n in GROUPS["mix"]}

    first = group_shards(0, "in")
    Wgot = {(0, "in"): whole_weights("in", chip_gather("gather_first_weights", first))}
    gathers = Stream()
    riding = {}
    for l in range(depth):
        for group in ("in", "mix", "ff"):
            if (l, group) != (0, "in"):
                shards = group_shards(l, group)
                rows = RIDE_PIECE_ROWS[1:] if group == "in" else RIDE_PIECE_ROWS[:1]
                riding[l, group] = (shards, gathers.add(RidingExchange("gather", shards, rows)))

    def need_weights(l, group):
        if (l, group) not in Wgot:
            shards, exchange = riding.pop((l, group))
            got = gathers.finish(exchange, "gather_rest")
            Wgot[l, group] = whole_weights(
                group, [lax.dynamic_update_index_in_dim(b, s, chip, 0) for b, s in zip(got, shards)])
        return Wgot[l, group]

    bucket = t5_bucket_map()
    bias = bias_table_fwd(rel_bias, bucket)

    def gain(name, l):
        return Wt[name][l][None]

    def layer_params(l):
        w2fp = jnp.zeros((LANES, B_QK), F32).at[0:GATE_RANK].set(whole["gla_w2_f"][l])
        w2bp = jnp.zeros((LANES, B_QK), F32).at[GATE_RANK:2 * GATE_RANK].set(whole["gla_w2_b"][l])
        gate = [w2fp, gla_b2_f[l][None], w2bp, gla_b2_b[l][None]]
        lru = [whole["conv_w"][l], conv_b[l][None], lru_wa[l], whole["lru_ba"][l], lru_wx[l], whole["lru_bx"][l],
               whole["lru_lambda"][l]]
        return attn_sink[l].reshape(A_KV_HEADS, A_GROUP, 1), gate, gla_norm[l][None], lru

    saved = []
    xcur = x
    (h1,) = rowmap("norm_first", f_norm, [x], [gain("norm_mix_pre", 0)], [(D, BF16)], ROW_TILE)
    loss_acc = None
    Wfull = []
    ride = gathers.take
    for l in range(depth):
        W = dict(need_weights(l, "in"))
        sink3, gate, gn, lru = layer_params(l)
        (proj,) = mm("mm_in", h1, W["w_in"], "nn", [(F32, 1)], pm=512, pn=1408, side=ride(3))
        oa = attn_fwd(proj, bias, sink3, side=ride(4))
        zrow, grow = (proj, LANES, OFF_Z // LANES), (proj, B_V, OFF_BG // B_V)
        laf, lab = rowmap("gla_gate", f_gla_gate, [zrow], gate, [(B_QK, F32), (B_QK, F32)], ROW_TILE)
        oraw = gla_fwd(proj, laf, lab, side=ride(2))
        (ob,) = rowmap("gla_post", f_gla_post, [oraw, grow], [gn], [(B_V, BF16)], ROW_TILE)
        oc = lru_fwd(proj, *lru, side=ride(2))
        cat = jnp.concatenate([oa, ob, oc], axis=1)
        W.update(need_weights(l, "mix"))
        (mixed,) = mm("mm_out", cat, W["w_out"], "nn", [(F32, 1)], side=ride(2))
        x1, h2 = rowmap("resnorm_mix", f_resnorm, [xcur, mixed], [gain("norm_mix_post", l), gain("norm_x_pre", l)],
                        [(D, F32), (D, BF16)], ROW_TILE)
        (memn,) = rowmap("norm_mem", f_norm, [mem], [gain("norm_mem", l)], [(D, BF16)], ROW_TILE)
        (q,) = mm("mm_xq", h2, W["xq"], "nn", [(BF16, 1)], side=ride(1))
        (k,) = mm("mm_xk", memn, W["xk"], "nn", [(F32, 1)])
        (v,) = mm("mm_xv", memn, W["xv"], "nn", [(F32, 1)])
        (o,) = rowmap("xattn", f_xattn, [q], [k, v], [(D, BF16)], ROW_TILE)
        (xo_out,) = mm("mm_xo", o, W["xo"], "nn", [(F32, 1)], side=ride(1))
        x2, h3 = rowmap("resnorm_x", f_resnorm, [x1, xo_out], [gain("norm_x_post", l), gain("norm_ff_pre", l)],
                        [(D, F32), (D, BF16)], ROW_TILE)
        W.update(need_weights(l, "ff"))
        Wfull.append(W)
        u, act = mm("mm_up", h3, W["w_up"], "nn", [(F32, 1), (BF16, 1)], epilogue=_relu2, side=ride(3))
        (ff,) = mm("mm_down", act, W["w_down"], "nn", [(F32, 1)], side=ride(2))
        saved.append(dict(x0=xcur, h1=h1, proj=proj, laf=laf, lab=lab, oraw=oraw, cat=cat, mixed=mixed, x1=x1, h2=h2,
                          memn=memn, q=q, k=k, v=v, o=o, xo_out=xo_out, x2=x2, h3=h3, u=u, act=act, ff=ff))
        if l < depth - 1:
            xcur, h1 = rowmap("resnorm_ff", f_resnorm, [x2, ff], [gain("norm_ff_post", l), gain("norm_mix_pre", l + 1)],
                              [(D, F32), (D, BF16)], ROW_TILE)
        else:
            (loss_acc,) = rowmap("final_loss", f_final_loss, [x2, ff, tgt], [gain("norm_ff_post", l)], [], ROW_TILE,
                                 accs=[(1, 1)])
    loss = lax.psum(loss_acc[0, 0], ("x", "y", "c"))

    small_g = {n: [None] * depth for n in SMALL if n != "rel_bias"}
    adam = {}
    dbias_all = []
    dx_next = dh1_next = None
    grad_x = None
    scatters = Stream()
    ride = scatters.take
    inflight = []
    updates = []

    def start_reduce(lyr, group, dW):
        if group == "ff":
            pack = dW["ff"]
        else:
            pack = pack_shards({n: dW[n].reshape(4, D // 4, D) for n in GROUPS["mix"]}, group, BF16)
        halves = [pack.reshape(4, 2, GROUP_ROWS[group] // 2, D)]
        if group == "mix":
            g_in = w_in_from_kernel_cols(dW["w_in"]).reshape(D, 4, D_IN // 4).transpose(1, 0, 2)
            halves.append(g_in.reshape(4, 2, D // 2, D_IN // 4))
        got = sibling_take("reduce_to_half_owner", halves)
        sums = [add_kept_half("reduce_chip_sum", h, g, core, SUM_TILE) for h, g in zip(halves, got)]
        inflight.append((lyr, group, sums, scatters.add(RidingExchange("scatter", sums, RIDE_PIECE_ROWS))))

    def finish_reduce():
        lyr, group, sums, exchange = inflight.pop(0)
        arrived = scatters.finish(exchange, "reduce_rest")
        totals = [sum_chips("reduce_sum_chips", a, s, chip, SUM_TILE) for a, s in zip(arrived, sums)]
        fulls = sibling_pair("reduce_share_halves", totals)
        full = fulls[0].reshape(GROUP_ROWS[group], D)
        updates.extend((n, lyr, full, PACK_OFF[n]) for n in GROUPS[group])
        if group == "mix":
            updates.append(("w_in", lyr, fulls[1].reshape(D, D_IN // 4).T, 0))

    w_in_t = [jnp.swapaxes(a, 1, 2) for a in (w_in, m_w_in, v_w_in)]

    def run_updates(carry):
        by_name = {}
        for u in sorted(updates, key=lambda u: -u[1]):
            by_name.setdefault(u[0], []).append(u)
        updates.clear()
        makers, carriers, rest = [], [], []
        for n, seq in by_name.items():
            if carry and n in GROUPS["ff"]:
                seq = seq[-1:] + seq[:-1]
                makers.append(seq[0])
                carriers += seq[1:-1]
                rest += seq[-1:] if len(seq) > 1 else []
            elif carry and n == "w_in":
                makers.append(seq[0])
                carriers += seq[1:]
            else:
                rest += seq
        for i, (n, lyr, g, off) in enumerate(makers + carriers + rest):
            side = None
            if len(makers) <= i < len(makers) + len(carriers):
                side = ride(2 if i == len(makers) else 1)
            if n == "w_in":
                adam[n] = adamw_layer("adamw_w_in", lyr, g, 0, *w_in_t, adam.get(n), W_IN_TILE[0], W_IN_TILE[1], side=side)
            else:
                adam[n] = adamw_layer("adamw_" + n, lyr, g, off, Wt[n], Mo[n], Vo[n], adam.get(n), SUM_TILE, side=side)

    for l in reversed(range(depth)):
        W, S = Wfull[l], saved[l]
        sink3, gate, gn, lru = layer_params(l)
        if l == depth - 1:
            (dx2, dff), (dgp,) = rowmap_bwd("final_bwd", f_final_rows, [S["x2"], S["ff"], tgt], [gain("norm_ff_post", l)],
                                            [None], ROW_TILE, [F32, F32, None], [True])
        else:
            (dx2, dff), (dgp, dgn_next) = rowmap_bwd(
                "resnorm_ff_bwd", f_resnorm, [S["x2"], S["ff"]], [gain("norm_ff_post", l), gain("norm_mix_pre", l + 1)],
                [dx_next, dh1_next], ROW_TILE, [F32, F32], [True, True])
            small_g["norm_mix_pre"][l + 1] = dgn_next[0]
        small_g["norm_ff_post"][l] = dgp[0]
        dW = {}
        (du,) = mm("mm_down_bwd", dff, W["w_down"], "nt", [(BF16, 1)], epilogue=_drelu2, extras=[S["u"]], side=ride(2))
        (pack,) = mm("mm_down_wgrad", S["act"], dff, "tn",
                     [(BF16, ("rows", GROUP_ROWS["ff"], PACK_OFF["w_down"], None))], side=ride(3))
        (dW["ff"],) = mm("mm_up_wgrad", S["h3"], du, "tn",
                         [(BF16, ("cols", GROUP_ROWS["ff"], PACK_OFF["w_up"], pack))], side=ride(2))
        if inflight:
            finish_reduce()
        start_reduce(l, "ff", dW)
        (dh3,) = mm("mm_up_bwd", du, W["w_up"], "nt", [(F32, 1)], pk=D, side=ride(2))
        (dx1, dxo_out), (dg1, dg2) = rowmap_bwd(
            "resnorm_x_bwd", f_resnorm, [S["x1"], S["xo_out"]], [gain("norm_x_post", l), gain("norm_ff_pre", l)],
            [dx2, dh3], ROW_TILE, [F32, F32], [True, True])
        small_g["norm_x_post"][l], small_g["norm_ff_pre"][l] = dg1[0], dg2[0]
        (do,) = mm("mm_xo_bwd", dxo_out, W["xo"], "nt", [(F32, 1)])
        (dW["xo"],) = mm("mm_xo_wgrad", S["o"], dxo_out, "tn", [(BF16, 1)])
        (dq,), (dk, dv) = rowmap_bwd("xattn_bwd", f_xattn, [S["q"]], [S["k"], S["v"]], [do], ROW_TILE, [BF16], [True, True])
        (dW["xq"],) = mm("mm_xq_wgrad", S["h2"], dq, "tn", [(BF16, 1)])
        (dh2,) = mm("mm_xq_bwd", dq, W["xq"], "nt", [(F32, 1)])
        (dW["xk"],) = mm("mm_xk_wgrad", S["memn"], dk, "tn", [(BF16, 1)])
        (dW["xv"],) = mm("mm_xv_wgrad", S["memn"], dv, "tn", [(BF16, 1)])
        (dmk,) = mm("mm_xk_bwd", dk, W["xk"], "nt", [(F32, 1)])
        (dmv,) = mm("mm_xv_bwd", dv, W["xv"], "nt", [(F32, 1)])
        _, (dgm,) = rowmap_bwd("norm_mem_bwd", f_norm_twice, [mem], [gain("norm_mem", l)], [dmk, dmv], ROW_TILE, [None], [True])
        small_g["norm_mem"][l] = dgm[0]
        (dx0, dmixed), (dg1, dg2) = rowmap_bwd(
            "resnorm_mix_bwd", f_resnorm, [S["x0"], S["mixed"]], [gain("norm_mix_post", l), gain("norm_x_pre", l)],
            [dx1, dh2], ROW_TILE, [F32, F32], [True, True])
        small_g["norm_mix_post"][l], small_g["norm_x_pre"][l] = dg1[0], dg2[0]
        (dcat,) = mm("mm_out_bwd", dmixed, W["w_out"], "nt", [(F32, 1)])
        (dW["w_out"],) = mm("mm_out_wgrad", S["cat"], dmixed, "tn", [(BF16, 1)])
        proj = S["proj"]
        daq, dak, dav, dbias, dsink = attn_bwd(proj, bias, sink3, dcat, side=ride(3))
        dbias_all.append(dbias)
        small_g["attn_sink"][l] = dsink.reshape(A_HEADS)
        zrow, grow = (proj, LANES, OFF_Z // LANES), (proj, B_V, OFF_BG // B_V)
        (doraw, dbg), (dgn,) = rowmap_bwd("gla_post_bwd", f_gla_post, [S["oraw"], grow], [gn], [(dcat, B_V, A_Q // B_V)],
                                          ROW_TILE, [F32, BF16], [True])
        dbq, dbk, dbv, dlaf, dlab = gla_bwd(proj, S["laf"], S["lab"], doraw, side=ride(3))
        (dz,), (dw2fp, db2f, dw2bp, db2b) = rowmap_bwd("gla_gate_bwd", f_gla_gate, [zrow], gate, [dlaf, dlab], ROW_TILE,
                                                        [BF16], [True] * 4)
        small_g["gla_norm"][l] = dgn[0]
        small_g["gla_w2_f"][l], small_g["gla_b2_f"][l] = dw2fp[0:GATE_RANK], db2f[0]
        small_g["gla_w2_b"][l], small_g["gla_b2_b"][l] = dw2bp[GATE_RANK:2 * GATE_RANK], db2b[0]
        dcx, dcy, dcw, dcb, dwa, dba, dwx, dbx, dlam = lru_bwd(proj, *lru, dcat, side=ride(2))
        small_g["conv_w"][l], small_g["conv_b"][l] = dcw, dcb[0]
        small_g["lru_wa"][l], small_g["lru_ba"][l], small_g["lru_wx"][l] = dwa, dba, dwx
        small_g["lru_bx"][l], small_g["lru_lambda"][l] = dbx, dlam
        dproj = jnp.concatenate([daq, dak, dav, dbq, dbk, dbv, dbg, dcx, dcy, dz], axis=1)
        (dW["w_in"],) = mm("mm_in_wgrad", S["h1"], dproj, "tn", [(BF16, 1)], pm=512, pn=1408, side=ride(1))
        (dh1,) = mm("mm_in_bwd", dproj, W["w_in"], "nt", [(F32, 1)], side=ride(1))
        if l > 0:
            dx_next, dh1_next = dx0, dh1
        else:
            (grad_x,), (dg0,) = rowmap_bwd("norm_first_bwd", f_norm_keep, [x], [gain("norm_mix_pre", 0)], [dx0, dh1],
                                           ROW_TILE, [F32], [True])
            small_g["norm_mix_pre"][0] = dg0[0]

        finish_reduce()
        start_reduce(l, "mix", dW)
    run_updates(carry=True)
    finish_reduce()
    run_updates(carry=False)

    dtab = bias_table_bwd(dbias_all, bucket)
    small_g["rel_bias"] = dtab[:, :A_HEADS]
    layered = [n != "rel_bias" for n in SMALL]
    sg_shapes = [(depth,) + small_g[n][0].shape if lay else small_g[n].shape for n, lay in zip(SMALL, layered)]
    contributions = all_gather8("gather_small_grads", _pack_small([small_g[n] for n in SMALL], layered))
    sg_sum = sum_slabs("sum_small_grads", contributions, F32, SMALL_TILE)
    sg = dict(zip(SMALL, _unpack_small(sg_sum, sg_shapes, layered)))
    for n in SMALL_SHARDED:
        w = Wt[n].shape[-1]
        sg[n] = lax.dynamic_slice_in_dim(sg[n], chip * w, w, axis=sg[n].ndim - 1)

    grads, delta, new_m, new_v = {}, {}, {}, {}
    adam["w_in"] = [jnp.swapaxes(a, 1, 2) for a in adam["w_in"]]
    for n in BIG:
        grads[n], delta[n], new_m[n], new_v[n] = adam[n]
    shapes = [Wt[n].shape for n in SMALL]
    packs = [_pack_small([src[n] for n in SMALL], layered) for src in (Wt, sg, Mo, Vo)]
    d_, m_, v_ = rowmap("adamw_small", f_adamw, packs, [], [(LANES, F32)] * 3, SMALL_TILE)
    for n, a, b, c_ in zip(SMALL, *[_unpack_small(p, shapes, layered) for p in (d_, m_, v_)]):
        grads[n], delta[n], new_m[n], new_v[n] = sg[n], a, b, c_

    return (loss, grad_x[None], *[grads[n] for n in WEIGHTS], *[delta[n] for n in WEIGHTS],
            *[new_m[n] for n in WEIGHTS], *[new_v[n] for n in WEIGHTS])
```

```python
import functools
import math

import numpy as np
import jax
import jax.numpy as jnp
from jax import lax
from jax.experimental import pallas as pl
from jax.experimental.pallas import tpu as pltpu

F32, BF16 = jnp.float32, jnp.bfloat16
HI = lax.Precision.HIGHEST
MESH = pl.DeviceIdType.MESH

VMEM_LIMIT_BYTES = 56 * 1024 * 1024
LANES = 128
SUBLANES = 8

D_MODEL = 2048
DEPTH = 4
A_HEAD_DIM = 128
A_HEADS = 8
A_KV_HEADS = 2
A_GROUP = 4
WINDOW = 128
BLOCK = 128
N_BUCKETS = 32
MAX_DISTANCE = 128
B_HEADS = 4
B_KEY_DIM = 64
B_VAL_DIM = 128
GATE_RANK = 16
GATE_TAU = 16.0
C_WIDTH = 512
C_BLOCKS = 4
C_BLOCK_DIM = 128
CONV_WIDTH = 4
CONV_LEFT = 2
LRU_C = 8.0
X_HEADS = 4
X_HEAD_DIM = 512
D_FF = 4 * D_MODEL
EPS = 1e-6
NEG_INF = -1e30
A_Q, A_KV, B_QK, B_V = 1024, 256, 256, 512
SPLIT_SIZES = (A_Q, A_KV, A_KV, B_QK, B_QK, B_V, B_V, GATE_RANK, GATE_RANK, C_WIDTH, C_WIDTH)
D_IN = sum(SPLIT_SIZES)
D_INP = 4224
OFF_AQ, OFF_AK, OFF_AV, OFF_BQ, OFF_BK, OFF_BV, OFF_BG, OFF_CX, OFF_CY, OFF_Z = (
    0, 1024, 1280, 1536, 1792, 2048, 2560, 3072, 3584, 4096)
GLA_CHUNK = 128

ADAM_LR, ADAM_B1, ADAM_B2, ADAM_EPS, ADAM_WD, ADAM_STEP = 0.001, 0.9, 0.999, 1e-08, 0.01, 10


def _cparams(sem=None):
    return pltpu.CompilerParams(dimension_semantics=sem, vmem_limit_bytes=VMEM_LIMIT_BYTES)


def _full_spec(a):
    nd = a.ndim
    return pl.BlockSpec(a.shape, lambda *_: (0,) * nd)


def _tup(r):
    return r if isinstance(r, tuple) else (r,)


HBM_SPEC = pl.BlockSpec(memory_space=pltpu.HBM)


def _place():
    x, y, c = lax.axis_index("x"), lax.axis_index("y"), lax.axis_index("c")
    others = [(1 - x, y), (x, 1 - y), (1 - x, 1 - y)]
    return x, y, c, 2 * x + y, others


def _remote(src, dst, send_sems, recv_sems, k, to):
    return pltpu.make_async_remote_copy(src_ref=src, dst_ref=dst, send_sem=send_sems.at[k], recv_sem=recv_sems.at[k],
                                        device_id=to, device_id_type=MESH)


def _chip_index(ch):
    return 2 * ch[0] + ch[1]


def _pcall(body, args, side=None, **kw):
    if side is None:
        res = pl.pallas_call(body, **kw)(*args)
        return list(res) if isinstance(res, (list, tuple)) else [res]
    single = not isinstance(kw["out_shape"], (list, tuple))
    out_shape = [kw.pop("out_shape")] if single else list(kw.pop("out_shape"))
    out_specs = [kw.pop("out_specs")] if single else list(kw.pop("out_specs"))
    in_specs = list(kw.pop("in_specs"))
    scratch = list(kw.pop("scratch_shapes", ()))
    grid = kw.get("grid", ())
    n_in, n_out, n_scr = len(in_specs), len(out_shape), len(scratch)
    srcs, bufs = side.srcs, side.bufs
    ns, nb = len(srcs), len(bufs)

    def wrapped(*refs):
        ins = refs[:n_in]
        src_refs = refs[n_in:n_in + ns]
        o0 = n_in + ns + nb
        outs = refs[o0:o0 + n_out]
        buf_refs = refs[o0 + n_out:o0 + n_out + nb]
        s0 = o0 + n_out + nb
        scr = refs[s0:s0 + n_scr]
        send_sems, recv_sems = refs[s0 + n_scr], refs[s0 + n_scr + 1]
        first = last = None
        for d, n in enumerate(grid):
            f, l_ = pl.program_id(d) == 0, pl.program_id(d) == n - 1
            first = f if first is None else first & f
            last = l_ if last is None else last & l_
        if first is None:
            side.start(src_refs, buf_refs, send_sems, recv_sems)
            body(*ins, *outs, *scr)
            side.finish(src_refs, buf_refs, send_sems, recv_sems)
            return
        pl.when(first)(lambda: side.start(src_refs, buf_refs, send_sems, recv_sems))
        body(*ins, *outs, *scr)
        pl.when(last)(lambda: side.finish(src_refs, buf_refs, send_sems, recv_sems))

    any_spec = pl.BlockSpec(memory_space=pl.ANY)
    aliases = dict(kw.pop("input_output_aliases", {}))
    aliases.update({n_in + ns + i: n_out + i for i in range(nb)})
    cp = kw.pop("compiler_params", None)
    if grid:
        cp = _cparams(("arbitrary",) * len(grid))
    res = pl.pallas_call(
        wrapped, in_specs=in_specs + [any_spec] * (ns + nb), out_specs=out_specs + [any_spec] * nb,
        out_shape=out_shape + [jax.ShapeDtypeStruct(b.shape, b.dtype) for b in bufs],
        scratch_shapes=scratch + [pltpu.SemaphoreType.DMA((side.n_sems,)), pltpu.SemaphoreType.DMA((side.n_sems,))],
        input_output_aliases=aliases, compiler_params=cp, **kw)(*args, *srcs, *bufs)
    side.done(list(res[n_out:]))
    return list(res[:n_out])


class _Side:
    def __init__(self, parts):
        self.parts = parts
        self.srcs = [s for p in parts for s in p[0].srcs]
        self.bufs = [b for p in parts for b in p[0].bufs]
        self.n_sems = max(1, sum(4 * len(now) + 4 * len(relay) + len(last) for _, now, relay, last in parts))

    def done(self, bufs):
        for p in self.parts:
            p[0].bufs, bufs = bufs[:len(p[0].bufs)], bufs[len(p[0].bufs):]

    def _copies(self, src_refs, buf_refs, send_sems, recv_sems):
        x, y, c, chip, others = _place()
        xn, yn, dg = others
        me, sibling = (x, y, c), (x, y, 1 - c)
        mine, landing = [], []
        k = o = 0

        def pair(src, dst, got, to):
            nonlocal k
            mine.append(_remote(src, dst, send_sems, recv_sems, k, to))
            landing.append(_remote(src, got, send_sems, recv_sems, k, me))
            k += 1

        for ex, now, relay, last in self.parts:
            srcs, bufs = src_refs[o:o + len(ex.srcs)], buf_refs[o:o + len(ex.srcs)]
            o += len(ex.srcs)
            for a, r0, n in now:
                rows = pl.ds(r0, n)
                if ex.kind == "gather":
                    for ch in (xn, yn):
                        pair(srcs[a].at[c, rows], bufs[a].at[chip, c, rows], bufs[a].at[_chip_index(ch), c, rows], (*ch, c))
                elif ex.kind == "scatter":
                    for ch in others:
                        pair(srcs[a].at[_chip_index(ch), rows], bufs[a].at[chip, rows], bufs[a].at[_chip_index(ch), rows],
                             (*ch, c))
                elif ex.kind == "take":
                    for s in range(4):
                        pair(srcs[a].at[s, 1 - c, rows], bufs[a].at[s, rows], bufs[a].at[s, rows], sibling)
                else:
                    pair(srcs[a].at[rows], bufs[a].at[c, rows], bufs[a].at[1 - c, rows], sibling)
            for a, r0, n in relay:
                top, bottom, rows = pl.ds(r0, n // 2), pl.ds(r0 + n // 2, n // 2), pl.ds(r0, n)
                from_x, from_y = bufs[a].at[_chip_index(xn), c, top], bufs[a].at[_chip_index(yn), c, bottom]
                pair(from_x, from_x, bufs[a].at[_chip_index(dg), c, top], (*yn, c))
                pair(from_y, from_y, bufs[a].at[_chip_index(dg), c, bottom], (*xn, c))
                for ch in (xn, yn):
                    here = bufs[a].at[_chip_index(ch), c, rows]
                    pair(here, here, bufs[a].at[_chip_index(ch), 1 - c, rows], sibling)
            for a, r0, n in last:
                here = bufs[a].at[_chip_index(dg), c, pl.ds(r0, n)]
                pair(here, here, bufs[a].at[_chip_index(dg), 1 - c, pl.ds(r0, n)], sibling)
        return mine, landing

    def start(self, src_refs, buf_refs, send_sems, recv_sems):
        for cp in self._copies(src_refs, buf_refs, send_sems, recv_sems)[0]:
            cp.start()

    def finish(self, src_refs, buf_refs, send_sems, recv_sems):
        mine, landing = self._copies(src_refs, buf_refs, send_sems, recv_sems)
        for cp in landing:
            cp.wait_recv()
        for cp in mine:
            cp.wait_send()


class RidingExchange:
    KINDS = {"gather": (lambda s: (4,) + s, 1, False), "scatter": (lambda s: s, 1, False),
             "take": (lambda s: (s[0],) + s[2:], 2, True), "pair": (lambda s: (2,) + s, 0, True)}

    def __init__(self, kind, srcs, piece_rows):
        self.kind, self.srcs = kind, list(srcs)
        shape_of, row_axis, self.cheap = self.KINDS[kind]
        self.bufs = [lax.empty(shape_of(tuple(s.shape)), s.dtype) for s in srcs]
        heights = [s.shape[row_axis] for s in srcs]
        per = [[(a, r0, min(pr, h - r0)) for r0 in range(0, h, pr)] for a, (h, pr) in enumerate(zip(heights, piece_rows))]
        self.pieces = list(per[0])
        for extra in per[1:]:
            step = max(1, len(self.pieces) // (len(extra) + 1))
            for i, p in enumerate(extra):
                self.pieces.insert(min(len(self.pieces), (i + 1) * step + i), p)
        self.landed, self.relayed = [], []

    def busy(self):
        return bool(self.pieces or self.landed or self.relayed)

    def step(self, n):
        out = []
        for a, r0, rows in self.pieces[:n]:
            if out and out[-1][0] == a and out[-1][1] + out[-1][2] == r0:
                out[-1] = (a, out[-1][1], out[-1][2] + rows)
            else:
                out.append((a, r0, rows))
        self.pieces = self.pieces[n:]
        relay, last = self.landed, self.relayed
        self.landed, self.relayed = (out if self.kind == "gather" else []), relay
        return out, relay, last


D2D_PIECES_A_CALL = 8


class Stream:
    def __init__(self):
        self.queue = []

    def add(self, exchange):
        self.queue.append(exchange)
        return exchange

    def take(self, n, only=None):
        parts = []
        for ex in (self.queue if only is None else [only]):
            had = len(ex.pieces)
            if ex.cheap and only is None:
                now, relay, last = ex.step(D2D_PIECES_A_CALL)
            else:
                now, relay, last = ex.step(n)
                n -= had - len(ex.pieces)
            if now or relay or last:
                parts.append((ex, now, relay, last))
        return _Side(parts) if parts else None

    def finish(self, exchange, name):
        while exchange.busy():
            side = self.take(len(exchange.pieces), only=exchange)
            _pcall(lambda: None, [], side, in_specs=[], out_specs=[], out_shape=[], name=name)
        self.queue.remove(exchange)
        return exchange.bufs


def _row_ops(rows, tr):
    arrs, specs, widths = [], [], []
    for r in rows:
        arr, n, j = r if isinstance(r, tuple) else (r, r.shape[1], 0)
        arrs.append(arr)
        widths.append(n)
        specs.append(pl.BlockSpec((tr, n), lambda i, j=j: (i, j)))
    return arrs, specs, widths


def rowmap(name, f, rows, params, outs, tr, accs=()):
    rows, row_specs, _ = _row_ops(rows, tr)
    T = rows[0].shape[0]
    nin, nout, nacc = len(rows) + len(params), len(outs), len(accs)

    def body(*refs):
        res = _tup(f(*[r[...] for r in refs[:nin]]))
        for o, r in zip(refs[nin:nin + nout], res[:nout]):
            o[...] = r.astype(o.dtype)
        arefs = refs[nin + nout:]
        if nacc:
            @pl.when(pl.program_id(0) == 0)
            def _():
                for a in arefs:
                    a[...] = jnp.zeros(a.shape, a.dtype)
            for a, r in zip(arefs, res[nout:]):
                a[...] += r.astype(F32)

    in_specs = row_specs + [_full_spec(p) for p in params]
    out_specs = [pl.BlockSpec((tr, n), lambda i: (i, 0)) for n, _ in outs] + \
                [pl.BlockSpec(s, lambda i, nd=len(s): (0,) * nd) for s in accs]
    out_shape = [jax.ShapeDtypeStruct((T, n), d) for n, d in outs] + [jax.ShapeDtypeStruct(s, F32) for s in accs]
    res = pl.pallas_call(body, grid=(T // tr,), in_specs=in_specs, out_specs=out_specs, out_shape=out_shape,
                         compiler_params=_cparams(("arbitrary",)), name=name)(*rows, *params)
    return tuple(res)


def rowmap_bwd(name, f, rows, params, cots, tr, drow_dtypes, want_params):
    rows, row_specs, widths = _row_ops(rows, tr)
    T = rows[0].shape[0]
    nr, npar = len(rows), len(params)
    cot_arrays, cot_specs, _ = _row_ops([c for c in cots if c is not None], tr)
    nc = len(cot_arrays)
    ridx = [i for i, d in enumerate(drow_dtypes) if d is not None]
    pidx = [i for i, w in enumerate(want_params) if w]

    def body(*refs):
        rvals = [r[...] for r in refs[:nr]]
        pvals = [r[...] for r in refs[nr:nr + npar]]
        crefs = list(refs[nr + npar:nr + npar + nc])
        orefs = refs[nr + npar + nc:]
        outs, vjp = jax.vjp(f, *rvals, *pvals)
        outs = _tup(outs)
        cts = []
        for c, o in zip(cots, outs):
            cts.append(jnp.ones(o.shape, o.dtype) if c is None else crefs.pop(0)[...].astype(o.dtype))
        grads = vjp(tuple(cts) if len(cts) > 1 else cts[0])
        for o, i in zip(orefs[:len(ridx)], ridx):
            o[...] = grads[i].astype(o.dtype)
        prefs = orefs[len(ridx):]
        if prefs:
            @pl.when(pl.program_id(0) == 0)
            def _():
                for a in prefs:
                    a[...] = jnp.zeros(a.shape, a.dtype)
            for a, i in zip(prefs, pidx):
                a[...] += grads[nr + i].astype(F32)

    in_specs = row_specs + [_full_spec(p) for p in params] + cot_specs
    out_specs = [pl.BlockSpec((tr, widths[i]), lambda i: (i, 0)) for i in ridx] + [_full_spec(params[i]) for i in pidx]
    out_shape = [jax.ShapeDtypeStruct((T, widths[i]), drow_dtypes[i]) for i in ridx] + \
                [jax.ShapeDtypeStruct(params[i].shape, F32) for i in pidx]
    res = pl.pallas_call(body, grid=(T // tr,), in_specs=in_specs, out_specs=out_specs, out_shape=out_shape,
                         compiler_params=_cparams(("arbitrary",)), name=name)(*rows, *params, *cot_arrays)
    res = tuple(res)
    return res[:len(ridx)], res[len(ridx):]


def _pick(n, pref):
    best = None
    for d in range(LANES, min(n, pref) + 1, LANES):
        if n % d == 0:
            best = d
    return best if best is not None else n


def _spec2(arr, tile, pos):
    tr, tc = tile
    if arr.ndim == 2:
        return pl.BlockSpec((tr, tc), lambda i, j, k: pos(i, j, k))
    assert arr.shape[2] % tc == 0, (arr.shape, tile)
    per = arr.shape[2] // tc

    def imap(i, j, k):
        r, c = pos(i, j, k)
        return (c // per, r, c % per)
    return pl.BlockSpec((None, tr, tc), imap)


def _dims2(arr):
    return (arr.shape[0], arr.shape[1]) if arr.ndim == 2 else (arr.shape[1], arr.shape[0] * arr.shape[2])


def mm(name, a, b, mode, outs, epilogue=None, extras=(), pm=1024, pn=512, pk=4224, side=None):
    ar, ac = _dims2(a)
    br, bc = _dims2(b)
    if mode == "nn":
        M, K, N = ar, ac, bc
    elif mode == "nt":
        M, K, N = ar, ac, br
    else:
        M, K, N = ac, ar, bc
    tm, tn, tk = _pick(M, pm), _pick(N, pn), _pick(K, pk)
    for arr in (a, b) + tuple(extras):
        if arr.ndim == 3:
            assert arr.shape[2] % LANES == 0
    if mode == "nn":
        a_spec = _spec2(a, (tm, tk), lambda i, j, k: (i, k))
        b_spec = _spec2(b, (tk, tn), lambda i, j, k: (k, j))
        dims = (((1,), (0,)), ((), ()))
    elif mode == "nt":
        a_spec = _spec2(a, (tm, tk), lambda i, j, k: (i, k))
        b_spec = _spec2(b, (tn, tk), lambda i, j, k: (j, k))
        dims = (((1,), (1,)), ((), ()))
    else:
        a_spec = _spec2(a, (tk, tm), lambda i, j, k: (k, i))
        b_spec = _spec2(b, (tk, tn), lambda i, j, k: (k, j))
        dims = (((0,), (0,)), ((), ()))
    nk = K // tk
    nex = len(extras)

    def body(*refs):
        a_ref, b_ref = refs[0], refs[1]
        ex_refs = refs[2:2 + nex]
        o_refs = refs[2 + nex:2 + nex + len(outs)]
        acc = refs[-1]
        k = pl.program_id(2)
        part = lax.dot_general(a_ref[...].astype(BF16), b_ref[...].astype(BF16), dims, preferred_element_type=F32)

        def finish(r):
            res = (r,) if epilogue is None else _tup(epilogue(r, *[e[...] for e in ex_refs]))
            for o, v in zip(o_refs, res):
                o[...] = v.astype(o.dtype)

        if nk == 1:
            finish(part)
            return

        @pl.when(k == 0)
        def _():
            acc[...] = part

        @pl.when(k > 0)
        def _():
            acc[...] += part

        @pl.when(k == nk - 1)
        def _():
            finish(acc[...])

    out_shape, out_specs = [], []
    args, in_specs, aliases = [a, b, *extras], [a_spec, b_spec], {}
    in_specs += [_spec2(e, (tm, tn), lambda i, j, k: (i, j)) for e in extras]
    for dt, chunks in outs:
        if isinstance(chunks, tuple):
            how, rows, off, buf = chunks
            o = jax.ShapeDtypeStruct((4, rows, D_MODEL), dt)
            assert off % tm == 0 and D_MODEL % tn == 0 and D_MODEL % tm == 0
            if how == "cols":
                per = D_MODEL // tn
                spec = pl.BlockSpec((None, tm, tn), lambda i, j, k: (j // per, off // tm + i, j % per))
            else:
                per = D_MODEL // tm
                spec = pl.BlockSpec((None, tm, tn), lambda i, j, k: (i // per, off // tm + i % per, j))
            if buf is not None:
                aliases[len(args)] = len(out_shape)
                args.append(buf)
                in_specs.append(pl.BlockSpec(memory_space=pl.ANY))
        else:
            o = jax.ShapeDtypeStruct((M, N) if chunks == 1 else (chunks, M, N // chunks), dt)
            spec = _spec2(o, (tm, tn), lambda i, j, k: (i, j))
        out_shape.append(o)
        out_specs.append(spec)
    n_extra_in = len(args) - 2 - nex
    res = _pcall(
        (lambda *refs: body(*refs[:2 + nex], *refs[2 + nex + n_extra_in:])) if n_extra_in else body, args, side,
        grid=(M // tm, N // tn, nk), in_specs=in_specs, out_specs=out_specs, out_shape=out_shape,
        scratch_shapes=[pltpu.VMEM((tm, tn), F32)] if nk > 1 else [], input_output_aliases=aliases,
        compiler_params=_cparams(("parallel", "parallel", "arbitrary")), name=name)
    return tuple(res)


def _rms(x, g):
    return x * lax.rsqrt(jnp.mean(x * x, axis=-1, keepdims=True) + EPS) * g


def f_norm(x, g):
    return _rms(x, g)


def f_norm_keep(x, g):
    return x, _rms(x, g)


def f_resnorm(xp, m, gpost, gnext):
    xn = xp + _rms(m, gpost)
    return xn, _rms(xn, gnext)


def f_final_rows(xp, m, tgt, gpost):
    xn = xp + _rms(m, gpost)
    return 0.5 * jnp.mean(jnp.square(xn - tgt), axis=-1, keepdims=True)


def f_final_loss(xp, m, tgt, gpost):
    return jnp.sum(f_final_rows(xp, m, tgt, gpost), axis=0, keepdims=True)


def f_norm_twice(x, g):
    y = _rms(x, g)
    return y, y


def f_xattn(q, k, v):
    outs = []
    for h in range(X_HEADS):
        sl = slice(h * X_HEAD_DIM, (h + 1) * X_HEAD_DIM)
        s = lax.dot_general(q[:, sl].astype(BF16), k[:, sl].astype(BF16), (((1,), (1,)), ((), ())),
                            preferred_element_type=F32) * (X_HEAD_DIM ** -0.5)
        m = jnp.max(s, axis=-1, keepdims=True)
        e = jnp.exp(s - m)
        p = e / jnp.sum(e, axis=-1, keepdims=True)
        outs.append(jnp.dot(p.astype(BF16), v[:, sl].astype(BF16), preferred_element_type=F32))
    return jnp.concatenate(outs, axis=1)


def f_adamw(w, g, m, v):
    m = ADAM_B1 * m + (1.0 - ADAM_B1) * g
    v = ADAM_B2 * v + (1.0 - ADAM_B2) * jnp.square(g)
    m_hat = m / (1.0 - ADAM_B1 ** ADAM_STEP)
    v_hat = v / (1.0 - ADAM_B2 ** ADAM_STEP)
    delta = -ADAM_LR * (m_hat / (jnp.sqrt(v_hat) + ADAM_EPS) + ADAM_WD * w)
    return delta, m, v


def f_sum8(*xs):
    t = xs[0]
    for x in xs[1:]:
        t = t + x
    return t


def t5_bucket_map():
    qi = jnp.arange(BLOCK)[:, None]
    kj = jnp.arange(3 * BLOCK)[None, :]
    rel = kj - BLOCK - qi
    nb = N_BUCKETS // 2
    max_exact = nb // 2
    ret = jnp.where(rel > 0, nb, 0)
    n = jnp.abs(rel)
    nf = jnp.maximum(n, 1).astype(jnp.float32)
    large = max_exact + (jnp.log(nf / max_exact) / math.log(MAX_DISTANCE / max_exact) * (nb - max_exact)).astype(jnp.int32)
    large = jnp.minimum(large, nb - 1)
    return (ret + jnp.where(n < max_exact, n, large)).astype(jnp.int32)


def bias_table_fwd(table, bucket):
    def body(t_ref, b_ref, o_ref):
        bk = b_ref[...]
        for h in range(A_HEADS):
            acc = jnp.zeros(bk.shape, F32)
            for b in range(N_BUCKETS):
                acc = jnp.where(bk == b, t_ref[b, h], acc)
            o_ref[h] = acc
    return pl.pallas_call(
        body, in_specs=[pl.BlockSpec(memory_space=pltpu.SMEM), pl.BlockSpec(memory_space=pltpu.VMEM)],
        out_specs=pl.BlockSpec(memory_space=pltpu.VMEM),
        out_shape=jax.ShapeDtypeStruct((A_HEADS, BLOCK, 3 * BLOCK), F32), name="bias_table_fwd")(table, bucket)


def bias_table_bwd(dbias_list, bucket):
    n = len(dbias_list)

    def body(*refs):
        b_ref, o_ref = refs[n], refs[n + 1]
        bk = b_ref[...]
        row = lax.broadcasted_iota(jnp.int32, (N_BUCKETS, LANES), 0)
        col = lax.broadcasted_iota(jnp.int32, (N_BUCKETS, LANES), 1)
        out = jnp.zeros((N_BUCKETS, LANES), F32)
        for h in range(A_HEADS):
            d = refs[0][h]
            for r in refs[1:n]:
                d = d + r[h]
            for b in range(N_BUCKETS):
                s = jnp.sum(jnp.where(bk == b, d, 0.0), keepdims=True)
                out = out + jnp.where((row == b) & (col == h), s, 0.0)
        o_ref[...] = out
    return pl.pallas_call(
        body, out_shape=jax.ShapeDtypeStruct((N_BUCKETS, LANES), F32), name="bias_table_bwd",
        compiler_params=_cparams())(*dbias_list, bucket)


def _attn_mask(n, nblk):
    i = lax.broadcasted_iota(jnp.int32, (BLOCK, 3 * BLOCK), 0)
    j = lax.broadcasted_iota(jnp.int32, (BLOCK, 3 * BLOCK), 1)
    kpos = n * BLOCK + j - BLOCK
    return (jnp.abs(j - BLOCK - i) <= WINDOW) & (kpos >= 0) & (kpos < nblk * BLOCK)


def f_attn_block(q, k3, v3, bias, sink, mask):
    kb, vb = k3.astype(BF16), v3.astype(BF16)
    outs = []
    for g in range(A_GROUP):
        qg = q[:, g * A_HEAD_DIM:(g + 1) * A_HEAD_DIM].astype(BF16)
        s = lax.dot_general(qg, kb, (((1,), (1,)), ((), ())), preferred_element_type=F32) * (A_HEAD_DIM ** -0.5)
        s = jnp.where(mask, s + bias[g], NEG_INF)
        sk = sink[g:g + 1, :]
        m = jnp.maximum(jnp.max(s, axis=-1, keepdims=True), sk)
        e = jnp.exp(s - m)
        den = jnp.sum(e, axis=-1, keepdims=True) + jnp.exp(sk - m)
        p = e / den
        outs.append(jnp.dot(p.astype(BF16), vb, preferred_element_type=F32))
    return jnp.concatenate(outs, axis=1)


def _attn_in_specs(nblk):
    qw = A_GROUP * A_HEAD_DIM
    kc, vc = OFF_AK // A_HEAD_DIM, OFF_AV // A_HEAD_DIM
    return [
        pl.BlockSpec((BLOCK, qw), lambda h, n: (n, h)),
        pl.BlockSpec((BLOCK, A_HEAD_DIM), lambda h, n: (jnp.maximum(n - 1, 0), kc + h)),
        pl.BlockSpec((BLOCK, A_HEAD_DIM), lambda h, n: (n, kc + h)),
        pl.BlockSpec((BLOCK, A_HEAD_DIM), lambda h, n: (jnp.minimum(n + 1, nblk - 1), kc + h)),
        pl.BlockSpec((BLOCK, A_HEAD_DIM), lambda h, n: (jnp.maximum(n - 1, 0), vc + h)),
        pl.BlockSpec((BLOCK, A_HEAD_DIM), lambda h, n: (n, vc + h)),
        pl.BlockSpec((BLOCK, A_HEAD_DIM), lambda h, n: (jnp.minimum(n + 1, nblk - 1), vc + h)),
        pl.BlockSpec((A_GROUP, BLOCK, 3 * BLOCK), lambda h, n: (h, 0, 0)),
        pl.BlockSpec((None, A_GROUP, 1), lambda h, n: (h, 0, 0)),
    ]


def attn_fwd(proj, bias, sink, side=None):
    T = proj.shape[0]
    nblk = T // BLOCK

    def body(q_ref, k0, k1, k2, v0, v1, v2, b_ref, s_ref, o_ref):
        n = pl.program_id(1)
        k3 = jnp.concatenate([k0[...], k1[...], k2[...]], axis=0)
        v3 = jnp.concatenate([v0[...], v1[...], v2[...]], axis=0)
        o = f_attn_block(q_ref[...], k3, v3, b_ref[...], s_ref[...], _attn_mask(n, nblk))
        o_ref[...] = o.astype(o_ref.dtype)

    return _pcall(
        body, [proj] * 7 + [bias, sink], side, grid=(A_KV_HEADS, nblk), in_specs=_attn_in_specs(nblk),
        out_specs=pl.BlockSpec((BLOCK, A_GROUP * A_HEAD_DIM), lambda h, n: (n, h)),
        out_shape=jax.ShapeDtypeStruct((T, A_Q), BF16),
        compiler_params=_cparams(("arbitrary", "arbitrary")), name="attn_fwd")[0]


def attn_bwd(proj, bias, sink, dcat, side=None):
    T = proj.shape[0]
    nblk = T // BLOCK
    qw = A_GROUP * A_HEAD_DIM

    def body(q_ref, k0, k1, k2, v0, v1, v2, b_ref, s_ref, do_ref, dq_ref, dk_ref, dv_ref, db_ref, ds_ref, dk_acc, dv_acc):
        n = pl.program_id(1)

        @pl.when(n == 0)
        def _():
            dk_acc[...] = jnp.zeros(dk_acc.shape, F32)
            dv_acc[...] = jnp.zeros(dv_acc.shape, F32)
            db_ref[...] = jnp.zeros(db_ref.shape, F32)
            ds_ref[...] = jnp.zeros(ds_ref.shape, F32)

        k3 = jnp.concatenate([k0[...], k1[...], k2[...]], axis=0)
        v3 = jnp.concatenate([v0[...], v1[...], v2[...]], axis=0)
        mask = _attn_mask(n, nblk)
        _, vjp = jax.vjp(lambda q, k, v, b, s: f_attn_block(q, k, v, b, s, mask), q_ref[...], k3, v3, b_ref[...], s_ref[...])
        dq, dk3, dv3, db, ds = vjp(do_ref[...])
        dq_ref[...] = dq.astype(dq_ref.dtype)
        db_ref[...] += db
        ds_ref[...] += ds
        mid = pl.multiple_of(n * BLOCK, BLOCK)
        dk_acc[pl.ds(mid, BLOCK), :] += dk3[BLOCK:2 * BLOCK]
        dv_acc[pl.ds(mid, BLOCK), :] += dv3[BLOCK:2 * BLOCK]

        @pl.when(n > 0)
        def _():
            lo = pl.multiple_of((n - 1) * BLOCK, BLOCK)
            dk_acc[pl.ds(lo, BLOCK), :] += dk3[0:BLOCK]
            dv_acc[pl.ds(lo, BLOCK), :] += dv3[0:BLOCK]

        @pl.when(n < nblk - 1)
        def _():
            hi = pl.multiple_of((n + 1) * BLOCK, BLOCK)
            dk_acc[pl.ds(hi, BLOCK), :] += dk3[2 * BLOCK:3 * BLOCK]
            dv_acc[pl.ds(hi, BLOCK), :] += dv3[2 * BLOCK:3 * BLOCK]

        @pl.when(n == nblk - 1)
        def _():
            dk_ref[...] = dk_acc[...].astype(dk_ref.dtype)
            dv_ref[...] = dv_acc[...].astype(dv_ref.dtype)

    in_specs = _attn_in_specs(nblk) + [pl.BlockSpec((BLOCK, qw), lambda h, n: (n, h))]
    out_specs = [
        pl.BlockSpec((BLOCK, qw), lambda h, n: (n, h)),
        pl.BlockSpec((T, A_HEAD_DIM), lambda h, n: (0, h)),
        pl.BlockSpec((T, A_HEAD_DIM), lambda h, n: (0, h)),
        pl.BlockSpec((A_GROUP, BLOCK, 3 * BLOCK), lambda h, n: (h, 0, 0)),
        pl.BlockSpec((None, A_GROUP, 1), lambda h, n: (h, 0, 0)),
    ]
    out_shape = [
        jax.ShapeDtypeStruct((T, A_Q), BF16), jax.ShapeDtypeStruct((T, A_KV), BF16), jax.ShapeDtypeStruct((T, A_KV), BF16),
        jax.ShapeDtypeStruct((A_HEADS, BLOCK, 3 * BLOCK), F32), jax.ShapeDtypeStruct((A_KV_HEADS, A_GROUP, 1), F32),
    ]
    return _pcall(
        body, [proj] * 7 + [bias, sink, dcat], side, grid=(A_KV_HEADS, nblk), in_specs=in_specs, out_specs=out_specs,
        out_shape=out_shape, scratch_shapes=[pltpu.VMEM((T, A_HEAD_DIM), F32), pltpu.VMEM((T, A_HEAD_DIM), F32)],
        compiler_params=_cparams(("arbitrary", "arbitrary")), name="attn_bwd")


def f_gla_gate(z, w2f, b2f, w2b, b2b):
    laf = jax.nn.log_sigmoid(jnp.dot(z, w2f, precision=HI, preferred_element_type=F32) + b2f) / GATE_TAU
    lab = jax.nn.log_sigmoid(jnp.dot(z, w2b, precision=HI, preferred_element_type=F32) + b2b) / GATE_TAU
    return laf, lab


def f_gla_post(o, g, gn):
    outs = []
    for h in range(B_HEADS):
        sl = slice(h * B_VAL_DIM, (h + 1) * B_VAL_DIM)
        oh = o[:, sl]
        outs.append(oh * lax.rsqrt(jnp.mean(oh * oh, axis=-1, keepdims=True) + EPS))
    return jnp.concatenate(outs, axis=1) * gn * jax.nn.silu(g)


def _gla_consts(forward):
    C = GLA_CHUNK
    i = lax.broadcasted_iota(jnp.int32, (C, C), 0)
    j = lax.broadcasted_iota(jnp.int32, (C, C), 1)
    if forward:
        return (j <= i).astype(F32), j <= i
    return (j >= i).astype(F32), j > i


def _gla_chunk(q, k, v, la, st, tri, msk, forward):
    C = q.shape[0]
    b = jnp.dot(tri, la, precision=HI, preferred_element_type=F32)
    bl = b[C - 1:C] if forward else b[0:1]
    qe = (q * (B_KEY_DIM ** -0.5)) * jnp.exp(b)
    ke = k * jnp.exp(-b)
    kl = k * jnp.exp(bl - b)
    att = lax.dot_general(qe.astype(BF16), ke.astype(BF16), (((1,), (1,)), ((), ())), preferred_element_type=F32)
    att = jnp.where(msk, att, 0.0)
    o = jnp.dot(att.astype(BF16), v.astype(BF16), preferred_element_type=F32)
    o = o + lax.dot_general(qe.astype(BF16), st.astype(BF16), (((1,), (1,)), ((), ())), preferred_element_type=F32)
    st_new = st * jnp.exp(bl) + lax.dot_general(v.astype(BF16), kl.astype(BF16), (((0,), (0,)), ((), ())),
                                                preferred_element_type=F32)
    return o, st_new


def _gla_specs(T):
    qc, kc, vc = OFF_BQ // LANES, OFF_BK // LANES, OFF_BV // (2 * B_VAL_DIM)
    return [
        pl.BlockSpec((T, LANES), lambda p: (0, qc + p)),
        pl.BlockSpec((T, LANES), lambda p: (0, kc + p)),
        pl.BlockSpec((T, 2 * B_VAL_DIM), lambda p: (0, vc + p)),
        pl.BlockSpec((T, LANES), lambda p: (0, p)),
        pl.BlockSpec((T, LANES), lambda p: (0, p)),
    ]


def _rows(c):
    return pl.ds(pl.multiple_of(c * GLA_CHUNK, GLA_CHUNK), GLA_CHUNK)


def gla_fwd(proj, laf, lab, side=None):
    T = proj.shape[0]
    nc = T // GLA_CHUNK

    def body(q_ref, k_ref, v_ref, laf_ref, lab_ref, o_ref, ob_scr):
        tri_f, msk_f = _gla_consts(True)
        tri_b, msk_b = _gla_consts(False)
        zero = jnp.zeros((B_VAL_DIM, B_KEY_DIM), F32)

        def step(c, carry):
            rf, rb = _rows(c), _rows(nc - 1 - c)
            new = []
            for hh in range(2):
                ks = slice(hh * B_KEY_DIM, (hh + 1) * B_KEY_DIM)
                vs = slice(hh * B_VAL_DIM, (hh + 1) * B_VAL_DIM)
                o, s = _gla_chunk(q_ref[rf, ks], k_ref[rf, ks], v_ref[rf, vs], laf_ref[rf, ks], carry[2 * hh], tri_f, msk_f, True)
                o_ref[rf, vs] = o
                new.append(s)
                o, s = _gla_chunk(q_ref[rb, ks], k_ref[rb, ks], v_ref[rb, vs], lab_ref[rb, ks], carry[2 * hh + 1], tri_b, msk_b, False)
                ob_scr[rb, vs] = o
                new.append(s)
            return tuple(new)

        lax.fori_loop(0, nc, step, (zero,) * 4)
        o_ref[...] += ob_scr[...]

    return _pcall(
        body, [proj, proj, proj, laf, lab], side, grid=(B_HEADS // 2,), in_specs=_gla_specs(T),
        out_specs=pl.BlockSpec((T, 2 * B_VAL_DIM), lambda p: (0, p)),
        out_shape=jax.ShapeDtypeStruct((T, B_V), F32),
        scratch_shapes=[pltpu.VMEM((T, 2 * B_VAL_DIM), F32)],
        compiler_params=_cparams(("arbitrary",)), name="gla_fwd")[0]


def gla_bwd(proj, laf, lab, do, side=None):
    T = proj.shape[0]
    nc = T // GLA_CHUNK
    SROWS = 2 * B_VAL_DIM

    def body(q_ref, k_ref, v_ref, laf_ref, lab_ref, do_ref, dq_ref, dk_ref, dv_ref, dlaf_ref, dlab_ref,
             sf_scr, sb_scr, dq_acc, dk_acc, dv_acc):
        tri_f, msk_f = _gla_consts(True)
        tri_b, msk_b = _gla_consts(False)
        zero = jnp.zeros((B_VAL_DIM, B_KEY_DIM), F32)
        dq_acc[...] = jnp.zeros(dq_acc.shape, F32)
        dk_acc[...] = jnp.zeros(dk_acc.shape, F32)
        dv_acc[...] = jnp.zeros(dv_acc.shape, F32)

        def srow(c, hh):
            return pl.ds(pl.multiple_of(c * SROWS + hh * B_VAL_DIM, B_VAL_DIM), B_VAL_DIM)

        def states(c, carry):
            cf, cb = c, nc - 1 - c
            rf, rb = _rows(cf), _rows(cb)
            new = []
            for hh in range(2):
                ks = slice(hh * B_KEY_DIM, (hh + 1) * B_KEY_DIM)
                vs = slice(hh * B_VAL_DIM, (hh + 1) * B_VAL_DIM)
                sf_scr[srow(cf, hh), :] = carry[2 * hh]
                _, s = _gla_chunk(q_ref[rf, ks], k_ref[rf, ks], v_ref[rf, vs], laf_ref[rf, ks], carry[2 * hh], tri_f, msk_f, True)
                new.append(s)
                sb_scr[srow(cb, hh), :] = carry[2 * hh + 1]
                _, s = _gla_chunk(q_ref[rb, ks], k_ref[rb, ks], v_ref[rb, vs], lab_ref[rb, ks], carry[2 * hh + 1], tri_b, msk_b, False)
                new.append(s)
            return tuple(new)

        lax.fori_loop(0, nc, states, (zero,) * 4)

        def back(c, carry):
            cf, cb = nc - 1 - c, c
            rf, rb = _rows(cf), _rows(cb)
            new = []
            for hh in range(2):
                ks = slice(hh * B_KEY_DIM, (hh + 1) * B_KEY_DIM)
                vs = slice(hh * B_VAL_DIM, (hh + 1) * B_VAL_DIM)
                for fwd, r, c_, la_ref, dla_ref, s_scr, g, tri, msk in (
                        (True, rf, cf, laf_ref, dlaf_ref, sf_scr, carry[2 * hh], tri_f, msk_f),
                        (False, rb, cb, lab_ref, dlab_ref, sb_scr, carry[2 * hh + 1], tri_b, msk_b)):
                    _, vjp = jax.vjp(
                        lambda q, k, v, la, st: _gla_chunk(q, k, v, la, st, tri, msk, fwd),
                        q_ref[r, ks], k_ref[r, ks], v_ref[r, vs], la_ref[r, ks], s_scr[srow(c_, hh), :])
                    dq, dk, dv, dla, dst = vjp((do_ref[r, vs], g))
                    dq_acc[r, ks] += dq
                    dk_acc[r, ks] += dk
                    dv_acc[r, vs] += dv
                    dla_ref[r, ks] = dla
                    new.append(dst)
            return tuple(new)

        lax.fori_loop(0, nc, back, (zero,) * 4)
        dq_ref[...] = dq_acc[...].astype(dq_ref.dtype)
        dk_ref[...] = dk_acc[...].astype(dk_ref.dtype)
        dv_ref[...] = dv_acc[...].astype(dv_ref.dtype)

    in_specs = _gla_specs(T) + [pl.BlockSpec((T, 2 * B_VAL_DIM), lambda p: (0, p))]
    out_specs = [
        pl.BlockSpec((T, LANES), lambda p: (0, p)), pl.BlockSpec((T, LANES), lambda p: (0, p)),
        pl.BlockSpec((T, 2 * B_VAL_DIM), lambda p: (0, p)),
        pl.BlockSpec((T, LANES), lambda p: (0, p)), pl.BlockSpec((T, LANES), lambda p: (0, p)),
    ]
    out_shape = [
        jax.ShapeDtypeStruct((T, B_QK), BF16), jax.ShapeDtypeStruct((T, B_QK), BF16), jax.ShapeDtypeStruct((T, B_V), BF16),
        jax.ShapeDtypeStruct((T, B_QK), F32), jax.ShapeDtypeStruct((T, B_QK), F32),
    ]
    scratch = [
        pltpu.VMEM((nc * SROWS, B_KEY_DIM), F32), pltpu.VMEM((nc * SROWS, B_KEY_DIM), F32),
        pltpu.VMEM((T, LANES), F32), pltpu.VMEM((T, LANES), F32), pltpu.VMEM((T, 2 * B_VAL_DIM), F32),
    ]
    return _pcall(
        body, [proj, proj, proj, laf, lab, do], side, grid=(B_HEADS // 2,), in_specs=in_specs, out_specs=out_specs,
        out_shape=out_shape, scratch_shapes=scratch, compiler_params=_cparams(("arbitrary",)), name="gla_bwd")


def _shift_raw(x, k):
    T = x.shape[0]
    r = lax.broadcasted_iota(jnp.int32, x.shape, 0)
    if k > 0:
        return jnp.where(r >= k, pltpu.roll(x, k, 0), 0.0)
    return jnp.where(r < T + k, pltpu.roll(x, T + k, 0), 0.0)


@functools.partial(jax.custom_vjp, nondiff_argnums=(1,))
def _shift(x, k):
    return _shift_raw(x, k)


_shift.defvjp(lambda x, k: (_shift_raw(x, k), None), lambda k, _, g: (_shift_raw(g, -k),))


def _scan_raw(a, u, reverse):
    T = a.shape[0]
    d = 1
    while d < T:
        k = -d if reverse else d
        u = a * _shift_raw(u, k) + u
        a = a * _shift_raw(a, k)
        d *= 2
    return u


@functools.partial(jax.custom_vjp, nondiff_argnums=(2,))
def _scan(a, u, reverse):
    return _scan_raw(a, u, reverse)


def _scan_f(a, u, reverse):
    h = _scan_raw(a, u, reverse)
    return h, (a, h)


def _scan_b(reverse, res, dh):
    a, h = res
    k = 1 if reverse else -1
    du = _scan_raw(_shift_raw(a, k), dh, not reverse)
    return du * _shift_raw(h, -k), du


_scan.defvjp(_scan_f, _scan_b)


def f_lru(cx, cy, cw, cb, wa, ba, wx, bx, lam, diff):
    shift, scan = (_shift, _scan) if diff else (_shift_raw, _scan_raw)
    xc = cx * cw[CONV_LEFT:CONV_LEFT + 1]
    for j in range(CONV_WIDTH):
        if j != CONV_LEFT:
            xc = xc + shift(cx, CONV_LEFT - j) * cw[j:j + 1]
    xc = xc + cb
    xb = xc.astype(BF16)
    h = None
    for s in range(2):
        r = jax.nn.sigmoid(jnp.dot(xb, wa[s].astype(BF16), preferred_element_type=F32) + ba[s:s + 1])
        i = jax.nn.sigmoid(jnp.dot(xb, wx[s].astype(BF16), preferred_element_type=F32) + bx[s:s + 1])
        log_a = -LRU_C * r * jax.nn.softplus(-lam[s:s + 1])
        a = jnp.exp(log_a)
        one_minus_a2 = -jnp.tanh(log_a) * (a * a + 1.0)
        u = jnp.sqrt(one_minus_a2) * (i * xc)
        hs = scan(a, u, s == 1)
        h = hs if h is None else h + hs
    return h * jax.nn.gelu(cy)


def _lru_specs(T):
    xc, yc = OFF_CX // LANES, OFF_CY // LANES
    return [
        pl.BlockSpec((T, LANES), lambda b: (0, xc + b)),
        pl.BlockSpec((T, LANES), lambda b: (0, yc + b)),
        pl.BlockSpec((CONV_WIDTH, LANES), lambda b: (0, b)),
        pl.BlockSpec((1, LANES), lambda b: (0, b)),
        pl.BlockSpec((2, None, C_BLOCK_DIM, C_BLOCK_DIM), lambda b: (0, b, 0, 0)),
        pl.BlockSpec((2, LANES), lambda b: (0, b)),
        pl.BlockSpec((2, None, C_BLOCK_DIM, C_BLOCK_DIM), lambda b: (0, b, 0, 0)),
        pl.BlockSpec((2, LANES), lambda b: (0, b)),
        pl.BlockSpec((2, LANES), lambda b: (0, b)),
    ]


def lru_fwd(proj, cw, cb, wa, ba, wx, bx, lam, side=None):
    T = proj.shape[0]

    def body(cx, cy, cw_r, cb_r, wa_r, ba_r, wx_r, bx_r, lam_r, o_ref):
        o = f_lru(cx[...], cy[...], cw_r[...], cb_r[...], wa_r[...], ba_r[...], wx_r[...], bx_r[...], lam_r[...], False)
        o_ref[...] = o.astype(o_ref.dtype)

    return _pcall(
        body, [proj, proj, cw, cb, wa, ba, wx, bx, lam], side, grid=(C_BLOCKS,), in_specs=_lru_specs(T),
        out_specs=pl.BlockSpec((T, LANES), lambda b: (0, b)), out_shape=jax.ShapeDtypeStruct((T, C_WIDTH), BF16),
        compiler_params=_cparams(("arbitrary",)), name="lru_fwd")[0]


def lru_bwd(proj, cw, cb, wa, ba, wx, bx, lam, dcat, side=None):
    T = proj.shape[0]
    oc = (A_Q + B_V) // LANES

    def body(cx, cy, cw_r, cb_r, wa_r, ba_r, wx_r, bx_r, lam_r, do_ref, *outs):
        _, vjp = jax.vjp(functools.partial(f_lru, diff=True), cx[...], cy[...], cw_r[...], cb_r[...], wa_r[...],
                         ba_r[...], wx_r[...], bx_r[...], lam_r[...])
        grads = vjp(do_ref[...])
        for o, g in zip(outs, grads):
            o[...] = g.astype(o.dtype)

    specs = _lru_specs(T)
    out_specs = [pl.BlockSpec((T, LANES), lambda b: (0, b)), pl.BlockSpec((T, LANES), lambda b: (0, b))] + specs[2:]
    out_shape = [jax.ShapeDtypeStruct((T, C_WIDTH), BF16), jax.ShapeDtypeStruct((T, C_WIDTH), BF16)] + \
                [jax.ShapeDtypeStruct(p.shape, F32) for p in (cw, cb, wa, ba, wx, bx, lam)]
    return _pcall(
        body, [proj, proj, cw, cb, wa, ba, wx, bx, lam, dcat], side, grid=(C_BLOCKS,),
        in_specs=specs + [pl.BlockSpec((T, LANES), lambda b: (0, oc + b))], out_specs=out_specs, out_shape=out_shape,
        compiler_params=_cparams(("arbitrary",)), name="lru_bwd")


def all_gather8(name, blk):
    def body(x_ref, out_ref, send_sems, recv_sems):
        x, y, c, _, others = _place()
        sibling = (x, y, 1 - c)

        def slab(px, py, pc):
            return out_ref.at[4 * px + 2 * py + pc]

        first = [_remote(x_ref, slab(x, y, c), send_sems, recv_sems, 0, sibling)]
        first += [_remote(x_ref, slab(x, y, c), send_sems, recv_sems, 1 + j, (*ch, c)) for j, ch in enumerate(others)]
        for cp in first:
            cp.start()
        passed = [_remote(slab(*ch, c), slab(*ch, c), send_sems, recv_sems, 4 + j, sibling) for j, ch in enumerate(others)]
        for j, ch in enumerate(others):
            _remote(x_ref, slab(*ch, c), send_sems, recv_sems, 1 + j, (x, y, c)).wait_recv()
            passed[j].start()
        _remote(x_ref, slab(x, y, 1 - c), send_sems, recv_sems, 0, (x, y, c)).wait_recv()
        for j, ch in enumerate(others):
            _remote(x_ref, slab(*ch, 1 - c), send_sems, recv_sems, 4 + j, (x, y, c)).wait_recv()
        for cp in first + passed:
            cp.wait_send()

    out = pl.pallas_call(
        body, out_shape=jax.ShapeDtypeStruct((8,) + blk.shape, blk.dtype), in_specs=[HBM_SPEC], out_specs=HBM_SPEC,
        scratch_shapes=[pltpu.SemaphoreType.DMA((7,)), pltpu.SemaphoreType.DMA((7,))], name=name)(blk)
    me = 4 * lax.axis_index("x") + 2 * lax.axis_index("y") + lax.axis_index("c")
    return lax.dynamic_update_index_in_dim(out, blk, me, 0)


def _exchange_call(name, body, arrays, out_shapes, n_sems):
    n = len(arrays)

    def kernel_body(*refs):
        body(refs[:n], refs[n:2 * n], refs[2 * n], refs[2 * n + 1])

    return pl.pallas_call(
        kernel_body, out_shape=out_shapes, in_specs=[HBM_SPEC] * n, out_specs=[HBM_SPEC] * n,
        scratch_shapes=[pltpu.SemaphoreType.DMA((n * n_sems,)), pltpu.SemaphoreType.DMA((n * n_sems,))], name=name)(*arrays)


def chip_gather(name, shards):
    def body(ins, outs, send_sems, recv_sems):
        x, y, c, chip, others = _place()
        sibling = (x, y, 1 - c)
        first, passed = [], []
        for a, (x_ref, out_ref) in enumerate(zip(ins, outs)):
            first += [_remote(x_ref.at[c], out_ref.at[chip, c], send_sems, recv_sems, 6 * a + j, (*ch, c))
                      for j, ch in enumerate(others)]
        for cp in first:
            cp.start()
        for a, (x_ref, out_ref) in enumerate(zip(ins, outs)):
            for j, ch in enumerate(others):
                here = out_ref.at[_chip_index(ch), c]
                _remote(x_ref.at[c], here, send_sems, recv_sems, 6 * a + j, (x, y, c)).wait_recv()
                cp = _remote(here, here, send_sems, recv_sems, 6 * a + 3 + j, sibling)
                cp.start()
                passed.append(cp)
        for a, (x_ref, out_ref) in enumerate(zip(ins, outs)):
            for j, ch in enumerate(others):
                _remote(x_ref.at[c], out_ref.at[_chip_index(ch), 1 - c], send_sems, recv_sems, 6 * a + 3 + j, (x, y, c)).wait_recv()
        for cp in first + passed:
            cp.wait_send()

    outs = _exchange_call(name, body, shards, [jax.ShapeDtypeStruct((4,) + s.shape, s.dtype) for s in shards], 6)
    chip = 2 * lax.axis_index("x") + lax.axis_index("y")
    return [lax.dynamic_update_index_in_dim(o, s, chip, 0) for o, s in zip(outs, shards)]


def chip_scatter(name, parts):
    def body(ins, outs, send_sems, recv_sems):
        x, y, c, chip, others = _place()
        sends = []
        for a, (x_ref, out_ref) in enumerate(zip(ins, outs)):
            sends += [_remote(x_ref.at[_chip_index(ch)], out_ref.at[chip], send_sems, recv_sems, 3 * a + j, (*ch, c))
                      for j, ch in enumerate(others)]
        for cp in sends:
            cp.start()
        for a, (x_ref, out_ref) in enumerate(zip(ins, outs)):
            for j, ch in enumerate(others):
                _remote(x_ref.at[chip], out_ref.at[_chip_index(ch)], send_sems, recv_sems, 3 * a + j, (x, y, c)).wait_recv()
        for cp in sends:
            cp.wait_send()

    return _exchange_call(name, body, parts, [jax.ShapeDtypeStruct(p.shape, p.dtype) for p in parts], 3)


def sibling_take(name, halves):
    def body(ins, outs, send_sems, recv_sems):
        x, y, c, _, _ = _place()
        cps = [_remote(x_ref.at[s, 1 - c], out_ref.at[s], send_sems, recv_sems, 4 * a + s, (x, y, 1 - c))
               for a, (x_ref, out_ref) in enumerate(zip(ins, outs)) for s in range(4)]
        for cp in cps:
            cp.start()
        for cp in cps:
            cp.wait()

    return _exchange_call(name, body, halves,
                          [jax.ShapeDtypeStruct((h.shape[0],) + h.shape[2:], h.dtype) for h in halves], 4)


def sibling_pair(name, mine):
    def body(ins, outs, send_sems, recv_sems):
        x, y, c, _, _ = _place()
        cps = [_remote(x_ref, out_ref.at[c], send_sems, recv_sems, a, (x, y, 1 - c))
               for a, (x_ref, out_ref) in enumerate(zip(ins, outs))]
        for cp in cps:
            cp.start()
        for a, (x_ref, out_ref) in enumerate(zip(ins, outs)):
            _remote(x_ref, out_ref.at[1 - c], send_sems, recv_sems, a, (x, y, c)).wait_recv()
        for cp in cps:
            cp.wait_send()

    outs = _exchange_call(name, body, mine, [jax.ShapeDtypeStruct((2,) + m.shape, m.dtype) for m in mine], 1)
    core = lax.axis_index("c")
    return [lax.dynamic_update_index_in_dim(o, m, core, 0) for o, m in zip(outs, mine)]


def sum_slabs(name, r, out_dtype, tr):
    S, R, W = r.shape

    def body(*refs):
        t = refs[0][...].astype(F32)
        for s in range(1, S):
            t = t + refs[s][...].astype(F32)
        refs[S][...] = t.astype(out_dtype)

    return pl.pallas_call(
        body, grid=(R // tr,), in_specs=[pl.BlockSpec((None, tr, W), lambda i, s=s: (s, i, 0)) for s in range(S)],
        out_specs=pl.BlockSpec((tr, W), lambda i: (i, 0)), out_shape=jax.ShapeDtypeStruct((R, W), out_dtype),
        compiler_params=_cparams(("parallel",)), name=name)(*([r] * S))


def sum_chips(name, arrived, own, chip, tr):
    S, R, W = arrived.shape

    def body(chip_ref, own_ref, *refs):
        me = chip_ref[0]
        t = None
        for s in range(S):
            term = jnp.where(me == s, own_ref[...].astype(F32), refs[s][...].astype(F32))
            t = term if t is None else t + term
        refs[S][...] = t

    grid_spec = pltpu.PrefetchScalarGridSpec(
        num_scalar_prefetch=1, grid=(R // tr,),
        in_specs=[pl.BlockSpec((None, tr, W), lambda i, ch: (ch[0], i, 0))] +
                 [pl.BlockSpec((None, tr, W), lambda i, ch, s=s: (s, i, 0)) for s in range(S)],
        out_specs=pl.BlockSpec((tr, W), lambda i, ch: (i, 0)))
    return pl.pallas_call(body, grid_spec=grid_spec, out_shape=jax.ShapeDtypeStruct((R, W), F32),
                          compiler_params=_cparams(("parallel",)), name=name)(
                              chip.reshape(1).astype(jnp.int32), own, *([arrived] * S))


def add_kept_half(name, halves, got, c, tr):
    S, _, R, W = halves.shape

    def body(c_ref, h_ref, g_ref, o_ref):
        o_ref[...] = (h_ref[...].astype(F32) + g_ref[...].astype(F32)).astype(o_ref.dtype)

    grid_spec = pltpu.PrefetchScalarGridSpec(
        num_scalar_prefetch=1, grid=(S, R // tr),
        in_specs=[pl.BlockSpec((None, None, tr, W), lambda s, i, c_ref: (s, c_ref[0], i, 0)),
                  pl.BlockSpec((None, tr, W), lambda s, i, c_ref: (s, i, 0))],
        out_specs=pl.BlockSpec((None, tr, W), lambda s, i, c_ref: (s, i, 0)))
    return pl.pallas_call(body, grid_spec=grid_spec, out_shape=jax.ShapeDtypeStruct((S, R, W), halves.dtype),
                          compiler_params=_cparams(("parallel", "parallel")), name=name)(
                              c.reshape(1).astype(jnp.int32), halves, got)


def adamw_layer(name, l, g, row_off, w, m, v, prev, tr, tc=None, side=None):
    L, R, C = w.shape
    tc = C if tc is None else tc
    off = row_off // tr

    def body(g_ref, w_ref, m_ref, v_ref, *rest):
        outs = rest[-4:]
        gv = g_ref[...]
        d, mn, vn = f_adamw(w_ref[...], gv, m_ref[...], v_ref[...])
        for o, val in zip(outs, (gv, d, mn, vn)):
            o[...] = val

    slab = pl.BlockSpec((None, tr, tc), lambda i, j: (l, i, j))
    in_specs = [pl.BlockSpec((tr, tc), lambda i, j: (off + i, j)), slab, slab, slab]
    args = [g, w, m, v]
    aliases = {}
    if prev is not None:
        in_specs += [pl.BlockSpec(memory_space=pl.ANY)] * 4
        args += list(prev)
        aliases = {4 + k: k for k in range(4)}
    return _pcall(
        body, args, side, grid=(R // tr, C // tc), in_specs=in_specs, out_specs=[slab] * 4,
        out_shape=[jax.ShapeDtypeStruct((L, R, C), F32)] * 4, input_output_aliases=aliases,
        compiler_params=_cparams(("parallel", "parallel")), name=name)


BIG = ("w_in", "w_out", "xq", "xk", "xv", "xo", "w_up", "w_down")
GROUPS = {"mix": ("w_out", "xq", "xk", "xv", "xo"), "ff": ("w_up", "w_down")}
PACK_ROWS = {"w_out": 512, "xq": 512, "xk": 512, "xv": 512, "xo": 512, "w_up": 2048, "w_down": 2048}
GROUP_ROWS = {g: sum(PACK_ROWS[n] for n in names) for g, names in GROUPS.items()}
SUM_TILE = 256
PACK_OFF = {}
for _names in GROUPS.values():
    _o = 0
    for _n in _names:
        PACK_OFF[_n] = _o
        _o += PACK_ROWS[_n]

_SPLIT_OFF = np.cumsum((0,) + SPLIT_SIZES)
_KORDER = (0, 1, 2, 3, 4, 5, 6, 9, 10, 7, 8)


def w_in_to_kernel_cols(w):
    parts = [w[..., _SPLIT_OFF[i]:_SPLIT_OFF[i + 1]] for i in _KORDER]
    parts.append(jnp.zeros(w.shape[:-1] + (D_INP - D_IN,), w.dtype))
    return jnp.concatenate(parts, axis=-1)


def w_in_from_kernel_cols(w):
    offs = np.cumsum((0,) + tuple(SPLIT_SIZES[i] for i in _KORDER))
    pos = {k: (offs[n], offs[n + 1]) for n, k in enumerate(_KORDER)}
    return jnp.concatenate([w[..., pos[i][0]:pos[i][1]] for i in range(len(SPLIT_SIZES))], axis=-1)


def pack_shards(shards, group, dtype):
    return jnp.concatenate([shards[n].astype(dtype) for n in GROUPS[group]], axis=-2)


def unpack_rows(packed, name):
    return packed[..., PACK_OFF[name]:PACK_OFF[name] + PACK_ROWS[name], :]


WEIGHTS = ("rel_bias", "w_in", "w_out", "attn_sink", "gla_w2_f", "gla_b2_f", "gla_w2_b", "gla_b2_b", "gla_norm", "conv_w",
           "conv_b", "lru_wa", "lru_ba", "lru_wx", "lru_bx", "lru_lambda", "xq", "xk", "xv", "xo", "w_up", "w_down",
           "norm_mix_pre", "norm_mix_post", "norm_mem", "norm_x_pre", "norm_x_post", "norm_ff_pre", "norm_ff_post")
SMALL = tuple(n for n in WEIGHTS if n not in BIG)
SMALL_SHARDED = ("gla_w2_f", "gla_w2_b", "conv_w", "lru_ba", "lru_bx", "lru_lambda")
ROW_TILE = 256
SMALL_TILE = 512
W_IN_TILE = (344, 1024)
RIDE_PIECE_ROWS = (256, 512)


def _small_rows(n):
    return -(-n // (SUBLANES * LANES)) * SUBLANES


def _as_rows(a2):
    L, n = a2.shape
    rows = _small_rows(n)
    if rows * LANES != n:
        a2 = jnp.pad(a2, ((0, 0), (0, rows * LANES - n)))
    return a2.reshape(L * rows, LANES)


def _pack_small(items, layered):
    parts = []
    for it, lay in zip(items, layered):
        if isinstance(it, (list, tuple)):
            parts += [_as_rows(e.astype(F32).reshape(1, -1)) for e in it]
        else:
            parts.append(_as_rows(it.astype(F32).reshape(it.shape[0] if lay else 1, -1)))
    pad = -sum(p.shape[0] for p in parts) % SMALL_TILE
    if pad:
        parts.append(jnp.zeros((pad, LANES), F32))
    return jnp.concatenate(parts, axis=0)


def _unpack_small(buf, shapes, layered):
    lead = buf.shape[:-2]
    out, o = [], 0
    for s, lay in zip(shapes, layered):
        L = s[0] if lay else 1
        n = int(np.prod(s)) // L
        rows = _small_rows(n)
        part = buf[..., o:o + L * rows, :]
        if n != rows * LANES:
            part = part.reshape(lead + (L, rows * LANES))[..., :n]
        out.append(part.reshape(lead + tuple(s)))
        o += L * rows
    return out


def _relu2(r):
    return r, jnp.square(jnp.maximum(r, 0.0))


def _drelu2(r, u):
    return r * (2.0 * jnp.maximum(u, 0.0))


def kernel(x, mem, rel_bias, w_in, w_out, attn_sink, gla_w2_f, gla_b2_f, gla_w2_b, gla_b2_b, gla_norm, conv_w, conv_b, lru_wa, lru_ba, lru_wx, lru_bx, lru_lambda, xq, xk, xv, xo, w_up, w_down, norm_mix_pre, norm_mix_post, norm_mem, norm_x_pre, norm_x_post, norm_ff_pre, norm_ff_post, loss_target, m_rel_bias, m_w_in, m_w_out, m_attn_sink, m_gla_w2_f, m_gla_b2_f, m_gla_w2_b, m_gla_b2_b, m_gla_norm, m_conv_w, m_conv_b, m_lru_wa, m_lru_ba, m_lru_wx, m_lru_bx, m_lru_lambda, m_xq, m_xk, m_xv, m_xo, m_w_up, m_w_down, m_norm_mix_pre, m_norm_mix_post, m_norm_mem, m_norm_x_pre, m_norm_x_post, m_norm_ff_pre, m_norm_ff_post, v_rel_bias, v_w_in, v_w_out, v_attn_sink, v_gla_w2_f, v_gla_b2_f, v_gla_w2_b, v_gla_b2_b, v_gla_norm, v_conv_w, v_conv_b, v_lru_wa, v_lru_ba, v_lru_wx, v_lru_bx, v_lru_lambda, v_xq, v_xk, v_xv, v_xo, v_w_up, v_w_down, v_norm_mix_pre, v_norm_mix_post, v_norm_mem, v_norm_x_pre, v_norm_x_post, v_norm_ff_pre, v_norm_ff_post):
    w_args = (rel_bias, w_in, w_out, attn_sink, gla_w2_f, gla_b2_f, gla_w2_b, gla_b2_b, gla_norm, conv_w, conv_b, lru_wa,
              lru_ba, lru_wx, lru_bx, lru_lambda, xq, xk, xv, xo, w_up, w_down, norm_mix_pre, norm_mix_post, norm_mem,
              norm_x_pre, norm_x_post, norm_ff_pre, norm_ff_post)
    m_args = (m_rel_bias, m_w_in, m_w_out, m_attn_sink, m_gla_w2_f, m_gla_b2_f, m_gla_w2_b, m_gla_b2_b, m_gla_norm, m_conv_w,
              m_conv_b, m_lru_wa, m_lru_ba, m_lru_wx, m_lru_bx, m_lru_lambda, m_xq, m_xk, m_xv, m_xo, m_w_up, m_w_down,
              m_norm_mix_pre, m_norm_mix_post, m_norm_mem, m_norm_x_pre, m_norm_x_post, m_norm_ff_pre, m_norm_ff_post)
    v_args = (v_rel_bias, v_w_in, v_w_out, v_attn_sink, v_gla_w2_f, v_gla_b2_f, v_gla_w2_b, v_gla_b2_b, v_gla_norm, v_conv_w,
              v_conv_b, v_lru_wa, v_lru_ba, v_lru_wx, v_lru_bx, v_lru_lambda, v_xq, v_xk, v_xv, v_xo, v_w_up, v_w_down,
              v_norm_mix_pre, v_norm_mix_post, v_norm_mem, v_norm_x_pre, v_norm_x_post, v_norm_ff_pre, v_norm_ff_post)
    Wt, Mo, Vo = dict(zip(WEIGHTS, w_args)), dict(zip(WEIGHTS, m_args)), dict(zip(WEIGHTS, v_args))
    x, mem, tgt = x[0], mem[0], loss_target[0]
    D = D_MODEL
    depth = w_in.shape[0]
    chip = 2 * lax.axis_index("x") + lax.axis_index("y")
    core = lax.axis_index("c")

    sm_shapes = [Wt[n].shape for n in SMALL_SHARDED]
    yes = [True] * len(SMALL_SHARDED)
    g8 = all_gather8("gather_small_weights", _pack_small([Wt[n] for n in SMALL_SHARDED], yes))
    per_chip = _unpack_small(g8[0::2], sm_shapes, yes)
    whole = {n: jnp.concatenate([p[j] for j in range(4)], axis=-1) for n, p in zip(SMALL_SHARDED, per_chip)}

    def group_shards(l, group):
        shard = pack_shards({n: Wt[n][l] for n in GROUPS[group]}, group, BF16).reshape(2, GROUP_ROWS[group] // 2, D)
        return [shard] + ([w_in[l].astype(BF16).reshape(2, D // 2, D_IN // 4)] if group == "mix" else [])

    def whole_weights(group, gathered):
        g = gathered[0].reshape(4, GROUP_ROWS[group], D)
        if group == "ff":
            return {"w_up": unpack_rows(g, "w_up"), "w_down": unpack_rows(g, "w_down").reshape(D_FF, D)}
        W = {n: unpack_rows(g, n).reshape(D, D) for n in GROUPS["mix"]}
        win = gathered[1].reshape(4, D, D_IN // 4).transpose(1, 0, 2).reshape(D, D_IN)
        W["w_in"] = w_in_to_kernel_cols(win)
        return W

    first = group_shards(0, "mix")
    Wgot = {(0, "mix"): whole_weights("mix", chip_gather("gather_first_weights", first))}
    gathers = Stream()
    riding = {}
    for l in range(depth):
        for group in ("mix", "ff"):
            if (l, group) != (0, "mix"):
                shards = group_shards(l, group)
                riding[l, group] = (shards, gathers.add(RidingExchange("gather", shards, RIDE_PIECE_ROWS)))

    def need_weights(l, group):
        if (l, group) not in Wgot:
            shards, exchange = riding.pop((l, group))
            got = gathers.finish(exchange, "gather_rest")
            Wgot[l, group] = whole_weights(
                group, [lax.dynamic_update_index_in_dim(b, s, chip, 0) for b, s in zip(got, shards)])
        return Wgot[l, group]

    bucket = t5_bucket_map()
    bias = bias_table_fwd(rel_bias, bucket)

    def gain(name, l):
        return Wt[name][l][None]

    def layer_params(l):
        w2fp = jnp.zeros((LANES, B_QK), F32).at[0:GATE_RANK].set(whole["gla_w2_f"][l])
        w2bp = jnp.zeros((LANES, B_QK), F32).at[GATE_RANK:2 * GATE_RANK].set(whole["gla_w2_b"][l])
        gate = [w2fp, gla_b2_f[l][None], w2bp, gla_b2_b[l][None]]
        lru = [whole["conv_w"][l], conv_b[l][None], lru_wa[l], whole["lru_ba"][l], lru_wx[l], whole["lru_bx"][l],
               whole["lru_lambda"][l]]
        return attn_sink[l].reshape(A_KV_HEADS, A_GROUP, 1), gate, gla_norm[l][None], lru

    saved = []
    xcur = x
    (h1,) = rowmap("norm_first", f_norm, [x], [gain("norm_mix_pre", 0)], [(D, BF16)], ROW_TILE)
    loss_acc = None
    Wfull = []
    ride = gathers.take
    for l in range(depth):
        W = dict(need_weights(l, "mix"))
        sink3, gate, gn, lru = layer_params(l)
        (proj,) = mm("mm_in", h1, W["w_in"], "nn", [(F32, 1)], pm=512, pn=1408, side=ride(2))
        oa = attn_fwd(proj, bias, sink3, side=ride(3))
        zrow, grow = (proj, LANES, OFF_Z // LANES), (proj, B_V, OFF_BG // B_V)
        laf, lab = rowmap("gla_gate", f_gla_gate, [zrow], gate, [(B_QK, F32), (B_QK, F32)], ROW_TILE)
        oraw = gla_fwd(proj, laf, lab, side=ride(2))
        (ob,) = rowmap("gla_post", f_gla_post, [oraw, grow], [gn], [(B_V, BF16)], ROW_TILE)
        oc = lru_fwd(proj, *lru, side=ride(1))
        cat = jnp.concatenate([oa, ob, oc], axis=1)
        (mixed,) = mm("mm_out", cat, W["w_out"], "nn", [(F32, 1)], side=ride(1))
        x1, h2 = rowmap("resnorm_mix", f_resnorm, [xcur, mixed], [gain("norm_mix_post", l), gain("norm_x_pre", l)],
                        [(D, F32), (D, BF16)], ROW_TILE)
        (memn,) = rowmap("norm_mem", f_norm, [mem], [gain("norm_mem", l)], [(D, BF16)], ROW_TILE)
        (q,) = mm("mm_xq", h2, W["xq"], "nn", [(BF16, 1)], side=ride(1))
        (k,) = mm("mm_xk", memn, W["xk"], "nn", [(F32, 1)])
        (v,) = mm("mm_xv", memn, W["xv"], "nn", [(F32, 1)])
        (o,) = rowmap("xattn", f_xattn, [q], [k, v], [(D, BF16)], ROW_TILE)
        (xo_out,) = mm("mm_xo", o, W["xo"], "nn", [(F32, 1)], side=ride(1))
        x2, h3 = rowmap("resnorm_x", f_resnorm, [x1, xo_out], [gain("norm_x_post", l), gain("norm_ff_pre", l)],
                        [(D, F32), (D, BF16)], ROW_TILE)
        W.update(need_weights(l, "ff"))
        Wfull.append(W)
        u, act = mm("mm_up", h3, W["w_up"], "nn", [(F32, 1), (BF16, 1)], epilogue=_relu2, side=ride(3))
        (ff,) = mm("mm_down", act, W["w_down"], "nn", [(F32, 1)], side=ride(2))
        saved.append(dict(x0=xcur, h1=h1, proj=proj, laf=laf, lab=lab, oraw=oraw, cat=cat, mixed=mixed, x1=x1, h2=h2,
                          memn=memn, q=q, k=k, v=v, o=o, xo_out=xo_out, x2=x2, h3=h3, u=u, act=act, ff=ff))
        if l < depth - 1:
            xcur, h1 = rowmap("resnorm_ff", f_resnorm, [x2, ff], [gain("norm_ff_post", l), gain("norm_mix_pre", l + 1)],
                              [(D, F32), (D, BF16)], ROW_TILE)
        else:
            (loss_acc,) = rowmap("final_loss", f_final_loss, [x2, ff, tgt], [gain("norm_ff_post", l)], [], ROW_TILE,
                                 accs=[(1, 1)])
    loss = lax.psum(loss_acc[0, 0], ("x", "y", "c"))

    small_g = {n: [None] * depth for n in SMALL if n != "rel_bias"}
    adam = {}
    dbias_all = []
    dx_next = dh1_next = None
    grad_x = None
    scatters = Stream()
    ride = scatters.take
    taking = []
    inflight = []
    sharing = []

    def start_reduce(lyr, group, dW):
        if group == "ff":
            pack = dW["ff"]
        else:
            pack = pack_shards({n: dW[n].reshape(4, D // 4, D) for n in GROUPS["mix"]}, group, BF16)
        halves = [pack.reshape(4, 2, GROUP_ROWS[group] // 2, D)]
        if group == "mix":
            g_in = w_in_from_kernel_cols(dW["w_in"]).reshape(D, 4, D_IN // 4).transpose(1, 0, 2)
            halves.append(g_in.reshape(4, 2, D // 2, D_IN // 4))
        taking.append((lyr, group, halves, scatters.add(RidingExchange("take", halves, RIDE_PIECE_ROWS))))

    def continue_reduce():
        lyr, group, halves, exchange = taking.pop()
        got = scatters.finish(exchange, "reduce_to_half_owner")
        sums = [add_kept_half("reduce_chip_sum", h, g, core, SUM_TILE) for h, g in zip(halves, got)]
        inflight.append((lyr, group, sums, scatters.add(RidingExchange("scatter", sums, RIDE_PIECE_ROWS))))

    def finish_reduce():
        lyr, group, sums, exchange = inflight.pop(0)
        arrived = scatters.finish(exchange, "reduce_rest")
        totals = [sum_chips("reduce_sum_chips", a, s, chip, SUM_TILE) for a, s in zip(arrived, sums)]
        sharing.append((lyr, group, totals, scatters.add(RidingExchange("pair", totals, RIDE_PIECE_ROWS))))

    w_in_t = [jnp.swapaxes(a, 1, 2) for a in (w_in, m_w_in, v_w_in)]

    def run_updates():
        for lyr, group, totals, exchange in sharing:
            both = scatters.finish(exchange, "reduce_share_halves")
            fulls = [lax.dynamic_update_index_in_dim(b, t, core, 0) for b, t in zip(both, totals)]
            full = fulls[0].reshape(GROUP_ROWS[group], D)
            for n in GROUPS[group]:
                adam[n] = adamw_layer("adamw_" + n, lyr, full, PACK_OFF[n], Wt[n], Mo[n], Vo[n], adam.get(n), SUM_TILE)
            if group == "mix":
                g = fulls[1].reshape(D, D_IN // 4).T
                adam["w_in"] = adamw_layer("adamw_w_in", lyr, g, 0, *w_in_t, adam.get("w_in"), *W_IN_TILE)

    for l in reversed(range(depth)):
        W, S = Wfull[l], saved[l]
        sink3, gate, gn, lru = layer_params(l)
        if l == depth - 1:
            (dx2, dff), (dgp,) = rowmap_bwd("final_bwd", f_final_rows, [S["x2"], S["ff"], tgt], [gain("norm_ff_post", l)],
                                            [None], ROW_TILE, [F32, F32, None], [True])
        else:
            (dx2, dff), (dgp, dgn_next) = rowmap_bwd(
                "resnorm_ff_bwd", f_resnorm, [S["x2"], S["ff"]], [gain("norm_ff_post", l), gain("norm_mix_pre", l + 1)],
                [dx_next, dh1_next], ROW_TILE, [F32, F32], [True, True])
            small_g["norm_mix_pre"][l + 1] = dgn_next[0]
        small_g["norm_ff_post"][l] = dgp[0]
        dW = {}
        (du,) = mm("mm_down_bwd", dff, W["w_down"], "nt", [(BF16, 1)], epilogue=_drelu2, extras=[S["u"]], side=ride(3))
        (pack,) = mm("mm_down_wgrad", S["act"], dff, "tn",
                     [(BF16, ("rows", GROUP_ROWS["ff"], PACK_OFF["w_down"], None))], side=ride(2))
        (dW["ff"],) = mm("mm_up_wgrad", S["h3"], du, "tn",
                         [(BF16, ("cols", GROUP_ROWS["ff"], PACK_OFF["w_up"], pack))], side=ride(2))
        if inflight:
            finish_reduce()
        start_reduce(l, "ff", dW)
        (dh3,) = mm("mm_up_bwd", du, W["w_up"], "nt", [(F32, 1)], pk=D, side=ride(2))
        continue_reduce()
        (dx1, dxo_out), (dg1, dg2) = rowmap_bwd(
            "resnorm_x_bwd", f_resnorm, [S["x1"], S["xo_out"]], [gain("norm_x_post", l), gain("norm_ff_pre", l)],
            [dx2, dh3], ROW_TILE, [F32, F32], [True, True])
        small_g["norm_x_post"][l], small_g["norm_ff_pre"][l] = dg1[0], dg2[0]
        (do,) = mm("mm_xo_bwd", dxo_out, W["xo"], "nt", [(F32, 1)])
        (dW["xo"],) = mm("mm_xo_wgrad", S["o"], dxo_out, "tn", [(BF16, 1)])
        (dq,), (dk, dv) = rowmap_bwd("xattn_bwd", f_xattn, [S["q"]], [S["k"], S["v"]], [do], ROW_TILE, [BF16], [True, True])
        (dW["xq"],) = mm("mm_xq_wgrad", S["h2"], dq, "tn", [(BF16, 1)])
        (dh2,) = mm("mm_xq_bwd", dq, W["xq"], "nt", [(F32, 1)])
        (dW["xk"],) = mm("mm_xk_wgrad", S["memn"], dk, "tn", [(BF16, 1)])
        (dW["xv"],) = mm("mm_xv_wgrad", S["memn"], dv, "tn", [(BF16, 1)])
        (dmk,) = mm("mm_xk_bwd", dk, W["xk"], "nt", [(F32, 1)])
        (dmv,) = mm("mm_xv_bwd", dv, W["xv"], "nt", [(F32, 1)])
        _, (dgm,) = rowmap_bwd("norm_mem_bwd", f_norm_twice, [mem], [gain("norm_mem", l)], [dmk, dmv], ROW_TILE, [None], [True])
        small_g["norm_mem"][l] = dgm[0]
        (dx0, dmixed), (dg1, dg2) = rowmap_bwd(
            "resnorm_mix_bwd", f_resnorm, [S["x0"], S["mixed"]], [gain("norm_mix_post", l), gain("norm_x_pre", l)],
            [dx1, dh2], ROW_TILE, [F32, F32], [True, True])
        small_g["norm_mix_post"][l], small_g["norm_x_pre"][l] = dg1[0], dg2[0]
        (dcat,) = mm("mm_out_bwd", dmixed, W["w_out"], "nt", [(F32, 1)])
        (dW["w_out"],) = mm("mm_out_wgrad", S["cat"], dmixed, "tn", [(BF16, 1)])
        proj = S["proj"]
        daq, dak, dav, dbias, dsink = attn_bwd(proj, bias, sink3, dcat, side=ride(3))
        dbias_all.append(dbias)
        small_g["attn_sink"][l] = dsink.reshape(A_HEADS)
        zrow, grow = (proj, LANES, OFF_Z // LANES), (proj, B_V, OFF_BG // B_V)
        (doraw, dbg), (dgn,) = rowmap_bwd("gla_post_bwd", f_gla_post, [S["oraw"], grow], [gn], [(dcat, B_V, A_Q // B_V)],
                                          ROW_TILE, [F32, BF16], [True])
        dbq, dbk, dbv, dlaf, dlab = gla_bwd(proj, S["laf"], S["lab"], doraw, side=ride(3))
        (dz,), (dw2fp, db2f, dw2bp, db2b) = rowmap_bwd("gla_gate_bwd", f_gla_gate, [zrow], gate, [dlaf, dlab], ROW_TILE,
                                                        [BF16], [True] * 4)
        small_g["gla_norm"][l] = dgn[0]
        small_g["gla_w2_f"][l], small_g["gla_b2_f"][l] = dw2fp[0:GATE_RANK], db2f[0]
        small_g["gla_w2_b"][l], small_g["gla_b2_b"][l] = dw2bp[GATE_RANK:2 * GATE_RANK], db2b[0]
        dcx, dcy, dcw, dcb, dwa, dba, dwx, dbx, dlam = lru_bwd(proj, *lru, dcat, side=ride(2))
        small_g["conv_w"][l], small_g["conv_b"][l] = dcw, dcb[0]
        small_g["lru_wa"][l], small_g["lru_ba"][l], small_g["lru_wx"][l] = dwa, dba, dwx
        small_g["lru_bx"][l], small_g["lru_lambda"][l] = dbx, dlam
        dproj = jnp.concatenate([daq, dak, dav, dbq, dbk, dbv, dbg, dcx, dcy, dz], axis=1)
        (dW["w_in"],) = mm("mm_in_wgrad", S["h1"], dproj, "tn", [(BF16, 1)], pm=512, pn=1408, side=ride(1))
        (dh1,) = mm("mm_in_bwd", dproj, W["w_in"], "nt", [(F32, 1)], side=ride(1))
        if l > 0:
            dx_next, dh1_next = dx0, dh1
        else:
            (grad_x,), (dg0,) = rowmap_bwd("norm_first_bwd", f_norm_keep, [x], [gain("norm_mix_pre", 0)], [dx0, dh1],
                                           ROW_TILE, [F32], [True])
            small_g["norm_mix_pre"][0] = dg0[0]

        finish_reduce()
        start_reduce(l, "mix", dW)
        continue_reduce()
    finish_reduce()
    run_updates()

    dtab = bias_table_bwd(dbias_all, bucket)
    small_g["rel_bias"] = dtab[:, :A_HEADS]
    layered = [n != "rel_bias" for n in SMALL]
    sg_shapes = [(depth,) + small_g[n][0].shape if lay else small_g[n].shape for n, lay in zip(SMALL, layered)]
    contributions = all_gather8("gather_small_grads", _pack_small([small_g[n] for n in SMALL], layered))
    sg_sum = sum_slabs("sum_small_grads", contributions, F32, SMALL_TILE)
    sg = dict(zip(SMALL, _unpack_small(sg_sum, sg_shapes, layered)))
    for n in SMALL_SHARDED:
        w = Wt[n].shape[-1]
        sg[n] = lax.dynamic_slice_in_dim(sg[n], chip * w, w, axis=sg[n].ndim - 1)

    grads, delta, new_m, new_v = {}, {}, {}, {}
    adam["w_in"] = [jnp.swapaxes(a, 1, 2) for a in adam["w_in"]]
    for n in BIG:
        grads[n], delta[n], new_m[n], new_v[n] = adam[n]
    shapes = [Wt[n].shape for n in SMALL]
    packs = [_pack_small([src[n] for n in SMALL], layered) for src in (Wt, sg, Mo, Vo)]
    d_, m_, v_ = rowmap("adamw_small", f_adamw, packs, [], [(LANES, F32)] * 3, SMALL_TILE)
    for n, a, b, c_ in zip(SMALL, *[_unpack_small(p, shapes, layered) for p in (d_, m_, v_)]):
        grads[n], delta[n], new_m[n], new_v[n] = sg[n], a, b, c_

    return (loss, grad_x[None], *[grads[n] for n in WEIGHTS], *[delta[n] for n in WEIGHTS],
            *[new_m[n] for n in WEIGHTS], *[new_v[n] for n in WEIGHTS])
```

```python
import functools
import math

import numpy as np
import jax
import jax.numpy as jnp
from jax import lax
from jax.experimental import pallas as pl
from jax.experimental.pallas import tpu as pltpu

F32, BF16 = jnp.float32, jnp.bfloat16
HI = lax.Precision.HIGHEST
MESH = pl.DeviceIdType.MESH

VMEM_LIMIT_BYTES = 56 * 1024 * 1024
LANES = 128
SUBLANES = 8

D_MODEL = 2048
DEPTH = 4
A_HEAD_DIM = 128
A_HEADS = 8
A_KV_HEADS = 2
A_GROUP = 4
WINDOW = 128
BLOCK = 128
N_BUCKETS = 32
MAX_DISTANCE = 128
B_HEADS = 4
B_KEY_DIM = 64
B_VAL_DIM = 128
GATE_RANK = 16
GATE_TAU = 16.0
C_WIDTH = 512
C_BLOCKS = 4
C_BLOCK_DIM = 128
CONV_WIDTH = 4
CONV_LEFT = 2
LRU_C = 8.0
X_HEADS = 4
X_HEAD_DIM = 512
D_FF = 4 * D_MODEL
EPS = 1e-6
NEG_INF = -1e30
A_Q, A_KV, B_QK, B_V = 1024, 256, 256, 512
SPLIT_SIZES = (A_Q, A_KV, A_KV, B_QK, B_QK, B_V, B_V, GATE_RANK, GATE_RANK, C_WIDTH, C_WIDTH)
D_IN = sum(SPLIT_SIZES)
D_INP = 4224
OFF_AQ, OFF_AK, OFF_AV, OFF_BQ, OFF_BK, OFF_BV, OFF_BG, OFF_CX, OFF_CY, OFF_Z = (
    0, 1024, 1280, 1536, 1792, 2048, 2560, 3072, 3584, 4096)
GLA_CHUNK = 128

ADAM_LR, ADAM_B1, ADAM_B2, ADAM_EPS, ADAM_WD, ADAM_STEP = 0.001, 0.9, 0.999, 1e-08, 0.01, 10


def _cparams(sem=None):
    return pltpu.CompilerParams(dimension_semantics=sem, vmem_limit_bytes=VMEM_LIMIT_BYTES)


def _full_spec(a):
    nd = a.ndim
    return pl.BlockSpec(a.shape, lambda *_: (0,) * nd)


def _tup(r):
    return r if isinstance(r, tuple) else (r,)


HBM_SPEC = pl.BlockSpec(memory_space=pltpu.HBM)


def _place():
    x, y, c = lax.axis_index("x"), lax.axis_index("y"), lax.axis_index("c")
    others = [(1 - x, y), (x, 1 - y), (1 - x, 1 - y)]
    return x, y, c, 2 * x + y, others


def _remote(src, dst, send_sems, recv_sems, k, to):
    return pltpu.make_async_remote_copy(src_ref=src, dst_ref=dst, send_sem=send_sems.at[k], recv_sem=recv_sems.at[k],
                                        device_id=to, device_id_type=MESH)


def _chip_index(ch):
    return 2 * ch[0] + ch[1]


def _pcall(body, args, side=None, **kw):
    if side is None:
        res = pl.pallas_call(body, **kw)(*args)
        return list(res) if isinstance(res, (list, tuple)) else [res]
    single = not isinstance(kw["out_shape"], (list, tuple))
    out_shape = [kw.pop("out_shape")] if single else list(kw.pop("out_shape"))
    out_specs = [kw.pop("out_specs")] if single else list(kw.pop("out_specs"))
    in_specs = list(kw.pop("in_specs"))
    scratch = list(kw.pop("scratch_shapes", ()))
    grid = kw.get("grid", ())
    n_in, n_out, n_scr = len(in_specs), len(out_shape), len(scratch)
    srcs, bufs = side.srcs, side.bufs
    ns, nb = len(srcs), len(bufs)

    def wrapped(*refs):
        ins = refs[:n_in]
        src_refs = refs[n_in:n_in + ns]
        o0 = n_in + ns + nb
        outs = refs[o0:o0 + n_out]
        buf_refs = refs[o0 + n_out:o0 + n_out + nb]
        s0 = o0 + n_out + nb
        scr = refs[s0:s0 + n_scr]
        send_sems, recv_sems = refs[s0 + n_scr], refs[s0 + n_scr + 1]
        first = last = None
        for d, n in enumerate(grid):
            f, l_ = pl.program_id(d) == 0, pl.program_id(d) == n - 1
            first = f if first is None else first & f
            last = l_ if last is None else last & l_
        if first is None:
            side.start(src_refs, buf_refs, send_sems, recv_sems)
            body(*ins, *outs, *scr)
            side.finish(src_refs, buf_refs, send_sems, recv_sems)
            return
        pl.when(first)(lambda: side.start(src_refs, buf_refs, send_sems, recv_sems))
        body(*ins, *outs, *scr)
        pl.when(last)(lambda: side.finish(src_refs, buf_refs, send_sems, recv_sems))

    any_spec = pl.BlockSpec(memory_space=pl.ANY)
    aliases = dict(kw.pop("input_output_aliases", {}))
    aliases.update({n_in + ns + i: n_out + i for i in range(nb)})
    cp = kw.pop("compiler_params", None)
    if grid:
        cp = _cparams(("arbitrary",) * len(grid))
    res = pl.pallas_call(
        wrapped, in_specs=in_specs + [any_spec] * (ns + nb), out_specs=out_specs + [any_spec] * nb,
        out_shape=out_shape + [jax.ShapeDtypeStruct(b.shape, b.dtype) for b in bufs],
        scratch_shapes=scratch + [pltpu.SemaphoreType.DMA((side.n_sems,)), pltpu.SemaphoreType.DMA((side.n_sems,))],
        input_output_aliases=aliases, compiler_params=cp, **kw)(*args, *srcs, *bufs)
    side.done(list(res[n_out:]))
    return list(res[:n_out])


class _Side:
    def __init__(self, parts):
        self.parts = parts
        self.srcs = [s for p in parts for s in p[0].srcs]
        self.bufs = [b for p in parts for b in p[0].bufs]
        self.n_sems = max(1, sum(4 * len(now) + 4 * len(relay) + len(last) for _, now, relay, last in parts))

    def done(self, bufs):
        for p in self.parts:
            p[0].bufs, bufs = bufs[:len(p[0].bufs)], bufs[len(p[0].bufs):]

    def _copies(self, src_refs, buf_refs, send_sems, recv_sems):
        x, y, c, chip, others = _place()
        xn, yn, dg = others
        me, sibling = (x, y, c), (x, y, 1 - c)
        mine, landing = [], []
        k = o = 0

        def pair(src, dst, got, to):
            nonlocal k
            mine.append(_remote(src, dst, send_sems, recv_sems, k, to))
            landing.append(_remote(src, got, send_sems, recv_sems, k, me))
            k += 1

        for ex, now, relay, last in self.parts:
            srcs, bufs = src_refs[o:o + len(ex.srcs)], buf_refs[o:o + len(ex.srcs)]
            o += len(ex.srcs)
            for a, r0, n in now:
                rows = pl.ds(r0, n)
                if ex.kind == "gather":
                    for ch in (xn, yn):
                        pair(srcs[a].at[c, rows], bufs[a].at[chip, c, rows], bufs[a].at[_chip_index(ch), c, rows], (*ch, c))
                elif ex.kind == "scatter":
                    for ch in others:
                        pair(srcs[a].at[_chip_index(ch), rows], bufs[a].at[chip, rows], bufs[a].at[_chip_index(ch), rows],
                             (*ch, c))
                elif ex.kind == "take":
                    for s in range(4):
                        pair(srcs[a].at[s, 1 - c, rows], bufs[a].at[s, rows], bufs[a].at[s, rows], sibling)
                else:
                    pair(srcs[a].at[rows], bufs[a].at[c, rows], bufs[a].at[1 - c, rows], sibling)
            for a, r0, n in relay:
                top, bottom, rows = pl.ds(r0, n // 2), pl.ds(r0 + n // 2, n // 2), pl.ds(r0, n)
                from_x, from_y = bufs[a].at[_chip_index(xn), c, top], bufs[a].at[_chip_index(yn), c, bottom]
                pair(from_x, from_x, bufs[a].at[_chip_index(dg), c, top], (*yn, c))
                pair(from_y, from_y, bufs[a].at[_chip_index(dg), c, bottom], (*xn, c))
                for ch in (xn, yn):
                    here = bufs[a].at[_chip_index(ch), c, rows]
                    pair(here, here, bufs[a].at[_chip_index(ch), 1 - c, rows], sibling)
            for a, r0, n in last:
                here = bufs[a].at[_chip_index(dg), c, pl.ds(r0, n)]
                pair(here, here, bufs[a].at[_chip_index(dg), 1 - c, pl.ds(r0, n)], sibling)
        return mine, landing

    def start(self, src_refs, buf_refs, send_sems, recv_sems):
        for cp in self._copies(src_refs, buf_refs, send_sems, recv_sems)[0]:
            cp.start()

    def finish(self, src_refs, buf_refs, send_sems, recv_sems):
        mine, landing = self._copies(src_refs, buf_refs, send_sems, recv_sems)
        for cp in landing:
            cp.wait_recv()
        for cp in mine:
            cp.wait_send()


class RidingExchange:
    KINDS = {"gather": (lambda s: (4,) + s, 1, False), "scatter": (lambda s: s, 1, False),
             "take": (lambda s: (s[0],) + s[2:], 2, True), "pair": (lambda s: (2,) + s, 0, True)}

    def __init__(self, kind, srcs, piece_rows):
        self.kind, self.srcs = kind, list(srcs)
        shape_of, row_axis, self.cheap = self.KINDS[kind]
        self.bufs = [lax.empty(shape_of(tuple(s.shape)), s.dtype) for s in srcs]
        heights = [s.shape[row_axis] for s in srcs]
        per = [[(a, r0, min(pr, h - r0)) for r0 in range(0, h, pr)] for a, (h, pr) in enumerate(zip(heights, piece_rows))]
        self.pieces = list(per[0])
        for extra in per[1:]:
            step = max(1, len(self.pieces) // (len(extra) + 1))
            for i, p in enumerate(extra):
                self.pieces.insert(min(len(self.pieces), (i + 1) * step + i), p)
        self.landed, self.relayed = [], []

    def busy(self):
        return bool(self.pieces or self.landed or self.relayed)

    def step(self, n):
        out = []
        for a, r0, rows in self.pieces[:n]:
            if out and out[-1][0] == a and out[-1][1] + out[-1][2] == r0:
                out[-1] = (a, out[-1][1], out[-1][2] + rows)
            else:
                out.append((a, r0, rows))
        self.pieces = self.pieces[n:]
        relay, last = self.landed, self.relayed
        self.landed, self.relayed = (out if self.kind == "gather" else []), relay
        return out, relay, last


D2D_PIECES_A_CALL = 8


class Stream:
    def __init__(self):
        self.queue = []

    def add(self, exchange):
        self.queue.append(exchange)
        return exchange

    def take(self, n, only=None):
        parts = []
        for ex in (self.queue if only is None else [only]):
            had = len(ex.pieces)
            if ex.cheap and only is None:
                now, relay, last = ex.step(D2D_PIECES_A_CALL)
            else:
                now, relay, last = ex.step(n)
                n -= had - len(ex.pieces)
            if now or relay or last:
                parts.append((ex, now, relay, last))
        return _Side(parts) if parts else None

    def finish(self, exchange, name):
        while exchange.busy():
            side = self.take(len(exchange.pieces), only=exchange)
            _pcall(lambda: None, [], side, in_specs=[], out_specs=[], out_shape=[], name=name)
        self.queue.remove(exchange)
        return exchange.bufs


def _row_ops(rows, tr):
    arrs, specs, widths = [], [], []
    for r in rows:
        arr, n, j = r if isinstance(r, tuple) else (r, r.shape[1], 0)
        arrs.append(arr)
        widths.append(n)
        specs.append(pl.BlockSpec((tr, n), lambda i, j=j: (i, j)))
    return arrs, specs, widths


def rowmap(name, f, rows, params, outs, tr, accs=()):
    rows, row_specs, _ = _row_ops(rows, tr)
    T = rows[0].shape[0]
    nin, nout, nacc = len(rows) + len(params), len(outs), len(accs)

    def body(*refs):
        res = _tup(f(*[r[...] for r in refs[:nin]]))
        for o, r in zip(refs[nin:nin + nout], res[:nout]):
            o[...] = r.astype(o.dtype)
        arefs = refs[nin + nout:]
        if nacc:
            @pl.when(pl.program_id(0) == 0)
            def _():
                for a in arefs:
                    a[...] = jnp.zeros(a.shape, a.dtype)
            for a, r in zip(arefs, res[nout:]):
                a[...] += r.astype(F32)

    in_specs = row_specs + [_full_spec(p) for p in params]
    out_specs = [pl.BlockSpec((tr, n), lambda i: (i, 0)) for n, _ in outs] + \
                [pl.BlockSpec(s, lambda i, nd=len(s): (0,) * nd) for s in accs]
    out_shape = [jax.ShapeDtypeStruct((T, n), d) for n, d in outs] + [jax.ShapeDtypeStruct(s, F32) for s in accs]
    res = pl.pallas_call(body, grid=(T // tr,), in_specs=in_specs, out_specs=out_specs, out_shape=out_shape,
                         compiler_params=_cparams(("arbitrary",)), name=name)(*rows, *params)
    return tuple(res)


def rowmap_bwd(name, f, rows, params, cots, tr, drow_dtypes, want_params):
    rows, row_specs, widths = _row_ops(rows, tr)
    T = rows[0].shape[0]
    nr, npar = len(rows), len(params)
    cot_arrays, cot_specs, _ = _row_ops([c for c in cots if c is not None], tr)
    nc = len(cot_arrays)
    ridx = [i for i, d in enumerate(drow_dtypes) if d is not None]
    pidx = [i for i, w in enumerate(want_params) if w]

    def body(*refs):
        rvals = [r[...] for r in refs[:nr]]
        pvals = [r[...] for r in refs[nr:nr + npar]]
        crefs = list(refs[nr + npar:nr + npar + nc])
        orefs = refs[nr + npar + nc:]
        outs, vjp = jax.vjp(f, *rvals, *pvals)
        outs = _tup(outs)
        cts = []
        for c, o in zip(cots, outs):
            cts.append(jnp.ones(o.shape, o.dtype) if c is None else crefs.pop(0)[...].astype(o.dtype))
        grads = vjp(tuple(cts) if len(cts) > 1 else cts[0])
        for o, i in zip(orefs[:len(ridx)], ridx):
            o[...] = grads[i].astype(o.dtype)
        prefs = orefs[len(ridx):]
        if prefs:
            @pl.when(pl.program_id(0) == 0)
            def _():
                for a in prefs:
                    a[...] = jnp.zeros(a.shape, a.dtype)
            for a, i in zip(prefs, pidx):
                a[...] += grads[nr + i].astype(F32)

    in_specs = row_specs + [_full_spec(p) for p in params] + cot_specs
    out_specs = [pl.BlockSpec((tr, widths[i]), lambda i: (i, 0)) for i in ridx] + [_full_spec(params[i]) for i in pidx]
    out_shape = [jax.ShapeDtypeStruct((T, widths[i]), drow_dtypes[i]) for i in ridx] + \
                [jax.ShapeDtypeStruct(params[i].shape, F32) for i in pidx]
    res = pl.pallas_call(body, grid=(T // tr,), in_specs=in_specs, out_specs=out_specs, out_shape=out_shape,
                         compiler_params=_cparams(("arbitrary",)), name=name)(*rows, *params, *cot_arrays)
    res = tuple(res)
    return res[:len(ridx)], res[len(ridx):]


def _pick(n, pref):
    best = None
    for d in range(LANES, min(n, pref) + 1, LANES):
        if n % d == 0:
            best = d
    return best if best is not None else n


def _spec2(arr, tile, pos):
    tr, tc = tile
    if arr.ndim == 2:
        return pl.BlockSpec((tr, tc), lambda i, j, k: pos(i, j, k))
    assert arr.shape[2] % tc == 0, (arr.shape, tile)
    per = arr.shape[2] // tc

    def imap(i, j, k):
        r, c = pos(i, j, k)
        return (c // per, r, c % per)
    return pl.BlockSpec((None, tr, tc), imap)


def _dims2(arr):
    return (arr.shape[0], arr.shape[1]) if arr.ndim == 2 else (arr.shape[1], arr.shape[0] * arr.shape[2])


def mm(name, a, b, mode, outs, epilogue=None, extras=(), pm=1024, pn=512, pk=4224, side=None):
    ar, ac = _dims2(a)
    br, bc = _dims2(b)
    if mode == "nn":
        M, K, N = ar, ac, bc
    elif mode == "nt":
        M, K, N = ar, ac, br
    else:
        M, K, N = ac, ar, bc
    tm, tn, tk = _pick(M, pm), _pick(N, pn), _pick(K, pk)
    for arr in (a, b) + tuple(extras):
        if arr.ndim == 3:
            assert arr.shape[2] % LANES == 0
    if mode == "nn":
        a_spec = _spec2(a, (tm, tk), lambda i, j, k: (i, k))
        b_spec = _spec2(b, (tk, tn), lambda i, j, k: (k, j))
        dims = (((1,), (0,)), ((), ()))
    elif mode == "nt":
        a_spec = _spec2(a, (tm, tk), lambda i, j, k: (i, k))
        b_spec = _spec2(b, (tn, tk), lambda i, j, k: (j, k))
        dims = (((1,), (1,)), ((), ()))
    else:
        a_spec = _spec2(a, (tk, tm), lambda i, j, k: (k, i))
        b_spec = _spec2(b, (tk, tn), lambda i, j, k: (k, j))
        dims = (((0,), (0,)), ((), ()))
    nk = K // tk
    nex = len(extras)

    def body(*refs):
        a_ref, b_ref = refs[0], refs[1]
        ex_refs = refs[2:2 + nex]
        o_refs = refs[2 + nex:2 + nex + len(outs)]
        acc = refs[-1]
        k = pl.program_id(2)
        part = lax.dot_general(a_ref[...].astype(BF16), b_ref[...].astype(BF16), dims, preferred_element_type=F32)

        def finish(r):
            res = (r,) if epilogue is None else _tup(epilogue(r, *[e[...] for e in ex_refs]))
            for o, v in zip(o_refs, res):
                o[...] = v.astype(o.dtype)

        if nk == 1:
            finish(part)
            return

        @pl.when(k == 0)
        def _():
            acc[...] = part

        @pl.when(k > 0)
        def _():
            acc[...] += part

        @pl.when(k == nk - 1)
        def _():
            finish(acc[...])

    out_shape, out_specs = [], []
    args, in_specs, aliases = [a, b, *extras], [a_spec, b_spec], {}
    in_specs += [_spec2(e, (tm, tn), lambda i, j, k: (i, j)) for e in extras]
    for dt, chunks in outs:
        if isinstance(chunks, tuple):
            how, rows, off, buf = chunks
            o = jax.ShapeDtypeStruct((4, rows, D_MODEL), dt)
            assert off % tm == 0 and D_MODEL % tn == 0 and D_MODEL % tm == 0
            if how == "cols":
                per = D_MODEL // tn
                spec = pl.BlockSpec((None, tm, tn), lambda i, j, k: (j // per, off // tm + i, j % per))
            else:
                per = D_MODEL // tm
                spec = pl.BlockSpec((None, tm, tn), lambda i, j, k: (i // per, off // tm + i % per, j))
            if buf is not None:
                aliases[len(args)] = len(out_shape)
                args.append(buf)
                in_specs.append(pl.BlockSpec(memory_space=pl.ANY))
        else:
            o = jax.ShapeDtypeStruct((M, N) if chunks == 1 else (chunks, M, N // chunks), dt)
            spec = _spec2(o, (tm, tn), lambda i, j, k: (i, j))
        out_shape.append(o)
        out_specs.append(spec)
    n_extra_in = len(args) - 2 - nex
    res = _pcall(
        (lambda *refs: body(*refs[:2 + nex], *refs[2 + nex + n_extra_in:])) if n_extra_in else body, args, side,
        grid=(M // tm, N // tn, nk), in_specs=in_specs, out_specs=out_specs, out_shape=out_shape,
        scratch_shapes=[pltpu.VMEM((tm, tn), F32)] if nk > 1 else [], input_output_aliases=aliases,
        compiler_params=_cparams(("parallel", "parallel", "arbitrary")), name=name)
    return tuple(res)


def _rms(x, g):
    return x * lax.rsqrt(jnp.mean(x * x, axis=-1, keepdims=True) + EPS) * g


def f_norm(x, g):
    return _rms(x, g)


def f_norm_keep(x, g):
    return x, _rms(x, g)


def f_resnorm(xp, m, gpost, gnext):
    xn = xp + _rms(m, gpost)
    return xn, _rms(xn, gnext)


def f_final_rows(xp, m, tgt, gpost):
    xn = xp + _rms(m, gpost)
    return 0.5 * jnp.mean(jnp.square(xn - tgt), axis=-1, keepdims=True)


def f_final_loss(xp, m, tgt, gpost):
    return jnp.sum(f_final_rows(xp, m, tgt, gpost), axis=0, keepdims=True)


def f_norm_twice(x, g):
    y = _rms(x, g)
    return y, y


def f_xattn(q, k, v):
    outs = []
    for h in range(X_HEADS):
        sl = slice(h * X_HEAD_DIM, (h + 1) * X_HEAD_DIM)
        s = lax.dot_general(q[:, sl].astype(BF16), k[:, sl].astype(BF16), (((1,), (1,)), ((), ())),
                            preferred_element_type=F32) * (X_HEAD_DIM ** -0.5)
        m = jnp.max(s, axis=-1, keepdims=True)
        e = jnp.exp(s - m)
        p = e / jnp.sum(e, axis=-1, keepdims=True)
        outs.append(jnp.dot(p.astype(BF16), v[:, sl].astype(BF16), preferred_element_type=F32))
    return jnp.concatenate(outs, axis=1)


def f_adamw(w, g, m, v):
    m = ADAM_B1 * m + (1.0 - ADAM_B1) * g
    v = ADAM_B2 * v + (1.0 - ADAM_B2) * jnp.square(g)
    m_hat = m / (1.0 - ADAM_B1 ** ADAM_STEP)
    v_hat = v / (1.0 - ADAM_B2 ** ADAM_STEP)
    delta = -ADAM_LR * (m_hat / (jnp.sqrt(v_hat) + ADAM_EPS) + ADAM_WD * w)
    return delta, m, v


def t5_bucket_map():
    qi = jnp.arange(BLOCK)[:, None]
    kj = jnp.arange(3 * BLOCK)[None, :]
    rel = kj - BLOCK - qi
    nb = N_BUCKETS // 2
    max_exact = nb // 2
    ret = jnp.where(rel > 0, nb, 0)
    n = jnp.abs(rel)
    nf = jnp.maximum(n, 1).astype(jnp.float32)
    large = max_exact + (jnp.log(nf / max_exact) / math.log(MAX_DISTANCE / max_exact) * (nb - max_exact)).astype(jnp.int32)
    large = jnp.minimum(large, nb - 1)
    return (ret + jnp.where(n < max_exact, n, large)).astype(jnp.int32)


def bias_table_fwd(table, bucket):
    def body(t_ref, b_ref, o_ref):
        bk = b_ref[...]
        for h in range(A_HEADS):
            acc = jnp.zeros(bk.shape, F32)
            for b in range(N_BUCKETS):
                acc = jnp.where(bk == b, t_ref[b, h], acc)
            o_ref[h] = acc
    return pl.pallas_call(
        body, in_specs=[pl.BlockSpec(memory_space=pltpu.SMEM), pl.BlockSpec(memory_space=pltpu.VMEM)],
        out_specs=pl.BlockSpec(memory_space=pltpu.VMEM),
        out_shape=jax.ShapeDtypeStruct((A_HEADS, BLOCK, 3 * BLOCK), F32), name="bias_table_fwd")(table, bucket)


def bias_table_bwd(dbias_list, bucket):
    n = len(dbias_list)

    def body(*refs):
        b_ref, o_ref = refs[n], refs[n + 1]
        bk = b_ref[...]
        row = lax.broadcasted_iota(jnp.int32, (N_BUCKETS, LANES), 0)
        col = lax.broadcasted_iota(jnp.int32, (N_BUCKETS, LANES), 1)
        out = jnp.zeros((N_BUCKETS, LANES), F32)
        for h in range(A_HEADS):
            d = refs[0][h]
            for r in refs[1:n]:
                d = d + r[h]
            for b in range(N_BUCKETS):
                s = jnp.sum(jnp.where(bk == b, d, 0.0), keepdims=True)
                out = out + jnp.where((row == b) & (col == h), s, 0.0)
        o_ref[...] = out
    return pl.pallas_call(
        body, out_shape=jax.ShapeDtypeStruct((N_BUCKETS, LANES), F32), name="bias_table_bwd",
        compiler_params=_cparams())(*dbias_list, bucket)


def _attn_mask(n, nblk):
    i = lax.broadcasted_iota(jnp.int32, (BLOCK, 3 * BLOCK), 0)
    j = lax.broadcasted_iota(jnp.int32, (BLOCK, 3 * BLOCK), 1)
    kpos = n * BLOCK + j - BLOCK
    return (jnp.abs(j - BLOCK - i) <= WINDOW) & (kpos >= 0) & (kpos < nblk * BLOCK)


def f_attn_block(q, k3, v3, bias, sink, mask):
    kb, vb = k3.astype(BF16), v3.astype(BF16)
    outs = []
    for g in range(A_GROUP):
        qg = q[:, g * A_HEAD_DIM:(g + 1) * A_HEAD_DIM].astype(BF16)
        s = lax.dot_general(qg, kb, (((1,), (1,)), ((), ())), preferred_element_type=F32) * (A_HEAD_DIM ** -0.5)
        s = jnp.where(mask, s + bias[g], NEG_INF)
        sk = sink[g:g + 1, :]
        m = jnp.maximum(jnp.max(s, axis=-1, keepdims=True), sk)
        e = jnp.exp(s - m)
        den = jnp.sum(e, axis=-1, keepdims=True) + jnp.exp(sk - m)
        p = e / den
        outs.append(jnp.dot(p.astype(BF16), vb, preferred_element_type=F32))
    return jnp.concatenate(outs, axis=1)


def _attn_in_specs(nblk):
    qw = A_GROUP * A_HEAD_DIM
    kc, vc = OFF_AK // A_HEAD_DIM, OFF_AV // A_HEAD_DIM
    return [
        pl.BlockSpec((BLOCK, qw), lambda h, n: (n, h)),
        pl.BlockSpec((BLOCK, A_HEAD_DIM), lambda h, n: (jnp.maximum(n - 1, 0), kc + h)),
        pl.BlockSpec((BLOCK, A_HEAD_DIM), lambda h, n: (n, kc + h)),
        pl.BlockSpec((BLOCK, A_HEAD_DIM), lambda h, n: (jnp.minimum(n + 1, nblk - 1), kc + h)),
        pl.BlockSpec((BLOCK, A_HEAD_DIM), lambda h, n: (jnp.maximum(n - 1, 0), vc + h)),
        pl.BlockSpec((BLOCK, A_HEAD_DIM), lambda h, n: (n, vc + h)),
        pl.BlockSpec((BLOCK, A_HEAD_DIM), lambda h, n: (jnp.minimum(n + 1, nblk - 1), vc + h)),
        pl.BlockSpec((A_GROUP, BLOCK, 3 * BLOCK), lambda h, n: (h, 0, 0)),
        pl.BlockSpec((None, A_GROUP, 1), lambda h, n: (h, 0, 0)),
    ]


def attn_fwd(proj, bias, sink, side=None):
    T = proj.shape[0]
    nblk = T // BLOCK

    def body(q_ref, k0, k1, k2, v0, v1, v2, b_ref, s_ref, o_ref):
        n = pl.program_id(1)
        k3 = jnp.concatenate([k0[...], k1[...], k2[...]], axis=0)
        v3 = jnp.concatenate([v0[...], v1[...], v2[...]], axis=0)
        o = f_attn_block(q_ref[...], k3, v3, b_ref[...], s_ref[...], _attn_mask(n, nblk))
        o_ref[...] = o.astype(o_ref.dtype)

    return _pcall(
        body, [proj] * 7 + [bias, sink], side, grid=(A_KV_HEADS, nblk), in_specs=_attn_in_specs(nblk),
        out_specs=pl.BlockSpec((BLOCK, A_GROUP * A_HEAD_DIM), lambda h, n: (n, h)),
        out_shape=jax.ShapeDtypeStruct((T, A_Q), BF16),
        compiler_params=_cparams(("arbitrary", "arbitrary")), name="attn_fwd")[0]


def attn_bwd(proj, bias, sink, dcat, side=None):
    T = proj.shape[0]
    nblk = T // BLOCK
    qw = A_GROUP * A_HEAD_DIM

    def body(q_ref, k0, k1, k2, v0, v1, v2, b_ref, s_ref, do_ref, dq_ref, dk_ref, dv_ref, db_ref, ds_ref, dk_acc, dv_acc):
        n = pl.program_id(1)

        @pl.when(n == 0)
        def _():
            dk_acc[...] = jnp.zeros(dk_acc.shape, F32)
            dv_acc[...] = jnp.zeros(dv_acc.shape, F32)
            db_ref[...] = jnp.zeros(db_ref.shape, F32)
            ds_ref[...] = jnp.zeros(ds_ref.shape, F32)

        k3 = jnp.concatenate([k0[...], k1[...], k2[...]], axis=0)
        v3 = jnp.concatenate([v0[...], v1[...], v2[...]], axis=0)
        mask = _attn_mask(n, nblk)
        _, vjp = jax.vjp(lambda q, k, v, b, s: f_attn_block(q, k, v, b, s, mask), q_ref[...], k3, v3, b_ref[...], s_ref[...])
        dq, dk3, dv3, db, ds = vjp(do_ref[...])
        dq_ref[...] = dq.astype(dq_ref.dtype)
        db_ref[...] += db
        ds_ref[...] += ds
        mid = pl.multiple_of(n * BLOCK, BLOCK)
        dk_acc[pl.ds(mid, BLOCK), :] += dk3[BLOCK:2 * BLOCK]
        dv_acc[pl.ds(mid, BLOCK), :] += dv3[BLOCK:2 * BLOCK]

        @pl.when(n > 0)
        def _():
            lo = pl.multiple_of((n - 1) * BLOCK, BLOCK)
            dk_acc[pl.ds(lo, BLOCK), :] += dk3[0:BLOCK]
            dv_acc[pl.ds(lo, BLOCK), :] += dv3[0:BLOCK]

        @pl.when(n < nblk - 1)
        def _():
            hi = pl.multiple_of((n + 1) * BLOCK, BLOCK)
            dk_acc[pl.ds(hi, BLOCK), :] += dk3[2 * BLOCK:3 * BLOCK]
            dv_acc[pl.ds(hi, BLOCK), :] += dv3[2 * BLOCK:3 * BLOCK]

        @pl.when(n == nblk - 1)
        def _():
            dk_ref[...] = dk_acc[...].astype(dk_ref.dtype)
            dv_ref[...] = dv_acc[...].astype(dv_ref.dtype)

    in_specs = _attn_in_specs(nblk) + [pl.BlockSpec((BLOCK, qw), lambda h, n: (n, h))]
    out_specs = [
        pl.BlockSpec((BLOCK, qw), lambda h, n: (n, h)),
        pl.BlockSpec((T, A_HEAD_DIM), lambda h, n: (0, h)),
        pl.BlockSpec((T, A_HEAD_DIM), lambda h, n: (0, h)),
        pl.BlockSpec((A_GROUP, BLOCK, 3 * BLOCK), lambda h, n: (h, 0, 0)),
        pl.BlockSpec((None, A_GROUP, 1), lambda h, n: (h, 0, 0)),
    ]
    out_shape = [
        jax.ShapeDtypeStruct((T, A_Q), BF16), jax.ShapeDtypeStruct((T, A_KV), BF16), jax.ShapeDtypeStruct((T, A_KV), BF16),
        jax.ShapeDtypeStruct((A_HEADS, BLOCK, 3 * BLOCK), F32), jax.ShapeDtypeStruct((A_KV_HEADS, A_GROUP, 1), F32),
    ]
    return _pcall(
        body, [proj] * 7 + [bias, sink, dcat], side, grid=(A_KV_HEADS, nblk), in_specs=in_specs, out_specs=out_specs,
        out_shape=out_shape, scratch_shapes=[pltpu.VMEM((T, A_HEAD_DIM), F32), pltpu.VMEM((T, A_HEAD_DIM), F32)],
        compiler_params=_cparams(("arbitrary", "arbitrary")), name="attn_bwd")


def f_gla_gate(z, w2f, b2f, w2b, b2b):
    laf = jax.nn.log_sigmoid(jnp.dot(z, w2f, precision=HI, preferred_element_type=F32) + b2f) / GATE_TAU
    lab = jax.nn.log_sigmoid(jnp.dot(z, w2b, precision=HI, preferred_element_type=F32) + b2b) / GATE_TAU
    return laf, lab


def f_gla_post(o, g, gn):
    outs = []
    for h in range(B_HEADS):
        sl = slice(h * B_VAL_DIM, (h + 1) * B_VAL_DIM)
        oh = o[:, sl]
        outs.append(oh * lax.rsqrt(jnp.mean(oh * oh, axis=-1, keepdims=True) + EPS))
    return jnp.concatenate(outs, axis=1) * gn * jax.nn.silu(g)


def _gla_consts(forward):
    C = GLA_CHUNK
    i = lax.broadcasted_iota(jnp.int32, (C, C), 0)
    j = lax.broadcasted_iota(jnp.int32, (C, C), 1)
    if forward:
        return (j <= i).astype(F32), j <= i
    return (j >= i).astype(F32), j > i


def _gla_chunk(q, k, v, la, st, tri, msk, forward):
    C = q.shape[0]
    b = jnp.dot(tri, la, precision=HI, preferred_element_type=F32)
    bl = b[C - 1:C] if forward else b[0:1]
    qe = (q * (B_KEY_DIM ** -0.5)) * jnp.exp(b)
    ke = k * jnp.exp(-b)
    kl = k * jnp.exp(bl - b)
    att = lax.dot_general(qe.astype(BF16), ke.astype(BF16), (((1,), (1,)), ((), ())), preferred_element_type=F32)
    att = jnp.where(msk, att, 0.0)
    o = jnp.dot(att.astype(BF16), v.astype(BF16), preferred_element_type=F32)
    o = o + lax.dot_general(qe.astype(BF16), st.astype(BF16), (((1,), (1,)), ((), ())), preferred_element_type=F32)
    st_new = st * jnp.exp(bl) + lax.dot_general(v.astype(BF16), kl.astype(BF16), (((0,), (0,)), ((), ())),
                                                preferred_element_type=F32)
    return o, st_new


def _gla_state(k, v, la, st, tri, forward):
    C = k.shape[0]
    b = jnp.dot(tri, la, precision=HI, preferred_element_type=F32)
    bl = b[C - 1:C] if forward else b[0:1]
    kl = k * jnp.exp(bl - b)
    return st * jnp.exp(bl) + lax.dot_general(v.astype(BF16), kl.astype(BF16), (((0,), (0,)), ((), ())),
                                              preferred_element_type=F32)


def _gla_specs(T):
    qc, kc, vc = OFF_BQ // LANES, OFF_BK // LANES, OFF_BV // (2 * B_VAL_DIM)
    return [
        pl.BlockSpec((T, LANES), lambda p: (0, qc + p)),
        pl.BlockSpec((T, LANES), lambda p: (0, kc + p)),
        pl.BlockSpec((T, 2 * B_VAL_DIM), lambda p: (0, vc + p)),
        pl.BlockSpec((T, LANES), lambda p: (0, p)),
        pl.BlockSpec((T, LANES), lambda p: (0, p)),
    ]


def _rows(c):
    return pl.ds(pl.multiple_of(c * GLA_CHUNK, GLA_CHUNK), GLA_CHUNK)


def gla_fwd(proj, laf, lab, side=None):
    T = proj.shape[0]
    nc = T // GLA_CHUNK

    def body(q_ref, k_ref, v_ref, laf_ref, lab_ref, o_ref, ob_scr):
        tri_f, msk_f = _gla_consts(True)
        tri_b, msk_b = _gla_consts(False)
        zero = jnp.zeros((B_VAL_DIM, B_KEY_DIM), F32)

        def step(c, carry):
            rf, rb = _rows(c), _rows(nc - 1 - c)
            new = []
            for hh in range(2):
                ks = slice(hh * B_KEY_DIM, (hh + 1) * B_KEY_DIM)
                vs = slice(hh * B_VAL_DIM, (hh + 1) * B_VAL_DIM)
                o, s = _gla_chunk(q_ref[rf, ks], k_ref[rf, ks], v_ref[rf, vs], laf_ref[rf, ks], carry[2 * hh], tri_f, msk_f, True)
                o_ref[rf, vs] = o
                new.append(s)
                o, s = _gla_chunk(q_ref[rb, ks], k_ref[rb, ks], v_ref[rb, vs], lab_ref[rb, ks], carry[2 * hh + 1], tri_b, msk_b, False)
                ob_scr[rb, vs] = o
                new.append(s)
            return tuple(new)

        lax.fori_loop(0, nc, step, (zero,) * 4)
        o_ref[...] += ob_scr[...]

    return _pcall(
        body, [proj, proj, proj, laf, lab], side, grid=(B_HEADS // 2,), in_specs=_gla_specs(T),
        out_specs=pl.BlockSpec((T, 2 * B_VAL_DIM), lambda p: (0, p)),
        out_shape=jax.ShapeDtypeStruct((T, B_V), F32),
        scratch_shapes=[pltpu.VMEM((T, 2 * B_VAL_DIM), F32)],
        compiler_params=_cparams(("arbitrary",)), name="gla_fwd")[0]


def gla_bwd(proj, laf, lab, do, side=None):
    T = proj.shape[0]
    nc = T // GLA_CHUNK
    SROWS = 2 * B_VAL_DIM

    def body(q_ref, k_ref, v_ref, laf_ref, lab_ref, do_ref, dq_ref, dk_ref, dv_ref, dlaf_ref, dlab_ref,
             sf_scr, sb_scr, dq_acc, dk_acc, dv_acc):
        tri_f, msk_f = _gla_consts(True)
        tri_b, msk_b = _gla_consts(False)
        zero = jnp.zeros((B_VAL_DIM, B_KEY_DIM), F32)
        dq_acc[...] = jnp.zeros(dq_acc.shape, F32)
        dk_acc[...] = jnp.zeros(dk_acc.shape, F32)
        dv_acc[...] = jnp.zeros(dv_acc.shape, F32)

        def srow(c, hh):
            return pl.ds(pl.multiple_of(c * SROWS + hh * B_VAL_DIM, B_VAL_DIM), B_VAL_DIM)

        def states(c, carry):
            cf, cb = c, nc - 1 - c
            rf, rb = _rows(cf), _rows(cb)
            new = []
            for hh in range(2):
                ks = slice(hh * B_KEY_DIM, (hh + 1) * B_KEY_DIM)
                vs = slice(hh * B_VAL_DIM, (hh + 1) * B_VAL_DIM)
                sf_scr[srow(cf, hh), :] = carry[2 * hh]
                new.append(_gla_state(k_ref[rf, ks], v_ref[rf, vs], laf_ref[rf, ks], carry[2 * hh], tri_f, True))
                sb_scr[srow(cb, hh), :] = carry[2 * hh + 1]
                new.append(_gla_state(k_ref[rb, ks], v_ref[rb, vs], lab_ref[rb, ks], carry[2 * hh + 1], tri_b, False))
            return tuple(new)

        lax.fori_loop(0, nc, states, (zero,) * 4)

        def back(c, carry):
            cf, cb = nc - 1 - c, c
            rf, rb = _rows(cf), _rows(cb)
            new = []
            for hh in range(2):
                ks = slice(hh * B_KEY_DIM, (hh + 1) * B_KEY_DIM)
                vs = slice(hh * B_VAL_DIM, (hh + 1) * B_VAL_DIM)
                for fwd, r, c_, la_ref, dla_ref, s_scr, g, tri, msk in (
                        (True, rf, cf, laf_ref, dlaf_ref, sf_scr, carry[2 * hh], tri_f, msk_f),
                        (False, rb, cb, lab_ref, dlab_ref, sb_scr, carry[2 * hh + 1], tri_b, msk_b)):
                    _, vjp = jax.vjp(
                        lambda q, k, v, la, st: _gla_chunk(q, k, v, la, st, tri, msk, fwd),
                        q_ref[r, ks], k_ref[r, ks], v_ref[r, vs], la_ref[r, ks], s_scr[srow(c_, hh), :])
                    dq, dk, dv, dla, dst = vjp((do_ref[r, vs], g))
                    dq_acc[r, ks] += dq
                    dk_acc[r, ks] += dk
                    dv_acc[r, vs] += dv
                    dla_ref[r, ks] = dla
                    new.append(dst)
            return tuple(new)

        lax.fori_loop(0, nc, back, (zero,) * 4)
        dq_ref[...] = dq_acc[...].astype(dq_ref.dtype)
        dk_ref[...] = dk_acc[...].astype(dk_ref.dtype)
        dv_ref[...] = dv_acc[...].astype(dv_ref.dtype)

    in_specs = _gla_specs(T) + [pl.BlockSpec((T, 2 * B_VAL_DIM), lambda p: (0, p))]
    out_specs = [
        pl.BlockSpec((T, LANES), lambda p: (0, p)), pl.BlockSpec((T, LANES), lambda p: (0, p)),
        pl.BlockSpec((T, 2 * B_VAL_DIM), lambda p: (0, p)),
        pl.BlockSpec((T, LANES), lambda p: (0, p)), pl.BlockSpec((T, LANES), lambda p: (0, p)),
    ]
    out_shape = [
        jax.ShapeDtypeStruct((T, B_QK), BF16), jax.ShapeDtypeStruct((T, B_QK), BF16), jax.ShapeDtypeStruct((T, B_V), BF16),
        jax.ShapeDtypeStruct((T, B_QK), F32), jax.ShapeDtypeStruct((T, B_QK), F32),
    ]
    scratch = [
        pltpu.VMEM((nc * SROWS, B_KEY_DIM), F32), pltpu.VMEM((nc * SROWS, B_KEY_DIM), F32),
        pltpu.VMEM((T, LANES), F32), pltpu.VMEM((T, LANES), F32), pltpu.VMEM((T, 2 * B_VAL_DIM), F32),
    ]
    return _pcall(
        body, [proj, proj, proj, laf, lab, do], side, grid=(B_HEADS // 2,), in_specs=in_specs, out_specs=out_specs,
        out_shape=out_shape, scratch_shapes=scratch, compiler_params=_cparams(("arbitrary",)), name="gla_bwd")


def _shift_raw(x, k):
    T = x.shape[0]
    r = lax.broadcasted_iota(jnp.int32, x.shape, 0)
    if k > 0:
        return jnp.where(r >= k, pltpu.roll(x, k, 0), 0.0)
    return jnp.where(r < T + k, pltpu.roll(x, T + k, 0), 0.0)


@functools.partial(jax.custom_vjp, nondiff_argnums=(1,))
def _shift(x, k):
    return _shift_raw(x, k)


_shift.defvjp(lambda x, k: (_shift_raw(x, k), None), lambda k, _, g: (_shift_raw(g, -k),))


def _scan_raw(a, u, reverse):
    T = a.shape[0]
    d = 1
    while d < T:
        k = -d if reverse else d
        u = a * _shift_raw(u, k) + u
        a = a * _shift_raw(a, k)
        d *= 2
    return u


@functools.partial(jax.custom_vjp, nondiff_argnums=(2,))
def _scan(a, u, reverse):
    return _scan_raw(a, u, reverse)


def _scan_f(a, u, reverse):
    h = _scan_raw(a, u, reverse)
    return h, (a, h)


def _scan_b(reverse, res, dh):
    a, h = res
    k = 1 if reverse else -1
    du = _scan_raw(_shift_raw(a, k), dh, not reverse)
    return du * _shift_raw(h, -k), du


_scan.defvjp(_scan_f, _scan_b)


def f_lru(cx, cy, cw, cb, wa, ba, wx, bx, lam, diff):
    shift, scan = (_shift, _scan) if diff else (_shift_raw, _scan_raw)
    xc = cx * cw[CONV_LEFT:CONV_LEFT + 1]
    for j in range(CONV_WIDTH):
        if j != CONV_LEFT:
            xc = xc + shift(cx, CONV_LEFT - j) * cw[j:j + 1]
    xc = xc + cb
    xb = xc.astype(BF16)
    h = None
    for s in range(2):
        r = jax.nn.sigmoid(jnp.dot(xb, wa[s].astype(BF16), preferred_element_type=F32) + ba[s:s + 1])
        i = jax.nn.sigmoid(jnp.dot(xb, wx[s].astype(BF16), preferred_element_type=F32) + bx[s:s + 1])
        log_a = -LRU_C * r * jax.nn.softplus(-lam[s:s + 1])
        a = jnp.exp(log_a)
        one_minus_a2 = -jnp.tanh(log_a) * (a * a + 1.0)
        u = jnp.sqrt(one_minus_a2) * (i * xc)
        hs = scan(a, u, s == 1)
        h = hs if h is None else h + hs
    return h * jax.nn.gelu(cy)


def _lru_specs(T):
    xc, yc = OFF_CX // LANES, OFF_CY // LANES
    return [
        pl.BlockSpec((T, LANES), lambda b: (0, xc + b)),
        pl.BlockSpec((T, LANES), lambda b: (0, yc + b)),
        pl.BlockSpec((CONV_WIDTH, LANES), lambda b: (0, b)),
        pl.BlockSpec((1, LANES), lambda b: (0, b)),
        pl.BlockSpec((2, None, C_BLOCK_DIM, C_BLOCK_DIM), lambda b: (0, b, 0, 0)),
        pl.BlockSpec((2, LANES), lambda b: (0, b)),
        pl.BlockSpec((2, None, C_BLOCK_DIM, C_BLOCK_DIM), lambda b: (0, b, 0, 0)),
        pl.BlockSpec((2, LANES), lambda b: (0, b)),
        pl.BlockSpec((2, LANES), lambda b: (0, b)),
    ]


def lru_fwd(proj, cw, cb, wa, ba, wx, bx, lam, side=None):
    T = proj.shape[0]

    def body(cx, cy, cw_r, cb_r, wa_r, ba_r, wx_r, bx_r, lam_r, o_ref):
        o = f_lru(cx[...], cy[...], cw_r[...], cb_r[...], wa_r[...], ba_r[...], wx_r[...], bx_r[...], lam_r[...], False)
        o_ref[...] = o.astype(o_ref.dtype)

    return _pcall(
        body, [proj, proj, cw, cb, wa, ba, wx, bx, lam], side, grid=(C_BLOCKS,), in_specs=_lru_specs(T),
        out_specs=pl.BlockSpec((T, LANES), lambda b: (0, b)), out_shape=jax.ShapeDtypeStruct((T, C_WIDTH), BF16),
        compiler_params=_cparams(("arbitrary",)), name="lru_fwd")[0]


def lru_bwd(proj, cw, cb, wa, ba, wx, bx, lam, dcat, side=None):
    T = proj.shape[0]
    oc = (A_Q + B_V) // LANES

    def body(cx, cy, cw_r, cb_r, wa_r, ba_r, wx_r, bx_r, lam_r, do_ref, *outs):
        _, vjp = jax.vjp(functools.partial(f_lru, diff=True), cx[...], cy[...], cw_r[...], cb_r[...], wa_r[...],
                         ba_r[...], wx_r[...], bx_r[...], lam_r[...])
        grads = vjp(do_ref[...])
        for o, g in zip(outs, grads):
            o[...] = g.astype(o.dtype)

    specs = _lru_specs(T)
    out_specs = [pl.BlockSpec((T, LANES), lambda b: (0, b)), pl.BlockSpec((T, LANES), lambda b: (0, b))] + specs[2:]
    out_shape = [jax.ShapeDtypeStruct((T, C_WIDTH), BF16), jax.ShapeDtypeStruct((T, C_WIDTH), BF16)] + \
                [jax.ShapeDtypeStruct(p.shape, F32) for p in (cw, cb, wa, ba, wx, bx, lam)]
    return _pcall(
        body, [proj, proj, cw, cb, wa, ba, wx, bx, lam, dcat], side, grid=(C_BLOCKS,),
        in_specs=specs + [pl.BlockSpec((T, LANES), lambda b: (0, oc + b))], out_specs=out_specs, out_shape=out_shape,
        compiler_params=_cparams(("arbitrary",)), name="lru_bwd")


def all_gather8(name, blk):
    def body(x_ref, out_ref, send_sems, recv_sems):
        x, y, c, _, others = _place()
        sibling = (x, y, 1 - c)

        def slab(px, py, pc):
            return out_ref.at[4 * px + 2 * py + pc]

        first = [_remote(x_ref, slab(x, y, c), send_sems, recv_sems, 0, sibling)]
        first += [_remote(x_ref, slab(x, y, c), send_sems, recv_sems, 1 + j, (*ch, c)) for j, ch in enumerate(others)]
        for cp in first:
            cp.start()
        passed = [_remote(slab(*ch, c), slab(*ch, c), send_sems, recv_sems, 4 + j, sibling) for j, ch in enumerate(others)]
        for j, ch in enumerate(others):
            _remote(x_ref, slab(*ch, c), send_sems, recv_sems, 1 + j, (x, y, c)).wait_recv()
            passed[j].start()
        _remote(x_ref, slab(x, y, 1 - c), send_sems, recv_sems, 0, (x, y, c)).wait_recv()
        for j, ch in enumerate(others):
            _remote(x_ref, slab(*ch, 1 - c), send_sems, recv_sems, 4 + j, (x, y, c)).wait_recv()
        for cp in first + passed:
            cp.wait_send()

    out = pl.pallas_call(
        body, out_shape=jax.ShapeDtypeStruct((8,) + blk.shape, blk.dtype), in_specs=[HBM_SPEC], out_specs=HBM_SPEC,
        scratch_shapes=[pltpu.SemaphoreType.DMA((7,)), pltpu.SemaphoreType.DMA((7,))], name=name)(blk)
    me = 4 * lax.axis_index("x") + 2 * lax.axis_index("y") + lax.axis_index("c")
    return lax.dynamic_update_index_in_dim(out, blk, me, 0)


def _exchange_call(name, body, arrays, out_shapes, n_sems):
    n = len(arrays)

    def kernel_body(*refs):
        body(refs[:n], refs[n:2 * n], refs[2 * n], refs[2 * n + 1])

    return pl.pallas_call(
        kernel_body, out_shape=out_shapes, in_specs=[HBM_SPEC] * n, out_specs=[HBM_SPEC] * n,
        scratch_shapes=[pltpu.SemaphoreType.DMA((n * n_sems,)), pltpu.SemaphoreType.DMA((n * n_sems,))], name=name)(*arrays)


def chip_gather(name, shards):
    def body(ins, outs, send_sems, recv_sems):
        x, y, c, chip, others = _place()
        sibling = (x, y, 1 - c)
        first, passed = [], []
        for a, (x_ref, out_ref) in enumerate(zip(ins, outs)):
            first += [_remote(x_ref.at[c], out_ref.at[chip, c], send_sems, recv_sems, 6 * a + j, (*ch, c))
                      for j, ch in enumerate(others)]
        for cp in first:
            cp.start()
        for a, (x_ref, out_ref) in enumerate(zip(ins, outs)):
            for j, ch in enumerate(others):
                here = out_ref.at[_chip_index(ch), c]
                _remote(x_ref.at[c], here, send_sems, recv_sems, 6 * a + j, (x, y, c)).wait_recv()
                cp = _remote(here, here, send_sems, recv_sems, 6 * a + 3 + j, sibling)
                cp.start()
                passed.append(cp)
        for a, (x_ref, out_ref) in enumerate(zip(ins, outs)):
            for j, ch in enumerate(others):
                _remote(x_ref.at[c], out_ref.at[_chip_index(ch), 1 - c], send_sems, recv_sems, 6 * a + 3 + j, (x, y, c)).wait_recv()
        for cp in first + passed:
            cp.wait_send()

    outs = _exchange_call(name, body, shards, [jax.ShapeDtypeStruct((4,) + s.shape, s.dtype) for s in shards], 6)
    chip = 2 * lax.axis_index("x") + lax.axis_index("y")
    return [lax.dynamic_update_index_in_dim(o, s, chip, 0) for o, s in zip(outs, shards)]


def sum_slabs(name, r, out_dtype, tr):
    S, R, W = r.shape

    def body(*refs):
        t = refs[0][...].astype(F32)
        for s in range(1, S):
            t = t + refs[s][...].astype(F32)
        refs[S][...] = t.astype(out_dtype)

    return pl.pallas_call(
        body, grid=(R // tr,), in_specs=[pl.BlockSpec((None, tr, W), lambda i, s=s: (s, i, 0)) for s in range(S)],
        out_specs=pl.BlockSpec((tr, W), lambda i: (i, 0)), out_shape=jax.ShapeDtypeStruct((R, W), out_dtype),
        compiler_params=_cparams(("parallel",)), name=name)(*([r] * S))


def sum_chips(name, arrived, own, chip, tr):
    S, R, W = arrived.shape

    def body(chip_ref, own_ref, *refs):
        me = chip_ref[0]
        t = None
        for s in range(S):
            term = jnp.where(me == s, own_ref[...].astype(F32), refs[s][...].astype(F32))
            t = term if t is None else t + term
        refs[S][...] = t

    grid_spec = pltpu.PrefetchScalarGridSpec(
        num_scalar_prefetch=1, grid=(R // tr,),
        in_specs=[pl.BlockSpec((None, tr, W), lambda i, ch: (ch[0], i, 0))] +
                 [pl.BlockSpec((None, tr, W), lambda i, ch, s=s: (s, i, 0)) for s in range(S)],
        out_specs=pl.BlockSpec((tr, W), lambda i, ch: (i, 0)))
    return pl.pallas_call(body, grid_spec=grid_spec, out_shape=jax.ShapeDtypeStruct((R, W), F32),
                          compiler_params=_cparams(("parallel",)), name=name)(
                              chip.reshape(1).astype(jnp.int32), own, *([arrived] * S))


def add_kept_half(name, halves, got, c, tr):
    S, _, R, W = halves.shape

    def body(c_ref, h_ref, g_ref, o_ref):
        o_ref[...] = (h_ref[...].astype(F32) + g_ref[...].astype(F32)).astype(o_ref.dtype)

    grid_spec = pltpu.PrefetchScalarGridSpec(
        num_scalar_prefetch=1, grid=(S, R // tr),
        in_specs=[pl.BlockSpec((None, None, tr, W), lambda s, i, c_ref: (s, c_ref[0], i, 0)),
                  pl.BlockSpec((None, tr, W), lambda s, i, c_ref: (s, i, 0))],
        out_specs=pl.BlockSpec((None, tr, W), lambda s, i, c_ref: (s, i, 0)))
    return pl.pallas_call(body, grid_spec=grid_spec, out_shape=jax.ShapeDtypeStruct((S, R, W), halves.dtype),
                          compiler_params=_cparams(("parallel", "parallel")), name=name)(
                              c.reshape(1).astype(jnp.int32), halves, got)


def adamw_layer(name, l, g, row_off, w, m, v, prev, tr, tc=None, side=None):
    L, R, C = w.shape
    tc = C if tc is None else tc
    off = row_off // tr

    def body(g_ref, w_ref, m_ref, v_ref, *rest):
        outs = rest[-4:]
        gv = g_ref[...]
        d, mn, vn = f_adamw(w_ref[...], gv, m_ref[...], v_ref[...])
        for o, val in zip(outs, (gv, d, mn, vn)):
            o[...] = val

    slab = pl.BlockSpec((None, tr, tc), lambda i, j: (l, i, j))
    in_specs = [pl.BlockSpec((tr, tc), lambda i, j: (off + i, j)), slab, slab, slab]
    args = [g, w, m, v]
    aliases = {}
    if prev is not None:
        in_specs += [pl.BlockSpec(memory_space=pl.ANY)] * 4
        args += list(prev)
        aliases = {4 + k: k for k in range(4)}
    return _pcall(
        body, args, side, grid=(R // tr, C // tc), in_specs=in_specs, out_specs=[slab] * 4,
        out_shape=[jax.ShapeDtypeStruct((L, R, C), F32)] * 4, input_output_aliases=aliases,
        compiler_params=_cparams(("parallel", "parallel")), name=name)


BIG = ("w_in", "w_out", "xq", "xk", "xv", "xo", "w_up", "w_down")
GROUPS = {"mix": ("w_out", "xq", "xk", "xv", "xo"), "ff": ("w_up", "w_down")}
PACK_ROWS = {"w_out": 512, "xq": 512, "xk": 512, "xv": 512, "xo": 512, "w_up": 2048, "w_down": 2048}
GROUP_ROWS = {g: sum(PACK_ROWS[n] for n in names) for g, names in GROUPS.items()}
SUM_TILE = 256
PACK_OFF = {}
for _names in GROUPS.values():
    _o = 0
    for _n in _names:
        PACK_OFF[_n] = _o
        _o += PACK_ROWS[_n]

_SPLIT_OFF = np.cumsum((0,) + SPLIT_SIZES)
_KORDER = (0, 1, 2, 3, 4, 5, 6, 9, 10, 7, 8)


def w_in_to_kernel_cols(w):
    parts = [w[..., _SPLIT_OFF[i]:_SPLIT_OFF[i + 1]] for i in _KORDER]
    parts.append(jnp.zeros(w.shape[:-1] + (D_INP - D_IN,), w.dtype))
    return jnp.concatenate(parts, axis=-1)


def w_in_from_kernel_cols(w):
    offs = np.cumsum((0,) + tuple(SPLIT_SIZES[i] for i in _KORDER))
    pos = {k: (offs[n], offs[n + 1]) for n, k in enumerate(_KORDER)}
    return jnp.concatenate([w[..., pos[i][0]:pos[i][1]] for i in range(len(SPLIT_SIZES))], axis=-1)


def pack_shards(shards, group, dtype):
    return jnp.concatenate([shards[n].astype(dtype) for n in GROUPS[group]], axis=-2)


def unpack_rows(packed, name):
    return packed[..., PACK_OFF[name]:PACK_OFF[name] + PACK_ROWS[name], :]


WEIGHTS = ("rel_bias", "w_in", "w_out", "attn_sink", "gla_w2_f", "gla_b2_f", "gla_w2_b", "gla_b2_b", "gla_norm", "conv_w",
           "conv_b", "lru_wa", "lru_ba", "lru_wx", "lru_bx", "lru_lambda", "xq", "xk", "xv", "xo", "w_up", "w_down",
           "norm_mix_pre", "norm_mix_post", "norm_mem", "norm_x_pre", "norm_x_post", "norm_ff_pre", "norm_ff_post")
SMALL = tuple(n for n in WEIGHTS if n not in BIG)
SMALL_SHARDED = ("gla_w2_f", "gla_w2_b", "conv_w", "lru_ba", "lru_bx", "lru_lambda")
ROW_TILE = 256
SMALL_TILE = 512
W_IN_TILE = (344, 1024)
RIDE_PIECE_ROWS = (256, 512)


def _small_rows(n):
    return -(-n // (SUBLANES * LANES)) * SUBLANES


def _as_rows(a2):
    L, n = a2.shape
    rows = _small_rows(n)
    if rows * LANES != n:
        a2 = jnp.pad(a2, ((0, 0), (0, rows * LANES - n)))
    return a2.reshape(L * rows, LANES)


def _pack_small(items, layered):
    parts = []
    for it, lay in zip(items, layered):
        if isinstance(it, (list, tuple)):
            parts += [_as_rows(e.astype(F32).reshape(1, -1)) for e in it]
        else:
            parts.append(_as_rows(it.astype(F32).reshape(it.shape[0] if lay else 1, -1)))
    pad = -sum(p.shape[0] for p in parts) % SMALL_TILE
    if pad:
        parts.append(jnp.zeros((pad, LANES), F32))
    return jnp.concatenate(parts, axis=0)


def _unpack_small(buf, shapes, layered):
    lead = buf.shape[:-2]
    out, o = [], 0
    for s, lay in zip(shapes, layered):
        L = s[0] if lay else 1
        n = int(np.prod(s)) // L
        rows = _small_rows(n)
        part = buf[..., o:o + L * rows, :]
        if n != rows * LANES:
            part = part.reshape(lead + (L, rows * LANES))[..., :n]
        out.append(part.reshape(lead + tuple(s)))
        o += L * rows
    return out


def _relu2(r):
    return r, jnp.square(jnp.maximum(r, 0.0))


def _drelu2(r, u):
    return r * (2.0 * jnp.maximum(u, 0.0))


def kernel(x, mem, rel_bias, w_in, w_out, attn_sink, gla_w2_f, gla_b2_f, gla_w2_b, gla_b2_b, gla_norm, conv_w, conv_b, lru_wa, lru_ba, lru_wx, lru_bx, lru_lambda, xq, xk, xv, xo, w_up, w_down, norm_mix_pre, norm_mix_post, norm_mem, norm_x_pre, norm_x_post, norm_ff_pre, norm_ff_post, loss_target, m_rel_bias, m_w_in, m_w_out, m_attn_sink, m_gla_w2_f, m_gla_b2_f, m_gla_w2_b, m_gla_b2_b, m_gla_norm, m_conv_w, m_conv_b, m_lru_wa, m_lru_ba, m_lru_wx, m_lru_bx, m_lru_lambda, m_xq, m_xk, m_xv, m_xo, m_w_up, m_w_down, m_norm_mix_pre, m_norm_mix_post, m_norm_mem, m_norm_x_pre, m_norm_x_post, m_norm_ff_pre, m_norm_ff_post, v_rel_bias, v_w_in, v_w_out, v_attn_sink, v_gla_w2_f, v_gla_b2_f, v_gla_w2_b, v_gla_b2_b, v_gla_norm, v_conv_w, v_conv_b, v_lru_wa, v_lru_ba, v_lru_wx, v_lru_bx, v_lru_lambda, v_xq, v_xk, v_xv, v_xo, v_w_up, v_w_down, v_norm_mix_pre, v_norm_mix_post, v_norm_mem, v_norm_x_pre, v_norm_x_post, v_norm_ff_pre, v_norm_ff_post):
    w_args = (rel_bias, w_in, w_out, attn_sink, gla_w2_f, gla_b2_f, gla_w2_b, gla_b2_b, gla_norm, conv_w, conv_b, lru_wa,
              lru_ba, lru_wx, lru_bx, lru_lambda, xq, xk, xv, xo, w_up, w_down, norm_mix_pre, norm_mix_post, norm_mem,
              norm_x_pre, norm_x_post, norm_ff_pre, norm_ff_post)
    m_args = (m_rel_bias, m_w_in, m_w_out, m_attn_sink, m_gla_w2_f, m_gla_b2_f, m_gla_w2_b, m_gla_b2_b, m_gla_norm, m_conv_w,
              m_conv_b, m_lru_wa, m_lru_ba, m_lru_wx, m_lru_bx, m_lru_lambda, m_xq, m_xk, m_xv, m_xo, m_w_up, m_w_down,
              m_norm_mix_pre, m_norm_mix_post, m_norm_mem, m_norm_x_pre, m_norm_x_post, m_norm_ff_pre, m_norm_ff_post)
    v_args = (v_rel_bias, v_w_in, v_w_out, v_attn_sink, v_gla_w2_f, v_gla_b2_f, v_gla_w2_b, v_gla_b2_b, v_gla_norm, v_conv_w,
              v_conv_b, v_lru_wa, v_lru_ba, v_lru_wx, v_lru_bx, v_lru_lambda, v_xq, v_xk, v_xv, v_xo, v_w_up, v_w_down,
              v_norm_mix_pre, v_norm_mix_post, v_norm_mem, v_norm_x_pre, v_norm_x_post, v_norm_ff_pre, v_norm_ff_post)
    Wt, Mo, Vo = dict(zip(WEIGHTS, w_args)), dict(zip(WEIGHTS, m_args)), dict(zip(WEIGHTS, v_args))
    x, mem, tgt = x[0], mem[0], loss_target[0]
    D = D_MODEL
    depth = w_in.shape[0]
    chip = 2 * lax.axis_index("x") + lax.axis_index("y")
    core = lax.axis_index("c")

    sm_shapes = [Wt[n].shape for n in SMALL_SHARDED]
    yes = [True] * len(SMALL_SHARDED)
    g8 = all_gather8("gather_small_weights", _pack_small([Wt[n] for n in SMALL_SHARDED], yes))
    per_chip = _unpack_small(g8[0::2], sm_shapes, yes)
    whole = {n: jnp.concatenate([p[j] for j in range(4)], axis=-1) for n, p in zip(SMALL_SHARDED, per_chip)}

    def group_shards(l, group):
        shard = pack_shards({n: Wt[n][l] for n in GROUPS[group]}, group, BF16).reshape(2, GROUP_ROWS[group] // 2, D)
        return [shard] + ([w_in[l].astype(BF16).reshape(2, D // 2, D_IN // 4)] if group == "mix" else [])

    def whole_weights(group, gathered):
        g = gathered[0].reshape(4, GROUP_ROWS[group], D)
        if group == "ff":
            return {"w_up": unpack_rows(g, "w_up"), "w_down": unpack_rows(g, "w_down").reshape(D_FF, D)}
        W = {n: unpack_rows(g, n).reshape(D, D) for n in GROUPS["mix"]}
        win = gathered[1].reshape(4, D, D_IN // 4).transpose(1, 0, 2).reshape(D, D_IN)
        W["w_in"] = w_in_to_kernel_cols(win)
        return W

    first = group_shards(0, "mix")
    Wgot = {(0, "mix"): whole_weights("mix", chip_gather("gather_first_weights", first))}
    gathers = Stream()
    riding = {}
    for l in range(depth):
        for group in ("mix", "ff"):
            if (l, group) != (0, "mix"):
                shards = group_shards(l, group)
                riding[l, group] = (shards, gathers.add(RidingExchange("gather", shards, RIDE_PIECE_ROWS)))

    def need_weights(l, group):
        if (l, group) not in Wgot:
            shards, exchange = riding.pop((l, group))
            got = gathers.finish(exchange, "gather_rest")
            Wgot[l, group] = whole_weights(
                group, [lax.dynamic_update_index_in_dim(b, s, chip, 0) for b, s in zip(got, shards)])
        return Wgot[l, group]

    bucket = t5_bucket_map()
    bias = bias_table_fwd(rel_bias, bucket)

    def gain(name, l):
        return Wt[name][l][None]

    def layer_params(l):
        w2fp = jnp.zeros((LANES, B_QK), F32).at[0:GATE_RANK].set(whole["gla_w2_f"][l])
        w2bp = jnp.zeros((LANES, B_QK), F32).at[GATE_RANK:2 * GATE_RANK].set(whole["gla_w2_b"][l])
        gate = [w2fp, gla_b2_f[l][None], w2bp, gla_b2_b[l][None]]
        lru = [whole["conv_w"][l], conv_b[l][None], lru_wa[l], whole["lru_ba"][l], lru_wx[l], whole["lru_bx"][l],
               whole["lru_lambda"][l]]
        return attn_sink[l].reshape(A_KV_HEADS, A_GROUP, 1), gate, gla_norm[l][None], lru

    saved = []
    xcur = x
    (h1,) = rowmap("norm_first", f_norm, [x], [gain("norm_mix_pre", 0)], [(D, BF16)], ROW_TILE)
    loss_acc = None
    Wfull = []
    ride = gathers.take
    for l in range(depth):
        W = dict(need_weights(l, "mix"))
        sink3, gate, gn, lru = layer_params(l)
        (proj,) = mm("mm_in", h1, W["w_in"], "nn", [(F32, 1)], pm=512, pn=1408, side=ride(2))
        oa = attn_fwd(proj, bias, sink3, side=ride(3))
        zrow, grow = (proj, LANES, OFF_Z // LANES), (proj, B_V, OFF_BG // B_V)
        laf, lab = rowmap("gla_gate", f_gla_gate, [zrow], gate, [(B_QK, F32), (B_QK, F32)], ROW_TILE)
        oraw = gla_fwd(proj, laf, lab, side=ride(2))
        (ob,) = rowmap("gla_post", f_gla_post, [oraw, grow], [gn], [(B_V, BF16)], ROW_TILE)
        oc = lru_fwd(proj, *lru, side=ride(1))
        cat = jnp.concatenate([oa, ob, oc], axis=1)
        (mixed,) = mm("mm_out", cat, W["w_out"], "nn", [(F32, 1)], side=ride(1))
        x1, h2 = rowmap("resnorm_mix", f_resnorm, [xcur, mixed], [gain("norm_mix_post", l), gain("norm_x_pre", l)],
                        [(D, F32), (D, BF16)], ROW_TILE)
        (memn,) = rowmap("norm_mem", f_norm, [mem], [gain("norm_mem", l)], [(D, BF16)], ROW_TILE)
        (q,) = mm("mm_xq", h2, W["xq"], "nn", [(BF16, 1)], side=ride(1))
        (k,) = mm("mm_xk", memn, W["xk"], "nn", [(F32, 1)])
        (v,) = mm("mm_xv", memn, W["xv"], "nn", [(F32, 1)])
        (o,) = rowmap("xattn", f_xattn, [q], [k, v], [(D, BF16)], ROW_TILE)
        (xo_out,) = mm("mm_xo", o, W["xo"], "nn", [(F32, 1)], side=ride(1))
        x2, h3 = rowmap("resnorm_x", f_resnorm, [x1, xo_out], [gain("norm_x_post", l), gain("norm_ff_pre", l)],
                        [(D, F32), (D, BF16)], ROW_TILE)
        W.update(need_weights(l, "ff"))
        Wfull.append(W)
        u, act = mm("mm_up", h3, W["w_up"], "nn", [(F32, 1), (BF16, 1)], epilogue=_relu2, side=ride(3))
        (ff,) = mm("mm_down", act, W["w_down"], "nn", [(F32, 1)], side=ride(2))
        saved.append(dict(x0=xcur, h1=h1, proj=proj, laf=laf, lab=lab, oraw=oraw, cat=cat, mixed=mixed, x1=x1, h2=h2,
                          memn=memn, q=q, k=k, v=v, o=o, xo_out=xo_out, x2=x2, h3=h3, u=u, act=act, ff=ff))
        if l < depth - 1:
            xcur, h1 = rowmap("resnorm_ff", f_resnorm, [x2, ff], [gain("norm_ff_post", l), gain("norm_mix_pre", l + 1)],
                              [(D, F32), (D, BF16)], ROW_TILE)
        else:
            (loss_acc,) = rowmap("final_loss", f_final_loss, [x2, ff, tgt], [gain("norm_ff_post", l)], [], ROW_TILE,
                                 accs=[(1, 1)])
    loss = lax.psum(loss_acc[0, 0], ("x", "y", "c"))

    small_g = {n: [None] * depth for n in SMALL if n != "rel_bias"}
    adam = {}
    dbias_all = []
    dx_next = dh1_next = None
    grad_x = None
    scatters = Stream()
    ride = scatters.take
    taking = []
    inflight = []
    sharing = []

    def start_reduce(lyr, group, dW):
        if group == "ff":
            pack = dW["ff"]
        else:
            pack = pack_shards({n: dW[n].reshape(4, D // 4, D) for n in GROUPS["mix"]}, group, BF16)
        halves = [pack.reshape(4, 2, GROUP_ROWS[group] // 2, D)]
        if group == "mix":
            g_in = w_in_from_kernel_cols(dW["w_in"]).reshape(D, 4, D_IN // 4).transpose(1, 0, 2)
            halves.append(g_in.reshape(4, 2, D // 2, D_IN // 4))
        taking.append((lyr, group, halves, scatters.add(RidingExchange("take", halves, RIDE_PIECE_ROWS))))

    def continue_reduce():
        lyr, group, halves, exchange = taking.pop()
        got = scatters.finish(exchange, "reduce_to_half_owner")
        sums = [add_kept_half("reduce_chip_sum", h, g, core, SUM_TILE) for h, g in zip(halves, got)]
        inflight.append((lyr, group, sums, scatters.add(RidingExchange("scatter", sums, RIDE_PIECE_ROWS))))

    def finish_reduce():
        lyr, group, sums, exchange = inflight.pop(0)
        arrived = scatters.finish(exchange, "reduce_rest")
        totals = [sum_chips("reduce_sum_chips", a, s, chip, SUM_TILE) for a, s in zip(arrived, sums)]
        sharing.append((lyr, group, totals, scatters.add(RidingExchange("pair", totals, RIDE_PIECE_ROWS))))

    w_in_t = [jnp.swapaxes(a, 1, 2) for a in (w_in, m_w_in, v_w_in)]

    def run_updates():
        for lyr, group, totals, exchange in sharing:
            both = scatters.finish(exchange, "reduce_share_halves")
            fulls = [lax.dynamic_update_index_in_dim(b, t, core, 0) for b, t in zip(both, totals)]
            full = fulls[0].reshape(GROUP_ROWS[group], D)
            for n in GROUPS[group]:
                adam[n] = adamw_layer("adamw_" + n, lyr, full, PACK_OFF[n], Wt[n], Mo[n], Vo[n], adam.get(n), SUM_TILE)
            if group == "mix":
                g = fulls[1].reshape(D, D_IN // 4).T
                adam["w_in"] = adamw_layer("adamw_w_in", lyr, g, 0, *w_in_t, adam.get("w_in"), *W_IN_TILE)

    for l in reversed(range(depth)):
        W, S = Wfull[l], saved[l]
        sink3, gate, gn, lru = layer_params(l)
        if l == depth - 1:
            (dx2, dff), (dgp,) = rowmap_bwd("final_bwd", f_final_rows, [S["x2"], S["ff"], tgt], [gain("norm_ff_post", l)],
                                            [None], ROW_TILE, [F32, F32, None], [True])
        else:
            (dx2, dff), (dgp, dgn_next) = rowmap_bwd(
                "resnorm_ff_bwd", f_resnorm, [S["x2"], S["ff"]], [gain("norm_ff_post", l), gain("norm_mix_pre", l + 1)],
                [dx_next, dh1_next], ROW_TILE, [F32, F32], [True, True])
            small_g["norm_mix_pre"][l + 1] = dgn_next[0]
        small_g["norm_ff_post"][l] = dgp[0]
        dW = {}
        (du,) = mm("mm_down_bwd", dff, W["w_down"], "nt", [(BF16, 1)], epilogue=_drelu2, extras=[S["u"]], side=ride(2))
        (pack,) = mm("mm_down_wgrad", S["act"], dff, "tn",
                     [(BF16, ("rows", GROUP_ROWS["ff"], PACK_OFF["w_down"], None))], side=ride(2))
        (dW["ff"],) = mm("mm_up_wgrad", S["h3"], du, "tn",
                         [(BF16, ("cols", GROUP_ROWS["ff"], PACK_OFF["w_up"], pack))], side=ride(2))
        start_reduce(l, "ff", dW)
        (dh3,) = mm("mm_up_bwd", du, W["w_up"], "nt", [(F32, 1)], pk=D, side=ride(2))
        continue_reduce()
        if len(inflight) > 1:
            finish_reduce()
        (dx1, dxo_out), (dg1, dg2) = rowmap_bwd(
            "resnorm_x_bwd", f_resnorm, [S["x1"], S["xo_out"]], [gain("norm_x_post", l), gain("norm_ff_pre", l)],
            [dx2, dh3], ROW_TILE, [F32, F32], [True, True])
        small_g["norm_x_post"][l], small_g["norm_ff_pre"][l] = dg1[0], dg2[0]
        (do,) = mm("mm_xo_bwd", dxo_out, W["xo"], "nt", [(F32, 1)])
        (dW["xo"],) = mm("mm_xo_wgrad", S["o"], dxo_out, "tn", [(BF16, 1)])
        (dq,), (dk, dv) = rowmap_bwd("xattn_bwd", f_xattn, [S["q"]], [S["k"], S["v"]], [do], ROW_TILE, [BF16], [True, True])
        (dW["xq"],) = mm("mm_xq_wgrad", S["h2"], dq, "tn", [(BF16, 1)])
        (dh2,) = mm("mm_xq_bwd", dq, W["xq"], "nt", [(F32, 1)])
        (dW["xk"],) = mm("mm_xk_wgrad", S["memn"], dk, "tn", [(BF16, 1)])
        (dW["xv"],) = mm("mm_xv_wgrad", S["memn"], dv, "tn", [(BF16, 1)])
        (dmk,) = mm("mm_xk_bwd", dk, W["xk"], "nt", [(F32, 1)])
        (dmv,) = mm("mm_xv_bwd", dv, W["xv"], "nt", [(F32, 1)])
        _, (dgm,) = rowmap_bwd("norm_mem_bwd", f_norm_twice, [mem], [gain("norm_mem", l)], [dmk, dmv], ROW_TILE, [None], [True])
        small_g["norm_mem"][l] = dgm[0]
        (dx0, dmixed), (dg1, dg2) = rowmap_bwd(
            "resnorm_mix_bwd", f_resnorm, [S["x0"], S["mixed"]], [gain("norm_mix_post", l), gain("norm_x_pre", l)],
            [dx1, dh2], ROW_TILE, [F32, F32], [True, True])
        small_g["norm_mix_post"][l], small_g["norm_x_pre"][l] = dg1[0], dg2[0]
        (dcat,) = mm("mm_out_bwd", dmixed, W["w_out"], "nt", [(F32, 1)])
        (dW["w_out"],) = mm("mm_out_wgrad", S["cat"], dmixed, "tn", [(BF16, 1)])
        proj = S["proj"]
        daq, dak, dav, dbias, dsink = attn_bwd(proj, bias, sink3, dcat, side=ride(2))
        dbias_all.append(dbias)
        small_g["attn_sink"][l] = dsink.reshape(A_HEADS)
        zrow, grow = (proj, LANES, OFF_Z // LANES), (proj, B_V, OFF_BG // B_V)
        (doraw, dbg), (dgn,) = rowmap_bwd("gla_post_bwd", f_gla_post, [S["oraw"], grow], [gn], [(dcat, B_V, A_Q // B_V)],
                                          ROW_TILE, [F32, BF16], [True])
        dbq, dbk, dbv, dlaf, dlab = gla_bwd(proj, S["laf"], S["lab"], doraw, side=ride(3))
        (dz,), (dw2fp, db2f, dw2bp, db2b) = rowmap_bwd("gla_gate_bwd", f_gla_gate, [zrow], gate, [dlaf, dlab], ROW_TILE,
                                                        [BF16], [True] * 4)
        small_g["gla_norm"][l] = dgn[0]
        small_g["gla_w2_f"][l], small_g["gla_b2_f"][l] = dw2fp[0:GATE_RANK], db2f[0]
        small_g["gla_w2_b"][l], small_g["gla_b2_b"][l] = dw2bp[GATE_RANK:2 * GATE_RANK], db2b[0]
        dcx, dcy, dcw, dcb, dwa, dba, dwx, dbx, dlam = lru_bwd(proj, *lru, dcat, side=ride(1))
        small_g["conv_w"][l], small_g["conv_b"][l] = dcw, dcb[0]
        small_g["lru_wa"][l], small_g["lru_ba"][l], small_g["lru_wx"][l] = dwa, dba, dwx
        small_g["lru_bx"][l], small_g["lru_lambda"][l] = dbx, dlam
        dproj = jnp.concatenate([daq, dak, dav, dbq, dbk, dbv, dbg, dcx, dcy, dz], axis=1)
        (dW["w_in"],) = mm("mm_in_wgrad", S["h1"], dproj, "tn", [(BF16, 1)], pm=512, pn=1408, side=ride(1))
        (dh1,) = mm("mm_in_bwd", dproj, W["w_in"], "nt", [(F32, 1)], side=ride(1))
        if l > 0:
            dx_next, dh1_next = dx0, dh1
        else:
            (grad_x,), (dg0,) = rowmap_bwd("norm_first_bwd", f_norm_keep, [x], [gain("norm_mix_pre", 0)], [dx0, dh1],
                                           ROW_TILE, [F32], [True])
            small_g["norm_mix_pre"][0] = dg0[0]

        finish_reduce()
        start_reduce(l, "mix", dW)
        continue_reduce()
    finish_reduce()
    run_updates()

    dtab = bias_table_bwd(dbias_all, bucket)
    small_g["rel_bias"] = dtab[:, :A_HEADS]
    layered = [n != "rel_bias" for n in SMALL]
    sg_shapes = [(depth,) + small_g[n][0].shape if lay else small_g[n].shape for n, lay in zip(SMALL, layered)]
    contributions = all_gather8("gather_small_grads", _pack_small([small_g[n] for n in SMALL], layered))
    sg_sum = sum_slabs("sum_small_grads", contributions, F32, SMALL_TILE)
    sg = dict(zip(SMALL, _unpack_small(sg_sum, sg_shapes, layered)))
    for n in SMALL_SHARDED:
        w = Wt[n].shape[-1]
        sg[n] = lax.dynamic_slice_in_dim(sg[n], chip * w, w, axis=sg[n].ndim - 1)

    grads, delta, new_m, new_v = {}, {}, {}, {}
    adam["w_in"] = [jnp.swapaxes(a, 1, 2) for a in adam["w_in"]]
    for n in BIG:
        grads[n], delta[n], new_m[n], new_v[n] = adam[n]
    shapes = [Wt[n].shape for n in SMALL]
    packs = [_pack_small([src[n] for n in SMALL], layered) for src in (Wt, sg, Mo, Vo)]
    d_, m_, v_ = rowmap("adamw_small", f_adamw, packs, [], [(LANES, F32)] * 3, SMALL_TILE)
    for n, a, b, c_ in zip(SMALL, *[_unpack_small(p, shapes, layered) for p in (d_, m_, v_)]):
        grads[n], delta[n], new_m[n], new_v[n] = sg[n], a, b, c_

    return (loss, grad_x[None], *[grads[n] for n in WEIGHTS], *[delta[n] for n in WEIGHTS],
            *[new_m[n] for n in WEIGHTS], *[new_v[n] for n in WEIGHTS])
```

```python
import functools
import math

import numpy as np
import jax
import jax.numpy as jnp
from jax import lax
from jax.experimental import pallas as pl
from jax.experimental.pallas import tpu as pltpu

F32, BF16 = jnp.float32, jnp.bfloat16
HI = lax.Precision.HIGHEST
MESH = pl.DeviceIdType.MESH

VMEM_LIMIT_BYTES = 56 * 1024 * 1024
LANES = 128
SUBLANES = 8

D_MODEL = 2048
DEPTH = 4
A_HEAD_DIM = 128
A_HEADS = 8
A_KV_HEADS = 2
A_GROUP = 4
WINDOW = 128
BLOCK = 128
N_BUCKETS = 32
MAX_DISTANCE = 128
B_HEADS = 4
B_KEY_DIM = 64
B_VAL_DIM = 128
GATE_RANK = 16
GATE_TAU = 16.0
C_WIDTH = 512
C_BLOCKS = 4
C_BLOCK_DIM = 128
CONV_WIDTH = 4
CONV_LEFT = 2
LRU_C = 8.0
X_HEADS = 4
X_HEAD_DIM = 512
D_FF = 4 * D_MODEL
EPS = 1e-6
NEG_INF = -1e30
A_Q, A_KV, B_QK, B_V = 1024, 256, 256, 512
SPLIT_SIZES = (A_Q, A_KV, A_KV, B_QK, B_QK, B_V, B_V, GATE_RANK, GATE_RANK, C_WIDTH, C_WIDTH)
D_IN = sum(SPLIT_SIZES)
D_INP = 4224
OFF_AQ, OFF_AK, OFF_AV, OFF_BQ, OFF_BK, OFF_BV, OFF_BG, OFF_CX, OFF_CY, OFF_Z = (
    0, 1024, 1280, 1536, 1792, 2048, 2560, 3072, 3584, 4096)
GLA_CHUNK = 128

ADAM_LR, ADAM_B1, ADAM_B2, ADAM_EPS, ADAM_WD, ADAM_STEP = 0.001, 0.9, 0.999, 1e-08, 0.01, 10


def _cparams(sem=None):
    return pltpu.CompilerParams(dimension_semantics=sem, vmem_limit_bytes=VMEM_LIMIT_BYTES)


def _full_spec(a):
    nd = a.ndim
    return pl.BlockSpec(a.shape, lambda *_: (0,) * nd)


def _tup(r):
    return r if isinstance(r, tuple) else (r,)


HBM_SPEC = pl.BlockSpec(memory_space=pltpu.HBM)


def _place():
    x, y, c = lax.axis_index("x"), lax.axis_index("y"), lax.axis_index("c")
    others = [(1 - x, y), (x, 1 - y), (1 - x, 1 - y)]
    return x, y, c, 2 * x + y, others


def _remote(src, dst, send_sems, recv_sems, k, to):
    return pltpu.make_async_remote_copy(src_ref=src, dst_ref=dst, send_sem=send_sems.at[k], recv_sem=recv_sems.at[k],
                                        device_id=to, device_id_type=MESH)


def _chip_index(ch):
    return 2 * ch[0] + ch[1]


def _pcall(body, args, side=None, **kw):
    if side is None:
        res = pl.pallas_call(body, **kw)(*args)
        return list(res) if isinstance(res, (list, tuple)) else [res]
    single = not isinstance(kw["out_shape"], (list, tuple))
    out_shape = [kw.pop("out_shape")] if single else list(kw.pop("out_shape"))
    out_specs = [kw.pop("out_specs")] if single else list(kw.pop("out_specs"))
    in_specs = list(kw.pop("in_specs"))
    scratch = list(kw.pop("scratch_shapes", ()))
    grid = kw.get("grid", ())
    n_in, n_out, n_scr = len(in_specs), len(out_shape), len(scratch)
    srcs, bufs = side.srcs, side.bufs
    ns, nb = len(srcs), len(bufs)

    def wrapped(*refs):
        ins = refs[:n_in]
        src_refs = refs[n_in:n_in + ns]
        o0 = n_in + ns + nb
        outs = refs[o0:o0 + n_out]
        buf_refs = refs[o0 + n_out:o0 + n_out + nb]
        s0 = o0 + n_out + nb
        scr = refs[s0:s0 + n_scr]
        send_sems, recv_sems = refs[s0 + n_scr], refs[s0 + n_scr + 1]
        first = last = None
        for d, n in enumerate(grid):
            f, l_ = pl.program_id(d) == 0, pl.program_id(d) == n - 1
            first = f if first is None else first & f
            last = l_ if last is None else last & l_
        if first is None:
            side.start(src_refs, buf_refs, send_sems, recv_sems)
            body(*ins, *outs, *scr)
            side.finish(src_refs, buf_refs, send_sems, recv_sems)
            return
        pl.when(first)(lambda: side.start(src_refs, buf_refs, send_sems, recv_sems))
        body(*ins, *outs, *scr)
        pl.when(last)(lambda: side.finish(src_refs, buf_refs, send_sems, recv_sems))

    any_spec = pl.BlockSpec(memory_space=pl.ANY)
    aliases = dict(kw.pop("input_output_aliases", {}))
    aliases.update({n_in + ns + i: n_out + i for i in range(nb)})
    cp = kw.pop("compiler_params", None)
    if grid:
        cp = _cparams(("arbitrary",) * len(grid))
    res = pl.pallas_call(
        wrapped, in_specs=in_specs + [any_spec] * (ns + nb), out_specs=out_specs + [any_spec] * nb,
        out_shape=out_shape + [jax.ShapeDtypeStruct(b.shape, b.dtype) for b in bufs],
        scratch_shapes=scratch + [pltpu.SemaphoreType.DMA((side.n_sems,)), pltpu.SemaphoreType.DMA((side.n_sems,))],
        input_output_aliases=aliases, compiler_params=cp, **kw)(*args, *srcs, *bufs)
    side.done(list(res[n_out:]))
    return list(res[:n_out])


class _Side:
    def __init__(self, parts):
        self.parts = parts
        self.srcs = [s for p in parts for s in p[0].srcs]
        self.bufs = [b for p in parts for b in p[0].bufs]
        self.n_sems = max(1, sum(4 * len(now) + 4 * len(relay) + len(last) for _, now, relay, last in parts))

    def done(self, bufs):
        for p in self.parts:
            p[0].bufs, bufs = bufs[:len(p[0].bufs)], bufs[len(p[0].bufs):]

    def _copies(self, src_refs, buf_refs, send_sems, recv_sems):
        x, y, c, chip, others = _place()
        xn, yn, dg = others
        me, sibling = (x, y, c), (x, y, 1 - c)
        mine, landing = [], []
        k = o = 0

        def pair(src, dst, got, to):
            nonlocal k
            mine.append(_remote(src, dst, send_sems, recv_sems, k, to))
            landing.append(_remote(src, got, send_sems, recv_sems, k, me))
            k += 1

        for ex, now, relay, last in self.parts:
            srcs, bufs = src_refs[o:o + len(ex.srcs)], buf_refs[o:o + len(ex.srcs)]
            o += len(ex.srcs)
            for a, r0, n in now:
                rows = pl.ds(r0, n)
                if ex.kind == "gather":
                    for ch in (xn, yn):
                        pair(srcs[a].at[c, rows], bufs[a].at[chip, c, rows], bufs[a].at[_chip_index(ch), c, rows], (*ch, c))
                elif ex.kind == "scatter":
                    for ch in others:
                        pair(srcs[a].at[_chip_index(ch), rows], bufs[a].at[chip, rows], bufs[a].at[_chip_index(ch), rows],
                             (*ch, c))
                elif ex.kind == "take":
                    for s in range(4):
                        pair(srcs[a].at[s, 1 - c, rows], bufs[a].at[s, rows], bufs[a].at[s, rows], sibling)
                else:
                    pair(srcs[a].at[rows], bufs[a].at[c, rows], bufs[a].at[1 - c, rows], sibling)
            for a, r0, n in relay:
                top, bottom, rows = pl.ds(r0, n // 2), pl.ds(r0 + n // 2, n // 2), pl.ds(r0, n)
                from_x, from_y = bufs[a].at[_chip_index(xn), c, top], bufs[a].at[_chip_index(yn), c, bottom]
                pair(from_x, from_x, bufs[a].at[_chip_index(dg), c, top], (*yn, c))
                pair(from_y, from_y, bufs[a].at[_chip_index(dg), c, bottom], (*xn, c))
                for ch in (xn, yn):
                    here = bufs[a].at[_chip_index(ch), c, rows]
                    pair(here, here, bufs[a].at[_chip_index(ch), 1 - c, rows], sibling)
            for a, r0, n in last:
                here = bufs[a].at[_chip_index(dg), c, pl.ds(r0, n)]
                pair(here, here, bufs[a].at[_chip_index(dg), 1 - c, pl.ds(r0, n)], sibling)
        return mine, landing

    def start(self, src_refs, buf_refs, send_sems, recv_sems):
        for cp in self._copies(src_refs, buf_refs, send_sems, recv_sems)[0]:
            cp.start()

    def finish(self, src_refs, buf_refs, send_sems, recv_sems):
        mine, landing = self._copies(src_refs, buf_refs, send_sems, recv_sems)
        for cp in landing:
            cp.wait_recv()
        for cp in mine:
            cp.wait_send()


class RidingExchange:
    KINDS = {"gather": (lambda s: (4,) + s, 1, False), "scatter": (lambda s: s, 1, False),
             "take": (lambda s: (s[0],) + s[2:], 2, True), "pair": (lambda s: (2,) + s, 0, True)}

    def __init__(self, kind, srcs, piece_rows):
        self.kind, self.srcs = kind, list(srcs)
        shape_of, row_axis, self.cheap = self.KINDS[kind]
        self.bufs = [lax.empty(shape_of(tuple(s.shape)), s.dtype) for s in srcs]
        heights = [s.shape[row_axis] for s in srcs]
        per = [[(a, r0, min(pr, h - r0)) for r0 in range(0, h, pr)] for a, (h, pr) in enumerate(zip(heights, piece_rows))]
        self.pieces = list(per[0])
        for extra in per[1:]:
            step = max(1, len(self.pieces) // (len(extra) + 1))
            for i, p in enumerate(extra):
                self.pieces.insert(min(len(self.pieces), (i + 1) * step + i), p)
        self.landed, self.relayed = [], []

    def busy(self):
        return bool(self.pieces or self.landed or self.relayed)

    def step(self, n):
        out = []
        for a, r0, rows in self.pieces[:n]:
            if out and out[-1][0] == a and out[-1][1] + out[-1][2] == r0:
                out[-1] = (a, out[-1][1], out[-1][2] + rows)
            else:
                out.append((a, r0, rows))
        self.pieces = self.pieces[n:]
        relay, last = self.landed, self.relayed
        self.landed, self.relayed = (out if self.kind == "gather" else []), relay
        return out, relay, last


D2D_PIECES_A_CALL = 8


class Stream:
    def __init__(self):
        self.queue = []

    def add(self, exchange):
        self.queue.append(exchange)
        return exchange

    def take(self, n, only=None):
        parts = []
        for ex in (self.queue if only is None else [only]):
            had = len(ex.pieces)
            if ex.cheap and only is None:
                now, relay, last = ex.step(D2D_PIECES_A_CALL)
            else:
                now, relay, last = ex.step(n)
                n -= had - len(ex.pieces)
            if now or relay or last:
                parts.append((ex, now, relay, last))
        return _Side(parts) if parts else None

    def finish(self, exchange, name):
        while exchange.busy():
            side = self.take(len(exchange.pieces), only=exchange)
            _pcall(lambda: None, [], side, in_specs=[], out_specs=[], out_shape=[], name=name)
        self.queue.remove(exchange)
        return exchange.bufs


def _row_ops(rows, tr):
    arrs, specs, widths = [], [], []
    for r in rows:
        arr, n, j = r if isinstance(r, tuple) else (r, r.shape[1], 0)
        arrs.append(arr)
        widths.append(n)
        specs.append(pl.BlockSpec((tr, n), lambda i, j=j: (i, j)))
    return arrs, specs, widths


def rowmap(name, f, rows, params, outs, tr, accs=()):
    rows, row_specs, _ = _row_ops(rows, tr)
    T = rows[0].shape[0]
    nin, nout, nacc = len(rows) + len(params), len(outs), len(accs)

    def body(*refs):
        res = _tup(f(*[r[...] for r in refs[:nin]]))
        for o, r in zip(refs[nin:nin + nout], res[:nout]):
            o[...] = r.astype(o.dtype)
        arefs = refs[nin + nout:]
        if nacc:
            @pl.when(pl.program_id(0) == 0)
            def _():
                for a in arefs:
                    a[...] = jnp.zeros(a.shape, a.dtype)
            for a, r in zip(arefs, res[nout:]):
                a[...] += r.astype(F32)

    in_specs = row_specs + [_full_spec(p) for p in params]
    out_specs = [pl.BlockSpec((tr, n), lambda i: (i, 0)) for n, _ in outs] + \
                [pl.BlockSpec(s, lambda i, nd=len(s): (0,) * nd) for s in accs]
    out_shape = [jax.ShapeDtypeStruct((T, n), d) for n, d in outs] + [jax.ShapeDtypeStruct(s, F32) for s in accs]
    res = pl.pallas_call(body, grid=(T // tr,), in_specs=in_specs, out_specs=out_specs, out_shape=out_shape,
                         compiler_params=_cparams(("arbitrary",)), name=name)(*rows, *params)
    return tuple(res)


def rowmap_bwd(name, f, rows, params, cots, tr, drow_dtypes, want_params):
    rows, row_specs, widths = _row_ops(rows, tr)
    T = rows[0].shape[0]
    nr, npar = len(rows), len(params)
    cot_arrays, cot_specs, _ = _row_ops([c for c in cots if c is not None], tr)
    nc = len(cot_arrays)
    ridx = [i for i, d in enumerate(drow_dtypes) if d is not None]
    pidx = [i for i, w in enumerate(want_params) if w]

    def body(*refs):
        rvals = [r[...] for r in refs[:nr]]
        pvals = [r[...] for r in refs[nr:nr + npar]]
        crefs = list(refs[nr + npar:nr + npar + nc])
        orefs = refs[nr + npar + nc:]
        outs, vjp = jax.vjp(f, *rvals, *pvals)
        outs = _tup(outs)
        cts = []
        for c, o in zip(cots, outs):
            cts.append(jnp.ones(o.shape, o.dtype) if c is None else crefs.pop(0)[...].astype(o.dtype))
        grads = vjp(tuple(cts) if len(cts) > 1 else cts[0])
        for o, i in zip(orefs[:len(ridx)], ridx):
            o[...] = grads[i].astype(o.dtype)
        prefs = orefs[len(ridx):]
        if prefs:
            @pl.when(pl.program_id(0) == 0)
            def _():
                for a in prefs:
                    a[...] = jnp.zeros(a.shape, a.dtype)
            for a, i in zip(prefs, pidx):
                a[...] += grads[nr + i].astype(F32)

    in_specs = row_specs + [_full_spec(p) for p in params] + cot_specs
    out_specs = [pl.BlockSpec((tr, widths[i]), lambda i: (i, 0)) for i in ridx] + [_full_spec(params[i]) for i in pidx]
    out_shape = [jax.ShapeDtypeStruct((T, widths[i]), drow_dtypes[i]) for i in ridx] + \
                [jax.ShapeDtypeStruct(params[i].shape, F32) for i in pidx]
    res = pl.pallas_call(body, grid=(T // tr,), in_specs=in_specs, out_specs=out_specs, out_shape=out_shape,
                         compiler_params=_cparams(("arbitrary",)), name=name)(*rows, *params, *cot_arrays)
    res = tuple(res)
    return res[:len(ridx)], res[len(ridx):]


def _pick(n, pref):
    best = None
    for d in range(LANES, min(n, pref) + 1, LANES):
        if n % d == 0:
            best = d
    return best if best is not None else n


def _spec2(arr, tile, pos):
    tr, tc = tile
    if arr.ndim == 2:
        return pl.BlockSpec((tr, tc), lambda i, j, k: pos(i, j, k))
    assert arr.shape[2] % tc == 0, (arr.shape, tile)
    per = arr.shape[2] // tc

    def imap(i, j, k):
        r, c = pos(i, j, k)
        return (c // per, r, c % per)
    return pl.BlockSpec((None, tr, tc), imap)


def _dims2(arr):
    return (arr.shape[0], arr.shape[1]) if arr.ndim == 2 else (arr.shape[1], arr.shape[0] * arr.shape[2])


def mm(name, a, b, mode, outs, epilogue=None, extras=(), pm=1024, pn=512, pk=4224, side=None):
    ar, ac = _dims2(a)
    br, bc = _dims2(b)
    if mode == "nn":
        M, K, N = ar, ac, bc
    elif mode == "nt":
        M, K, N = ar, ac, br
    else:
        M, K, N = ac, ar, bc
    tm, tn, tk = _pick(M, pm), _pick(N, pn), _pick(K, pk)
    for arr in (a, b) + tuple(extras):
        if arr.ndim == 3:
            assert arr.shape[2] % LANES == 0
    if mode == "nn":
        a_spec = _spec2(a, (tm, tk), lambda i, j, k: (i, k))
        b_spec = _spec2(b, (tk, tn), lambda i, j, k: (k, j))
        dims = (((1,), (0,)), ((), ()))
    elif mode == "nt":
        a_spec = _spec2(a, (tm, tk), lambda i, j, k: (i, k))
        b_spec = _spec2(b, (tn, tk), lambda i, j, k: (j, k))
        dims = (((1,), (1,)), ((), ()))
    else:
        a_spec = _spec2(a, (tk, tm), lambda i, j, k: (k, i))
        b_spec = _spec2(b, (tk, tn), lambda i, j, k: (k, j))
        dims = (((0,), (0,)), ((), ()))
    nk = K // tk
    nex = len(extras)

    def body(*refs):
        a_ref, b_ref = refs[0], refs[1]
        ex_refs = refs[2:2 + nex]
        o_refs = refs[2 + nex:2 + nex + len(outs)]
        acc = refs[-1]
        k = pl.program_id(2)
        part = lax.dot_general(a_ref[...].astype(BF16), b_ref[...].astype(BF16), dims, preferred_element_type=F32)

        def finish(r):
            res = (r,) if epilogue is None else _tup(epilogue(r, *[e[...] for e in ex_refs]))
            for o, v in zip(o_refs, res):
                o[...] = v.astype(o.dtype)

        if nk == 1:
            finish(part)
            return

        @pl.when(k == 0)
        def _():
            acc[...] = part

        @pl.when(k > 0)
        def _():
            acc[...] += part

        @pl.when(k == nk - 1)
        def _():
            finish(acc[...])

    out_shape, out_specs = [], []
    args, in_specs, aliases = [a, b, *extras], [a_spec, b_spec], {}
    in_specs += [_spec2(e, (tm, tn), lambda i, j, k: (i, j)) for e in extras]
    for dt, chunks in outs:
        if isinstance(chunks, tuple):
            how, rows, off, buf = chunks
            o = jax.ShapeDtypeStruct((4, rows, D_MODEL), dt)
            assert off % tm == 0 and D_MODEL % tn == 0 and D_MODEL % tm == 0
            if how == "cols":
                per = D_MODEL // tn
                spec = pl.BlockSpec((None, tm, tn), lambda i, j, k: (j // per, off // tm + i, j % per))
            else:
                assert (M // 4) % tm == 0
                per = M // 4 // tm
                spec = pl.BlockSpec((None, tm, tn), lambda i, j, k: (i // per, off // tm + i % per, j))
            if buf is not None:
                aliases[len(args)] = len(out_shape)
                args.append(buf)
                in_specs.append(pl.BlockSpec(memory_space=pl.ANY))
        else:
            o = jax.ShapeDtypeStruct((M, N) if chunks == 1 else (chunks, M, N // chunks), dt)
            spec = _spec2(o, (tm, tn), lambda i, j, k: (i, j))
        out_shape.append(o)
        out_specs.append(spec)
    n_extra_in = len(args) - 2 - nex
    res = _pcall(
        (lambda *refs: body(*refs[:2 + nex], *refs[2 + nex + n_extra_in:])) if n_extra_in else body, args, side,
        grid=(M // tm, N // tn, nk), in_specs=in_specs, out_specs=out_specs, out_shape=out_shape,
        scratch_shapes=[pltpu.VMEM((tm, tn), F32)] if nk > 1 else [], input_output_aliases=aliases,
        compiler_params=_cparams(("parallel", "parallel", "arbitrary")), name=name)
    return tuple(res)


def _rms(x, g):
    return x * lax.rsqrt(jnp.mean(x * x, axis=-1, keepdims=True) + EPS) * g


def f_norm(x, g):
    return _rms(x, g)


def f_norm_keep(x, g):
    return x, _rms(x, g)


def f_resnorm(xp, m, gpost, gnext):
    xn = xp + _rms(m, gpost)
    return xn, _rms(xn, gnext)


def f_final_rows(xp, m, tgt, gpost):
    xn = xp + _rms(m, gpost)
    return 0.5 * jnp.mean(jnp.square(xn - tgt), axis=-1, keepdims=True)


def f_final_loss(xp, m, tgt, gpost):
    return jnp.sum(f_final_rows(xp, m, tgt, gpost), axis=0, keepdims=True)


def f_norm_twice(x, g):
    y = _rms(x, g)
    return y, y


def f_xattn(q, k, v):
    outs = []
    for h in range(X_HEADS):
        sl = slice(h * X_HEAD_DIM, (h + 1) * X_HEAD_DIM)
        s = lax.dot_general(q[:, sl].astype(BF16), k[:, sl].astype(BF16), (((1,), (1,)), ((), ())),
                            preferred_element_type=F32) * (X_HEAD_DIM ** -0.5)
        m = jnp.max(s, axis=-1, keepdims=True)
        e = jnp.exp(s - m)
        p = e / jnp.sum(e, axis=-1, keepdims=True)
        outs.append(jnp.dot(p.astype(BF16), v[:, sl].astype(BF16), preferred_element_type=F32))
    return jnp.concatenate(outs, axis=1)


def f_adamw(w, g, m, v):
    m = ADAM_B1 * m + (1.0 - ADAM_B1) * g
    v = ADAM_B2 * v + (1.0 - ADAM_B2) * jnp.square(g)
    m_hat = m / (1.0 - ADAM_B1 ** ADAM_STEP)
    v_hat = v / (1.0 - ADAM_B2 ** ADAM_STEP)
    delta = -ADAM_LR * (m_hat / (jnp.sqrt(v_hat) + ADAM_EPS) + ADAM_WD * w)
    return delta, m, v


def t5_bucket_map():
    qi = jnp.arange(BLOCK)[:, None]
    kj = jnp.arange(3 * BLOCK)[None, :]
    rel = kj - BLOCK - qi
    nb = N_BUCKETS // 2
    max_exact = nb // 2
    ret = jnp.where(rel > 0, nb, 0)
    n = jnp.abs(rel)
    nf = jnp.maximum(n, 1).astype(jnp.float32)
    large = max_exact + (jnp.log(nf / max_exact) / math.log(MAX_DISTANCE / max_exact) * (nb - max_exact)).astype(jnp.int32)
    large = jnp.minimum(large, nb - 1)
    return (ret + jnp.where(n < max_exact, n, large)).astype(jnp.int32)


def bias_table_fwd(table, bucket):
    def body(t_ref, b_ref, o_ref):
        bk = b_ref[...]
        for h in range(A_HEADS):
            acc = jnp.zeros(bk.shape, F32)
            for b in range(N_BUCKETS):
                acc = jnp.where(bk == b, t_ref[b, h], acc)
            o_ref[h] = acc
    return pl.pallas_call(
        body, in_specs=[pl.BlockSpec(memory_space=pltpu.SMEM), pl.BlockSpec(memory_space=pltpu.VMEM)],
        out_specs=pl.BlockSpec(memory_space=pltpu.VMEM),
        out_shape=jax.ShapeDtypeStruct((A_HEADS, BLOCK, 3 * BLOCK), F32), name="bias_table_fwd")(table, bucket)


def bias_table_bwd(dbias_list, bucket):
    n = len(dbias_list)

    def body(*refs):
        b_ref, o_ref = refs[n], refs[n + 1]
        bk = b_ref[...]
        row = lax.broadcasted_iota(jnp.int32, (N_BUCKETS, LANES), 0)
        col = lax.broadcasted_iota(jnp.int32, (N_BUCKETS, LANES), 1)
        out = jnp.zeros((N_BUCKETS, LANES), F32)
        for h in range(A_HEADS):
            d = refs[0][h]
            for r in refs[1:n]:
                d = d + r[h]
            for b in range(N_BUCKETS):
                s = jnp.sum(jnp.where(bk == b, d, 0.0), keepdims=True)
                out = out + jnp.where((row == b) & (col == h), s, 0.0)
        o_ref[...] = out
    return pl.pallas_call(
        body, out_shape=jax.ShapeDtypeStruct((N_BUCKETS, LANES), F32), name="bias_table_bwd",
        compiler_params=_cparams())(*dbias_list, bucket)


def _attn_mask(n, nblk):
    i = lax.broadcasted_iota(jnp.int32, (BLOCK, 3 * BLOCK), 0)
    j = lax.broadcasted_iota(jnp.int32, (BLOCK, 3 * BLOCK), 1)
    kpos = n * BLOCK + j - BLOCK
    return (jnp.abs(j - BLOCK - i) <= WINDOW) & (kpos >= 0) & (kpos < nblk * BLOCK)


def f_attn_block(q, k3, v3, bias, sink, mask):
    kb, vb = k3.astype(BF16), v3.astype(BF16)
    outs = []
    for g in range(A_GROUP):
        qg = q[:, g * A_HEAD_DIM:(g + 1) * A_HEAD_DIM].astype(BF16)
        s = lax.dot_general(qg, kb, (((1,), (1,)), ((), ())), preferred_element_type=F32) * (A_HEAD_DIM ** -0.5)
        s = jnp.where(mask, s + bias[g], NEG_INF)
        sk = sink[g:g + 1, :]
        m = jnp.maximum(jnp.max(s, axis=-1, keepdims=True), sk)
        e = jnp.exp(s - m)
        den = jnp.sum(e, axis=-1, keepdims=True) + jnp.exp(sk - m)
        p = e / den
        outs.append(jnp.dot(p.astype(BF16), vb, preferred_element_type=F32))
    return jnp.concatenate(outs, axis=1)


def _attn_in_specs(nblk):
    qw = A_GROUP * A_HEAD_DIM
    kc, vc = OFF_AK // A_HEAD_DIM, OFF_AV // A_HEAD_DIM
    return [
        pl.BlockSpec((BLOCK, qw), lambda h, n: (n, h)),
        pl.BlockSpec((BLOCK, A_HEAD_DIM), lambda h, n: (jnp.maximum(n - 1, 0), kc + h)),
        pl.BlockSpec((BLOCK, A_HEAD_DIM), lambda h, n: (n, kc + h)),
        pl.BlockSpec((BLOCK, A_HEAD_DIM), lambda h, n: (jnp.minimum(n + 1, nblk - 1), kc + h)),
        pl.BlockSpec((BLOCK, A_HEAD_DIM), lambda h, n: (jnp.maximum(n - 1, 0), vc + h)),
        pl.BlockSpec((BLOCK, A_HEAD_DIM), lambda h, n: (n, vc + h)),
        pl.BlockSpec((BLOCK, A_HEAD_DIM), lambda h, n: (jnp.minimum(n + 1, nblk - 1), vc + h)),
        pl.BlockSpec((A_GROUP, BLOCK, 3 * BLOCK), lambda h, n: (h, 0, 0)),
        pl.BlockSpec((None, A_GROUP, 1), lambda h, n: (h, 0, 0)),
    ]


def attn_fwd(proj, bias, sink, side=None):
    T = proj.shape[0]
    nblk = T // BLOCK

    def body(q_ref, k0, k1, k2, v0, v1, v2, b_ref, s_ref, o_ref):
        n = pl.program_id(1)
        k3 = jnp.concatenate([k0[...], k1[...], k2[...]], axis=0)
        v3 = jnp.concatenate([v0[...], v1[...], v2[...]], axis=0)
        o = f_attn_block(q_ref[...], k3, v3, b_ref[...], s_ref[...], _attn_mask(n, nblk))
        o_ref[...] = o.astype(o_ref.dtype)

    return _pcall(
        body, [proj] * 7 + [bias, sink], side, grid=(A_KV_HEADS, nblk), in_specs=_attn_in_specs(nblk),
        out_specs=pl.BlockSpec((BLOCK, A_GROUP * A_HEAD_DIM), lambda h, n: (n, h)),
        out_shape=jax.ShapeDtypeStruct((T, A_Q), BF16),
        compiler_params=_cparams(("arbitrary", "arbitrary")), name="attn_fwd")[0]


def attn_bwd(proj, bias, sink, dcat, side=None):
    T = proj.shape[0]
    nblk = T // BLOCK
    qw = A_GROUP * A_HEAD_DIM

    def body(q_ref, k0, k1, k2, v0, v1, v2, b_ref, s_ref, do_ref, dq_ref, dk_ref, dv_ref, db_ref, ds_ref, dk_acc, dv_acc):
        n = pl.program_id(1)

        @pl.when(n == 0)
        def _():
            dk_acc[...] = jnp.zeros(dk_acc.shape, F32)
            dv_acc[...] = jnp.zeros(dv_acc.shape, F32)
            db_ref[...] = jnp.zeros(db_ref.shape, F32)
            ds_ref[...] = jnp.zeros(ds_ref.shape, F32)

        k3 = jnp.concatenate([k0[...], k1[...], k2[...]], axis=0)
        v3 = jnp.concatenate([v0[...], v1[...], v2[...]], axis=0)
        mask = _attn_mask(n, nblk)
        _, vjp = jax.vjp(lambda q, k, v, b, s: f_attn_block(q, k, v, b, s, mask), q_ref[...], k3, v3, b_ref[...], s_ref[...])
        dq, dk3, dv3, db, ds = vjp(do_ref[...])
        dq_ref[...] = dq.astype(dq_ref.dtype)
        db_ref[...] += db
        ds_ref[...] += ds
        mid = pl.multiple_of(n * BLOCK, BLOCK)
        dk_acc[pl.ds(mid, BLOCK), :] += dk3[BLOCK:2 * BLOCK]
        dv_acc[pl.ds(mid, BLOCK), :] += dv3[BLOCK:2 * BLOCK]

        @pl.when(n > 0)
        def _():
            lo = pl.multiple_of((n - 1) * BLOCK, BLOCK)
            dk_acc[pl.ds(lo, BLOCK), :] += dk3[0:BLOCK]
            dv_acc[pl.ds(lo, BLOCK), :] += dv3[0:BLOCK]

        @pl.when(n < nblk - 1)
        def _():
            hi = pl.multiple_of((n + 1) * BLOCK, BLOCK)
            dk_acc[pl.ds(hi, BLOCK), :] += dk3[2 * BLOCK:3 * BLOCK]
            dv_acc[pl.ds(hi, BLOCK), :] += dv3[2 * BLOCK:3 * BLOCK]

        @pl.when(n == nblk - 1)
        def _():
            dk_ref[...] = dk_acc[...].astype(dk_ref.dtype)
            dv_ref[...] = dv_acc[...].astype(dv_ref.dtype)

    in_specs = _attn_in_specs(nblk) + [pl.BlockSpec((BLOCK, qw), lambda h, n: (n, h))]
    out_specs = [
        pl.BlockSpec((BLOCK, qw), lambda h, n: (n, h)),
        pl.BlockSpec((T, A_HEAD_DIM), lambda h, n: (0, h)),
        pl.BlockSpec((T, A_HEAD_DIM), lambda h, n: (0, h)),
        pl.BlockSpec((A_GROUP, BLOCK, 3 * BLOCK), lambda h, n: (h, 0, 0)),
        pl.BlockSpec((None, A_GROUP, 1), lambda h, n: (h, 0, 0)),
    ]
    out_shape = [
        jax.ShapeDtypeStruct((T, A_Q), BF16), jax.ShapeDtypeStruct((T, A_KV), BF16), jax.ShapeDtypeStruct((T, A_KV), BF16),
        jax.ShapeDtypeStruct((A_HEADS, BLOCK, 3 * BLOCK), F32), jax.ShapeDtypeStruct((A_KV_HEADS, A_GROUP, 1), F32),
    ]
    return _pcall(
        body, [proj] * 7 + [bias, sink, dcat], side, grid=(A_KV_HEADS, nblk), in_specs=in_specs, out_specs=out_specs,
        out_shape=out_shape, scratch_shapes=[pltpu.VMEM((T, A_HEAD_DIM), F32), pltpu.VMEM((T, A_HEAD_DIM), F32)],
        compiler_params=_cparams(("arbitrary", "arbitrary")), name="attn_bwd")


def f_gla_gate(z, w2f, b2f, w2b, b2b):
    laf = jax.nn.log_sigmoid(jnp.dot(z, w2f, precision=HI, preferred_element_type=F32) + b2f) / GATE_TAU
    lab = jax.nn.log_sigmoid(jnp.dot(z, w2b, precision=HI, preferred_element_type=F32) + b2b) / GATE_TAU
    return laf, lab


def f_gla_post(o, g, gn):
    outs = []
    for h in range(B_HEADS):
        sl = slice(h * B_VAL_DIM, (h + 1) * B_VAL_DIM)
        oh = o[:, sl]
        outs.append(oh * lax.rsqrt(jnp.mean(oh * oh, axis=-1, keepdims=True) + EPS))
    return jnp.concatenate(outs, axis=1) * gn * jax.nn.silu(g)


def _gla_consts(forward):
    C = GLA_CHUNK
    i = lax.broadcasted_iota(jnp.int32, (C, C), 0)
    j = lax.broadcasted_iota(jnp.int32, (C, C), 1)
    if forward:
        return (j <= i).astype(F32), j <= i
    return (j >= i).astype(F32), j > i


def _gla_chunk(q, k, v, la, st, tri, msk, forward):
    C = q.shape[0]
    b = jnp.dot(tri, la, precision=HI, preferred_element_type=F32)
    bl = b[C - 1:C] if forward else b[0:1]
    qe = (q * (B_KEY_DIM ** -0.5)) * jnp.exp(b)
    ke = k * jnp.exp(-b)
    kl = k * jnp.exp(bl - b)
    att = lax.dot_general(qe.astype(BF16), ke.astype(BF16), (((1,), (1,)), ((), ())), preferred_element_type=F32)
    att = jnp.where(msk, att, 0.0)
    o = jnp.dot(att.astype(BF16), v.astype(BF16), preferred_element_type=F32)
    o = o + lax.dot_general(qe.astype(BF16), st.astype(BF16), (((1,), (1,)), ((), ())), preferred_element_type=F32)
    st_new = st * jnp.exp(bl) + lax.dot_general(v.astype(BF16), kl.astype(BF16), (((0,), (0,)), ((), ())),
                                                preferred_element_type=F32)
    return o, st_new


def _gla_state(k, v, la, st, tri, forward):
    C = k.shape[0]
    b = jnp.dot(tri, la, precision=HI, preferred_element_type=F32)
    bl = b[C - 1:C] if forward else b[0:1]
    kl = k * jnp.exp(bl - b)
    return st * jnp.exp(bl) + lax.dot_general(v.astype(BF16), kl.astype(BF16), (((0,), (0,)), ((), ())),
                                              preferred_element_type=F32)


def _gla_specs(T):
    qc, kc, vc = OFF_BQ // LANES, OFF_BK // LANES, OFF_BV // (2 * B_VAL_DIM)
    return [
        pl.BlockSpec((T, LANES), lambda p: (0, qc + p)),
        pl.BlockSpec((T, LANES), lambda p: (0, kc + p)),
        pl.BlockSpec((T, 2 * B_VAL_DIM), lambda p: (0, vc + p)),
        pl.BlockSpec((T, LANES), lambda p: (0, p)),
        pl.BlockSpec((T, LANES), lambda p: (0, p)),
    ]


def _rows(c):
    return pl.ds(pl.multiple_of(c * GLA_CHUNK, GLA_CHUNK), GLA_CHUNK)


def gla_fwd(proj, laf, lab, side=None):
    T = proj.shape[0]
    nc = T // GLA_CHUNK

    def body(q_ref, k_ref, v_ref, laf_ref, lab_ref, o_ref, ob_scr):
        tri_f, msk_f = _gla_consts(True)
        tri_b, msk_b = _gla_consts(False)
        zero = jnp.zeros((B_VAL_DIM, B_KEY_DIM), F32)

        def step(c, carry):
            rf, rb = _rows(c), _rows(nc - 1 - c)
            new = []
            for hh in range(2):
                ks = slice(hh * B_KEY_DIM, (hh + 1) * B_KEY_DIM)
                vs = slice(hh * B_VAL_DIM, (hh + 1) * B_VAL_DIM)
                o, s = _gla_chunk(q_ref[rf, ks], k_ref[rf, ks], v_ref[rf, vs], laf_ref[rf, ks], carry[2 * hh], tri_f, msk_f, True)
                o_ref[rf, vs] = o
                new.append(s)
                o, s = _gla_chunk(q_ref[rb, ks], k_ref[rb, ks], v_ref[rb, vs], lab_ref[rb, ks], carry[2 * hh + 1], tri_b, msk_b, False)
                ob_scr[rb, vs] = o
                new.append(s)
            return tuple(new)

        lax.fori_loop(0, nc, step, (zero,) * 4)
        o_ref[...] += ob_scr[...]

    return _pcall(
        body, [proj, proj, proj, laf, lab], side, grid=(B_HEADS // 2,), in_specs=_gla_specs(T),
        out_specs=pl.BlockSpec((T, 2 * B_VAL_DIM), lambda p: (0, p)),
        out_shape=jax.ShapeDtypeStruct((T, B_V), F32),
        scratch_shapes=[pltpu.VMEM((T, 2 * B_VAL_DIM), F32)],
        compiler_params=_cparams(("arbitrary",)), name="gla_fwd")[0]


def gla_bwd(proj, laf, lab, do, side=None):
    T = proj.shape[0]
    nc = T // GLA_CHUNK
    SROWS = 2 * B_VAL_DIM

    def body(q_ref, k_ref, v_ref, laf_ref, lab_ref, do_ref, dq_ref, dk_ref, dv_ref, dlaf_ref, dlab_ref,
             sf_scr, sb_scr, dq_acc, dk_acc, dv_acc):
        tri_f, msk_f = _gla_consts(True)
        tri_b, msk_b = _gla_consts(False)
        zero = jnp.zeros((B_VAL_DIM, B_KEY_DIM), F32)
        dq_acc[...] = jnp.zeros(dq_acc.shape, F32)
        dk_acc[...] = jnp.zeros(dk_acc.shape, F32)
        dv_acc[...] = jnp.zeros(dv_acc.shape, F32)

        def srow(c, hh):
            return pl.ds(pl.multiple_of(c * SROWS + hh * B_VAL_DIM, B_VAL_DIM), B_VAL_DIM)

        def states(c, carry):
            cf, cb = c, nc - 1 - c
            rf, rb = _rows(cf), _rows(cb)
            new = []
            for hh in range(2):
                ks = slice(hh * B_KEY_DIM, (hh + 1) * B_KEY_DIM)
                vs = slice(hh * B_VAL_DIM, (hh + 1) * B_VAL_DIM)
                sf_scr[srow(cf, hh), :] = carry[2 * hh]
                new.append(_gla_state(k_ref[rf, ks], v_ref[rf, vs], laf_ref[rf, ks], carry[2 * hh], tri_f, True))
                sb_scr[srow(cb, hh), :] = carry[2 * hh + 1]
                new.append(_gla_state(k_ref[rb, ks], v_ref[rb, vs], lab_ref[rb, ks], carry[2 * hh + 1], tri_b, False))
            return tuple(new)

        lax.fori_loop(0, nc, states, (zero,) * 4)

        def back(c, carry):
            cf, cb = nc - 1 - c, c
            rf, rb = _rows(cf), _rows(cb)
            new = []
            for hh in range(2):
                ks = slice(hh * B_KEY_DIM, (hh + 1) * B_KEY_DIM)
                vs = slice(hh * B_VAL_DIM, (hh + 1) * B_VAL_DIM)
                for fwd, r, c_, la_ref, dla_ref, s_scr, g, tri, msk in (
                        (True, rf, cf, laf_ref, dlaf_ref, sf_scr, carry[2 * hh], tri_f, msk_f),
                        (False, rb, cb, lab_ref, dlab_ref, sb_scr, carry[2 * hh + 1], tri_b, msk_b)):
                    _, vjp = jax.vjp(
                        lambda q, k, v, la, st: _gla_chunk(q, k, v, la, st, tri, msk, fwd),
                        q_ref[r, ks], k_ref[r, ks], v_ref[r, vs], la_ref[r, ks], s_scr[srow(c_, hh), :])
                    dq, dk, dv, dla, dst = vjp((do_ref[r, vs], g))
                    dq_acc[r, ks] += dq
                    dk_acc[r, ks] += dk
                    dv_acc[r, vs] += dv
                    dla_ref[r, ks] = dla
                    new.append(dst)
            return tuple(new)

        lax.fori_loop(0, nc, back, (zero,) * 4)
        dq_ref[...] = dq_acc[...].astype(dq_ref.dtype)
        dk_ref[...] = dk_acc[...].astype(dk_ref.dtype)
        dv_ref[...] = dv_acc[...].astype(dv_ref.dtype)

    in_specs = _gla_specs(T) + [pl.BlockSpec((T, 2 * B_VAL_DIM), lambda p: (0, p))]
    out_specs = [
        pl.BlockSpec((T, LANES), lambda p: (0, p)), pl.BlockSpec((T, LANES), lambda p: (0, p)),
        pl.BlockSpec((T, 2 * B_VAL_DIM), lambda p: (0, p)),
        pl.BlockSpec((T, LANES), lambda p: (0, p)), pl.BlockSpec((T, LANES), lambda p: (0, p)),
    ]
    out_shape = [
        jax.ShapeDtypeStruct((T, B_QK), BF16), jax.ShapeDtypeStruct((T, B_QK), BF16), jax.ShapeDtypeStruct((T, B_V), BF16),
        jax.ShapeDtypeStruct((T, B_QK), F32), jax.ShapeDtypeStruct((T, B_QK), F32),
    ]
    scratch = [
        pltpu.VMEM((nc * SROWS, B_KEY_DIM), F32), pltpu.VMEM((nc * SROWS, B_KEY_DIM), F32),
        pltpu.VMEM((T, LANES), F32), pltpu.VMEM((T, LANES), F32), pltpu.VMEM((T, 2 * B_VAL_DIM), F32),
    ]
    return _pcall(
        body, [proj, proj, proj, laf, lab, do], side, grid=(B_HEADS // 2,), in_specs=in_specs, out_specs=out_specs,
        out_shape=out_shape, scratch_shapes=scratch, compiler_params=_cparams(("arbitrary",)), name="gla_bwd")


def _shift_raw(x, k):
    T = x.shape[0]
    r = lax.broadcasted_iota(jnp.int32, x.shape, 0)
    if k > 0:
        return jnp.where(r >= k, pltpu.roll(x, k, 0), 0.0)
    return jnp.where(r < T + k, pltpu.roll(x, T + k, 0), 0.0)


@functools.partial(jax.custom_vjp, nondiff_argnums=(1,))
def _shift(x, k):
    return _shift_raw(x, k)


_shift.defvjp(lambda x, k: (_shift_raw(x, k), None), lambda k, _, g: (_shift_raw(g, -k),))


def _scan_raw(a, u, reverse):
    T = a.shape[0]
    d = 1
    while d < T:
        k = -d if reverse else d
        u = a * _shift_raw(u, k) + u
        a = a * _shift_raw(a, k)
        d *= 2
    return u


@functools.partial(jax.custom_vjp, nondiff_argnums=(2,))
def _scan(a, u, reverse):
    return _scan_raw(a, u, reverse)


def _scan_f(a, u, reverse):
    h = _scan_raw(a, u, reverse)
    return h, (a, h)


def _scan_b(reverse, res, dh):
    a, h = res
    k = 1 if reverse else -1
    du = _scan_raw(_shift_raw(a, k), dh, not reverse)
    return du * _shift_raw(h, -k), du


_scan.defvjp(_scan_f, _scan_b)


def f_lru(cx, cy, cw, cb, wa, ba, wx, bx, lam, diff):
    shift, scan = (_shift, _scan) if diff else (_shift_raw, _scan_raw)
    xc = cx * cw[CONV_LEFT:CONV_LEFT + 1]
    for j in range(CONV_WIDTH):
        if j != CONV_LEFT:
            xc = xc + shift(cx, CONV_LEFT - j) * cw[j:j + 1]
    xc = xc + cb
    xb = xc.astype(BF16)
    h = None
    for s in range(2):
        r = jax.nn.sigmoid(jnp.dot(xb, wa[s].astype(BF16), preferred_element_type=F32) + ba[s:s + 1])
        i = jax.nn.sigmoid(jnp.dot(xb, wx[s].astype(BF16), preferred_element_type=F32) + bx[s:s + 1])
        log_a = -LRU_C * r * jax.nn.softplus(-lam[s:s + 1])
        a = jnp.exp(log_a)
        one_minus_a2 = -jnp.tanh(log_a) * (a * a + 1.0)
        u = jnp.sqrt(one_minus_a2) * (i * xc)
        hs = scan(a, u, s == 1)
        h = hs if h is None else h + hs
    return h * jax.nn.gelu(cy)


def _lru_specs(T):
    xc, yc = OFF_CX // LANES, OFF_CY // LANES
    return [
        pl.BlockSpec((T, LANES), lambda b: (0, xc + b)),
        pl.BlockSpec((T, LANES), lambda b: (0, yc + b)),
        pl.BlockSpec((CONV_WIDTH, LANES), lambda b: (0, b)),
        pl.BlockSpec((1, LANES), lambda b: (0, b)),
        pl.BlockSpec((2, None, C_BLOCK_DIM, C_BLOCK_DIM), lambda b: (0, b, 0, 0)),
        pl.BlockSpec((2, LANES), lambda b: (0, b)),
        pl.BlockSpec((2, None, C_BLOCK_DIM, C_BLOCK_DIM), lambda b: (0, b, 0, 0)),
        pl.BlockSpec((2, LANES), lambda b: (0, b)),
        pl.BlockSpec((2, LANES), lambda b: (0, b)),
    ]


def lru_fwd(proj, cw, cb, wa, ba, wx, bx, lam, side=None):
    T = proj.shape[0]

    def body(cx, cy, cw_r, cb_r, wa_r, ba_r, wx_r, bx_r, lam_r, o_ref):
        o = f_lru(cx[...], cy[...], cw_r[...], cb_r[...], wa_r[...], ba_r[...], wx_r[...], bx_r[...], lam_r[...], False)
        o_ref[...] = o.astype(o_ref.dtype)

    return _pcall(
        body, [proj, proj, cw, cb, wa, ba, wx, bx, lam], side, grid=(C_BLOCKS,), in_specs=_lru_specs(T),
        out_specs=pl.BlockSpec((T, LANES), lambda b: (0, b)), out_shape=jax.ShapeDtypeStruct((T, C_WIDTH), BF16),
        compiler_params=_cparams(("arbitrary",)), name="lru_fwd")[0]


def lru_bwd(proj, cw, cb, wa, ba, wx, bx, lam, dcat, side=None):
    T = proj.shape[0]
    oc = (A_Q + B_V) // LANES

    def body(cx, cy, cw_r, cb_r, wa_r, ba_r, wx_r, bx_r, lam_r, do_ref, *outs):
        _, vjp = jax.vjp(functools.partial(f_lru, diff=True), cx[...], cy[...], cw_r[...], cb_r[...], wa_r[...],
                         ba_r[...], wx_r[...], bx_r[...], lam_r[...])
        grads = vjp(do_ref[...])
        for o, g in zip(outs, grads):
            o[...] = g.astype(o.dtype)

    specs = _lru_specs(T)
    out_specs = [pl.BlockSpec((T, LANES), lambda b: (0, b)), pl.BlockSpec((T, LANES), lambda b: (0, b))] + specs[2:]
    out_shape = [jax.ShapeDtypeStruct((T, C_WIDTH), BF16), jax.ShapeDtypeStruct((T, C_WIDTH), BF16)] + \
                [jax.ShapeDtypeStruct(p.shape, F32) for p in (cw, cb, wa, ba, wx, bx, lam)]
    return _pcall(
        body, [proj, proj, cw, cb, wa, ba, wx, bx, lam, dcat], side, grid=(C_BLOCKS,),
        in_specs=specs + [pl.BlockSpec((T, LANES), lambda b: (0, oc + b))], out_specs=out_specs, out_shape=out_shape,
        compiler_params=_cparams(("arbitrary",)), name="lru_bwd")


def all_gather8(name, blk):
    def body(x_ref, out_ref, send_sems, recv_sems):
        x, y, c, _, others = _place()
        sibling = (x, y, 1 - c)

        def slab(px, py, pc):
            return out_ref.at[4 * px + 2 * py + pc]

        first = [_remote(x_ref, slab(x, y, c), send_sems, recv_sems, 0, sibling)]
        first += [_remote(x_ref, slab(x, y, c), send_sems, recv_sems, 1 + j, (*ch, c)) for j, ch in enumerate(others)]
        for cp in first:
            cp.start()
        passed = [_remote(slab(*ch, c), slab(*ch, c), send_sems, recv_sems, 4 + j, sibling) for j, ch in enumerate(others)]
        for j, ch in enumerate(others):
            _remote(x_ref, slab(*ch, c), send_sems, recv_sems, 1 + j, (x, y, c)).wait_recv()
            passed[j].start()
        _remote(x_ref, slab(x, y, 1 - c), send_sems, recv_sems, 0, (x, y, c)).wait_recv()
        for j, ch in enumerate(others):
            _remote(x_ref, slab(*ch, 1 - c), send_sems, recv_sems, 4 + j, (x, y, c)).wait_recv()
        for cp in first + passed:
            cp.wait_send()

    out = pl.pallas_call(
        body, out_shape=jax.ShapeDtypeStruct((8,) + blk.shape, blk.dtype), in_specs=[HBM_SPEC], out_specs=HBM_SPEC,
        scratch_shapes=[pltpu.SemaphoreType.DMA((7,)), pltpu.SemaphoreType.DMA((7,))], name=name)(blk)
    me = 4 * lax.axis_index("x") + 2 * lax.axis_index("y") + lax.axis_index("c")
    return lax.dynamic_update_index_in_dim(out, blk, me, 0)


def _exchange_call(name, body, arrays, out_shapes, n_sems):
    n = len(arrays)

    def kernel_body(*refs):
        body(refs[:n], refs[n:2 * n], refs[2 * n], refs[2 * n + 1])

    return pl.pallas_call(
        kernel_body, out_shape=out_shapes, in_specs=[HBM_SPEC] * n, out_specs=[HBM_SPEC] * n,
        scratch_shapes=[pltpu.SemaphoreType.DMA((n * n_sems,)), pltpu.SemaphoreType.DMA((n * n_sems,))], name=name)(*arrays)


def chip_gather(name, shards):
    def body(ins, outs, send_sems, recv_sems):
        x, y, c, chip, others = _place()
        sibling = (x, y, 1 - c)
        first, passed = [], []
        for a, (x_ref, out_ref) in enumerate(zip(ins, outs)):
            first += [_remote(x_ref.at[c], out_ref.at[chip, c], send_sems, recv_sems, 6 * a + j, (*ch, c))
                      for j, ch in enumerate(others)]
        for cp in first:
            cp.start()
        for a, (x_ref, out_ref) in enumerate(zip(ins, outs)):
            for j, ch in enumerate(others):
                here = out_ref.at[_chip_index(ch), c]
                _remote(x_ref.at[c], here, send_sems, recv_sems, 6 * a + j, (x, y, c)).wait_recv()
                cp = _remote(here, here, send_sems, recv_sems, 6 * a + 3 + j, sibling)
                cp.start()
                passed.append(cp)
        for a, (x_ref, out_ref) in enumerate(zip(ins, outs)):
            for j, ch in enumerate(others):
                _remote(x_ref.at[c], out_ref.at[_chip_index(ch), 1 - c], send_sems, recv_sems, 6 * a + 3 + j, (x, y, c)).wait_recv()
        for cp in first + passed:
            cp.wait_send()

    outs = _exchange_call(name, body, shards, [jax.ShapeDtypeStruct((4,) + s.shape, s.dtype) for s in shards], 6)
    chip = 2 * lax.axis_index("x") + lax.axis_index("y")
    return [lax.dynamic_update_index_in_dim(o, s, chip, 0) for o, s in zip(outs, shards)]


def sum_slabs(name, r, out_dtype, tr):
    S, R, W = r.shape

    def body(*refs):
        t = refs[0][...].astype(F32)
        for s in range(1, S):
            t = t + refs[s][...].astype(F32)
        refs[S][...] = t.astype(out_dtype)

    return pl.pallas_call(
        body, grid=(R // tr,), in_specs=[pl.BlockSpec((None, tr, W), lambda i, s=s: (s, i, 0)) for s in range(S)],
        out_specs=pl.BlockSpec((tr, W), lambda i: (i, 0)), out_shape=jax.ShapeDtypeStruct((R, W), out_dtype),
        compiler_params=_cparams(("parallel",)), name=name)(*([r] * S))


def sum_chips(name, arrived, own, chip, tr):
    S, R, W = arrived.shape

    def body(chip_ref, own_ref, *refs):
        me = chip_ref[0]
        t = None
        for s in range(S):
            term = jnp.where(me == s, own_ref[...].astype(F32), refs[s][...].astype(F32))
            t = term if t is None else t + term
        refs[S][...] = t

    grid_spec = pltpu.PrefetchScalarGridSpec(
        num_scalar_prefetch=1, grid=(R // tr,),
        in_specs=[pl.BlockSpec((None, tr, W), lambda i, ch: (ch[0], i, 0))] +
                 [pl.BlockSpec((None, tr, W), lambda i, ch, s=s: (s, i, 0)) for s in range(S)],
        out_specs=pl.BlockSpec((tr, W), lambda i, ch: (i, 0)))
    return pl.pallas_call(body, grid_spec=grid_spec, out_shape=jax.ShapeDtypeStruct((R, W), F32),
                          compiler_params=_cparams(("parallel",)), name=name)(
                              chip.reshape(1).astype(jnp.int32), own, *([arrived] * S))


def add_kept_half(name, halves, got, c, tr):
    S, _, R, W = halves.shape

    def body(c_ref, h_ref, g_ref, o_ref):
        o_ref[...] = (h_ref[...].astype(F32) + g_ref[...].astype(F32)).astype(o_ref.dtype)

    grid_spec = pltpu.PrefetchScalarGridSpec(
        num_scalar_prefetch=1, grid=(S, R // tr),
        in_specs=[pl.BlockSpec((None, None, tr, W), lambda s, i, c_ref: (s, c_ref[0], i, 0)),
                  pl.BlockSpec((None, tr, W), lambda s, i, c_ref: (s, i, 0))],
        out_specs=pl.BlockSpec((None, tr, W), lambda s, i, c_ref: (s, i, 0)))
    return pl.pallas_call(body, grid_spec=grid_spec, out_shape=jax.ShapeDtypeStruct((S, R, W), halves.dtype),
                          compiler_params=_cparams(("parallel", "parallel")), name=name)(
                              c.reshape(1).astype(jnp.int32), halves, got)


def adamw_layer(name, l, g, row_off, w, m, v, prev, tr, tc=None, side=None):
    L, R, C = w.shape
    tc = C if tc is None else tc
    off = row_off // tr

    def body(g_ref, w_ref, m_ref, v_ref, *rest):
        outs = rest[-4:]
        gv = g_ref[...]
        d, mn, vn = f_adamw(w_ref[...], gv, m_ref[...], v_ref[...])
        for o, val in zip(outs, (gv, d, mn, vn)):
            o[...] = val

    slab = pl.BlockSpec((None, tr, tc), lambda i, j: (l, i, j))
    in_specs = [pl.BlockSpec((tr, tc), lambda i, j: (off + i, j)), slab, slab, slab]
    args = [g, w, m, v]
    aliases = {}
    if prev is not None:
        in_specs += [pl.BlockSpec(memory_space=pl.ANY)] * 4
        args += list(prev)
        aliases = {4 + k: k for k in range(4)}
    return _pcall(
        body, args, side, grid=(R // tr, C // tc), in_specs=in_specs, out_specs=[slab] * 4,
        out_shape=[jax.ShapeDtypeStruct((L, R, C), F32)] * 4, input_output_aliases=aliases,
        compiler_params=_cparams(("parallel", "parallel")), name=name)


BIG = ("w_in", "w_out", "xq", "xk", "xv", "xo", "w_up", "w_down")
GROUPS = {"mix": ("w_out", "xq", "xk", "xv", "xo"), "ff": ("w_up", "w_down")}
PACK_ROWS = {"w_out": 512, "xq": 512, "xk": 512, "xv": 512, "xo": 512, "w_up": 2048, "w_down": 2048}
GROUP_ROWS = {g: sum(PACK_ROWS[n] for n in names) for g, names in GROUPS.items()}
SUM_TILE = 256
PACK_OFF = {}
for _names in GROUPS.values():
    _o = 0
    for _n in _names:
        PACK_OFF[_n] = _o
        _o += PACK_ROWS[_n]

_SPLIT_OFF = np.cumsum((0,) + SPLIT_SIZES)
_KORDER = (0, 1, 2, 3, 4, 5, 6, 9, 10, 7, 8)


def w_in_to_kernel_cols(w):
    parts = [w[..., _SPLIT_OFF[i]:_SPLIT_OFF[i + 1]] for i in _KORDER]
    parts.append(jnp.zeros(w.shape[:-1] + (D_INP - D_IN,), w.dtype))
    return jnp.concatenate(parts, axis=-1)


def w_in_from_kernel_cols(w):
    offs = np.cumsum((0,) + tuple(SPLIT_SIZES[i] for i in _KORDER))
    pos = {k: (offs[n], offs[n + 1]) for n, k in enumerate(_KORDER)}
    return jnp.concatenate([w[..., pos[i][0]:pos[i][1]] for i in range(len(SPLIT_SIZES))], axis=-1)


def pack_shards(shards, group, dtype):
    return jnp.concatenate([shards[n].astype(dtype) for n in GROUPS[group]], axis=-2)


def unpack_rows(packed, name):
    return packed[..., PACK_OFF[name]:PACK_OFF[name] + PACK_ROWS[name], :]


WEIGHTS = ("rel_bias", "w_in", "w_out", "attn_sink", "gla_w2_f", "gla_b2_f", "gla_w2_b", "gla_b2_b", "gla_norm", "conv_w",
           "conv_b", "lru_wa", "lru_ba", "lru_wx", "lru_bx", "lru_lambda", "xq", "xk", "xv", "xo", "w_up", "w_down",
           "norm_mix_pre", "norm_mix_post", "norm_mem", "norm_x_pre", "norm_x_post", "norm_ff_pre", "norm_ff_post")
SMALL = tuple(n for n in WEIGHTS if n not in BIG)
SMALL_SHARDED = ("gla_w2_f", "gla_w2_b", "conv_w", "lru_ba", "lru_bx", "lru_lambda")
ROW_TILE = 256
SMALL_TILE = 512
W_IN_TILE = (344, 1024)
RIDE_PIECE_ROWS = (256, 512)


def _small_rows(n):
    return -(-n // (SUBLANES * LANES)) * SUBLANES


def _as_rows(a2):
    L, n = a2.shape
    rows = _small_rows(n)
    if rows * LANES != n:
        a2 = jnp.pad(a2, ((0, 0), (0, rows * LANES - n)))
    return a2.reshape(L * rows, LANES)


def _pack_small(items, layered):
    parts = []
    for it, lay in zip(items, layered):
        if isinstance(it, (list, tuple)):
            parts += [_as_rows(e.astype(F32).reshape(1, -1)) for e in it]
        else:
            parts.append(_as_rows(it.astype(F32).reshape(it.shape[0] if lay else 1, -1)))
    pad = -sum(p.shape[0] for p in parts) % SMALL_TILE
    if pad:
        parts.append(jnp.zeros((pad, LANES), F32))
    return jnp.concatenate(parts, axis=0)


def _unpack_small(buf, shapes, layered):
    lead = buf.shape[:-2]
    out, o = [], 0
    for s, lay in zip(shapes, layered):
        L = s[0] if lay else 1
        n = int(np.prod(s)) // L
        rows = _small_rows(n)
        part = buf[..., o:o + L * rows, :]
        if n != rows * LANES:
            part = part.reshape(lead + (L, rows * LANES))[..., :n]
        out.append(part.reshape(lead + tuple(s)))
        o += L * rows
    return out


def _relu2(r):
    return r, jnp.square(jnp.maximum(r, 0.0))


def _drelu2(r, u):
    return r * (2.0 * jnp.maximum(u, 0.0))


def kernel(x, mem, rel_bias, w_in, w_out, attn_sink, gla_w2_f, gla_b2_f, gla_w2_b, gla_b2_b, gla_norm, conv_w, conv_b, lru_wa, lru_ba, lru_wx, lru_bx, lru_lambda, xq, xk, xv, xo, w_up, w_down, norm_mix_pre, norm_mix_post, norm_mem, norm_x_pre, norm_x_post, norm_ff_pre, norm_ff_post, loss_target, m_rel_bias, m_w_in, m_w_out, m_attn_sink, m_gla_w2_f, m_gla_b2_f, m_gla_w2_b, m_gla_b2_b, m_gla_norm, m_conv_w, m_conv_b, m_lru_wa, m_lru_ba, m_lru_wx, m_lru_bx, m_lru_lambda, m_xq, m_xk, m_xv, m_xo, m_w_up, m_w_down, m_norm_mix_pre, m_norm_mix_post, m_norm_mem, m_norm_x_pre, m_norm_x_post, m_norm_ff_pre, m_norm_ff_post, v_rel_bias, v_w_in, v_w_out, v_attn_sink, v_gla_w2_f, v_gla_b2_f, v_gla_w2_b, v_gla_b2_b, v_gla_norm, v_conv_w, v_conv_b, v_lru_wa, v_lru_ba, v_lru_wx, v_lru_bx, v_lru_lambda, v_xq, v_xk, v_xv, v_xo, v_w_up, v_w_down, v_norm_mix_pre, v_norm_mix_post, v_norm_mem, v_norm_x_pre, v_norm_x_post, v_norm_ff_pre, v_norm_ff_post):
    w_args = (rel_bias, w_in, w_out, attn_sink, gla_w2_f, gla_b2_f, gla_w2_b, gla_b2_b, gla_norm, conv_w, conv_b, lru_wa,
              lru_ba, lru_wx, lru_bx, lru_lambda, xq, xk, xv, xo, w_up, w_down, norm_mix_pre, norm_mix_post, norm_mem,
              norm_x_pre, norm_x_post, norm_ff_pre, norm_ff_post)
    m_args = (m_rel_bias, m_w_in, m_w_out, m_attn_sink, m_gla_w2_f, m_gla_b2_f, m_gla_w2_b, m_gla_b2_b, m_gla_norm, m_conv_w,
              m_conv_b, m_lru_wa, m_lru_ba, m_lru_wx, m_lru_bx, m_lru_lambda, m_xq, m_xk, m_xv, m_xo, m_w_up, m_w_down,
              m_norm_mix_pre, m_norm_mix_post, m_norm_mem, m_norm_x_pre, m_norm_x_post, m_norm_ff_pre, m_norm_ff_post)
    v_args = (v_rel_bias, v_w_in, v_w_out, v_attn_sink, v_gla_w2_f, v_gla_b2_f, v_gla_w2_b, v_gla_b2_b, v_gla_norm, v_conv_w,
              v_conv_b, v_lru_wa, v_lru_ba, v_lru_wx, v_lru_bx, v_lru_lambda, v_xq, v_xk, v_xv, v_xo, v_w_up, v_w_down,
              v_norm_mix_pre, v_norm_mix_post, v_norm_mem, v_norm_x_pre, v_norm_x_post, v_norm_ff_pre, v_norm_ff_post)
    Wt, Mo, Vo = dict(zip(WEIGHTS, w_args)), dict(zip(WEIGHTS, m_args)), dict(zip(WEIGHTS, v_args))
    x, mem, tgt = x[0], mem[0], loss_target[0]
    D = D_MODEL
    depth = w_in.shape[0]
    chip = 2 * lax.axis_index("x") + lax.axis_index("y")
    core = lax.axis_index("c")

    sm_shapes = [Wt[n].shape for n in SMALL_SHARDED]
    yes = [True] * len(SMALL_SHARDED)
    g8 = all_gather8("gather_small_weights", _pack_small([Wt[n] for n in SMALL_SHARDED], yes))
    per_chip = _unpack_small(g8[0::2], sm_shapes, yes)
    whole = {n: jnp.concatenate([p[j] for j in range(4)], axis=-1) for n, p in zip(SMALL_SHARDED, per_chip)}

    def group_shards(l, group):
        shard = pack_shards({n: Wt[n][l] for n in GROUPS[group]}, group, BF16).reshape(2, GROUP_ROWS[group] // 2, D)
        return [shard] + ([w_in[l].astype(BF16).reshape(2, D // 2, D_IN // 4)] if group == "mix" else [])

    def whole_weights(group, gathered):
        g = gathered[0].reshape(4, GROUP_ROWS[group], D)
        if group == "ff":
            return {"w_up": unpack_rows(g, "w_up"), "w_down": unpack_rows(g, "w_down").reshape(D_FF, D)}
        W = {n: unpack_rows(g, n).reshape(D, D) for n in GROUPS["mix"]}
        win = gathered[1].reshape(4, D, D_IN // 4).transpose(1, 0, 2).reshape(D, D_IN)
        W["w_in"] = w_in_to_kernel_cols(win)
        return W

    first = group_shards(0, "mix")
    Wgot = {(0, "mix"): whole_weights("mix", chip_gather("gather_first_weights", first))}
    gathers = Stream()
    riding = {}
    for l in range(depth):
        for group in ("mix", "ff"):
            if (l, group) != (0, "mix"):
                shards = group_shards(l, group)
                riding[l, group] = (shards, gathers.add(RidingExchange("gather", shards, RIDE_PIECE_ROWS)))

    def need_weights(l, group):
        if (l, group) not in Wgot:
            shards, exchange = riding.pop((l, group))
            got = gathers.finish(exchange, "gather_rest")
            Wgot[l, group] = whole_weights(
                group, [lax.dynamic_update_index_in_dim(b, s, chip, 0) for b, s in zip(got, shards)])
        return Wgot[l, group]

    bucket = t5_bucket_map()
    bias = bias_table_fwd(rel_bias, bucket)

    def gain(name, l):
        return Wt[name][l][None]

    def layer_params(l):
        w2fp = jnp.zeros((LANES, B_QK), F32).at[0:GATE_RANK].set(whole["gla_w2_f"][l])
        w2bp = jnp.zeros((LANES, B_QK), F32).at[GATE_RANK:2 * GATE_RANK].set(whole["gla_w2_b"][l])
        gate = [w2fp, gla_b2_f[l][None], w2bp, gla_b2_b[l][None]]
        lru = [whole["conv_w"][l], conv_b[l][None], lru_wa[l], whole["lru_ba"][l], lru_wx[l], whole["lru_bx"][l],
               whole["lru_lambda"][l]]
        return attn_sink[l].reshape(A_KV_HEADS, A_GROUP, 1), gate, gla_norm[l][None], lru

    saved = []
    xcur = x
    (h1,) = rowmap("norm_first", f_norm, [x], [gain("norm_mix_pre", 0)], [(D, BF16)], ROW_TILE)
    loss_acc = None
    Wfull = []
    ride = gathers.take
    for l in range(depth):
        W = dict(need_weights(l, "mix"))
        sink3, gate, gn, lru = layer_params(l)
        (proj,) = mm("mm_in", h1, W["w_in"], "nn", [(F32, 1)], pm=512, pn=1408, side=ride(2))
        oa = attn_fwd(proj, bias, sink3, side=ride(3))
        zrow, grow = (proj, LANES, OFF_Z // LANES), (proj, B_V, OFF_BG // B_V)
        laf, lab = rowmap("gla_gate", f_gla_gate, [zrow], gate, [(B_QK, F32), (B_QK, F32)], ROW_TILE)
        oraw = gla_fwd(proj, laf, lab, side=ride(2))
        (ob,) = rowmap("gla_post", f_gla_post, [oraw, grow], [gn], [(B_V, BF16)], ROW_TILE)
        oc = lru_fwd(proj, *lru, side=ride(1))
        cat = jnp.concatenate([oa, ob, oc], axis=1)
        (mixed,) = mm("mm_out", cat, W["w_out"], "nn", [(F32, 1)], side=ride(1))
        x1, h2 = rowmap("resnorm_mix", f_resnorm, [xcur, mixed], [gain("norm_mix_post", l), gain("norm_x_pre", l)],
                        [(D, F32), (D, BF16)], ROW_TILE)
        (memn,) = rowmap("norm_mem", f_norm, [mem], [gain("norm_mem", l)], [(D, BF16)], ROW_TILE)
        (q,) = mm("mm_xq", h2, W["xq"], "nn", [(BF16, 1)], side=ride(1))
        (k,) = mm("mm_xk", memn, W["xk"], "nn", [(F32, 1)])
        (v,) = mm("mm_xv", memn, W["xv"], "nn", [(F32, 1)])
        (o,) = rowmap("xattn", f_xattn, [q], [k, v], [(D, BF16)], ROW_TILE)
        (xo_out,) = mm("mm_xo", o, W["xo"], "nn", [(F32, 1)], side=ride(1))
        x2, h3 = rowmap("resnorm_x", f_resnorm, [x1, xo_out], [gain("norm_x_post", l), gain("norm_ff_pre", l)],
                        [(D, F32), (D, BF16)], ROW_TILE)
        W.update(need_weights(l, "ff"))
        Wfull.append(W)
        u, act = mm("mm_up", h3, W["w_up"], "nn", [(F32, 1), (BF16, 1)], epilogue=_relu2, side=ride(3))
        (ff,) = mm("mm_down", act, W["w_down"], "nn", [(F32, 1)], side=ride(2))
        saved.append(dict(x0=xcur, h1=h1, proj=proj, laf=laf, lab=lab, oraw=oraw, cat=cat, mixed=mixed, x1=x1, h2=h2,
                          memn=memn, q=q, k=k, v=v, o=o, xo_out=xo_out, x2=x2, h3=h3, u=u, act=act, ff=ff))
        if l < depth - 1:
            xcur, h1 = rowmap("resnorm_ff", f_resnorm, [x2, ff], [gain("norm_ff_post", l), gain("norm_mix_pre", l + 1)],
                              [(D, F32), (D, BF16)], ROW_TILE)
        else:
            (loss_acc,) = rowmap("final_loss", f_final_loss, [x2, ff, tgt], [gain("norm_ff_post", l)], [], ROW_TILE,
                                 accs=[(1, 1)])
    loss = lax.psum(loss_acc[0, 0], ("x", "y", "c"))

    small_g = {n: [None] * depth for n in SMALL if n != "rel_bias"}
    adam = {}
    dbias_all = []
    dx_next = dh1_next = None
    grad_x = None
    scatters = Stream()
    ride = scatters.take
    taking = []
    inflight = []
    sharing = []

    def start_reduce(lyr, group, dW):
        halves = [dW[group].reshape(4, 2, GROUP_ROWS[group] // 2, D)]
        if group == "mix":
            g_in = w_in_from_kernel_cols(dW["w_in"]).reshape(D, 4, D_IN // 4).transpose(1, 0, 2)
            halves.append(g_in.reshape(4, 2, D // 2, D_IN // 4))
        taking.append((lyr, group, halves, scatters.add(RidingExchange("take", halves, RIDE_PIECE_ROWS))))

    def continue_reduce():
        lyr, group, halves, exchange = taking.pop()
        got = scatters.finish(exchange, "reduce_to_half_owner")
        sums = [add_kept_half("reduce_chip_sum", h, g, core, SUM_TILE) for h, g in zip(halves, got)]
        inflight.append((lyr, group, sums, scatters.add(RidingExchange("scatter", sums, RIDE_PIECE_ROWS))))

    def finish_reduce():
        lyr, group, sums, exchange = inflight.pop(0)
        arrived = scatters.finish(exchange, "reduce_rest")
        totals = [sum_chips("reduce_sum_chips", a, s, chip, SUM_TILE) for a, s in zip(arrived, sums)]
        sharing.append((lyr, group, totals, scatters.add(RidingExchange("pair", totals, RIDE_PIECE_ROWS))))

    w_in_t = [jnp.swapaxes(a, 1, 2) for a in (w_in, m_w_in, v_w_in)]

    def run_updates():
        for lyr, group, totals, exchange in sharing:
            both = scatters.finish(exchange, "reduce_share_halves")
            fulls = [lax.dynamic_update_index_in_dim(b, t, core, 0) for b, t in zip(both, totals)]
            full = fulls[0].reshape(GROUP_ROWS[group], D)
            for n in GROUPS[group]:
                adam[n] = adamw_layer("adamw_" + n, lyr, full, PACK_OFF[n], Wt[n], Mo[n], Vo[n], adam.get(n), SUM_TILE)
            if group == "mix":
                g = fulls[1].reshape(D, D_IN // 4).T
                adam["w_in"] = adamw_layer("adamw_w_in", lyr, g, 0, *w_in_t, adam.get("w_in"), *W_IN_TILE)

    for l in reversed(range(depth)):
        W, S = Wfull[l], saved[l]
        sink3, gate, gn, lru = layer_params(l)
        if l == depth - 1:
            (dx2, dff), (dgp,) = rowmap_bwd("final_bwd", f_final_rows, [S["x2"], S["ff"], tgt], [gain("norm_ff_post", l)],
                                            [None], ROW_TILE, [F32, F32, None], [True])
        else:
            (dx2, dff), (dgp, dgn_next) = rowmap_bwd(
                "resnorm_ff_bwd", f_resnorm, [S["x2"], S["ff"]], [gain("norm_ff_post", l), gain("norm_mix_pre", l + 1)],
                [dx_next, dh1_next], ROW_TILE, [F32, F32], [True, True])
            small_g["norm_mix_pre"][l + 1] = dgn_next[0]
        small_g["norm_ff_post"][l] = dgp[0]
        dW = {}
        (du,) = mm("mm_down_bwd", dff, W["w_down"], "nt", [(BF16, 1)], epilogue=_drelu2, extras=[S["u"]], side=ride(2))
        (pack,) = mm("mm_down_wgrad", S["act"], dff, "tn",
                     [(BF16, ("rows", GROUP_ROWS["ff"], PACK_OFF["w_down"], None))], side=ride(2))
        (dW["ff"],) = mm("mm_up_wgrad", S["h3"], du, "tn",
                         [(BF16, ("cols", GROUP_ROWS["ff"], PACK_OFF["w_up"], pack))], side=ride(2))
        start_reduce(l, "ff", dW)
        (dh3,) = mm("mm_up_bwd", du, W["w_up"], "nt", [(F32, 1)], pk=D, side=ride(2))
        continue_reduce()
        if len(inflight) > 1:
            finish_reduce()
        (dx1, dxo_out), (dg1, dg2) = rowmap_bwd(
            "resnorm_x_bwd", f_resnorm, [S["x1"], S["xo_out"]], [gain("norm_x_post", l), gain("norm_ff_pre", l)],
            [dx2, dh3], ROW_TILE, [F32, F32], [True, True])
        small_g["norm_x_post"][l], small_g["norm_ff_pre"][l] = dg1[0], dg2[0]
        (do,) = mm("mm_xo_bwd", dxo_out, W["xo"], "nt", [(F32, 1)])
        def into_mix(name, so_far):
            return [(BF16, ("rows", GROUP_ROWS["mix"], PACK_OFF[name], so_far))]

        (mix_pack,) = mm("mm_xo_wgrad", S["o"], dxo_out, "tn", into_mix("xo", None), pm=512)
        (dq,), (dk, dv) = rowmap_bwd("xattn_bwd", f_xattn, [S["q"]], [S["k"], S["v"]], [do], ROW_TILE, [BF16], [True, True])
        (mix_pack,) = mm("mm_xq_wgrad", S["h2"], dq, "tn", into_mix("xq", mix_pack), pm=512)
        (dh2,) = mm("mm_xq_bwd", dq, W["xq"], "nt", [(F32, 1)])
        (mix_pack,) = mm("mm_xk_wgrad", S["memn"], dk, "tn", into_mix("xk", mix_pack), pm=512)
        (mix_pack,) = mm("mm_xv_wgrad", S["memn"], dv, "tn", into_mix("xv", mix_pack), pm=512)
        (dmk,) = mm("mm_xk_bwd", dk, W["xk"], "nt", [(F32, 1)])
        (dmv,) = mm("mm_xv_bwd", dv, W["xv"], "nt", [(F32, 1)])
        _, (dgm,) = rowmap_bwd("norm_mem_bwd", f_norm_twice, [mem], [gain("norm_mem", l)], [dmk, dmv], ROW_TILE, [None], [True])
        small_g["norm_mem"][l] = dgm[0]
        (dx0, dmixed), (dg1, dg2) = rowmap_bwd(
            "resnorm_mix_bwd", f_resnorm, [S["x0"], S["mixed"]], [gain("norm_mix_post", l), gain("norm_x_pre", l)],
            [dx1, dh2], ROW_TILE, [F32, F32], [True, True])
        small_g["norm_mix_post"][l], small_g["norm_x_pre"][l] = dg1[0], dg2[0]
        (dcat,) = mm("mm_out_bwd", dmixed, W["w_out"], "nt", [(F32, 1)])
        (dW["mix"],) = mm("mm_out_wgrad", S["cat"], dmixed, "tn", into_mix("w_out", mix_pack), pm=512)
        proj = S["proj"]
        daq, dak, dav, dbias, dsink = attn_bwd(proj, bias, sink3, dcat, side=ride(2))
        dbias_all.append(dbias)
        small_g["attn_sink"][l] = dsink.reshape(A_HEADS)
        zrow, grow = (proj, LANES, OFF_Z // LANES), (proj, B_V, OFF_BG // B_V)
        (doraw, dbg), (dgn,) = rowmap_bwd("gla_post_bwd", f_gla_post, [S["oraw"], grow], [gn], [(dcat, B_V, A_Q // B_V)],
                                          ROW_TILE, [F32, BF16], [True])
        dbq, dbk, dbv, dlaf, dlab = gla_bwd(proj, S["laf"], S["lab"], doraw, side=ride(3))
        (dz,), (dw2fp, db2f, dw2bp, db2b) = rowmap_bwd("gla_gate_bwd", f_gla_gate, [zrow], gate, [dlaf, dlab], ROW_TILE,
                                                        [BF16], [True] * 4)
        small_g["gla_norm"][l] = dgn[0]
        small_g["gla_w2_f"][l], small_g["gla_b2_f"][l] = dw2fp[0:GATE_RANK], db2f[0]
        small_g["gla_w2_b"][l], small_g["gla_b2_b"][l] = dw2bp[GATE_RANK:2 * GATE_RANK], db2b[0]
        dcx, dcy, dcw, dcb, dwa, dba, dwx, dbx, dlam = lru_bwd(proj, *lru, dcat, side=ride(1))
        small_g["conv_w"][l], small_g["conv_b"][l] = dcw, dcb[0]
        small_g["lru_wa"][l], small_g["lru_ba"][l], small_g["lru_wx"][l] = dwa, dba, dwx
        small_g["lru_bx"][l], small_g["lru_lambda"][l] = dbx, dlam
        dproj = jnp.concatenate([daq, dak, dav, dbq, dbk, dbv, dbg, dcx, dcy, dz], axis=1)
        (dW["w_in"],) = mm("mm_in_wgrad", S["h1"], dproj, "tn", [(BF16, 1)], pm=512, pn=1408, side=ride(1))
        (dh1,) = mm("mm_in_bwd", dproj, W["w_in"], "nt", [(F32, 1)], side=ride(1))
        if l > 0:
            dx_next, dh1_next = dx0, dh1
        else:
            (grad_x,), (dg0,) = rowmap_bwd("norm_first_bwd", f_norm_keep, [x], [gain("norm_mix_pre", 0)], [dx0, dh1],
                                           ROW_TILE, [F32], [True])
            small_g["norm_mix_pre"][0] = dg0[0]

        finish_reduce()
        start_reduce(l, "mix", dW)
        continue_reduce()
    finish_reduce()
    run_updates()

    dtab = bias_table_bwd(dbias_all, bucket)
    small_g["rel_bias"] = dtab[:, :A_HEADS]
    layered = [n != "rel_bias" for n in SMALL]
    sg_shapes = [(depth,) + small_g[n][0].shape if lay else small_g[n].shape for n, lay in zip(SMALL, layered)]
    contributions = all_gather8("gather_small_grads", _pack_small([small_g[n] for n in SMALL], layered))
    sg_sum = sum_slabs("sum_small_grads", contributions, F32, SMALL_TILE)
    sg = dict(zip(SMALL, _unpack_small(sg_sum, sg_shapes, layered)))
    for n in SMALL_SHARDED:
        w = Wt[n].shape[-1]
        sg[n] = lax.dynamic_slice_in_dim(sg[n], chip * w, w, axis=sg[n].ndim - 1)

    grads, delta, new_m, new_v = {}, {}, {}, {}
    adam["w_in"] = [jnp.swapaxes(a, 1, 2) for a in adam["w_in"]]
    for n in BIG:
        grads[n], delta[n], new_m[n], new_v[n] = adam[n]
    shapes = [Wt[n].shape for n in SMALL]
    packs = [_pack_small([src[n] for n in SMALL], layered) for src in (Wt, sg, Mo, Vo)]
    d_, m_, v_ = rowmap("adamw_small", f_adamw, packs, [], [(LANES, F32)] * 3, SMALL_TILE)
    for n, a, b, c_ in zip(SMALL, *[_unpack_small(p, shapes, layered) for p in (d_, m_, v_)]):
        grads[n], delta[n], new_m[n], new_v[n] = sg[n], a, b, c_

    return (loss, grad_x[None], *[grads[n] for n in WEIGHTS], *[delta[n] for n in WEIGHTS],
            *[new_m[n] for n in WEIGHTS], *[new_v[n] for n in WEIGHTS])
```

```python
import functools
import math

import numpy as np
import jax
import jax.numpy as jnp
from jax import lax
from jax.experimental import pallas as pl
from jax.experimental.pallas import tpu as pltpu

F32, BF16 = jnp.float32, jnp.bfloat16
HI = lax.Precision.HIGHEST
MESH = pl.DeviceIdType.MESH

VMEM_LIMIT_BYTES = 56 * 1024 * 1024
LANES = 128
SUBLANES = 8

D_MODEL = 2048
DEPTH = 4
A_HEAD_DIM = 128
A_HEADS = 8
A_KV_HEADS = 2
A_GROUP = 4
WINDOW = 128
BLOCK = 128
N_BUCKETS = 32
MAX_DISTANCE = 128
B_HEADS = 4
B_KEY_DIM = 64
B_VAL_DIM = 128
GATE_RANK = 16
GATE_TAU = 16.0
C_WIDTH = 512
C_BLOCKS = 4
C_BLOCK_DIM = 128
CONV_WIDTH = 4
CONV_LEFT = 2
LRU_C = 8.0
X_HEADS = 4
X_HEAD_DIM = 512
D_FF = 4 * D_MODEL
EPS = 1e-6
NEG_INF = -1e30
A_Q, A_KV, B_QK, B_V = 1024, 256, 256, 512
SPLIT_SIZES = (A_Q, A_KV, A_KV, B_QK, B_QK, B_V, B_V, GATE_RANK, GATE_RANK, C_WIDTH, C_WIDTH)
D_IN = sum(SPLIT_SIZES)
D_INP = 4224
OFF_AQ, OFF_AK, OFF_AV, OFF_BQ, OFF_BK, OFF_BV, OFF_BG, OFF_CX, OFF_CY, OFF_Z = (
    0, 1024, 1280, 1536, 1792, 2048, 2560, 3072, 3584, 4096)
GLA_CHUNK = 128

ADAM_LR, ADAM_B1, ADAM_B2, ADAM_EPS, ADAM_WD, ADAM_STEP = 0.001, 0.9, 0.999, 1e-08, 0.01, 10


def _cparams(sem=None):
    return pltpu.CompilerParams(dimension_semantics=sem, vmem_limit_bytes=VMEM_LIMIT_BYTES)


def _full_spec(a):
    nd = a.ndim
    return pl.BlockSpec(a.shape, lambda *_: (0,) * nd)


def _tup(r):
    return r if isinstance(r, tuple) else (r,)


HBM_SPEC = pl.BlockSpec(memory_space=pltpu.HBM)


def _place():
    x, y, c = lax.axis_index("x"), lax.axis_index("y"), lax.axis_index("c")
    others = [(1 - x, y), (x, 1 - y), (1 - x, 1 - y)]
    return x, y, c, 2 * x + y, others


def _remote(src, dst, send_sems, recv_sems, k, to):
    return pltpu.make_async_remote_copy(src_ref=src, dst_ref=dst, send_sem=send_sems.at[k], recv_sem=recv_sems.at[k],
                                        device_id=to, device_id_type=MESH)


def _chip_index(ch):
    return 2 * ch[0] + ch[1]


def _pcall(body, args, side=None, **kw):
    if side is None:
        res = pl.pallas_call(body, **kw)(*args)
        return list(res) if isinstance(res, (list, tuple)) else [res]
    single = not isinstance(kw["out_shape"], (list, tuple))
    out_shape = [kw.pop("out_shape")] if single else list(kw.pop("out_shape"))
    out_specs = [kw.pop("out_specs")] if single else list(kw.pop("out_specs"))
    in_specs = list(kw.pop("in_specs"))
    scratch = list(kw.pop("scratch_shapes", ()))
    grid = kw.get("grid", ())
    n_in, n_out, n_scr = len(in_specs), len(out_shape), len(scratch)
    srcs, bufs = side.srcs, side.bufs
    ns, nb = len(srcs), len(bufs)

    def wrapped(*refs):
        ins = refs[:n_in]
        src_refs = refs[n_in:n_in + ns]
        o0 = n_in + ns + nb
        outs = refs[o0:o0 + n_out]
        buf_refs = refs[o0 + n_out:o0 + n_out + nb]
        s0 = o0 + n_out + nb
        scr = refs[s0:s0 + n_scr]
        send_sems, recv_sems = refs[s0 + n_scr], refs[s0 + n_scr + 1]
        first = last = None
        for d, n in enumerate(grid):
            f, l_ = pl.program_id(d) == 0, pl.program_id(d) == n - 1
            first = f if first is None else first & f
            last = l_ if last is None else last & l_
        if first is None:
            side.start(src_refs, buf_refs, send_sems, recv_sems)
            body(*ins, *outs, *scr)
            side.finish(src_refs, buf_refs, send_sems, recv_sems)
            return
        pl.when(first)(lambda: side.start(src_refs, buf_refs, send_sems, recv_sems))
        body(*ins, *outs, *scr)
        pl.when(last)(lambda: side.finish(src_refs, buf_refs, send_sems, recv_sems))

    any_spec = pl.BlockSpec(memory_space=pl.ANY)
    aliases = dict(kw.pop("input_output_aliases", {}))
    aliases.update({n_in + ns + i: n_out + i for i in range(nb)})
    cp = kw.pop("compiler_params", None)
    if grid:
        cp = _cparams(("arbitrary",) * len(grid))
    res = pl.pallas_call(
        wrapped, in_specs=in_specs + [any_spec] * (ns + nb), out_specs=out_specs + [any_spec] * nb,
        out_shape=out_shape + [jax.ShapeDtypeStruct(b.shape, b.dtype) for b in bufs],
        scratch_shapes=scratch + [pltpu.SemaphoreType.DMA((side.n_sems,)), pltpu.SemaphoreType.DMA((side.n_sems,))],
        input_output_aliases=aliases, compiler_params=cp, **kw)(*args, *srcs, *bufs)
    side.done(list(res[n_out:]))
    return list(res[:n_out])


class _Side:
    def __init__(self, parts):
        self.parts = parts
        self.srcs = [s for p in parts for s in p[0].srcs]
        self.bufs = [b for p in parts for b in p[0].bufs]
        self.n_sems = max(1, sum(4 * len(now) + 4 * len(relay) + len(last) for _, now, relay, last in parts))

    def done(self, bufs):
        for p in self.parts:
            p[0].bufs, bufs = bufs[:len(p[0].bufs)], bufs[len(p[0].bufs):]

    def _copies(self, src_refs, buf_refs, send_sems, recv_sems):
        x, y, c, chip, others = _place()
        xn, yn, dg = others
        me, sibling = (x, y, c), (x, y, 1 - c)
        mine, landing = [], []
        k = o = 0

        def pair(src, dst, got, to):
            nonlocal k
            mine.append(_remote(src, dst, send_sems, recv_sems, k, to))
            landing.append(_remote(src, got, send_sems, recv_sems, k, me))
            k += 1

        for ex, now, relay, last in self.parts:
            srcs, bufs = src_refs[o:o + len(ex.srcs)], buf_refs[o:o + len(ex.srcs)]
            o += len(ex.srcs)
            for a, r0, n in now:
                rows = pl.ds(r0, n)
                if ex.kind == "gather":
                    for ch in (xn, yn):
                        pair(srcs[a].at[c, rows], bufs[a].at[chip, c, rows], bufs[a].at[_chip_index(ch), c, rows], (*ch, c))
                elif ex.kind == "scatter":
                    for ch in others:
                        pair(srcs[a].at[_chip_index(ch), rows], bufs[a].at[chip, rows], bufs[a].at[_chip_index(ch), rows],
                             (*ch, c))
                elif ex.kind == "take":
                    for s in range(4):
                        pair(srcs[a].at[s, 1 - c, rows], bufs[a].at[s, rows], bufs[a].at[s, rows], sibling)
                else:
                    pair(srcs[a].at[rows], bufs[a].at[c, rows], bufs[a].at[1 - c, rows], sibling)
            for a, r0, n in relay:
                top, bottom, rows = pl.ds(r0, n // 2), pl.ds(r0 + n // 2, n // 2), pl.ds(r0, n)
                from_x, from_y = bufs[a].at[_chip_index(xn), c, top], bufs[a].at[_chip_index(yn), c, bottom]
                pair(from_x, from_x, bufs[a].at[_chip_index(dg), c, top], (*yn, c))
                pair(from_y, from_y, bufs[a].at[_chip_index(dg), c, bottom], (*xn, c))
                for ch in (xn, yn):
                    here = bufs[a].at[_chip_index(ch), c, rows]
                    pair(here, here, bufs[a].at[_chip_index(ch), 1 - c, rows], sibling)
            for a, r0, n in last:
                here = bufs[a].at[_chip_index(dg), c, pl.ds(r0, n)]
                pair(here, here, bufs[a].at[_chip_index(dg), 1 - c, pl.ds(r0, n)], sibling)
        return mine, landing

    def start(self, src_refs, buf_refs, send_sems, recv_sems):
        for cp in self._copies(src_refs, buf_refs, send_sems, recv_sems)[0]:
            cp.start()

    def finish(self, src_refs, buf_refs, send_sems, recv_sems):
        mine, landing = self._copies(src_refs, buf_refs, send_sems, recv_sems)
        for cp in landing:
            cp.wait_recv()
        for cp in mine:
            cp.wait_send()


class _StagedSide(_Side):
    def start(self, src_refs, buf_refs, send_sems, recv_sems):
        pass

    def finish(self, src_refs, buf_refs, send_sems, recv_sems):
        mine, landing = self._copies(src_refs, buf_refs, send_sems, recv_sems)
        (_, now, relay, _), = self.parts
        cuts = [0, 2 * len(now), 2 * len(now) + 4 * len(relay), len(mine)]
        for lo, hi in zip(cuts, cuts[1:]):
            for cp in mine[lo:hi]:
                cp.start()
            for cp in landing[lo:hi]:
                cp.wait_recv()
        for cp in mine:
            cp.wait_send()


class RidingExchange:
    KINDS = {"gather": (lambda s: (4,) + s, 1, False), "scatter": (lambda s: s, 1, False),
             "take": (lambda s: (s[0],) + s[2:], 2, True), "pair": (lambda s: (2,) + s, 0, True)}

    def __init__(self, kind, srcs, piece_rows):
        self.kind, self.srcs = kind, list(srcs)
        shape_of, row_axis, self.cheap = self.KINDS[kind]
        self.bufs = [lax.empty(shape_of(tuple(s.shape)), s.dtype) for s in srcs]
        heights = [s.shape[row_axis] for s in srcs]
        per = [[(a, r0, min(pr, h - r0)) for r0 in range(0, h, pr)] for a, (h, pr) in enumerate(zip(heights, piece_rows))]
        self.pieces = list(per[0])
        for extra in per[1:]:
            step = max(1, len(self.pieces) // (len(extra) + 1))
            for i, p in enumerate(extra):
                self.pieces.insert(min(len(self.pieces), (i + 1) * step + i), p)
        self.landed, self.relayed = [], []

    def busy(self):
        return bool(self.pieces or self.landed or self.relayed)

    def step(self, n):
        out = []
        for a, r0, rows in self.pieces[:n]:
            if out and out[-1][0] == a and out[-1][1] + out[-1][2] == r0:
                out[-1] = (a, out[-1][1], out[-1][2] + rows)
            else:
                out.append((a, r0, rows))
        self.pieces = self.pieces[n:]
        relay, last = self.landed, self.relayed
        self.landed, self.relayed = (out if self.kind == "gather" else []), relay
        return out, relay, last


D2D_PIECES_A_CALL = 8


class Stream:
    def __init__(self):
        self.queue = []

    def add(self, exchange):
        self.queue.append(exchange)
        return exchange

    def take(self, n, only=None):
        parts = []
        for ex in (self.queue if only is None else [only]):
            had = len(ex.pieces)
            if ex.cheap and only is None:
                now, relay, last = ex.step(D2D_PIECES_A_CALL)
            else:
                now, relay, last = ex.step(n)
                n -= had - len(ex.pieces)
            if now or relay or last:
                parts.append((ex, now, relay, last))
        return _Side(parts) if parts else None

    def finish(self, exchange, name):
        while exchange.busy():
            side = self.take(len(exchange.pieces), only=exchange)
            _pcall(lambda: None, [], side, in_specs=[], out_specs=[], out_shape=[], name=name)
        self.queue.remove(exchange)
        return exchange.bufs


def _row_ops(rows, tr):
    arrs, specs, widths = [], [], []
    for r in rows:
        arr, n, j = r if isinstance(r, tuple) else (r, r.shape[1], 0)
        arrs.append(arr)
        widths.append(n)
        specs.append(pl.BlockSpec((tr, n), lambda i, j=j: (i, j)))
    return arrs, specs, widths


def rowmap(name, f, rows, params, outs, tr, accs=()):
    rows, row_specs, _ = _row_ops(rows, tr)
    T = rows[0].shape[0]
    nin, nout, nacc = len(rows) + len(params), len(outs), len(accs)

    def body(*refs):
        res = _tup(f(*[r[...] for r in refs[:nin]]))
        for o, r in zip(refs[nin:nin + nout], res[:nout]):
            o[...] = r.astype(o.dtype)
        arefs = refs[nin + nout:]
        if nacc:
            @pl.when(pl.program_id(0) == 0)
            def _():
                for a in arefs:
                    a[...] = jnp.zeros(a.shape, a.dtype)
            for a, r in zip(arefs, res[nout:]):
                a[...] += r.astype(F32)

    in_specs = row_specs + [_full_spec(p) for p in params]
    out_specs = [pl.BlockSpec((tr, n), lambda i: (i, 0)) for n, _ in outs] + \
                [pl.BlockSpec(s, lambda i, nd=len(s): (0,) * nd) for s in accs]
    out_shape = [jax.ShapeDtypeStruct((T, n), d) for n, d in outs] + [jax.ShapeDtypeStruct(s, F32) for s in accs]
    res = pl.pallas_call(body, grid=(T // tr,), in_specs=in_specs, out_specs=out_specs, out_shape=out_shape,
                         compiler_params=_cparams(("arbitrary",)), name=name)(*rows, *params)
    return tuple(res)


def rowmap_bwd(name, f, rows, params, cots, tr, drow_dtypes, want_params):
    rows, row_specs, widths = _row_ops(rows, tr)
    T = rows[0].shape[0]
    nr, npar = len(rows), len(params)
    cot_arrays, cot_specs, _ = _row_ops([c for c in cots if c is not None], tr)
    nc = len(cot_arrays)
    ridx = [i for i, d in enumerate(drow_dtypes) if d is not None]
    pidx = [i for i, w in enumerate(want_params) if w]

    def body(*refs):
        rvals = [r[...] for r in refs[:nr]]
        pvals = [r[...] for r in refs[nr:nr + npar]]
        crefs = list(refs[nr + npar:nr + npar + nc])
        orefs = refs[nr + npar + nc:]
        outs, vjp = jax.vjp(f, *rvals, *pvals)
        outs = _tup(outs)
        cts = []
        for c, o in zip(cots, outs):
            cts.append(jnp.ones(o.shape, o.dtype) if c is None else crefs.pop(0)[...].astype(o.dtype))
        grads = vjp(tuple(cts) if len(cts) > 1 else cts[0])
        for o, i in zip(orefs[:len(ridx)], ridx):
            o[...] = grads[i].astype(o.dtype)
        prefs = orefs[len(ridx):]
        if prefs:
            @pl.when(pl.program_id(0) == 0)
            def _():
                for a in prefs:
                    a[...] = jnp.zeros(a.shape, a.dtype)
            for a, i in zip(prefs, pidx):
                a[...] += grads[nr + i].astype(F32)

    in_specs = row_specs + [_full_spec(p) for p in params] + cot_specs
    out_specs = [pl.BlockSpec((tr, widths[i]), lambda i: (i, 0)) for i in ridx] + [_full_spec(params[i]) for i in pidx]
    out_shape = [jax.ShapeDtypeStruct((T, widths[i]), drow_dtypes[i]) for i in ridx] + \
                [jax.ShapeDtypeStruct(params[i].shape, F32) for i in pidx]
    res = pl.pallas_call(body, grid=(T // tr,), in_specs=in_specs, out_specs=out_specs, out_shape=out_shape,
                         compiler_params=_cparams(("arbitrary",)), name=name)(*rows, *params, *cot_arrays)
    res = tuple(res)
    return res[:len(ridx)], res[len(ridx):]


def _pick(n, pref):
    best = None
    for d in range(LANES, min(n, pref) + 1, LANES):
        if n % d == 0:
            best = d
    return best if best is not None else n


def _spec2(arr, tile, pos):
    tr, tc = tile
    if arr.ndim == 2:
        return pl.BlockSpec((tr, tc), lambda i, j, k: pos(i, j, k))
    assert arr.shape[2] % tc == 0, (arr.shape, tile)
    per = arr.shape[2] // tc

    def imap(i, j, k):
        r, c = pos(i, j, k)
        return (c // per, r, c % per)
    return pl.BlockSpec((None, tr, tc), imap)


def _dims2(arr):
    return (arr.shape[0], arr.shape[1]) if arr.ndim == 2 else (arr.shape[1], arr.shape[0] * arr.shape[2])


def mm(name, a, b, mode, outs, epilogue=None, extras=(), pm=1024, pn=512, pk=4224, side=None):
    ar, ac = _dims2(a)
    br, bc = _dims2(b)
    if mode == "nn":
        M, K, N = ar, ac, bc
    elif mode == "nt":
        M, K, N = ar, ac, br
    else:
        M, K, N = ac, ar, bc
    tm, tn, tk = _pick(M, pm), _pick(N, pn), _pick(K, pk)
    for arr in (a, b) + tuple(extras):
        if arr.ndim == 3:
            assert arr.shape[2] % LANES == 0
    if mode == "nn":
        a_spec = _spec2(a, (tm, tk), lambda i, j, k: (i, k))
        b_spec = _spec2(b, (tk, tn), lambda i, j, k: (k, j))
        dims = (((1,), (0,)), ((), ()))
    elif mode == "nt":
        a_spec = _spec2(a, (tm, tk), lambda i, j, k: (i, k))
        b_spec = _spec2(b, (tn, tk), lambda i, j, k: (j, k))
        dims = (((1,), (1,)), ((), ()))
    else:
        a_spec = _spec2(a, (tk, tm), lambda i, j, k: (k, i))
        b_spec = _spec2(b, (tk, tn), lambda i, j, k: (k, j))
        dims = (((0,), (0,)), ((), ()))
    nk = K // tk
    nex = len(extras)

    def body(*refs):
        a_ref, b_ref = refs[0], refs[1]
        ex_refs = refs[2:2 + nex]
        o_refs = refs[2 + nex:2 + nex + len(outs)]
        acc = refs[-1]
        k = pl.program_id(2)
        part = lax.dot_general(a_ref[...].astype(BF16), b_ref[...].astype(BF16), dims, preferred_element_type=F32)

        def finish(r):
            res = (r,) if epilogue is None else _tup(epilogue(r, *[e[...] for e in ex_refs]))
            for o, v in zip(o_refs, res):
                o[...] = v.astype(o.dtype)

        if nk == 1:
            finish(part)
            return

        @pl.when(k == 0)
        def _():
            acc[...] = part

        @pl.when(k > 0)
        def _():
            acc[...] += part

        @pl.when(k == nk - 1)
        def _():
            finish(acc[...])

    out_shape, out_specs = [], []
    args, in_specs, aliases = [a, b, *extras], [a_spec, b_spec], {}
    in_specs += [_spec2(e, (tm, tn), lambda i, j, k: (i, j)) for e in extras]
    for dt, chunks in outs:
        if isinstance(chunks, tuple):
            how, rows, off, buf = chunks
            o = jax.ShapeDtypeStruct((4, rows, D_MODEL), dt)
            assert off % tm == 0 and D_MODEL % tn == 0 and D_MODEL % tm == 0
            if how == "cols":
                per = D_MODEL // tn
                spec = pl.BlockSpec((None, tm, tn), lambda i, j, k: (j // per, off // tm + i, j % per))
            else:
                per = D_MODEL // tm
                spec = pl.BlockSpec((None, tm, tn), lambda i, j, k: (i // per, off // tm + i % per, j))
            if buf is not None:
                aliases[len(args)] = len(out_shape)
                args.append(buf)
                in_specs.append(pl.BlockSpec(memory_space=pl.ANY))
        else:
            o = jax.ShapeDtypeStruct((M, N) if chunks == 1 else (chunks, M, N // chunks), dt)
            spec = _spec2(o, (tm, tn), lambda i, j, k: (i, j))
        out_shape.append(o)
        out_specs.append(spec)
    n_extra_in = len(args) - 2 - nex
    res = _pcall(
        (lambda *refs: body(*refs[:2 + nex], *refs[2 + nex + n_extra_in:])) if n_extra_in else body, args, side,
        grid=(M // tm, N // tn, nk), in_specs=in_specs, out_specs=out_specs, out_shape=out_shape,
        scratch_shapes=[pltpu.VMEM((tm, tn), F32)] if nk > 1 else [], input_output_aliases=aliases,
        compiler_params=_cparams(("parallel", "parallel", "arbitrary")), name=name)
    return tuple(res)


def _rms(x, g):
    return x * lax.rsqrt(jnp.mean(x * x, axis=-1, keepdims=True) + EPS) * g


def f_norm(x, g):
    return _rms(x, g)


def f_norm_keep(x, g):
    return x, _rms(x, g)


def f_resnorm(xp, m, gpost, gnext):
    xn = xp + _rms(m, gpost)
    return xn, _rms(xn, gnext)


def f_final_rows(xp, m, tgt, gpost):
    xn = xp + _rms(m, gpost)
    return 0.5 * jnp.mean(jnp.square(xn - tgt), axis=-1, keepdims=True)


def f_final_loss(xp, m, tgt, gpost):
    return jnp.sum(f_final_rows(xp, m, tgt, gpost), axis=0, keepdims=True)


def f_norm_twice(x, g):
    y = _rms(x, g)
    return y, y


def f_xattn(q, k, v):
    outs = []
    for h in range(X_HEADS):
        sl = slice(h * X_HEAD_DIM, (h + 1) * X_HEAD_DIM)
        s = lax.dot_general(q[:, sl].astype(BF16), k[:, sl].astype(BF16), (((1,), (1,)), ((), ())),
                            preferred_element_type=F32) * (X_HEAD_DIM ** -0.5)
        m = jnp.max(s, axis=-1, keepdims=True)
        e = jnp.exp(s - m)
        p = e / jnp.sum(e, axis=-1, keepdims=True)
        outs.append(jnp.dot(p.astype(BF16), v[:, sl].astype(BF16), preferred_element_type=F32))
    return jnp.concatenate(outs, axis=1)


def f_adamw(w, g, m, v):
    m = ADAM_B1 * m + (1.0 - ADAM_B1) * g
    v = ADAM_B2 * v + (1.0 - ADAM_B2) * jnp.square(g)
    m_hat = m / (1.0 - ADAM_B1 ** ADAM_STEP)
    v_hat = v / (1.0 - ADAM_B2 ** ADAM_STEP)
    delta = -ADAM_LR * (m_hat / (jnp.sqrt(v_hat) + ADAM_EPS) + ADAM_WD * w)
    return delta, m, v


def t5_bucket_map():
    qi = jnp.arange(BLOCK)[:, None]
    kj = jnp.arange(3 * BLOCK)[None, :]
    rel = kj - BLOCK - qi
    nb = N_BUCKETS // 2
    max_exact = nb // 2
    ret = jnp.where(rel > 0, nb, 0)
    n = jnp.abs(rel)
    nf = jnp.maximum(n, 1).astype(jnp.float32)
    large = max_exact + (jnp.log(nf / max_exact) / math.log(MAX_DISTANCE / max_exact) * (nb - max_exact)).astype(jnp.int32)
    large = jnp.minimum(large, nb - 1)
    return (ret + jnp.where(n < max_exact, n, large)).astype(jnp.int32)


def bias_table_fwd(table, bucket):
    def body(t_ref, b_ref, o_ref):
        bk = b_ref[...]
        for h in range(A_HEADS):
            acc = jnp.zeros(bk.shape, F32)
            for b in range(N_BUCKETS):
                acc = jnp.where(bk == b, t_ref[b, h], acc)
            o_ref[h] = acc
    return pl.pallas_call(
        body, in_specs=[pl.BlockSpec(memory_space=pltpu.SMEM), pl.BlockSpec(memory_space=pltpu.VMEM)],
        out_specs=pl.BlockSpec(memory_space=pltpu.VMEM),
        out_shape=jax.ShapeDtypeStruct((A_HEADS, BLOCK, 3 * BLOCK), F32), name="bias_table_fwd")(table, bucket)


def bias_table_bwd(dbias_list, bucket):
    n = len(dbias_list)

    def body(*refs):
        b_ref, o_ref = refs[n], refs[n + 1]
        bk = b_ref[...]
        row = lax.broadcasted_iota(jnp.int32, (N_BUCKETS, LANES), 0)
        col = lax.broadcasted_iota(jnp.int32, (N_BUCKETS, LANES), 1)
        out = jnp.zeros((N_BUCKETS, LANES), F32)
        for h in range(A_HEADS):
            d = refs[0][h]
            for r in refs[1:n]:
                d = d + r[h]
            for b in range(N_BUCKETS):
                s = jnp.sum(jnp.where(bk == b, d, 0.0), keepdims=True)
                out = out + jnp.where((row == b) & (col == h), s, 0.0)
        o_ref[...] = out
    return pl.pallas_call(
        body, out_shape=jax.ShapeDtypeStruct((N_BUCKETS, LANES), F32), name="bias_table_bwd",
        compiler_params=_cparams())(*dbias_list, bucket)


def _attn_mask(n, nblk):
    i = lax.broadcasted_iota(jnp.int32, (BLOCK, 3 * BLOCK), 0)
    j = lax.broadcasted_iota(jnp.int32, (BLOCK, 3 * BLOCK), 1)
    kpos = n * BLOCK + j - BLOCK
    return (jnp.abs(j - BLOCK - i) <= WINDOW) & (kpos >= 0) & (kpos < nblk * BLOCK)


def f_attn_block(q, k3, v3, bias, sink, mask):
    kb, vb = k3.astype(BF16), v3.astype(BF16)
    outs = []
    for g in range(A_GROUP):
        qg = q[:, g * A_HEAD_DIM:(g + 1) * A_HEAD_DIM].astype(BF16)
        s = lax.dot_general(qg, kb, (((1,), (1,)), ((), ())), preferred_element_type=F32) * (A_HEAD_DIM ** -0.5)
        s = jnp.where(mask, s + bias[g], NEG_INF)
        sk = sink[g:g + 1, :]
        m = jnp.maximum(jnp.max(s, axis=-1, keepdims=True), sk)
        e = jnp.exp(s - m)
        den = jnp.sum(e, axis=-1, keepdims=True) + jnp.exp(sk - m)
        p = e / den
        outs.append(jnp.dot(p.astype(BF16), vb, preferred_element_type=F32))
    return jnp.concatenate(outs, axis=1)


def _attn_in_specs(nblk):
    qw = A_GROUP * A_HEAD_DIM
    kc, vc = OFF_AK // A_HEAD_DIM, OFF_AV // A_HEAD_DIM
    return [
        pl.BlockSpec((BLOCK, qw), lambda h, n: (n, h)),
        pl.BlockSpec((BLOCK, A_HEAD_DIM), lambda h, n: (jnp.maximum(n - 1, 0), kc + h)),
        pl.BlockSpec((BLOCK, A_HEAD_DIM), lambda h, n: (n, kc + h)),
        pl.BlockSpec((BLOCK, A_HEAD_DIM), lambda h, n: (jnp.minimum(n + 1, nblk - 1), kc + h)),
        pl.BlockSpec((BLOCK, A_HEAD_DIM), lambda h, n: (jnp.maximum(n - 1, 0), vc + h)),
        pl.BlockSpec((BLOCK, A_HEAD_DIM), lambda h, n: (n, vc + h)),
        pl.BlockSpec((BLOCK, A_HEAD_DIM), lambda h, n: (jnp.minimum(n + 1, nblk - 1), vc + h)),
        pl.BlockSpec((A_GROUP, BLOCK, 3 * BLOCK), lambda h, n: (h, 0, 0)),
        pl.BlockSpec((None, A_GROUP, 1), lambda h, n: (h, 0, 0)),
    ]


def attn_fwd(proj, bias, sink, side=None):
    T = proj.shape[0]
    nblk = T // BLOCK

    def body(q_ref, k0, k1, k2, v0, v1, v2, b_ref, s_ref, o_ref):
        n = pl.program_id(1)
        k3 = jnp.concatenate([k0[...], k1[...], k2[...]], axis=0)
        v3 = jnp.concatenate([v0[...], v1[...], v2[...]], axis=0)
        o = f_attn_block(q_ref[...], k3, v3, b_ref[...], s_ref[...], _attn_mask(n, nblk))
        o_ref[...] = o.astype(o_ref.dtype)

    return _pcall(
        body, [proj] * 7 + [bias, sink], side, grid=(A_KV_HEADS, nblk), in_specs=_attn_in_specs(nblk),
        out_specs=pl.BlockSpec((BLOCK, A_GROUP * A_HEAD_DIM), lambda h, n: (n, h)),
        out_shape=jax.ShapeDtypeStruct((T, A_Q), BF16),
        compiler_params=_cparams(("arbitrary", "arbitrary")), name="attn_fwd")[0]


def attn_bwd(proj, bias, sink, dcat, side=None):
    T = proj.shape[0]
    nblk = T // BLOCK
    qw = A_GROUP * A_HEAD_DIM

    def body(q_ref, k0, k1, k2, v0, v1, v2, b_ref, s_ref, do_ref, dq_ref, dk_ref, dv_ref, db_ref, ds_ref, dk_acc, dv_acc):
        n = pl.program_id(1)

        @pl.when(n == 0)
        def _():
            dk_acc[...] = jnp.zeros(dk_acc.shape, F32)
            dv_acc[...] = jnp.zeros(dv_acc.shape, F32)
            db_ref[...] = jnp.zeros(db_ref.shape, F32)
            ds_ref[...] = jnp.zeros(ds_ref.shape, F32)

        k3 = jnp.concatenate([k0[...], k1[...], k2[...]], axis=0)
        v3 = jnp.concatenate([v0[...], v1[...], v2[...]], axis=0)
        mask = _attn_mask(n, nblk)
        _, vjp = jax.vjp(lambda q, k, v, b, s: f_attn_block(q, k, v, b, s, mask), q_ref[...], k3, v3, b_ref[...], s_ref[...])
        dq, dk3, dv3, db, ds = vjp(do_ref[...])
        dq_ref[...] = dq.astype(dq_ref.dtype)
        db_ref[...] += db
        ds_ref[...] += ds
        mid = pl.multiple_of(n * BLOCK, BLOCK)
        dk_acc[pl.ds(mid, BLOCK), :] += dk3[BLOCK:2 * BLOCK]
        dv_acc[pl.ds(mid, BLOCK), :] += dv3[BLOCK:2 * BLOCK]

        @pl.when(n > 0)
        def _():
            lo = pl.multiple_of((n - 1) * BLOCK, BLOCK)
            dk_acc[pl.ds(lo, BLOCK), :] += dk3[0:BLOCK]
            dv_acc[pl.ds(lo, BLOCK), :] += dv3[0:BLOCK]

        @pl.when(n < nblk - 1)
        def _():
            hi = pl.multiple_of((n + 1) * BLOCK, BLOCK)
            dk_acc[pl.ds(hi, BLOCK), :] += dk3[2 * BLOCK:3 * BLOCK]
            dv_acc[pl.ds(hi, BLOCK), :] += dv3[2 * BLOCK:3 * BLOCK]

        @pl.when(n == nblk - 1)
        def _():
            dk_ref[...] = dk_acc[...].astype(dk_ref.dtype)
            dv_ref[...] = dv_acc[...].astype(dv_ref.dtype)

    in_specs = _attn_in_specs(nblk) + [pl.BlockSpec((BLOCK, qw), lambda h, n: (n, h))]
    out_specs = [
        pl.BlockSpec((BLOCK, qw), lambda h, n: (n, h)),
        pl.BlockSpec((T, A_HEAD_DIM), lambda h, n: (0, h)),
        pl.BlockSpec((T, A_HEAD_DIM), lambda h, n: (0, h)),
        pl.BlockSpec((A_GROUP, BLOCK, 3 * BLOCK), lambda h, n: (h, 0, 0)),
        pl.BlockSpec((None, A_GROUP, 1), lambda h, n: (h, 0, 0)),
    ]
    out_shape = [
        jax.ShapeDtypeStruct((T, A_Q), BF16), jax.ShapeDtypeStruct((T, A_KV), BF16), jax.ShapeDtypeStruct((T, A_KV), BF16),
        jax.ShapeDtypeStruct((A_HEADS, BLOCK, 3 * BLOCK), F32), jax.ShapeDtypeStruct((A_KV_HEADS, A_GROUP, 1), F32),
    ]
    return _pcall(
        body, [proj] * 7 + [bias, sink, dcat], side, grid=(A_KV_HEADS, nblk), in_specs=in_specs, out_specs=out_specs,
        out_shape=out_shape, scratch_shapes=[pltpu.VMEM((T, A_HEAD_DIM), F32), pltpu.VMEM((T, A_HEAD_DIM), F32)],
        compiler_params=_cparams(("arbitrary", "arbitrary")), name="attn_bwd")


def f_gla_gate(z, w2f, b2f, w2b, b2b):
    laf = jax.nn.log_sigmoid(jnp.dot(z, w2f, precision=HI, preferred_element_type=F32) + b2f) / GATE_TAU
    lab = jax.nn.log_sigmoid(jnp.dot(z, w2b, precision=HI, preferred_element_type=F32) + b2b) / GATE_TAU
    return laf, lab


def f_gla_post(o, g, gn):
    outs = []
    for h in range(B_HEADS):
        sl = slice(h * B_VAL_DIM, (h + 1) * B_VAL_DIM)
        oh = o[:, sl]
        outs.append(oh * lax.rsqrt(jnp.mean(oh * oh, axis=-1, keepdims=True) + EPS))
    return jnp.concatenate(outs, axis=1) * gn * jax.nn.silu(g)


def _gla_consts(forward):
    C = GLA_CHUNK
    i = lax.broadcasted_iota(jnp.int32, (C, C), 0)
    j = lax.broadcasted_iota(jnp.int32, (C, C), 1)
    if forward:
        return (j <= i).astype(F32), j <= i
    return (j >= i).astype(F32), j > i


def _gla_chunk(q, k, v, la, st, tri, msk, forward):
    C = q.shape[0]
    b = jnp.dot(tri, la, precision=HI, preferred_element_type=F32)
    bl = b[C - 1:C] if forward else b[0:1]
    qe = (q * (B_KEY_DIM ** -0.5)) * jnp.exp(b)
    ke = k * jnp.exp(-b)
    kl = k * jnp.exp(bl - b)
    att = lax.dot_general(qe.astype(BF16), ke.astype(BF16), (((1,), (1,)), ((), ())), preferred_element_type=F32)
    att = jnp.where(msk, att, 0.0)
    o = jnp.dot(att.astype(BF16), v.astype(BF16), preferred_element_type=F32)
    o = o + lax.dot_general(qe.astype(BF16), st.astype(BF16), (((1,), (1,)), ((), ())), preferred_element_type=F32)
    st_new = st * jnp.exp(bl) + lax.dot_general(v.astype(BF16), kl.astype(BF16), (((0,), (0,)), ((), ())),
                                                preferred_element_type=F32)
    return o, st_new


def _gla_state(k, v, la, st, tri, forward):
    C = k.shape[0]
    b = jnp.dot(tri, la, precision=HI, preferred_element_type=F32)
    bl = b[C - 1:C] if forward else b[0:1]
    kl = k * jnp.exp(bl - b)
    return st * jnp.exp(bl) + lax.dot_general(v.astype(BF16), kl.astype(BF16), (((0,), (0,)), ((), ())),
                                              preferred_element_type=F32)


def _gla_specs(T):
    qc, kc, vc = OFF_BQ // LANES, OFF_BK // LANES, OFF_BV // (2 * B_VAL_DIM)
    return [
        pl.BlockSpec((T, LANES), lambda p: (0, qc + p)),
        pl.BlockSpec((T, LANES), lambda p: (0, kc + p)),
        pl.BlockSpec((T, 2 * B_VAL_DIM), lambda p: (0, vc + p)),
        pl.BlockSpec((T, LANES), lambda p: (0, p)),
        pl.BlockSpec((T, LANES), lambda p: (0, p)),
    ]


def _rows(c):
    return pl.ds(pl.multiple_of(c * GLA_CHUNK, GLA_CHUNK), GLA_CHUNK)


def gla_fwd(proj, laf, lab, side=None):
    T = proj.shape[0]
    nc = T // GLA_CHUNK

    def body(q_ref, k_ref, v_ref, laf_ref, lab_ref, o_ref, ob_scr):
        tri_f, msk_f = _gla_consts(True)
        tri_b, msk_b = _gla_consts(False)
        zero = jnp.zeros((B_VAL_DIM, B_KEY_DIM), F32)

        def step(c, carry):
            rf, rb = _rows(c), _rows(nc - 1 - c)
            new = []
            for hh in range(2):
                ks = slice(hh * B_KEY_DIM, (hh + 1) * B_KEY_DIM)
                vs = slice(hh * B_VAL_DIM, (hh + 1) * B_VAL_DIM)
                o, s = _gla_chunk(q_ref[rf, ks], k_ref[rf, ks], v_ref[rf, vs], laf_ref[rf, ks], carry[2 * hh], tri_f, msk_f, True)
                o_ref[rf, vs] = o
                new.append(s)
                o, s = _gla_chunk(q_ref[rb, ks], k_ref[rb, ks], v_ref[rb, vs], lab_ref[rb, ks], carry[2 * hh + 1], tri_b, msk_b, False)
                ob_scr[rb, vs] = o
                new.append(s)
            return tuple(new)

        lax.fori_loop(0, nc, step, (zero,) * 4)
        o_ref[...] += ob_scr[...]

    return _pcall(
        body, [proj, proj, proj, laf, lab], side, grid=(B_HEADS // 2,), in_specs=_gla_specs(T),
        out_specs=pl.BlockSpec((T, 2 * B_VAL_DIM), lambda p: (0, p)),
        out_shape=jax.ShapeDtypeStruct((T, B_V), F32),
        scratch_shapes=[pltpu.VMEM((T, 2 * B_VAL_DIM), F32)],
        compiler_params=_cparams(("arbitrary",)), name="gla_fwd")[0]


def gla_bwd(proj, laf, lab, do, side=None):
    T = proj.shape[0]
    nc = T // GLA_CHUNK
    SROWS = 2 * B_VAL_DIM

    def body(q_ref, k_ref, v_ref, laf_ref, lab_ref, do_ref, dq_ref, dk_ref, dv_ref, dlaf_ref, dlab_ref,
             sf_scr, sb_scr, dq_acc, dk_acc, dv_acc):
        tri_f, msk_f = _gla_consts(True)
        tri_b, msk_b = _gla_consts(False)
        zero = jnp.zeros((B_VAL_DIM, B_KEY_DIM), F32)
        dq_acc[...] = jnp.zeros(dq_acc.shape, F32)
        dk_acc[...] = jnp.zeros(dk_acc.shape, F32)
        dv_acc[...] = jnp.zeros(dv_acc.shape, F32)

        def srow(c, hh):
            return pl.ds(pl.multiple_of(c * SROWS + hh * B_VAL_DIM, B_VAL_DIM), B_VAL_DIM)

        def states(c, carry):
            cf, cb = c, nc - 1 - c
            rf, rb = _rows(cf), _rows(cb)
            new = []
            for hh in range(2):
                ks = slice(hh * B_KEY_DIM, (hh + 1) * B_KEY_DIM)
                vs = slice(hh * B_VAL_DIM, (hh + 1) * B_VAL_DIM)
                sf_scr[srow(cf, hh), :] = carry[2 * hh]
                new.append(_gla_state(k_ref[rf, ks], v_ref[rf, vs], laf_ref[rf, ks], carry[2 * hh], tri_f, True))
                sb_scr[srow(cb, hh), :] = carry[2 * hh + 1]
                new.append(_gla_state(k_ref[rb, ks], v_ref[rb, vs], lab_ref[rb, ks], carry[2 * hh + 1], tri_b, False))
            return tuple(new)

        lax.fori_loop(0, nc, states, (zero,) * 4)

        def back(c, carry):
            cf, cb = nc - 1 - c, c
            rf, rb = _rows(cf), _rows(cb)
            new = []
            for hh in range(2):
                ks = slice(hh * B_KEY_DIM, (hh + 1) * B_KEY_DIM)
                vs = slice(hh * B_VAL_DIM, (hh + 1) * B_VAL_DIM)
                for fwd, r, c_, la_ref, dla_ref, s_scr, g, tri, msk in (
                        (True, rf, cf, laf_ref, dlaf_ref, sf_scr, carry[2 * hh], tri_f, msk_f),
                        (False, rb, cb, lab_ref, dlab_ref, sb_scr, carry[2 * hh + 1], tri_b, msk_b)):
                    _, vjp = jax.vjp(
                        lambda q, k, v, la, st: _gla_chunk(q, k, v, la, st, tri, msk, fwd),
                        q_ref[r, ks], k_ref[r, ks], v_ref[r, vs], la_ref[r, ks], s_scr[srow(c_, hh), :])
                    dq, dk, dv, dla, dst = vjp((do_ref[r, vs], g))
                    dq_acc[r, ks] += dq
                    dk_acc[r, ks] += dk
                    dv_acc[r, vs] += dv
                    dla_ref[r, ks] = dla
                    new.append(dst)
            return tuple(new)

        lax.fori_loop(0, nc, back, (zero,) * 4)
        dq_ref[...] = dq_acc[...].astype(dq_ref.dtype)
        dk_ref[...] = dk_acc[...].astype(dk_ref.dtype)
        dv_ref[...] = dv_acc[...].astype(dv_ref.dtype)

    in_specs = _gla_specs(T) + [pl.BlockSpec((T, 2 * B_VAL_DIM), lambda p: (0, p))]
    out_specs = [
        pl.BlockSpec((T, LANES), lambda p: (0, p)), pl.BlockSpec((T, LANES), lambda p: (0, p)),
        pl.BlockSpec((T, 2 * B_VAL_DIM), lambda p: (0, p)),
        pl.BlockSpec((T, LANES), lambda p: (0, p)), pl.BlockSpec((T, LANES), lambda p: (0, p)),
    ]
    out_shape = [
        jax.ShapeDtypeStruct((T, B_QK), BF16), jax.ShapeDtypeStruct((T, B_QK), BF16), jax.ShapeDtypeStruct((T, B_V), BF16),
        jax.ShapeDtypeStruct((T, B_QK), F32), jax.ShapeDtypeStruct((T, B_QK), F32),
    ]
    scratch = [
        pltpu.VMEM((nc * SROWS, B_KEY_DIM), F32), pltpu.VMEM((nc * SROWS, B_KEY_DIM), F32),
        pltpu.VMEM((T, LANES), F32), pltpu.VMEM((T, LANES), F32), pltpu.VMEM((T, 2 * B_VAL_DIM), F32),
    ]
    return _pcall(
        body, [proj, proj, proj, laf, lab, do], side, grid=(B_HEADS // 2,), in_specs=in_specs, out_specs=out_specs,
        out_shape=out_shape, scratch_shapes=scratch, compiler_params=_cparams(("arbitrary",)), name="gla_bwd")


def _shift_raw(x, k):
    T = x.shape[0]
    r = lax.broadcasted_iota(jnp.int32, x.shape, 0)
    if k > 0:
        return jnp.where(r >= k, pltpu.roll(x, k, 0), 0.0)
    return jnp.where(r < T + k, pltpu.roll(x, T + k, 0), 0.0)


@functools.partial(jax.custom_vjp, nondiff_argnums=(1,))
def _shift(x, k):
    return _shift_raw(x, k)


_shift.defvjp(lambda x, k: (_shift_raw(x, k), None), lambda k, _, g: (_shift_raw(g, -k),))


def _scan_raw(a, u, reverse):
    T = a.shape[0]
    d = 1
    while d < T:
        k = -d if reverse else d
        u = a * _shift_raw(u, k) + u
        a = a * _shift_raw(a, k)
        d *= 2
    return u


@functools.partial(jax.custom_vjp, nondiff_argnums=(2,))
def _scan(a, u, reverse):
    return _scan_raw(a, u, reverse)


def _scan_f(a, u, reverse):
    h = _scan_raw(a, u, reverse)
    return h, (a, h)


def _scan_b(reverse, res, dh):
    a, h = res
    k = 1 if reverse else -1
    du = _scan_raw(_shift_raw(a, k), dh, not reverse)
    return du * _shift_raw(h, -k), du


_scan.defvjp(_scan_f, _scan_b)


def f_lru(cx, cy, cw, cb, wa, ba, wx, bx, lam, diff):
    shift, scan = (_shift, _scan) if diff else (_shift_raw, _scan_raw)
    xc = cx * cw[CONV_LEFT:CONV_LEFT + 1]
    for j in range(CONV_WIDTH):
        if j != CONV_LEFT:
            xc = xc + shift(cx, CONV_LEFT - j) * cw[j:j + 1]
    xc = xc + cb
    xb = xc.astype(BF16)
    h = None
    for s in range(2):
        r = jax.nn.sigmoid(jnp.dot(xb, wa[s].astype(BF16), preferred_element_type=F32) + ba[s:s + 1])
        i = jax.nn.sigmoid(jnp.dot(xb, wx[s].astype(BF16), preferred_element_type=F32) + bx[s:s + 1])
        log_a = -LRU_C * r * jax.nn.softplus(-lam[s:s + 1])
        a = jnp.exp(log_a)
        one_minus_a2 = -jnp.tanh(log_a) * (a * a + 1.0)
        u = jnp.sqrt(one_minus_a2) * (i * xc)
        hs = scan(a, u, s == 1)
        h = hs if h is None else h + hs
    return h * jax.nn.gelu(cy)


def _lru_specs(T):
    xc, yc = OFF_CX // LANES, OFF_CY // LANES
    return [
        pl.BlockSpec((T, LANES), lambda b: (0, xc + b)),
        pl.BlockSpec((T, LANES), lambda b: (0, yc + b)),
        pl.BlockSpec((CONV_WIDTH, LANES), lambda b: (0, b)),
        pl.BlockSpec((1, LANES), lambda b: (0, b)),
        pl.BlockSpec((2, None, C_BLOCK_DIM, C_BLOCK_DIM), lambda b: (0, b, 0, 0)),
        pl.BlockSpec((2, LANES), lambda b: (0, b)),
        pl.BlockSpec((2, None, C_BLOCK_DIM, C_BLOCK_DIM), lambda b: (0, b, 0, 0)),
        pl.BlockSpec((2, LANES), lambda b: (0, b)),
        pl.BlockSpec((2, LANES), lambda b: (0, b)),
    ]


def lru_fwd(proj, cw, cb, wa, ba, wx, bx, lam, side=None):
    T = proj.shape[0]

    def body(cx, cy, cw_r, cb_r, wa_r, ba_r, wx_r, bx_r, lam_r, o_ref):
        o = f_lru(cx[...], cy[...], cw_r[...], cb_r[...], wa_r[...], ba_r[...], wx_r[...], bx_r[...], lam_r[...], False)
        o_ref[...] = o.astype(o_ref.dtype)

    return _pcall(
        body, [proj, proj, cw, cb, wa, ba, wx, bx, lam], side, grid=(C_BLOCKS,), in_specs=_lru_specs(T),
        out_specs=pl.BlockSpec((T, LANES), lambda b: (0, b)), out_shape=jax.ShapeDtypeStruct((T, C_WIDTH), BF16),
        compiler_params=_cparams(("arbitrary",)), name="lru_fwd")[0]


def lru_bwd(proj, cw, cb, wa, ba, wx, bx, lam, dcat, side=None):
    T = proj.shape[0]
    oc = (A_Q + B_V) // LANES

    def body(cx, cy, cw_r, cb_r, wa_r, ba_r, wx_r, bx_r, lam_r, do_ref, *outs):
        _, vjp = jax.vjp(functools.partial(f_lru, diff=True), cx[...], cy[...], cw_r[...], cb_r[...], wa_r[...],
                         ba_r[...], wx_r[...], bx_r[...], lam_r[...])
        grads = vjp(do_ref[...])
        for o, g in zip(outs, grads):
            o[...] = g.astype(o.dtype)

    specs = _lru_specs(T)
    out_specs = [pl.BlockSpec((T, LANES), lambda b: (0, b)), pl.BlockSpec((T, LANES), lambda b: (0, b))] + specs[2:]
    out_shape = [jax.ShapeDtypeStruct((T, C_WIDTH), BF16), jax.ShapeDtypeStruct((T, C_WIDTH), BF16)] + \
                [jax.ShapeDtypeStruct(p.shape, F32) for p in (cw, cb, wa, ba, wx, bx, lam)]
    return _pcall(
        body, [proj, proj, cw, cb, wa, ba, wx, bx, lam, dcat], side, grid=(C_BLOCKS,),
        in_specs=specs + [pl.BlockSpec((T, LANES), lambda b: (0, oc + b))], out_specs=out_specs, out_shape=out_shape,
        compiler_params=_cparams(("arbitrary",)), name="lru_bwd")


def all_gather8(name, blk):
    def body(x_ref, out_ref, send_sems, recv_sems):
        x, y, c, _, others = _place()
        sibling = (x, y, 1 - c)

        def slab(px, py, pc):
            return out_ref.at[4 * px + 2 * py + pc]

        first = [_remote(x_ref, slab(x, y, c), send_sems, recv_sems, 0, sibling)]
        first += [_remote(x_ref, slab(x, y, c), send_sems, recv_sems, 1 + j, (*ch, c)) for j, ch in enumerate(others)]
        for cp in first:
            cp.start()
        passed = [_remote(slab(*ch, c), slab(*ch, c), send_sems, recv_sems, 4 + j, sibling) for j, ch in enumerate(others)]
        for j, ch in enumerate(others):
            _remote(x_ref, slab(*ch, c), send_sems, recv_sems, 1 + j, (x, y, c)).wait_recv()
            passed[j].start()
        _remote(x_ref, slab(x, y, 1 - c), send_sems, recv_sems, 0, (x, y, c)).wait_recv()
        for j, ch in enumerate(others):
            _remote(x_ref, slab(*ch, 1 - c), send_sems, recv_sems, 4 + j, (x, y, c)).wait_recv()
        for cp in first + passed:
            cp.wait_send()

    out = pl.pallas_call(
        body, out_shape=jax.ShapeDtypeStruct((8,) + blk.shape, blk.dtype), in_specs=[HBM_SPEC], out_specs=HBM_SPEC,
        scratch_shapes=[pltpu.SemaphoreType.DMA((7,)), pltpu.SemaphoreType.DMA((7,))], name=name)(blk)
    me = 4 * lax.axis_index("x") + 2 * lax.axis_index("y") + lax.axis_index("c")
    return lax.dynamic_update_index_in_dim(out, blk, me, 0)


def chip_gather(name, shards):
    exchange = RidingExchange("gather", shards, [s.shape[1] for s in shards])
    now, _, _ = exchange.step(len(exchange.pieces))
    _pcall(lambda: None, [], _StagedSide([(exchange, now, now, now)]), in_specs=[], out_specs=[], out_shape=[], name=name)
    chip = 2 * lax.axis_index("x") + lax.axis_index("y")
    return [lax.dynamic_update_index_in_dim(o, s, chip, 0) for o, s in zip(exchange.bufs, shards)]


def sum_slabs(name, r, out_dtype, tr):
    S, R, W = r.shape

    def body(*refs):
        t = refs[0][...].astype(F32)
        for s in range(1, S):
            t = t + refs[s][...].astype(F32)
        refs[S][...] = t.astype(out_dtype)

    return pl.pallas_call(
        body, grid=(R // tr,), in_specs=[pl.BlockSpec((None, tr, W), lambda i, s=s: (s, i, 0)) for s in range(S)],
        out_specs=pl.BlockSpec((tr, W), lambda i: (i, 0)), out_shape=jax.ShapeDtypeStruct((R, W), out_dtype),
        compiler_params=_cparams(("parallel",)), name=name)(*([r] * S))


def sum_chips(name, arrived, own, chip, tr):
    S, R, W = arrived.shape

    def body(chip_ref, own_ref, *refs):
        me = chip_ref[0]
        t = None
        for s in range(S):
            term = jnp.where(me == s, own_ref[...].astype(F32), refs[s][...].astype(F32))
            t = term if t is None else t + term
        refs[S][...] = t

    grid_spec = pltpu.PrefetchScalarGridSpec(
        num_scalar_prefetch=1, grid=(R // tr,),
        in_specs=[pl.BlockSpec((None, tr, W), lambda i, ch: (ch[0], i, 0))] +
                 [pl.BlockSpec((None, tr, W), lambda i, ch, s=s: (s, i, 0)) for s in range(S)],
        out_specs=pl.BlockSpec((tr, W), lambda i, ch: (i, 0)))
    return pl.pallas_call(body, grid_spec=grid_spec, out_shape=jax.ShapeDtypeStruct((R, W), F32),
                          compiler_params=_cparams(("parallel",)), name=name)(
                              chip.reshape(1).astype(jnp.int32), own, *([arrived] * S))


def add_kept_half(name, halves, got, c, tr):
    S, _, R, W = halves.shape

    def body(c_ref, h_ref, g_ref, o_ref):
        o_ref[...] = (h_ref[...].astype(F32) + g_ref[...].astype(F32)).astype(o_ref.dtype)

    grid_spec = pltpu.PrefetchScalarGridSpec(
        num_scalar_prefetch=1, grid=(S, R // tr),
        in_specs=[pl.BlockSpec((None, None, tr, W), lambda s, i, c_ref: (s, c_ref[0], i, 0)),
                  pl.BlockSpec((None, tr, W), lambda s, i, c_ref: (s, i, 0))],
        out_specs=pl.BlockSpec((None, tr, W), lambda s, i, c_ref: (s, i, 0)))
    return pl.pallas_call(body, grid_spec=grid_spec, out_shape=jax.ShapeDtypeStruct((S, R, W), halves.dtype),
                          compiler_params=_cparams(("parallel", "parallel")), name=name)(
                              c.reshape(1).astype(jnp.int32), halves, got)


def adamw_layer(name, l, g, row_off, w, m, v, prev, tr, tc=None, side=None):
    L, R, C = w.shape
    tc = C if tc is None else tc
    off = row_off // tr

    def body(g_ref, w_ref, m_ref, v_ref, *rest):
        outs = rest[-4:]
        gv = g_ref[...]
        d, mn, vn = f_adamw(w_ref[...], gv, m_ref[...], v_ref[...])
        for o, val in zip(outs, (gv, d, mn, vn)):
            o[...] = val

    slab = pl.BlockSpec((None, tr, tc), lambda i, j: (l, i, j))
    in_specs = [pl.BlockSpec((tr, tc), lambda i, j: (off + i, j)), slab, slab, slab]
    args = [g, w, m, v]
    aliases = {}
    if prev is not None:
        in_specs += [pl.BlockSpec(memory_space=pl.ANY)] * 4
        args += list(prev)
        aliases = {4 + k: k for k in range(4)}
    return _pcall(
        body, args, side, grid=(R // tr, C // tc), in_specs=in_specs, out_specs=[slab] * 4,
        out_shape=[jax.ShapeDtypeStruct((L, R, C), F32)] * 4, input_output_aliases=aliases,
        compiler_params=_cparams(("parallel", "parallel")), name=name)


BIG = ("w_in", "w_out", "xq", "xk", "xv", "xo", "w_up", "w_down")
GROUPS = {"mix": ("w_out", "xq", "xk", "xv", "xo"), "ff": ("w_up", "w_down")}
PACK_ROWS = {"w_out": 512, "xq": 512, "xk": 512, "xv": 512, "xo": 512, "w_up": 2048, "w_down": 2048}
GROUP_ROWS = {g: sum(PACK_ROWS[n] for n in names) for g, names in GROUPS.items()}
SUM_TILE = 256
PACK_OFF = {}
for _names in GROUPS.values():
    _o = 0
    for _n in _names:
        PACK_OFF[_n] = _o
        _o += PACK_ROWS[_n]

_SPLIT_OFF = np.cumsum((0,) + SPLIT_SIZES)
_KORDER = (0, 1, 2, 3, 4, 5, 6, 9, 10, 7, 8)


def w_in_to_kernel_cols(w):
    parts = [w[..., _SPLIT_OFF[i]:_SPLIT_OFF[i + 1]] for i in _KORDER]
    parts.append(jnp.zeros(w.shape[:-1] + (D_INP - D_IN,), w.dtype))
    return jnp.concatenate(parts, axis=-1)


def w_in_from_kernel_cols(w):
    offs = np.cumsum((0,) + tuple(SPLIT_SIZES[i] for i in _KORDER))
    pos = {k: (offs[n], offs[n + 1]) for n, k in enumerate(_KORDER)}
    return jnp.concatenate([w[..., pos[i][0]:pos[i][1]] for i in range(len(SPLIT_SIZES))], axis=-1)


def pack_shards(shards, group, dtype):
    return jnp.concatenate([shards[n].astype(dtype) for n in GROUPS[group]], axis=-2)


def unpack_rows(packed, name):
    return packed[..., PACK_OFF[name]:PACK_OFF[name] + PACK_ROWS[name], :]


WEIGHTS = ("rel_bias", "w_in", "w_out", "attn_sink", "gla_w2_f", "gla_b2_f", "gla_w2_b", "gla_b2_b", "gla_norm", "conv_w",
           "conv_b", "lru_wa", "lru_ba", "lru_wx", "lru_bx", "lru_lambda", "xq", "xk", "xv", "xo", "w_up", "w_down",
           "norm_mix_pre", "norm_mix_post", "norm_mem", "norm_x_pre", "norm_x_post", "norm_ff_pre", "norm_ff_post")
SMALL = tuple(n for n in WEIGHTS if n not in BIG)
SMALL_SHARDED = ("gla_w2_f", "gla_w2_b", "conv_w", "lru_ba", "lru_bx", "lru_lambda")
ROW_TILE = 256
SMALL_TILE = 512
W_IN_TILE = (344, 1024)
RIDE_PIECE_ROWS = (256, 512)


def _small_rows(n):
    return -(-n // (SUBLANES * LANES)) * SUBLANES


def _as_rows(a2):
    L, n = a2.shape
    rows = _small_rows(n)
    if rows * LANES != n:
        a2 = jnp.pad(a2, ((0, 0), (0, rows * LANES - n)))
    return a2.reshape(L * rows, LANES)


def _pack_small(items, layered):
    parts = []
    for it, lay in zip(items, layered):
        if isinstance(it, (list, tuple)):
            parts += [_as_rows(e.astype(F32).reshape(1, -1)) for e in it]
        else:
            parts.append(_as_rows(it.astype(F32).reshape(it.shape[0] if lay else 1, -1)))
    pad = -sum(p.shape[0] for p in parts) % SMALL_TILE
    if pad:
        parts.append(jnp.zeros((pad, LANES), F32))
    return jnp.concatenate(parts, axis=0)


def _unpack_small(buf, shapes, layered):
    lead = buf.shape[:-2]
    out, o = [], 0
    for s, lay in zip(shapes, layered):
        L = s[0] if lay else 1
        n = int(np.prod(s)) // L
        rows = _small_rows(n)
        part = buf[..., o:o + L * rows, :]
        if n != rows * LANES:
            part = part.reshape(lead + (L, rows * LANES))[..., :n]
        out.append(part.reshape(lead + tuple(s)))
        o += L * rows
    return out


def _relu2(r):
    return r, jnp.square(jnp.maximum(r, 0.0))


def _drelu2(r, u):
    return r * (2.0 * jnp.maximum(u, 0.0))


def kernel(x, mem, rel_bias, w_in, w_out, attn_sink, gla_w2_f, gla_b2_f, gla_w2_b, gla_b2_b, gla_norm, conv_w, conv_b, lru_wa, lru_ba, lru_wx, lru_bx, lru_lambda, xq, xk, xv, xo, w_up, w_down, norm_mix_pre, norm_mix_post, norm_mem, norm_x_pre, norm_x_post, norm_ff_pre, norm_ff_post, loss_target, m_rel_bias, m_w_in, m_w_out, m_attn_sink, m_gla_w2_f, m_gla_b2_f, m_gla_w2_b, m_gla_b2_b, m_gla_norm, m_conv_w, m_conv_b, m_lru_wa, m_lru_ba, m_lru_wx, m_lru_bx, m_lru_lambda, m_xq, m_xk, m_xv, m_xo, m_w_up, m_w_down, m_norm_mix_pre, m_norm_mix_post, m_norm_mem, m_norm_x_pre, m_norm_x_post, m_norm_ff_pre, m_norm_ff_post, v_rel_bias, v_w_in, v_w_out, v_attn_sink, v_gla_w2_f, v_gla_b2_f, v_gla_w2_b, v_gla_b2_b, v_gla_norm, v_conv_w, v_conv_b, v_lru_wa, v_lru_ba, v_lru_wx, v_lru_bx, v_lru_lambda, v_xq, v_xk, v_xv, v_xo, v_w_up, v_w_down, v_norm_mix_pre, v_norm_mix_post, v_norm_mem, v_norm_x_pre, v_norm_x_post, v_norm_ff_pre, v_norm_ff_post):
    w_args = (rel_bias, w_in, w_out, attn_sink, gla_w2_f, gla_b2_f, gla_w2_b, gla_b2_b, gla_norm, conv_w, conv_b, lru_wa,
              lru_ba, lru_wx, lru_bx, lru_lambda, xq, xk, xv, xo, w_up, w_down, norm_mix_pre, norm_mix_post, norm_mem,
              norm_x_pre, norm_x_post, norm_ff_pre, norm_ff_post)
    m_args = (m_rel_bias, m_w_in, m_w_out, m_attn_sink, m_gla_w2_f, m_gla_b2_f, m_gla_w2_b, m_gla_b2_b, m_gla_norm, m_conv_w,
              m_conv_b, m_lru_wa, m_lru_ba, m_lru_wx, m_lru_bx, m_lru_lambda, m_xq, m_xk, m_xv, m_xo, m_w_up, m_w_down,
              m_norm_mix_pre, m_norm_mix_post, m_norm_mem, m_norm_x_pre, m_norm_x_post, m_norm_ff_pre, m_norm_ff_post)
    v_args = (v_rel_bias, v_w_in, v_w_out, v_attn_sink, v_gla_w2_f, v_gla_b2_f, v_gla_w2_b, v_gla_b2_b, v_gla_norm, v_conv_w,
              v_conv_b, v_lru_wa, v_lru_ba, v_lru_wx, v_lru_bx, v_lru_lambda, v_xq, v_xk, v_xv, v_xo, v_w_up, v_w_down,
              v_norm_mix_pre, v_norm_mix_post, v_norm_mem, v_norm_x_pre, v_norm_x_post, v_norm_ff_pre, v_norm_ff_post)
    Wt, Mo, Vo = dict(zip(WEIGHTS, w_args)), dict(zip(WEIGHTS, m_args)), dict(zip(WEIGHTS, v_args))
    x, mem, tgt = x[0], mem[0], loss_target[0]
    D = D_MODEL
    depth = w_in.shape[0]
    chip = 2 * lax.axis_index("x") + lax.axis_index("y")
    core = lax.axis_index("c")

    sm_shapes = [Wt[n].shape for n in SMALL_SHARDED]
    yes = [True] * len(SMALL_SHARDED)
    g8 = all_gather8("gather_small_weights", _pack_small([Wt[n] for n in SMALL_SHARDED], yes))
    per_chip = _unpack_small(g8[0::2], sm_shapes, yes)
    whole = {n: jnp.concatenate([p[j] for j in range(4)], axis=-1) for n, p in zip(SMALL_SHARDED, per_chip)}

    def group_shards(l, group):
        shard = pack_shards({n: Wt[n][l] for n in GROUPS[group]}, group, BF16).reshape(2, GROUP_ROWS[group] // 2, D)
        return [shard] + ([w_in[l].astype(BF16).reshape(2, D // 2, D_IN // 4)] if group == "mix" else [])

    def whole_weights(group, gathered):
        g = gathered[0].reshape(4, GROUP_ROWS[group], D)
        if group == "ff":
            return {"w_up": unpack_rows(g, "w_up"), "w_down": unpack_rows(g, "w_down").reshape(D_FF, D)}
        W = {n: unpack_rows(g, n).reshape(D, D) for n in GROUPS["mix"]}
        win = gathered[1].reshape(4, D, D_IN // 4).transpose(1, 0, 2).reshape(D, D_IN)
        W["w_in"] = w_in_to_kernel_cols(win)
        return W

    first = group_shards(0, "mix")
    Wgot = {(0, "mix"): whole_weights("mix", chip_gather("gather_first_weights", first))}
    gathers = Stream()
    riding = {}
    for l in range(depth):
        for group in ("mix", "ff"):
            if (l, group) != (0, "mix"):
                shards = group_shards(l, group)
                riding[l, group] = (shards, gathers.add(RidingExchange("gather", shards, RIDE_PIECE_ROWS)))

    def need_weights(l, group):
        if (l, group) not in Wgot:
            shards, exchange = riding.pop((l, group))
            got = gathers.finish(exchange, "gather_rest")
            Wgot[l, group] = whole_weights(
                group, [lax.dynamic_update_index_in_dim(b, s, chip, 0) for b, s in zip(got, shards)])
        return Wgot[l, group]

    bucket = t5_bucket_map()
    bias = bias_table_fwd(rel_bias, bucket)

    def gain(name, l):
        return Wt[name][l][None]

    def layer_params(l):
        w2fp = jnp.zeros((LANES, B_QK), F32).at[0:GATE_RANK].set(whole["gla_w2_f"][l])
        w2bp = jnp.zeros((LANES, B_QK), F32).at[GATE_RANK:2 * GATE_RANK].set(whole["gla_w2_b"][l])
        gate = [w2fp, gla_b2_f[l][None], w2bp, gla_b2_b[l][None]]
        lru = [whole["conv_w"][l], conv_b[l][None], lru_wa[l], whole["lru_ba"][l], lru_wx[l], whole["lru_bx"][l],
               whole["lru_lambda"][l]]
        return attn_sink[l].reshape(A_KV_HEADS, A_GROUP, 1), gate, gla_norm[l][None], lru

    saved = []
    xcur = x
    (h1,) = rowmap("norm_first", f_norm, [x], [gain("norm_mix_pre", 0)], [(D, BF16)], ROW_TILE)
    loss_acc = None
    Wfull = []
    ride = gathers.take
    for l in range(depth):
        W = dict(need_weights(l, "mix"))
        sink3, gate, gn, lru = layer_params(l)
        (proj,) = mm("mm_in", h1, W["w_in"], "nn", [(F32, 1)], pm=512, pn=1408, side=ride(2))
        oa = attn_fwd(proj, bias, sink3, side=ride(3))
        zrow, grow = (proj, LANES, OFF_Z // LANES), (proj, B_V, OFF_BG // B_V)
        laf, lab = rowmap("gla_gate", f_gla_gate, [zrow], gate, [(B_QK, F32), (B_QK, F32)], ROW_TILE)
        oraw = gla_fwd(proj, laf, lab, side=ride(2))
        (ob,) = rowmap("gla_post", f_gla_post, [oraw, grow], [gn], [(B_V, BF16)], ROW_TILE)
        oc = lru_fwd(proj, *lru, side=ride(1))
        cat = jnp.concatenate([oa, ob, oc], axis=1)
        (mixed,) = mm("mm_out", cat, W["w_out"], "nn", [(F32, 1)], side=ride(1))
        x1, h2 = rowmap("resnorm_mix", f_resnorm, [xcur, mixed], [gain("norm_mix_post", l), gain("norm_x_pre", l)],
                        [(D, F32), (D, BF16)], ROW_TILE)
        (memn,) = rowmap("norm_mem", f_norm, [mem], [gain("norm_mem", l)], [(D, BF16)], ROW_TILE)
        (q,) = mm("mm_xq", h2, W["xq"], "nn", [(BF16, 1)], side=ride(1))
        (k,) = mm("mm_xk", memn, W["xk"], "nn", [(F32, 1)])
        (v,) = mm("mm_xv", memn, W["xv"], "nn", [(F32, 1)])
        (o,) = rowmap("xattn", f_xattn, [q], [k, v], [(D, BF16)], ROW_TILE)
        (xo_out,) = mm("mm_xo", o, W["xo"], "nn", [(F32, 1)], side=ride(1))
        x2, h3 = rowmap("resnorm_x", f_resnorm, [x1, xo_out], [gain("norm_x_post", l), gain("norm_ff_pre", l)],
                        [(D, F32), (D, BF16)], ROW_TILE)
        W.update(need_weights(l, "ff"))
        Wfull.append(W)
        u, act = mm("mm_up", h3, W["w_up"], "nn", [(F32, 1), (BF16, 1)], epilogue=_relu2, side=ride(3))
        (ff,) = mm("mm_down", act, W["w_down"], "nn", [(F32, 1)], side=ride(2))
        saved.append(dict(x0=xcur, h1=h1, proj=proj, laf=laf, lab=lab, oraw=oraw, cat=cat, mixed=mixed, x1=x1, h2=h2,
                          memn=memn, q=q, k=k, v=v, o=o, xo_out=xo_out, x2=x2, h3=h3, u=u, act=act, ff=ff))
        if l < depth - 1:
            xcur, h1 = rowmap("resnorm_ff", f_resnorm, [x2, ff], [gain("norm_ff_post", l), gain("norm_mix_pre", l + 1)],
                              [(D, F32), (D, BF16)], ROW_TILE)
        else:
            (loss_acc,) = rowmap("final_loss", f_final_loss, [x2, ff, tgt], [gain("norm_ff_post", l)], [], ROW_TILE,
                                 accs=[(1, 1)])
    loss = lax.psum(loss_acc[0, 0], ("x", "y", "c"))

    small_g = {n: [None] * depth for n in SMALL if n != "rel_bias"}
    adam = {}
    dbias_all = []
    dx_next = dh1_next = None
    grad_x = None
    scatters = Stream()
    ride = scatters.take
    taking = []
    inflight = []
    sharing = []

    def start_reduce(lyr, group, dW):
        if group == "ff":
            pack = dW["ff"]
        else:
            pack = pack_shards({n: dW[n].reshape(4, D // 4, D) for n in GROUPS["mix"]}, group, BF16)
        halves = [pack.reshape(4, 2, GROUP_ROWS[group] // 2, D)]
        if group == "mix":
            g_in = w_in_from_kernel_cols(dW["w_in"]).reshape(D, 4, D_IN // 4).transpose(1, 0, 2)
            halves.append(g_in.reshape(4, 2, D // 2, D_IN // 4))
        taking.append((lyr, group, halves, scatters.add(RidingExchange("take", halves, RIDE_PIECE_ROWS))))

    def continue_reduce():
        lyr, group, halves, exchange = taking.pop()
        got = scatters.finish(exchange, "reduce_to_half_owner")
        sums = [add_kept_half("reduce_chip_sum", h, g, core, SUM_TILE) for h, g in zip(halves, got)]
        inflight.append((lyr, group, sums, scatters.add(RidingExchange("scatter", sums, RIDE_PIECE_ROWS))))

    def finish_reduce():
        lyr, group, sums, exchange = inflight.pop(0)
        arrived = scatters.finish(exchange, "reduce_rest")
        totals = [sum_chips("reduce_sum_chips", a, s, chip, SUM_TILE) for a, s in zip(arrived, sums)]
        sharing.append((lyr, group, totals, scatters.add(RidingExchange("pair", totals, RIDE_PIECE_ROWS))))

    w_in_t = [jnp.swapaxes(a, 1, 2) for a in (w_in, m_w_in, v_w_in)]

    def run_updates():
        for lyr, group, totals, exchange in sharing:
            both = scatters.finish(exchange, "reduce_share_halves")
            fulls = [lax.dynamic_update_index_in_dim(b, t, core, 0) for b, t in zip(both, totals)]
            full = fulls[0].reshape(GROUP_ROWS[group], D)
            for n in GROUPS[group]:
                adam[n] = adamw_layer("adamw_" + n, lyr, full, PACK_OFF[n], Wt[n], Mo[n], Vo[n], adam.get(n), SUM_TILE)
            if group == "mix":
                g = fulls[1].reshape(D, D_IN // 4).T
                adam["w_in"] = adamw_layer("adamw_w_in", lyr, g, 0, *w_in_t, adam.get("w_in"), *W_IN_TILE)

    for l in reversed(range(depth)):
        W, S = Wfull[l], saved[l]
        sink3, gate, gn, lru = layer_params(l)
        if l == depth - 1:
            (dx2, dff), (dgp,) = rowmap_bwd("final_bwd", f_final_rows, [S["x2"], S["ff"], tgt], [gain("norm_ff_post", l)],
                                            [None], ROW_TILE, [F32, F32, None], [True])
        else:
            (dx2, dff), (dgp, dgn_next) = rowmap_bwd(
                "resnorm_ff_bwd", f_resnorm, [S["x2"], S["ff"]], [gain("norm_ff_post", l), gain("norm_mix_pre", l + 1)],
                [dx_next, dh1_next], ROW_TILE, [F32, F32], [True, True])
            small_g["norm_mix_pre"][l + 1] = dgn_next[0]
        small_g["norm_ff_post"][l] = dgp[0]
        dW = {}
        (du,) = mm("mm_down_bwd", dff, W["w_down"], "nt", [(BF16, 1)], epilogue=_drelu2, extras=[S["u"]], side=ride(2))
        (pack,) = mm("mm_down_wgrad", S["act"], dff, "tn",
                     [(BF16, ("rows", GROUP_ROWS["ff"], PACK_OFF["w_down"], None))], side=ride(2))
        (dW["ff"],) = mm("mm_up_wgrad", S["h3"], du, "tn",
                         [(BF16, ("cols", GROUP_ROWS["ff"], PACK_OFF["w_up"], pack))], side=ride(2))
        start_reduce(l, "ff", dW)
        (dh3,) = mm("mm_up_bwd", du, W["w_up"], "nt", [(F32, 1)], pk=D, side=ride(2))
        continue_reduce()
        if len(inflight) > 1:
            finish_reduce()
        (dx1, dxo_out), (dg1, dg2) = rowmap_bwd(
            "resnorm_x_bwd", f_resnorm, [S["x1"], S["xo_out"]], [gain("norm_x_post", l), gain("norm_ff_pre", l)],
            [dx2, dh3], ROW_TILE, [F32, F32], [True, True])
        small_g["norm_x_post"][l], small_g["norm_ff_pre"][l] = dg1[0], dg2[0]
        (do,) = mm("mm_xo_bwd", dxo_out, W["xo"], "nt", [(F32, 1)])
        (dW["xo"],) = mm("mm_xo_wgrad", S["o"], dxo_out, "tn", [(BF16, 1)])
        (dq,), (dk, dv) = rowmap_bwd("xattn_bwd", f_xattn, [S["q"]], [S["k"], S["v"]], [do], ROW_TILE, [BF16], [True, True])
        (dW["xq"],) = mm("mm_xq_wgrad", S["h2"], dq, "tn", [(BF16, 1)])
        (dh2,) = mm("mm_xq_bwd", dq, W["xq"], "nt", [(F32, 1)])
        (dW["xk"],) = mm("mm_xk_wgrad", S["memn"], dk, "tn", [(BF16, 1)])
        (dW["xv"],) = mm("mm_xv_wgrad", S["memn"], dv, "tn", [(BF16, 1)])
        (dmk,) = mm("mm_xk_bwd", dk, W["xk"], "nt", [(F32, 1)])
        (dmv,) = mm("mm_xv_bwd", dv, W["xv"], "nt", [(F32, 1)])
        _, (dgm,) = rowmap_bwd("norm_mem_bwd", f_norm_twice, [mem], [gain("norm_mem", l)], [dmk, dmv], ROW_TILE, [None], [True])
        small_g["norm_mem"][l] = dgm[0]
        (dx0, dmixed), (dg1, dg2) = rowmap_bwd(
            "resnorm_mix_bwd", f_resnorm, [S["x0"], S["mixed"]], [gain("norm_mix_post", l), gain("norm_x_pre", l)],
            [dx1, dh2], ROW_TILE, [F32, F32], [True, True])
        small_g["norm_mix_post"][l], small_g["norm_x_pre"][l] = dg1[0], dg2[0]
        (dcat,) = mm("mm_out_bwd", dmixed, W["w_out"], "nt", [(F32, 1)])
        (dW["w_out"],) = mm("mm_out_wgrad", S["cat"], dmixed, "tn", [(BF16, 1)])
        proj = S["proj"]
        daq, dak, dav, dbias, dsink = attn_bwd(proj, bias, sink3, dcat, side=ride(2))
        dbias_all.append(dbias)
        small_g["attn_sink"][l] = dsink.reshape(A_HEADS)
        zrow, grow = (proj, LANES, OFF_Z // LANES), (proj, B_V, OFF_BG // B_V)
        (doraw, dbg), (dgn,) = rowmap_bwd("gla_post_bwd", f_gla_post, [S["oraw"], grow], [gn], [(dcat, B_V, A_Q // B_V)],
                                          ROW_TILE, [F32, BF16], [True])
        dbq, dbk, dbv, dlaf, dlab = gla_bwd(proj, S["laf"], S["lab"], doraw, side=ride(3))
        (dz,), (dw2fp, db2f, dw2bp, db2b) = rowmap_bwd("gla_gate_bwd", f_gla_gate, [zrow], gate, [dlaf, dlab], ROW_TILE,
                                                        [BF16], [True] * 4)
        small_g["gla_norm"][l] = dgn[0]
        small_g["gla_w2_f"][l], small_g["gla_b2_f"][l] = dw2fp[0:GATE_RANK], db2f[0]
        small_g["gla_w2_b"][l], small_g["gla_b2_b"][l] = dw2bp[GATE_RANK:2 * GATE_RANK], db2b[0]
        dcx, dcy, dcw, dcb, dwa, dba, dwx, dbx, dlam = lru_bwd(proj, *lru, dcat, side=ride(1))
        small_g["conv_w"][l], small_g["conv_b"][l] = dcw, dcb[0]
        small_g["lru_wa"][l], small_g["lru_ba"][l], small_g["lru_wx"][l] = dwa, dba, dwx
        small_g["lru_bx"][l], small_g["lru_lambda"][l] = dbx, dlam
        dproj = jnp.concatenate([daq, dak, dav, dbq, dbk, dbv, dbg, dcx, dcy, dz], axis=1)
        (dW["w_in"],) = mm("mm_in_wgrad", S["h1"], dproj, "tn", [(BF16, 1)], pm=512, pn=1408, side=ride(1))
        (dh1,) = mm("mm_in_bwd", dproj, W["w_in"], "nt", [(F32, 1)], side=ride(1))
        if l > 0:
            dx_next, dh1_next = dx0, dh1
        else:
            (grad_x,), (dg0,) = rowmap_bwd("norm_first_bwd", f_norm_keep, [x], [gain("norm_mix_pre", 0)], [dx0, dh1],
                                           ROW_TILE, [F32], [True])
            small_g["norm_mix_pre"][0] = dg0[0]

        finish_reduce()
        start_reduce(l, "mix", dW)
        continue_reduce()
    finish_reduce()
    run_updates()

    dtab = bias_table_bwd(dbias_all, bucket)
    small_g["rel_bias"] = dtab[:, :A_HEADS]
    layered = [n != "rel_bias" for n in SMALL]
    sg_shapes = [(depth,) + small_g[n][0].shape if lay else small_g[n].shape for n, lay in zip(SMALL, layered)]
    contributions = all_gather8("gather_small_grads", _pack_small([small_g[n] for n in SMALL], layered))
    sg_sum = sum_slabs("sum_small_grads", contributions, F32, SMALL_TILE)
    sg = dict(zip(SMALL, _unpack_small(sg_sum, sg_shapes, layered)))
    for n in SMALL_SHARDED:
        w = Wt[n].shape[-1]
        sg[n] = lax.dynamic_slice_in_dim(sg[n], chip * w, w, axis=sg[n].ndim - 1)

    grads, delta, new_m, new_v = {}, {}, {}, {}
    adam["w_in"] = [jnp.swapaxes(a, 1, 2) for a in adam["w_in"]]
    for n in BIG:
        grads[n], delta[n], new_m[n], new_v[n] = adam[n]
    shapes = [Wt[n].shape for n in SMALL]
    packs = [_pack_small([src[n] for n in SMALL], layered) for src in (Wt, sg, Mo, Vo)]
    d_, m_, v_ = rowmap("adamw_small", f_adamw, packs, [], [(LANES, F32)] * 3, SMALL_TILE)
    for n, a, b, c_ in zip(SMALL, *[_unpack_small(p, shapes, layered) for p in (d_, m_, v_)]):
        grads[n], delta[n], new_m[n], new_v[n] = sg[n], a, b, c_

    return (loss, grad_x[None], *[grads[n] for n in WEIGHTS], *[delta[n] for n in WEIGHTS],
            *[new_m[n] for n in WEIGHTS], *[new_v[n] for n in WEIGHTS])
```

```python
import functools
import math

import numpy as np
import jax
import jax.numpy as jnp
from jax import lax
from jax.experimental import pallas as pl
from jax.experimental.pallas import tpu as pltpu

F32, BF16 = jnp.float32, jnp.bfloat16
HI = lax.Precision.HIGHEST
MESH = pl.DeviceIdType.MESH

VMEM_LIMIT_BYTES = 56 * 1024 * 1024
LANES = 128
SUBLANES = 8

D_MODEL = 2048
DEPTH = 4
A_HEAD_DIM = 128
A_HEADS = 8
A_KV_HEADS = 2
A_GROUP = 4
WINDOW = 128
BLOCK = 128
N_BUCKETS = 32
MAX_DISTANCE = 128
B_HEADS = 4
B_KEY_DIM = 64
B_VAL_DIM = 128
GATE_RANK = 16
GATE_TAU = 16.0
C_WIDTH = 512
C_BLOCKS = 4
C_BLOCK_DIM = 128
CONV_WIDTH = 4
CONV_LEFT = 2
LRU_C = 8.0
X_HEADS = 4
X_HEAD_DIM = 512
D_FF = 4 * D_MODEL
EPS = 1e-6
NEG_INF = -1e30
A_Q, A_KV, B_QK, B_V = 1024, 256, 256, 512
SPLIT_SIZES = (A_Q, A_KV, A_KV, B_QK, B_QK, B_V, B_V, GATE_RANK, GATE_RANK, C_WIDTH, C_WIDTH)
D_IN = sum(SPLIT_SIZES)
D_INP = 4224
OFF_AQ, OFF_AK, OFF_AV, OFF_BQ, OFF_BK, OFF_BV, OFF_BG, OFF_CX, OFF_CY, OFF_Z = (
    0, 1024, 1280, 1536, 1792, 2048, 2560, 3072, 3584, 4096)
GLA_CHUNK = 128

ADAM_LR, ADAM_B1, ADAM_B2, ADAM_EPS, ADAM_WD, ADAM_STEP = 0.001, 0.9, 0.999, 1e-08, 0.01, 10


def _cparams(sem=None):
    return pltpu.CompilerParams(dimension_semantics=sem, vmem_limit_bytes=VMEM_LIMIT_BYTES)


def _full_spec(a):
    nd = a.ndim
    return pl.BlockSpec(a.shape, lambda *_: (0,) * nd)


def _tup(r):
    return r if isinstance(r, tuple) else (r,)


HBM_SPEC = pl.BlockSpec(memory_space=pltpu.HBM)


def _place():
    x, y, c = lax.axis_index("x"), lax.axis_index("y"), lax.axis_index("c")
    others = [(1 - x, y), (x, 1 - y), (1 - x, 1 - y)]
    return x, y, c, 2 * x + y, others


def _remote(src, dst, send_sems, recv_sems, k, to):
    return pltpu.make_async_remote_copy(src_ref=src, dst_ref=dst, send_sem=send_sems.at[k], recv_sem=recv_sems.at[k],
                                        device_id=to, device_id_type=MESH)


def _chip_index(ch):
    return 2 * ch[0] + ch[1]


def _pcall(body, args, side=None, **kw):
    if side is None:
        res = pl.pallas_call(body, **kw)(*args)
        return list(res) if isinstance(res, (list, tuple)) else [res]
    single = not isinstance(kw["out_shape"], (list, tuple))
    out_shape = [kw.pop("out_shape")] if single else list(kw.pop("out_shape"))
    out_specs = [kw.pop("out_specs")] if single else list(kw.pop("out_specs"))
    in_specs = list(kw.pop("in_specs"))
    scratch = list(kw.pop("scratch_shapes", ()))
    grid = kw.get("grid", ())
    n_in, n_out, n_scr = len(in_specs), len(out_shape), len(scratch)
    srcs, bufs = side.srcs, side.bufs
    ns, nb = len(srcs), len(bufs)

    def wrapped(*refs):
        ins = refs[:n_in]
        src_refs = refs[n_in:n_in + ns]
        o0 = n_in + ns + nb
        outs = refs[o0:o0 + n_out]
        buf_refs = refs[o0 + n_out:o0 + n_out + nb]
        s0 = o0 + n_out + nb
        scr = refs[s0:s0 + n_scr]
        send_sems, recv_sems = refs[s0 + n_scr], refs[s0 + n_scr + 1]
        first = last = None
        for d, n in enumerate(grid):
            f, l_ = pl.program_id(d) == 0, pl.program_id(d) == n - 1
            first = f if first is None else first & f
            last = l_ if last is None else last & l_
        if first is None:
            side.start(src_refs, buf_refs, send_sems, recv_sems)
            body(*ins, *outs, *scr)
            side.finish(src_refs, buf_refs, send_sems, recv_sems)
            return
        pl.when(first)(lambda: side.start(src_refs, buf_refs, send_sems, recv_sems))
        body(*ins, *outs, *scr)
        pl.when(last)(lambda: side.finish(src_refs, buf_refs, send_sems, recv_sems))

    any_spec = pl.BlockSpec(memory_space=pl.ANY)
    aliases = dict(kw.pop("input_output_aliases", {}))
    aliases.update({n_in + ns + i: n_out + i for i in range(nb)})
    cp = kw.pop("compiler_params", None)
    if grid:
        cp = _cparams(("arbitrary",) * len(grid))
    res = pl.pallas_call(
        wrapped, in_specs=in_specs + [any_spec] * (ns + nb), out_specs=out_specs + [any_spec] * nb,
        out_shape=out_shape + [jax.ShapeDtypeStruct(b.shape, b.dtype) for b in bufs],
        scratch_shapes=scratch + [pltpu.SemaphoreType.DMA((side.n_sems,)), pltpu.SemaphoreType.DMA((side.n_sems,))],
        input_output_aliases=aliases, compiler_params=cp, **kw)(*args, *srcs, *bufs)
    side.done(list(res[n_out:]))
    return list(res[:n_out])


class _Side:
    def __init__(self, parts):
        self.parts = parts
        self.srcs = [s for p in parts for s in p[0].srcs]
        self.bufs = [b for p in parts for b in p[0].bufs]
        self.n_sems = max(1, sum(4 * len(now) + 4 * len(relay) + len(last) for _, now, relay, last in parts))

    def done(self, bufs):
        for p in self.parts:
            p[0].bufs, bufs = bufs[:len(p[0].bufs)], bufs[len(p[0].bufs):]

    def _copies(self, src_refs, buf_refs, send_sems, recv_sems):
        x, y, c, chip, others = _place()
        xn, yn, dg = others
        me, sibling = (x, y, c), (x, y, 1 - c)
        mine, landing = [], []
        k = o = 0

        def pair(src, dst, got, to):
            nonlocal k
            mine.append(_remote(src, dst, send_sems, recv_sems, k, to))
            landing.append(_remote(src, got, send_sems, recv_sems, k, me))
            k += 1

        for ex, now, relay, last in self.parts:
            srcs, bufs = src_refs[o:o + len(ex.srcs)], buf_refs[o:o + len(ex.srcs)]
            o += len(ex.srcs)
            for a, r0, n in now:
                rows = pl.ds(r0, n)
                if ex.kind == "gather":
                    for ch in (xn, yn):
                        pair(srcs[a].at[c, rows], bufs[a].at[chip, c, rows], bufs[a].at[_chip_index(ch), c, rows], (*ch, c))
                elif ex.kind == "scatter":
                    for ch in others:
                        pair(srcs[a].at[_chip_index(ch), rows], bufs[a].at[chip, rows], bufs[a].at[_chip_index(ch), rows],
                             (*ch, c))
                elif ex.kind == "take":
                    for s in range(4):
                        pair(srcs[a].at[s, 1 - c, rows], bufs[a].at[s, rows], bufs[a].at[s, rows], sibling)
                else:
                    pair(srcs[a].at[rows], bufs[a].at[c, rows], bufs[a].at[1 - c, rows], sibling)
            for a, r0, n in relay:
                top, bottom, rows = pl.ds(r0, n // 2), pl.ds(r0 + n // 2, n // 2), pl.ds(r0, n)
                from_x, from_y = bufs[a].at[_chip_index(xn), c, top], bufs[a].at[_chip_index(yn), c, bottom]
                pair(from_x, from_x, bufs[a].at[_chip_index(dg), c, top], (*yn, c))
                pair(from_y, from_y, bufs[a].at[_chip_index(dg), c, bottom], (*xn, c))
                for ch in (xn, yn):
                    here = bufs[a].at[_chip_index(ch), c, rows]
                    pair(here, here, bufs[a].at[_chip_index(ch), 1 - c, rows], sibling)
            for a, r0, n in last:
                here = bufs[a].at[_chip_index(dg), c, pl.ds(r0, n)]
                pair(here, here, bufs[a].at[_chip_index(dg), 1 - c, pl.ds(r0, n)], sibling)
        return mine, landing

    def start(self, src_refs, buf_refs, send_sems, recv_sems):
        for cp in self._copies(src_refs, buf_refs, send_sems, recv_sems)[0]:
            cp.start()

    def finish(self, src_refs, buf_refs, send_sems, recv_sems):
        mine, landing = self._copies(src_refs, buf_refs, send_sems, recv_sems)
        for cp in landing:
            cp.wait_recv()
        for cp in mine:
            cp.wait_send()


class _StagedSide(_Side):
    def start(self, src_refs, buf_refs, send_sems, recv_sems):
        pass

    def finish(self, src_refs, buf_refs, send_sems, recv_sems):
        mine, landing = self._copies(src_refs, buf_refs, send_sems, recv_sems)
        (_, now, relay, _), = self.parts
        cuts = [0, 2 * len(now), 2 * len(now) + 4 * len(relay), len(mine)]
        for lo, hi in zip(cuts, cuts[1:]):
            for cp in mine[lo:hi]:
                cp.start()
            for cp in landing[lo:hi]:
                cp.wait_recv()
        for cp in mine:
            cp.wait_send()


class RidingExchange:
    KINDS = {"gather": (lambda s: (4,) + s, 1, False), "scatter": (lambda s: s, 1, False),
             "take": (lambda s: (s[0],) + s[2:], 2, True), "pair": (lambda s: (2,) + s, 0, True)}

    def __init__(self, kind, srcs, piece_rows):
        self.kind, self.srcs = kind, list(srcs)
        shape_of, row_axis, self.cheap = self.KINDS[kind]
        self.bufs = [lax.empty(shape_of(tuple(s.shape)), s.dtype) for s in srcs]
        heights = [s.shape[row_axis] for s in srcs]
        per = [[(a, r0, min(pr, h - r0)) for r0 in range(0, h, pr)] for a, (h, pr) in enumerate(zip(heights, piece_rows))]
        self.pieces = list(per[0])
        for extra in per[1:]:
            step = max(1, len(self.pieces) // (len(extra) + 1))
            for i, p in enumerate(extra):
                self.pieces.insert(min(len(self.pieces), (i + 1) * step + i), p)
        self.landed, self.relayed = [], []

    def busy(self):
        return bool(self.pieces or self.landed or self.relayed)

    def step(self, n):
        out = []
        for a, r0, rows in self.pieces[:n]:
            if out and out[-1][0] == a and out[-1][1] + out[-1][2] == r0:
                out[-1] = (a, out[-1][1], out[-1][2] + rows)
            else:
                out.append((a, r0, rows))
        self.pieces = self.pieces[n:]
        relay, last = self.landed, self.relayed
        self.landed, self.relayed = (out if self.kind == "gather" else []), relay
        return out, relay, last


D2D_PIECES_A_CALL = 8


class Stream:
    def __init__(self):
        self.queue = []

    def add(self, exchange):
        self.queue.append(exchange)
        return exchange

    def take(self, n, only=None):
        parts = []
        for ex in (self.queue if only is None else [only]):
            had = len(ex.pieces)
            if ex.cheap and only is None:
                now, relay, last = ex.step(D2D_PIECES_A_CALL)
            else:
                now, relay, last = ex.step(n)
                n -= had - len(ex.pieces)
            if now or relay or last:
                parts.append((ex, now, relay, last))
        return _Side(parts) if parts else None

    def finish(self, exchange, name):
        while exchange.busy():
            side = self.take(len(exchange.pieces), only=exchange)
            _pcall(lambda: None, [], side, in_specs=[], out_specs=[], out_shape=[], name=name)
        self.queue.remove(exchange)
        return exchange.bufs


def _row_ops(rows, tr):
    arrs, specs, widths = [], [], []
    for r in rows:
        arr, n, j = r if isinstance(r, tuple) else (r, r.shape[1], 0)
        arrs.append(arr)
        widths.append(n)
        specs.append(pl.BlockSpec((tr, n), lambda i, j=j: (i, j)))
    return arrs, specs, widths


def rowmap(name, f, rows, params, outs, tr, accs=()):
    rows, row_specs, _ = _row_ops(rows, tr)
    T = rows[0].shape[0]
    nin, nout, nacc = len(rows) + len(params), len(outs), len(accs)

    def body(*refs):
        res = _tup(f(*[r[...] for r in refs[:nin]]))
        for o, r in zip(refs[nin:nin + nout], res[:nout]):
            o[...] = r.astype(o.dtype)
        arefs = refs[nin + nout:]
        if nacc:
            @pl.when(pl.program_id(0) == 0)
            def _():
                for a in arefs:
                    a[...] = jnp.zeros(a.shape, a.dtype)
            for a, r in zip(arefs, res[nout:]):
                a[...] += r.astype(F32)

    in_specs = row_specs + [_full_spec(p) for p in params]
    out_specs = [pl.BlockSpec((tr, n), lambda i: (i, 0)) for n, _ in outs] + \
                [pl.BlockSpec(s, lambda i, nd=len(s): (0,) * nd) for s in accs]
    out_shape = [jax.ShapeDtypeStruct((T, n), d) for n, d in outs] + [jax.ShapeDtypeStruct(s, F32) for s in accs]
    res = pl.pallas_call(body, grid=(T // tr,), in_specs=in_specs, out_specs=out_specs, out_shape=out_shape,
                         compiler_params=_cparams(("arbitrary",)), name=name)(*rows, *params)
    return tuple(res)


def rowmap_bwd(name, f, rows, params, cots, tr, drow_dtypes, want_params):
    rows, row_specs, widths = _row_ops(rows, tr)
    T = rows[0].shape[0]
    nr, npar = len(rows), len(params)
    cot_arrays, cot_specs, _ = _row_ops([c for c in cots if c is not None], tr)
    nc = len(cot_arrays)
    ridx = [i for i, d in enumerate(drow_dtypes) if d is not None]
    pidx = [i for i, w in enumerate(want_params) if w]

    def body(*refs):
        rvals = [r[...] for r in refs[:nr]]
        pvals = [r[...] for r in refs[nr:nr + npar]]
        crefs = list(refs[nr + npar:nr + npar + nc])
        orefs = refs[nr + npar + nc:]
        outs, vjp = jax.vjp(f, *rvals, *pvals)
        outs = _tup(outs)
        cts = []
        for c, o in zip(cots, outs):
            cts.append(jnp.ones(o.shape, o.dtype) if c is None else crefs.pop(0)[...].astype(o.dtype))
        grads = vjp(tuple(cts) if len(cts) > 1 else cts[0])
        for o, i in zip(orefs[:len(ridx)], ridx):
            o[...] = grads[i].astype(o.dtype)
        prefs = orefs[len(ridx):]
        if prefs:
            @pl.when(pl.program_id(0) == 0)
            def _():
                for a in prefs:
                    a[...] = jnp.zeros(a.shape, a.dtype)
            for a, i in zip(prefs, pidx):
                a[...] += grads[nr + i].astype(F32)

    in_specs = row_specs + [_full_spec(p) for p in params] + cot_specs
    out_specs = [pl.BlockSpec((tr, widths[i]), lambda i: (i, 0)) for i in ridx] + [_full_spec(params[i]) for i in pidx]
    out_shape = [jax.ShapeDtypeStruct((T, widths[i]), drow_dtypes[i]) for i in ridx] + \
                [jax.ShapeDtypeStruct(params[i].shape, F32) for i in pidx]
    res = pl.pallas_call(body, grid=(T // tr,), in_specs=in_specs, out_specs=out_specs, out_shape=out_shape,
                         compiler_params=_cparams(("arbitrary",)), name=name)(*rows, *params, *cot_arrays)
    res = tuple(res)
    return res[:len(ridx)], res[len(ridx):]


def _pick(n, pref):
    best = None
    for d in range(LANES, min(n, pref) + 1, LANES):
        if n % d == 0:
            best = d
    return best if best is not None else n


def _spec2(arr, tile, pos):
    tr, tc = tile
    if arr.ndim == 2:
        return pl.BlockSpec((tr, tc), lambda i, j, k: pos(i, j, k))
    assert arr.shape[2] % tc == 0, (arr.shape, tile)
    per = arr.shape[2] // tc

    def imap(i, j, k):
        r, c = pos(i, j, k)
        return (c // per, r, c % per)
    return pl.BlockSpec((None, tr, tc), imap)


def _dims2(arr):
    return (arr.shape[0], arr.shape[1]) if arr.ndim == 2 else (arr.shape[1], arr.shape[0] * arr.shape[2])


def mm(name, a, b, mode, outs, epilogue=None, extras=(), pm=1024, pn=512, pk=4224, side=None):
    ar, ac = _dims2(a)
    br, bc = _dims2(b)
    if mode == "nn":
        M, K, N = ar, ac, bc
    elif mode == "nt":
        M, K, N = ar, ac, br
    else:
        M, K, N = ac, ar, bc
    tm, tn, tk = _pick(M, pm), _pick(N, pn), _pick(K, pk)
    for arr in (a, b) + tuple(extras):
        if arr.ndim == 3:
            assert arr.shape[2] % LANES == 0
    if mode == "nn":
        a_spec = _spec2(a, (tm, tk), lambda i, j, k: (i, k))
        b_spec = _spec2(b, (tk, tn), lambda i, j, k: (k, j))
        dims = (((1,), (0,)), ((), ()))
    elif mode == "nt":
        a_spec = _spec2(a, (tm, tk), lambda i, j, k: (i, k))
        b_spec = _spec2(b, (tn, tk), lambda i, j, k: (j, k))
        dims = (((1,), (1,)), ((), ()))
    else:
        a_spec = _spec2(a, (tk, tm), lambda i, j, k: (k, i))
        b_spec = _spec2(b, (tk, tn), lambda i, j, k: (k, j))
        dims = (((0,), (0,)), ((), ()))
    nk = K // tk
    nex = len(extras)

    def body(*refs):
        a_ref, b_ref = refs[0], refs[1]
        ex_refs = refs[2:2 + nex]
        o_refs = refs[2 + nex:2 + nex + len(outs)]
        acc = refs[-1]
        k = pl.program_id(2)
        part = lax.dot_general(a_ref[...].astype(BF16), b_ref[...].astype(BF16), dims, preferred_element_type=F32)

        def finish(r):
            res = (r,) if epilogue is None else _tup(epilogue(r, *[e[...] for e in ex_refs]))
            for o, v in zip(o_refs, res):
                o[...] = v.astype(o.dtype)

        if nk == 1:
            finish(part)
            return

        @pl.when(k == 0)
        def _():
            acc[...] = part

        @pl.when(k > 0)
        def _():
            acc[...] += part

        @pl.when(k == nk - 1)
        def _():
            finish(acc[...])

    out_shape, out_specs = [], []
    args, in_specs, aliases = [a, b, *extras], [a_spec, b_spec], {}
    in_specs += [_spec2(e, (tm, tn), lambda i, j, k: (i, j)) for e in extras]
    for dt, chunks in outs:
        if isinstance(chunks, tuple):
            how, rows, off, buf = chunks
            o = jax.ShapeDtypeStruct((4, rows, D_MODEL), dt)
            assert off % tm == 0 and D_MODEL % tn == 0 and D_MODEL % tm == 0
            if how == "cols":
                per = D_MODEL // tn
                spec = pl.BlockSpec((None, tm, tn), lambda i, j, k: (j // per, off // tm + i, j % per))
            else:
                per = D_MODEL // tm
                spec = pl.BlockSpec((None, tm, tn), lambda i, j, k: (i // per, off // tm + i % per, j))
            if buf is not None:
                aliases[len(args)] = len(out_shape)
                args.append(buf)
                in_specs.append(pl.BlockSpec(memory_space=pl.ANY))
        else:
            o = jax.ShapeDtypeStruct((M, N) if chunks == 1 else (chunks, M, N // chunks), dt)
            spec = _spec2(o, (tm, tn), lambda i, j, k: (i, j))
        out_shape.append(o)
        out_specs.append(spec)
    n_extra_in = len(args) - 2 - nex
    res = _pcall(
        (lambda *refs: body(*refs[:2 + nex], *refs[2 + nex + n_extra_in:])) if n_extra_in else body, args, side,
        grid=(M // tm, N // tn, nk), in_specs=in_specs, out_specs=out_specs, out_shape=out_shape,
        scratch_shapes=[pltpu.VMEM((tm, tn), F32)] if nk > 1 else [], input_output_aliases=aliases,
        compiler_params=_cparams(("parallel", "parallel", "arbitrary")), name=name)
    return tuple(res)


def _rms(x, g):
    return x * lax.rsqrt(jnp.mean(x * x, axis=-1, keepdims=True) + EPS) * g


def f_norm(x, g):
    return _rms(x, g)


def f_norm_keep(x, g):
    return x, _rms(x, g)


def f_resnorm(xp, m, gpost, gnext):
    xn = xp + _rms(m, gpost)
    return xn, _rms(xn, gnext)


def f_final_rows(xp, m, tgt, gpost):
    xn = xp + _rms(m, gpost)
    return 0.5 * jnp.mean(jnp.square(xn - tgt), axis=-1, keepdims=True)


def f_final_loss(xp, m, tgt, gpost):
    return jnp.sum(f_final_rows(xp, m, tgt, gpost), axis=0, keepdims=True)


def f_norm_twice(x, g):
    y = _rms(x, g)
    return y, y


def f_xattn(q, k, v):
    outs = []
    for h in range(X_HEADS):
        sl = slice(h * X_HEAD_DIM, (h + 1) * X_HEAD_DIM)
        s = lax.dot_general(q[:, sl].astype(BF16), k[:, sl].astype(BF16), (((1,), (1,)), ((), ())),
                            preferred_element_type=F32) * (X_HEAD_DIM ** -0.5)
        m = jnp.max(s, axis=-1, keepdims=True)
        e = jnp.exp(s - m)
        p = e / jnp.sum(e, axis=-1, keepdims=True)
        outs.append(jnp.dot(p.astype(BF16), v[:, sl].astype(BF16), preferred_element_type=F32))
    return jnp.concatenate(outs, axis=1)


def f_adamw(w, g, m, v):
    m = ADAM_B1 * m + (1.0 - ADAM_B1) * g
    v = ADAM_B2 * v + (1.0 - ADAM_B2) * jnp.square(g)
    m_hat = m / (1.0 - ADAM_B1 ** ADAM_STEP)
    v_hat = v / (1.0 - ADAM_B2 ** ADAM_STEP)
    delta = -ADAM_LR * (m_hat / (jnp.sqrt(v_hat) + ADAM_EPS) + ADAM_WD * w)
    return delta, m, v


def t5_bucket_map():
    qi = jnp.arange(BLOCK)[:, None]
    kj = jnp.arange(3 * BLOCK)[None, :]
    rel = kj - BLOCK - qi
    nb = N_BUCKETS // 2
    max_exact = nb // 2
    ret = jnp.where(rel > 0, nb, 0)
    n = jnp.abs(rel)
    nf = jnp.maximum(n, 1).astype(jnp.float32)
    large = max_exact + (jnp.log(nf / max_exact) / math.log(MAX_DISTANCE / max_exact) * (nb - max_exact)).astype(jnp.int32)
    large = jnp.minimum(large, nb - 1)
    return (ret + jnp.where(n < max_exact, n, large)).astype(jnp.int32)


def bias_table_fwd(table, bucket):
    def body(t_ref, b_ref, o_ref):
        bk = b_ref[...]
        for h in range(A_HEADS):
            acc = jnp.zeros(bk.shape, F32)
            for b in range(N_BUCKETS):
                acc = jnp.where(bk == b, t_ref[b, h], acc)
            o_ref[h] = acc
    return pl.pallas_call(
        body, in_specs=[pl.BlockSpec(memory_space=pltpu.SMEM), pl.BlockSpec(memory_space=pltpu.VMEM)],
        out_specs=pl.BlockSpec(memory_space=pltpu.VMEM),
        out_shape=jax.ShapeDtypeStruct((A_HEADS, BLOCK, 3 * BLOCK), F32), name="bias_table_fwd")(table, bucket)


def bias_table_bwd(dbias_list, bucket):
    n = len(dbias_list)

    def body(*refs):
        b_ref, o_ref = refs[n], refs[n + 1]
        bk = b_ref[...]
        row = lax.broadcasted_iota(jnp.int32, (N_BUCKETS, LANES), 0)
        col = lax.broadcasted_iota(jnp.int32, (N_BUCKETS, LANES), 1)
        out = jnp.zeros((N_BUCKETS, LANES), F32)
        for h in range(A_HEADS):
            d = refs[0][h]
            for r in refs[1:n]:
                d = d + r[h]
            for b in range(N_BUCKETS):
                s = jnp.sum(jnp.where(bk == b, d, 0.0), keepdims=True)
                out = out + jnp.where((row == b) & (col == h), s, 0.0)
        o_ref[...] = out
    return pl.pallas_call(
        body, out_shape=jax.ShapeDtypeStruct((N_BUCKETS, LANES), F32), name="bias_table_bwd",
        compiler_params=_cparams())(*dbias_list, bucket)


def _attn_mask(n, nblk):
    i = lax.broadcasted_iota(jnp.int32, (BLOCK, 3 * BLOCK), 0)
    j = lax.broadcasted_iota(jnp.int32, (BLOCK, 3 * BLOCK), 1)
    kpos = n * BLOCK + j - BLOCK
    return (jnp.abs(j - BLOCK - i) <= WINDOW) & (kpos >= 0) & (kpos < nblk * BLOCK)


def f_attn_block(q, k3, v3, bias, sink, mask):
    kb, vb = k3.astype(BF16), v3.astype(BF16)
    outs = []
    for g in range(A_GROUP):
        qg = q[:, g * A_HEAD_DIM:(g + 1) * A_HEAD_DIM].astype(BF16)
        s = lax.dot_general(qg, kb, (((1,), (1,)), ((), ())), preferred_element_type=F32) * (A_HEAD_DIM ** -0.5)
        s = jnp.where(mask, s + bias[g], NEG_INF)
        sk = sink[g:g + 1, :]
        m = jnp.maximum(jnp.max(s, axis=-1, keepdims=True), sk)
        e = jnp.exp(s - m)
        den = jnp.sum(e, axis=-1, keepdims=True) + jnp.exp(sk - m)
        p = e / den
        outs.append(jnp.dot(p.astype(BF16), vb, preferred_element_type=F32))
    return jnp.concatenate(outs, axis=1)


def _attn_in_specs(nblk):
    qw = A_GROUP * A_HEAD_DIM
    kc, vc = OFF_AK // A_HEAD_DIM, OFF_AV // A_HEAD_DIM
    return [
        pl.BlockSpec((BLOCK, qw), lambda h, n: (n, h)),
        pl.BlockSpec((BLOCK, A_HEAD_DIM), lambda h, n: (jnp.maximum(n - 1, 0), kc + h)),
        pl.BlockSpec((BLOCK, A_HEAD_DIM), lambda h, n: (n, kc + h)),
        pl.BlockSpec((BLOCK, A_HEAD_DIM), lambda h, n: (jnp.minimum(n + 1, nblk - 1), kc + h)),
        pl.BlockSpec((BLOCK, A_HEAD_DIM), lambda h, n: (jnp.maximum(n - 1, 0), vc + h)),
        pl.BlockSpec((BLOCK, A_HEAD_DIM), lambda h, n: (n, vc + h)),
        pl.BlockSpec((BLOCK, A_HEAD_DIM), lambda h, n: (jnp.minimum(n + 1, nblk - 1), vc + h)),
        pl.BlockSpec((A_GROUP, BLOCK, 3 * BLOCK), lambda h, n: (h, 0, 0)),
        pl.BlockSpec((None, A_GROUP, 1), lambda h, n: (h, 0, 0)),
    ]


def attn_fwd(proj, bias, sink, side=None):
    T = proj.shape[0]
    nblk = T // BLOCK

    def body(q_ref, k0, k1, k2, v0, v1, v2, b_ref, s_ref, o_ref):
        n = pl.program_id(1)
        k3 = jnp.concatenate([k0[...], k1[...], k2[...]], axis=0)
        v3 = jnp.concatenate([v0[...], v1[...], v2[...]], axis=0)
        o = f_attn_block(q_ref[...], k3, v3, b_ref[...], s_ref[...], _attn_mask(n, nblk))
        o_ref[...] = o.astype(o_ref.dtype)

    return _pcall(
        body, [proj] * 7 + [bias, sink], side, grid=(A_KV_HEADS, nblk), in_specs=_attn_in_specs(nblk),
        out_specs=pl.BlockSpec((BLOCK, A_GROUP * A_HEAD_DIM), lambda h, n: (n, h)),
        out_shape=jax.ShapeDtypeStruct((T, A_Q), BF16),
        compiler_params=_cparams(("arbitrary", "arbitrary")), name="attn_fwd")[0]


def attn_bwd(proj, bias, sink, dcat, side=None):
    T = proj.shape[0]
    nblk = T // BLOCK
    qw = A_GROUP * A_HEAD_DIM

    def body(q_ref, k0, k1, k2, v0, v1, v2, b_ref, s_ref, do_ref, dq_ref, dk_ref, dv_ref, db_ref, ds_ref, dk_acc, dv_acc):
        n = pl.program_id(1)

        @pl.when(n == 0)
        def _():
            dk_acc[...] = jnp.zeros(dk_acc.shape, F32)
            dv_acc[...] = jnp.zeros(dv_acc.shape, F32)
            db_ref[...] = jnp.zeros(db_ref.shape, F32)
            ds_ref[...] = jnp.zeros(ds_ref.shape, F32)

        k3 = jnp.concatenate([k0[...], k1[...], k2[...]], axis=0)
        v3 = jnp.concatenate([v0[...], v1[...], v2[...]], axis=0)
        mask = _attn_mask(n, nblk)
        _, vjp = jax.vjp(lambda q, k, v, b, s: f_attn_block(q, k, v, b, s, mask), q_ref[...], k3, v3, b_ref[...], s_ref[...])
        dq, dk3, dv3, db, ds = vjp(do_ref[...])
        dq_ref[...] = dq.astype(dq_ref.dtype)
        db_ref[...] += db
        ds_ref[...] += ds
        mid = pl.multiple_of(n * BLOCK, BLOCK)
        dk_acc[pl.ds(mid, BLOCK), :] += dk3[BLOCK:2 * BLOCK]
        dv_acc[pl.ds(mid, BLOCK), :] += dv3[BLOCK:2 * BLOCK]

        @pl.when(n > 0)
        def _():
            lo = pl.multiple_of((n - 1) * BLOCK, BLOCK)
            dk_acc[pl.ds(lo, BLOCK), :] += dk3[0:BLOCK]
            dv_acc[pl.ds(lo, BLOCK), :] += dv3[0:BLOCK]

        @pl.when(n < nblk - 1)
        def _():
            hi = pl.multiple_of((n + 1) * BLOCK, BLOCK)
            dk_acc[pl.ds(hi, BLOCK), :] += dk3[2 * BLOCK:3 * BLOCK]
            dv_acc[pl.ds(hi, BLOCK), :] += dv3[2 * BLOCK:3 * BLOCK]

        @pl.when(n == nblk - 1)
        def _():
            dk_ref[...] = dk_acc[...].astype(dk_ref.dtype)
            dv_ref[...] = dv_acc[...].astype(dv_ref.dtype)

    in_specs = _attn_in_specs(nblk) + [pl.BlockSpec((BLOCK, qw), lambda h, n: (n, h))]
    out_specs = [
        pl.BlockSpec((BLOCK, qw), lambda h, n: (n, h)),
        pl.BlockSpec((T, A_HEAD_DIM), lambda h, n: (0, h)),
        pl.BlockSpec((T, A_HEAD_DIM), lambda h, n: (0, h)),
        pl.BlockSpec((A_GROUP, BLOCK, 3 * BLOCK), lambda h, n: (h, 0, 0)),
        pl.BlockSpec((None, A_GROUP, 1), lambda h, n: (h, 0, 0)),
    ]
    out_shape = [
        jax.ShapeDtypeStruct((T, A_Q), BF16), jax.ShapeDtypeStruct((T, A_KV), BF16), jax.ShapeDtypeStruct((T, A_KV), BF16),
        jax.ShapeDtypeStruct((A_HEADS, BLOCK, 3 * BLOCK), F32), jax.ShapeDtypeStruct((A_KV_HEADS, A_GROUP, 1), F32),
    ]
    return _pcall(
        body, [proj] * 7 + [bias, sink, dcat], side, grid=(A_KV_HEADS, nblk), in_specs=in_specs, out_specs=out_specs,
        out_shape=out_shape, scratch_shapes=[pltpu.VMEM((T, A_HEAD_DIM), F32), pltpu.VMEM((T, A_HEAD_DIM), F32)],
        compiler_params=_cparams(("arbitrary", "arbitrary")), name="attn_bwd")


def f_gla_gate(z, w2f, b2f, w2b, b2b):
    laf = jax.nn.log_sigmoid(jnp.dot(z, w2f, precision=HI, preferred_element_type=F32) + b2f) / GATE_TAU
    lab = jax.nn.log_sigmoid(jnp.dot(z, w2b, precision=HI, preferred_element_type=F32) + b2b) / GATE_TAU
    return laf, lab


def f_gla_post(o, g, gn):
    outs = []
    for h in range(B_HEADS):
        sl = slice(h * B_VAL_DIM, (h + 1) * B_VAL_DIM)
        oh = o[:, sl]
        outs.append(oh * lax.rsqrt(jnp.mean(oh * oh, axis=-1, keepdims=True) + EPS))
    return jnp.concatenate(outs, axis=1) * gn * jax.nn.silu(g)


def _gla_consts(forward):
    C = GLA_CHUNK
    i = lax.broadcasted_iota(jnp.int32, (C, C), 0)
    j = lax.broadcasted_iota(jnp.int32, (C, C), 1)
    if forward:
        return (j <= i).astype(F32), j <= i
    return (j >= i).astype(F32), j > i


def _gla_chunk(q, k, v, la, st, tri, msk, forward):
    C = q.shape[0]
    b = jnp.dot(tri, la, precision=HI, preferred_element_type=F32)
    bl = b[C - 1:C] if forward else b[0:1]
    qe = (q * (B_KEY_DIM ** -0.5)) * jnp.exp(b)
    ke = k * jnp.exp(-b)
    kl = k * jnp.exp(bl - b)
    att = lax.dot_general(qe.astype(BF16), ke.astype(BF16), (((1,), (1,)), ((), ())), preferred_element_type=F32)
    att = jnp.where(msk, att, 0.0)
    o = jnp.dot(att.astype(BF16), v.astype(BF16), preferred_element_type=F32)
    o = o + lax.dot_general(qe.astype(BF16), st.astype(BF16), (((1,), (1,)), ((), ())), preferred_element_type=F32)
    st_new = st * jnp.exp(bl) + lax.dot_general(v.astype(BF16), kl.astype(BF16), (((0,), (0,)), ((), ())),
                                                preferred_element_type=F32)
    return o, st_new


def _gla_state(k, v, la, st, tri, forward):
    C = k.shape[0]
    b = jnp.dot(tri, la, precision=HI, preferred_element_type=F32)
    bl = b[C - 1:C] if forward else b[0:1]
    kl = k * jnp.exp(bl - b)
    return st * jnp.exp(bl) + lax.dot_general(v.astype(BF16), kl.astype(BF16), (((0,), (0,)), ((), ())),
                                              preferred_element_type=F32)


def _gla_specs(T):
    qc, kc, vc = OFF_BQ // LANES, OFF_BK // LANES, OFF_BV // (2 * B_VAL_DIM)
    return [
        pl.BlockSpec((T, LANES), lambda p: (0, qc + p)),
        pl.BlockSpec((T, LANES), lambda p: (0, kc + p)),
        pl.BlockSpec((T, 2 * B_VAL_DIM), lambda p: (0, vc + p)),
        pl.BlockSpec((T, LANES), lambda p: (0, p)),
        pl.BlockSpec((T, LANES), lambda p: (0, p)),
    ]


def _rows(c):
    return pl.ds(pl.multiple_of(c * GLA_CHUNK, GLA_CHUNK), GLA_CHUNK)


def _chunk_loop(step, n, per_trip, carry):
    while n % per_trip:
        per_trip //= 2

    def trip(t, carry):
        for u in range(per_trip):
            carry = step(per_trip * t + u, carry)
        return carry

    return lax.fori_loop(0, n // per_trip, trip, carry)


def gla_fwd(proj, laf, lab, side=None):
    T = proj.shape[0]
    nc = T // GLA_CHUNK

    def body(q_ref, k_ref, v_ref, laf_ref, lab_ref, o_ref, ob_scr):
        tri_f, msk_f = _gla_consts(True)
        tri_b, msk_b = _gla_consts(False)
        zero = jnp.zeros((B_VAL_DIM, B_KEY_DIM), F32)

        def step(c, carry):
            rf, rb = _rows(c), _rows(nc - 1 - c)
            new = []
            for hh in range(2):
                ks = slice(hh * B_KEY_DIM, (hh + 1) * B_KEY_DIM)
                vs = slice(hh * B_VAL_DIM, (hh + 1) * B_VAL_DIM)
                o, s = _gla_chunk(q_ref[rf, ks], k_ref[rf, ks], v_ref[rf, vs], laf_ref[rf, ks], carry[2 * hh], tri_f, msk_f, True)
                o_ref[rf, vs] = o
                new.append(s)
                o, s = _gla_chunk(q_ref[rb, ks], k_ref[rb, ks], v_ref[rb, vs], lab_ref[rb, ks], carry[2 * hh + 1], tri_b, msk_b, False)
                ob_scr[rb, vs] = o
                new.append(s)
            return tuple(new)

        _chunk_loop(step, nc, 4, (zero,) * 4)
        o_ref[...] += ob_scr[...]

    return _pcall(
        body, [proj, proj, proj, laf, lab], side, grid=(B_HEADS // 2,), in_specs=_gla_specs(T),
        out_specs=pl.BlockSpec((T, 2 * B_VAL_DIM), lambda p: (0, p)),
        out_shape=jax.ShapeDtypeStruct((T, B_V), F32),
        scratch_shapes=[pltpu.VMEM((T, 2 * B_VAL_DIM), F32)],
        compiler_params=_cparams(("arbitrary",)), name="gla_fwd")[0]


def gla_bwd(proj, laf, lab, do, side=None):
    T = proj.shape[0]
    nc = T // GLA_CHUNK
    SROWS = 2 * B_VAL_DIM

    def body(q_ref, k_ref, v_ref, laf_ref, lab_ref, do_ref, dq_ref, dk_ref, dv_ref, dlaf_ref, dlab_ref,
             sf_scr, sb_scr, dq_acc, dk_acc, dv_acc):
        tri_f, msk_f = _gla_consts(True)
        tri_b, msk_b = _gla_consts(False)
        zero = jnp.zeros((B_VAL_DIM, B_KEY_DIM), F32)
        dq_acc[...] = jnp.zeros(dq_acc.shape, F32)
        dk_acc[...] = jnp.zeros(dk_acc.shape, F32)
        dv_acc[...] = jnp.zeros(dv_acc.shape, F32)

        def srow(c, hh):
            return pl.ds(pl.multiple_of(c * SROWS + hh * B_VAL_DIM, B_VAL_DIM), B_VAL_DIM)

        def states(c, carry):
            cf, cb = c, nc - 1 - c
            rf, rb = _rows(cf), _rows(cb)
            new = []
            for hh in range(2):
                ks = slice(hh * B_KEY_DIM, (hh + 1) * B_KEY_DIM)
                vs = slice(hh * B_VAL_DIM, (hh + 1) * B_VAL_DIM)
                sf_scr[srow(cf, hh), :] = carry[2 * hh]
                new.append(_gla_state(k_ref[rf, ks], v_ref[rf, vs], laf_ref[rf, ks], carry[2 * hh], tri_f, True))
                sb_scr[srow(cb, hh), :] = carry[2 * hh + 1]
                new.append(_gla_state(k_ref[rb, ks], v_ref[rb, vs], lab_ref[rb, ks], carry[2 * hh + 1], tri_b, False))
            return tuple(new)

        _chunk_loop(states, nc, 2, (zero,) * 4)

        def back(c, carry):
            cf, cb = nc - 1 - c, c
            rf, rb = _rows(cf), _rows(cb)
            new = []
            for hh in range(2):
                ks = slice(hh * B_KEY_DIM, (hh + 1) * B_KEY_DIM)
                vs = slice(hh * B_VAL_DIM, (hh + 1) * B_VAL_DIM)
                for fwd, r, c_, la_ref, dla_ref, s_scr, g, tri, msk in (
                        (True, rf, cf, laf_ref, dlaf_ref, sf_scr, carry[2 * hh], tri_f, msk_f),
                        (False, rb, cb, lab_ref, dlab_ref, sb_scr, carry[2 * hh + 1], tri_b, msk_b)):
                    _, vjp = jax.vjp(
                        lambda q, k, v, la, st: _gla_chunk(q, k, v, la, st, tri, msk, fwd),
                        q_ref[r, ks], k_ref[r, ks], v_ref[r, vs], la_ref[r, ks], s_scr[srow(c_, hh), :])
                    dq, dk, dv, dla, dst = vjp((do_ref[r, vs], g))
                    dq_acc[r, ks] += dq
                    dk_acc[r, ks] += dk
                    dv_acc[r, vs] += dv
                    dla_ref[r, ks] = dla
                    new.append(dst)
            return tuple(new)

        _chunk_loop(back, nc, 2, (zero,) * 4)
        dq_ref[...] = dq_acc[...].astype(dq_ref.dtype)
        dk_ref[...] = dk_acc[...].astype(dk_ref.dtype)
        dv_ref[...] = dv_acc[...].astype(dv_ref.dtype)

    in_specs = _gla_specs(T) + [pl.BlockSpec((T, 2 * B_VAL_DIM), lambda p: (0, p))]
    out_specs = [
        pl.BlockSpec((T, LANES), lambda p: (0, p)), pl.BlockSpec((T, LANES), lambda p: (0, p)),
        pl.BlockSpec((T, 2 * B_VAL_DIM), lambda p: (0, p)),
        pl.BlockSpec((T, LANES), lambda p: (0, p)), pl.BlockSpec((T, LANES), lambda p: (0, p)),
    ]
    out_shape = [
        jax.ShapeDtypeStruct((T, B_QK), BF16), jax.ShapeDtypeStruct((T, B_QK), BF16), jax.ShapeDtypeStruct((T, B_V), BF16),
        jax.ShapeDtypeStruct((T, B_QK), F32), jax.ShapeDtypeStruct((T, B_QK), F32),
    ]
    scratch = [
        pltpu.VMEM((nc * SROWS, B_KEY_DIM), F32), pltpu.VMEM((nc * SROWS, B_KEY_DIM), F32),
        pltpu.VMEM((T, LANES), F32), pltpu.VMEM((T, LANES), F32), pltpu.VMEM((T, 2 * B_VAL_DIM), F32),
    ]
    return _pcall(
        body, [proj, proj, proj, laf, lab, do], side, grid=(B_HEADS // 2,), in_specs=in_specs, out_specs=out_specs,
        out_shape=out_shape, scratch_shapes=scratch, compiler_params=_cparams(("arbitrary",)), name="gla_bwd")


def _shift_raw(x, k):
    T = x.shape[0]
    r = lax.broadcasted_iota(jnp.int32, x.shape, 0)
    if k > 0:
        return jnp.where(r >= k, pltpu.roll(x, k, 0), 0.0)
    return jnp.where(r < T + k, pltpu.roll(x, T + k, 0), 0.0)


@functools.partial(jax.custom_vjp, nondiff_argnums=(1,))
def _shift(x, k):
    return _shift_raw(x, k)


_shift.defvjp(lambda x, k: (_shift_raw(x, k), None), lambda k, _, g: (_shift_raw(g, -k),))


def _scan_raw(a, u, reverse):
    T = a.shape[0]
    d = 1
    while d < T:
        k = -d if reverse else d
        u = a * _shift_raw(u, k) + u
        a = a * _shift_raw(a, k)
        d *= 2
    return u


@functools.partial(jax.custom_vjp, nondiff_argnums=(2,))
def _scan(a, u, reverse):
    return _scan_raw(a, u, reverse)


def _scan_f(a, u, reverse):
    h = _scan_raw(a, u, reverse)
    return h, (a, h)


def _scan_b(reverse, res, dh):
    a, h = res
    k = 1 if reverse else -1
    du = _scan_raw(_shift_raw(a, k), dh, not reverse)
    return du * _shift_raw(h, -k), du


_scan.defvjp(_scan_f, _scan_b)


def f_lru(cx, cy, cw, cb, wa, ba, wx, bx, lam, diff):
    shift, scan = (_shift, _scan) if diff else (_shift_raw, _scan_raw)
    xc = cx * cw[CONV_LEFT:CONV_LEFT + 1]
    for j in range(CONV_WIDTH):
        if j != CONV_LEFT:
            xc = xc + shift(cx, CONV_LEFT - j) * cw[j:j + 1]
    xc = xc + cb
    xb = xc.astype(BF16)
    h = None
    for s in range(2):
        r = jax.nn.sigmoid(jnp.dot(xb, wa[s].astype(BF16), preferred_element_type=F32) + ba[s:s + 1])
        i = jax.nn.sigmoid(jnp.dot(xb, wx[s].astype(BF16), preferred_element_type=F32) + bx[s:s + 1])
        log_a = -LRU_C * r * jax.nn.softplus(-lam[s:s + 1])
        a = jnp.exp(log_a)
        one_minus_a2 = -jnp.tanh(log_a) * (a * a + 1.0)
        u = jnp.sqrt(one_minus_a2) * (i * xc)
        hs = scan(a, u, s == 1)
        h = hs if h is None else h + hs
    return h * jax.nn.gelu(cy)


def _lru_specs(T):
    xc, yc = OFF_CX // LANES, OFF_CY // LANES
    return [
        pl.BlockSpec((T, LANES), lambda b: (0, xc + b)),
        pl.BlockSpec((T, LANES), lambda b: (0, yc + b)),
        pl.BlockSpec((CONV_WIDTH, LANES), lambda b: (0, b)),
        pl.BlockSpec((1, LANES), lambda b: (0, b)),
        pl.BlockSpec((2, None, C_BLOCK_DIM, C_BLOCK_DIM), lambda b: (0, b, 0, 0)),
        pl.BlockSpec((2, LANES), lambda b: (0, b)),
        pl.BlockSpec((2, None, C_BLOCK_DIM, C_BLOCK_DIM), lambda b: (0, b, 0, 0)),
        pl.BlockSpec((2, LANES), lambda b: (0, b)),
        pl.BlockSpec((2, LANES), lambda b: (0, b)),
    ]


def lru_fwd(proj, cw, cb, wa, ba, wx, bx, lam, side=None):
    T = proj.shape[0]

    def body(cx, cy, cw_r, cb_r, wa_r, ba_r, wx_r, bx_r, lam_r, o_ref):
        o = f_lru(cx[...], cy[...], cw_r[...], cb_r[...], wa_r[...], ba_r[...], wx_r[...], bx_r[...], lam_r[...], False)
        o_ref[...] = o.astype(o_ref.dtype)

    return _pcall(
        body, [proj, proj, cw, cb, wa, ba, wx, bx, lam], side, grid=(C_BLOCKS,), in_specs=_lru_specs(T),
        out_specs=pl.BlockSpec((T, LANES), lambda b: (0, b)), out_shape=jax.ShapeDtypeStruct((T, C_WIDTH), BF16),
        compiler_params=_cparams(("arbitrary",)), name="lru_fwd")[0]


def lru_bwd(proj, cw, cb, wa, ba, wx, bx, lam, dcat, side=None):
    T = proj.shape[0]
    oc = (A_Q + B_V) // LANES

    def body(cx, cy, cw_r, cb_r, wa_r, ba_r, wx_r, bx_r, lam_r, do_ref, *outs):
        _, vjp = jax.vjp(functools.partial(f_lru, diff=True), cx[...], cy[...], cw_r[...], cb_r[...], wa_r[...],
                         ba_r[...], wx_r[...], bx_r[...], lam_r[...])
        grads = vjp(do_ref[...])
        for o, g in zip(outs, grads):
            o[...] = g.astype(o.dtype)

    specs = _lru_specs(T)
    out_specs = [pl.BlockSpec((T, LANES), lambda b: (0, b)), pl.BlockSpec((T, LANES), lambda b: (0, b))] + specs[2:]
    out_shape = [jax.ShapeDtypeStruct((T, C_WIDTH), BF16), jax.ShapeDtypeStruct((T, C_WIDTH), BF16)] + \
                [jax.ShapeDtypeStruct(p.shape, F32) for p in (cw, cb, wa, ba, wx, bx, lam)]
    return _pcall(
        body, [proj, proj, cw, cb, wa, ba, wx, bx, lam, dcat], side, grid=(C_BLOCKS,),
        in_specs=specs + [pl.BlockSpec((T, LANES), lambda b: (0, oc + b))], out_specs=out_specs, out_shape=out_shape,
        compiler_params=_cparams(("arbitrary",)), name="lru_bwd")


def all_gather8(name, blk):
    def body(x_ref, out_ref, send_sems, recv_sems):
        x, y, c, _, others = _place()
        sibling = (x, y, 1 - c)

        def slab(px, py, pc):
            return out_ref.at[4 * px + 2 * py + pc]

        first = [_remote(x_ref, slab(x, y, c), send_sems, recv_sems, 0, sibling)]
        first += [_remote(x_ref, slab(x, y, c), send_sems, recv_sems, 1 + j, (*ch, c)) for j, ch in enumerate(others)]
        for cp in first:
            cp.start()
        passed = [_remote(slab(*ch, c), slab(*ch, c), send_sems, recv_sems, 4 + j, sibling) for j, ch in enumerate(others)]
        for j, ch in enumerate(others):
            _remote(x_ref, slab(*ch, c), send_sems, recv_sems, 1 + j, (x, y, c)).wait_recv()
            passed[j].start()
        _remote(x_ref, slab(x, y, 1 - c), send_sems, recv_sems, 0, (x, y, c)).wait_recv()
        for j, ch in enumerate(others):
            _remote(x_ref, slab(*ch, 1 - c), send_sems, recv_sems, 4 + j, (x, y, c)).wait_recv()
        for cp in first + passed:
            cp.wait_send()

    out = pl.pallas_call(
        body, out_shape=jax.ShapeDtypeStruct((8,) + blk.shape, blk.dtype), in_specs=[HBM_SPEC], out_specs=HBM_SPEC,
        scratch_shapes=[pltpu.SemaphoreType.DMA((7,)), pltpu.SemaphoreType.DMA((7,))], name=name)(blk)
    me = 4 * lax.axis_index("x") + 2 * lax.axis_index("y") + lax.axis_index("c")
    return lax.dynamic_update_index_in_dim(out, blk, me, 0)


def chip_gather(name, shards):
    exchange = RidingExchange("gather", shards, [s.shape[1] for s in shards])
    now, _, _ = exchange.step(len(exchange.pieces))
    _pcall(lambda: None, [], _StagedSide([(exchange, now, now, now)]), in_specs=[], out_specs=[], out_shape=[], name=name)
    chip = 2 * lax.axis_index("x") + lax.axis_index("y")
    return [lax.dynamic_update_index_in_dim(o, s, chip, 0) for o, s in zip(exchange.bufs, shards)]


def sum_slabs(name, r, out_dtype, tr):
    S, R, W = r.shape

    def body(*refs):
        t = refs[0][...].astype(F32)
        for s in range(1, S):
            t = t + refs[s][...].astype(F32)
        refs[S][...] = t.astype(out_dtype)

    return pl.pallas_call(
        body, grid=(R // tr,), in_specs=[pl.BlockSpec((None, tr, W), lambda i, s=s: (s, i, 0)) for s in range(S)],
        out_specs=pl.BlockSpec((tr, W), lambda i: (i, 0)), out_shape=jax.ShapeDtypeStruct((R, W), out_dtype),
        compiler_params=_cparams(("parallel",)), name=name)(*([r] * S))


def sum_chips(name, arrived, own, chip, tr):
    S, R, W = arrived.shape

    def body(chip_ref, own_ref, *refs):
        me = chip_ref[0]
        t = None
        for s in range(S):
            term = jnp.where(me == s, own_ref[...].astype(F32), refs[s][...].astype(F32))
            t = term if t is None else t + term
        refs[S][...] = t

    grid_spec = pltpu.PrefetchScalarGridSpec(
        num_scalar_prefetch=1, grid=(R // tr,),
        in_specs=[pl.BlockSpec((None, tr, W), lambda i, ch: (ch[0], i, 0))] +
                 [pl.BlockSpec((None, tr, W), lambda i, ch, s=s: (s, i, 0)) for s in range(S)],
        out_specs=pl.BlockSpec((tr, W), lambda i, ch: (i, 0)))
    return pl.pallas_call(body, grid_spec=grid_spec, out_shape=jax.ShapeDtypeStruct((R, W), F32),
                          compiler_params=_cparams(("parallel",)), name=name)(
                              chip.reshape(1).astype(jnp.int32), own, *([arrived] * S))


def add_kept_half(name, halves, got, c, tr):
    S, _, R, W = halves.shape

    def body(c_ref, h_ref, g_ref, o_ref):
        o_ref[...] = (h_ref[...].astype(F32) + g_ref[...].astype(F32)).astype(o_ref.dtype)

    grid_spec = pltpu.PrefetchScalarGridSpec(
        num_scalar_prefetch=1, grid=(S, R // tr),
        in_specs=[pl.BlockSpec((None, None, tr, W), lambda s, i, c_ref: (s, c_ref[0], i, 0)),
                  pl.BlockSpec((None, tr, W), lambda s, i, c_ref: (s, i, 0))],
        out_specs=pl.BlockSpec((None, tr, W), lambda s, i, c_ref: (s, i, 0)))
    return pl.pallas_call(body, grid_spec=grid_spec, out_shape=jax.ShapeDtypeStruct((S, R, W), halves.dtype),
                          compiler_params=_cparams(("parallel", "parallel")), name=name)(
                              c.reshape(1).astype(jnp.int32), halves, got)


def adamw_layer(name, l, g, row_off, w, m, v, prev, tr, tc=None, side=None):
    L, R, C = w.shape
    tc = C if tc is None else tc
    off = row_off // tr

    def body(g_ref, w_ref, m_ref, v_ref, *rest):
        outs = rest[-4:]
        gv = g_ref[...]
        d, mn, vn = f_adamw(w_ref[...], gv, m_ref[...], v_ref[...])
        for o, val in zip(outs, (gv, d, mn, vn)):
            o[...] = val

    slab = pl.BlockSpec((None, tr, tc), lambda i, j: (l, i, j))
    in_specs = [pl.BlockSpec((tr, tc), lambda i, j: (off + i, j)), slab, slab, slab]
    args = [g, w, m, v]
    aliases = {}
    if prev is not None:
        in_specs += [pl.BlockSpec(memory_space=pl.ANY)] * 4
        args += list(prev)
        aliases = {4 + k: k for k in range(4)}
    return _pcall(
        body, args, side, grid=(R // tr, C // tc), in_specs=in_specs, out_specs=[slab] * 4,
        out_shape=[jax.ShapeDtypeStruct((L, R, C), F32)] * 4, input_output_aliases=aliases,
        compiler_params=_cparams(("parallel", "parallel")), name=name)


BIG = ("w_in", "w_out", "xq", "xk", "xv", "xo", "w_up", "w_down")
GROUPS = {"mix": ("w_out", "xq", "xk", "xv", "xo"), "ff": ("w_up", "w_down")}
PACK_ROWS = {"w_out": 512, "xq": 512, "xk": 512, "xv": 512, "xo": 512, "w_up": 2048, "w_down": 2048}
GROUP_ROWS = {g: sum(PACK_ROWS[n] for n in names) for g, names in GROUPS.items()}
SUM_TILE = 256
PACK_OFF = {}
for _names in GROUPS.values():
    _o = 0
    for _n in _names:
        PACK_OFF[_n] = _o
        _o += PACK_ROWS[_n]

_SPLIT_OFF = np.cumsum((0,) + SPLIT_SIZES)
_KORDER = (0, 1, 2, 3, 4, 5, 6, 9, 10, 7, 8)


def w_in_to_kernel_cols(w):
    parts = [w[..., _SPLIT_OFF[i]:_SPLIT_OFF[i + 1]] for i in _KORDER]
    parts.append(jnp.zeros(w.shape[:-1] + (D_INP - D_IN,), w.dtype))
    return jnp.concatenate(parts, axis=-1)


def w_in_from_kernel_cols(w):
    offs = np.cumsum((0,) + tuple(SPLIT_SIZES[i] for i in _KORDER))
    pos = {k: (offs[n], offs[n + 1]) for n, k in enumerate(_KORDER)}
    return jnp.concatenate([w[..., pos[i][0]:pos[i][1]] for i in range(len(SPLIT_SIZES))], axis=-1)


def pack_shards(shards, group, dtype):
    return jnp.concatenate([shards[n].astype(dtype) for n in GROUPS[group]], axis=-2)


def unpack_rows(packed, name):
    return packed[..., PACK_OFF[name]:PACK_OFF[name] + PACK_ROWS[name], :]


WEIGHTS = ("rel_bias", "w_in", "w_out", "attn_sink", "gla_w2_f", "gla_b2_f", "gla_w2_b", "gla_b2_b", "gla_norm", "conv_w",
           "conv_b", "lru_wa", "lru_ba", "lru_wx", "lru_bx", "lru_lambda", "xq", "xk", "xv", "xo", "w_up", "w_down",
           "norm_mix_pre", "norm_mix_post", "norm_mem", "norm_x_pre", "norm_x_post", "norm_ff_pre", "norm_ff_post")
SMALL = tuple(n for n in WEIGHTS if n not in BIG)
SMALL_SHARDED = ("gla_w2_f", "gla_w2_b", "conv_w", "lru_ba", "lru_bx", "lru_lambda")
ROW_TILE = 256
SMALL_TILE = 512
W_IN_TILE = (344, 1024)
RIDE_PIECE_ROWS = (256, 512)


def _small_rows(n):
    return -(-n // (SUBLANES * LANES)) * SUBLANES


def _as_rows(a2):
    L, n = a2.shape
    rows = _small_rows(n)
    if rows * LANES != n:
        a2 = jnp.pad(a2, ((0, 0), (0, rows * LANES - n)))
    return a2.reshape(L * rows, LANES)


def _pack_small(items, layered):
    parts = []
    for it, lay in zip(items, layered):
        if isinstance(it, (list, tuple)):
            parts += [_as_rows(e.astype(F32).reshape(1, -1)) for e in it]
        else:
            parts.append(_as_rows(it.astype(F32).reshape(it.shape[0] if lay else 1, -1)))
    pad = -sum(p.shape[0] for p in parts) % SMALL_TILE
    if pad:
        parts.append(jnp.zeros((pad, LANES), F32))
    return jnp.concatenate(parts, axis=0)


def _unpack_small(buf, shapes, layered):
    lead = buf.shape[:-2]
    out, o = [], 0
    for s, lay in zip(shapes, layered):
        L = s[0] if lay else 1
        n = int(np.prod(s)) // L
        rows = _small_rows(n)
        part = buf[..., o:o + L * rows, :]
        if n != rows * LANES:
            part = part.reshape(lead + (L, rows * LANES))[..., :n]
        out.append(part.reshape(lead + tuple(s)))
        o += L * rows
    return out


def _relu2(r):
    return r, jnp.square(jnp.maximum(r, 0.0))


def _drelu2(r, u):
    return r * (2.0 * jnp.maximum(u, 0.0))


def kernel(x, mem, rel_bias, w_in, w_out, attn_sink, gla_w2_f, gla_b2_f, gla_w2_b, gla_b2_b, gla_norm, conv_w, conv_b, lru_wa, lru_ba, lru_wx, lru_bx, lru_lambda, xq, xk, xv, xo, w_up, w_down, norm_mix_pre, norm_mix_post, norm_mem, norm_x_pre, norm_x_post, norm_ff_pre, norm_ff_post, loss_target, m_rel_bias, m_w_in, m_w_out, m_attn_sink, m_gla_w2_f, m_gla_b2_f, m_gla_w2_b, m_gla_b2_b, m_gla_norm, m_conv_w, m_conv_b, m_lru_wa, m_lru_ba, m_lru_wx, m_lru_bx, m_lru_lambda, m_xq, m_xk, m_xv, m_xo, m_w_up, m_w_down, m_norm_mix_pre, m_norm_mix_post, m_norm_mem, m_norm_x_pre, m_norm_x_post, m_norm_ff_pre, m_norm_ff_post, v_rel_bias, v_w_in, v_w_out, v_attn_sink, v_gla_w2_f, v_gla_b2_f, v_gla_w2_b, v_gla_b2_b, v_gla_norm, v_conv_w, v_conv_b, v_lru_wa, v_lru_ba, v_lru_wx, v_lru_bx, v_lru_lambda, v_xq, v_xk, v_xv, v_xo, v_w_up, v_w_down, v_norm_mix_pre, v_norm_mix_post, v_norm_mem, v_norm_x_pre, v_norm_x_post, v_norm_ff_pre, v_norm_ff_post):
    w_args = (rel_bias, w_in, w_out, attn_sink, gla_w2_f, gla_b2_f, gla_w2_b, gla_b2_b, gla_norm, conv_w, conv_b, lru_wa,
              lru_ba, lru_wx, lru_bx, lru_lambda, xq, xk, xv, xo, w_up, w_down, norm_mix_pre, norm_mix_post, norm_mem,
              norm_x_pre, norm_x_post, norm_ff_pre, norm_ff_post)
    m_args = (m_rel_bias, m_w_in, m_w_out, m_attn_sink, m_gla_w2_f, m_gla_b2_f, m_gla_w2_b, m_gla_b2_b, m_gla_norm, m_conv_w,
              m_conv_b, m_lru_wa, m_lru_ba, m_lru_wx, m_lru_bx, m_lru_lambda, m_xq, m_xk, m_xv, m_xo, m_w_up, m_w_down,
              m_norm_mix_pre, m_norm_mix_post, m_norm_mem, m_norm_x_pre, m_norm_x_post, m_norm_ff_pre, m_norm_ff_post)
    v_args = (v_rel_bias, v_w_in, v_w_out, v_attn_sink, v_gla_w2_f, v_gla_b2_f, v_gla_w2_b, v_gla_b2_b, v_gla_norm, v_conv_w,
              v_conv_b, v_lru_wa, v_lru_ba, v_lru_wx, v_lru_bx, v_lru_lambda, v_xq, v_xk, v_xv, v_xo, v_w_up, v_w_down,
              v_norm_mix_pre, v_norm_mix_post, v_norm_mem, v_norm_x_pre, v_norm_x_post, v_norm_ff_pre, v_norm_ff_post)
    Wt, Mo, Vo = dict(zip(WEIGHTS, w_args)), dict(zip(WEIGHTS, m_args)), dict(zip(WEIGHTS, v_args))
    x, mem, tgt = x[0], mem[0], loss_target[0]
    D = D_MODEL
    depth = w_in.shape[0]
    chip = 2 * lax.axis_index("x") + lax.axis_index("y")
    core = lax.axis_index("c")

    sm_shapes = [Wt[n].shape for n in SMALL_SHARDED]
    yes = [True] * len(SMALL_SHARDED)
    g8 = all_gather8("gather_small_weights", _pack_small([Wt[n] for n in SMALL_SHARDED], yes))
    per_chip = _unpack_small(g8[0::2], sm_shapes, yes)
    whole = {n: jnp.concatenate([p[j] for j in range(4)], axis=-1) for n, p in zip(SMALL_SHARDED, per_chip)}

    def group_shards(l, group):
        shard = pack_shards({n: Wt[n][l] for n in GROUPS[group]}, group, BF16).reshape(2, GROUP_ROWS[group] // 2, D)
        return [shard] + ([w_in[l].astype(BF16).reshape(2, D // 2, D_IN // 4)] if group == "mix" else [])

    def whole_weights(group, gathered):
        g = gathered[0].reshape(4, GROUP_ROWS[group], D)
        if group == "ff":
            return {"w_up": unpack_rows(g, "w_up"), "w_down": unpack_rows(g, "w_down").reshape(D_FF, D)}
        W = {n: unpack_rows(g, n).reshape(D, D) for n in GROUPS["mix"]}
        win = gathered[1].reshape(4, D, D_IN // 4).transpose(1, 0, 2).reshape(D, D_IN)
        W["w_in"] = w_in_to_kernel_cols(win)
        return W

    first = group_shards(0, "mix")
    Wgot = {(0, "mix"): whole_weights("mix", chip_gather("gather_first_weights", first))}
    gathers = Stream()
    riding = {}
    for l in range(depth):
        for group in ("mix", "ff"):
            if (l, group) != (0, "mix"):
                shards = group_shards(l, group)
                riding[l, group] = (shards, gathers.add(RidingExchange("gather", shards, RIDE_PIECE_ROWS)))

    def need_weights(l, group):
        if (l, group) not in Wgot:
            shards, exchange = riding.pop((l, group))
            got = gathers.finish(exchange, "gather_rest")
            Wgot[l, group] = whole_weights(
                group, [lax.dynamic_update_index_in_dim(b, s, chip, 0) for b, s in zip(got, shards)])
        return Wgot[l, group]

    bucket = t5_bucket_map()
    bias = bias_table_fwd(rel_bias, bucket)

    def gain(name, l):
        return Wt[name][l][None]

    def layer_params(l):
        w2fp = jnp.zeros((LANES, B_QK), F32).at[0:GATE_RANK].set(whole["gla_w2_f"][l])
        w2bp = jnp.zeros((LANES, B_QK), F32).at[GATE_RANK:2 * GATE_RANK].set(whole["gla_w2_b"][l])
        gate = [w2fp, gla_b2_f[l][None], w2bp, gla_b2_b[l][None]]
        lru = [whole["conv_w"][l], conv_b[l][None], lru_wa[l], whole["lru_ba"][l], lru_wx[l], whole["lru_bx"][l],
               whole["lru_lambda"][l]]
        return attn_sink[l].reshape(A_KV_HEADS, A_GROUP, 1), gate, gla_norm[l][None], lru

    saved = []
    xcur = x
    (h1,) = rowmap("norm_first", f_norm, [x], [gain("norm_mix_pre", 0)], [(D, BF16)], ROW_TILE)
    loss_acc = None
    Wfull = []
    ride = gathers.take
    for l in range(depth):
        W = dict(need_weights(l, "mix"))
        sink3, gate, gn, lru = layer_params(l)
        (proj,) = mm("mm_in", h1, W["w_in"], "nn", [(F32, 1)], pm=512, pn=1408, side=ride(2))
        oa = attn_fwd(proj, bias, sink3, side=ride(3))
        zrow, grow = (proj, LANES, OFF_Z // LANES), (proj, B_V, OFF_BG // B_V)
        laf, lab = rowmap("gla_gate", f_gla_gate, [zrow], gate, [(B_QK, F32), (B_QK, F32)], ROW_TILE)
        oraw = gla_fwd(proj, laf, lab, side=ride(2))
        (ob,) = rowmap("gla_post", f_gla_post, [oraw, grow], [gn], [(B_V, BF16)], ROW_TILE)
        oc = lru_fwd(proj, *lru, side=ride(1))
        cat = jnp.concatenate([oa, ob, oc], axis=1)
        (mixed,) = mm("mm_out", cat, W["w_out"], "nn", [(F32, 1)], side=ride(1))
        x1, h2 = rowmap("resnorm_mix", f_resnorm, [xcur, mixed], [gain("norm_mix_post", l), gain("norm_x_pre", l)],
                        [(D, F32), (D, BF16)], ROW_TILE)
        (memn,) = rowmap("norm_mem", f_norm, [mem], [gain("norm_mem", l)], [(D, BF16)], ROW_TILE)
        (q,) = mm("mm_xq", h2, W["xq"], "nn", [(BF16, 1)], side=ride(1))
        (k,) = mm("mm_xk", memn, W["xk"], "nn", [(F32, 1)])
        (v,) = mm("mm_xv", memn, W["xv"], "nn", [(F32, 1)])
        (o,) = rowmap("xattn", f_xattn, [q], [k, v], [(D, BF16)], ROW_TILE)
        (xo_out,) = mm("mm_xo", o, W["xo"], "nn", [(F32, 1)], side=ride(1))
        x2, h3 = rowmap("resnorm_x", f_resnorm, [x1, xo_out], [gain("norm_x_post", l), gain("norm_ff_pre", l)],
                        [(D, F32), (D, BF16)], ROW_TILE)
        W.update(need_weights(l, "ff"))
        Wfull.append(W)
        u, act = mm("mm_up", h3, W["w_up"], "nn", [(F32, 1), (BF16, 1)], epilogue=_relu2, side=ride(3))
        (ff,) = mm("mm_down", act, W["w_down"], "nn", [(F32, 1)], side=ride(2))
        saved.append(dict(x0=xcur, h1=h1, proj=proj, laf=laf, lab=lab, oraw=oraw, cat=cat, mixed=mixed, x1=x1, h2=h2,
                          memn=memn, q=q, k=k, v=v, o=o, xo_out=xo_out, x2=x2, h3=h3, u=u, act=act, ff=ff))
        if l < depth - 1:
            xcur, h1 = rowmap("resnorm_ff", f_resnorm, [x2, ff], [gain("norm_ff_post", l), gain("norm_mix_pre", l + 1)],
                              [(D, F32), (D, BF16)], ROW_TILE)
        else:
            (loss_acc,) = rowmap("final_loss", f_final_loss, [x2, ff, tgt], [gain("norm_ff_post", l)], [], ROW_TILE,
                                 accs=[(1, 1)])
    loss = lax.psum(loss_acc[0, 0], ("x", "y", "c"))

    small_g = {n: [None] * depth for n in SMALL if n != "rel_bias"}
    adam = {}
    dbias_all = []
    dx_next = dh1_next = None
    grad_x = None
    scatters = Stream()
    ride = scatters.take
    taking = []
    inflight = []
    sharing = []

    def start_reduce(lyr, group, dW):
        if group == "ff":
            pack = dW["ff"]
        else:
            pack = pack_shards({n: dW[n].reshape(4, D // 4, D) for n in GROUPS["mix"]}, group, BF16)
        halves = [pack.reshape(4, 2, GROUP_ROWS[group] // 2, D)]
        if group == "mix":
            g_in = w_in_from_kernel_cols(dW["w_in"]).reshape(D, 4, D_IN // 4).transpose(1, 0, 2)
            halves.append(g_in.reshape(4, 2, D // 2, D_IN // 4))
        taking.append((lyr, group, halves, scatters.add(RidingExchange("take", halves, RIDE_PIECE_ROWS))))

    def continue_reduce():
        lyr, group, halves, exchange = taking.pop()
        got = scatters.finish(exchange, "reduce_to_half_owner")
        sums = [add_kept_half("reduce_chip_sum", h, g, core, SUM_TILE) for h, g in zip(halves, got)]
        inflight.append((lyr, group, sums, scatters.add(RidingExchange("scatter", sums, RIDE_PIECE_ROWS))))

    def finish_reduce():
        lyr, group, sums, exchange = inflight.pop(0)
        arrived = scatters.finish(exchange, "reduce_rest")
        totals = [sum_chips("reduce_sum_chips", a, s, chip, SUM_TILE) for a, s in zip(arrived, sums)]
        sharing.append((lyr, group, totals, scatters.add(RidingExchange("pair", totals, RIDE_PIECE_ROWS))))

    w_in_t = [jnp.swapaxes(a, 1, 2) for a in (w_in, m_w_in, v_w_in)]

    def run_updates():
        for lyr, group, totals, exchange in sharing:
            both = scatters.finish(exchange, "reduce_share_halves")
            fulls = [lax.dynamic_update_index_in_dim(b, t, core, 0) for b, t in zip(both, totals)]
            full = fulls[0].reshape(GROUP_ROWS[group], D)
            for n in GROUPS[group]:
                adam[n] = adamw_layer("adamw_" + n, lyr, full, PACK_OFF[n], Wt[n], Mo[n], Vo[n], adam.get(n), SUM_TILE)
            if group == "mix":
                g = fulls[1].reshape(D, D_IN // 4).T
                adam["w_in"] = adamw_layer("adamw_w_in", lyr, g, 0, *w_in_t, adam.get("w_in"), *W_IN_TILE)

    for l in reversed(range(depth)):
        W, S = Wfull[l], saved[l]
        sink3, gate, gn, lru = layer_params(l)
        if l == depth - 1:
            (dx2, dff), (dgp,) = rowmap_bwd("final_bwd", f_final_rows, [S["x2"], S["ff"], tgt], [gain("norm_ff_post", l)],
                                            [None], ROW_TILE, [F32, F32, None], [True])
        else:
            (dx2, dff), (dgp, dgn_next) = rowmap_bwd(
                "resnorm_ff_bwd", f_resnorm, [S["x2"], S["ff"]], [gain("norm_ff_post", l), gain("norm_mix_pre", l + 1)],
                [dx_next, dh1_next], ROW_TILE, [F32, F32], [True, True])
            small_g["norm_mix_pre"][l + 1] = dgn_next[0]
        small_g["norm_ff_post"][l] = dgp[0]
        dW = {}
        (du,) = mm("mm_down_bwd", dff, W["w_down"], "nt", [(BF16, 1)], epilogue=_drelu2, extras=[S["u"]], side=ride(2))
        (pack,) = mm("mm_down_wgrad", S["act"], dff, "tn",
                     [(BF16, ("rows", GROUP_ROWS["ff"], PACK_OFF["w_down"], None))], side=ride(2))
        (dW["ff"],) = mm("mm_up_wgrad", S["h3"], du, "tn",
                         [(BF16, ("cols", GROUP_ROWS["ff"], PACK_OFF["w_up"], pack))], side=ride(2))
        start_reduce(l, "ff", dW)
        (dh3,) = mm("mm_up_bwd", du, W["w_up"], "nt", [(F32, 1)], pk=D, side=ride(2))
        continue_reduce()
        if len(inflight) > 1:
            finish_reduce()
        (dx1, dxo_out), (dg1, dg2) = rowmap_bwd(
            "resnorm_x_bwd", f_resnorm, [S["x1"], S["xo_out"]], [gain("norm_x_post", l), gain("norm_ff_pre", l)],
            [dx2, dh3], ROW_TILE, [F32, F32], [True, True])
        small_g["norm_x_post"][l], small_g["norm_ff_pre"][l] = dg1[0], dg2[0]
        (do,) = mm("mm_xo_bwd", dxo_out, W["xo"], "nt", [(F32, 1)])
        (dW["xo"],) = mm("mm_xo_wgrad", S["o"], dxo_out, "tn", [(BF16, 1)])
        (dq,), (dk, dv) = rowmap_bwd("xattn_bwd", f_xattn, [S["q"]], [S["k"], S["v"]], [do], ROW_TILE, [BF16], [True, True])
        (dW["xq"],) = mm("mm_xq_wgrad", S["h2"], dq, "tn", [(BF16, 1)])
        (dh2,) = mm("mm_xq_bwd", dq, W["xq"], "nt", [(F32, 1)])
        (dW["xk"],) = mm("mm_xk_wgrad", S["memn"], dk, "tn", [(BF16, 1)])
        (dW["xv"],) = mm("mm_xv_wgrad", S["memn"], dv, "tn", [(BF16, 1)])
        (dmk,) = mm("mm_xk_bwd", dk, W["xk"], "nt", [(F32, 1)])
        (dmv,) = mm("mm_xv_bwd", dv, W["xv"], "nt", [(F32, 1)])
        _, (dgm,) = rowmap_bwd("norm_mem_bwd", f_norm_twice, [mem], [gain("norm_mem", l)], [dmk, dmv], ROW_TILE, [None], [True])
        small_g["norm_mem"][l] = dgm[0]
        (dx0, dmixed), (dg1, dg2) = rowmap_bwd(
            "resnorm_mix_bwd", f_resnorm, [S["x0"], S["mixed"]], [gain("norm_mix_post", l), gain("norm_x_pre", l)],
            [dx1, dh2], ROW_TILE, [F32, F32], [True, True])
        small_g["norm_mix_post"][l], small_g["norm_x_pre"][l] = dg1[0], dg2[0]
        (dcat,) = mm("mm_out_bwd", dmixed, W["w_out"], "nt", [(F32, 1)])
        (dW["w_out"],) = mm("mm_out_wgrad", S["cat"], dmixed, "tn", [(BF16, 1)])
        proj = S["proj"]
        daq, dak, dav, dbias, dsink = attn_bwd(proj, bias, sink3, dcat, side=ride(2))
        dbias_all.append(dbias)
        small_g["attn_sink"][l] = dsink.reshape(A_HEADS)
        zrow, grow = (proj, LANES, OFF_Z // LANES), (proj, B_V, OFF_BG // B_V)
        (doraw, dbg), (dgn,) = rowmap_bwd("gla_post_bwd", f_gla_post, [S["oraw"], grow], [gn], [(dcat, B_V, A_Q // B_V)],
                                          ROW_TILE, [F32, BF16], [True])
        dbq, dbk, dbv, dlaf, dlab = gla_bwd(proj, S["laf"], S["lab"], doraw, side=ride(3))
        (dz,), (dw2fp, db2f, dw2bp, db2b) = rowmap_bwd("gla_gate_bwd", f_gla_gate, [zrow], gate, [dlaf, dlab], ROW_TILE,
                                                        [BF16], [True] * 4)
        small_g["gla_norm"][l] = dgn[0]
        small_g["gla_w2_f"][l], small_g["gla_b2_f"][l] = dw2fp[0:GATE_RANK], db2f[0]
        small_g["gla_w2_b"][l], small_g["gla_b2_b"][l] = dw2bp[GATE_RANK:2 * GATE_RANK], db2b[0]
        dcx, dcy, dcw, dcb, dwa, dba, dwx, dbx, dlam = lru_bwd(proj, *lru, dcat, side=ride(1))
        small_g["conv_w"][l], small_g["conv_b"][l] = dcw, dcb[0]
        small_g["lru_wa"][l], small_g["lru_ba"][l], small_g["lru_wx"][l] = dwa, dba, dwx
        small_g["lru_bx"][l], small_g["lru_lambda"][l] = dbx, dlam
        dproj = jnp.concatenate([daq, dak, dav, dbq, dbk, dbv, dbg, dcx, dcy, dz], axis=1)
        (dW["w_in"],) = mm("mm_in_wgrad", S["h1"], dproj, "tn", [(BF16, 1)], pm=512, pn=1408, side=ride(1))
        (dh1,) = mm("mm_in_bwd", dproj, W["w_in"], "nt", [(F32, 1)], side=ride(1))
        if l > 0:
            dx_next, dh1_next = dx0, dh1
        else:
            (grad_x,), (dg0,) = rowmap_bwd("norm_first_bwd", f_norm_keep, [x], [gain("norm_mix_pre", 0)], [dx0, dh1],
                                           ROW_TILE, [F32], [True])
            small_g["norm_mix_pre"][0] = dg0[0]

        finish_reduce()
        start_reduce(l, "mix", dW)
        continue_reduce()
    finish_reduce()
    run_updates()

    dtab = bias_table_bwd(dbias_all, bucket)
    small_g["rel_bias"] = dtab[:, :A_HEADS]
    layered = [n != "rel_bias" for n in SMALL]
    sg_shapes = [(depth,) + small_g[n][0].shape if lay else small_g[n].shape for n, lay in zip(SMALL, layered)]
    contributions = all_gather8("gather_small_grads", _pack_small([small_g[n] for n in SMALL], layered))
    sg_sum = sum_slabs("sum_small_grads", contributions, F32, SMALL_TILE)
    sg = dict(zip(SMALL, _unpack_small(sg_sum, sg_shapes, layered)))
    for n in SMALL_SHARDED:
        w = Wt[n].shape[-1]
        sg[n] = lax.dynamic_slice_in_dim(sg[n], chip * w, w, axis=sg[n].ndim - 1)

    grads, delta, new_m, new_v = {}, {}, {}, {}
    adam["w_in"] = [jnp.swapaxes(a, 1, 2) for a in adam["w_in"]]
    for n in BIG:
        grads[n], delta[n], new_m[n], new_v[n] = adam[n]
    shapes = [Wt[n].shape for n in SMALL]
    packs = [_pack_small([src[n] for n in SMALL], layered) for src in (Wt, sg, Mo, Vo)]
    d_, m_, v_ = rowmap("adamw_small", f_adamw, packs, [], [(LANES, F32)] * 3, SMALL_TILE)
    for n, a, b, c_ in zip(SMALL, *[_unpack_small(p, shapes, layered) for p in (d_, m_, v_)]):
        grads[n], delta[n], new_m[n], new_v[n] = sg[n], a, b, c_

    return (loss, grad_x[None], *[grads[n] for n in WEIGHTS], *[delta[n] for n in WEIGHTS],
            *[new_m[n] for n in WEIGHTS], *[new_v[n] for n in WEIGHTS])
```
